```python
import jax, jax.numpy as jnp
from jax import lax
import numpy as np

D_MODEL = 1024
BATCH = 8
SEQ = 8192
DEPTH = 4

GRID_W = 64
QBLOCK = 128
NORM_EPS = 1e-6
ROPE_THETA = 10000.0
NEG_INF = -1e30

MLA_HEADS = 8
MLA_Q_RANK = 384
MLA_KV_RANK = 256
MLA_NOPE = 64
MLA_ROPE = 32
MLA_V = 64

DIL_PAIRS = ((128, 1), (512, 4), (2048, 16))
DIL_HALF = 64
DIL_SLOTS = 4
DIL_GROUPS = len(DIL_PAIRS)
DIL_HEADS = DIL_SLOTS * DIL_GROUPS
DIL_HEAD_DIM = 64

GQA_HEADS = 16
GQA_KV_HEADS = 4
GQA_HEAD_DIM = 64

FFN_HIDDEN = -(-8 * D_MODEL // (3 * 256)) * 256

IN_A = MLA_Q_RANK + MLA_KV_RANK + MLA_ROPE
IN_B = 3 * DIL_HEADS * DIL_HEAD_DIM
MIX_IN = IN_A + IN_B
MIX_OUT = MLA_HEADS * MLA_V + DIL_SLOTS * DIL_HEAD_DIM
N_EVEN = (DEPTH + 1) // 2
N_ODD = DEPTH // 2

kernel_name = "hybrid_mla_dilated_axial_gqa_encoder"


def rmsnorm(x, g):
    xf = x.astype(jnp.float32)
    y = xf * lax.rsqrt(jnp.mean(xf * xf, axis=-1, keepdims=True) + NORM_EPS)
    return (y * g.astype(jnp.float32)).astype(x.dtype)


def rope_angles(pos, dim):
    freqs = ROPE_THETA ** (-jnp.arange(0, dim, 2, dtype=jnp.float32) / dim)
    ang = pos.astype(jnp.float32)[:, None] * freqs[None, :]
    return jnp.cos(ang), jnp.sin(ang)


def apply_rope(x, cos, sin):
    xf = x.astype(jnp.float32)
    x1, x2 = jnp.split(xf, 2, axis=-1)
    return jnp.concatenate([x1 * cos - x2 * sin, x1 * sin + x2 * cos], axis=-1).astype(x.dtype)


def blocked_attention(q, k, v, scale):
    b, h, s, dk = q.shape
    g = k.shape[1]
    r = h // g
    nq = s // QBLOCK
    qb = q.reshape(b, g, r, nq, QBLOCK, dk).transpose(3, 0, 1, 2, 4, 5)

    def one_block(qblk):
        sc = jnp.einsum('bgrqd,bgkd->bgrqk', qblk, k, preferred_element_type=jnp.float32) * scale
        p = jax.nn.softmax(sc, axis=-1)
        return jnp.einsum('bgrqk,bgkd->bgrqd', p.astype(v.dtype), v)

    out = lax.map(one_block, qb)
    return out.transpose(1, 2, 3, 0, 4, 5).reshape(b, h, s, -1)


def mla_mixer(h_a, q_norm_g, kv_norm_g, w_uq, w_ukv, cos, sin):
    b, s, _ = h_a.shape
    cq, ckv, k_rope = jnp.split(h_a, [MLA_Q_RANK, MLA_Q_RANK + MLA_KV_RANK], axis=-1)
    cq = rmsnorm(cq, q_norm_g)
    ckv = rmsnorm(ckv, kv_norm_g)
    q = jnp.einsum('bsr,rhd->bhsd', cq, w_uq)
    kv = jnp.einsum('bsr,rhd->bhsd', ckv, w_ukv)
    q_nope, q_rope = q[..., :MLA_NOPE], q[..., MLA_NOPE:]
    k_nope, v = kv[..., :MLA_NOPE], kv[..., MLA_NOPE:]
    q_rope = apply_rope(q_rope, cos, sin)
    k_rope = apply_rope(k_rope, cos, sin)[:, None]
    k_rope = jnp.broadcast_to(k_rope, (b, MLA_HEADS, s, MLA_ROPE))
    qh = jnp.concatenate([q_nope, q_rope], axis=-1)
    kh = jnp.concatenate([k_nope, k_rope], axis=-1)
    o = blocked_attention(qh, kh, v, (MLA_NOPE + MLA_ROPE) ** -0.5)
    return o.transpose(0, 2, 1, 3).reshape(b, s, MLA_HEADS * MLA_V)


def dilated_group_attention(q, k, v, dilation, slopes):
    b, hg, s, dh = q.shape
    P = DIL_HALF
    L = s // dilation
    nb = -(-L // P)
    Lp = nb * P

    def to_residue(t):
        return t.reshape(b, hg, L, dilation, dh).transpose(0, 1, 3, 2, 4)

    qr, kr, vr = to_residue(q), to_residue(k), to_residue(v)
    qb = jnp.pad(qr, ((0, 0), (0, 0), (0, 0), (0, Lp - L), (0, 0))).reshape(b, hg, dilation, nb, P, dh)

    def band(t):
        tb = jnp.pad(t, ((0, 0), (0, 0), (0, 0), (P, Lp - L + P), (0, 0))).reshape(b, hg, dilation, nb + 2, P, dh)
        return jnp.concatenate([tb[:, :, :, :-2], tb[:, :, :, 1:-1], tb[:, :, :, 2:]], axis=4)

    kb, vb = band(kr), band(vr)
    sc = jnp.einsum('bhrnqd,bhrnkd->bhrnqk', qb, kb, preferred_element_type=jnp.float32) * dh ** -0.5

    i = jnp.arange(P)[:, None]
    c = jnp.arange(3 * P)[None, :]
    rel = c - P - i
    uk = jnp.arange(nb)[:, None, None] * P + c - P
    valid = (jnp.abs(rel) <= DIL_HALF)[None] & (uk >= 0) & (uk < L)
    dist = (dilation * jnp.abs(rel)).astype(jnp.float32)
    bias = -slopes.astype(jnp.float32)[:, None, None, None, None] * dist

    sc = jnp.where(valid, sc + bias, NEG_INF)
    m = jnp.max(sc, axis=-1, keepdims=True)
    e = jnp.exp(sc - m)
    den = jnp.sum(e, axis=-1, keepdims=True)
    o = jnp.einsum('bhrnqk,bhrnkd->bhrnqd', (e / den).astype(v.dtype), vb)
    lse = (m + jnp.log(den))[..., 0]

    o = o.reshape(b, hg, dilation, Lp, dh)[:, :, :, :L].transpose(0, 1, 3, 2, 4).reshape(b, hg, s, dh)
    lse = lse.reshape(b, hg, dilation, Lp)[:, :, :, :L].transpose(0, 1, 3, 2).reshape(b, hg, s)
    return o, lse


def dilated_mixer(h_b, slopes):
    b, s, _ = h_b.shape

    def heads(t):
        return t.reshape(b, s, DIL_GROUPS, DIL_SLOTS, DIL_HEAD_DIM).transpose(2, 0, 3, 1, 4)

    q, k, v = (heads(t) for t in jnp.split(h_b, 3, axis=-1))
    outs, lses = [], []
    for g, (_, dil) in enumerate(DIL_PAIRS):
        o, lse = dilated_group_attention(q[g], k[g], v[g], dil, slopes[g * DIL_SLOTS:(g + 1) * DIL_SLOTS])
        outs.append(o)
        lses.append(lse)
    outs = jnp.stack(outs, axis=0)
    wts = jax.nn.softmax(jnp.stack(lses, axis=0), axis=0)
    comb = jnp.sum(wts[..., None] * outs.astype(jnp.float32), axis=0).astype(h_b.dtype)
    return comb.transpose(0, 2, 1, 3).reshape(b, s, DIL_SLOTS * DIL_HEAD_DIM)


def gqa_axial_mixer(hn, w_q, w_kv, q_gain, k_gain, w_o, cos_r, sin_r, cos_c, sin_c):
    b, s, _ = hn.shape
    q = (hn @ w_q).reshape(b, s, GQA_HEADS, GQA_HEAD_DIM).transpose(0, 2, 1, 3)
    kv = (hn @ w_kv).reshape(b, s, 2, GQA_KV_HEADS, GQA_HEAD_DIM)
    k = kv[:, :, 0].transpose(0, 2, 1, 3)
    v = kv[:, :, 1].transpose(0, 2, 1, 3)
    q = rmsnorm(q, q_gain)
    k = rmsnorm(k, k_gain)
    half = GQA_HEAD_DIM // 2

    def axial(t):
        return jnp.concatenate([apply_rope(t[..., :half], cos_r, sin_r),
                                apply_rope(t[..., half:], cos_c, sin_c)], axis=-1)

    o = blocked_attention(axial(q), axial(k), v, GQA_HEAD_DIM ** -0.5)
    return o.transpose(0, 2, 1, 3).reshape(b, s, GQA_HEADS * GQA_HEAD_DIM) @ w_o


def swiglu(hn, w_in, w_out):
    gate, up = jnp.split(hn @ w_in, 2, axis=-1)
    return (jax.nn.silu(gate) * up) @ w_out


def _fwd_setup_inputs(seed: int = 0) -> dict:
    key = jax.random.key(seed)
    ks = jax.random.split(key, 18)
    f32 = jnp.float32

    def w(k, shape, fan_in):
        return jax.random.normal(k, shape, f32) * fan_in ** -0.5

    def gain(k, shape):
        return 1.0 + 0.02 * jax.random.normal(k, shape, f32)

    ne, no = N_EVEN, N_ODD
    return {
        "x": jax.random.normal(ks[0], (BATCH, SEQ, D_MODEL), f32),
        "mix_norm_ab": gain(ks[1], (ne, D_MODEL)),
        "w_in_ab": w(ks[2], (ne, D_MODEL, MIX_IN), D_MODEL),
        "mla_q_norm": gain(ks[3], (ne, MLA_Q_RANK)),
        "mla_kv_norm": gain(ks[4], (ne, MLA_KV_RANK)),
        "mla_w_uq": w(ks[5], (ne, MLA_Q_RANK, MLA_HEADS, MLA_NOPE + MLA_ROPE), MLA_Q_RANK),
        "mla_w_ukv": w(ks[6], (ne, MLA_KV_RANK, MLA_HEADS, MLA_NOPE + MLA_V), MLA_KV_RANK),
        "w_out_ab": w(ks[7], (ne, MIX_OUT, D_MODEL), MIX_OUT),
        "mix_norm_c": gain(ks[8], (no, D_MODEL)),
        "gqa_w_q": w(ks[9], (no, D_MODEL, GQA_HEADS * GQA_HEAD_DIM), D_MODEL),
        "gqa_w_kv": w(ks[10], (no, D_MODEL, 2 * GQA_KV_HEADS * GQA_HEAD_DIM), D_MODEL),
        "gqa_q_norm": gain(ks[11], (no, GQA_HEAD_DIM)),
        "gqa_k_norm": gain(ks[12], (no, GQA_HEAD_DIM)),
        "gqa_w_o": w(ks[13], (no, GQA_HEADS * GQA_HEAD_DIM, D_MODEL), GQA_HEADS * GQA_HEAD_DIM),
        "ffn_norm": gain(ks[14], (DEPTH, D_MODEL)),
        "ffn_w_in": w(ks[15], (DEPTH, D_MODEL, 2 * FFN_HIDDEN), D_MODEL),
        "ffn_w_out": w(ks[16], (DEPTH, FFN_HIDDEN, D_MODEL), FFN_HIDDEN),
        "final_norm": gain(ks[17], (D_MODEL,)),
    }


def _fwd_reference(x, mix_norm_ab, w_in_ab, mla_q_norm, mla_kv_norm, mla_w_uq, mla_w_ukv, w_out_ab,
              mix_norm_c, gqa_w_q, gqa_w_kv, gqa_q_norm, gqa_k_norm, gqa_w_o,
              ffn_norm, ffn_w_in, ffn_w_out, final_norm):
    s = x.shape[1]
    rows = s // GRID_W
    pos = jnp.arange(s)
    cos_t, sin_t = rope_angles(pos, MLA_ROPE)
    row_idx = jnp.broadcast_to(jnp.arange(rows)[:, None], (rows, GRID_W)).reshape(-1)
    col_idx = jnp.broadcast_to(jnp.arange(GRID_W)[None, :], (rows, GRID_W)).reshape(-1)
    cos_r, sin_r = rope_angles(row_idx, GQA_HEAD_DIM // 2)
    cos_c, sin_c = rope_angles(col_idx, GQA_HEAD_DIM // 2)
    slopes = jnp.exp2(-8.0 * jnp.arange(1, DIL_HEADS + 1, dtype=jnp.float32) / DIL_HEADS)

    for layer in range(DEPTH):
        i = layer // 2
        if layer % 2 == 0:
            z = rmsnorm(x, mix_norm_ab[i]) @ w_in_ab[i]
            o_a = mla_mixer(z[..., :IN_A], mla_q_norm[i], mla_kv_norm[i], mla_w_uq[i], mla_w_ukv[i],
                            cos_t, sin_t)
            o_b = dilated_mixer(z[..., IN_A:], slopes)
            x = x + jnp.concatenate([o_a, o_b], axis=-1) @ w_out_ab[i]
        else:
            x = x + gqa_axial_mixer(rmsnorm(x, mix_norm_c[i]), gqa_w_q[i], gqa_w_kv[i], gqa_q_norm[i],
                                    gqa_k_norm[i], gqa_w_o[i], cos_r, sin_r, cos_c, sin_c)
        x = x + swiglu(rmsnorm(x, ffn_norm[layer]), ffn_w_in[layer], ffn_w_out[layer])
    return rmsnorm(x, final_norm)


import jax as _jax
import jax.numpy as _jnp

TWIN_FORMAT = 'train_step'
FWD_PARAMS = ['x', 'mix_norm_ab', 'w_in_ab', 'mla_q_norm', 'mla_kv_norm', 'mla_w_uq', 'mla_w_ukv', 'w_out_ab', 'mix_norm_c', 'gqa_w_q', 'gqa_w_kv', 'gqa_q_norm', 'gqa_k_norm', 'gqa_w_o', 'ffn_norm', 'ffn_w_in', 'ffn_w_out', 'final_norm']
TWIN_WEIGHTS = ['mix_norm_ab', 'w_in_ab', 'mla_q_norm', 'mla_kv_norm', 'mla_w_uq', 'mla_w_ukv', 'w_out_ab', 'mix_norm_c', 'gqa_w_q', 'gqa_w_kv', 'gqa_q_norm', 'gqa_k_norm', 'gqa_w_o', 'ffn_norm', 'ffn_w_in', 'ffn_w_out', 'final_norm']
TWIN_DIFF_INPUT = 'x'
TWIN_INPUTS = ['x', 'mix_norm_ab', 'w_in_ab', 'mla_q_norm', 'mla_kv_norm', 'mla_w_uq', 'mla_w_ukv', 'w_out_ab', 'mix_norm_c', 'gqa_w_q', 'gqa_w_kv', 'gqa_q_norm', 'gqa_k_norm', 'gqa_w_o', 'ffn_norm', 'ffn_w_in', 'ffn_w_out', 'final_norm', 'loss_target', 'm_mix_norm_ab', 'm_w_in_ab', 'm_mla_q_norm', 'm_mla_kv_norm', 'm_mla_w_uq', 'm_mla_w_ukv', 'm_w_out_ab', 'm_mix_norm_c', 'm_gqa_w_q', 'm_gqa_w_kv', 'm_gqa_q_norm', 'm_gqa_k_norm', 'm_gqa_w_o', 'm_ffn_norm', 'm_ffn_w_in', 'm_ffn_w_out', 'm_final_norm', 'v_mix_norm_ab', 'v_w_in_ab', 'v_mla_q_norm', 'v_mla_kv_norm', 'v_mla_w_uq', 'v_mla_w_ukv', 'v_w_out_ab', 'v_mix_norm_c', 'v_gqa_w_q', 'v_gqa_w_kv', 'v_gqa_q_norm', 'v_gqa_k_norm', 'v_gqa_w_o', 'v_ffn_norm', 'v_ffn_w_in', 'v_ffn_w_out', 'v_final_norm']
TWIN_OUTPUTS = ['loss', 'grad_x', 'grad_mix_norm_ab', 'grad_w_in_ab', 'grad_mla_q_norm', 'grad_mla_kv_norm', 'grad_mla_w_uq', 'grad_mla_w_ukv', 'grad_w_out_ab', 'grad_mix_norm_c', 'grad_gqa_w_q', 'grad_gqa_w_kv', 'grad_gqa_q_norm', 'grad_gqa_k_norm', 'grad_gqa_w_o', 'grad_ffn_norm', 'grad_ffn_w_in', 'grad_ffn_w_out', 'grad_final_norm', 'delta_mix_norm_ab', 'delta_w_in_ab', 'delta_mla_q_norm', 'delta_mla_kv_norm', 'delta_mla_w_uq', 'delta_mla_w_ukv', 'delta_w_out_ab', 'delta_mix_norm_c', 'delta_gqa_w_q', 'delta_gqa_w_kv', 'delta_gqa_q_norm', 'delta_gqa_k_norm', 'delta_gqa_w_o', 'delta_ffn_norm', 'delta_ffn_w_in', 'delta_ffn_w_out', 'delta_final_norm', 'new_m_mix_norm_ab', 'new_m_w_in_ab', 'new_m_mla_q_norm', 'new_m_mla_kv_norm', 'new_m_mla_w_uq', 'new_m_mla_w_ukv', 'new_m_w_out_ab', 'new_m_mix_norm_c', 'new_m_gqa_w_q', 'new_m_gqa_w_kv', 'new_m_gqa_q_norm', 'new_m_gqa_k_norm', 'new_m_gqa_w_o', 'new_m_ffn_norm', 'new_m_ffn_w_in', 'new_m_ffn_w_out', 'new_m_final_norm', 'new_v_mix_norm_ab', 'new_v_w_in_ab', 'new_v_mla_q_norm', 'new_v_mla_kv_norm', 'new_v_mla_w_uq', 'new_v_mla_w_ukv', 'new_v_w_out_ab', 'new_v_mix_norm_c', 'new_v_gqa_w_q', 'new_v_gqa_w_kv', 'new_v_gqa_q_norm', 'new_v_gqa_k_norm', 'new_v_gqa_w_o', 'new_v_ffn_norm', 'new_v_ffn_w_in', 'new_v_ffn_w_out', 'new_v_final_norm']
TWIN_LEAF_KINDS = {'loss': 'loss', 'grad_x': 'grad_x', 'grad_mix_norm_ab': 'grad_w', 'grad_w_in_ab': 'grad_w', 'grad_mla_q_norm': 'grad_w', 'grad_mla_kv_norm': 'grad_w', 'grad_mla_w_uq': 'grad_w', 'grad_mla_w_ukv': 'grad_w', 'grad_w_out_ab': 'grad_w', 'grad_mix_norm_c': 'grad_w', 'grad_gqa_w_q': 'grad_w', 'grad_gqa_w_kv': 'grad_w', 'grad_gqa_q_norm': 'grad_w', 'grad_gqa_k_norm': 'grad_w', 'grad_gqa_w_o': 'grad_w', 'grad_ffn_norm': 'grad_w', 'grad_ffn_w_in': 'grad_w', 'grad_ffn_w_out': 'grad_w', 'grad_final_norm': 'grad_w', 'delta_mix_norm_ab': 'delta_w', 'delta_w_in_ab': 'delta_w', 'delta_mla_q_norm': 'delta_w', 'delta_mla_kv_norm': 'delta_w', 'delta_mla_w_uq': 'delta_w', 'delta_mla_w_ukv': 'delta_w', 'delta_w_out_ab': 'delta_w', 'delta_mix_norm_c': 'delta_w', 'delta_gqa_w_q': 'delta_w', 'delta_gqa_w_kv': 'delta_w', 'delta_gqa_q_norm': 'delta_w', 'delta_gqa_k_norm': 'delta_w', 'delta_gqa_w_o': 'delta_w', 'delta_ffn_norm': 'delta_w', 'delta_ffn_w_in': 'delta_w', 'delta_ffn_w_out': 'delta_w', 'delta_final_norm': 'delta_w', 'new_m_mix_norm_ab': 'new_m', 'new_m_w_in_ab': 'new_m', 'new_m_mla_q_norm': 'new_m', 'new_m_mla_kv_norm': 'new_m', 'new_m_mla_w_uq': 'new_m', 'new_m_mla_w_ukv': 'new_m', 'new_m_w_out_ab': 'new_m', 'new_m_mix_norm_c': 'new_m', 'new_m_gqa_w_q': 'new_m', 'new_m_gqa_w_kv': 'new_m', 'new_m_gqa_q_norm': 'new_m', 'new_m_gqa_k_norm': 'new_m', 'new_m_gqa_w_o': 'new_m', 'new_m_ffn_norm': 'new_m', 'new_m_ffn_w_in': 'new_m', 'new_m_ffn_w_out': 'new_m', 'new_m_final_norm': 'new_m', 'new_v_mix_norm_ab': 'new_v', 'new_v_w_in_ab': 'new_v', 'new_v_mla_q_norm': 'new_v', 'new_v_mla_kv_norm': 'new_v', 'new_v_mla_w_uq': 'new_v', 'new_v_mla_w_ukv': 'new_v', 'new_v_w_out_ab': 'new_v', 'new_v_mix_norm_c': 'new_v', 'new_v_gqa_w_q': 'new_v', 'new_v_gqa_w_kv': 'new_v', 'new_v_gqa_q_norm': 'new_v', 'new_v_gqa_k_norm': 'new_v', 'new_v_gqa_w_o': 'new_v', 'new_v_ffn_norm': 'new_v', 'new_v_ffn_w_in': 'new_v', 'new_v_ffn_w_out': 'new_v', 'new_v_final_norm': 'new_v'}


def _forward(args):
    return _fwd_reference(*[args[k] for k in FWD_PARAMS])


def _output_shape():
    def fwd():
        inp = _fwd_setup_inputs(0)
        return _fwd_reference(*[inp[k] for k in FWD_PARAMS])
    out = _jax.eval_shape(fwd)
    return out.shape, out.dtype

N_MICROBATCH = 1
ADAM_LR = 0.001
ADAM_B1 = 0.9
ADAM_B2 = 0.999
ADAM_EPS = 1e-08
ADAM_WD = 0.01
ADAM_STEP = 10
PER_EXAMPLE_BATCH_AXIS = {'x': 0, 'loss_target': 0}
SHARED_INPUTS = []
_WEIGHT_DTYPES = {'mix_norm_ab': _jnp.float32, 'w_in_ab': _jnp.float32, 'mla_q_norm': _jnp.float32, 'mla_kv_norm': _jnp.float32, 'mla_w_uq': _jnp.float32, 'mla_w_ukv': _jnp.float32, 'w_out_ab': _jnp.float32, 'mix_norm_c': _jnp.float32, 'gqa_w_q': _jnp.float32, 'gqa_w_kv': _jnp.float32, 'gqa_q_norm': _jnp.float32, 'gqa_k_norm': _jnp.float32, 'gqa_w_o': _jnp.float32, 'ffn_norm': _jnp.float32, 'ffn_w_in': _jnp.float32, 'ffn_w_out': _jnp.float32, 'final_norm': _jnp.float32}
MOMENT_SCALE = {'mix_norm_ab': 8.499277e-02, 'w_in_ab': 4.948366e-02, 'mla_q_norm': 4.465288e-02, 'mla_kv_norm': 8.772074e-02, 'mla_w_uq': 3.227787e-02, 'mla_w_ukv': 4.108646e-02, 'w_out_ab': 5.416832e-02, 'mix_norm_c': 4.557014e-02, 'gqa_w_q': 2.338369e-02, 'gqa_w_kv': 5.505040e-02, 'gqa_q_norm': 9.330292e-02, 'gqa_k_norm': 9.552233e-02, 'gqa_w_o': 3.109390e-02, 'ffn_norm': 1.924528e-01, 'ffn_w_in': 7.931901e-02, 'ffn_w_out': 1.295003e-01, 'final_norm': 6.401958e+01}


def _to_microbatches(a, axis):
    t = _jnp.moveaxis(a, axis, 0)
    t = t.reshape((N_MICROBATCH, t.shape[0] // N_MICROBATCH) + t.shape[1:])
    return _jnp.moveaxis(t, 1, axis + 1)


def setup_inputs(seed: int = 0) -> dict:
    inp = _fwd_setup_inputs(seed)
    key = _jax.random.fold_in(_jax.random.key(seed), 7919)
    shape, _ = _output_shape()
    out = dict(inp)
    out["loss_target"] = _jax.random.normal(_jax.random.fold_in(key, 0), shape, _jnp.float32)
    for i, name in enumerate(TWIN_WEIGHTS):
        w = inp[name].astype(_jnp.float32)
        if MOMENT_SCALE is None:
            s = _jnp.sqrt(_jnp.mean(_jnp.square(w)) + 1e-30)
        else:
            s = MOMENT_SCALE[name]
        km, kv = _jax.random.split(_jax.random.fold_in(key, i + 1))
        out[name] = w
        out["m_" + name] = s * _jax.random.normal(km, w.shape, _jnp.float32)
        out["v_" + name] = (s * s) * _jax.random.uniform(kv, w.shape, _jnp.float32, 0.5, 1.5)
    if N_MICROBATCH > 1:
        for name, axis in PER_EXAMPLE_BATCH_AXIS.items():
            out[name] = _to_microbatches(out[name], axis)
    return {'x': out['x'], 'mix_norm_ab': out['mix_norm_ab'], 'w_in_ab': out['w_in_ab'], 'mla_q_norm': out['mla_q_norm'], 'mla_kv_norm': out['mla_kv_norm'], 'mla_w_uq': out['mla_w_uq'], 'mla_w_ukv': out['mla_w_ukv'], 'w_out_ab': out['w_out_ab'], 'mix_norm_c': out['mix_norm_c'], 'gqa_w_q': out['gqa_w_q'], 'gqa_w_kv': out['gqa_w_kv'], 'gqa_q_norm': out['gqa_q_norm'], 'gqa_k_norm': out['gqa_k_norm'], 'gqa_w_o': out['gqa_w_o'], 'ffn_norm': out['ffn_norm'], 'ffn_w_in': out['ffn_w_in'], 'ffn_w_out': out['ffn_w_out'], 'final_norm': out['final_norm'], 'loss_target': out['loss_target'], 'm_mix_norm_ab': out['m_mix_norm_ab'], 'm_w_in_ab': out['m_w_in_ab'], 'm_mla_q_norm': out['m_mla_q_norm'], 'm_mla_kv_norm': out['m_mla_kv_norm'], 'm_mla_w_uq': out['m_mla_w_uq'], 'm_mla_w_ukv': out['m_mla_w_ukv'], 'm_w_out_ab': out['m_w_out_ab'], 'm_mix_norm_c': out['m_mix_norm_c'], 'm_gqa_w_q': out['m_gqa_w_q'], 'm_gqa_w_kv': out['m_gqa_w_kv'], 'm_gqa_q_norm': out['m_gqa_q_norm'], 'm_gqa_k_norm': out['m_gqa_k_norm'], 'm_gqa_w_o': out['m_gqa_w_o'], 'm_ffn_norm': out['m_ffn_norm'], 'm_ffn_w_in': out['m_ffn_w_in'], 'm_ffn_w_out': out['m_ffn_w_out'], 'm_final_norm': out['m_final_norm'], 'v_mix_norm_ab': out['v_mix_norm_ab'], 'v_w_in_ab': out['v_w_in_ab'], 'v_mla_q_norm': out['v_mla_q_norm'], 'v_mla_kv_norm': out['v_mla_kv_norm'], 'v_mla_w_uq': out['v_mla_w_uq'], 'v_mla_w_ukv': out['v_mla_w_ukv'], 'v_w_out_ab': out['v_w_out_ab'], 'v_mix_norm_c': out['v_mix_norm_c'], 'v_gqa_w_q': out['v_gqa_w_q'], 'v_gqa_w_kv': out['v_gqa_w_kv'], 'v_gqa_q_norm': out['v_gqa_q_norm'], 'v_gqa_k_norm': out['v_gqa_k_norm'], 'v_gqa_w_o': out['v_gqa_w_o'], 'v_ffn_norm': out['v_ffn_norm'], 'v_ffn_w_in': out['v_ffn_w_in'], 'v_ffn_w_out': out['v_ffn_w_out'], 'v_final_norm': out['v_final_norm']}


def _loss(weights, diff, rest, loss_target):
    with _jax.named_scope("forward"):
        args = {**rest, TWIN_DIFF_INPUT: diff, **{k: w.astype(_WEIGHT_DTYPES[k]) for k, w in weights.items()}}
        y = _forward(args)
    with _jax.named_scope("loss_head"):
        err = _jnp.square(y.astype(_jnp.float32) - loss_target)
        return 0.5 * _jnp.sum(_jnp.mean(err, axis=-1)) if err.ndim else 0.5 * err


def _adamw(w, g, m, v):
    m = ADAM_B1 * m + (1.0 - ADAM_B1) * g
    v = ADAM_B2 * v + (1.0 - ADAM_B2) * _jnp.square(g)
    m_hat = m / (1.0 - ADAM_B1 ** ADAM_STEP)
    v_hat = v / (1.0 - ADAM_B2 ** ADAM_STEP)
    delta = -ADAM_LR * (m_hat / (_jnp.sqrt(v_hat) + ADAM_EPS) + ADAM_WD * w)
    return delta, m, v


def reference(x, mix_norm_ab, w_in_ab, mla_q_norm, mla_kv_norm, mla_w_uq, mla_w_ukv, w_out_ab, mix_norm_c, gqa_w_q, gqa_w_kv, gqa_q_norm, gqa_k_norm, gqa_w_o, ffn_norm, ffn_w_in, ffn_w_out, final_norm, loss_target, m_mix_norm_ab, m_w_in_ab, m_mla_q_norm, m_mla_kv_norm, m_mla_w_uq, m_mla_w_ukv, m_w_out_ab, m_mix_norm_c, m_gqa_w_q, m_gqa_w_kv, m_gqa_q_norm, m_gqa_k_norm, m_gqa_w_o, m_ffn_norm, m_ffn_w_in, m_ffn_w_out, m_final_norm, v_mix_norm_ab, v_w_in_ab, v_mla_q_norm, v_mla_kv_norm, v_mla_w_uq, v_mla_w_ukv, v_w_out_ab, v_mix_norm_c, v_gqa_w_q, v_gqa_w_kv, v_gqa_q_norm, v_gqa_k_norm, v_gqa_w_o, v_ffn_norm, v_ffn_w_in, v_ffn_w_out, v_final_norm):
    given = dict(x=x, mix_norm_ab=mix_norm_ab, w_in_ab=w_in_ab, mla_q_norm=mla_q_norm, mla_kv_norm=mla_kv_norm, mla_w_uq=mla_w_uq, mla_w_ukv=mla_w_ukv, w_out_ab=w_out_ab, mix_norm_c=mix_norm_c, gqa_w_q=gqa_w_q, gqa_w_kv=gqa_w_kv, gqa_q_norm=gqa_q_norm, gqa_k_norm=gqa_k_norm, gqa_w_o=gqa_w_o, ffn_norm=ffn_norm, ffn_w_in=ffn_w_in, ffn_w_out=ffn_w_out, final_norm=final_norm, loss_target=loss_target, m_mix_norm_ab=m_mix_norm_ab, m_w_in_ab=m_w_in_ab, m_mla_q_norm=m_mla_q_norm, m_mla_kv_norm=m_mla_kv_norm, m_mla_w_uq=m_mla_w_uq, m_mla_w_ukv=m_mla_w_ukv, m_w_out_ab=m_w_out_ab, m_mix_norm_c=m_mix_norm_c, m_gqa_w_q=m_gqa_w_q, m_gqa_w_kv=m_gqa_w_kv, m_gqa_q_norm=m_gqa_q_norm, m_gqa_k_norm=m_gqa_k_norm, m_gqa_w_o=m_gqa_w_o, m_ffn_norm=m_ffn_norm, m_ffn_w_in=m_ffn_w_in, m_ffn_w_out=m_ffn_w_out, m_final_norm=m_final_norm, v_mix_norm_ab=v_mix_norm_ab, v_w_in_ab=v_w_in_ab, v_mla_q_norm=v_mla_q_norm, v_mla_kv_norm=v_mla_kv_norm, v_mla_w_uq=v_mla_w_uq, v_mla_w_ukv=v_mla_w_ukv, v_w_out_ab=v_w_out_ab, v_mix_norm_c=v_mix_norm_c, v_gqa_w_q=v_gqa_w_q, v_gqa_w_kv=v_gqa_w_kv, v_gqa_q_norm=v_gqa_q_norm, v_gqa_k_norm=v_gqa_k_norm, v_gqa_w_o=v_gqa_w_o, v_ffn_norm=v_ffn_norm, v_ffn_w_in=v_ffn_w_in, v_ffn_w_out=v_ffn_w_out, v_final_norm=v_final_norm)
    weights = {n: given[n] for n in TWIN_WEIGHTS}
    shared = {n: given[n] for n in SHARED_INPUTS}
    per_example = {n: given[n] for n in ['x']}
    grad_fn = _jax.value_and_grad(_loss, argnums=(0, 1))

    def one_microbatch(ex, loss_target):
        ex = dict(ex)
        diff = ex.pop(TWIN_DIFF_INPUT)
        return grad_fn(weights, diff, {**shared, **ex}, loss_target)

    if N_MICROBATCH == 1:
        loss, (grad_w, grad_x) = one_microbatch(per_example, given["loss_target"])
    else:
        def body(carry, xs):
            loss_sum, grad_sum = carry
            l_k, (gw_k, gx_k) = one_microbatch(xs[0], xs[1])
            with _jax.named_scope("update"):
                return (loss_sum + l_k, _jax.tree.map(_jnp.add, grad_sum, gw_k)), gx_k

        init = (_jnp.zeros((), _jnp.float32), _jax.tree.map(_jnp.zeros_like, weights))
        (loss, grad_w), grad_x = _jax.lax.scan(body, init, (per_example, given["loss_target"]))
    with _jax.named_scope("update"):
        delta_w, new_m, new_v = {}, {}, {}
        for n in TWIN_WEIGHTS:
            delta_w[n], new_m[n], new_v[n] = _adamw(weights[n], grad_w[n], given["m_" + n], given["v_" + n])
    return (loss, grad_x, *[grad_w[n] for n in TWIN_WEIGHTS], *[delta_w[n] for n in TWIN_WEIGHTS],
            *[new_m[n] for n in TWIN_WEIGHTS], *[new_v[n] for n in TWIN_WEIGHTS])
```

```python
import functools

import jax
import jax.numpy as jnp
from jax import lax
from jax.experimental import pallas as pl
from jax.experimental.pallas import tpu as pltpu

D_MODEL = 1024
DEPTH = 4
GRID_W = 64
NORM_EPS = 1e-6
ROPE_THETA = 10000.0
NEG_INF = -1e30
MLA_HEADS = 8
MLA_Q_RANK = 384
MLA_KV_RANK = 256
MLA_NOPE = 64
MLA_ROPE = 32
MLA_V = 64
DIL_PAIRS = ((128, 1), (512, 4), (2048, 16))
DIL_HALF = 64
DIL_SLOTS = 4
DIL_GROUPS = 3
DIL_HEADS = 12
DIL_HEAD_DIM = 64
GQA_HEADS = 16
GQA_KV_HEADS = 4
GQA_HEAD_DIM = 64
FFN_HIDDEN = 2816
IN_A = MLA_Q_RANK + MLA_KV_RANK + MLA_ROPE
IN_A_PAD = 768
IN_B = 3 * DIL_HEADS * DIL_HEAD_DIM
ADAM_LR = 0.001
ADAM_B1 = 0.9
ADAM_B2 = 0.999
ADAM_EPS = 1e-08
ADAM_WD = 0.01
ADAM_STEP = 10

LANES = 128
SUBLANES_16BIT = 16
VMEM_LIMIT_BYTES = 56 * 1024 * 1024

N_DEV = 8
PACK_COLS = 1024

_MXU = jnp.bfloat16
_ACT = jnp.bfloat16
_F32 = jnp.float32

_AXES = ("x", "y", "c")


def _params(sem):
    return pltpu.CompilerParams(dimension_semantics=sem, vmem_limit_bytes=VMEM_LIMIT_BYTES)


def _pick(n, cap):
    for t in range(cap - cap % LANES, 0, -LANES):
        if n % t == 0:
            return t
    return n


def _rows(m, target):
    t = m
    while t > target and t % 2 == 0:
        t //= 2
    return t


def _matmul(a, b, *, trans_a=False, trans_b=False, res=None, out_dtype=_F32, name):
    if trans_a:
        k, m = a.shape
    else:
        m, k = a.shape
    if trans_b:
        n, kb = b.shape
    else:
        kb, n = b.shape
    assert k == kb, (a.shape, b.shape)
    tm, tn, tk = _pick(m, 1024), _pick(n, 1408), _pick(k, 512)
    nk = k // tk
    dims = (((0 if trans_a else 1,), (1 if trans_b else 0,)), ((), ()))

    def body(*refs):
        if res is None:
            a_ref, b_ref, o_ref, acc = refs
            r_ref = None
        else:
            a_ref, b_ref, r_ref, o_ref, acc = refs
        kk = pl.program_id(2)

        @pl.when(kk == 0)
        def _():
            acc[...] = jnp.zeros_like(acc)

        acc[...] += lax.dot_general(a_ref[...].astype(_MXU), b_ref[...].astype(_MXU), dims,
                                    preferred_element_type=_F32)

        @pl.when(kk == nk - 1)
        def _():
            r = acc[...]
            if r_ref is not None:
                r = r + r_ref[...].astype(_F32)
            o_ref[...] = r.astype(out_dtype)

    a_spec = (pl.BlockSpec((tk, tm), lambda i, j, kk: (kk, i)) if trans_a
              else pl.BlockSpec((tm, tk), lambda i, j, kk: (i, kk)))
    b_spec = (pl.BlockSpec((tn, tk), lambda i, j, kk: (j, kk)) if trans_b
              else pl.BlockSpec((tk, tn), lambda i, j, kk: (kk, j)))
    o_spec = pl.BlockSpec((tm, tn), lambda i, j, kk: (i, j))
    in_specs = [a_spec, b_spec] + ([o_spec] if res is not None else [])
    args = (a, b) + ((res,) if res is not None else ())
    return pl.pallas_call(
        body, name=name, grid=(m // tm, n // tn, nk),
        in_specs=in_specs, out_specs=o_spec,
        out_shape=jax.ShapeDtypeStruct((m, n), out_dtype),
        scratch_shapes=[pltpu.VMEM((tm, tn), _F32)],
        compiler_params=_params(("parallel", "parallel", "arbitrary")),
    )(*args)


def _rmsnorm(x, g, *, out_dtype, name, rows=512):
    m, d = x.shape
    tm = _rows(m, rows)

    def body(x_ref, g_ref, o_ref):
        xf = x_ref[...].astype(_F32)
        r = lax.rsqrt(jnp.mean(xf * xf, axis=-1, keepdims=True) + NORM_EPS)
        o_ref[...] = ((xf * r) * g_ref[...]).astype(out_dtype)

    return pl.pallas_call(
        body, name=name, grid=(m // tm,),
        in_specs=[pl.BlockSpec((tm, d), lambda i: (i, 0)), pl.BlockSpec((1, d), lambda i: (0, 0))],
        out_specs=pl.BlockSpec((tm, d), lambda i: (i, 0)),
        out_shape=jax.ShapeDtypeStruct((m, d), out_dtype),
        compiler_params=_params(("parallel",)),
    )(x, g.reshape(1, d).astype(_F32))


def _rmsnorm_bwd(x, g, dy, dres=None, *, out_dtype=_F32, name, rows=512):
    m, d = x.shape
    tm = _rows(m, rows)

    def body(*refs):
        if dres is None:
            x_ref, g_ref, dy_ref, dx_ref, dg_ref = refs
            r_ref = None
        else:
            x_ref, g_ref, dy_ref, r_ref, dx_ref, dg_ref = refs

        @pl.when(pl.program_id(0) == 0)
        def _():
            dg_ref[...] = jnp.zeros_like(dg_ref)

        xf = x_ref[...].astype(_F32)
        r = lax.rsqrt(jnp.mean(xf * xf, axis=-1, keepdims=True) + NORM_EPS)
        xh = xf * r
        dyf = dy_ref[...].astype(_F32)
        dg_ref[...] += jnp.sum(dyf * xh, axis=0, keepdims=True)
        gdy = dyf * g_ref[...]
        dx = r * (gdy - xh * jnp.mean(gdy * xh, axis=-1, keepdims=True))
        if r_ref is not None:
            dx = dx + r_ref[...].astype(_F32)
        dx_ref[...] = dx.astype(out_dtype)

    row = pl.BlockSpec((tm, d), lambda i: (i, 0))
    vec = pl.BlockSpec((1, d), lambda i: (0, 0))
    in_specs = [row, vec, row] + ([row] if dres is not None else [])
    args = (x, g.reshape(1, d).astype(_F32), dy) + ((dres,) if dres is not None else ())
    dx, dg = pl.pallas_call(
        body, name=name, grid=(m // tm,),
        in_specs=in_specs, out_specs=[row, vec],
        out_shape=[jax.ShapeDtypeStruct((m, d), out_dtype), jax.ShapeDtypeStruct((1, d), _F32)],
        compiler_params=_params(("arbitrary",)),
    )(*args)
    return dx, dg.reshape(d)


def _rope(x, cos_t, sin_t, *, out_dtype, name, sum_chunks=False):
    s, w = x.shape
    assert w % LANES == 0
    reps = w // LANES
    ts = _rows(s, 512)
    ow = LANES if sum_chunks else w

    def body(x_ref, c_ref, s_ref, o_ref):
        xf = x_ref[...].astype(_F32)
        c, sn = c_ref[...], s_ref[...]
        if reps > 1:
            c, sn = jnp.tile(c, (1, reps)), jnp.tile(sn, (1, reps))
        lane = lax.broadcasted_iota(jnp.int32, xf.shape, 1)
        first = (lane & 31) < 16
        sw = jnp.where(first, pltpu.roll(xf, w - 16, 1), pltpu.roll(xf, 16, 1))
        y = xf * c + sw * sn
        if sum_chunks:
            shift = w // 2
            while shift >= 32:
                y = y + pltpu.roll(y, shift, 1)
                shift //= 2
            y = y[:, :LANES]
        o_ref[...] = y.astype(out_dtype)

    return pl.pallas_call(
        body, name=name, grid=(s // ts,),
        in_specs=[pl.BlockSpec((ts, w), lambda i: (i, 0)), pl.BlockSpec((ts, LANES), lambda i: (i, 0)),
                  pl.BlockSpec((ts, LANES), lambda i: (i, 0))],
        out_specs=pl.BlockSpec((ts, ow), lambda i: (i, 0)),
        out_shape=jax.ShapeDtypeStruct((s, ow), out_dtype),
        compiler_params=_params(("parallel",)),
    )(x, cos_t, sin_t)


def _attn_geometry(s, dil):
    t = min(s, 256 if dil else 512)
    n = s // t
    if dil:
        w = -(-(DIL_HALF * dil) // t)
        return t, n, w, 2 * w + 1
    return t, n, 0, n


def _dil_mask(i_q, i_k, t, dil, slope):
    rows = lax.broadcasted_iota(jnp.int32, (t, t), 0)
    cols = lax.broadcasted_iota(jnp.int32, (t, t), 1)
    rel = (i_k - i_q) * t + cols - rows
    arel = jnp.abs(rel)
    valid = (arel <= DIL_HALF * dil) & ((rel & (dil - 1)) == 0)
    return valid, -slope * arel.astype(_F32)


_NT = (((1,), (1,)), ((), ()))
_TN = (((0,), (0,)), ((), ()))


def _flash_fwd(q, k, v, *, scale, dil=0, slopes=None, out_dtype, name):
    h, s, dk = q.shape
    g, _, dv = v.shape
    r = h // g
    t, n, w, ninner = _attn_geometry(s, dil)

    def body(*refs):
        if dil:
            sl_ref, q_ref, k_ref, v_ref, o_ref, lse_ref, m_sc, l_sc, acc = refs
        else:
            q_ref, k_ref, v_ref, o_ref, lse_ref, m_sc, l_sc, acc = refs
        hh, i, j = pl.program_id(0), pl.program_id(1), pl.program_id(2)

        @pl.when(j == 0)
        def _():
            m_sc[...] = jnp.full_like(m_sc, NEG_INF)
            l_sc[...] = jnp.zeros_like(l_sc)
            acc[...] = jnp.zeros_like(acc)

        def step():
            sc = lax.dot_general(q_ref[0], k_ref[0], _NT, preferred_element_type=_F32) * scale
            if dil:
                valid, bias = _dil_mask(i, i + j - w, t, dil, sl_ref[hh])
                sc = jnp.where(valid, sc + bias, NEG_INF)
            m_prev = m_sc[...]
            m_new = jnp.maximum(m_prev, jnp.max(sc, axis=1, keepdims=True))
            p = jnp.exp(sc - m_new)
            if dil:
                p = jnp.where(valid, p, 0.0)
            alpha = jnp.exp(m_prev - m_new)
            l_sc[...] = alpha * l_sc[...] + jnp.sum(p, axis=1, keepdims=True)
            acc[...] = alpha * acc[...] + jnp.dot(p.astype(_MXU), v_ref[0], preferred_element_type=_F32)
            m_sc[...] = m_new

        if dil:
            kb = i + j - w
            pl.when((kb >= 0) & (kb < n))(step)
        else:
            step()

        @pl.when(j == ninner - 1)
        def _():
            o_ref[0] = (acc[...] / l_sc[...]).astype(out_dtype)
            lse_ref[0] = m_sc[...] + jnp.log(l_sc[...])

    if dil:
        kv_blk = lambda hh, i, j: (hh // r, jnp.clip(i + j - w, 0, n - 1), 0)
    else:
        kv_blk = lambda hh, i, j: (hh // r, j, 0)
    q_blk = lambda hh, i, j: (hh, i, 0)
    in_specs = [pl.BlockSpec((1, t, dk), q_blk), pl.BlockSpec((1, t, dk), kv_blk), pl.BlockSpec((1, t, dv), kv_blk)]
    args = (q, k, v)
    if dil:
        in_specs = [pl.BlockSpec(memory_space=pltpu.SMEM)] + in_specs
        args = (slopes.astype(_F32),) + args
    return pl.pallas_call(
        body, name=name, grid=(h, n, ninner),
        in_specs=in_specs,
        out_specs=[pl.BlockSpec((1, t, dv), q_blk), pl.BlockSpec((1, t, 1), q_blk)],
        out_shape=[jax.ShapeDtypeStruct((h, s, dv), out_dtype), jax.ShapeDtypeStruct((h, s, 1), _F32)],
        scratch_shapes=[pltpu.VMEM((t, 1), _F32), pltpu.VMEM((t, 1), _F32), pltpu.VMEM((t, dv), _F32)],
        compiler_params=_params(("parallel", "parallel", "arbitrary")),
    )(*args)


def _flash_bwd(q, k, v, do, lse, delta, *, scale, dil=0, slopes=None, name):
    h, s, dk = q.shape
    g, _, dv = v.shape
    r = h // g
    t, n, w, ninner = _attn_geometry(s, dil)

    def body(*refs):
        if dil:
            sl_ref, q_ref, k_ref, v_ref, do_ref, lse_ref, dl_ref, dq_ref, dk_ref, dv_ref = refs
        else:
            q_ref, k_ref, v_ref, do_ref, lse_ref, dl_ref, dq_ref, dk_ref, dv_ref = refs
        gg, rr, j, i = pl.program_id(0), pl.program_id(1), pl.program_id(2), pl.program_id(3)

        @pl.when((j == 0) & (i == 0))
        def _():
            dq_ref[...] = jnp.zeros_like(dq_ref)

        @pl.when((rr == 0) & (j == 0) & (i == 0))
        def _():
            dk_ref[...] = jnp.zeros_like(dk_ref)
            dv_ref[...] = jnp.zeros_like(dv_ref)

        qb = j + i - w if dil else i

        def step():
            qv, kv, vv, dov = q_ref[0], k_ref[0], v_ref[0], do_ref[0]
            sc = lax.dot_general(qv, kv, _NT, preferred_element_type=_F32) * scale
            if dil:
                valid, bias = _dil_mask(qb, j, t, dil, sl_ref[gg * r + rr])
                sc = jnp.where(valid, sc + bias, NEG_INF)
            p = jnp.exp(sc - lse_ref[0])
            if dil:
                p = jnp.where(valid, p, 0.0)
            dp = lax.dot_general(dov, vv, _NT, preferred_element_type=_F32)
            ds = (p * (dp - dl_ref[0]) * scale).astype(_MXU)
            krows = pl.ds(pl.multiple_of(j * t, t), t)
            qrows = pl.ds(pl.multiple_of(qb * t, t), t)
            dv_ref[0, krows, :] += lax.dot_general(p.astype(_MXU), dov, _TN, preferred_element_type=_F32)
            dk_ref[0, krows, :] += lax.dot_general(ds, qv, _TN, preferred_element_type=_F32)
            dq_ref[0, qrows, :] += jnp.dot(ds, kv, preferred_element_type=_F32)

        if dil:
            pl.when((qb >= 0) & (qb < n))(step)
        else:
            step()

    if dil:
        q_blk = lambda gg, rr, j, i: (gg * r + rr, jnp.clip(j + i - w, 0, n - 1), 0)
    else:
        q_blk = lambda gg, rr, j, i: (gg * r + rr, i, 0)
    kv_blk = lambda gg, rr, j, i: (gg, j, 0)
    in_specs = [pl.BlockSpec((1, t, dk), q_blk), pl.BlockSpec((1, t, dk), kv_blk), pl.BlockSpec((1, t, dv), kv_blk),
                pl.BlockSpec((1, t, dv), q_blk), pl.BlockSpec((1, t, 1), q_blk), pl.BlockSpec((1, t, 1), q_blk)]
    args = (q, k, v, do, lse, delta)
    if dil:
        in_specs = [pl.BlockSpec(memory_space=pltpu.SMEM)] + in_specs
        args = (slopes.astype(_F32),) + args
    return pl.pallas_call(
        body, name=name, grid=(g, r, n, ninner),
        in_specs=in_specs,
        out_specs=[pl.BlockSpec((1, s, dk), lambda gg, rr, j, i: (gg * r + rr, 0, 0)),
                   pl.BlockSpec((1, s, dk), lambda gg, rr, j, i: (gg, 0, 0)),
                   pl.BlockSpec((1, s, dv), lambda gg, rr, j, i: (gg, 0, 0))],
        out_shape=[jax.ShapeDtypeStruct((h, s, dk), _F32), jax.ShapeDtypeStruct((g, s, dk), _F32),
                   jax.ShapeDtypeStruct((g, s, dv), _F32)],
        compiler_params=_params(("arbitrary", "arbitrary", "arbitrary", "arbitrary")),
    )(*args)


def _attn_delta(o, do, *, name):
    h, s, d = o.shape
    t = _rows(s, 512)

    def body(o_ref, do_ref, d_ref):
        d_ref[0] = jnp.sum(o_ref[0].astype(_F32) * do_ref[0].astype(_F32), axis=-1, keepdims=True)

    blk = pl.BlockSpec((1, t, d), lambda hh, i: (hh, i, 0))
    return pl.pallas_call(
        body, name=name, grid=(h, s // t), in_specs=[blk, blk],
        out_specs=pl.BlockSpec((1, t, 1), lambda hh, i: (hh, i, 0)),
        out_shape=jax.ShapeDtypeStruct((h, s, 1), _F32),
        compiler_params=_params(("parallel", "parallel")),
    )(o, do)


def _merge_weights(lse):
    mx = jnp.max(lse, axis=0, keepdims=True)
    e = jnp.exp(lse - mx)
    return e / jnp.sum(e, axis=0, keepdims=True)


def _merge_fwd(o3, lse3, *, name):
    ng, sl, s, d = o3.shape
    t = _rows(s, 512)

    def body(o_ref, l_ref, c_ref):
        wts = _merge_weights(l_ref[:, 0])
        c_ref[0] = jnp.sum(wts * o_ref[:, 0], axis=0).astype(c_ref.dtype)

    return pl.pallas_call(
        body, name=name, grid=(sl, s // t),
        in_specs=[pl.BlockSpec((ng, 1, t, d), lambda a, i: (0, a, i, 0)),
                  pl.BlockSpec((ng, 1, t, 1), lambda a, i: (0, a, i, 0))],
        out_specs=pl.BlockSpec((1, t, d), lambda a, i: (a, i, 0)),
        out_shape=jax.ShapeDtypeStruct((sl, s, d), _ACT),
        compiler_params=_params(("parallel", "parallel")),
    )(o3, lse3)


def _merge_bwd(dcomb, o3, lse3, *, name):
    ng, sl, s, d = o3.shape
    t = _rows(s, 512)

    def body(dc_ref, o_ref, l_ref, do_ref, dl_ref):
        wts = _merge_weights(l_ref[:, 0])
        dc = dc_ref[0].astype(_F32)
        comb = jnp.sum(wts * o_ref[:, 0], axis=0)
        do_ref[:, 0] = (wts * dc[None]).astype(do_ref.dtype)
        dl_ref[:, 0] = wts * jnp.sum(dc * comb, axis=-1, keepdims=True)[None]

    big = pl.BlockSpec((ng, 1, t, d), lambda a, i: (0, a, i, 0))
    small = pl.BlockSpec((ng, 1, t, 1), lambda a, i: (0, a, i, 0))
    return pl.pallas_call(
        body, name=name, grid=(sl, s // t),
        in_specs=[pl.BlockSpec((1, t, d), lambda a, i: (a, i, 0)), big, small],
        out_specs=[big, small],
        out_shape=[jax.ShapeDtypeStruct((ng, sl, s, d), _ACT), jax.ShapeDtypeStruct((ng, sl, s, 1), _F32)],
        compiler_params=_params(("parallel", "parallel")),
    )(dcomb, o3, lse3)


def _swiglu(gate, up, *, name):
    m, n = gate.shape
    tm, tn = _rows(m, 512), _pick(n, 1408)

    def body(g_ref, u_ref, a_ref):
        gf = g_ref[...].astype(_F32)
        a_ref[...] = (gf * jax.nn.sigmoid(gf) * u_ref[...].astype(_F32)).astype(a_ref.dtype)

    blk = pl.BlockSpec((tm, tn), lambda i, j: (i, j))
    return pl.pallas_call(
        body, name=name, grid=(m // tm, n // tn), in_specs=[blk, blk], out_specs=blk,
        out_shape=jax.ShapeDtypeStruct((m, n), _ACT),
        compiler_params=_params(("parallel", "parallel")),
    )(gate, up)


def _swiglu_bwd(gate, up, da, *, name):
    m, n = gate.shape
    tm, tn = _rows(m, 512), _pick(n, 1408)

    def body(g_ref, u_ref, da_ref, dg_ref, du_ref):
        gf = g_ref[...].astype(_F32)
        uf = u_ref[...].astype(_F32)
        daf = da_ref[...].astype(_F32)
        sg = jax.nn.sigmoid(gf)
        dg_ref[...] = (daf * uf * (sg + gf * sg * (1.0 - sg))).astype(dg_ref.dtype)
        du_ref[...] = (daf * (gf * sg)).astype(du_ref.dtype)

    blk = pl.BlockSpec((tm, tn), lambda i, j: (i, j))
    return pl.pallas_call(
        body, name=name, grid=(m // tm, n // tn), in_specs=[blk, blk, blk], out_specs=[blk, blk],
        out_shape=[jax.ShapeDtypeStruct((m, n), _ACT), jax.ShapeDtypeStruct((m, n), _ACT)],
        compiler_params=_params(("parallel", "parallel")),
    )(gate, up, da)


def _final_loss(x, g, target, *, name):
    m, d = x.shape
    tm = _rows(m, 512)

    def body(x_ref, g_ref, t_ref, loss_ref, dx_ref, dg_ref):
        @pl.when(pl.program_id(0) == 0)
        def _():
            loss_ref[...] = jnp.zeros_like(loss_ref)
            dg_ref[...] = jnp.zeros_like(dg_ref)

        xf = x_ref[...]
        r = lax.rsqrt(jnp.mean(xf * xf, axis=-1, keepdims=True) + NORM_EPS)
        xh = xf * r
        err = xh * g_ref[...] - t_ref[...]
        loss_ref[...] += 0.5 * jnp.sum(jnp.mean(err * err, axis=-1, keepdims=True))
        dy = err * (1.0 / d)
        dg_ref[...] += jnp.sum(dy * xh, axis=0, keepdims=True)
        gdy = dy * g_ref[...]
        dx_ref[...] = r * (gdy - xh * jnp.mean(gdy * xh, axis=-1, keepdims=True))

    row = pl.BlockSpec((tm, d), lambda i: (i, 0))
    vec = pl.BlockSpec((1, d), lambda i: (0, 0))
    loss, dx, dg = pl.pallas_call(
        body, name=name, grid=(m // tm,),
        in_specs=[row, vec, row],
        out_specs=[pl.BlockSpec((8, LANES), lambda i: (0, 0)), row, vec],
        out_shape=[jax.ShapeDtypeStruct((8, LANES), _F32), jax.ShapeDtypeStruct((m, d), _F32),
                   jax.ShapeDtypeStruct((1, d), _F32)],
        compiler_params=_params(("arbitrary",)),
    )(x, g.reshape(1, d), target)
    return loss[0, 0], dx, dg.reshape(d)


def _reduce_adamw(parts, w, m, v, *, name):
    rws, cols = w.shape
    tr = rws
    for cand in range(min(rws, 256), 0, -SUBLANES_16BIT):
        if cand % SUBLANES_16BIT == 0 and rws % cand == 0:
            tr = cand
            break

    def body(p_ref, w_ref, m_ref, v_ref, g_ref, d_ref, nm_ref, nv_ref):
        gsum = p_ref[0].astype(_F32)
        for dev in range(1, N_DEV):
            gsum = gsum + p_ref[dev].astype(_F32)
        m2 = ADAM_B1 * m_ref[...] + (1.0 - ADAM_B1) * gsum
        v2 = ADAM_B2 * v_ref[...] + (1.0 - ADAM_B2) * (gsum * gsum)
        m_hat = m2 / (1.0 - ADAM_B1 ** ADAM_STEP)
        v_hat = v2 / (1.0 - ADAM_B2 ** ADAM_STEP)
        g_ref[...] = gsum
        d_ref[...] = -ADAM_LR * (m_hat / (jnp.sqrt(v_hat) + ADAM_EPS) + ADAM_WD * w_ref[...])
        nm_ref[...] = m2
        nv_ref[...] = v2

    blk = pl.BlockSpec((tr, cols), lambda i: (i, 0))
    out = jax.ShapeDtypeStruct((rws, cols), _F32)
    return pl.pallas_call(
        body, name=name, grid=(rws // tr,),
        in_specs=[pl.BlockSpec((N_DEV, tr, cols), lambda i: (0, i, 0)), blk, blk, blk],
        out_specs=[blk, blk, blk, blk], out_shape=[out, out, out, out],
        compiler_params=_params(("parallel",)),
    )(parts, w, m, v)


def _mesh_pos():
    return lax.axis_index("x"), lax.axis_index("y"), lax.axis_index("c")


def _all_gather(block):
    rws, cols = block.shape

    def body(x_ref, out_ref, send_sems, recv_sems, local_sem):
        x, y, c = _mesh_pos()
        me, sibling = (x, y, c), (x, y, 1 - c)
        chips = [(1 - x, y), (x, 1 - y), (1 - x, 1 - y)]

        def slot(px, py, pc):
            return out_ref.at[4 * px + 2 * py + pc]

        def copy(k, blk, to, src=None):
            return pltpu.make_async_remote_copy(
                src_ref=slot(*blk) if src is None else src, dst_ref=slot(*blk),
                send_sem=send_sems.at[k], recv_sem=recv_sems.at[k],
                device_id=to, device_id_type=pl.DeviceIdType.MESH)

        mine = pltpu.make_async_copy(x_ref, slot(*me), local_sem)
        mine.start()
        first = [copy(0, me, sibling, src=x_ref)]
        first += [copy(1 + j, me, (*chip, c), src=x_ref) for j, chip in enumerate(chips)]
        for cp in first:
            cp.start()
        passed = [copy(4 + j, (*chip, c), sibling) for j, chip in enumerate(chips)]
        for j, chip in enumerate(chips):
            copy(1 + j, (*chip, c), me).wait_recv()
            passed[j].start()
        copy(0, sibling, me).wait_recv()
        for j, chip in enumerate(chips):
            copy(4 + j, (*chip, 1 - c), me).wait_recv()
        for cp in first + passed:
            cp.wait_send()
        mine.wait()

    return pl.pallas_call(
        body, name="weights_all_gather",
        out_shape=jax.ShapeDtypeStruct((N_DEV, rws, cols), block.dtype),
        in_specs=[pl.BlockSpec(memory_space=pl.ANY)],
        out_specs=pl.BlockSpec(memory_space=pl.ANY),
        scratch_shapes=[pltpu.SemaphoreType.DMA((7,)), pltpu.SemaphoreType.DMA((7,)), pltpu.SemaphoreType.DMA],
    )(block)


def _exchange(parts):
    _, rws, cols = parts.shape

    def body(g_ref, out_ref, send_sems, recv_sems, local_sem):
        x, y, c = _mesh_pos()
        me = 4 * x + 2 * y + c
        mine = pltpu.make_async_copy(g_ref.at[me], out_ref.at[me], local_sem)
        mine.start()
        copies = []
        for k in range(1, N_DEV):
            px = 1 - x if k & 4 else x
            py = 1 - y if k & 2 else y
            pc = 1 - c if k & 1 else c
            peer = 4 * px + 2 * py + pc
            send = pltpu.make_async_remote_copy(
                src_ref=g_ref.at[peer], dst_ref=out_ref.at[me],
                send_sem=send_sems.at[k - 1], recv_sem=recv_sems.at[k - 1],
                device_id=(px, py, pc), device_id_type=pl.DeviceIdType.MESH)
            send.start()
            arrival = pltpu.make_async_remote_copy(
                src_ref=g_ref.at[peer], dst_ref=out_ref.at[peer],
                send_sem=send_sems.at[k - 1], recv_sem=recv_sems.at[k - 1],
                device_id=(px, py, pc), device_id_type=pl.DeviceIdType.MESH)
            copies.append((send, arrival))
        for _, arrival in copies:
            arrival.wait_recv()
        for send, _ in copies:
            send.wait_send()
        mine.wait()

    return pl.pallas_call(
        body, name="grads_exchange",
        out_shape=jax.ShapeDtypeStruct(parts.shape, parts.dtype),
        in_specs=[pl.BlockSpec(memory_space=pl.ANY)],
        out_specs=pl.BlockSpec(memory_space=pl.ANY),
        scratch_shapes=[pltpu.SemaphoreType.DMA((7,)), pltpu.SemaphoreType.DMA((7,)), pltpu.SemaphoreType.DMA],
    )(parts)


_BIG = (("w_in_ab", 2), ("mla_w_uq", 1), ("mla_w_ukv", 1), ("w_out_ab", 2), ("gqa_w_q", 1), ("gqa_w_kv", 1),
        ("gqa_w_o", 1), ("ffn_w_in", 2), ("ffn_w_out", 1))
_SMALL = ("mix_norm_ab", "ffn_norm", "final_norm", "mix_norm_c", "mla_q_norm", "mla_kv_norm", "gqa_q_norm", "gqa_k_norm")
SMALL_ROWS = 16
GATHER_TAIL_ROWS = 16
EXCHANGE_TAIL_ROWS = 32


def _view3(a):
    return a.reshape(a.shape[0], a.shape[1], -1)


def _pad_rows(a2d):
    pad = -a2d.shape[0] % SUBLANES_16BIT
    return jnp.pad(a2d, ((0, pad), (0, 0))) if pad else a2d


def _pack_rows(shard):
    return _pad_rows(shard.reshape(-1, PACK_COLS))


def _packed_rows(shape):
    n = 1
    for d in shape:
        n *= d
    rows = n // PACK_COLS
    return rows + (-rows % SUBLANES_16BIT)


def _pack_big(shards):
    return jnp.concatenate([_pack_rows(shards[n]) for n, _ in _BIG], axis=0)


def _unpack_big(packed, shapes):
    out, off = {}, 0
    for n, _ in _BIG:
        size = 1
        for d in shapes[n]:
            size *= d
        out[n] = packed[off:off + size // PACK_COLS].reshape(shapes[n])
        off += _packed_rows(shapes[n])
    return out


def _unpack_gathered(gathered, shapes):
    out, off = {}, 0
    for n, axis in _BIG:
        size = 1
        for d in shapes[n]:
            size *= d
        l3 = (shapes[n][0], shapes[n][1], size // (shapes[n][0] * shapes[n][1]))
        sh = gathered[:, off:off + size // PACK_COLS].reshape((N_DEV,) + l3)
        if axis == 1:
            full = sh.transpose(1, 0, 2, 3).reshape(l3[0], N_DEV * l3[1], l3[2])
        else:
            full = sh.transpose(1, 2, 0, 3).reshape(l3[0], l3[1], N_DEV * l3[2])
        out[n] = full
        off += _packed_rows(shapes[n])
    return out


def _split_for_devices(full, axis):
    l, rws, cols = full.shape
    if axis == 1:
        sh = full.reshape(l, N_DEV, rws // N_DEV, cols).transpose(1, 0, 2, 3)
    else:
        sh = full.reshape(l, rws, N_DEV, cols // N_DEV).transpose(2, 0, 1, 3)
    flat = sh.reshape(N_DEV, -1, PACK_COLS)
    pad = -flat.shape[1] % SUBLANES_16BIT
    return jnp.pad(flat, ((0, 0), (0, pad), (0, 0))) if pad else flat


def _to_bits16(a_f32_rows):
    r = a_f32_rows.shape[0]
    return lax.bitcast_convert_type(a_f32_rows, jnp.bfloat16).reshape(2 * r, PACK_COLS)


def _from_bits16(a_bf16_rows):
    lead, r = a_bf16_rows.shape[:-2], a_bf16_rows.shape[-2]
    return lax.bitcast_convert_type(a_bf16_rows.reshape(lead + (r // 2, PACK_COLS, 2)), _F32)


def _small_sizes():
    return {"mix_norm_ab": 2 * D_MODEL, "ffn_norm": DEPTH * D_MODEL, "final_norm": D_MODEL, "mix_norm_c": 2 * D_MODEL,
            "mla_q_norm": 2 * MLA_Q_RANK, "mla_kv_norm": 2 * MLA_KV_RANK, "gqa_q_norm": 2 * GQA_HEAD_DIM,
            "gqa_k_norm": 2 * GQA_HEAD_DIM}


def _pack_small(vals):
    flat = jnp.concatenate([vals[n].reshape(-1).astype(_F32) for n in _SMALL])
    return jnp.pad(flat, (0, SMALL_ROWS * PACK_COLS - flat.shape[0])).reshape(SMALL_ROWS, PACK_COLS)


def _unpack_small(pack):
    flat, out, off = pack.reshape(-1), {}, 0
    sizes = _small_sizes()
    for n in _SMALL:
        out[n] = flat[off:off + sizes[n]]
        off += sizes[n]
    return out


def _angles(pos, dim):
    freqs = ROPE_THETA ** (-jnp.arange(0, dim, 2, dtype=_F32) / dim)
    ang = pos.astype(_F32)[:, None] * freqs[None, :]
    return jnp.cos(ang), jnp.sin(ang)


def _rope_tables(s):
    pos = jnp.arange(s)
    cos_t, sin_t = _angles(pos, MLA_ROPE)
    mla_c = jnp.tile(jnp.concatenate([cos_t, cos_t], -1), (1, LANES // 32))
    mla_s = jnp.tile(jnp.concatenate([-sin_t, sin_t], -1), (1, LANES // 32))
    rows = s // GRID_W
    row_idx = jnp.broadcast_to(jnp.arange(rows)[:, None], (rows, GRID_W)).reshape(-1)
    col_idx = jnp.broadcast_to(jnp.arange(GRID_W)[None, :], (rows, GRID_W)).reshape(-1)
    cos_r, sin_r = _angles(row_idx, GQA_HEAD_DIM // 2)
    cos_c, sin_c = _angles(col_idx, GQA_HEAD_DIM // 2)
    gqa_c = jnp.tile(jnp.concatenate([cos_r, cos_r, cos_c, cos_c], -1), (1, LANES // GQA_HEAD_DIM))
    gqa_s = jnp.tile(jnp.concatenate([-sin_r, sin_r, -sin_c, sin_c], -1), (1, LANES // GQA_HEAD_DIM))
    return (mla_c, mla_s), (gqa_c, gqa_s)


def _heads(x2d, h):
    s = x2d.shape[0]
    return x2d.reshape(s, h, -1).transpose(1, 0, 2)


def _unheads(xh):
    h, s, d = xh.shape
    return xh.transpose(1, 0, 2).reshape(s, h * d)


def _ffn_fwd(x, w, tag):
    hn = _rmsnorm(x, w["norm"], out_dtype=_ACT, name=f"{tag}_norm")
    gate = _matmul(hn, w["w_gate"], out_dtype=_ACT, name=f"{tag}_gate")
    up = _matmul(hn, w["w_up"], out_dtype=_ACT, name=f"{tag}_up")
    act = _swiglu(gate, up, name=f"{tag}_swiglu")
    out = _matmul(act, w["w_out"], res=x, name=f"{tag}_out")
    return out, (x, hn, gate, up, act)


def _ffn_bwd(dout, saved, w, tag):
    x, hn, gate, up, act = saved
    dact = _matmul(dout, w["w_out"], trans_b=True, out_dtype=_ACT, name=f"{tag}_dact")
    d_w_out = _matmul(act, dout, trans_a=True, name=f"{tag}_dwout")
    dgate, dup = _swiglu_bwd(gate, up, dact, name=f"{tag}_dswiglu")
    d_w_gate = _matmul(hn, dgate, trans_a=True, name=f"{tag}_dwgate")
    d_w_up = _matmul(hn, dup, trans_a=True, name=f"{tag}_dwup")
    dhn = _matmul(dgate, w["w_gate"], trans_b=True, name=f"{tag}_dhn_gate")
    dhn = _matmul(dup, w["w_up"], trans_b=True, res=dhn, name=f"{tag}_dhn_up")
    dx, dnorm = _rmsnorm_bwd(x, w["norm"], dhn, dout, name=f"{tag}_dnorm")
    return dx, {"norm": dnorm, "w_gate": d_w_gate, "w_up": d_w_up, "w_out": d_w_out}


def _even_fwd(x, w, tabs, slopes, tag):
    s = x.shape[0]
    (mla_c, mla_s), _ = tabs
    hn = _rmsnorm(x, w["norm"], out_dtype=_ACT, name=f"{tag}_norm")
    za = _matmul(hn, w["w_a"], name=f"{tag}_in_a")
    zb = _matmul(hn, w["w_b"], out_dtype=_ACT, name=f"{tag}_in_b")
    cq, ckv, kr = za[:, :MLA_Q_RANK], za[:, MLA_Q_RANK:MLA_Q_RANK + MLA_KV_RANK], za[:, MLA_Q_RANK + MLA_KV_RANK:]
    cqn = _rmsnorm(cq, w["q_norm"], out_dtype=_ACT, name=f"{tag}_qnorm")
    ckvn = _rmsnorm(ckv, w["kv_norm"], out_dtype=_ACT, name=f"{tag}_kvnorm")
    q = _matmul(cqn, w["w_uq"], name=f"{tag}_uq")
    kv = _matmul(ckvn, w["w_ukv"], out_dtype=_ACT, name=f"{tag}_ukv")
    nn = MLA_HEADS * MLA_NOPE
    q_rope = _rope(q[:, nn:], mla_c, mla_s, out_dtype=_ACT, name=f"{tag}_rope_q")
    k_rope = _rope(kr, mla_c, mla_s, out_dtype=_ACT, name=f"{tag}_rope_k")[:, :MLA_ROPE]
    qh = jnp.concatenate([_heads(q[:, :nn].astype(_ACT), MLA_HEADS), _heads(q_rope, MLA_HEADS)], axis=-1)
    kh = jnp.concatenate([_heads(kv[:, :nn], MLA_HEADS),
                          jnp.broadcast_to(k_rope[None], (MLA_HEADS, s, MLA_ROPE))], axis=-1)
    vh = _heads(kv[:, nn:], MLA_HEADS)
    oa, lse_a = _flash_fwd(qh, kh, vh, scale=(MLA_NOPE + MLA_ROPE) ** -0.5, out_dtype=_ACT, name=f"{tag}_mla")

    nd = DIL_HEADS * DIL_HEAD_DIM
    qd, kd, vd = (_heads(zb[:, a * nd:(a + 1) * nd], DIL_HEADS) for a in range(3))
    outs, lses = [], []
    for gi, (_, dil) in enumerate(DIL_PAIRS):
        hs = slice(gi * DIL_SLOTS, (gi + 1) * DIL_SLOTS)
        o, l = _flash_fwd(qd[hs], kd[hs], vd[hs], scale=DIL_HEAD_DIM ** -0.5, dil=dil, slopes=slopes[hs],
                          out_dtype=_F32, name=f"{tag}_dil{gi}")
        outs.append(o)
        lses.append(l)
    o3, lse3 = jnp.stack(outs), jnp.stack(lses)
    comb = _merge_fwd(o3, lse3, name=f"{tag}_merge")
    cat = jnp.concatenate([_unheads(oa), _unheads(comb)], axis=-1)
    out = _matmul(cat, w["w_out"], res=x, name=f"{tag}_out")
    return out, (x, hn, cq, ckv, cqn, ckvn, qh, kh, vh, oa, lse_a, qd, kd, vd, o3, lse3, cat)


def _even_bwd(dout, saved, w, tabs, slopes, tag):
    x, hn, cq, ckv, cqn, ckvn, qh, kh, vh, oa, lse_a, qd, kd, vd, o3, lse3, cat = saved
    (mla_c, mla_s), _ = tabs
    nn = MLA_HEADS * MLA_NOPE
    dcat = _matmul(dout, w["w_out"], trans_b=True, out_dtype=_ACT, name=f"{tag}_dcat")
    d_w_out = _matmul(cat, dout, trans_a=True, name=f"{tag}_dwout")
    nv = MLA_HEADS * MLA_V

    doa = _heads(dcat[:, :nv], MLA_HEADS)
    delta = _attn_delta(oa, doa, name=f"{tag}_mla_delta")
    dqh, dkh, dvh = _flash_bwd(qh, kh, vh, doa, lse_a, delta, scale=(MLA_NOPE + MLA_ROPE) ** -0.5,
                               name=f"{tag}_mla_bwd")
    dq_rope = _rope(_unheads(dqh[..., MLA_NOPE:]), mla_c, -mla_s, out_dtype=_ACT, name=f"{tag}_drope_q")
    dq = jnp.concatenate([_unheads(dqh[..., :MLA_NOPE]).astype(_ACT), dq_rope], axis=-1)
    dkr = _rope(_unheads(dkh[..., MLA_NOPE:]), mla_c, -mla_s, out_dtype=_F32, sum_chunks=True,
                name=f"{tag}_drope_k")
    dkv = jnp.concatenate([_unheads(dkh[..., :MLA_NOPE]), _unheads(dvh)], axis=-1).astype(_ACT)
    d_w_uq = _matmul(cqn, dq, trans_a=True, name=f"{tag}_dwuq")
    d_w_ukv = _matmul(ckvn, dkv, trans_a=True, name=f"{tag}_dwukv")
    dcqn = _matmul(dq, w["w_uq"], trans_b=True, name=f"{tag}_dcqn")
    dckvn = _matmul(dkv, w["w_ukv"], trans_b=True, name=f"{tag}_dckvn")
    dcq, d_q_norm = _rmsnorm_bwd(cq, w["q_norm"], dcqn, out_dtype=_ACT, name=f"{tag}_dqnorm")
    dckv, d_kv_norm = _rmsnorm_bwd(ckv, w["kv_norm"], dckvn, out_dtype=_ACT, name=f"{tag}_dkvnorm")
    lane = jnp.arange(LANES) < MLA_ROPE
    dza = jnp.concatenate([dcq, dckv, jnp.where(lane[None], dkr, 0.0).astype(_ACT)], axis=-1)

    dcomb = _heads(dcat[:, nv:], DIL_SLOTS)
    do3, delta3 = _merge_bwd(dcomb, o3, lse3, name=f"{tag}_dmerge")
    dqs, dks, dvs = [], [], []
    for gi, (_, dil) in enumerate(DIL_PAIRS):
        hs = slice(gi * DIL_SLOTS, (gi + 1) * DIL_SLOTS)
        a, b, c = _flash_bwd(qd[hs], kd[hs], vd[hs], do3[gi], lse3[gi], delta3[gi], scale=DIL_HEAD_DIM ** -0.5,
                             dil=dil, slopes=slopes[hs], name=f"{tag}_dil{gi}_bwd")
        dqs.append(_unheads(a))
        dks.append(_unheads(b))
        dvs.append(_unheads(c))
    dzb = jnp.concatenate(dqs + dks + dvs, axis=-1).astype(_ACT)

    d_w_a = _matmul(hn, dza, trans_a=True, name=f"{tag}_dwa")
    d_w_b = _matmul(hn, dzb, trans_a=True, name=f"{tag}_dwb")
    dhn = _matmul(dza, w["w_a"], trans_b=True, name=f"{tag}_dhn_a")
    dhn = _matmul(dzb, w["w_b"], trans_b=True, res=dhn, name=f"{tag}_dhn_b")
    dx, dnorm = _rmsnorm_bwd(x, w["norm"], dhn, dout, name=f"{tag}_dnorm")
    return dx, {"norm": dnorm, "w_a": d_w_a, "w_b": d_w_b, "q_norm": d_q_norm, "kv_norm": d_kv_norm,
                "w_uq": d_w_uq, "w_ukv": d_w_ukv, "w_out": d_w_out}


def _odd_fwd(x, w, tabs, tag):
    s = x.shape[0]
    _, (gqa_c, gqa_s) = tabs
    nk = GQA_KV_HEADS * GQA_HEAD_DIM
    hn = _rmsnorm(x, w["norm"], out_dtype=_ACT, name=f"{tag}_norm")
    q = _matmul(hn, w["w_q"], name=f"{tag}_q")
    kv = _matmul(hn, w["w_kv"], name=f"{tag}_kv")
    k = kv[:, :nk]
    qn = _rmsnorm(q.reshape(s * GQA_HEADS, GQA_HEAD_DIM), w["q_norm"], out_dtype=_F32, name=f"{tag}_qnorm",
                  rows=4096).reshape(s, -1)
    kn = _rmsnorm(k.reshape(s * GQA_KV_HEADS, GQA_HEAD_DIM), w["k_norm"], out_dtype=_F32, name=f"{tag}_knorm",
                  rows=4096).reshape(s, -1)
    qh = _heads(_rope(qn, gqa_c, gqa_s, out_dtype=_ACT, name=f"{tag}_rope_q"), GQA_HEADS)
    kh = _heads(_rope(kn, gqa_c, gqa_s, out_dtype=_ACT, name=f"{tag}_rope_k"), GQA_KV_HEADS)
    vh = _heads(kv[:, nk:].astype(_ACT), GQA_KV_HEADS)
    o, lse = _flash_fwd(qh, kh, vh, scale=GQA_HEAD_DIM ** -0.5, out_dtype=_ACT, name=f"{tag}_gqa")
    ocat = _unheads(o)
    out = _matmul(ocat, w["w_o"], res=x, name=f"{tag}_out")
    return out, (x, hn, q, k, qh, kh, vh, o, lse, ocat)


def _odd_bwd(dout, saved, w, tabs, tag):
    x, hn, q, k, qh, kh, vh, o, lse, ocat = saved
    s = x.shape[0]
    _, (gqa_c, gqa_s) = tabs
    docat = _matmul(dout, w["w_o"], trans_b=True, out_dtype=_ACT, name=f"{tag}_docat")
    d_w_o = _matmul(ocat, dout, trans_a=True, name=f"{tag}_dwo")
    doh = _heads(docat, GQA_HEADS)
    delta = _attn_delta(o, doh, name=f"{tag}_gqa_delta")
    dqh, dkh, dvh = _flash_bwd(qh, kh, vh, doh, lse, delta, scale=GQA_HEAD_DIM ** -0.5, name=f"{tag}_gqa_bwd")
    dqn = _rope(_unheads(dqh), gqa_c, -gqa_s, out_dtype=_F32, name=f"{tag}_drope_q")
    dkn = _rope(_unheads(dkh), gqa_c, -gqa_s, out_dtype=_F32, name=f"{tag}_drope_k")
    dq, d_q_norm = _rmsnorm_bwd(q.reshape(s * GQA_HEADS, GQA_HEAD_DIM), w["q_norm"],
                                dqn.reshape(s * GQA_HEADS, GQA_HEAD_DIM), out_dtype=_ACT, name=f"{tag}_dqnorm", rows=4096)
    dk, d_k_norm = _rmsnorm_bwd(k.reshape(s * GQA_KV_HEADS, GQA_HEAD_DIM), w["k_norm"],
                                dkn.reshape(s * GQA_KV_HEADS, GQA_HEAD_DIM), out_dtype=_ACT, name=f"{tag}_dknorm",
                                rows=4096)
    dq = dq.reshape(s, -1)
    dkv = jnp.concatenate([dk.reshape(s, -1), _unheads(dvh).astype(_ACT)], axis=-1)
    d_w_q = _matmul(hn, dq, trans_a=True, name=f"{tag}_dwq")
    d_w_kv = _matmul(hn, dkv, trans_a=True, name=f"{tag}_dwkv")
    dhn = _matmul(dq, w["w_q"], trans_b=True, name=f"{tag}_dhn_q")
    dhn = _matmul(dkv, w["w_kv"], trans_b=True, res=dhn, name=f"{tag}_dhn_kv")
    dx, dnorm = _rmsnorm_bwd(x, w["norm"], dhn, dout, name=f"{tag}_dnorm")
    return dx, {"norm": dnorm, "w_q": d_w_q, "w_kv": d_w_kv, "q_norm": d_q_norm, "k_norm": d_k_norm, "w_o": d_w_o}


def _split_heads_cols(wm, heads, first):
    rws = wm.shape[0]
    w3 = wm.reshape(rws, heads, -1)
    return jnp.concatenate([w3[:, :, :first].reshape(rws, -1), w3[:, :, first:].reshape(rws, -1)], axis=-1)


def _merge_heads_cols(wm, heads, first):
    rws, cols = wm.shape
    a = wm[:, :heads * first].reshape(rws, heads, first)
    b = wm[:, heads * first:].reshape(rws, heads, cols // heads - first)
    return jnp.concatenate([a, b], axis=-1).reshape(rws, cols)


def _layer_weights(full, gains, layer):
    i = layer // 2
    ffn = {"norm": gains["ffn_norm"][layer], "w_gate": full["ffn_w_in"][layer][:, :FFN_HIDDEN],
           "w_up": full["ffn_w_in"][layer][:, FFN_HIDDEN:], "w_out": full["ffn_w_out"][layer]}
    if layer % 2 == 0:
        w_in = full["w_in_ab"][i]
        mix = {"norm": gains["mix_norm_ab"][i],
               "w_a": jnp.pad(w_in[:, :IN_A], ((0, 0), (0, IN_A_PAD - IN_A))), "w_b": w_in[:, IN_A:],
               "q_norm": gains["mla_q_norm"][i], "kv_norm": gains["mla_kv_norm"][i],
               "w_uq": _split_heads_cols(full["mla_w_uq"][i], MLA_HEADS, MLA_NOPE),
               "w_ukv": _split_heads_cols(full["mla_w_ukv"][i], MLA_HEADS, MLA_NOPE),
               "w_out": full["w_out_ab"][i]}
    else:
        mix = {"norm": gains["mix_norm_c"][i], "w_q": full["gqa_w_q"][i], "w_kv": full["gqa_w_kv"][i],
               "q_norm": gains["gqa_q_norm"][i], "k_norm": gains["gqa_k_norm"][i], "w_o": full["gqa_w_o"][i]}
    return mix, ffn


def kernel(x, mix_norm_ab, w_in_ab, mla_q_norm, mla_kv_norm, mla_w_uq, mla_w_ukv, w_out_ab, mix_norm_c, gqa_w_q, gqa_w_kv, gqa_q_norm, gqa_k_norm, gqa_w_o, ffn_norm, ffn_w_in, ffn_w_out, final_norm, loss_target, m_mix_norm_ab, m_w_in_ab, m_mla_q_norm, m_mla_kv_norm, m_mla_w_uq, m_mla_w_ukv, m_w_out_ab, m_mix_norm_c, m_gqa_w_q, m_gqa_w_kv, m_gqa_q_norm, m_gqa_k_norm, m_gqa_w_o, m_ffn_norm, m_ffn_w_in, m_ffn_w_out, m_final_norm, v_mix_norm_ab, v_w_in_ab, v_mla_q_norm, v_mla_kv_norm, v_mla_w_uq, v_mla_w_ukv, v_w_out_ab, v_mix_norm_c, v_gqa_w_q, v_gqa_w_kv, v_gqa_q_norm, v_gqa_k_norm, v_gqa_w_o, v_ffn_norm, v_ffn_w_in, v_ffn_w_out, v_final_norm):
    wts = dict(mix_norm_ab=mix_norm_ab, w_in_ab=w_in_ab, mla_q_norm=mla_q_norm, mla_kv_norm=mla_kv_norm,
               mla_w_uq=mla_w_uq, mla_w_ukv=mla_w_ukv, w_out_ab=w_out_ab, mix_norm_c=mix_norm_c, gqa_w_q=gqa_w_q,
               gqa_w_kv=gqa_w_kv, gqa_q_norm=gqa_q_norm, gqa_k_norm=gqa_k_norm, gqa_w_o=gqa_w_o, ffn_norm=ffn_norm,
               ffn_w_in=ffn_w_in, ffn_w_out=ffn_w_out, final_norm=final_norm)
    mom = dict(mix_norm_ab=m_mix_norm_ab, w_in_ab=m_w_in_ab, mla_q_norm=m_mla_q_norm, mla_kv_norm=m_mla_kv_norm,
               mla_w_uq=m_mla_w_uq, mla_w_ukv=m_mla_w_ukv, w_out_ab=m_w_out_ab, mix_norm_c=m_mix_norm_c,
               gqa_w_q=m_gqa_w_q, gqa_w_kv=m_gqa_w_kv, gqa_q_norm=m_gqa_q_norm, gqa_k_norm=m_gqa_k_norm,
               gqa_w_o=m_gqa_w_o, ffn_norm=m_ffn_norm, ffn_w_in=m_ffn_w_in, ffn_w_out=m_ffn_w_out,
               final_norm=m_final_norm)
    var = dict(mix_norm_ab=v_mix_norm_ab, w_in_ab=v_w_in_ab, mla_q_norm=v_mla_q_norm, mla_kv_norm=v_mla_kv_norm,
               mla_w_uq=v_mla_w_uq, mla_w_ukv=v_mla_w_ukv, w_out_ab=v_w_out_ab, mix_norm_c=v_mix_norm_c,
               gqa_w_q=v_gqa_w_q, gqa_w_kv=v_gqa_w_kv, gqa_q_norm=v_gqa_q_norm, gqa_k_norm=v_gqa_k_norm,
               gqa_w_o=v_gqa_w_o, ffn_norm=v_ffn_norm, ffn_w_in=v_ffn_w_in, ffn_w_out=v_ffn_w_out,
               final_norm=v_final_norm)
    big_names = [n for n, _ in _BIG]
    shard_shapes = {n: wts[n].shape for n in big_names}
    xs = x[0]
    s = xs.shape[0]
    me = 4 * lax.axis_index("x") + 2 * lax.axis_index("y") + lax.axis_index("c")
    c_cols = mix_norm_c.shape[1]

    w_pack = _pack_big({n: wts[n] for n in big_names})
    n_rows = w_pack.shape[0]
    tail = jnp.pad(mix_norm_c.reshape(-1), (0, GATHER_TAIL_ROWS // 2 * PACK_COLS - mix_norm_c.size))
    tail = _to_bits16(tail.reshape(GATHER_TAIL_ROWS // 2, PACK_COLS))
    gathered = _all_gather(jnp.concatenate([w_pack.astype(jnp.bfloat16), tail], axis=0))
    full = _unpack_gathered(gathered[:, :n_rows], shard_shapes)
    c_all = _from_bits16(gathered[:, n_rows:]).reshape(N_DEV, -1)[:, :mix_norm_c.size]
    c_full = c_all.reshape(N_DEV, 2, c_cols).transpose(1, 0, 2).reshape(2, N_DEV * c_cols)
    gains = dict(mix_norm_ab=mix_norm_ab, mla_q_norm=mla_q_norm, mla_kv_norm=mla_kv_norm, mix_norm_c=c_full,
                 gqa_q_norm=gqa_q_norm, gqa_k_norm=gqa_k_norm, ffn_norm=ffn_norm)

    tabs = _rope_tables(s)
    slopes = jnp.exp2(-8.0 * jnp.arange(1, DIL_HEADS + 1, dtype=_F32) / DIL_HEADS)

    h = xs
    saved = []
    for layer in range(DEPTH):
        mix_w, ffn_w = _layer_weights(full, gains, layer)
        if layer % 2 == 0:
            h, sv_mix = _even_fwd(h, mix_w, tabs, slopes, f"l{layer}_mix")
        else:
            h, sv_mix = _odd_fwd(h, mix_w, tabs, f"l{layer}_mix")
        h, sv_ffn = _ffn_fwd(h, ffn_w, f"l{layer}_ffn")
        saved.append((mix_w, ffn_w, sv_mix, sv_ffn))
    loss_local, dh, d_final = _final_loss(h, final_norm, loss_target[0], name="final_loss")

    gfull = {n: [None] * wts[n].shape[0] for n in big_names}
    gsmall = {n: [None] * (wts[n].shape[0] if wts[n].ndim > 1 else 1) for n in _SMALL}
    gsmall["final_norm"][0] = d_final
    for layer in reversed(range(DEPTH)):
        mix_w, ffn_w, sv_mix, sv_ffn = saved[layer]
        i = layer // 2
        dh, gf = _ffn_bwd(dh, sv_ffn, ffn_w, f"l{layer}_ffn")
        gsmall["ffn_norm"][layer] = gf["norm"]
        gfull["ffn_w_in"][layer] = jnp.concatenate([gf["w_gate"], gf["w_up"]], axis=-1)
        gfull["ffn_w_out"][layer] = gf["w_out"]
        if layer % 2 == 0:
            dh, gm = _even_bwd(dh, sv_mix, mix_w, tabs, slopes, f"l{layer}_mix")
            gsmall["mix_norm_ab"][i] = gm["norm"]
            gsmall["mla_q_norm"][i] = gm["q_norm"]
            gsmall["mla_kv_norm"][i] = gm["kv_norm"]
            gfull["w_in_ab"][i] = jnp.concatenate([gm["w_a"][:, :IN_A], gm["w_b"]], axis=-1)
            gfull["mla_w_uq"][i] = _merge_heads_cols(gm["w_uq"], MLA_HEADS, MLA_NOPE)
            gfull["mla_w_ukv"][i] = _merge_heads_cols(gm["w_ukv"], MLA_HEADS, MLA_NOPE)
            gfull["w_out_ab"][i] = gm["w_out"]
        else:
            dh, gm = _odd_bwd(dh, sv_mix, mix_w, tabs, f"l{layer}_mix")
            gsmall["mix_norm_c"][i] = gm["norm"]
            gsmall["gqa_q_norm"][i] = gm["q_norm"]
            gsmall["gqa_k_norm"][i] = gm["k_norm"]
            gfull["gqa_w_q"][i] = gm["w_q"]
            gfull["gqa_w_kv"][i] = gm["w_kv"]
            gfull["gqa_w_o"][i] = gm["w_o"]
    grad_x = dh[None]

    g_parts = jnp.concatenate([_split_for_devices(jnp.stack(gfull[n]), axis) for n, axis in _BIG], axis=1)
    small_part = _pack_small({n: jnp.stack(gsmall[n]) for n in _SMALL})
    small_bits = jnp.broadcast_to(_to_bits16(small_part)[None], (N_DEV, EXCHANGE_TAIL_ROWS, PACK_COLS))
    received = _exchange(jnp.concatenate([g_parts.astype(jnp.bfloat16), small_bits], axis=1))
    g_big, d_big, m_big, v_big = _reduce_adamw(
        received, w_pack, _pack_big({n: mom[n] for n in big_names}), _pack_big({n: var[n] for n in big_names}),
        name="adamw_big")

    def widen_c(shard):
        return lax.dynamic_update_slice(jnp.zeros((2, N_DEV * c_cols), _F32), shard, (0, me * c_cols))

    def small_of(src):
        return _pack_small({n: (widen_c(src[n]) if n == "mix_norm_c" else src[n]) for n in _SMALL})

    small_recv = _from_bits16(received[:, n_rows:])
    g_sm, d_sm, m_sm, v_sm = _reduce_adamw(small_recv, small_of(wts), small_of(mom), small_of(var), name="adamw_small")

    def outputs_of(big_pack, small_pack):
        big = _unpack_big(big_pack, shard_shapes)
        small = _unpack_small(small_pack)
        res = {}
        for n in wts:
            if n in big:
                res[n] = big[n]
            elif n == "mix_norm_c":
                res[n] = lax.dynamic_slice(small[n].reshape(2, N_DEV * c_cols), (0, me * c_cols), (2, c_cols))
            else:
                res[n] = small[n].reshape(wts[n].shape)
        return [res[n] for n in wts]

    loss = lax.psum(loss_local, _AXES)
    return (loss, grad_x, *outputs_of(g_big, g_sm), *outputs_of(d_big, d_sm), *outputs_of(m_big, m_sm),
            *outputs_of(v_big, v_sm))
```

```python
import functools

import jax
import jax.numpy as jnp
from jax import lax
from jax.experimental import pallas as pl
from jax.experimental.pallas import tpu as pltpu

D_MODEL = 1024
DEPTH = 4
GRID_W = 64
NORM_EPS = 1e-6
ROPE_THETA = 10000.0
NEG_INF = -1e30
MLA_HEADS = 8
MLA_Q_RANK = 384
MLA_KV_RANK = 256
MLA_NOPE = 64
MLA_ROPE = 32
MLA_V = 64
DIL_PAIRS = ((128, 1), (512, 4), (2048, 16))
DIL_HALF = 64
DIL_SLOTS = 4
DIL_GROUPS = 3
DIL_HEADS = 12
DIL_HEAD_DIM = 64
GQA_HEADS = 16
GQA_KV_HEADS = 4
GQA_HEAD_DIM = 64
FFN_HIDDEN = 2816
IN_A = MLA_Q_RANK + MLA_KV_RANK + MLA_ROPE
IN_A_PAD = 768
IN_B = 3 * DIL_HEADS * DIL_HEAD_DIM
ADAM_LR = 0.001
ADAM_B1 = 0.9
ADAM_B2 = 0.999
ADAM_EPS = 1e-08
ADAM_WD = 0.01
ADAM_STEP = 10

LANES = 128
SUBLANES_16BIT = 16
VMEM_LIMIT_BYTES = 56 * 1024 * 1024

N_DEV = 8
PACK_COLS = 1024

_MXU = jnp.bfloat16
_ACT = jnp.bfloat16
_F32 = jnp.float32

_AXES = ("x", "y", "c")


def _params(sem):
    return pltpu.CompilerParams(dimension_semantics=sem, vmem_limit_bytes=VMEM_LIMIT_BYTES)


def _pick(n, cap):
    for t in range(cap - cap % LANES, 0, -LANES):
        if n % t == 0:
            return t
    return n


def _rows(m, target):
    t = m
    while t > target and t % 2 == 0:
        t //= 2
    return t


def _matmul(a, b, *, trans_a=False, trans_b=False, res=None, scale=None, out_dtype=_F32, name):
    if trans_a:
        k, m = a.shape
    else:
        m, k = a.shape
    if trans_b:
        n, kb = b.shape
    else:
        kb, n = b.shape
    assert k == kb, (a.shape, b.shape)
    tm, tn, tk = _pick(m, 1024), _pick(n, 1408), _pick(k, 512)
    nk = k // tk
    dims = (((0 if trans_a else 1,), (1 if trans_b else 0,)), ((), ()))

    def body(*refs):
        if res is None:
            a_ref, b_ref, o_ref, acc = refs
            r_ref = None
        else:
            a_ref, b_ref, r_ref, o_ref, acc = refs
        kk = pl.program_id(2)

        @pl.when(kk == 0)
        def _():
            acc[...] = jnp.zeros_like(acc)

        acc[...] += lax.dot_general(a_ref[...].astype(_MXU), b_ref[...].astype(_MXU), dims,
                                    preferred_element_type=_F32)

        @pl.when(kk == nk - 1)
        def _():
            r = acc[...]
            if scale is not None:
                r = r * scale
            if r_ref is not None:
                r = r + r_ref[...].astype(_F32)
            o_ref[...] = r.astype(out_dtype)

    a_spec = (pl.BlockSpec((tk, tm), lambda i, j, kk: (kk, i)) if trans_a
              else pl.BlockSpec((tm, tk), lambda i, j, kk: (i, kk)))
    b_spec = (pl.BlockSpec((tn, tk), lambda i, j, kk: (j, kk)) if trans_b
              else pl.BlockSpec((tk, tn), lambda i, j, kk: (kk, j)))
    o_spec = pl.BlockSpec((tm, tn), lambda i, j, kk: (i, j))
    in_specs = [a_spec, b_spec] + ([o_spec] if res is not None else [])
    args = (a, b) + ((res,) if res is not None else ())
    return pl.pallas_call(
        body, name=name, grid=(m // tm, n // tn, nk),
        in_specs=in_specs, out_specs=o_spec,
        out_shape=jax.ShapeDtypeStruct((m, n), out_dtype),
        scratch_shapes=[pltpu.VMEM((tm, tn), _F32)],
        compiler_params=_params(("parallel", "parallel", "arbitrary")),
    )(*args)


def _rmsnorm(x, g, *, out_dtype, name, rows=512):
    m, d = x.shape
    tm = _rows(m, rows)

    def body(x_ref, g_ref, o_ref):
        xf = x_ref[...].astype(_F32)
        r = lax.rsqrt(jnp.mean(xf * xf, axis=-1, keepdims=True) + NORM_EPS)
        o_ref[...] = ((xf * r) * g_ref[...]).astype(out_dtype)

    return pl.pallas_call(
        body, name=name, grid=(m // tm,),
        in_specs=[pl.BlockSpec((tm, d), lambda i: (i, 0)), pl.BlockSpec((1, d), lambda i: (0, 0))],
        out_specs=pl.BlockSpec((tm, d), lambda i: (i, 0)),
        out_shape=jax.ShapeDtypeStruct((m, d), out_dtype),
        compiler_params=_params(("parallel",)),
    )(x, g.reshape(1, d).astype(_F32))


def _rmsnorm_bwd(x, g, dy, dres=None, *, out_dtype=_F32, name, rows=512):
    m, d = x.shape
    tm = _rows(m, rows)

    def body(*refs):
        if dres is None:
            x_ref, g_ref, dy_ref, dx_ref, dg_ref = refs
            r_ref = None
        else:
            x_ref, g_ref, dy_ref, r_ref, dx_ref, dg_ref = refs

        @pl.when(pl.program_id(0) == 0)
        def _():
            dg_ref[...] = jnp.zeros_like(dg_ref)

        xf = x_ref[...].astype(_F32)
        r = lax.rsqrt(jnp.mean(xf * xf, axis=-1, keepdims=True) + NORM_EPS)
        xh = xf * r
        dyf = dy_ref[...].astype(_F32)
        dg_ref[...] += jnp.sum(dyf * xh, axis=0, keepdims=True)
        gdy = dyf * g_ref[...]
        dx = r * (gdy - xh * jnp.mean(gdy * xh, axis=-1, keepdims=True))
        if r_ref is not None:
            dx = dx + r_ref[...].astype(_F32)
        dx_ref[...] = dx.astype(out_dtype)

    row = pl.BlockSpec((tm, d), lambda i: (i, 0))
    vec = pl.BlockSpec((1, d), lambda i: (0, 0))
    in_specs = [row, vec, row] + ([row] if dres is not None else [])
    args = (x, g.reshape(1, d).astype(_F32), dy) + ((dres,) if dres is not None else ())
    dx, dg = pl.pallas_call(
        body, name=name, grid=(m // tm,),
        in_specs=in_specs, out_specs=[row, vec],
        out_shape=[jax.ShapeDtypeStruct((m, d), out_dtype), jax.ShapeDtypeStruct((1, d), _F32)],
        compiler_params=_params(("arbitrary",)),
    )(*args)
    return dx, dg.reshape(d)


def _rope(x, cos_t, sin_t, *, out_dtype, name, sum_chunks=False, scale=None):
    s, w = x.shape
    assert w % LANES == 0
    reps = w // LANES
    ts = _rows(s, 512)
    ow = LANES if sum_chunks else w

    def body(x_ref, c_ref, s_ref, o_ref):
        xf = x_ref[...].astype(_F32)
        c, sn = c_ref[...], s_ref[...]
        if reps > 1:
            c, sn = jnp.tile(c, (1, reps)), jnp.tile(sn, (1, reps))
        lane = lax.broadcasted_iota(jnp.int32, xf.shape, 1)
        first = (lane & 31) < 16
        sw = jnp.where(first, pltpu.roll(xf, w - 16, 1), pltpu.roll(xf, 16, 1))
        y = xf * c + sw * sn
        if scale is not None:
            y = y * scale
        if sum_chunks:
            shift = w // 2
            while shift >= 32:
                y = y + pltpu.roll(y, shift, 1)
                shift //= 2
            y = y[:, :LANES]
        o_ref[...] = y.astype(out_dtype)

    return pl.pallas_call(
        body, name=name, grid=(s // ts,),
        in_specs=[pl.BlockSpec((ts, w), lambda i: (i, 0)), pl.BlockSpec((ts, LANES), lambda i: (i, 0)),
                  pl.BlockSpec((ts, LANES), lambda i: (i, 0))],
        out_specs=pl.BlockSpec((ts, ow), lambda i: (i, 0)),
        out_shape=jax.ShapeDtypeStruct((s, ow), out_dtype),
        compiler_params=_params(("parallel",)),
    )(x, cos_t, sin_t)


_NT = (((1,), (1,)), ((), ()))
_TN = (((0,), (0,)), ((), ()))
LOG2_E = 1.4426950408889634
LN_2 = 0.6931471805599453
ATTN_ROWS = 512
ATTN_CHAINS = 2
ATTN_FWD_KEYS = 2048
ATTN_BWD_KEYS = 1024


def _attn_fwd(q, k, v, *, out_dtype, name):
    h, s, dk = q.shape
    g, _, dv = v.shape
    r = h // g
    rc = min(ATTN_ROWS, s)
    hp = min(r, ATTN_CHAINS)
    nrc = max(1, min(ATTN_CHAINS // hp, s // rc))
    tq = rc * nrc
    tk = min(ATTN_FWD_KEYS, s)
    nhp, nq, nk = r // hp, s // tq, s // tk
    units = [(a, c) for a in range(hp) for c in range(nrc)]

    def body(q_ref, k_ref, v_ref, o_ref, lse_ref):
        qs = [q_ref[a, c * rc:(c + 1) * rc, :] for a, c in units]
        init = tuple((jnp.full((rc, 1), NEG_INF, _F32), jnp.zeros((rc, 1), _F32), jnp.zeros((rc, dv), _F32))
                     for _ in units)

        def trip(j, carry):
            rows = pl.ds(pl.multiple_of(j * tk, tk), tk)
            kb, vb = k_ref[0, rows, :], v_ref[0, rows, :]
            out = []
            for u in range(len(units)):
                m, l, acc = carry[u]
                sc = lax.dot_general(qs[u], kb, _NT, preferred_element_type=_F32)
                m_new = jnp.maximum(m, jnp.max(sc, axis=1, keepdims=True))
                p = jnp.exp2(sc - m_new)
                alpha = jnp.exp2(m - m_new)
                l = alpha * l + jnp.sum(p, axis=1, keepdims=True)
                acc = alpha * acc + jnp.dot(p.astype(_MXU), vb, preferred_element_type=_F32)
                out.append((m_new, l, acc))
            return tuple(out)

        fin = lax.fori_loop(0, nk, trip, init)
        for u, (a, c) in enumerate(units):
            m, l, acc = fin[u]
            o_ref[a, c * rc:(c + 1) * rc, :] = (acc / l).astype(out_dtype)
            lse_ref[a, c * rc:(c + 1) * rc, :] = m + jnp.log2(l)

    q_blk = lambda gg, hh, i: (gg * nhp + hh, i, 0)
    kv_blk = lambda gg, hh, i: (gg, 0, 0)
    return pl.pallas_call(
        body, name=name, grid=(g, nhp, nq),
        in_specs=[pl.BlockSpec((hp, tq, dk), q_blk), pl.BlockSpec((1, s, dk), kv_blk),
                  pl.BlockSpec((1, s, dv), kv_blk)],
        out_specs=[pl.BlockSpec((hp, tq, dv), q_blk), pl.BlockSpec((hp, tq, 1), q_blk)],
        out_shape=[jax.ShapeDtypeStruct((h, s, dv), out_dtype), jax.ShapeDtypeStruct((h, s, 1), _F32)],
        compiler_params=_params(("parallel", "parallel", "parallel")),
    )(q, k, v)


def _attn_bwd(q, k, v, do, lse2, delta, *, scale, name):
    h, s, dk = q.shape
    g, _, dv = v.shape
    r = h // g
    hp = min(r, ATTN_CHAINS)
    tq = min(ATTN_ROWS, s)
    tk = min(ATTN_BWD_KEYS * (ATTN_CHAINS // hp), s)
    nhp, nq, nk = r // hp, s // tq, s // tk

    def body(q_ref, k_ref, v_ref, do_ref, lse_ref, dl_ref, dq_ref, dk_ref, dv_ref):
        hh, i = pl.program_id(1), pl.program_id(2)

        @pl.when((hh == 0) & (i == 0))
        def _():
            dk_ref[...] = jnp.zeros_like(dk_ref)
            dv_ref[...] = jnp.zeros_like(dv_ref)

        qs = [q_ref[a] for a in range(hp)]
        dos = [do_ref[a] for a in range(hp)]
        ls = [lse_ref[a] for a in range(hp)]
        dls = [dl_ref[a] for a in range(hp)]

        def trip(j, carry):
            rows = pl.ds(pl.multiple_of(j * tk, tk), tk)
            kb, vb = k_ref[0, rows, :], v_ref[0, rows, :]
            out = []
            for a in range(hp):
                p = jnp.exp2(lax.dot_general(qs[a], kb, _NT, preferred_element_type=_F32) - ls[a])
                dp = lax.dot_general(dos[a], vb, _NT, preferred_element_type=_F32)
                ds = (p * (dp - dls[a])).astype(_MXU)
                dv_ref[0, rows, :] += lax.dot_general(p.astype(_MXU), dos[a], _TN, preferred_element_type=_F32)
                dk_ref[0, rows, :] += lax.dot_general(ds, qs[a], _TN, preferred_element_type=_F32)
                out.append(carry[a] + jnp.dot(ds, kb, preferred_element_type=_F32))
            return tuple(out)

        fin = lax.fori_loop(0, nk, trip, tuple(jnp.zeros((tq, dk), _F32) for _ in range(hp)))
        for a in range(hp):
            dq_ref[a] = fin[a] * scale

        @pl.when((hh == nhp - 1) & (i == nq - 1))
        def _():
            dk_ref[...] = dk_ref[...] * LN_2

    q_blk = lambda gg, hh, i: (gg * nhp + hh, i, 0)
    kv_blk = lambda gg, hh, i: (gg, 0, 0)
    return pl.pallas_call(
        body, name=name, grid=(g, nhp, nq),
        in_specs=[pl.BlockSpec((hp, tq, dk), q_blk), pl.BlockSpec((1, s, dk), kv_blk), pl.BlockSpec((1, s, dv), kv_blk),
                  pl.BlockSpec((hp, tq, dv), q_blk), pl.BlockSpec((hp, tq, 1), q_blk), pl.BlockSpec((hp, tq, 1), q_blk)],
        out_specs=[pl.BlockSpec((hp, tq, dk), q_blk), pl.BlockSpec((1, s, dk), kv_blk),
                   pl.BlockSpec((1, s, dv), kv_blk)],
        out_shape=[jax.ShapeDtypeStruct((h, s, dk), _F32), jax.ShapeDtypeStruct((g, s, dk), _F32),
                   jax.ShapeDtypeStruct((g, s, dv), _F32)],
        compiler_params=_params(("arbitrary", "arbitrary", "arbitrary")),
    )(q, k, v, do, lse2, delta)


WIN_ROWS = 512


def _win_geometry(s, dil):
    t = min(WIN_ROWS, s)
    length = s // dil
    lg = length.bit_length() - 1
    assert 1 << lg == length and t % DIL_HALF == 0 and s % t == 0
    return t, t + 2 * DIL_HALF, lg, t // DIL_HALF, s // DIL_HALF


def _win_specs(hn, t, d, halo_per_blk, n_halo):
    return [pl.BlockSpec((hn, DIL_HALF, d), lambda i: (0, jnp.maximum(i * halo_per_blk - 1, 0), 0)),
            pl.BlockSpec((hn, t, d), lambda i: (0, i, 0)),
            pl.BlockSpec((hn, DIL_HALF, d), lambda i: (0, jnp.minimum((i + 1) * halo_per_blk, n_halo - 1), 0))]


def _win_mask(i, t, w, lg, dil, wide_rows):
    shape = (w, t) if wide_rows else (t, w)
    rows = lax.broadcasted_iota(jnp.int32, shape, 0)
    cols = lax.broadcasted_iota(jnp.int32, shape, 1)
    base = i * t
    if wide_rows:
        pq, pk = base - DIL_HALF + rows, base + cols
    else:
        pq, pk = base + rows, base - DIL_HALF + cols
    arel = jnp.abs(pk - pq)
    valid = (arel <= DIL_HALF) & ((pk >> lg) == (pq >> lg))
    return valid, (-dil * arel).astype(_F32)


def _win3(lo_ref, mid_ref, hi_ref, a):
    return jnp.concatenate([lo_ref[a], mid_ref[a], hi_ref[a]], axis=0)


def _win_fwd(q, k, v, slopes, *, dil, name):
    hn, s, d = q.shape
    t, w, lg, hpb, n_halo = _win_geometry(s, dil)
    scale = d ** -0.5

    def body(sl_ref, q_ref, klo, kmid, khi, vlo, vmid, vhi, o_ref, lse_ref):
        valid, nb = _win_mask(pl.program_id(0), t, w, lg, dil, False)
        for a in range(hn):
            kw, vw = _win3(klo, kmid, khi, a), _win3(vlo, vmid, vhi, a)
            sc = lax.dot_general(q_ref[a], kw, _NT, preferred_element_type=_F32) * scale
            sc = jnp.where(valid, sc + sl_ref[a] * nb, NEG_INF)
            m = jnp.max(sc, axis=1, keepdims=True)
            e = jnp.exp(sc - m)
            den = jnp.sum(e, axis=1, keepdims=True)
            o_ref[a] = jnp.dot(e.astype(_MXU), vw, preferred_element_type=_F32) / den
            lse_ref[a] = m + jnp.log(den)

    kv = _win_specs(hn, t, d, hpb, n_halo)
    blk = lambda c: pl.BlockSpec((hn, t, c), lambda i: (0, i, 0))
    return pl.pallas_call(
        body, name=name, grid=(s // t,),
        in_specs=[pl.BlockSpec(memory_space=pltpu.SMEM), blk(d)] + kv + kv,
        out_specs=[blk(d), blk(1)],
        out_shape=[jax.ShapeDtypeStruct((hn, s, d), _F32), jax.ShapeDtypeStruct((hn, s, 1), _F32)],
        compiler_params=_params(("parallel",)),
    )(slopes.astype(_F32), q, k, k, k, v, v, v)


def _win_bwd_dq(q, k, v, do, lse, delta, slopes, *, dil, name):
    hn, s, d = q.shape
    t, w, lg, hpb, n_halo = _win_geometry(s, dil)
    scale = d ** -0.5

    def body(sl_ref, q_ref, klo, kmid, khi, vlo, vmid, vhi, do_ref, lse_ref, dl_ref, dq_ref):
        valid, nb = _win_mask(pl.program_id(0), t, w, lg, dil, False)
        for a in range(hn):
            kw, vw = _win3(klo, kmid, khi, a), _win3(vlo, vmid, vhi, a)
            sc = lax.dot_general(q_ref[a], kw, _NT, preferred_element_type=_F32) * scale
            p = jnp.exp(jnp.where(valid, sc + sl_ref[a] * nb, NEG_INF) - lse_ref[a])
            dp = lax.dot_general(do_ref[a], vw, _NT, preferred_element_type=_F32)
            ds = (p * (dp - dl_ref[a])).astype(_MXU)
            dq_ref[a] = jnp.dot(ds, kw, preferred_element_type=_F32) * scale

    kv = _win_specs(hn, t, d, hpb, n_halo)
    blk = lambda c: pl.BlockSpec((hn, t, c), lambda i: (0, i, 0))
    return pl.pallas_call(
        body, name=name, grid=(s // t,),
        in_specs=[pl.BlockSpec(memory_space=pltpu.SMEM), blk(d)] + kv + kv + [blk(d), blk(1), blk(1)],
        out_specs=blk(d),
        out_shape=jax.ShapeDtypeStruct((hn, s, d), _F32),
        compiler_params=_params(("parallel",)),
    )(slopes.astype(_F32), q, k, k, k, v, v, v, do, lse, delta)


def _win_bwd_dkv(q, k, v, do, lse, delta, slopes, *, dil, name):
    hn, s, d = q.shape
    t, w, lg, hpb, n_halo = _win_geometry(s, dil)
    scale = d ** -0.5

    def body(sl_ref, qlo, qmid, qhi, dolo, domid, dohi, llo, lmid, lhi, dllo, dlmid, dlhi, k_ref, v_ref,
             dk_ref, dv_ref):
        valid, nb = _win_mask(pl.program_id(0), t, w, lg, dil, True)
        for a in range(hn):
            qw, dow = _win3(qlo, qmid, qhi, a), _win3(dolo, domid, dohi, a)
            lw, dlw = _win3(llo, lmid, lhi, a), _win3(dllo, dlmid, dlhi, a)
            sc = lax.dot_general(qw, k_ref[a], _NT, preferred_element_type=_F32) * scale
            p = jnp.exp(jnp.where(valid, sc + sl_ref[a] * nb, NEG_INF) - lw)
            dp = lax.dot_general(dow, v_ref[a], _NT, preferred_element_type=_F32)
            ds = (p * (dp - dlw)).astype(_MXU)
            dv_ref[a] = lax.dot_general(p.astype(_MXU), dow, _TN, preferred_element_type=_F32)
            dk_ref[a] = lax.dot_general(ds, qw, _TN, preferred_element_type=_F32) * scale

    blk = lambda c: pl.BlockSpec((hn, t, c), lambda i: (0, i, 0))
    return pl.pallas_call(
        body, name=name, grid=(s // t,),
        in_specs=([pl.BlockSpec(memory_space=pltpu.SMEM)] + _win_specs(hn, t, d, hpb, n_halo)
                  + _win_specs(hn, t, d, hpb, n_halo) + _win_specs(hn, t, 1, hpb, n_halo)
                  + _win_specs(hn, t, 1, hpb, n_halo) + [blk(d), blk(d)]),
        out_specs=[blk(d), blk(d)],
        out_shape=[jax.ShapeDtypeStruct((hn, s, d), _F32), jax.ShapeDtypeStruct((hn, s, d), _F32)],
        compiler_params=_params(("parallel",)),
    )(slopes.astype(_F32), q, q, q, do, do, do, lse, lse, lse, delta, delta, delta, k, v)


def _attn_delta(o, do, *, name):
    h, s, d = o.shape
    t = _rows(s, 512)

    def body(o_ref, do_ref, d_ref):
        d_ref[0] = jnp.sum(o_ref[0].astype(_F32) * do_ref[0].astype(_F32), axis=-1, keepdims=True)

    blk = pl.BlockSpec((1, t, d), lambda hh, i: (hh, i, 0))
    return pl.pallas_call(
        body, name=name, grid=(h, s // t), in_specs=[blk, blk],
        out_specs=pl.BlockSpec((1, t, 1), lambda hh, i: (hh, i, 0)),
        out_shape=jax.ShapeDtypeStruct((h, s, 1), _F32),
        compiler_params=_params(("parallel", "parallel")),
    )(o, do)


def _merge_weights(lse):
    mx = jnp.max(lse, axis=0, keepdims=True)
    e = jnp.exp(lse - mx)
    return e / jnp.sum(e, axis=0, keepdims=True)


def _merge_fwd(o3, lse3, *, name):
    ng, sl, s, d = o3.shape
    t = _rows(s, 512)

    def body(o_ref, l_ref, c_ref):
        wts = _merge_weights(l_ref[:, 0])
        c_ref[0] = jnp.sum(wts * o_ref[:, 0], axis=0).astype(c_ref.dtype)

    return pl.pallas_call(
        body, name=name, grid=(sl, s // t),
        in_specs=[pl.BlockSpec((ng, 1, t, d), lambda a, i: (0, a, i, 0)),
                  pl.BlockSpec((ng, 1, t, 1), lambda a, i: (0, a, i, 0))],
        out_specs=pl.BlockSpec((1, t, d), lambda a, i: (a, i, 0)),
        out_shape=jax.ShapeDtypeStruct((sl, s, d), _ACT),
        compiler_params=_params(("parallel", "parallel")),
    )(o3, lse3)


def _merge_bwd(dcomb, o3, lse3, *, name):
    ng, sl, s, d = o3.shape
    t = _rows(s, 512)

    def body(dc_ref, o_ref, l_ref, do_ref, dl_ref):
        wts = _merge_weights(l_ref[:, 0])
        dc = dc_ref[0].astype(_F32)
        comb = jnp.sum(wts * o_ref[:, 0], axis=0)
        do_ref[:, 0] = (wts * dc[None]).astype(do_ref.dtype)
        dl_ref[:, 0] = wts * jnp.sum(dc * comb, axis=-1, keepdims=True)[None]

    big = pl.BlockSpec((ng, 1, t, d), lambda a, i: (0, a, i, 0))
    small = pl.BlockSpec((ng, 1, t, 1), lambda a, i: (0, a, i, 0))
    return pl.pallas_call(
        body, name=name, grid=(sl, s // t),
        in_specs=[pl.BlockSpec((1, t, d), lambda a, i: (a, i, 0)), big, small],
        out_specs=[big, small],
        out_shape=[jax.ShapeDtypeStruct((ng, sl, s, d), _ACT), jax.ShapeDtypeStruct((ng, sl, s, 1), _F32)],
        compiler_params=_params(("parallel", "parallel")),
    )(dcomb, o3, lse3)


def _swiglu(gate, up, *, name):
    m, n = gate.shape
    tm, tn = _rows(m, 512), _pick(n, 1408)

    def body(g_ref, u_ref, a_ref):
        gf = g_ref[...].astype(_F32)
        a_ref[...] = (gf * jax.nn.sigmoid(gf) * u_ref[...].astype(_F32)).astype(a_ref.dtype)

    blk = pl.BlockSpec((tm, tn), lambda i, j: (i, j))
    return pl.pallas_call(
        body, name=name, grid=(m // tm, n // tn), in_specs=[blk, blk], out_specs=blk,
        out_shape=jax.ShapeDtypeStruct((m, n), _ACT),
        compiler_params=_params(("parallel", "parallel")),
    )(gate, up)


def _swiglu_bwd(gate, up, da, *, name):
    m, n = gate.shape
    tm, tn = _rows(m, 512), _pick(n, 1408)

    def body(g_ref, u_ref, da_ref, dg_ref, du_ref):
        gf = g_ref[...].astype(_F32)
        uf = u_ref[...].astype(_F32)
        daf = da_ref[...].astype(_F32)
        sg = jax.nn.sigmoid(gf)
        dg_ref[...] = (daf * uf * (sg + gf * sg * (1.0 - sg))).astype(dg_ref.dtype)
        du_ref[...] = (daf * (gf * sg)).astype(du_ref.dtype)

    blk = pl.BlockSpec((tm, tn), lambda i, j: (i, j))
    return pl.pallas_call(
        body, name=name, grid=(m // tm, n // tn), in_specs=[blk, blk, blk], out_specs=[blk, blk],
        out_shape=[jax.ShapeDtypeStruct((m, n), _ACT), jax.ShapeDtypeStruct((m, n), _ACT)],
        compiler_params=_params(("parallel", "parallel")),
    )(gate, up, da)


def _final_loss(x, g, target, *, name):
    m, d = x.shape
    tm = _rows(m, 512)

    def body(x_ref, g_ref, t_ref, loss_ref, dx_ref, dg_ref):
        @pl.when(pl.program_id(0) == 0)
        def _():
            loss_ref[...] = jnp.zeros_like(loss_ref)
            dg_ref[...] = jnp.zeros_like(dg_ref)

        xf = x_ref[...]
        r = lax.rsqrt(jnp.mean(xf * xf, axis=-1, keepdims=True) + NORM_EPS)
        xh = xf * r
        err = xh * g_ref[...] - t_ref[...]
        loss_ref[...] += 0.5 * jnp.sum(jnp.mean(err * err, axis=-1, keepdims=True))
        dy = err * (1.0 / d)
        dg_ref[...] += jnp.sum(dy * xh, axis=0, keepdims=True)
        gdy = dy * g_ref[...]
        dx_ref[...] = r * (gdy - xh * jnp.mean(gdy * xh, axis=-1, keepdims=True))

    row = pl.BlockSpec((tm, d), lambda i: (i, 0))
    vec = pl.BlockSpec((1, d), lambda i: (0, 0))
    loss, dx, dg = pl.pallas_call(
        body, name=name, grid=(m // tm,),
        in_specs=[row, vec, row],
        out_specs=[pl.BlockSpec((8, LANES), lambda i: (0, 0)), row, vec],
        out_shape=[jax.ShapeDtypeStruct((8, LANES), _F32), jax.ShapeDtypeStruct((m, d), _F32),
                   jax.ShapeDtypeStruct((1, d), _F32)],
        compiler_params=_params(("arbitrary",)),
    )(x, g.reshape(1, d), target)
    return loss[0, 0], dx, dg.reshape(d)


def _reduce_adamw(parts, w, m, v, *, name):
    rws, cols = w.shape
    tr = rws
    for cand in range(min(rws, 256), 0, -SUBLANES_16BIT):
        if cand % SUBLANES_16BIT == 0 and rws % cand == 0:
            tr = cand
            break

    def body(p_ref, w_ref, m_ref, v_ref, g_ref, d_ref, nm_ref, nv_ref):
        gsum = p_ref[0].astype(_F32)
        for dev in range(1, N_DEV):
            gsum = gsum + p_ref[dev].astype(_F32)
        m2 = ADAM_B1 * m_ref[...] + (1.0 - ADAM_B1) * gsum
        v2 = ADAM_B2 * v_ref[...] + (1.0 - ADAM_B2) * (gsum * gsum)
        m_hat = m2 / (1.0 - ADAM_B1 ** ADAM_STEP)
        v_hat = v2 / (1.0 - ADAM_B2 ** ADAM_STEP)
        g_ref[...] = gsum
        d_ref[...] = -ADAM_LR * (m_hat / (jnp.sqrt(v_hat) + ADAM_EPS) + ADAM_WD * w_ref[...])
        nm_ref[...] = m2
        nv_ref[...] = v2

    blk = pl.BlockSpec((tr, cols), lambda i: (i, 0))
    out = jax.ShapeDtypeStruct((rws, cols), _F32)
    return pl.pallas_call(
        body, name=name, grid=(rws // tr,),
        in_specs=[pl.BlockSpec((N_DEV, tr, cols), lambda i: (0, i, 0)), blk, blk, blk],
        out_specs=[blk, blk, blk, blk], out_shape=[out, out, out, out],
        compiler_params=_params(("parallel",)),
    )(parts, w, m, v)


def _mesh_pos():
    return lax.axis_index("x"), lax.axis_index("y"), lax.axis_index("c")


def _all_gather(block):
    rws, cols = block.shape

    def body(x_ref, out_ref, send_sems, recv_sems, local_sem):
        x, y, c = _mesh_pos()
        me, sibling = (x, y, c), (x, y, 1 - c)
        chips = [(1 - x, y), (x, 1 - y), (1 - x, 1 - y)]

        def slot(px, py, pc):
            return out_ref.at[4 * px + 2 * py + pc]

        def copy(k, blk, to, src=None):
            return pltpu.make_async_remote_copy(
                src_ref=slot(*blk) if src is None else src, dst_ref=slot(*blk),
                send_sem=send_sems.at[k], recv_sem=recv_sems.at[k],
                device_id=to, device_id_type=pl.DeviceIdType.MESH)

        mine = pltpu.make_async_copy(x_ref, slot(*me), local_sem)
        mine.start()
        first = [copy(0, me, sibling, src=x_ref)]
        first += [copy(1 + j, me, (*chip, c), src=x_ref) for j, chip in enumerate(chips)]
        for cp in first:
            cp.start()
        passed = [copy(4 + j, (*chip, c), sibling) for j, chip in enumerate(chips)]
        for j, chip in enumerate(chips):
            copy(1 + j, (*chip, c), me).wait_recv()
            passed[j].start()
        copy(0, sibling, me).wait_recv()
        for j, chip in enumerate(chips):
            copy(4 + j, (*chip, 1 - c), me).wait_recv()
        for cp in first + passed:
            cp.wait_send()
        mine.wait()

    return pl.pallas_call(
        body, name="weights_all_gather",
        out_shape=jax.ShapeDtypeStruct((N_DEV, rws, cols), block.dtype),
        in_specs=[pl.BlockSpec(memory_space=pl.ANY)],
        out_specs=pl.BlockSpec(memory_space=pl.ANY),
        scratch_shapes=[pltpu.SemaphoreType.DMA((7,)), pltpu.SemaphoreType.DMA((7,)), pltpu.SemaphoreType.DMA],
    )(block)


def _exchange(parts):
    _, rws, cols = parts.shape

    def body(g_ref, out_ref, send_sems, recv_sems, local_sem):
        x, y, c = _mesh_pos()
        me = 4 * x + 2 * y + c
        mine = pltpu.make_async_copy(g_ref.at[me], out_ref.at[me], local_sem)
        mine.start()
        copies = []
        for k in range(1, N_DEV):
            px = 1 - x if k & 4 else x
            py = 1 - y if k & 2 else y
            pc = 1 - c if k & 1 else c
            peer = 4 * px + 2 * py + pc
            send = pltpu.make_async_remote_copy(
                src_ref=g_ref.at[peer], dst_ref=out_ref.at[me],
                send_sem=send_sems.at[k - 1], recv_sem=recv_sems.at[k - 1],
                device_id=(px, py, pc), device_id_type=pl.DeviceIdType.MESH)
            send.start()
            arrival = pltpu.make_async_remote_copy(
                src_ref=g_ref.at[peer], dst_ref=out_ref.at[peer],
                send_sem=send_sems.at[k - 1], recv_sem=recv_sems.at[k - 1],
                device_id=(px, py, pc), device_id_type=pl.DeviceIdType.MESH)
            copies.append((send, arrival))
        for _, arrival in copies:
            arrival.wait_recv()
        for send, _ in copies:
            send.wait_send()
        mine.wait()

    return pl.pallas_call(
        body, name="grads_exchange",
        out_shape=jax.ShapeDtypeStruct(parts.shape, parts.dtype),
        in_specs=[pl.BlockSpec(memory_space=pl.ANY)],
        out_specs=pl.BlockSpec(memory_space=pl.ANY),
        scratch_shapes=[pltpu.SemaphoreType.DMA((7,)), pltpu.SemaphoreType.DMA((7,)), pltpu.SemaphoreType.DMA],
    )(parts)


_BIG = (("w_in_ab", 2), ("mla_w_uq", 1), ("mla_w_ukv", 1), ("w_out_ab", 2), ("gqa_w_q", 1), ("gqa_w_kv", 1),
        ("gqa_w_o", 1), ("ffn_w_in", 2), ("ffn_w_out", 1))
_SMALL = ("mix_norm_ab", "ffn_norm", "final_norm", "mix_norm_c", "mla_q_norm", "mla_kv_norm", "gqa_q_norm", "gqa_k_norm")
SMALL_ROWS = 16
GATHER_TAIL_ROWS = 16
EXCHANGE_TAIL_ROWS = 32


def _view3(a):
    return a.reshape(a.shape[0], a.shape[1], -1)


def _pad_rows(a2d):
    pad = -a2d.shape[0] % SUBLANES_16BIT
    return jnp.pad(a2d, ((0, pad), (0, 0))) if pad else a2d


def _pack_rows(shard):
    return _pad_rows(shard.reshape(-1, PACK_COLS))


def _packed_rows(shape):
    n = 1
    for d in shape:
        n *= d
    rows = n // PACK_COLS
    return rows + (-rows % SUBLANES_16BIT)


def _pack_big(shards):
    return jnp.concatenate([_pack_rows(shards[n]) for n, _ in _BIG], axis=0)


def _unpack_big(packed, shapes):
    out, off = {}, 0
    for n, _ in _BIG:
        size = 1
        for d in shapes[n]:
            size *= d
        out[n] = packed[off:off + size // PACK_COLS].reshape(shapes[n])
        off += _packed_rows(shapes[n])
    return out


def _unpack_gathered(gathered, shapes):
    out, off = {}, 0
    for n, axis in _BIG:
        size = 1
        for d in shapes[n]:
            size *= d
        l3 = (shapes[n][0], shapes[n][1], size // (shapes[n][0] * shapes[n][1]))
        sh = gathered[:, off:off + size // PACK_COLS].reshape((N_DEV,) + l3)
        if axis == 1:
            full = sh.transpose(1, 0, 2, 3).reshape(l3[0], N_DEV * l3[1], l3[2])
        else:
            full = sh.transpose(1, 2, 0, 3).reshape(l3[0], l3[1], N_DEV * l3[2])
        out[n] = full
        off += _packed_rows(shapes[n])
    return out


def _split_for_devices(full, axis):
    l, rws, cols = full.shape
    if axis == 1:
        sh = full.reshape(l, N_DEV, rws // N_DEV, cols).transpose(1, 0, 2, 3)
    else:
        sh = full.reshape(l, rws, N_DEV, cols // N_DEV).transpose(2, 0, 1, 3)
    flat = sh.reshape(N_DEV, -1, PACK_COLS)
    pad = -flat.shape[1] % SUBLANES_16BIT
    return jnp.pad(flat, ((0, 0), (0, pad), (0, 0))) if pad else flat


def _to_bits16(a_f32_rows):
    r = a_f32_rows.shape[0]
    return lax.bitcast_convert_type(a_f32_rows, jnp.bfloat16).reshape(2 * r, PACK_COLS)


def _from_bits16(a_bf16_rows):
    lead, r = a_bf16_rows.shape[:-2], a_bf16_rows.shape[-2]
    return lax.bitcast_convert_type(a_bf16_rows.reshape(lead + (r // 2, PACK_COLS, 2)), _F32)


def _small_sizes():
    return {"mix_norm_ab": 2 * D_MODEL, "ffn_norm": DEPTH * D_MODEL, "final_norm": D_MODEL, "mix_norm_c": 2 * D_MODEL,
            "mla_q_norm": 2 * MLA_Q_RANK, "mla_kv_norm": 2 * MLA_KV_RANK, "gqa_q_norm": 2 * GQA_HEAD_DIM,
            "gqa_k_norm": 2 * GQA_HEAD_DIM}


def _pack_small(vals):
    flat = jnp.concatenate([vals[n].reshape(-1).astype(_F32) for n in _SMALL])
    return jnp.pad(flat, (0, SMALL_ROWS * PACK_COLS - flat.shape[0])).reshape(SMALL_ROWS, PACK_COLS)


def _unpack_small(pack):
    flat, out, off = pack.reshape(-1), {}, 0
    sizes = _small_sizes()
    for n in _SMALL:
        out[n] = flat[off:off + sizes[n]]
        off += sizes[n]
    return out


def _angles(pos, dim):
    freqs = ROPE_THETA ** (-jnp.arange(0, dim, 2, dtype=_F32) / dim)
    ang = pos.astype(_F32)[:, None] * freqs[None, :]
    return jnp.cos(ang), jnp.sin(ang)


def _rope_tables(s):
    pos = jnp.arange(s)
    cos_t, sin_t = _angles(pos, MLA_ROPE)
    mla_c = jnp.tile(jnp.concatenate([cos_t, cos_t], -1), (1, LANES // 32))
    mla_s = jnp.tile(jnp.concatenate([-sin_t, sin_t], -1), (1, LANES // 32))
    rows = s // GRID_W
    row_idx = jnp.broadcast_to(jnp.arange(rows)[:, None], (rows, GRID_W)).reshape(-1)
    col_idx = jnp.broadcast_to(jnp.arange(GRID_W)[None, :], (rows, GRID_W)).reshape(-1)
    cos_r, sin_r = _angles(row_idx, GQA_HEAD_DIM // 2)
    cos_c, sin_c = _angles(col_idx, GQA_HEAD_DIM // 2)
    gqa_c = jnp.tile(jnp.concatenate([cos_r, cos_r, cos_c, cos_c], -1), (1, LANES // GQA_HEAD_DIM))
    gqa_s = jnp.tile(jnp.concatenate([-sin_r, sin_r, -sin_c, sin_c], -1), (1, LANES // GQA_HEAD_DIM))
    return (mla_c, mla_s), (gqa_c, gqa_s)


def _heads(x2d, h):
    s = x2d.shape[0]
    return x2d.reshape(s, h, -1).transpose(1, 0, 2)


def _unheads(xh):
    h, s, d = xh.shape
    return xh.transpose(1, 0, 2).reshape(s, h * d)


def _to_res(x2d, dil):
    s = x2d.shape[0]
    return x2d.reshape(s // dil, dil, DIL_SLOTS, -1).transpose(2, 1, 0, 3).reshape(DIL_SLOTS, s, -1)


def _from_res(xh, dil):
    sl, s, d = xh.shape
    return xh.reshape(sl, dil, s // dil, d).transpose(2, 1, 0, 3).reshape(s, sl * d)


def _res_to_tok(xh, dil):
    sl, s, c = xh.shape
    return xh.reshape(sl, dil, s // dil, c).transpose(0, 2, 1, 3).reshape(sl, s, c)


def _tok_to_res(xh, dil):
    sl, s, c = xh.shape
    return xh.reshape(sl, s // dil, dil, c).transpose(0, 2, 1, 3).reshape(sl, s, c)


def _ffn_fwd(x, w, tag):
    hn = _rmsnorm(x, w["norm"], out_dtype=_ACT, name=f"{tag}_norm")
    gate = _matmul(hn, w["w_gate"], out_dtype=_ACT, name=f"{tag}_gate")
    up = _matmul(hn, w["w_up"], out_dtype=_ACT, name=f"{tag}_up")
    act = _swiglu(gate, up, name=f"{tag}_swiglu")
    out = _matmul(act, w["w_out"], res=x, name=f"{tag}_out")
    return out, (x, hn, gate, up, act)


def _ffn_bwd(dout, saved, w, tag):
    x, hn, gate, up, act = saved
    dact = _matmul(dout, w["w_out"], trans_b=True, out_dtype=_ACT, name=f"{tag}_dact")
    d_w_out = _matmul(act, dout, trans_a=True, name=f"{tag}_dwout")
    dgate, dup = _swiglu_bwd(gate, up, dact, name=f"{tag}_dswiglu")
    d_w_gate = _matmul(hn, dgate, trans_a=True, name=f"{tag}_dwgate")
    d_w_up = _matmul(hn, dup, trans_a=True, name=f"{tag}_dwup")
    dhn = _matmul(dgate, w["w_gate"], trans_b=True, name=f"{tag}_dhn_gate")
    dhn = _matmul(dup, w["w_up"], trans_b=True, res=dhn, name=f"{tag}_dhn_up")
    dx, dnorm = _rmsnorm_bwd(x, w["norm"], dhn, dout, name=f"{tag}_dnorm")
    return dx, {"norm": dnorm, "w_gate": d_w_gate, "w_up": d_w_up, "w_out": d_w_out}


def _even_fwd(x, w, tabs, slopes, tag):
    s = x.shape[0]
    (mla_c, mla_s), _ = tabs
    hn = _rmsnorm(x, w["norm"], out_dtype=_ACT, name=f"{tag}_norm")
    za = _matmul(hn, w["w_a"], name=f"{tag}_in_a")
    zb = _matmul(hn, w["w_b"], out_dtype=_ACT, name=f"{tag}_in_b")
    cq, ckv, kr = za[:, :MLA_Q_RANK], za[:, MLA_Q_RANK:MLA_Q_RANK + MLA_KV_RANK], za[:, MLA_Q_RANK + MLA_KV_RANK:]
    cqn = _rmsnorm(cq, w["q_norm"], out_dtype=_ACT, name=f"{tag}_qnorm")
    ckvn = _rmsnorm(ckv, w["kv_norm"], out_dtype=_ACT, name=f"{tag}_kvnorm")
    q = _matmul(cqn, w["w_uq"], scale=(MLA_NOPE + MLA_ROPE) ** -0.5 * LOG2_E,
                name=f"{tag}_uq")
    kv = _matmul(ckvn, w["w_ukv"], out_dtype=_ACT, name=f"{tag}_ukv")
    nn = MLA_HEADS * MLA_NOPE
    q_rope = _rope(q[:, nn:], mla_c, mla_s, out_dtype=_ACT, name=f"{tag}_rope_q")
    k_rope = _rope(kr, mla_c, mla_s, out_dtype=_ACT, name=f"{tag}_rope_k")[:, :MLA_ROPE]
    qh = jnp.concatenate([_heads(q[:, :nn].astype(_ACT), MLA_HEADS), _heads(q_rope, MLA_HEADS)], axis=-1)
    kh = jnp.concatenate([_heads(kv[:, :nn], MLA_HEADS),
                          jnp.broadcast_to(k_rope[None], (MLA_HEADS, s, MLA_ROPE))], axis=-1)
    vh = _heads(kv[:, nn:], MLA_HEADS)
    oa, lse_a = _attn_fwd(qh, kh, vh, out_dtype=_ACT, name=f"{tag}_mla")

    nd, ng = DIL_HEADS * DIL_HEAD_DIM, DIL_SLOTS * DIL_HEAD_DIM
    outs, lses, dil_saved = [], [], []
    for gi, (_, dil) in enumerate(DIL_PAIRS):
        hs = slice(gi * DIL_SLOTS, (gi + 1) * DIL_SLOTS)
        qr, kr_, vr = (_to_res(zb[:, a * nd + gi * ng:a * nd + (gi + 1) * ng], dil) for a in range(3))
        o, l = _win_fwd(qr, kr_, vr, slopes[hs], dil=dil, name=f"{tag}_dil{gi}")
        dil_saved.append((qr, kr_, vr, l))
        outs.append(_res_to_tok(o, dil))
        lses.append(_res_to_tok(l, dil))
    o3, lse3 = jnp.stack(outs), jnp.stack(lses)
    comb = _merge_fwd(o3, lse3, name=f"{tag}_merge")
    cat = jnp.concatenate([_unheads(oa), _unheads(comb)], axis=-1)
    out = _matmul(cat, w["w_out"], res=x, name=f"{tag}_out")
    return out, (x, hn, cq, ckv, cqn, ckvn, qh, kh, vh, oa, lse_a, dil_saved, o3, lse3, cat)


def _even_bwd(dout, saved, w, tabs, slopes, tag):
    x, hn, cq, ckv, cqn, ckvn, qh, kh, vh, oa, lse_a, dil_saved, o3, lse3, cat = saved
    (mla_c, mla_s), _ = tabs
    nn = MLA_HEADS * MLA_NOPE
    dcat = _matmul(dout, w["w_out"], trans_b=True, out_dtype=_ACT, name=f"{tag}_dcat")
    d_w_out = _matmul(cat, dout, trans_a=True, name=f"{tag}_dwout")
    nv = MLA_HEADS * MLA_V

    doa = _heads(dcat[:, :nv], MLA_HEADS)
    delta = _attn_delta(oa, doa, name=f"{tag}_mla_delta")
    dqh, dkh, dvh = _attn_bwd(qh, kh, vh, doa, lse_a, delta, scale=(MLA_NOPE + MLA_ROPE) ** -0.5,
                              name=f"{tag}_mla_bwd")
    dq_rope = _rope(_unheads(dqh[..., MLA_NOPE:]), mla_c, -mla_s, out_dtype=_ACT, name=f"{tag}_drope_q")
    dq = jnp.concatenate([_unheads(dqh[..., :MLA_NOPE]).astype(_ACT), dq_rope], axis=-1)
    dkr = _rope(_unheads(dkh[..., MLA_NOPE:]), mla_c, -mla_s, out_dtype=_F32, sum_chunks=True,
                name=f"{tag}_drope_k")
    dkv = jnp.concatenate([_unheads(dkh[..., :MLA_NOPE]), _unheads(dvh)], axis=-1).astype(_ACT)
    d_w_uq = _matmul(cqn, dq, trans_a=True, name=f"{tag}_dwuq")
    d_w_ukv = _matmul(ckvn, dkv, trans_a=True, name=f"{tag}_dwukv")
    dcqn = _matmul(dq, w["w_uq"], trans_b=True, name=f"{tag}_dcqn")
    dckvn = _matmul(dkv, w["w_ukv"], trans_b=True, name=f"{tag}_dckvn")
    dcq, d_q_norm = _rmsnorm_bwd(cq, w["q_norm"], dcqn, out_dtype=_ACT, name=f"{tag}_dqnorm")
    dckv, d_kv_norm = _rmsnorm_bwd(ckv, w["kv_norm"], dckvn, out_dtype=_ACT, name=f"{tag}_dkvnorm")
    lane = jnp.arange(LANES) < MLA_ROPE
    dza = jnp.concatenate([dcq, dckv, jnp.where(lane[None], dkr, 0.0).astype(_ACT)], axis=-1)

    dcomb = _heads(dcat[:, nv:], DIL_SLOTS)
    do3, delta3 = _merge_bwd(dcomb, o3, lse3, name=f"{tag}_dmerge")
    dqs, dks, dvs = [], [], []
    for gi, (_, dil) in enumerate(DIL_PAIRS):
        hs = slice(gi * DIL_SLOTS, (gi + 1) * DIL_SLOTS)
        qr, kr_, vr, l = dil_saved[gi]
        grads = (qr, kr_, vr, _tok_to_res(do3[gi], dil), l, _tok_to_res(delta3[gi], dil), slopes[hs])
        dqs.append(_from_res(_win_bwd_dq(*grads, dil=dil, name=f"{tag}_dil{gi}_dq"), dil))
        b, c = _win_bwd_dkv(*grads, dil=dil, name=f"{tag}_dil{gi}_dkv")
        dks.append(_from_res(b, dil))
        dvs.append(_from_res(c, dil))
    dzb = jnp.concatenate(dqs + dks + dvs, axis=-1).astype(_ACT)

    d_w_a = _matmul(hn, dza, trans_a=True, name=f"{tag}_dwa")
    d_w_b = _matmul(hn, dzb, trans_a=True, name=f"{tag}_dwb")
    dhn = _matmul(dza, w["w_a"], trans_b=True, name=f"{tag}_dhn_a")
    dhn = _matmul(dzb, w["w_b"], trans_b=True, res=dhn, name=f"{tag}_dhn_b")
    dx, dnorm = _rmsnorm_bwd(x, w["norm"], dhn, dout, name=f"{tag}_dnorm")
    return dx, {"norm": dnorm, "w_a": d_w_a, "w_b": d_w_b, "q_norm": d_q_norm, "kv_norm": d_kv_norm,
                "w_uq": d_w_uq, "w_ukv": d_w_ukv, "w_out": d_w_out}


def _odd_fwd(x, w, tabs, tag):
    s = x.shape[0]
    _, (gqa_c, gqa_s) = tabs
    nk = GQA_KV_HEADS * GQA_HEAD_DIM
    hn = _rmsnorm(x, w["norm"], out_dtype=_ACT, name=f"{tag}_norm")
    q = _matmul(hn, w["w_q"], name=f"{tag}_q")
    kv = _matmul(hn, w["w_kv"], name=f"{tag}_kv")
    k = kv[:, :nk]
    qn = _rmsnorm(q.reshape(s * GQA_HEADS, GQA_HEAD_DIM), w["q_norm"], out_dtype=_F32, name=f"{tag}_qnorm",
                  rows=4096).reshape(s, -1)
    kn = _rmsnorm(k.reshape(s * GQA_KV_HEADS, GQA_HEAD_DIM), w["k_norm"], out_dtype=_F32, name=f"{tag}_knorm",
                  rows=4096).reshape(s, -1)
    qh = _heads(_rope(qn, gqa_c, gqa_s, out_dtype=_ACT, scale=GQA_HEAD_DIM ** -0.5 * LOG2_E, name=f"{tag}_rope_q"),
                GQA_HEADS)
    kh = _heads(_rope(kn, gqa_c, gqa_s, out_dtype=_ACT, name=f"{tag}_rope_k"), GQA_KV_HEADS)
    vh = _heads(kv[:, nk:].astype(_ACT), GQA_KV_HEADS)
    o, lse = _attn_fwd(qh, kh, vh, out_dtype=_ACT, name=f"{tag}_gqa")
    ocat = _unheads(o)
    out = _matmul(ocat, w["w_o"], res=x, name=f"{tag}_out")
    return out, (x, hn, q, k, qh, kh, vh, o, lse, ocat)


def _odd_bwd(dout, saved, w, tabs, tag):
    x, hn, q, k, qh, kh, vh, o, lse, ocat = saved
    s = x.shape[0]
    _, (gqa_c, gqa_s) = tabs
    docat = _matmul(dout, w["w_o"], trans_b=True, out_dtype=_ACT, name=f"{tag}_docat")
    d_w_o = _matmul(ocat, dout, trans_a=True, name=f"{tag}_dwo")
    doh = _heads(docat, GQA_HEADS)
    delta = _attn_delta(o, doh, name=f"{tag}_gqa_delta")
    dqh, dkh, dvh = _attn_bwd(qh, kh, vh, doh, lse, delta, scale=GQA_HEAD_DIM ** -0.5, name=f"{tag}_gqa_bwd")
    dqn = _rope(_unheads(dqh), gqa_c, -gqa_s, out_dtype=_F32, name=f"{tag}_drope_q")
    dkn = _rope(_unheads(dkh), gqa_c, -gqa_s, out_dtype=_F32, name=f"{tag}_drope_k")
    dq, d_q_norm = _rmsnorm_bwd(q.reshape(s * GQA_HEADS, GQA_HEAD_DIM), w["q_norm"],
                                dqn.reshape(s * GQA_HEADS, GQA_HEAD_DIM), out_dtype=_ACT, name=f"{tag}_dqnorm", rows=4096)
    dk, d_k_norm = _rmsnorm_bwd(k.reshape(s * GQA_KV_HEADS, GQA_HEAD_DIM), w["k_norm"],
                                dkn.reshape(s * GQA_KV_HEADS, GQA_HEAD_DIM), out_dtype=_ACT, name=f"{tag}_dknorm",
                                rows=4096)
    dq = dq.reshape(s, -1)
    dkv = jnp.concatenate([dk.reshape(s, -1), _unheads(dvh).astype(_ACT)], axis=-1)
    d_w_q = _matmul(hn, dq, trans_a=True, name=f"{tag}_dwq")
    d_w_kv = _matmul(hn, dkv, trans_a=True, name=f"{tag}_dwkv")
    dhn = _matmul(dq, w["w_q"], trans_b=True, name=f"{tag}_dhn_q")
    dhn = _matmul(dkv, w["w_kv"], trans_b=True, res=dhn, name=f"{tag}_dhn_kv")
    dx, dnorm = _rmsnorm_bwd(x, w["norm"], dhn, dout, name=f"{tag}_dnorm")
    return dx, {"norm": dnorm, "w_q": d_w_q, "w_kv": d_w_kv, "q_norm": d_q_norm, "k_norm": d_k_norm, "w_o": d_w_o}


def _split_heads_cols(wm, heads, first):
    rws = wm.shape[0]
    w3 = wm.reshape(rws, heads, -1)
    return jnp.concatenate([w3[:, :, :first].reshape(rws, -1), w3[:, :, first:].reshape(rws, -1)], axis=-1)


def _merge_heads_cols(wm, heads, first):
    rws, cols = wm.shape
    a = wm[:, :heads * first].reshape(rws, heads, first)
    b = wm[:, heads * first:].reshape(rws, heads, cols // heads - first)
    return jnp.concatenate([a, b], axis=-1).reshape(rws, cols)


def _layer_weights(full, gains, layer):
    i = layer // 2
    ffn = {"norm": gains["ffn_norm"][layer], "w_gate": full["ffn_w_in"][layer][:, :FFN_HIDDEN],
           "w_up": full["ffn_w_in"][layer][:, FFN_HIDDEN:], "w_out": full["ffn_w_out"][layer]}
    if layer % 2 == 0:
        w_in = full["w_in_ab"][i]
        mix = {"norm": gains["mix_norm_ab"][i],
               "w_a": jnp.pad(w_in[:, :IN_A], ((0, 0), (0, IN_A_PAD - IN_A))), "w_b": w_in[:, IN_A:],
               "q_norm": gains["mla_q_norm"][i], "kv_norm": gains["mla_kv_norm"][i],
               "w_uq": _split_heads_cols(full["mla_w_uq"][i], MLA_HEADS, MLA_NOPE),
               "w_ukv": _split_heads_cols(full["mla_w_ukv"][i], MLA_HEADS, MLA_NOPE),
               "w_out": full["w_out_ab"][i]}
    else:
        mix = {"norm": gains["mix_norm_c"][i], "w_q": full["gqa_w_q"][i], "w_kv": full["gqa_w_kv"][i],
               "q_norm": gains["gqa_q_norm"][i], "k_norm": gains["gqa_k_norm"][i], "w_o": full["gqa_w_o"][i]}
    return mix, ffn


def kernel(x, mix_norm_ab, w_in_ab, mla_q_norm, mla_kv_norm, mla_w_uq, mla_w_ukv, w_out_ab, mix_norm_c, gqa_w_q, gqa_w_kv, gqa_q_norm, gqa_k_norm, gqa_w_o, ffn_norm, ffn_w_in, ffn_w_out, final_norm, loss_target, m_mix_norm_ab, m_w_in_ab, m_mla_q_norm, m_mla_kv_norm, m_mla_w_uq, m_mla_w_ukv, m_w_out_ab, m_mix_norm_c, m_gqa_w_q, m_gqa_w_kv, m_gqa_q_norm, m_gqa_k_norm, m_gqa_w_o, m_ffn_norm, m_ffn_w_in, m_ffn_w_out, m_final_norm, v_mix_norm_ab, v_w_in_ab, v_mla_q_norm, v_mla_kv_norm, v_mla_w_uq, v_mla_w_ukv, v_w_out_ab, v_mix_norm_c, v_gqa_w_q, v_gqa_w_kv, v_gqa_q_norm, v_gqa_k_norm, v_gqa_w_o, v_ffn_norm, v_ffn_w_in, v_ffn_w_out, v_final_norm):
    wts = dict(mix_norm_ab=mix_norm_ab, w_in_ab=w_in_ab, mla_q_norm=mla_q_norm, mla_kv_norm=mla_kv_norm,
               mla_w_uq=mla_w_uq, mla_w_ukv=mla_w_ukv, w_out_ab=w_out_ab, mix_norm_c=mix_norm_c, gqa_w_q=gqa_w_q,
               gqa_w_kv=gqa_w_kv, gqa_q_norm=gqa_q_norm, gqa_k_norm=gqa_k_norm, gqa_w_o=gqa_w_o, ffn_norm=ffn_norm,
               ffn_w_in=ffn_w_in, ffn_w_out=ffn_w_out, final_norm=final_norm)
    mom = dict(mix_norm_ab=m_mix_norm_ab, w_in_ab=m_w_in_ab, mla_q_norm=m_mla_q_norm, mla_kv_norm=m_mla_kv_norm,
               mla_w_uq=m_mla_w_uq, mla_w_ukv=m_mla_w_ukv, w_out_ab=m_w_out_ab, mix_norm_c=m_mix_norm_c,
               gqa_w_q=m_gqa_w_q, gqa_w_kv=m_gqa_w_kv, gqa_q_norm=m_gqa_q_norm, gqa_k_norm=m_gqa_k_norm,
               gqa_w_o=m_gqa_w_o, ffn_norm=m_ffn_norm, ffn_w_in=m_ffn_w_in, ffn_w_out=m_ffn_w_out,
               final_norm=m_final_norm)
    var = dict(mix_norm_ab=v_mix_norm_ab, w_in_ab=v_w_in_ab, mla_q_norm=v_mla_q_norm, mla_kv_norm=v_mla_kv_norm,
               mla_w_uq=v_mla_w_uq, mla_w_ukv=v_mla_w_ukv, w_out_ab=v_w_out_ab, mix_norm_c=v_mix_norm_c,
               gqa_w_q=v_gqa_w_q, gqa_w_kv=v_gqa_w_kv, gqa_q_norm=v_gqa_q_norm, gqa_k_norm=v_gqa_k_norm,
               gqa_w_o=v_gqa_w_o, ffn_norm=v_ffn_norm, ffn_w_in=v_ffn_w_in, ffn_w_out=v_ffn_w_out,
               final_norm=v_final_norm)
    big_names = [n for n, _ in _BIG]
    shard_shapes = {n: wts[n].shape for n in big_names}
    xs = x[0]
    s = xs.shape[0]
    me = 4 * lax.axis_index("x") + 2 * lax.axis_index("y") + lax.axis_index("c")
    c_cols = mix_norm_c.shape[1]

    w_pack = _pack_big({n: wts[n] for n in big_names})
    n_rows = w_pack.shape[0]
    tail = jnp.pad(mix_norm_c.reshape(-1), (0, GATHER_TAIL_ROWS // 2 * PACK_COLS - mix_norm_c.size))
    tail = _to_bits16(tail.reshape(GATHER_TAIL_ROWS // 2, PACK_COLS))
    gathered = _all_gather(jnp.concatenate([w_pack.astype(jnp.bfloat16), tail], axis=0))
    full = _unpack_gathered(gathered[:, :n_rows], shard_shapes)
    c_all = _from_bits16(gathered[:, n_rows:]).reshape(N_DEV, -1)[:, :mix_norm_c.size]
    c_full = c_all.reshape(N_DEV, 2, c_cols).transpose(1, 0, 2).reshape(2, N_DEV * c_cols)
    gains = dict(mix_norm_ab=mix_norm_ab, mla_q_norm=mla_q_norm, mla_kv_norm=mla_kv_norm, mix_norm_c=c_full,
                 gqa_q_norm=gqa_q_norm, gqa_k_norm=gqa_k_norm, ffn_norm=ffn_norm)

    tabs = _rope_tables(s)
    slopes = jnp.exp2(-8.0 * jnp.arange(1, DIL_HEADS + 1, dtype=_F32) / DIL_HEADS)

    h = xs
    saved = []
    for layer in range(DEPTH):
        mix_w, ffn_w = _layer_weights(full, gains, layer)
        if layer % 2 == 0:
            h, sv_mix = _even_fwd(h, mix_w, tabs, slopes, f"l{layer}_mix")
        else:
            h, sv_mix = _odd_fwd(h, mix_w, tabs, f"l{layer}_mix")
        h, sv_ffn = _ffn_fwd(h, ffn_w, f"l{layer}_ffn")
        saved.append((mix_w, ffn_w, sv_mix, sv_ffn))
    loss_local, dh, d_final = _final_loss(h, final_norm, loss_target[0], name="final_loss")

    gfull = {n: [None] * wts[n].shape[0] for n in big_names}
    gsmall = {n: [None] * (wts[n].shape[0] if wts[n].ndim > 1 else 1) for n in _SMALL}
    gsmall["final_norm"][0] = d_final
    for layer in reversed(range(DEPTH)):
        mix_w, ffn_w, sv_mix, sv_ffn = saved[layer]
        i = layer // 2
        dh, gf = _ffn_bwd(dh, sv_ffn, ffn_w, f"l{layer}_ffn")
        gsmall["ffn_norm"][layer] = gf["norm"]
        gfull["ffn_w_in"][layer] = jnp.concatenate([gf["w_gate"], gf["w_up"]], axis=-1)
        gfull["ffn_w_out"][layer] = gf["w_out"]
        if layer % 2 == 0:
            dh, gm = _even_bwd(dh, sv_mix, mix_w, tabs, slopes, f"l{layer}_mix")
            gsmall["mix_norm_ab"][i] = gm["norm"]
            gsmall["mla_q_norm"][i] = gm["q_norm"]
            gsmall["mla_kv_norm"][i] = gm["kv_norm"]
            gfull["w_in_ab"][i] = jnp.concatenate([gm["w_a"][:, :IN_A], gm["w_b"]], axis=-1)
            gfull["mla_w_uq"][i] = _merge_heads_cols(gm["w_uq"], MLA_HEADS, MLA_NOPE)
            gfull["mla_w_ukv"][i] = _merge_heads_cols(gm["w_ukv"], MLA_HEADS, MLA_NOPE)
            gfull["w_out_ab"][i] = gm["w_out"]
        else:
            dh, gm = _odd_bwd(dh, sv_mix, mix_w, tabs, f"l{layer}_mix")
            gsmall["mix_norm_c"][i] = gm["norm"]
            gsmall["gqa_q_norm"][i] = gm["q_norm"]
            gsmall["gqa_k_norm"][i] = gm["k_norm"]
            gfull["gqa_w_q"][i] = gm["w_q"]
            gfull["gqa_w_kv"][i] = gm["w_kv"]
            gfull["gqa_w_o"][i] = gm["w_o"]
    grad_x = dh[None]

    g_parts = jnp.concatenate([_split_for_devices(jnp.stack(gfull[n]), axis) for n, axis in _BIG], axis=1)
    small_part = _pack_small({n: jnp.stack(gsmall[n]) for n in _SMALL})
    small_bits = jnp.broadcast_to(_to_bits16(small_part)[None], (N_DEV, EXCHANGE_TAIL_ROWS, PACK_COLS))
    received = _exchange(jnp.concatenate([g_parts.astype(jnp.bfloat16), small_bits], axis=1))
    g_big, d_big, m_big, v_big = _reduce_adamw(
        received, w_pack, _pack_big({n: mom[n] for n in big_names}), _pack_big({n: var[n] for n in big_names}),
        name="adamw_big")

    def widen_c(shard):
        return lax.dynamic_update_slice(jnp.zeros((2, N_DEV * c_cols), _F32), shard, (0, me * c_cols))

    def small_of(src):
        return _pack_small({n: (widen_c(src[n]) if n == "mix_norm_c" else src[n]) for n in _SMALL})

    small_recv = _from_bits16(received[:, n_rows:])
    g_sm, d_sm, m_sm, v_sm = _reduce_adamw(small_recv, small_of(wts), small_of(mom), small_of(var), name="adamw_small")

    def outputs_of(big_pack, small_pack):
        big = _unpack_big(big_pack, shard_shapes)
        small = _unpack_small(small_pack)
        res = {}
        for n in wts:
            if n in big:
                res[n] = big[n]
            elif n == "mix_norm_c":
                res[n] = lax.dynamic_slice(small[n].reshape(2, N_DEV * c_cols), (0, me * c_cols), (2, c_cols))
            else:
                res[n] = small[n].reshape(wts[n].shape)
        return [res[n] for n in wts]

    loss = lax.psum(loss_local, _AXES)
    return (loss, grad_x, *outputs_of(g_big, g_sm), *outputs_of(d_big, d_sm), *outputs_of(m_big, m_sm),
            *outputs_of(v_big, v_sm))
```

```python
import functools

import jax
import jax.numpy as jnp
from jax import lax
from jax.experimental import pallas as pl
from jax.experimental.pallas import tpu as pltpu

D_MODEL = 1024
DEPTH = 4
GRID_W = 64
NORM_EPS = 1e-6
ROPE_THETA = 10000.0
NEG_INF = -1e30
MLA_HEADS = 8
MLA_Q_RANK = 384
MLA_KV_RANK = 256
MLA_NOPE = 64
MLA_ROPE = 32
MLA_V = 64
DIL_PAIRS = ((128, 1), (512, 4), (2048, 16))
DIL_HALF = 64
DIL_SLOTS = 4
DIL_GROUPS = 3
DIL_HEADS = 12
DIL_HEAD_DIM = 64
GQA_HEADS = 16
GQA_KV_HEADS = 4
GQA_HEAD_DIM = 64
FFN_HIDDEN = 2816
IN_A = MLA_Q_RANK + MLA_KV_RANK + MLA_ROPE
IN_A_PAD = 768
IN_B = 3 * DIL_HEADS * DIL_HEAD_DIM
ADAM_LR = 0.001
ADAM_B1 = 0.9
ADAM_B2 = 0.999
ADAM_EPS = 1e-08
ADAM_WD = 0.01
ADAM_STEP = 10

LANES = 128
SUBLANES_16BIT = 16
VMEM_LIMIT_BYTES = 56 * 1024 * 1024

MM_TILE = 1408

N_DEV = 8
PACK_COLS = 1024

_MXU = jnp.bfloat16
_ACT = jnp.bfloat16
_F32 = jnp.float32

_AXES = ("x", "y", "c")


def _params(sem):
    return pltpu.CompilerParams(dimension_semantics=sem, vmem_limit_bytes=VMEM_LIMIT_BYTES)


def _pick(n, cap):
    for t in range(cap - cap % LANES, 0, -LANES):
        if n % t == 0:
            return t
    return n


def _rows(m, target):
    t = m
    while t > target and t % 2 == 0:
        t //= 2
    return t


def _matmul(a, b, *, trans_a=False, trans_b=False, res=None, scale=None, out_dtype=_F32, name):
    if trans_a:
        k, m = a.shape
    else:
        m, k = a.shape
    if trans_b:
        n, kb = b.shape
    else:
        kb, n = b.shape
    assert k == kb, (a.shape, b.shape)
    tm, tn, tk = _pick(m, MM_TILE), _pick(n, MM_TILE), _pick(k, MM_TILE)
    nk = k // tk
    dims = (((0 if trans_a else 1,), (1 if trans_b else 0,)), ((), ()))

    def body(*refs):
        if res is None:
            a_ref, b_ref, o_ref, acc = refs
            r_ref = None
        else:
            a_ref, b_ref, r_ref, o_ref, acc = refs
        kk = pl.program_id(2)

        @pl.when(kk == 0)
        def _():
            acc[...] = jnp.zeros_like(acc)

        acc[...] += lax.dot_general(a_ref[...].astype(_MXU), b_ref[...].astype(_MXU), dims,
                                    preferred_element_type=_F32)

        @pl.when(kk == nk - 1)
        def _():
            r = acc[...]
            if scale is not None:
                r = r * scale
            if r_ref is not None:
                r = r + r_ref[...].astype(_F32)
            o_ref[...] = r.astype(out_dtype)

    a_spec = (pl.BlockSpec((tk, tm), lambda i, j, kk: (kk, i)) if trans_a
              else pl.BlockSpec((tm, tk), lambda i, j, kk: (i, kk)))
    b_spec = (pl.BlockSpec((tn, tk), lambda i, j, kk: (j, kk)) if trans_b
              else pl.BlockSpec((tk, tn), lambda i, j, kk: (kk, j)))
    o_spec = pl.BlockSpec((tm, tn), lambda i, j, kk: (i, j))
    in_specs = [a_spec, b_spec] + ([o_spec] if res is not None else [])
    args = (a, b) + ((res,) if res is not None else ())
    return pl.pallas_call(
        body, name=name, grid=(m // tm, n // tn, nk),
        in_specs=in_specs, out_specs=o_spec,
        out_shape=jax.ShapeDtypeStruct((m, n), out_dtype),
        scratch_shapes=[pltpu.VMEM((tm, tn), _F32)],
        compiler_params=_params(("parallel", "parallel", "arbitrary")),
    )(*args)


def _rmsnorm(x, g, *, out_dtype, name, rows=512):
    m, d = x.shape
    tm = _rows(m, rows)

    def body(x_ref, g_ref, o_ref):
        xf = x_ref[...].astype(_F32)
        r = lax.rsqrt(jnp.mean(xf * xf, axis=-1, keepdims=True) + NORM_EPS)
        o_ref[...] = ((xf * r) * g_ref[...]).astype(out_dtype)

    return pl.pallas_call(
        body, name=name, grid=(m // tm,),
        in_specs=[pl.BlockSpec((tm, d), lambda i: (i, 0)), pl.BlockSpec((1, d), lambda i: (0, 0))],
        out_specs=pl.BlockSpec((tm, d), lambda i: (i, 0)),
        out_shape=jax.ShapeDtypeStruct((m, d), out_dtype),
        compiler_params=_params(("parallel",)),
    )(x, g.reshape(1, d).astype(_F32))


def _rmsnorm_bwd(x, g, dy, dres=None, *, out_dtype=_F32, name, rows=512):
    m, d = x.shape
    tm = _rows(m, rows)

    def body(*refs):
        if dres is None:
            x_ref, g_ref, dy_ref, dx_ref, dg_ref = refs
            r_ref = None
        else:
            x_ref, g_ref, dy_ref, r_ref, dx_ref, dg_ref = refs

        @pl.when(pl.program_id(0) == 0)
        def _():
            dg_ref[...] = jnp.zeros_like(dg_ref)

        xf = x_ref[...].astype(_F32)
        r = lax.rsqrt(jnp.mean(xf * xf, axis=-1, keepdims=True) + NORM_EPS)
        xh = xf * r
        dyf = dy_ref[...].astype(_F32)
        dg_ref[...] += jnp.sum(dyf * xh, axis=0, keepdims=True)
        gdy = dyf * g_ref[...]
        dx = r * (gdy - xh * jnp.mean(gdy * xh, axis=-1, keepdims=True))
        if r_ref is not None:
            dx = dx + r_ref[...].astype(_F32)
        dx_ref[...] = dx.astype(out_dtype)

    row = pl.BlockSpec((tm, d), lambda i: (i, 0))
    vec = pl.BlockSpec((1, d), lambda i: (0, 0))
    in_specs = [row, vec, row] + ([row] if dres is not None else [])
    args = (x, g.reshape(1, d).astype(_F32), dy) + ((dres,) if dres is not None else ())
    dx, dg = pl.pallas_call(
        body, name=name, grid=(m // tm,),
        in_specs=in_specs, out_specs=[row, vec],
        out_shape=[jax.ShapeDtypeStruct((m, d), out_dtype), jax.ShapeDtypeStruct((1, d), _F32)],
        compiler_params=_params(("arbitrary",)),
    )(*args)
    return dx, dg.reshape(d)


def _rotate(xf, c, sn):
    w = xf.shape[1]
    if w > LANES:
        c, sn = jnp.tile(c, (1, w // LANES)), jnp.tile(sn, (1, w // LANES))
    lane = lax.broadcasted_iota(jnp.int32, xf.shape, 1)
    sw = jnp.where((lane & 31) < 16, pltpu.roll(xf, w - 16, 1), pltpu.roll(xf, 16, 1))
    return xf * c + sw * sn


def _seg_mean(v, seg_ref):
    outs = []
    for c in range(v.shape[1] // LANES):
        piece = v[:, c * LANES:(c + 1) * LANES]
        hi = piece.astype(jnp.bfloat16)
        lo = (piece - hi.astype(_F32)).astype(jnp.bfloat16)
        outs.append(jnp.dot(hi, seg_ref[...], preferred_element_type=_F32)
                    + jnp.dot(lo, seg_ref[...], preferred_element_type=_F32))
    return jnp.concatenate(outs, axis=1) if len(outs) > 1 else outs[0]


def _seg_matrix():
    lane = jnp.arange(LANES) // GQA_HEAD_DIM
    return ((lane[:, None] == lane[None, :]).astype(_F32) / GQA_HEAD_DIM).astype(jnp.bfloat16)


def _headnorm_rope(x, gain, cos_t, sin_t, *, scale=None, name):
    s, w = x.shape
    ts = _rows(s, 512)

    def body(x_ref, g_ref, seg_ref, c_ref, s_ref, o_ref):
        xf = x_ref[...]
        r = lax.rsqrt(_seg_mean(xf * xf, seg_ref) + NORM_EPS)
        y = _rotate((xf * r) * g_ref[...], c_ref[...], s_ref[...])
        if scale is not None:
            y = y * scale
        o_ref[...] = y.astype(o_ref.dtype)

    row = pl.BlockSpec((ts, w), lambda i: (i, 0))
    tab = pl.BlockSpec((ts, LANES), lambda i: (i, 0))
    return pl.pallas_call(
        body, name=name, grid=(s // ts,),
        in_specs=[row, pl.BlockSpec((1, w), lambda i: (0, 0)), pl.BlockSpec((LANES, LANES), lambda i: (0, 0)), tab, tab],
        out_specs=row, out_shape=jax.ShapeDtypeStruct((s, w), _ACT),
        compiler_params=_params(("parallel",)),
    )(x, jnp.tile(gain.astype(_F32), w // GQA_HEAD_DIM).reshape(1, w), _seg_matrix(), cos_t, sin_t)


def _headnorm_rope_bwd(x, gain, dy, cos_t, sin_t, *, name):
    s, w = x.shape
    ts = _rows(s, 512)

    def body(x_ref, g_ref, seg_ref, c_ref, s_ref, dy_ref, dx_ref, dg_ref):
        @pl.when(pl.program_id(0) == 0)
        def _():
            dg_ref[...] = jnp.zeros_like(dg_ref)

        xf = x_ref[...]
        r = lax.rsqrt(_seg_mean(xf * xf, seg_ref) + NORM_EPS)
        xh = xf * r
        dyn = _rotate(dy_ref[...].astype(_F32), c_ref[...], -s_ref[...])
        dg_ref[...] += jnp.sum(dyn * xh, axis=0, keepdims=True)
        gdy = dyn * g_ref[...]
        dx_ref[...] = (r * (gdy - xh * _seg_mean(gdy * xh, seg_ref))).astype(dx_ref.dtype)

    row = pl.BlockSpec((ts, w), lambda i: (i, 0))
    vec = pl.BlockSpec((1, w), lambda i: (0, 0))
    tab = pl.BlockSpec((ts, LANES), lambda i: (i, 0))
    dx, dg = pl.pallas_call(
        body, name=name, grid=(s // ts,),
        in_specs=[row, vec, pl.BlockSpec((LANES, LANES), lambda i: (0, 0)), tab, tab, row],
        out_specs=[row, vec],
        out_shape=[jax.ShapeDtypeStruct((s, w), _ACT), jax.ShapeDtypeStruct((1, w), _F32)],
        compiler_params=_params(("arbitrary",)),
    )(x, jnp.tile(gain.astype(_F32), w // GQA_HEAD_DIM).reshape(1, w), _seg_matrix(), cos_t, sin_t, dy)
    return dx, dg.reshape(w // GQA_HEAD_DIM, GQA_HEAD_DIM).sum(axis=0)


def _rope(x, cos_t, sin_t, *, out_dtype, name, sum_chunks=False, scale=None):
    s, w = x.shape
    assert w % LANES == 0
    ts = _rows(s, 512)
    ow = LANES if sum_chunks else w

    def body(x_ref, c_ref, s_ref, o_ref):
        y = _rotate(x_ref[...].astype(_F32), c_ref[...], s_ref[...])
        if scale is not None:
            y = y * scale
        if sum_chunks:
            shift = w // 2
            while shift >= 32:
                y = y + pltpu.roll(y, shift, 1)
                shift //= 2
            y = y[:, :LANES]
        o_ref[...] = y.astype(out_dtype)

    return pl.pallas_call(
        body, name=name, grid=(s // ts,),
        in_specs=[pl.BlockSpec((ts, w), lambda i: (i, 0)), pl.BlockSpec((ts, LANES), lambda i: (i, 0)),
                  pl.BlockSpec((ts, LANES), lambda i: (i, 0))],
        out_specs=pl.BlockSpec((ts, ow), lambda i: (i, 0)),
        out_shape=jax.ShapeDtypeStruct((s, ow), out_dtype),
        compiler_params=_params(("parallel",)),
    )(x, cos_t, sin_t)


_NT = (((1,), (1,)), ((), ()))
_TN = (((0,), (0,)), ((), ()))
LOG2_E = 1.4426950408889634
LN_2 = 0.6931471805599453
ATTN_FWD_ROWS = 1024
ATTN_ROWS = 512
ATTN_CHAINS = 2
ATTN_FWD_KEYS = 2048
ATTN_BWD_KEYS = 1024


def _attn_fwd(q, k, v, *, out_dtype, name):
    h, s, dk = q.shape
    g, _, dv = v.shape
    r = h // g
    rc = min(ATTN_FWD_ROWS, s)
    hp = min(r, ATTN_CHAINS)
    nrc = max(1, min(ATTN_CHAINS // hp, s // rc))
    tq = rc * nrc
    tk = min(ATTN_FWD_KEYS, s)
    nhp, nq, nk = r // hp, s // tq, s // tk
    units = [(a, c) for a in range(hp) for c in range(nrc)]

    def body(q_ref, k_ref, v_ref, o_ref, lse_ref):
        qs = [q_ref[a, c * rc:(c + 1) * rc, :] for a, c in units]
        init = tuple((jnp.full((rc, 1), NEG_INF, _F32), jnp.zeros((rc, 1), _F32), jnp.zeros((rc, dv), _F32))
                     for _ in units)

        def trip(j, carry):
            rows = pl.ds(pl.multiple_of(j * tk, tk), tk)
            kb, vb = k_ref[0, rows, :], v_ref[0, rows, :]
            out = []
            for u in range(len(units)):
                m, l, acc = carry[u]
                sc = lax.dot_general(qs[u], kb, _NT, preferred_element_type=_F32)
                m_new = jnp.maximum(m, jnp.max(sc, axis=1, keepdims=True))
                p = jnp.exp2(sc - m_new)
                alpha = jnp.exp2(m - m_new)
                l = alpha * l + jnp.sum(p, axis=1, keepdims=True)
                acc = alpha * acc + jnp.dot(p.astype(_MXU), vb, preferred_element_type=_F32)
                out.append((m_new, l, acc))
            return tuple(out)

        fin = lax.fori_loop(0, nk, trip, init)
        for u, (a, c) in enumerate(units):
            m, l, acc = fin[u]
            o_ref[a, c * rc:(c + 1) * rc, :] = (acc / l).astype(out_dtype)
            lse_ref[a, c * rc:(c + 1) * rc, :] = m + jnp.log2(l)

    q_blk = lambda gg, hh, i: (gg * nhp + hh, i, 0)
    kv_blk = lambda gg, hh, i: (gg, 0, 0)
    return pl.pallas_call(
        body, name=name, grid=(g, nhp, nq),
        in_specs=[pl.BlockSpec((hp, tq, dk), q_blk), pl.BlockSpec((1, s, dk), kv_blk),
                  pl.BlockSpec((1, s, dv), kv_blk)],
        out_specs=[pl.BlockSpec((hp, tq, dv), q_blk), pl.BlockSpec((hp, tq, 1), q_blk)],
        out_shape=[jax.ShapeDtypeStruct((h, s, dv), out_dtype), jax.ShapeDtypeStruct((h, s, 1), _F32)],
        compiler_params=_params(("parallel", "parallel", "parallel")),
    )(q, k, v)


def _attn_bwd(q, k, v, o, do, lse2, *, scale, name):
    h, s, dk = q.shape
    g, _, dv = v.shape
    r = h // g
    hp = min(r, ATTN_CHAINS)
    tq = min(ATTN_ROWS, s)
    tk = min(ATTN_BWD_KEYS * (ATTN_CHAINS // hp), s)
    nhp, nq, nk = r // hp, s // tq, s // tk

    def body(q_ref, k_ref, v_ref, o_ref, do_ref, lse_ref, dq_ref, dk_ref, dv_ref):
        hh, i = pl.program_id(1), pl.program_id(2)

        @pl.when((hh == 0) & (i == 0))
        def _():
            dk_ref[...] = jnp.zeros_like(dk_ref)
            dv_ref[...] = jnp.zeros_like(dv_ref)

        qs = [q_ref[a] for a in range(hp)]
        dos = [do_ref[a] for a in range(hp)]
        ls = [lse_ref[a] for a in range(hp)]
        dls = [jnp.sum(do_ref[a].astype(_F32) * o_ref[a].astype(_F32), axis=1, keepdims=True) for a in range(hp)]

        def trip(j, carry):
            rows = pl.ds(pl.multiple_of(j * tk, tk), tk)
            kb, vb = k_ref[0, rows, :], v_ref[0, rows, :]
            out = []
            for a in range(hp):
                p = jnp.exp2(lax.dot_general(qs[a], kb, _NT, preferred_element_type=_F32) - ls[a])
                dp = lax.dot_general(dos[a], vb, _NT, preferred_element_type=_F32)
                ds = (p * (dp - dls[a])).astype(_MXU)
                dv_ref[0, rows, :] += lax.dot_general(p.astype(_MXU), dos[a], _TN, preferred_element_type=_F32)
                dk_ref[0, rows, :] += lax.dot_general(ds, qs[a], _TN, preferred_element_type=_F32)
                out.append(carry[a] + jnp.dot(ds, kb, preferred_element_type=_F32))
            return tuple(out)

        fin = lax.fori_loop(0, nk, trip, tuple(jnp.zeros((tq, dk), _F32) for _ in range(hp)))
        for a in range(hp):
            dq_ref[a] = fin[a] * scale

        @pl.when((hh == nhp - 1) & (i == nq - 1))
        def _():
            dk_ref[...] = dk_ref[...] * LN_2

    q_blk = lambda gg, hh, i: (gg * nhp + hh, i, 0)
    kv_blk = lambda gg, hh, i: (gg, 0, 0)
    return pl.pallas_call(
        body, name=name, grid=(g, nhp, nq),
        in_specs=[pl.BlockSpec((hp, tq, dk), q_blk), pl.BlockSpec((1, s, dk), kv_blk), pl.BlockSpec((1, s, dv), kv_blk),
                  pl.BlockSpec((hp, tq, dv), q_blk), pl.BlockSpec((hp, tq, dv), q_blk), pl.BlockSpec((hp, tq, 1), q_blk)],
        out_specs=[pl.BlockSpec((hp, tq, dk), q_blk), pl.BlockSpec((1, s, dk), kv_blk),
                   pl.BlockSpec((1, s, dv), kv_blk)],
        out_shape=[jax.ShapeDtypeStruct((h, s, dk), _F32), jax.ShapeDtypeStruct((g, s, dk), _F32),
                   jax.ShapeDtypeStruct((g, s, dv), _F32)],
        compiler_params=_params(("arbitrary", "arbitrary", "arbitrary")),
    )(q, k, v, o, do, lse2)


WIN_ROWS = 512


def _win_geometry(s, dil):
    t = min(WIN_ROWS, s)
    length = s // dil
    lg = length.bit_length() - 1
    assert 1 << lg == length and t % DIL_HALF == 0 and s % t == 0
    return t, t + 2 * DIL_HALF, lg, t // DIL_HALF, s // DIL_HALF


def _win_specs(hn, t, d, halo_per_blk, n_halo):
    return [pl.BlockSpec((hn, DIL_HALF, d), lambda i: (0, jnp.maximum(i * halo_per_blk - 1, 0), 0)),
            pl.BlockSpec((hn, t, d), lambda i: (0, i, 0)),
            pl.BlockSpec((hn, DIL_HALF, d), lambda i: (0, jnp.minimum((i + 1) * halo_per_blk, n_halo - 1), 0))]


def _win_mask(i, t, w, lg, dil, wide_rows):
    shape = (w, t) if wide_rows else (t, w)
    rows = lax.broadcasted_iota(jnp.int32, shape, 0)
    cols = lax.broadcasted_iota(jnp.int32, shape, 1)
    base = i * t
    if wide_rows:
        pq, pk = base - DIL_HALF + rows, base + cols
    else:
        pq, pk = base + rows, base - DIL_HALF + cols
    arel = jnp.abs(pk - pq)
    valid = (arel <= DIL_HALF) & ((pk >> lg) == (pq >> lg))
    return valid, (-dil * arel).astype(_F32)


def _win3(lo_ref, mid_ref, hi_ref, a):
    return jnp.concatenate([lo_ref[a], mid_ref[a], hi_ref[a]], axis=0)


def _win_fwd(q, k, v, slopes, *, dil, name):
    hn, s, d = q.shape
    t, w, lg, hpb, n_halo = _win_geometry(s, dil)
    scale = d ** -0.5

    def body(sl_ref, q_ref, klo, kmid, khi, vlo, vmid, vhi, o_ref, lse_ref):
        valid, nb = _win_mask(pl.program_id(0), t, w, lg, dil, False)
        for a in range(hn):
            kw, vw = _win3(klo, kmid, khi, a), _win3(vlo, vmid, vhi, a)
            sc = lax.dot_general(q_ref[a], kw, _NT, preferred_element_type=_F32) * scale
            sc = jnp.where(valid, sc + sl_ref[a] * nb, NEG_INF)
            m = jnp.max(sc, axis=1, keepdims=True)
            e = jnp.exp(sc - m)
            den = jnp.sum(e, axis=1, keepdims=True)
            o_ref[a] = jnp.dot(e.astype(_MXU), vw, preferred_element_type=_F32) / den
            lse_ref[a] = m + jnp.log(den)

    kv = _win_specs(hn, t, d, hpb, n_halo)
    blk = lambda c: pl.BlockSpec((hn, t, c), lambda i: (0, i, 0))
    return pl.pallas_call(
        body, name=name, grid=(s // t,),
        in_specs=[pl.BlockSpec(memory_space=pltpu.SMEM), blk(d)] + kv + kv,
        out_specs=[blk(d), blk(1)],
        out_shape=[jax.ShapeDtypeStruct((hn, s, d), _F32), jax.ShapeDtypeStruct((hn, s, 1), _F32)],
        compiler_params=_params(("parallel",)),
    )(slopes.astype(_F32), q, k, k, k, v, v, v)


def _win_bwd_dq(q, k, v, do, lse, delta, slopes, *, dil, name):
    hn, s, d = q.shape
    t, w, lg, hpb, n_halo = _win_geometry(s, dil)
    scale = d ** -0.5

    def body(sl_ref, q_ref, klo, kmid, khi, vlo, vmid, vhi, do_ref, lse_ref, dl_ref, dq_ref):
        valid, nb = _win_mask(pl.program_id(0), t, w, lg, dil, False)
        for a in range(hn):
            kw, vw = _win3(klo, kmid, khi, a), _win3(vlo, vmid, vhi, a)
            sc = lax.dot_general(q_ref[a], kw, _NT, preferred_element_type=_F32) * scale
            p = jnp.exp(jnp.where(valid, sc + sl_ref[a] * nb, NEG_INF) - lse_ref[a])
            dp = lax.dot_general(do_ref[a], vw, _NT, preferred_element_type=_F32)
            ds = (p * (dp - dl_ref[a])).astype(_MXU)
            dq_ref[a] = jnp.dot(ds, kw, preferred_element_type=_F32) * scale

    kv = _win_specs(hn, t, d, hpb, n_halo)
    blk = lambda c: pl.BlockSpec((hn, t, c), lambda i: (0, i, 0))
    return pl.pallas_call(
        body, name=name, grid=(s // t,),
        in_specs=[pl.BlockSpec(memory_space=pltpu.SMEM), blk(d)] + kv + kv + [blk(d), blk(1), blk(1)],
        out_specs=blk(d),
        out_shape=jax.ShapeDtypeStruct((hn, s, d), _F32),
        compiler_params=_params(("parallel",)),
    )(slopes.astype(_F32), q, k, k, k, v, v, v, do, lse, delta)


def _win_bwd_dkv(q, k, v, do, lse, delta, slopes, *, dil, name):
    hn, s, d = q.shape
    t, w, lg, hpb, n_halo = _win_geometry(s, dil)
    scale = d ** -0.5

    def body(sl_ref, qlo, qmid, qhi, dolo, domid, dohi, llo, lmid, lhi, dllo, dlmid, dlhi, k_ref, v_ref,
             dk_ref, dv_ref):
        valid, nb = _win_mask(pl.program_id(0), t, w, lg, dil, True)
        for a in range(hn):
            qw, dow = _win3(qlo, qmid, qhi, a), _win3(dolo, domid, dohi, a)
            lw, dlw = _win3(llo, lmid, lhi, a), _win3(dllo, dlmid, dlhi, a)
            sc = lax.dot_general(qw, k_ref[a], _NT, preferred_element_type=_F32) * scale
            p = jnp.exp(jnp.where(valid, sc + sl_ref[a] * nb, NEG_INF) - lw)
            dp = lax.dot_general(dow, v_ref[a], _NT, preferred_element_type=_F32)
            ds = (p * (dp - dlw)).astype(_MXU)
            dv_ref[a] = lax.dot_general(p.astype(_MXU), dow, _TN, preferred_element_type=_F32)
            dk_ref[a] = lax.dot_general(ds, qw, _TN, preferred_element_type=_F32) * scale

    blk = lambda c: pl.BlockSpec((hn, t, c), lambda i: (0, i, 0))
    return pl.pallas_call(
        body, name=name, grid=(s // t,),
        in_specs=([pl.BlockSpec(memory_space=pltpu.SMEM)] + _win_specs(hn, t, d, hpb, n_halo)
                  + _win_specs(hn, t, d, hpb, n_halo) + _win_specs(hn, t, 1, hpb, n_halo)
                  + _win_specs(hn, t, 1, hpb, n_halo) + [blk(d), blk(d)]),
        out_specs=[blk(d), blk(d)],
        out_shape=[jax.ShapeDtypeStruct((hn, s, d), _F32), jax.ShapeDtypeStruct((hn, s, d), _F32)],
        compiler_params=_params(("parallel",)),
    )(slopes.astype(_F32), q, q, q, do, do, do, lse, lse, lse, delta, delta, delta, k, v)


def _merge_weights(lse):
    mx = jnp.max(lse, axis=0, keepdims=True)
    e = jnp.exp(lse - mx)
    return e / jnp.sum(e, axis=0, keepdims=True)


def _merge_fwd(o3, lse3, *, name):
    ng, sl, s, d = o3.shape
    t = _rows(s, 512)

    def body(o_ref, l_ref, c_ref):
        wts = _merge_weights(l_ref[:, 0])
        c_ref[0] = jnp.sum(wts * o_ref[:, 0], axis=0).astype(c_ref.dtype)

    return pl.pallas_call(
        body, name=name, grid=(sl, s // t),
        in_specs=[pl.BlockSpec((ng, 1, t, d), lambda a, i: (0, a, i, 0)),
                  pl.BlockSpec((ng, 1, t, 1), lambda a, i: (0, a, i, 0))],
        out_specs=pl.BlockSpec((1, t, d), lambda a, i: (a, i, 0)),
        out_shape=jax.ShapeDtypeStruct((sl, s, d), _ACT),
        compiler_params=_params(("parallel", "parallel")),
    )(o3, lse3)


def _merge_bwd(dcomb, o3, lse3, *, name):
    ng, sl, s, d = o3.shape
    t = _rows(s, 512)

    def body(dc_ref, o_ref, l_ref, do_ref, dl_ref):
        wts = _merge_weights(l_ref[:, 0])
        dc = dc_ref[0].astype(_F32)
        comb = jnp.sum(wts * o_ref[:, 0], axis=0)
        do_ref[:, 0] = (wts * dc[None]).astype(do_ref.dtype)
        dl_ref[:, 0] = wts * jnp.sum(dc * comb, axis=-1, keepdims=True)[None]

    big = pl.BlockSpec((ng, 1, t, d), lambda a, i: (0, a, i, 0))
    small = pl.BlockSpec((ng, 1, t, 1), lambda a, i: (0, a, i, 0))
    return pl.pallas_call(
        body, name=name, grid=(sl, s // t),
        in_specs=[pl.BlockSpec((1, t, d), lambda a, i: (a, i, 0)), big, small],
        out_specs=[big, small],
        out_shape=[jax.ShapeDtypeStruct((ng, sl, s, d), _ACT), jax.ShapeDtypeStruct((ng, sl, s, 1), _F32)],
        compiler_params=_params(("parallel", "parallel")),
    )(dcomb, o3, lse3)


def _swiglu(gate, up, *, name):
    m, n = gate.shape
    tm, tn = _rows(m, 512), _pick(n, 1408)

    def body(g_ref, u_ref, a_ref):
        gf = g_ref[...].astype(_F32)
        a_ref[...] = (gf * jax.nn.sigmoid(gf) * u_ref[...].astype(_F32)).astype(a_ref.dtype)

    blk = pl.BlockSpec((tm, tn), lambda i, j: (i, j))
    return pl.pallas_call(
        body, name=name, grid=(m // tm, n // tn), in_specs=[blk, blk], out_specs=blk,
        out_shape=jax.ShapeDtypeStruct((m, n), _ACT),
        compiler_params=_params(("parallel", "parallel")),
    )(gate, up)


def _swiglu_bwd(gate, up, da, *, name):
    m, n = gate.shape
    tm, tn = _rows(m, 512), _pick(n, 1408)

    def body(g_ref, u_ref, da_ref, dg_ref, du_ref):
        gf = g_ref[...].astype(_F32)
        uf = u_ref[...].astype(_F32)
        daf = da_ref[...].astype(_F32)
        sg = jax.nn.sigmoid(gf)
        dg_ref[...] = (daf * uf * (sg + gf * sg * (1.0 - sg))).astype(dg_ref.dtype)
        du_ref[...] = (daf * (gf * sg)).astype(du_ref.dtype)

    blk = pl.BlockSpec((tm, tn), lambda i, j: (i, j))
    return pl.pallas_call(
        body, name=name, grid=(m // tm, n // tn), in_specs=[blk, blk, blk], out_specs=[blk, blk],
        out_shape=[jax.ShapeDtypeStruct((m, n), _ACT), jax.ShapeDtypeStruct((m, n), _ACT)],
        compiler_params=_params(("parallel", "parallel")),
    )(gate, up, da)


def _final_loss(x, g, target, *, name):
    m, d = x.shape
    tm = _rows(m, 512)

    def body(x_ref, g_ref, t_ref, loss_ref, dx_ref, dg_ref):
        @pl.when(pl.program_id(0) == 0)
        def _():
            loss_ref[...] = jnp.zeros_like(loss_ref)
            dg_ref[...] = jnp.zeros_like(dg_ref)

        xf = x_ref[...]
        r = lax.rsqrt(jnp.mean(xf * xf, axis=-1, keepdims=True) + NORM_EPS)
        xh = xf * r
        err = xh * g_ref[...] - t_ref[...]
        loss_ref[...] += 0.5 * jnp.sum(jnp.mean(err * err, axis=-1, keepdims=True))
        dy = err * (1.0 / d)
        dg_ref[...] += jnp.sum(dy * xh, axis=0, keepdims=True)
        gdy = dy * g_ref[...]
        dx_ref[...] = r * (gdy - xh * jnp.mean(gdy * xh, axis=-1, keepdims=True))

    row = pl.BlockSpec((tm, d), lambda i: (i, 0))
    vec = pl.BlockSpec((1, d), lambda i: (0, 0))
    loss, dx, dg = pl.pallas_call(
        body, name=name, grid=(m // tm,),
        in_specs=[row, vec, row],
        out_specs=[pl.BlockSpec((8, LANES), lambda i: (0, 0)), row, vec],
        out_shape=[jax.ShapeDtypeStruct((8, LANES), _F32), jax.ShapeDtypeStruct((m, d), _F32),
                   jax.ShapeDtypeStruct((1, d), _F32)],
        compiler_params=_params(("arbitrary",)),
    )(x, g.reshape(1, d), target)
    return loss[0, 0], dx, dg.reshape(d)


def _reduce_adamw(parts, w, m, v, *, name):
    rws, cols = w.shape
    tr = rws
    for cand in range(min(rws, 256), 0, -SUBLANES_16BIT):
        if cand % SUBLANES_16BIT == 0 and rws % cand == 0:
            tr = cand
            break

    def body(p_ref, w_ref, m_ref, v_ref, g_ref, d_ref, nm_ref, nv_ref):
        gsum = p_ref[0].astype(_F32)
        for dev in range(1, N_DEV):
            gsum = gsum + p_ref[dev].astype(_F32)
        m2 = ADAM_B1 * m_ref[...] + (1.0 - ADAM_B1) * gsum
        v2 = ADAM_B2 * v_ref[...] + (1.0 - ADAM_B2) * (gsum * gsum)
        m_hat = m2 / (1.0 - ADAM_B1 ** ADAM_STEP)
        v_hat = v2 / (1.0 - ADAM_B2 ** ADAM_STEP)
        g_ref[...] = gsum
        d_ref[...] = -ADAM_LR * (m_hat / (jnp.sqrt(v_hat) + ADAM_EPS) + ADAM_WD * w_ref[...])
        nm_ref[...] = m2
        nv_ref[...] = v2

    blk = pl.BlockSpec((tr, cols), lambda i: (i, 0))
    out = jax.ShapeDtypeStruct((rws, cols), _F32)
    return pl.pallas_call(
        body, name=name, grid=(rws // tr,),
        in_specs=[pl.BlockSpec((N_DEV, tr, cols), lambda i: (0, i, 0)), blk, blk, blk],
        out_specs=[blk, blk, blk, blk], out_shape=[out, out, out, out],
        compiler_params=_params(("parallel",)),
    )(parts, w, m, v)


def _mesh_pos():
    return lax.axis_index("x"), lax.axis_index("y"), lax.axis_index("c")


def _all_gather(block):
    rws, cols = block.shape

    def body(x_ref, out_ref, send_sems, recv_sems, local_sem):
        x, y, c = _mesh_pos()
        me, sibling = (x, y, c), (x, y, 1 - c)
        chips = [(1 - x, y), (x, 1 - y), (1 - x, 1 - y)]

        def slot(px, py, pc):
            return out_ref.at[4 * px + 2 * py + pc]

        def copy(k, blk, to, src=None):
            return pltpu.make_async_remote_copy(
                src_ref=slot(*blk) if src is None else src, dst_ref=slot(*blk),
                send_sem=send_sems.at[k], recv_sem=recv_sems.at[k],
                device_id=to, device_id_type=pl.DeviceIdType.MESH)

        mine = pltpu.make_async_copy(x_ref, slot(*me), local_sem)
        mine.start()
        first = [copy(0, me, sibling, src=x_ref)]
        first += [copy(1 + j, me, (*chip, c), src=x_ref) for j, chip in enumerate(chips)]
        for cp in first:
            cp.start()
        passed = [copy(4 + j, (*chip, c), sibling) for j, chip in enumerate(chips)]
        for j, chip in enumerate(chips):
            copy(1 + j, (*chip, c), me).wait_recv()
            passed[j].start()
        copy(0, sibling, me).wait_recv()
        for j, chip in enumerate(chips):
            copy(4 + j, (*chip, 1 - c), me).wait_recv()
        for cp in first + passed:
            cp.wait_send()
        mine.wait()

    return pl.pallas_call(
        body, name="weights_all_gather",
        out_shape=jax.ShapeDtypeStruct((N_DEV, rws, cols), block.dtype),
        in_specs=[pl.BlockSpec(memory_space=pl.ANY)],
        out_specs=pl.BlockSpec(memory_space=pl.ANY),
        scratch_shapes=[pltpu.SemaphoreType.DMA((7,)), pltpu.SemaphoreType.DMA((7,)), pltpu.SemaphoreType.DMA],
    )(block)


def _exchange(parts):
    _, rws, cols = parts.shape

    def body(g_ref, out_ref, send_sems, recv_sems, local_sem):
        x, y, c = _mesh_pos()
        me = 4 * x + 2 * y + c
        mine = pltpu.make_async_copy(g_ref.at[me], out_ref.at[me], local_sem)
        mine.start()
        copies = []
        for k in range(1, N_DEV):
            px = 1 - x if k & 4 else x
            py = 1 - y if k & 2 else y
            pc = 1 - c if k & 1 else c
            peer = 4 * px + 2 * py + pc
            send = pltpu.make_async_remote_copy(
                src_ref=g_ref.at[peer], dst_ref=out_ref.at[me],
                send_sem=send_sems.at[k - 1], recv_sem=recv_sems.at[k - 1],
                device_id=(px, py, pc), device_id_type=pl.DeviceIdType.MESH)
            send.start()
            arrival = pltpu.make_async_remote_copy(
                src_ref=g_ref.at[peer], dst_ref=out_ref.at[peer],
                send_sem=send_sems.at[k - 1], recv_sem=recv_sems.at[k - 1],
                device_id=(px, py, pc), device_id_type=pl.DeviceIdType.MESH)
            copies.append((send, arrival))
        for _, arrival in copies:
            arrival.wait_recv()
        for send, _ in copies:
            send.wait_send()
        mine.wait()

    return pl.pallas_call(
        body, name="grads_exchange",
        out_shape=jax.ShapeDtypeStruct(parts.shape, parts.dtype),
        in_specs=[pl.BlockSpec(memory_space=pl.ANY)],
        out_specs=pl.BlockSpec(memory_space=pl.ANY),
        scratch_shapes=[pltpu.SemaphoreType.DMA((7,)), pltpu.SemaphoreType.DMA((7,)), pltpu.SemaphoreType.DMA],
    )(parts)


_BIG = (("w_in_ab", 2), ("mla_w_uq", 1), ("mla_w_ukv", 1), ("w_out_ab", 2), ("gqa_w_q", 1), ("gqa_w_kv", 1),
        ("gqa_w_o", 1), ("ffn_w_in", 2), ("ffn_w_out", 1))
_SMALL = ("mix_norm_ab", "ffn_norm", "final_norm", "mix_norm_c", "mla_q_norm", "mla_kv_norm", "gqa_q_norm", "gqa_k_norm")
SMALL_ROWS = 16
GATHER_TAIL_ROWS = 16
EXCHANGE_TAIL_ROWS = 32


def _view3(a):
    return a.reshape(a.shape[0], a.shape[1], -1)


def _pad_rows(a2d):
    pad = -a2d.shape[0] % SUBLANES_16BIT
    return jnp.pad(a2d, ((0, pad), (0, 0))) if pad else a2d


def _pack_rows(shard):
    return _pad_rows(shard.reshape(-1, PACK_COLS))


def _packed_rows(shape):
    n = 1
    for d in shape:
        n *= d
    rows = n // PACK_COLS
    return rows + (-rows % SUBLANES_16BIT)


def _pack_big(shards):
    return jnp.concatenate([_pack_rows(shards[n]) for n, _ in _BIG], axis=0)


def _unpack_big(packed, shapes):
    out, off = {}, 0
    for n, _ in _BIG:
        size = 1
        for d in shapes[n]:
            size *= d
        out[n] = packed[off:off + size // PACK_COLS].reshape(shapes[n])
        off += _packed_rows(shapes[n])
    return out


def _unpack_gathered(gathered, shapes):
    out, off = {}, 0
    for n, axis in _BIG:
        size = 1
        for d in shapes[n]:
            size *= d
        l3 = (shapes[n][0], shapes[n][1], size // (shapes[n][0] * shapes[n][1]))
        sh = gathered[:, off:off + size // PACK_COLS].reshape((N_DEV,) + l3)
        if axis == 1:
            full = sh.transpose(1, 0, 2, 3).reshape(l3[0], N_DEV * l3[1], l3[2])
        else:
            full = sh.transpose(1, 2, 0, 3).reshape(l3[0], l3[1], N_DEV * l3[2])
        out[n] = full
        off += _packed_rows(shapes[n])
    return out


def _split_for_devices(full, axis):
    l, rws, cols = full.shape
    if axis == 1:
        sh = full.reshape(l, N_DEV, rws // N_DEV, cols).transpose(1, 0, 2, 3)
    else:
        sh = full.reshape(l, rws, N_DEV, cols // N_DEV).transpose(2, 0, 1, 3)
    flat = sh.reshape(N_DEV, -1, PACK_COLS)
    pad = -flat.shape[1] % SUBLANES_16BIT
    return jnp.pad(flat, ((0, 0), (0, pad), (0, 0))) if pad else flat


def _to_bits16(a_f32_rows):
    r = a_f32_rows.shape[0]
    return lax.bitcast_convert_type(a_f32_rows, jnp.bfloat16).reshape(2 * r, PACK_COLS)


def _from_bits16(a_bf16_rows):
    lead, r = a_bf16_rows.shape[:-2], a_bf16_rows.shape[-2]
    return lax.bitcast_convert_type(a_bf16_rows.reshape(lead + (r // 2, PACK_COLS, 2)), _F32)


def _small_sizes():
    return {"mix_norm_ab": 2 * D_MODEL, "ffn_norm": DEPTH * D_MODEL, "final_norm": D_MODEL, "mix_norm_c": 2 * D_MODEL,
            "mla_q_norm": 2 * MLA_Q_RANK, "mla_kv_norm": 2 * MLA_KV_RANK, "gqa_q_norm": 2 * GQA_HEAD_DIM,
            "gqa_k_norm": 2 * GQA_HEAD_DIM}


def _pack_small(vals):
    flat = jnp.concatenate([vals[n].reshape(-1).astype(_F32) for n in _SMALL])
    return jnp.pad(flat, (0, SMALL_ROWS * PACK_COLS - flat.shape[0])).reshape(SMALL_ROWS, PACK_COLS)


def _unpack_small(pack):
    flat, out, off = pack.reshape(-1), {}, 0
    sizes = _small_sizes()
    for n in _SMALL:
        out[n] = flat[off:off + sizes[n]]
        off += sizes[n]
    return out


def _angles(pos, dim):
    freqs = ROPE_THETA ** (-jnp.arange(0, dim, 2, dtype=_F32) / dim)
    ang = pos.astype(_F32)[:, None] * freqs[None, :]
    return jnp.cos(ang), jnp.sin(ang)


def _rope_tables(s):
    pos = jnp.arange(s)
    cos_t, sin_t = _angles(pos, MLA_ROPE)
    mla_c = jnp.tile(jnp.concatenate([cos_t, cos_t], -1), (1, LANES // 32))
    mla_s = jnp.tile(jnp.concatenate([-sin_t, sin_t], -1), (1, LANES // 32))
    rows = s // GRID_W
    row_idx = jnp.broadcast_to(jnp.arange(rows)[:, None], (rows, GRID_W)).reshape(-1)
    col_idx = jnp.broadcast_to(jnp.arange(GRID_W)[None, :], (rows, GRID_W)).reshape(-1)
    cos_r, sin_r = _angles(row_idx, GQA_HEAD_DIM // 2)
    cos_c, sin_c = _angles(col_idx, GQA_HEAD_DIM // 2)
    gqa_c = jnp.tile(jnp.concatenate([cos_r, cos_r, cos_c, cos_c], -1), (1, LANES // GQA_HEAD_DIM))
    gqa_s = jnp.tile(jnp.concatenate([-sin_r, sin_r, -sin_c, sin_c], -1), (1, LANES // GQA_HEAD_DIM))
    return (mla_c, mla_s), (gqa_c, gqa_s)


def _heads(x2d, h):
    s = x2d.shape[0]
    return x2d.reshape(s, h, -1).transpose(1, 0, 2)


def _unheads(xh):
    h, s, d = xh.shape
    return xh.transpose(1, 0, 2).reshape(s, h * d)


def _to_res(x2d, dil):
    s = x2d.shape[0]
    return x2d.reshape(s // dil, dil, DIL_SLOTS, -1).transpose(2, 1, 0, 3).reshape(DIL_SLOTS, s, -1)


def _from_res(xh, dil):
    sl, s, d = xh.shape
    return xh.reshape(sl, dil, s // dil, d).transpose(2, 1, 0, 3).reshape(s, sl * d)


def _res_to_tok(xh, dil):
    sl, s, c = xh.shape
    return xh.reshape(sl, dil, s // dil, c).transpose(0, 2, 1, 3).reshape(sl, s, c)


def _tok_to_res(xh, dil):
    sl, s, c = xh.shape
    return xh.reshape(sl, s // dil, dil, c).transpose(0, 2, 1, 3).reshape(sl, s, c)


def _ffn_fwd(x, w, tag):
    hn = _rmsnorm(x, w["norm"], out_dtype=_ACT, name=f"{tag}_norm")
    gate = _matmul(hn, w["w_gate"], out_dtype=_ACT, name=f"{tag}_gate")
    up = _matmul(hn, w["w_up"], out_dtype=_ACT, name=f"{tag}_up")
    act = _swiglu(gate, up, name=f"{tag}_swiglu")
    out = _matmul(act, w["w_out"], res=x, name=f"{tag}_out")
    return out, (x, hn, gate, up, act)


def _ffn_bwd(dout, saved, w, tag):
    x, hn, gate, up, act = saved
    dact = _matmul(dout, w["w_out"], trans_b=True, out_dtype=_ACT, name=f"{tag}_dact")
    d_w_out = _matmul(act, dout, trans_a=True, name=f"{tag}_dwout")
    dgate, dup = _swiglu_bwd(gate, up, dact, name=f"{tag}_dswiglu")
    d_w_gate = _matmul(hn, dgate, trans_a=True, name=f"{tag}_dwgate")
    d_w_up = _matmul(hn, dup, trans_a=True, name=f"{tag}_dwup")
    dhn = _matmul(dgate, w["w_gate"], trans_b=True, name=f"{tag}_dhn_gate")
    dhn = _matmul(dup, w["w_up"], trans_b=True, res=dhn, name=f"{tag}_dhn_up")
    dx, dnorm = _rmsnorm_bwd(x, w["norm"], dhn, dout, name=f"{tag}_dnorm")
    return dx, {"norm": dnorm, "w_gate": d_w_gate, "w_up": d_w_up, "w_out": d_w_out}


def _even_fwd(x, w, tabs, slopes, tag):
    s = x.shape[0]
    (mla_c, mla_s), _ = tabs
    hn = _rmsnorm(x, w["norm"], out_dtype=_ACT, name=f"{tag}_norm")
    za = _matmul(hn, w["w_a"], name=f"{tag}_in_a")
    zb = _matmul(hn, w["w_b"], out_dtype=_ACT, name=f"{tag}_in_b")
    cq, ckv, kr = za[:, :MLA_Q_RANK], za[:, MLA_Q_RANK:MLA_Q_RANK + MLA_KV_RANK], za[:, MLA_Q_RANK + MLA_KV_RANK:]
    cqn = _rmsnorm(cq, w["q_norm"], out_dtype=_ACT, name=f"{tag}_qnorm")
    ckvn = _rmsnorm(ckv, w["kv_norm"], out_dtype=_ACT, name=f"{tag}_kvnorm")
    q = _matmul(cqn, w["w_uq"], scale=(MLA_NOPE + MLA_ROPE) ** -0.5 * LOG2_E,
                name=f"{tag}_uq")
    kv = _matmul(ckvn, w["w_ukv"], out_dtype=_ACT, name=f"{tag}_ukv")
    nn = MLA_HEADS * MLA_NOPE
    q_rope = _rope(q[:, nn:], mla_c, mla_s, out_dtype=_ACT, name=f"{tag}_rope_q")
    k_rope = _rope(kr, mla_c, mla_s, out_dtype=_ACT, name=f"{tag}_rope_k")[:, :MLA_ROPE]
    qh = jnp.concatenate([_heads(q[:, :nn].astype(_ACT), MLA_HEADS), _heads(q_rope, MLA_HEADS)], axis=-1)
    kh = jnp.concatenate([_heads(kv[:, :nn], MLA_HEADS),
                          jnp.broadcast_to(k_rope[None], (MLA_HEADS, s, MLA_ROPE))], axis=-1)
    vh = _heads(kv[:, nn:], MLA_HEADS)
    oa, lse_a = _attn_fwd(qh, kh, vh, out_dtype=_ACT, name=f"{tag}_mla")

    nd, ng = DIL_HEADS * DIL_HEAD_DIM, DIL_SLOTS * DIL_HEAD_DIM
    outs, lses, dil_saved = [], [], []
    for gi, (_, dil) in enumerate(DIL_PAIRS):
        hs = slice(gi * DIL_SLOTS, (gi + 1) * DIL_SLOTS)
        qr, kr_, vr = (_to_res(zb[:, a * nd + gi * ng:a * nd + (gi + 1) * ng], dil) for a in range(3))
        o, l = _win_fwd(qr, kr_, vr, slopes[hs], dil=dil, name=f"{tag}_dil{gi}")
        dil_saved.append((qr, kr_, vr, l))
        outs.append(_res_to_tok(o, dil))
        lses.append(_res_to_tok(l, dil))
    o3, lse3 = jnp.stack(outs), jnp.stack(lses)
    comb = _merge_fwd(o3, lse3, name=f"{tag}_merge")
    cat = jnp.concatenate([_unheads(oa), _unheads(comb)], axis=-1)
    out = _matmul(cat, w["w_out"], res=x, name=f"{tag}_out")
    return out, (x, hn, cq, ckv, cqn, ckvn, qh, kh, vh, oa, lse_a, dil_saved, o3, lse3, cat)


def _even_bwd(dout, saved, w, tabs, slopes, tag):
    x, hn, cq, ckv, cqn, ckvn, qh, kh, vh, oa, lse_a, dil_saved, o3, lse3, cat = saved
    (mla_c, mla_s), _ = tabs
    nn = MLA_HEADS * MLA_NOPE
    dcat = _matmul(dout, w["w_out"], trans_b=True, out_dtype=_ACT, name=f"{tag}_dcat")
    d_w_out = _matmul(cat, dout, trans_a=True, name=f"{tag}_dwout")
    nv = MLA_HEADS * MLA_V

    doa = _heads(dcat[:, :nv], MLA_HEADS)
    dqh, dkh, dvh = _attn_bwd(qh, kh, vh, oa, doa, lse_a, scale=(MLA_NOPE + MLA_ROPE) ** -0.5,
                              name=f"{tag}_mla_bwd")
    dq_rope = _rope(_unheads(dqh[..., MLA_NOPE:]), mla_c, -mla_s, out_dtype=_ACT, name=f"{tag}_drope_q")
    dq = jnp.concatenate([_unheads(dqh[..., :MLA_NOPE]).astype(_ACT), dq_rope], axis=-1)
    dkr = _rope(_unheads(dkh[..., MLA_NOPE:]), mla_c, -mla_s, out_dtype=_F32, sum_chunks=True,
                name=f"{tag}_drope_k")
    dkv = jnp.concatenate([_unheads(dkh[..., :MLA_NOPE]), _unheads(dvh)], axis=-1).astype(_ACT)
    d_w_uq = _matmul(cqn, dq, trans_a=True, name=f"{tag}_dwuq")
    d_w_ukv = _matmul(ckvn, dkv, trans_a=True, name=f"{tag}_dwukv")
    dcqn = _matmul(dq, w["w_uq"], trans_b=True, name=f"{tag}_dcqn")
    dckvn = _matmul(dkv, w["w_ukv"], trans_b=True, name=f"{tag}_dckvn")
    dcq, d_q_norm = _rmsnorm_bwd(cq, w["q_norm"], dcqn, out_dtype=_ACT, name=f"{tag}_dqnorm")
    dckv, d_kv_norm = _rmsnorm_bwd(ckv, w["kv_norm"], dckvn, out_dtype=_ACT, name=f"{tag}_dkvnorm")
    lane = jnp.arange(LANES) < MLA_ROPE
    dza = jnp.concatenate([dcq, dckv, jnp.where(lane[None], dkr, 0.0).astype(_ACT)], axis=-1)

    dcomb = _heads(dcat[:, nv:], DIL_SLOTS)
    do3, delta3 = _merge_bwd(dcomb, o3, lse3, name=f"{tag}_dmerge")
    dqs, dks, dvs = [], [], []
    for gi, (_, dil) in enumerate(DIL_PAIRS):
        hs = slice(gi * DIL_SLOTS, (gi + 1) * DIL_SLOTS)
        qr, kr_, vr, l = dil_saved[gi]
        grads = (qr, kr_, vr, _tok_to_res(do3[gi], dil), l, _tok_to_res(delta3[gi], dil), slopes[hs])
        dqs.append(_from_res(_win_bwd_dq(*grads, dil=dil, name=f"{tag}_dil{gi}_dq"), dil))
        b, c = _win_bwd_dkv(*grads, dil=dil, name=f"{tag}_dil{gi}_dkv")
        dks.append(_from_res(b, dil))
        dvs.append(_from_res(c, dil))
    dzb = jnp.concatenate(dqs + dks + dvs, axis=-1).astype(_ACT)

    d_w_a = _matmul(hn, dza, trans_a=True, name=f"{tag}_dwa")
    d_w_b = _matmul(hn, dzb, trans_a=True, name=f"{tag}_dwb")
    dhn = _matmul(dza, w["w_a"], trans_b=True, name=f"{tag}_dhn_a")
    dhn = _matmul(dzb, w["w_b"], trans_b=True, res=dhn, name=f"{tag}_dhn_b")
    dx, dnorm = _rmsnorm_bwd(x, w["norm"], dhn, dout, name=f"{tag}_dnorm")
    return dx, {"norm": dnorm, "w_a": d_w_a, "w_b": d_w_b, "q_norm": d_q_norm, "kv_norm": d_kv_norm,
                "w_uq": d_w_uq, "w_ukv": d_w_ukv, "w_out": d_w_out}


def _odd_fwd(x, w, tabs, tag):
    s = x.shape[0]
    _, (gqa_c, gqa_s) = tabs
    nk = GQA_KV_HEADS * GQA_HEAD_DIM
    hn = _rmsnorm(x, w["norm"], out_dtype=_ACT, name=f"{tag}_norm")
    q = _matmul(hn, w["w_q"], name=f"{tag}_q")
    kv = _matmul(hn, w["w_kv"], name=f"{tag}_kv")
    k = kv[:, :nk]
    qh = _heads(_headnorm_rope(q, w["q_norm"], gqa_c, gqa_s, scale=GQA_HEAD_DIM ** -0.5 * LOG2_E,
                               name=f"{tag}_prep_q"), GQA_HEADS)
    kh = _heads(_headnorm_rope(k, w["k_norm"], gqa_c, gqa_s, name=f"{tag}_prep_k"), GQA_KV_HEADS)
    vh = _heads(kv[:, nk:].astype(_ACT), GQA_KV_HEADS)
    o, lse = _attn_fwd(qh, kh, vh, out_dtype=_ACT, name=f"{tag}_gqa")
    ocat = _unheads(o)
    out = _matmul(ocat, w["w_o"], res=x, name=f"{tag}_out")
    return out, (x, hn, q, k, qh, kh, vh, o, lse, ocat)


def _odd_bwd(dout, saved, w, tabs, tag):
    x, hn, q, k, qh, kh, vh, o, lse, ocat = saved
    s = x.shape[0]
    _, (gqa_c, gqa_s) = tabs
    docat = _matmul(dout, w["w_o"], trans_b=True, out_dtype=_ACT, name=f"{tag}_docat")
    d_w_o = _matmul(ocat, dout, trans_a=True, name=f"{tag}_dwo")
    doh = _heads(docat, GQA_HEADS)
    dqh, dkh, dvh = _attn_bwd(qh, kh, vh, o, doh, lse, scale=GQA_HEAD_DIM ** -0.5, name=f"{tag}_gqa_bwd")
    dq, d_q_norm = _headnorm_rope_bwd(q, w["q_norm"], _unheads(dqh), gqa_c, gqa_s, name=f"{tag}_dprep_q")
    dk, d_k_norm = _headnorm_rope_bwd(k, w["k_norm"], _unheads(dkh), gqa_c, gqa_s, name=f"{tag}_dprep_k")
    dkv = jnp.concatenate([dk, _unheads(dvh).astype(_ACT)], axis=-1)
    d_w_q = _matmul(hn, dq, trans_a=True, name=f"{tag}_dwq")
    d_w_kv = _matmul(hn, dkv, trans_a=True, name=f"{tag}_dwkv")
    dhn = _matmul(dq, w["w_q"], trans_b=True, name=f"{tag}_dhn_q")
    dhn = _matmul(dkv, w["w_kv"], trans_b=True, res=dhn, name=f"{tag}_dhn_kv")
    dx, dnorm = _rmsnorm_bwd(x, w["norm"], dhn, dout, name=f"{tag}_dnorm")
    return dx, {"norm": dnorm, "w_q": d_w_q, "w_kv": d_w_kv, "q_norm": d_q_norm, "k_norm": d_k_norm, "w_o": d_w_o}


def _split_heads_cols(wm, heads, first):
    rws = wm.shape[0]
    w3 = wm.reshape(rws, heads, -1)
    return jnp.concatenate([w3[:, :, :first].reshape(rws, -1), w3[:, :, first:].reshape(rws, -1)], axis=-1)


def _merge_heads_cols(wm, heads, first):
    rws, cols = wm.shape
    a = wm[:, :heads * first].reshape(rws, heads, first)
    b = wm[:, heads * first:].reshape(rws, heads, cols // heads - first)
    return jnp.concatenate([a, b], axis=-1).reshape(rws, cols)


def _layer_weights(full, gains, layer):
    i = layer // 2
    ffn = {"norm": gains["ffn_norm"][layer], "w_gate": full["ffn_w_in"][layer][:, :FFN_HIDDEN],
           "w_up": full["ffn_w_in"][layer][:, FFN_HIDDEN:], "w_out": full["ffn_w_out"][layer]}
    if layer % 2 == 0:
        w_in = full["w_in_ab"][i]
        mix = {"norm": gains["mix_norm_ab"][i],
               "w_a": jnp.pad(w_in[:, :IN_A], ((0, 0), (0, IN_A_PAD - IN_A))), "w_b": w_in[:, IN_A:],
               "q_norm": gains["mla_q_norm"][i], "kv_norm": gains["mla_kv_norm"][i],
               "w_uq": _split_heads_cols(full["mla_w_uq"][i], MLA_HEADS, MLA_NOPE),
               "w_ukv": _split_heads_cols(full["mla_w_ukv"][i], MLA_HEADS, MLA_NOPE),
               "w_out": full["w_out_ab"][i]}
    else:
        mix = {"norm": gains["mix_norm_c"][i], "w_q": full["gqa_w_q"][i], "w_kv": full["gqa_w_kv"][i],
               "q_norm": gains["gqa_q_norm"][i], "k_norm": gains["gqa_k_norm"][i], "w_o": full["gqa_w_o"][i]}
    return mix, ffn


def kernel(x, mix_norm_ab, w_in_ab, mla_q_norm, mla_kv_norm, mla_w_uq, mla_w_ukv, w_out_ab, mix_norm_c, gqa_w_q, gqa_w_kv, gqa_q_norm, gqa_k_norm, gqa_w_o, ffn_norm, ffn_w_in, ffn_w_out, final_norm, loss_target, m_mix_norm_ab, m_w_in_ab, m_mla_q_norm, m_mla_kv_norm, m_mla_w_uq, m_mla_w_ukv, m_w_out_ab, m_mix_norm_c, m_gqa_w_q, m_gqa_w_kv, m_gqa_q_norm, m_gqa_k_norm, m_gqa_w_o, m_ffn_norm, m_ffn_w_in, m_ffn_w_out, m_final_norm, v_mix_norm_ab, v_w_in_ab, v_mla_q_norm, v_mla_kv_norm, v_mla_w_uq, v_mla_w_ukv, v_w_out_ab, v_mix_norm_c, v_gqa_w_q, v_gqa_w_kv, v_gqa_q_norm, v_gqa_k_norm, v_gqa_w_o, v_ffn_norm, v_ffn_w_in, v_ffn_w_out, v_final_norm):
    wts = dict(mix_norm_ab=mix_norm_ab, w_in_ab=w_in_ab, mla_q_norm=mla_q_norm, mla_kv_norm=mla_kv_norm,
               mla_w_uq=mla_w_uq, mla_w_ukv=mla_w_ukv, w_out_ab=w_out_ab, mix_norm_c=mix_norm_c, gqa_w_q=gqa_w_q,
               gqa_w_kv=gqa_w_kv, gqa_q_norm=gqa_q_norm, gqa_k_norm=gqa_k_norm, gqa_w_o=gqa_w_o, ffn_norm=ffn_norm,
               ffn_w_in=ffn_w_in, ffn_w_out=ffn_w_out, final_norm=final_norm)
    mom = dict(mix_norm_ab=m_mix_norm_ab, w_in_ab=m_w_in_ab, mla_q_norm=m_mla_q_norm, mla_kv_norm=m_mla_kv_norm,
               mla_w_uq=m_mla_w_uq, mla_w_ukv=m_mla_w_ukv, w_out_ab=m_w_out_ab, mix_norm_c=m_mix_norm_c,
               gqa_w_q=m_gqa_w_q, gqa_w_kv=m_gqa_w_kv, gqa_q_norm=m_gqa_q_norm, gqa_k_norm=m_gqa_k_norm,
               gqa_w_o=m_gqa_w_o, ffn_norm=m_ffn_norm, ffn_w_in=m_ffn_w_in, ffn_w_out=m_ffn_w_out,
               final_norm=m_final_norm)
    var = dict(mix_norm_ab=v_mix_norm_ab, w_in_ab=v_w_in_ab, mla_q_norm=v_mla_q_norm, mla_kv_norm=v_mla_kv_norm,
               mla_w_uq=v_mla_w_uq, mla_w_ukv=v_mla_w_ukv, w_out_ab=v_w_out_ab, mix_norm_c=v_mix_norm_c,
               gqa_w_q=v_gqa_w_q, gqa_w_kv=v_gqa_w_kv, gqa_q_norm=v_gqa_q_norm, gqa_k_norm=v_gqa_k_norm,
               gqa_w_o=v_gqa_w_o, ffn_norm=v_ffn_norm, ffn_w_in=v_ffn_w_in, ffn_w_out=v_ffn_w_out,
               final_norm=v_final_norm)
    big_names = [n for n, _ in _BIG]
    shard_shapes = {n: wts[n].shape for n in big_names}
    xs = x[0]
    s = xs.shape[0]
    me = 4 * lax.axis_index("x") + 2 * lax.axis_index("y") + lax.axis_index("c")
    c_cols = mix_norm_c.shape[1]

    w_pack = _pack_big({n: wts[n] for n in big_names})
    n_rows = w_pack.shape[0]
    tail = jnp.pad(mix_norm_c.reshape(-1), (0, GATHER_TAIL_ROWS // 2 * PACK_COLS - mix_norm_c.size))
    tail = _to_bits16(tail.reshape(GATHER_TAIL_ROWS // 2, PACK_COLS))
    gathered = _all_gather(jnp.concatenate([w_pack.astype(jnp.bfloat16), tail], axis=0))
    full = _unpack_gathered(gathered[:, :n_rows], shard_shapes)
    c_all = _from_bits16(gathered[:, n_rows:]).reshape(N_DEV, -1)[:, :mix_norm_c.size]
    c_full = c_all.reshape(N_DEV, 2, c_cols).transpose(1, 0, 2).reshape(2, N_DEV * c_cols)
    gains = dict(mix_norm_ab=mix_norm_ab, mla_q_norm=mla_q_norm, mla_kv_norm=mla_kv_norm, mix_norm_c=c_full,
                 gqa_q_norm=gqa_q_norm, gqa_k_norm=gqa_k_norm, ffn_norm=ffn_norm)

    tabs = _rope_tables(s)
    slopes = jnp.exp2(-8.0 * jnp.arange(1, DIL_HEADS + 1, dtype=_F32) / DIL_HEADS)

    h = xs
    saved = []
    for layer in range(DEPTH):
        mix_w, ffn_w = _layer_weights(full, gains, layer)
        if layer % 2 == 0:
            h, sv_mix = _even_fwd(h, mix_w, tabs, slopes, f"l{layer}_mix")
        else:
            h, sv_mix = _odd_fwd(h, mix_w, tabs, f"l{layer}_mix")
        h, sv_ffn = _ffn_fwd(h, ffn_w, f"l{layer}_ffn")
        saved.append((mix_w, ffn_w, sv_mix, sv_ffn))
    loss_local, dh, d_final = _final_loss(h, final_norm, loss_target[0], name="final_loss")

    gfull = {n: [None] * wts[n].shape[0] for n in big_names}
    gsmall = {n: [None] * (wts[n].shape[0] if wts[n].ndim > 1 else 1) for n in _SMALL}
    gsmall["final_norm"][0] = d_final
    for layer in reversed(range(DEPTH)):
        mix_w, ffn_w, sv_mix, sv_ffn = saved[layer]
        i = layer // 2
        dh, gf = _ffn_bwd(dh, sv_ffn, ffn_w, f"l{layer}_ffn")
        gsmall["ffn_norm"][layer] = gf["norm"]
        gfull["ffn_w_in"][layer] = jnp.concatenate([gf["w_gate"], gf["w_up"]], axis=-1)
        gfull["ffn_w_out"][layer] = gf["w_out"]
        if layer % 2 == 0:
            dh, gm = _even_bwd(dh, sv_mix, mix_w, tabs, slopes, f"l{layer}_mix")
            gsmall["mix_norm_ab"][i] = gm["norm"]
            gsmall["mla_q_norm"][i] = gm["q_norm"]
            gsmall["mla_kv_norm"][i] = gm["kv_norm"]
            gfull["w_in_ab"][i] = jnp.concatenate([gm["w_a"][:, :IN_A], gm["w_b"]], axis=-1)
            gfull["mla_w_uq"][i] = _merge_heads_cols(gm["w_uq"], MLA_HEADS, MLA_NOPE)
            gfull["mla_w_ukv"][i] = _merge_heads_cols(gm["w_ukv"], MLA_HEADS, MLA_NOPE)
            gfull["w_out_ab"][i] = gm["w_out"]
        else:
            dh, gm = _odd_bwd(dh, sv_mix, mix_w, tabs, f"l{layer}_mix")
            gsmall["mix_norm_c"][i] = gm["norm"]
            gsmall["gqa_q_norm"][i] = gm["q_norm"]
            gsmall["gqa_k_norm"][i] = gm["k_norm"]
            gfull["gqa_w_q"][i] = gm["w_q"]
            gfull["gqa_w_kv"][i] = gm["w_kv"]
            gfull["gqa_w_o"][i] = gm["w_o"]
    grad_x = dh[None]

    g_parts = jnp.concatenate([_split_for_devices(jnp.stack(gfull[n]), axis) for n, axis in _BIG], axis=1)
    small_part = _pack_small({n: jnp.stack(gsmall[n]) for n in _SMALL})
    small_bits = jnp.broadcast_to(_to_bits16(small_part)[None], (N_DEV, EXCHANGE_TAIL_ROWS, PACK_COLS))
    received = _exchange(jnp.concatenate([g_parts.astype(jnp.bfloat16), small_bits], axis=1))
    g_big, d_big, m_big, v_big = _reduce_adamw(
        received, w_pack, _pack_big({n: mom[n] for n in big_names}), _pack_big({n: var[n] for n in big_names}),
        name="adamw_big")

    def widen_c(shard):
        return lax.dynamic_update_slice(jnp.zeros((2, N_DEV * c_cols), _F32), shard, (0, me * c_cols))

    def small_of(src):
        return _pack_small({n: (widen_c(src[n]) if n == "mix_norm_c" else src[n]) for n in _SMALL})

    small_recv = _from_bits16(received[:, n_rows:])
    g_sm, d_sm, m_sm, v_sm = _reduce_adamw(small_recv, small_of(wts), small_of(mom), small_of(var), name="adamw_small")

    def outputs_of(big_pack, small_pack):
        big = _unpack_big(big_pack, shard_shapes)
        small = _unpack_small(small_pack)
        res = {}
        for n in wts:
            if n in big:
                res[n] = big[n]
            elif n == "mix_norm_c":
                res[n] = lax.dynamic_slice(small[n].reshape(2, N_DEV * c_cols), (0, me * c_cols), (2, c_cols))
            else:
                res[n] = small[n].reshape(wts[n].shape)
        return [res[n] for n in wts]

    loss = lax.psum(loss_local, _AXES)
    return (loss, grad_x, *outputs_of(g_big, g_sm), *outputs_of(d_big, d_sm), *outputs_of(m_big, m_sm),
            *outputs_of(v_big, v_sm))
```

```python
import functools

import jax
import jax.numpy as jnp
from jax import lax
from jax.experimental import pallas as pl
from jax.experimental.pallas import tpu as pltpu

D_MODEL = 1024
DEPTH = 4
GRID_W = 64
NORM_EPS = 1e-6
ROPE_THETA = 10000.0
NEG_INF = -1e30
MLA_HEADS = 8
MLA_Q_RANK = 384
MLA_KV_RANK = 256
MLA_NOPE = 64
MLA_ROPE = 32
MLA_V = 64
DIL_PAIRS = ((128, 1), (512, 4), (2048, 16))
DIL_HALF = 64
DIL_SLOTS = 4
DIL_GROUPS = 3
DIL_HEADS = 12
DIL_HEAD_DIM = 64
GQA_HEADS = 16
GQA_KV_HEADS = 4
GQA_HEAD_DIM = 64
FFN_HIDDEN = 2816
IN_A = MLA_Q_RANK + MLA_KV_RANK + MLA_ROPE
IN_A_PAD = 768
IN_B = 3 * DIL_HEADS * DIL_HEAD_DIM
ADAM_LR = 0.001
ADAM_B1 = 0.9
ADAM_B2 = 0.999
ADAM_EPS = 1e-08
ADAM_WD = 0.01
ADAM_STEP = 10

LANES = 128
SUBLANES_16BIT = 16
VMEM_LIMIT_BYTES = 56 * 1024 * 1024

MM_TILE = 1408

N_DEV = 8
PACK_COLS = 1024

_MXU = jnp.bfloat16
_ACT = jnp.bfloat16
_F32 = jnp.float32

_AXES = ("x", "y", "c")


def _params(sem):
    return pltpu.CompilerParams(dimension_semantics=sem, vmem_limit_bytes=VMEM_LIMIT_BYTES)


def _pick(n, cap):
    for t in range(cap - cap % LANES, 0, -LANES):
        if n % t == 0:
            return t
    return n


def _rows(m, target):
    t = m
    while t > target and t % 2 == 0:
        t //= 2
    return t


def _matmul(a, b, *, trans_a=False, trans_b=False, res=None, scale=None, out_dtype=_F32, name):
    if trans_a:
        k, m = a.shape
    else:
        m, k = a.shape
    if trans_b:
        n, kb = b.shape
    else:
        kb, n = b.shape
    assert k == kb, (a.shape, b.shape)
    tm, tn, tk = _pick(m, MM_TILE), _pick(n, MM_TILE), _pick(k, MM_TILE)
    nk = k // tk
    dims = (((0 if trans_a else 1,), (1 if trans_b else 0,)), ((), ()))

    def body(*refs):
        if res is None:
            a_ref, b_ref, o_ref, acc = refs
            r_ref = None
        else:
            a_ref, b_ref, r_ref, o_ref, acc = refs
        kk = pl.program_id(2)

        @pl.when(kk == 0)
        def _():
            acc[...] = jnp.zeros_like(acc)

        acc[...] += lax.dot_general(a_ref[...].astype(_MXU), b_ref[...].astype(_MXU), dims,
                                    preferred_element_type=_F32)

        @pl.when(kk == nk - 1)
        def _():
            r = acc[...]
            if scale is not None:
                r = r * scale
            if r_ref is not None:
                r = r + r_ref[...].astype(_F32)
            o_ref[...] = r.astype(out_dtype)

    a_spec = (pl.BlockSpec((tk, tm), lambda i, j, kk: (kk, i)) if trans_a
              else pl.BlockSpec((tm, tk), lambda i, j, kk: (i, kk)))
    b_spec = (pl.BlockSpec((tn, tk), lambda i, j, kk: (j, kk)) if trans_b
              else pl.BlockSpec((tk, tn), lambda i, j, kk: (kk, j)))
    o_spec = pl.BlockSpec((tm, tn), lambda i, j, kk: (i, j))
    in_specs = [a_spec, b_spec] + ([o_spec] if res is not None else [])
    args = (a, b) + ((res,) if res is not None else ())
    return pl.pallas_call(
        body, name=name, grid=(m // tm, n // tn, nk),
        in_specs=in_specs, out_specs=o_spec,
        out_shape=jax.ShapeDtypeStruct((m, n), out_dtype),
        scratch_shapes=[pltpu.VMEM((tm, tn), _F32)],
        compiler_params=_params(("parallel", "parallel", "arbitrary")),
    )(*args)


def _rmsnorm(x, g, *, out_dtype, name, rows=512):
    m, d = x.shape
    tm = _rows(m, rows)

    def body(x_ref, g_ref, o_ref):
        xf = x_ref[...].astype(_F32)
        r = lax.rsqrt(jnp.mean(xf * xf, axis=-1, keepdims=True) + NORM_EPS)
        o_ref[...] = ((xf * r) * g_ref[...]).astype(out_dtype)

    return pl.pallas_call(
        body, name=name, grid=(m // tm,),
        in_specs=[pl.BlockSpec((tm, d), lambda i: (i, 0)), pl.BlockSpec((1, d), lambda i: (0, 0))],
        out_specs=pl.BlockSpec((tm, d), lambda i: (i, 0)),
        out_shape=jax.ShapeDtypeStruct((m, d), out_dtype),
        compiler_params=_params(("parallel",)),
    )(x, g.reshape(1, d).astype(_F32))


def _rmsnorm_bwd(x, g, dy, dres=None, *, out_dtype=_F32, name, rows=512):
    m, d = x.shape
    tm = _rows(m, rows)

    def body(*refs):
        if dres is None:
            x_ref, g_ref, dy_ref, dx_ref, dg_ref = refs
            r_ref = None
        else:
            x_ref, g_ref, dy_ref, r_ref, dx_ref, dg_ref = refs

        @pl.when(pl.program_id(0) == 0)
        def _():
            dg_ref[...] = jnp.zeros_like(dg_ref)

        xf = x_ref[...].astype(_F32)
        r = lax.rsqrt(jnp.mean(xf * xf, axis=-1, keepdims=True) + NORM_EPS)
        xh = xf * r
        dyf = dy_ref[...].astype(_F32)
        dg_ref[...] += jnp.sum(dyf * xh, axis=0, keepdims=True)
        gdy = dyf * g_ref[...]
        dx = r * (gdy - xh * jnp.mean(gdy * xh, axis=-1, keepdims=True))
        if r_ref is not None:
            dx = dx + r_ref[...].astype(_F32)
        dx_ref[...] = dx.astype(out_dtype)

    row = pl.BlockSpec((tm, d), lambda i: (i, 0))
    vec = pl.BlockSpec((1, d), lambda i: (0, 0))
    in_specs = [row, vec, row] + ([row] if dres is not None else [])
    args = (x, g.reshape(1, d).astype(_F32), dy) + ((dres,) if dres is not None else ())
    dx, dg = pl.pallas_call(
        body, name=name, grid=(m // tm,),
        in_specs=in_specs, out_specs=[row, vec],
        out_shape=[jax.ShapeDtypeStruct((m, d), out_dtype), jax.ShapeDtypeStruct((1, d), _F32)],
        compiler_params=_params(("arbitrary",)),
    )(*args)
    return dx, dg.reshape(d)


def _rotate(xf, c, sn):
    w = xf.shape[1]
    if w > LANES:
        c, sn = jnp.tile(c, (1, w // LANES)), jnp.tile(sn, (1, w // LANES))
    lane = lax.broadcasted_iota(jnp.int32, xf.shape, 1)
    sw = jnp.where((lane & 31) < 16, pltpu.roll(xf, w - 16, 1), pltpu.roll(xf, 16, 1))
    return xf * c + sw * sn


def _seg_mean(v, seg_ref):
    outs = []
    for c in range(v.shape[1] // LANES):
        piece = v[:, c * LANES:(c + 1) * LANES]
        hi = piece.astype(jnp.bfloat16)
        lo = (piece - hi.astype(_F32)).astype(jnp.bfloat16)
        outs.append(jnp.dot(hi, seg_ref[...], preferred_element_type=_F32)
                    + jnp.dot(lo, seg_ref[...], preferred_element_type=_F32))
    return jnp.concatenate(outs, axis=1) if len(outs) > 1 else outs[0]


def _seg_matrix():
    lane = jnp.arange(LANES) // GQA_HEAD_DIM
    return ((lane[:, None] == lane[None, :]).astype(_F32) / GQA_HEAD_DIM).astype(jnp.bfloat16)


def _headnorm_rope(x, gain, cos_t, sin_t, *, scale=None, name):
    s, w = x.shape
    ts = _rows(s, 512)

    def body(x_ref, g_ref, seg_ref, c_ref, s_ref, o_ref):
        xf = x_ref[...]
        r = lax.rsqrt(_seg_mean(xf * xf, seg_ref) + NORM_EPS)
        y = _rotate((xf * r) * g_ref[...], c_ref[...], s_ref[...])
        if scale is not None:
            y = y * scale
        o_ref[...] = y.astype(o_ref.dtype)

    row = pl.BlockSpec((ts, w), lambda i: (i, 0))
    tab = pl.BlockSpec((ts, LANES), lambda i: (i, 0))
    return pl.pallas_call(
        body, name=name, grid=(s // ts,),
        in_specs=[row, pl.BlockSpec((1, w), lambda i: (0, 0)), pl.BlockSpec((LANES, LANES), lambda i: (0, 0)), tab, tab],
        out_specs=row, out_shape=jax.ShapeDtypeStruct((s, w), _ACT),
        compiler_params=_params(("parallel",)),
    )(x, jnp.tile(gain.astype(_F32), w // GQA_HEAD_DIM).reshape(1, w), _seg_matrix(), cos_t, sin_t)


def _headnorm_rope_bwd(x, gain, dy, cos_t, sin_t, *, name):
    s, w = x.shape
    ts = _rows(s, 512)

    def body(x_ref, g_ref, seg_ref, c_ref, s_ref, dy_ref, dx_ref, dg_ref):
        @pl.when(pl.program_id(0) == 0)
        def _():
            dg_ref[...] = jnp.zeros_like(dg_ref)

        xf = x_ref[...]
        r = lax.rsqrt(_seg_mean(xf * xf, seg_ref) + NORM_EPS)
        xh = xf * r
        dyn = _rotate(dy_ref[...].astype(_F32), c_ref[...], -s_ref[...])
        dg_ref[...] += jnp.sum(dyn * xh, axis=0, keepdims=True)
        gdy = dyn * g_ref[...]
        dx_ref[...] = (r * (gdy - xh * _seg_mean(gdy * xh, seg_ref))).astype(dx_ref.dtype)

    row = pl.BlockSpec((ts, w), lambda i: (i, 0))
    vec = pl.BlockSpec((1, w), lambda i: (0, 0))
    tab = pl.BlockSpec((ts, LANES), lambda i: (i, 0))
    dx, dg = pl.pallas_call(
        body, name=name, grid=(s // ts,),
        in_specs=[row, vec, pl.BlockSpec((LANES, LANES), lambda i: (0, 0)), tab, tab, row],
        out_specs=[row, vec],
        out_shape=[jax.ShapeDtypeStruct((s, w), _ACT), jax.ShapeDtypeStruct((1, w), _F32)],
        compiler_params=_params(("arbitrary",)),
    )(x, jnp.tile(gain.astype(_F32), w // GQA_HEAD_DIM).reshape(1, w), _seg_matrix(), cos_t, sin_t, dy)
    return dx, dg.reshape(w // GQA_HEAD_DIM, GQA_HEAD_DIM).sum(axis=0)


def _rope(x, cos_t, sin_t, *, out_dtype, name, sum_chunks=False, scale=None):
    s, w = x.shape
    assert w % LANES == 0
    ts = _rows(s, 512)
    ow = LANES if sum_chunks else w

    def body(x_ref, c_ref, s_ref, o_ref):
        y = _rotate(x_ref[...].astype(_F32), c_ref[...], s_ref[...])
        if scale is not None:
            y = y * scale
        if sum_chunks:
            shift = w // 2
            while shift >= 32:
                y = y + pltpu.roll(y, shift, 1)
                shift //= 2
            y = y[:, :LANES]
        o_ref[...] = y.astype(out_dtype)

    return pl.pallas_call(
        body, name=name, grid=(s // ts,),
        in_specs=[pl.BlockSpec((ts, w), lambda i: (i, 0)), pl.BlockSpec((ts, LANES), lambda i: (i, 0)),
                  pl.BlockSpec((ts, LANES), lambda i: (i, 0))],
        out_specs=pl.BlockSpec((ts, ow), lambda i: (i, 0)),
        out_shape=jax.ShapeDtypeStruct((s, ow), out_dtype),
        compiler_params=_params(("parallel",)),
    )(x, cos_t, sin_t)


_NT = (((1,), (1,)), ((), ()))
_TN = (((0,), (0,)), ((), ()))
LOG2_E = 1.4426950408889634
LN_2 = 0.6931471805599453
ATTN_FWD_ROWS = 1024
ATTN_ROWS = 512
ATTN_CHAINS = 2
ATTN_FWD_KEYS = 2048
ATTN_BWD_KEYS = 1024


def _attn_fwd(q, k, v, *, out_dtype, name):
    h, s, dk = q.shape
    g, _, dv = v.shape
    r = h // g
    rc = min(ATTN_FWD_ROWS, s)
    hp = min(r, ATTN_CHAINS)
    nrc = max(1, min(ATTN_CHAINS // hp, s // rc))
    tq = rc * nrc
    tk = min(ATTN_FWD_KEYS, s)
    nhp, nq, nk = r // hp, s // tq, s // tk
    units = [(a, c) for a in range(hp) for c in range(nrc)]

    def body(q_ref, k_ref, v_ref, o_ref, lse_ref):
        qs = [q_ref[a, c * rc:(c + 1) * rc, :] for a, c in units]
        init = tuple((jnp.full((rc, 1), NEG_INF, _F32), jnp.zeros((rc, 1), _F32), jnp.zeros((rc, dv), _F32))
                     for _ in units)

        def trip(j, carry):
            rows = pl.ds(pl.multiple_of(j * tk, tk), tk)
            kb, vb = k_ref[0, rows, :], v_ref[0, rows, :]
            out = []
            for u in range(len(units)):
                m, l, acc = carry[u]
                sc = lax.dot_general(qs[u], kb, _NT, preferred_element_type=_F32)
                m_new = jnp.maximum(m, jnp.max(sc, axis=1, keepdims=True))
                p = jnp.exp2(sc - m_new)
                alpha = jnp.exp2(m - m_new)
                l = alpha * l + jnp.sum(p, axis=1, keepdims=True)
                acc = alpha * acc + jnp.dot(p.astype(_MXU), vb, preferred_element_type=_F32)
                out.append((m_new, l, acc))
            return tuple(out)

        fin = lax.fori_loop(0, nk, trip, init)
        for u, (a, c) in enumerate(units):
            m, l, acc = fin[u]
            o_ref[a, c * rc:(c + 1) * rc, :] = (acc / l).astype(out_dtype)
            lse_ref[a, c * rc:(c + 1) * rc, :] = m + jnp.log2(l)

    q_blk = lambda gg, hh, i: (gg * nhp + hh, i, 0)
    kv_blk = lambda gg, hh, i: (gg, 0, 0)
    return pl.pallas_call(
        body, name=name, grid=(g, nhp, nq),
        in_specs=[pl.BlockSpec((hp, tq, dk), q_blk), pl.BlockSpec((1, s, dk), kv_blk),
                  pl.BlockSpec((1, s, dv), kv_blk)],
        out_specs=[pl.BlockSpec((hp, tq, dv), q_blk), pl.BlockSpec((hp, tq, 1), q_blk)],
        out_shape=[jax.ShapeDtypeStruct((h, s, dv), out_dtype), jax.ShapeDtypeStruct((h, s, 1), _F32)],
        compiler_params=_params(("parallel", "parallel", "parallel")),
    )(q, k, v)


def _attn_bwd(q, k, v, o, do, lse2, *, scale, name):
    h, s, dk = q.shape
    g, _, dv = v.shape
    r = h // g
    hp = min(r, ATTN_CHAINS)
    tq = min(ATTN_ROWS, s)
    tk = min(ATTN_BWD_KEYS * (ATTN_CHAINS // hp), s)
    nhp, nq, nk = r // hp, s // tq, s // tk

    def body(q_ref, k_ref, v_ref, o_ref, do_ref, lse_ref, dq_ref, dk_ref, dv_ref):
        hh, i = pl.program_id(1), pl.program_id(2)

        @pl.when((hh == 0) & (i == 0))
        def _():
            dk_ref[...] = jnp.zeros_like(dk_ref)
            dv_ref[...] = jnp.zeros_like(dv_ref)

        qs = [q_ref[a] for a in range(hp)]
        dos = [do_ref[a] for a in range(hp)]
        qts = [q_ref[a].T for a in range(hp)]
        dots = [do_ref[a].T for a in range(hp)]
        ls = [lse_ref[a] for a in range(hp)]
        dls = [jnp.sum(do_ref[a].astype(_F32) * o_ref[a].astype(_F32), axis=1, keepdims=True) for a in range(hp)]

        def trip(j, carry):
            rows = pl.ds(pl.multiple_of(j * tk, tk), tk)
            kb, vb = k_ref[0, rows, :], v_ref[0, rows, :]
            out = []
            for a in range(hp):
                p = jnp.exp2(lax.dot_general(qs[a], kb, _NT, preferred_element_type=_F32) - ls[a])
                dp = lax.dot_general(dos[a], vb, _NT, preferred_element_type=_F32)
                ds = (p * (dp - dls[a])).astype(_MXU)
                dv_ref[0, j] += jnp.dot(dots[a], p.astype(_MXU), preferred_element_type=_F32)
                dk_ref[0, j] += jnp.dot(qts[a], ds, preferred_element_type=_F32)
                out.append(carry[a] + jnp.dot(ds, kb, preferred_element_type=_F32))
            return tuple(out)

        fin = lax.fori_loop(0, nk, trip, tuple(jnp.zeros((tq, dk), _F32) for _ in range(hp)))
        for a in range(hp):
            dq_ref[a] = fin[a] * scale

        @pl.when((hh == nhp - 1) & (i == nq - 1))
        def _():
            dk_ref[...] = dk_ref[...] * LN_2

    q_blk = lambda gg, hh, i: (gg * nhp + hh, i, 0)
    kv_blk = lambda gg, hh, i: (gg, 0, 0)
    return pl.pallas_call(
        body, name=name, grid=(g, nhp, nq),
        in_specs=[pl.BlockSpec((hp, tq, dk), q_blk), pl.BlockSpec((1, s, dk), kv_blk), pl.BlockSpec((1, s, dv), kv_blk),
                  pl.BlockSpec((hp, tq, dv), q_blk), pl.BlockSpec((hp, tq, dv), q_blk), pl.BlockSpec((hp, tq, 1), q_blk)],
        out_specs=[pl.BlockSpec((hp, tq, dk), q_blk), pl.BlockSpec((1, nk, dk, tk), lambda gg, hh, i: (gg, 0, 0, 0)),
                   pl.BlockSpec((1, nk, dv, tk), lambda gg, hh, i: (gg, 0, 0, 0))],
        out_shape=[jax.ShapeDtypeStruct((h, s, dk), _F32), jax.ShapeDtypeStruct((g, nk, dk, tk), _F32),
                   jax.ShapeDtypeStruct((g, nk, dv, tk), _F32)],
        compiler_params=_params(("arbitrary", "arbitrary", "arbitrary")),
    )(q, k, v, o, do, lse2)


def _unchunk(xt):
    g, nk, d, tk = xt.shape
    return xt.transpose(1, 3, 0, 2).reshape(nk * tk, g * d)


WIN_ROWS = 512


def _win_geometry(s, dil):
    t = min(WIN_ROWS, s)
    length = s // dil
    lg = length.bit_length() - 1
    assert 1 << lg == length and t % DIL_HALF == 0 and s % t == 0
    return t, t + 2 * DIL_HALF, lg, t // DIL_HALF, s // DIL_HALF


def _win_specs(hn, t, d, halo_per_blk, n_halo):
    return [pl.BlockSpec((hn, DIL_HALF, d), lambda i: (0, jnp.maximum(i * halo_per_blk - 1, 0), 0)),
            pl.BlockSpec((hn, t, d), lambda i: (0, i, 0)),
            pl.BlockSpec((hn, DIL_HALF, d), lambda i: (0, jnp.minimum((i + 1) * halo_per_blk, n_halo - 1), 0))]


def _win_mask(i, t, w, lg, dil, wide_rows):
    shape = (w, t) if wide_rows else (t, w)
    rows = lax.broadcasted_iota(jnp.int32, shape, 0)
    cols = lax.broadcasted_iota(jnp.int32, shape, 1)
    base = i * t
    if wide_rows:
        pq, pk = base - DIL_HALF + rows, base + cols
    else:
        pq, pk = base + rows, base - DIL_HALF + cols
    arel = jnp.abs(pk - pq)
    valid = (arel <= DIL_HALF) & ((pk >> lg) == (pq >> lg))
    return valid, (-dil * arel).astype(_F32)


def _win3(lo_ref, mid_ref, hi_ref, a):
    return jnp.concatenate([lo_ref[a], mid_ref[a], hi_ref[a]], axis=0)


def _win_fwd(q, k, v, slopes, *, dil, name):
    hn, s, d = q.shape
    t, w, lg, hpb, n_halo = _win_geometry(s, dil)
    scale = d ** -0.5

    def body(sl_ref, q_ref, klo, kmid, khi, vlo, vmid, vhi, o_ref, lse_ref):
        valid, nb = _win_mask(pl.program_id(0), t, w, lg, dil, False)
        for a in range(hn):
            kw, vw = _win3(klo, kmid, khi, a), _win3(vlo, vmid, vhi, a)
            sc = lax.dot_general(q_ref[a], kw, _NT, preferred_element_type=_F32) * scale
            sc = jnp.where(valid, sc + sl_ref[a] * nb, NEG_INF)
            m = jnp.max(sc, axis=1, keepdims=True)
            e = jnp.exp(sc - m)
            den = jnp.sum(e, axis=1, keepdims=True)
            o_ref[a] = jnp.dot(e.astype(_MXU), vw, preferred_element_type=_F32) / den
            lse_ref[a] = m + jnp.log(den)

    kv = _win_specs(hn, t, d, hpb, n_halo)
    blk = lambda c: pl.BlockSpec((hn, t, c), lambda i: (0, i, 0))
    return pl.pallas_call(
        body, name=name, grid=(s // t,),
        in_specs=[pl.BlockSpec(memory_space=pltpu.SMEM), blk(d)] + kv + kv,
        out_specs=[blk(d), blk(1)],
        out_shape=[jax.ShapeDtypeStruct((hn, s, d), _F32), jax.ShapeDtypeStruct((hn, s, 1), _F32)],
        compiler_params=_params(("parallel",)),
    )(slopes.astype(_F32), q, k, k, k, v, v, v)


def _win_bwd_dq(q, k, v, do, lse, delta, slopes, *, dil, name):
    hn, s, d = q.shape
    t, w, lg, hpb, n_halo = _win_geometry(s, dil)
    scale = d ** -0.5

    def body(sl_ref, q_ref, klo, kmid, khi, vlo, vmid, vhi, do_ref, lse_ref, dl_ref, dq_ref):
        valid, nb = _win_mask(pl.program_id(0), t, w, lg, dil, False)
        for a in range(hn):
            kw, vw = _win3(klo, kmid, khi, a), _win3(vlo, vmid, vhi, a)
            sc = lax.dot_general(q_ref[a], kw, _NT, preferred_element_type=_F32) * scale
            p = jnp.exp(jnp.where(valid, sc + sl_ref[a] * nb, NEG_INF) - lse_ref[a])
            dp = lax.dot_general(do_ref[a], vw, _NT, preferred_element_type=_F32)
            ds = (p * (dp - dl_ref[a])).astype(_MXU)
            dq_ref[a] = jnp.dot(ds, kw, preferred_element_type=_F32) * scale

    kv = _win_specs(hn, t, d, hpb, n_halo)
    blk = lambda c: pl.BlockSpec((hn, t, c), lambda i: (0, i, 0))
    return pl.pallas_call(
        body, name=name, grid=(s // t,),
        in_specs=[pl.BlockSpec(memory_space=pltpu.SMEM), blk(d)] + kv + kv + [blk(d), blk(1), blk(1)],
        out_specs=blk(d),
        out_shape=jax.ShapeDtypeStruct((hn, s, d), _F32),
        compiler_params=_params(("parallel",)),
    )(slopes.astype(_F32), q, k, k, k, v, v, v, do, lse, delta)


def _win_bwd_dkv(q, k, v, do, lse, delta, slopes, *, dil, name):
    hn, s, d = q.shape
    t, w, lg, hpb, n_halo = _win_geometry(s, dil)
    scale = d ** -0.5

    def body(sl_ref, qlo, qmid, qhi, dolo, domid, dohi, llo, lmid, lhi, dllo, dlmid, dlhi, k_ref, v_ref,
             dk_ref, dv_ref):
        valid, nb = _win_mask(pl.program_id(0), t, w, lg, dil, True)
        for a in range(hn):
            qw, dow = _win3(qlo, qmid, qhi, a), _win3(dolo, domid, dohi, a)
            lw, dlw = _win3(llo, lmid, lhi, a), _win3(dllo, dlmid, dlhi, a)
            sc = lax.dot_general(qw, k_ref[a], _NT, preferred_element_type=_F32) * scale
            p = jnp.exp(jnp.where(valid, sc + sl_ref[a] * nb, NEG_INF) - lw)
            dp = lax.dot_general(dow, v_ref[a], _NT, preferred_element_type=_F32)
            ds = (p * (dp - dlw)).astype(_MXU)
            dv_ref[a] = lax.dot_general(p.astype(_MXU), dow, _TN, preferred_element_type=_F32)
            dk_ref[a] = lax.dot_general(ds, qw, _TN, preferred_element_type=_F32) * scale

    blk = lambda c: pl.BlockSpec((hn, t, c), lambda i: (0, i, 0))
    return pl.pallas_call(
        body, name=name, grid=(s // t,),
        in_specs=([pl.BlockSpec(memory_space=pltpu.SMEM)] + _win_specs(hn, t, d, hpb, n_halo)
                  + _win_specs(hn, t, d, hpb, n_halo) + _win_specs(hn, t, 1, hpb, n_halo)
                  + _win_specs(hn, t, 1, hpb, n_halo) + [blk(d), blk(d)]),
        out_specs=[blk(d), blk(d)],
        out_shape=[jax.ShapeDtypeStruct((hn, s, d), _F32), jax.ShapeDtypeStruct((hn, s, d), _F32)],
        compiler_params=_params(("parallel",)),
    )(slopes.astype(_F32), q, q, q, do, do, do, lse, lse, lse, delta, delta, delta, k, v)


def _merge_weights(lse):
    mx = jnp.max(lse, axis=0, keepdims=True)
    e = jnp.exp(lse - mx)
    return e / jnp.sum(e, axis=0, keepdims=True)


def _merge_fwd(o3, lse3, *, name):
    ng, sl, s, d = o3.shape
    t = _rows(s, 512)

    def body(o_ref, l_ref, c_ref):
        wts = _merge_weights(l_ref[:, 0])
        c_ref[0] = jnp.sum(wts * o_ref[:, 0], axis=0).astype(c_ref.dtype)

    return pl.pallas_call(
        body, name=name, grid=(sl, s // t),
        in_specs=[pl.BlockSpec((ng, 1, t, d), lambda a, i: (0, a, i, 0)),
                  pl.BlockSpec((ng, 1, t, 1), lambda a, i: (0, a, i, 0))],
        out_specs=pl.BlockSpec((1, t, d), lambda a, i: (a, i, 0)),
        out_shape=jax.ShapeDtypeStruct((sl, s, d), _ACT),
        compiler_params=_params(("parallel", "parallel")),
    )(o3, lse3)


def _merge_bwd(dcomb, o3, lse3, *, name):
    ng, sl, s, d = o3.shape
    t = _rows(s, 512)

    def body(dc_ref, o_ref, l_ref, do_ref, dl_ref):
        wts = _merge_weights(l_ref[:, 0])
        dc = dc_ref[0].astype(_F32)
        comb = jnp.sum(wts * o_ref[:, 0], axis=0)
        do_ref[:, 0] = (wts * dc[None]).astype(do_ref.dtype)
        dl_ref[:, 0] = wts * jnp.sum(dc * comb, axis=-1, keepdims=True)[None]

    big = pl.BlockSpec((ng, 1, t, d), lambda a, i: (0, a, i, 0))
    small = pl.BlockSpec((ng, 1, t, 1), lambda a, i: (0, a, i, 0))
    return pl.pallas_call(
        body, name=name, grid=(sl, s // t),
        in_specs=[pl.BlockSpec((1, t, d), lambda a, i: (a, i, 0)), big, small],
        out_specs=[big, small],
        out_shape=[jax.ShapeDtypeStruct((ng, sl, s, d), _ACT), jax.ShapeDtypeStruct((ng, sl, s, 1), _F32)],
        compiler_params=_params(("parallel", "parallel")),
    )(dcomb, o3, lse3)


SWIGLU_ROWS = 512


def _gate_up(a, w_gate, w_up, *, name):
    m, k = a.shape
    n = w_gate.shape[1]
    tm, tn, tk = _pick(m, SWIGLU_ROWS), _pick(n, MM_TILE), _pick(k, MM_TILE)
    nk = k // tk

    def body(a_ref, g_ref, u_ref, go_ref, uo_ref, act_ref, acc_g, acc_u):
        kk = pl.program_id(2)

        @pl.when(kk == 0)
        def _():
            acc_g[...] = jnp.zeros_like(acc_g)
            acc_u[...] = jnp.zeros_like(acc_u)

        av = a_ref[...].astype(_MXU)
        acc_g[...] += jnp.dot(av, g_ref[...].astype(_MXU), preferred_element_type=_F32)
        acc_u[...] += jnp.dot(av, u_ref[...].astype(_MXU), preferred_element_type=_F32)

        @pl.when(kk == nk - 1)
        def _():
            gf, uf = acc_g[...], acc_u[...]
            go_ref[...] = gf.astype(go_ref.dtype)
            uo_ref[...] = uf.astype(uo_ref.dtype)
            act_ref[...] = (gf * jax.nn.sigmoid(gf) * uf).astype(act_ref.dtype)

    w_spec = pl.BlockSpec((tk, tn), lambda i, j, kk: (kk, j))
    o_spec = pl.BlockSpec((tm, tn), lambda i, j, kk: (i, j))
    out = jax.ShapeDtypeStruct((m, n), _ACT)
    return pl.pallas_call(
        body, name=name, grid=(m // tm, n // tn, nk),
        in_specs=[pl.BlockSpec((tm, tk), lambda i, j, kk: (i, kk)), w_spec, w_spec],
        out_specs=[o_spec, o_spec, o_spec], out_shape=[out, out, out],
        scratch_shapes=[pltpu.VMEM((tm, tn), _F32), pltpu.VMEM((tm, tn), _F32)],
        compiler_params=_params(("parallel", "parallel", "arbitrary")),
    )(a, w_gate, w_up)


def _gate_up_bwd(dout, w_out, gate, up, *, name):
    m, k = dout.shape
    n = w_out.shape[0]
    tm, tn, tk = _pick(m, SWIGLU_ROWS), _pick(n, MM_TILE), _pick(k, MM_TILE)
    nk = k // tk

    def body(d_ref, w_ref, g_ref, u_ref, dg_ref, du_ref, acc):
        kk = pl.program_id(2)

        @pl.when(kk == 0)
        def _():
            acc[...] = jnp.zeros_like(acc)

        acc[...] += lax.dot_general(d_ref[...].astype(_MXU), w_ref[...].astype(_MXU), _NT,
                                    preferred_element_type=_F32)

        @pl.when(kk == nk - 1)
        def _():
            daf = acc[...]
            gf = g_ref[...].astype(_F32)
            sg = jax.nn.sigmoid(gf)
            dg_ref[...] = (daf * u_ref[...].astype(_F32) * (sg + gf * sg * (1.0 - sg))).astype(dg_ref.dtype)
            du_ref[...] = (daf * (gf * sg)).astype(du_ref.dtype)

    o_spec = pl.BlockSpec((tm, tn), lambda i, j, kk: (i, j))
    out = jax.ShapeDtypeStruct((m, n), _ACT)
    return pl.pallas_call(
        body, name=name, grid=(m // tm, n // tn, nk),
        in_specs=[pl.BlockSpec((tm, tk), lambda i, j, kk: (i, kk)), pl.BlockSpec((tn, tk), lambda i, j, kk: (j, kk)),
                  o_spec, o_spec],
        out_specs=[o_spec, o_spec], out_shape=[out, out],
        scratch_shapes=[pltpu.VMEM((tm, tn), _F32)],
        compiler_params=_params(("parallel", "parallel", "arbitrary")),
    )(dout, w_out, gate, up)


def _final_loss(x, g, target, *, name):
    m, d = x.shape
    tm = _rows(m, 512)

    def body(x_ref, g_ref, t_ref, loss_ref, dx_ref, dg_ref):
        @pl.when(pl.program_id(0) == 0)
        def _():
            loss_ref[...] = jnp.zeros_like(loss_ref)
            dg_ref[...] = jnp.zeros_like(dg_ref)

        xf = x_ref[...]
        r = lax.rsqrt(jnp.mean(xf * xf, axis=-1, keepdims=True) + NORM_EPS)
        xh = xf * r
        err = xh * g_ref[...] - t_ref[...]
        loss_ref[...] += 0.5 * jnp.sum(jnp.mean(err * err, axis=-1, keepdims=True))
        dy = err * (1.0 / d)
        dg_ref[...] += jnp.sum(dy * xh, axis=0, keepdims=True)
        gdy = dy * g_ref[...]
        dx_ref[...] = r * (gdy - xh * jnp.mean(gdy * xh, axis=-1, keepdims=True))

    row = pl.BlockSpec((tm, d), lambda i: (i, 0))
    vec = pl.BlockSpec((1, d), lambda i: (0, 0))
    loss, dx, dg = pl.pallas_call(
        body, name=name, grid=(m // tm,),
        in_specs=[row, vec, row],
        out_specs=[pl.BlockSpec((8, LANES), lambda i: (0, 0)), row, vec],
        out_shape=[jax.ShapeDtypeStruct((8, LANES), _F32), jax.ShapeDtypeStruct((m, d), _F32),
                   jax.ShapeDtypeStruct((1, d), _F32)],
        compiler_params=_params(("arbitrary",)),
    )(x, g.reshape(1, d), target)
    return loss[0, 0], dx, dg.reshape(d)


def _reduce_adamw(parts, w, m, v, *, name):
    rws, cols = w.shape
    tr = rws
    for cand in range(min(rws, 256), 0, -SUBLANES_16BIT):
        if cand % SUBLANES_16BIT == 0 and rws % cand == 0:
            tr = cand
            break

    def body(p_ref, w_ref, m_ref, v_ref, g_ref, d_ref, nm_ref, nv_ref):
        gsum = p_ref[0].astype(_F32)
        for dev in range(1, N_DEV):
            gsum = gsum + p_ref[dev].astype(_F32)
        m2 = ADAM_B1 * m_ref[...] + (1.0 - ADAM_B1) * gsum
        v2 = ADAM_B2 * v_ref[...] + (1.0 - ADAM_B2) * (gsum * gsum)
        m_hat = m2 / (1.0 - ADAM_B1 ** ADAM_STEP)
        v_hat = v2 / (1.0 - ADAM_B2 ** ADAM_STEP)
        g_ref[...] = gsum
        d_ref[...] = -ADAM_LR * (m_hat / (jnp.sqrt(v_hat) + ADAM_EPS) + ADAM_WD * w_ref[...])
        nm_ref[...] = m2
        nv_ref[...] = v2

    blk = pl.BlockSpec((tr, cols), lambda i: (i, 0))
    out = jax.ShapeDtypeStruct((rws, cols), _F32)
    return pl.pallas_call(
        body, name=name, grid=(rws // tr,),
        in_specs=[pl.BlockSpec((N_DEV, tr, cols), lambda i: (0, i, 0)), blk, blk, blk],
        out_specs=[blk, blk, blk, blk], out_shape=[out, out, out, out],
        compiler_params=_params(("parallel",)),
    )(parts, w, m, v)


def _mesh_pos():
    return lax.axis_index("x"), lax.axis_index("y"), lax.axis_index("c")


def _all_gather(block):
    rws, cols = block.shape

    def body(x_ref, out_ref, send_sems, recv_sems, local_sem):
        x, y, c = _mesh_pos()
        me, sibling = (x, y, c), (x, y, 1 - c)
        chips = [(1 - x, y), (x, 1 - y), (1 - x, 1 - y)]

        def slot(px, py, pc):
            return out_ref.at[4 * px + 2 * py + pc]

        def copy(k, blk, to, src=None):
            return pltpu.make_async_remote_copy(
                src_ref=slot(*blk) if src is None else src, dst_ref=slot(*blk),
                send_sem=send_sems.at[k], recv_sem=recv_sems.at[k],
                device_id=to, device_id_type=pl.DeviceIdType.MESH)

        mine = pltpu.make_async_copy(x_ref, slot(*me), local_sem)
        mine.start()
        first = [copy(0, me, sibling, src=x_ref)]
        first += [copy(1 + j, me, (*chip, c), src=x_ref) for j, chip in enumerate(chips)]
        for cp in first:
            cp.start()
        passed = [copy(4 + j, (*chip, c), sibling) for j, chip in enumerate(chips)]
        for j, chip in enumerate(chips):
            copy(1 + j, (*chip, c), me).wait_recv()
            passed[j].start()
        copy(0, sibling, me).wait_recv()
        for j, chip in enumerate(chips):
            copy(4 + j, (*chip, 1 - c), me).wait_recv()
        for cp in first + passed:
            cp.wait_send()
        mine.wait()

    return pl.pallas_call(
        body, name="weights_all_gather",
        out_shape=jax.ShapeDtypeStruct((N_DEV, rws, cols), block.dtype),
        in_specs=[pl.BlockSpec(memory_space=pl.ANY)],
        out_specs=pl.BlockSpec(memory_space=pl.ANY),
        scratch_shapes=[pltpu.SemaphoreType.DMA((7,)), pltpu.SemaphoreType.DMA((7,)), pltpu.SemaphoreType.DMA],
    )(block)


def _exchange(parts):
    _, rws, cols = parts.shape

    def body(g_ref, out_ref, send_sems, recv_sems, local_sem):
        x, y, c = _mesh_pos()
        me = 4 * x + 2 * y + c
        mine = pltpu.make_async_copy(g_ref.at[me], out_ref.at[me], local_sem)
        mine.start()
        copies = []
        for k in range(1, N_DEV):
            px = 1 - x if k & 4 else x
            py = 1 - y if k & 2 else y
            pc = 1 - c if k & 1 else c
            peer = 4 * px + 2 * py + pc
            send = pltpu.make_async_remote_copy(
                src_ref=g_ref.at[peer], dst_ref=out_ref.at[me],
                send_sem=send_sems.at[k - 1], recv_sem=recv_sems.at[k - 1],
                device_id=(px, py, pc), device_id_type=pl.DeviceIdType.MESH)
            send.start()
            arrival = pltpu.make_async_remote_copy(
                src_ref=g_ref.at[peer], dst_ref=out_ref.at[peer],
                send_sem=send_sems.at[k - 1], recv_sem=recv_sems.at[k - 1],
                device_id=(px, py, pc), device_id_type=pl.DeviceIdType.MESH)
            copies.append((send, arrival))
        for _, arrival in copies:
            arrival.wait_recv()
        for send, _ in copies:
            send.wait_send()
        mine.wait()

    return pl.pallas_call(
        body, name="grads_exchange",
        out_shape=jax.ShapeDtypeStruct(parts.shape, parts.dtype),
        in_specs=[pl.BlockSpec(memory_space=pl.ANY)],
        out_specs=pl.BlockSpec(memory_space=pl.ANY),
        scratch_shapes=[pltpu.SemaphoreType.DMA((7,)), pltpu.SemaphoreType.DMA((7,)), pltpu.SemaphoreType.DMA],
    )(parts)


_BIG = (("w_in_ab", 2), ("mla_w_uq", 1), ("mla_w_ukv", 1), ("w_out_ab", 2), ("gqa_w_q", 1), ("gqa_w_kv", 1),
        ("gqa_w_o", 1), ("ffn_w_in", 2), ("ffn_w_out", 1))
_SMALL = ("mix_norm_ab", "ffn_norm", "final_norm", "mix_norm_c", "mla_q_norm", "mla_kv_norm", "gqa_q_norm", "gqa_k_norm")
SMALL_ROWS = 16
GATHER_TAIL_ROWS = 16
EXCHANGE_TAIL_ROWS = 32


def _view3(a):
    return a.reshape(a.shape[0], a.shape[1], -1)


def _pad_rows(a2d):
    pad = -a2d.shape[0] % SUBLANES_16BIT
    return jnp.pad(a2d, ((0, pad), (0, 0))) if pad else a2d


def _pack_rows(shard):
    return _pad_rows(shard.reshape(-1, PACK_COLS))


def _packed_rows(shape):
    n = 1
    for d in shape:
        n *= d
    rows = n // PACK_COLS
    return rows + (-rows % SUBLANES_16BIT)


def _pack_big(shards):
    return jnp.concatenate([_pack_rows(shards[n]) for n, _ in _BIG], axis=0)


def _unpack_big(packed, shapes):
    out, off = {}, 0
    for n, _ in _BIG:
        size = 1
        for d in shapes[n]:
            size *= d
        out[n] = packed[off:off + size // PACK_COLS].reshape(shapes[n])
        off += _packed_rows(shapes[n])
    return out


def _unpack_gathered(gathered, shapes):
    out, off = {}, 0
    for n, axis in _BIG:
        size = 1
        for d in shapes[n]:
            size *= d
        l3 = (shapes[n][0], shapes[n][1], size // (shapes[n][0] * shapes[n][1]))
        sh = gathered[:, off:off + size // PACK_COLS].reshape((N_DEV,) + l3)
        if axis == 1:
            full = sh.transpose(1, 0, 2, 3).reshape(l3[0], N_DEV * l3[1], l3[2])
        else:
            full = sh.transpose(1, 2, 0, 3).reshape(l3[0], l3[1], N_DEV * l3[2])
        out[n] = full
        off += _packed_rows(shapes[n])
    return out


def _split_for_devices(full, axis):
    l, rws, cols = full.shape
    if axis == 1:
        sh = full.reshape(l, N_DEV, rws // N_DEV, cols).transpose(1, 0, 2, 3)
    else:
        sh = full.reshape(l, rws, N_DEV, cols // N_DEV).transpose(2, 0, 1, 3)
    flat = sh.reshape(N_DEV, -1, PACK_COLS)
    pad = -flat.shape[1] % SUBLANES_16BIT
    return jnp.pad(flat, ((0, 0), (0, pad), (0, 0))) if pad else flat


def _to_bits16(a_f32_rows):
    r = a_f32_rows.shape[0]
    return lax.bitcast_convert_type(a_f32_rows, jnp.bfloat16).reshape(2 * r, PACK_COLS)


def _from_bits16(a_bf16_rows):
    lead, r = a_bf16_rows.shape[:-2], a_bf16_rows.shape[-2]
    return lax.bitcast_convert_type(a_bf16_rows.reshape(lead + (r // 2, PACK_COLS, 2)), _F32)


def _small_sizes():
    return {"mix_norm_ab": 2 * D_MODEL, "ffn_norm": DEPTH * D_MODEL, "final_norm": D_MODEL, "mix_norm_c": 2 * D_MODEL,
            "mla_q_norm": 2 * MLA_Q_RANK, "mla_kv_norm": 2 * MLA_KV_RANK, "gqa_q_norm": 2 * GQA_HEAD_DIM,
            "gqa_k_norm": 2 * GQA_HEAD_DIM}


def _pack_small(vals):
    flat = jnp.concatenate([vals[n].reshape(-1).astype(_F32) for n in _SMALL])
    return jnp.pad(flat, (0, SMALL_ROWS * PACK_COLS - flat.shape[0])).reshape(SMALL_ROWS, PACK_COLS)


def _unpack_small(pack):
    flat, out, off = pack.reshape(-1), {}, 0
    sizes = _small_sizes()
    for n in _SMALL:
        out[n] = flat[off:off + sizes[n]]
        off += sizes[n]
    return out


def _angles(pos, dim):
    freqs = ROPE_THETA ** (-jnp.arange(0, dim, 2, dtype=_F32) / dim)
    ang = pos.astype(_F32)[:, None] * freqs[None, :]
    return jnp.cos(ang), jnp.sin(ang)


def _rope_tables(s):
    pos = jnp.arange(s)
    cos_t, sin_t = _angles(pos, MLA_ROPE)
    mla_c = jnp.tile(jnp.concatenate([cos_t, cos_t], -1), (1, LANES // 32))
    mla_s = jnp.tile(jnp.concatenate([-sin_t, sin_t], -1), (1, LANES // 32))
    rows = s // GRID_W
    row_idx = jnp.broadcast_to(jnp.arange(rows)[:, None], (rows, GRID_W)).reshape(-1)
    col_idx = jnp.broadcast_to(jnp.arange(GRID_W)[None, :], (rows, GRID_W)).reshape(-1)
    cos_r, sin_r = _angles(row_idx, GQA_HEAD_DIM // 2)
    cos_c, sin_c = _angles(col_idx, GQA_HEAD_DIM // 2)
    gqa_c = jnp.tile(jnp.concatenate([cos_r, cos_r, cos_c, cos_c], -1), (1, LANES // GQA_HEAD_DIM))
    gqa_s = jnp.tile(jnp.concatenate([-sin_r, sin_r, -sin_c, sin_c], -1), (1, LANES // GQA_HEAD_DIM))
    return (mla_c, mla_s), (gqa_c, gqa_s)


def _heads(x2d, h):
    s = x2d.shape[0]
    return x2d.reshape(s, h, -1).transpose(1, 0, 2)


def _unheads(xh):
    h, s, d = xh.shape
    return xh.transpose(1, 0, 2).reshape(s, h * d)


def _to_res(x2d, dil):
    s = x2d.shape[0]
    return x2d.reshape(s // dil, dil, DIL_SLOTS, -1).transpose(2, 1, 0, 3).reshape(DIL_SLOTS, s, -1)


def _from_res(xh, dil):
    sl, s, d = xh.shape
    return xh.reshape(sl, dil, s // dil, d).transpose(2, 1, 0, 3).reshape(s, sl * d)


def _res_to_tok(xh, dil):
    sl, s, c = xh.shape
    return xh.reshape(sl, dil, s // dil, c).transpose(0, 2, 1, 3).reshape(sl, s, c)


def _tok_to_res(xh, dil):
    sl, s, c = xh.shape
    return xh.reshape(sl, s // dil, dil, c).transpose(0, 2, 1, 3).reshape(sl, s, c)


def _ffn_fwd(x, w, tag):
    hn = _rmsnorm(x, w["norm"], out_dtype=_ACT, name=f"{tag}_norm")
    gate, up, act = _gate_up(hn, w["w_gate"], w["w_up"], name=f"{tag}_gate_up")
    out = _matmul(act, w["w_out"], res=x, name=f"{tag}_out")
    return out, (x, hn, gate, up, act)


def _ffn_bwd(dout, saved, w, tag):
    x, hn, gate, up, act = saved
    d_w_out = _matmul(act, dout, trans_a=True, name=f"{tag}_dwout")
    dgate, dup = _gate_up_bwd(dout, w["w_out"], gate, up, name=f"{tag}_dgate_up")
    d_w_gate = _matmul(hn, dgate, trans_a=True, name=f"{tag}_dwgate")
    d_w_up = _matmul(hn, dup, trans_a=True, name=f"{tag}_dwup")
    dhn = _matmul(dgate, w["w_gate"], trans_b=True, name=f"{tag}_dhn_gate")
    dhn = _matmul(dup, w["w_up"], trans_b=True, res=dhn, name=f"{tag}_dhn_up")
    dx, dnorm = _rmsnorm_bwd(x, w["norm"], dhn, dout, name=f"{tag}_dnorm")
    return dx, {"norm": dnorm, "w_gate": d_w_gate, "w_up": d_w_up, "w_out": d_w_out}


def _even_fwd(x, w, tabs, slopes, tag):
    s = x.shape[0]
    (mla_c, mla_s), _ = tabs
    hn = _rmsnorm(x, w["norm"], out_dtype=_ACT, name=f"{tag}_norm")
    za = _matmul(hn, w["w_a"], name=f"{tag}_in_a")
    zb = _matmul(hn, w["w_b"], out_dtype=_ACT, name=f"{tag}_in_b")
    cq, ckv, kr = za[:, :MLA_Q_RANK], za[:, MLA_Q_RANK:MLA_Q_RANK + MLA_KV_RANK], za[:, MLA_Q_RANK + MLA_KV_RANK:]
    cqn = _rmsnorm(cq, w["q_norm"], out_dtype=_ACT, name=f"{tag}_qnorm")
    ckvn = _rmsnorm(ckv, w["kv_norm"], out_dtype=_ACT, name=f"{tag}_kvnorm")
    q = _matmul(cqn, w["w_uq"], scale=(MLA_NOPE + MLA_ROPE) ** -0.5 * LOG2_E,
                name=f"{tag}_uq")
    kv = _matmul(ckvn, w["w_ukv"], out_dtype=_ACT, name=f"{tag}_ukv")
    nn = MLA_HEADS * MLA_NOPE
    q_rope = _rope(q[:, nn:], mla_c, mla_s, out_dtype=_ACT, name=f"{tag}_rope_q")
    k_rope = _rope(kr, mla_c, mla_s, out_dtype=_ACT, name=f"{tag}_rope_k")[:, :MLA_ROPE]
    qh = jnp.concatenate([_heads(q[:, :nn].astype(_ACT), MLA_HEADS), _heads(q_rope, MLA_HEADS)], axis=-1)
    kh = jnp.concatenate([_heads(kv[:, :nn], MLA_HEADS),
                          jnp.broadcast_to(k_rope[None], (MLA_HEADS, s, MLA_ROPE))], axis=-1)
    vh = _heads(kv[:, nn:], MLA_HEADS)
    oa, lse_a = _attn_fwd(qh, kh, vh, out_dtype=_ACT, name=f"{tag}_mla")

    nd, ng = DIL_HEADS * DIL_HEAD_DIM, DIL_SLOTS * DIL_HEAD_DIM
    outs, lses, dil_saved = [], [], []
    for gi, (_, dil) in enumerate(DIL_PAIRS):
        hs = slice(gi * DIL_SLOTS, (gi + 1) * DIL_SLOTS)
        qr, kr_, vr = (_to_res(zb[:, a * nd + gi * ng:a * nd + (gi + 1) * ng], dil) for a in range(3))
        o, l = _win_fwd(qr, kr_, vr, slopes[hs], dil=dil, name=f"{tag}_dil{gi}")
        dil_saved.append((qr, kr_, vr, l))
        outs.append(_res_to_tok(o, dil))
        lses.append(_res_to_tok(l, dil))
    o3, lse3 = jnp.stack(outs), jnp.stack(lses)
    comb = _merge_fwd(o3, lse3, name=f"{tag}_merge")
    cat = jnp.concatenate([_unheads(oa), _unheads(comb)], axis=-1)
    out = _matmul(cat, w["w_out"], res=x, name=f"{tag}_out")
    return out, (x, hn, cq, ckv, cqn, ckvn, qh, kh, vh, oa, lse_a, dil_saved, o3, lse3, cat)


def _even_bwd(dout, saved, w, tabs, slopes, tag):
    x, hn, cq, ckv, cqn, ckvn, qh, kh, vh, oa, lse_a, dil_saved, o3, lse3, cat = saved
    (mla_c, mla_s), _ = tabs
    nn = MLA_HEADS * MLA_NOPE
    dcat = _matmul(dout, w["w_out"], trans_b=True, out_dtype=_ACT, name=f"{tag}_dcat")
    d_w_out = _matmul(cat, dout, trans_a=True, name=f"{tag}_dwout")
    nv = MLA_HEADS * MLA_V

    doa = _heads(dcat[:, :nv], MLA_HEADS)
    dqh, dkt, dvt = _attn_bwd(qh, kh, vh, oa, doa, lse_a, scale=(MLA_NOPE + MLA_ROPE) ** -0.5,
                              name=f"{tag}_mla_bwd")
    dq_rope = _rope(_unheads(dqh[..., MLA_NOPE:]), mla_c, -mla_s, out_dtype=_ACT, name=f"{tag}_drope_q")
    dq = jnp.concatenate([_unheads(dqh[..., :MLA_NOPE]).astype(_ACT), dq_rope], axis=-1)
    dk3 = _unchunk(dkt).reshape(-1, MLA_HEADS, MLA_NOPE + MLA_ROPE)
    dkr = _rope(dk3[..., MLA_NOPE:].reshape(-1, MLA_HEADS * MLA_ROPE), mla_c, -mla_s, out_dtype=_F32,
                sum_chunks=True, name=f"{tag}_drope_k")
    dkv = jnp.concatenate([dk3[..., :MLA_NOPE].reshape(-1, nn), _unchunk(dvt)], axis=-1).astype(_ACT)
    d_w_uq = _matmul(cqn, dq, trans_a=True, name=f"{tag}_dwuq")
    d_w_ukv = _matmul(ckvn, dkv, trans_a=True, name=f"{tag}_dwukv")
    dcqn = _matmul(dq, w["w_uq"], trans_b=True, name=f"{tag}_dcqn")
    dckvn = _matmul(dkv, w["w_ukv"], trans_b=True, name=f"{tag}_dckvn")
    dcq, d_q_norm = _rmsnorm_bwd(cq, w["q_norm"], dcqn, out_dtype=_ACT, name=f"{tag}_dqnorm")
    dckv, d_kv_norm = _rmsnorm_bwd(ckv, w["kv_norm"], dckvn, out_dtype=_ACT, name=f"{tag}_dkvnorm")
    lane = jnp.arange(LANES) < MLA_ROPE
    dza = jnp.concatenate([dcq, dckv, jnp.where(lane[None], dkr, 0.0).astype(_ACT)], axis=-1)

    dcomb = _heads(dcat[:, nv:], DIL_SLOTS)
    do3, delta3 = _merge_bwd(dcomb, o3, lse3, name=f"{tag}_dmerge")
    dqs, dks, dvs = [], [], []
    for gi, (_, dil) in enumerate(DIL_PAIRS):
        hs = slice(gi * DIL_SLOTS, (gi + 1) * DIL_SLOTS)
        qr, kr_, vr, l = dil_saved[gi]
        grads = (qr, kr_, vr, _tok_to_res(do3[gi], dil), l, _tok_to_res(delta3[gi], dil), slopes[hs])
        dqs.append(_from_res(_win_bwd_dq(*grads, dil=dil, name=f"{tag}_dil{gi}_dq"), dil))
        b, c = _win_bwd_dkv(*grads, dil=dil, name=f"{tag}_dil{gi}_dkv")
        dks.append(_from_res(b, dil))
        dvs.append(_from_res(c, dil))
    dzb = jnp.concatenate(dqs + dks + dvs, axis=-1).astype(_ACT)

    d_w_a = _matmul(hn, dza, trans_a=True, name=f"{tag}_dwa")
    d_w_b = _matmul(hn, dzb, trans_a=True, name=f"{tag}_dwb")
    dhn = _matmul(dza, w["w_a"], trans_b=True, name=f"{tag}_dhn_a")
    dhn = _matmul(dzb, w["w_b"], trans_b=True, res=dhn, name=f"{tag}_dhn_b")
    dx, dnorm = _rmsnorm_bwd(x, w["norm"], dhn, dout, name=f"{tag}_dnorm")
    return dx, {"norm": dnorm, "w_a": d_w_a, "w_b": d_w_b, "q_norm": d_q_norm, "kv_norm": d_kv_norm,
                "w_uq": d_w_uq, "w_ukv": d_w_ukv, "w_out": d_w_out}


def _odd_fwd(x, w, tabs, tag):
    s = x.shape[0]
    _, (gqa_c, gqa_s) = tabs
    nk = GQA_KV_HEADS * GQA_HEAD_DIM
    hn = _rmsnorm(x, w["norm"], out_dtype=_ACT, name=f"{tag}_norm")
    q = _matmul(hn, w["w_q"], name=f"{tag}_q")
    kv = _matmul(hn, w["w_kv"], name=f"{tag}_kv")
    k = kv[:, :nk]
    qh = _heads(_headnorm_rope(q, w["q_norm"], gqa_c, gqa_s, scale=GQA_HEAD_DIM ** -0.5 * LOG2_E,
                               name=f"{tag}_prep_q"), GQA_HEADS)
    kh = _heads(_headnorm_rope(k, w["k_norm"], gqa_c, gqa_s, name=f"{tag}_prep_k"), GQA_KV_HEADS)
    vh = _heads(kv[:, nk:].astype(_ACT), GQA_KV_HEADS)
    o, lse = _attn_fwd(qh, kh, vh, out_dtype=_ACT, name=f"{tag}_gqa")
    ocat = _unheads(o)
    out = _matmul(ocat, w["w_o"], res=x, name=f"{tag}_out")
    return out, (x, hn, q, k, qh, kh, vh, o, lse, ocat)


def _odd_bwd(dout, saved, w, tabs, tag):
    x, hn, q, k, qh, kh, vh, o, lse, ocat = saved
    s = x.shape[0]
    _, (gqa_c, gqa_s) = tabs
    docat = _matmul(dout, w["w_o"], trans_b=True, out_dtype=_ACT, name=f"{tag}_docat")
    d_w_o = _matmul(ocat, dout, trans_a=True, name=f"{tag}_dwo")
    doh = _heads(docat, GQA_HEADS)
    dqh, dkt, dvt = _attn_bwd(qh, kh, vh, o, doh, lse, scale=GQA_HEAD_DIM ** -0.5, name=f"{tag}_gqa_bwd")
    dq, d_q_norm = _headnorm_rope_bwd(q, w["q_norm"], _unheads(dqh), gqa_c, gqa_s, name=f"{tag}_dprep_q")
    dk, d_k_norm = _headnorm_rope_bwd(k, w["k_norm"], _unchunk(dkt), gqa_c, gqa_s, name=f"{tag}_dprep_k")
    dkv = jnp.concatenate([dk, _unchunk(dvt).astype(_ACT)], axis=-1)
    d_w_q = _matmul(hn, dq, trans_a=True, name=f"{tag}_dwq")
    d_w_kv = _matmul(hn, dkv, trans_a=True, name=f"{tag}_dwkv")
    dhn = _matmul(dq, w["w_q"], trans_b=True, name=f"{tag}_dhn_q")
    dhn = _matmul(dkv, w["w_kv"], trans_b=True, res=dhn, name=f"{tag}_dhn_kv")
    dx, dnorm = _rmsnorm_bwd(x, w["norm"], dhn, dout, name=f"{tag}_dnorm")
    return dx, {"norm": dnorm, "w_q": d_w_q, "w_kv": d_w_kv, "q_norm": d_q_norm, "k_norm": d_k_norm, "w_o": d_w_o}


def _split_heads_cols(wm, heads, first):
    rws = wm.shape[0]
    w3 = wm.reshape(rws, heads, -1)
    return jnp.concatenate([w3[:, :, :first].reshape(rws, -1), w3[:, :, first:].reshape(rws, -1)], axis=-1)


def _merge_heads_cols(wm, heads, first):
    rws, cols = wm.shape
    a = wm[:, :heads * first].reshape(rws, heads, first)
    b = wm[:, heads * first:].reshape(rws, heads, cols // heads - first)
    return jnp.concatenate([a, b], axis=-1).reshape(rws, cols)


def _layer_weights(full, gains, layer):
    i = layer // 2
    ffn = {"norm": gains["ffn_norm"][layer], "w_gate": full["ffn_w_in"][layer][:, :FFN_HIDDEN],
           "w_up": full["ffn_w_in"][layer][:, FFN_HIDDEN:], "w_out": full["ffn_w_out"][layer]}
    if layer % 2 == 0:
        w_in = full["w_in_ab"][i]
        mix = {"norm": gains["mix_norm_ab"][i],
               "w_a": jnp.pad(w_in[:, :IN_A], ((0, 0), (0, IN_A_PAD - IN_A))), "w_b": w_in[:, IN_A:],
               "q_norm": gains["mla_q_norm"][i], "kv_norm": gains["mla_kv_norm"][i],
               "w_uq": _split_heads_cols(full["mla_w_uq"][i], MLA_HEADS, MLA_NOPE),
               "w_ukv": _split_heads_cols(full["mla_w_ukv"][i], MLA_HEADS, MLA_NOPE),
               "w_out": full["w_out_ab"][i]}
    else:
        mix = {"norm": gains["mix_norm_c"][i], "w_q": full["gqa_w_q"][i], "w_kv": full["gqa_w_kv"][i],
               "q_norm": gains["gqa_q_norm"][i], "k_norm": gains["gqa_k_norm"][i], "w_o": full["gqa_w_o"][i]}
    return mix, ffn


def kernel(x, mix_norm_ab, w_in_ab, mla_q_norm, mla_kv_norm, mla_w_uq, mla_w_ukv, w_out_ab, mix_norm_c, gqa_w_q, gqa_w_kv, gqa_q_norm, gqa_k_norm, gqa_w_o, ffn_norm, ffn_w_in, ffn_w_out, final_norm, loss_target, m_mix_norm_ab, m_w_in_ab, m_mla_q_norm, m_mla_kv_norm, m_mla_w_uq, m_mla_w_ukv, m_w_out_ab, m_mix_norm_c, m_gqa_w_q, m_gqa_w_kv, m_gqa_q_norm, m_gqa_k_norm, m_gqa_w_o, m_ffn_norm, m_ffn_w_in, m_ffn_w_out, m_final_norm, v_mix_norm_ab, v_w_in_ab, v_mla_q_norm, v_mla_kv_norm, v_mla_w_uq, v_mla_w_ukv, v_w_out_ab, v_mix_norm_c, v_gqa_w_q, v_gqa_w_kv, v_gqa_q_norm, v_gqa_k_norm, v_gqa_w_o, v_ffn_norm, v_ffn_w_in, v_ffn_w_out, v_final_norm):
    wts = dict(mix_norm_ab=mix_norm_ab, w_in_ab=w_in_ab, mla_q_norm=mla_q_norm, mla_kv_norm=mla_kv_norm,
               mla_w_uq=mla_w_uq, mla_w_ukv=mla_w_ukv, w_out_ab=w_out_ab, mix_norm_c=mix_norm_c, gqa_w_q=gqa_w_q,
               gqa_w_kv=gqa_w_kv, gqa_q_norm=gqa_q_norm, gqa_k_norm=gqa_k_norm, gqa_w_o=gqa_w_o, ffn_norm=ffn_norm,
               ffn_w_in=ffn_w_in, ffn_w_out=ffn_w_out, final_norm=final_norm)
    mom = dict(mix_norm_ab=m_mix_norm_ab, w_in_ab=m_w_in_ab, mla_q_norm=m_mla_q_norm, mla_kv_norm=m_mla_kv_norm,
               mla_w_uq=m_mla_w_uq, mla_w_ukv=m_mla_w_ukv, w_out_ab=m_w_out_ab, mix_norm_c=m_mix_norm_c,
               gqa_w_q=m_gqa_w_q, gqa_w_kv=m_gqa_w_kv, gqa_q_norm=m_gqa_q_norm, gqa_k_norm=m_gqa_k_norm,
               gqa_w_o=m_gqa_w_o, ffn_norm=m_ffn_norm, ffn_w_in=m_ffn_w_in, ffn_w_out=m_ffn_w_out,
               final_norm=m_final_norm)
    var = dict(mix_norm_ab=v_mix_norm_ab, w_in_ab=v_w_in_ab, mla_q_norm=v_mla_q_norm, mla_kv_norm=v_mla_kv_norm,
               mla_w_uq=v_mla_w_uq, mla_w_ukv=v_mla_w_ukv, w_out_ab=v_w_out_ab, mix_norm_c=v_mix_norm_c,
               gqa_w_q=v_gqa_w_q, gqa_w_kv=v_gqa_w_kv, gqa_q_norm=v_gqa_q_norm, gqa_k_norm=v_gqa_k_norm,
               gqa_w_o=v_gqa_w_o, ffn_norm=v_ffn_norm, ffn_w_in=v_ffn_w_in, ffn_w_out=v_ffn_w_out,
               final_norm=v_final_norm)
    big_names = [n for n, _ in _BIG]
    shard_shapes = {n: wts[n].shape for n in big_names}
    xs = x[0]
    s = xs.shape[0]
    me = 4 * lax.axis_index("x") + 2 * lax.axis_index("y") + lax.axis_index("c")
    c_cols = mix_norm_c.shape[1]

    w_pack = _pack_big({n: wts[n] for n in big_names})
    n_rows = w_pack.shape[0]
    tail = jnp.pad(mix_norm_c.reshape(-1), (0, GATHER_TAIL_ROWS // 2 * PACK_COLS - mix_norm_c.size))
    tail = _to_bits16(tail.reshape(GATHER_TAIL_ROWS // 2, PACK_COLS))
    gathered = _all_gather(jnp.concatenate([w_pack.astype(jnp.bfloat16), tail], axis=0))
    full = _unpack_gathered(gathered[:, :n_rows], shard_shapes)
    c_all = _from_bits16(gathered[:, n_rows:]).reshape(N_DEV, -1)[:, :mix_norm_c.size]
    c_full = c_all.reshape(N_DEV, 2, c_cols).transpose(1, 0, 2).reshape(2, N_DEV * c_cols)
    gains = dict(mix_norm_ab=mix_norm_ab, mla_q_norm=mla_q_norm, mla_kv_norm=mla_kv_norm, mix_norm_c=c_full,
                 gqa_q_norm=gqa_q_norm, gqa_k_norm=gqa_k_norm, ffn_norm=ffn_norm)

    tabs = _rope_tables(s)
    slopes = jnp.exp2(-8.0 * jnp.arange(1, DIL_HEADS + 1, dtype=_F32) / DIL_HEADS)

    h = xs
    saved = []
    for layer in range(DEPTH):
        mix_w, ffn_w = _layer_weights(full, gains, layer)
        if layer % 2 == 0:
            h, sv_mix = _even_fwd(h, mix_w, tabs, slopes, f"l{layer}_mix")
        else:
            h, sv_mix = _odd_fwd(h, mix_w, tabs, f"l{layer}_mix")
        h, sv_ffn = _ffn_fwd(h, ffn_w, f"l{layer}_ffn")
        saved.append((mix_w, ffn_w, sv_mix, sv_ffn))
    loss_local, dh, d_final = _final_loss(h, final_norm, loss_target[0], name="final_loss")

    gfull = {n: [None] * wts[n].shape[0] for n in big_names}
    gsmall = {n: [None] * (wts[n].shape[0] if wts[n].ndim > 1 else 1) for n in _SMALL}
    gsmall["final_norm"][0] = d_final
    for layer in reversed(range(DEPTH)):
        mix_w, ffn_w, sv_mix, sv_ffn = saved[layer]
        i = layer // 2
        dh, gf = _ffn_bwd(dh, sv_ffn, ffn_w, f"l{layer}_ffn")
        gsmall["ffn_norm"][layer] = gf["norm"]
        gfull["ffn_w_in"][layer] = jnp.concatenate([gf["w_gate"], gf["w_up"]], axis=-1)
        gfull["ffn_w_out"][layer] = gf["w_out"]
        if layer % 2 == 0:
            dh, gm = _even_bwd(dh, sv_mix, mix_w, tabs, slopes, f"l{layer}_mix")
            gsmall["mix_norm_ab"][i] = gm["norm"]
            gsmall["mla_q_norm"][i] = gm["q_norm"]
            gsmall["mla_kv_norm"][i] = gm["kv_norm"]
            gfull["w_in_ab"][i] = jnp.concatenate([gm["w_a"][:, :IN_A], gm["w_b"]], axis=-1)
            gfull["mla_w_uq"][i] = _merge_heads_cols(gm["w_uq"], MLA_HEADS, MLA_NOPE)
            gfull["mla_w_ukv"][i] = _merge_heads_cols(gm["w_ukv"], MLA_HEADS, MLA_NOPE)
            gfull["w_out_ab"][i] = gm["w_out"]
        else:
            dh, gm = _odd_bwd(dh, sv_mix, mix_w, tabs, f"l{layer}_mix")
            gsmall["mix_norm_c"][i] = gm["norm"]
            gsmall["gqa_q_norm"][i] = gm["q_norm"]
            gsmall["gqa_k_norm"][i] = gm["k_norm"]
            gfull["gqa_w_q"][i] = gm["w_q"]
            gfull["gqa_w_kv"][i] = gm["w_kv"]
            gfull["gqa_w_o"][i] = gm["w_o"]
    grad_x = dh[None]

    g_parts = jnp.concatenate([_split_for_devices(jnp.stack(gfull[n]), axis) for n, axis in _BIG], axis=1)
    small_part = _pack_small({n: jnp.stack(gsmall[n]) for n in _SMALL})
    small_bits = jnp.broadcast_to(_to_bits16(small_part)[None], (N_DEV, EXCHANGE_TAIL_ROWS, PACK_COLS))
    received = _exchange(jnp.concatenate([g_parts.astype(jnp.bfloat16), small_bits], axis=1))
    g_big, d_big, m_big, v_big = _reduce_adamw(
        received, w_pack, _pack_big({n: mom[n] for n in big_names}), _pack_big({n: var[n] for n in big_names}),
        name="adamw_big")

    def widen_c(shard):
        return lax.dynamic_update_slice(jnp.zeros((2, N_DEV * c_cols), _F32), shard, (0, me * c_cols))

    def small_of(src):
        return _pack_small({n: (widen_c(src[n]) if n == "mix_norm_c" else src[n]) for n in _SMALL})

    small_recv = _from_bits16(received[:, n_rows:])
    g_sm, d_sm, m_sm, v_sm = _reduce_adamw(small_recv, small_of(wts), small_of(mom), small_of(var), name="adamw_small")

    def outputs_of(big_pack, small_pack):
        big = _unpack_big(big_pack, shard_shapes)
        small = _unpack_small(small_pack)
        res = {}
        for n in wts:
            if n in big:
                res[n] = big[n]
            elif n == "mix_norm_c":
                res[n] = lax.dynamic_slice(small[n].reshape(2, N_DEV * c_cols), (0, me * c_cols), (2, c_cols))
            else:
                res[n] = small[n].reshape(wts[n].shape)
        return [res[n] for n in wts]

    loss = lax.psum(loss_local, _AXES)
    return (loss, grad_x, *outputs_of(g_big, g_sm), *outputs_of(d_big, d_sm), *outputs_of(m_big, m_sm),
            *outputs_of(v_big, v_sm))
```

```python
import functools

import jax
import jax.numpy as jnp
from jax import lax
from jax.experimental import pallas as pl
from jax.experimental.pallas import tpu as pltpu

D_MODEL = 1024
DEPTH = 4
GRID_W = 64
NORM_EPS = 1e-6
ROPE_THETA = 10000.0
NEG_INF = -1e30
MLA_HEADS = 8
MLA_Q_RANK = 384
MLA_KV_RANK = 256
MLA_NOPE = 64
MLA_ROPE = 32
MLA_V = 64
DIL_PAIRS = ((128, 1), (512, 4), (2048, 16))
DIL_HALF = 64
DIL_SLOTS = 4
DIL_GROUPS = 3
DIL_HEADS = 12
DIL_HEAD_DIM = 64
GQA_HEADS = 16
GQA_KV_HEADS = 4
GQA_HEAD_DIM = 64
FFN_HIDDEN = 2816
IN_A = MLA_Q_RANK + MLA_KV_RANK + MLA_ROPE
IN_A_PAD = 768
IN_B = 3 * DIL_HEADS * DIL_HEAD_DIM
ADAM_LR = 0.001
ADAM_B1 = 0.9
ADAM_B2 = 0.999
ADAM_EPS = 1e-08
ADAM_WD = 0.01
ADAM_STEP = 10

LANES = 128
SUBLANES_16BIT = 16
VMEM_LIMIT_BYTES = 56 * 1024 * 1024

MM_TILE = 1408

N_DEV = 8
PACK_COLS = 1024

_MXU = jnp.bfloat16
_ACT = jnp.bfloat16
_F32 = jnp.float32

_AXES = ("x", "y", "c")


def _params(sem):
    return pltpu.CompilerParams(dimension_semantics=sem, vmem_limit_bytes=VMEM_LIMIT_BYTES)


def _pick(n, cap):
    for t in range(cap - cap % LANES, 0, -LANES):
        if n % t == 0:
            return t
    return n


def _rows(m, target):
    t = m
    while t > target and t % 2 == 0:
        t //= 2
    return t


def _matmul(a, b, *, trans_a=False, trans_b=False, res=None, scale=None, out_dtype=_F32, name):
    if trans_a:
        k, m = a.shape
    else:
        m, k = a.shape
    if trans_b:
        n, kb = b.shape
    else:
        kb, n = b.shape
    assert k == kb, (a.shape, b.shape)
    tm, tn, tk = _pick(m, MM_TILE), _pick(n, MM_TILE), _pick(k, MM_TILE)
    nk = k // tk
    dims = (((0 if trans_a else 1,), (1 if trans_b else 0,)), ((), ()))

    def body(*refs):
        if res is None:
            a_ref, b_ref, o_ref, acc = refs
            r_ref = None
        else:
            a_ref, b_ref, r_ref, o_ref, acc = refs
        kk = pl.program_id(2)

        @pl.when(kk == 0)
        def _():
            acc[...] = jnp.zeros_like(acc)

        acc[...] += lax.dot_general(a_ref[...].astype(_MXU), b_ref[...].astype(_MXU), dims,
                                    preferred_element_type=_F32)

        @pl.when(kk == nk - 1)
        def _():
            r = acc[...]
            if scale is not None:
                r = r * scale
            if r_ref is not None:
                r = r + r_ref[...].astype(_F32)
            o_ref[...] = r.astype(out_dtype)

    a_spec = (pl.BlockSpec((tk, tm), lambda i, j, kk: (kk, i)) if trans_a
              else pl.BlockSpec((tm, tk), lambda i, j, kk: (i, kk)))
    b_spec = (pl.BlockSpec((tn, tk), lambda i, j, kk: (j, kk)) if trans_b
              else pl.BlockSpec((tk, tn), lambda i, j, kk: (kk, j)))
    o_spec = pl.BlockSpec((tm, tn), lambda i, j, kk: (i, j))
    in_specs = [a_spec, b_spec] + ([o_spec] if res is not None else [])
    args = (a, b) + ((res,) if res is not None else ())
    return pl.pallas_call(
        body, name=name, grid=(m // tm, n // tn, nk),
        in_specs=in_specs, out_specs=o_spec,
        out_shape=jax.ShapeDtypeStruct((m, n), out_dtype),
        scratch_shapes=[pltpu.VMEM((tm, tn), _F32)],
        compiler_params=_params(("parallel", "parallel", "arbitrary")),
    )(*args)


def _rmsnorm(x, g, *, out_dtype, name, rows=512):
    m, d = x.shape
    tm = _rows(m, rows)

    def body(x_ref, g_ref, o_ref):
        xf = x_ref[...].astype(_F32)
        r = lax.rsqrt(jnp.mean(xf * xf, axis=-1, keepdims=True) + NORM_EPS)
        o_ref[...] = ((xf * r) * g_ref[...]).astype(out_dtype)

    return pl.pallas_call(
        body, name=name, grid=(m // tm,),
        in_specs=[pl.BlockSpec((tm, d), lambda i: (i, 0)), pl.BlockSpec((1, d), lambda i: (0, 0))],
        out_specs=pl.BlockSpec((tm, d), lambda i: (i, 0)),
        out_shape=jax.ShapeDtypeStruct((m, d), out_dtype),
        compiler_params=_params(("parallel",)),
    )(x, g.reshape(1, d).astype(_F32))


def _rmsnorm_bwd(x, g, dy, dres=None, *, out_dtype=_F32, name, rows=512):
    m, d = x.shape
    tm = _rows(m, rows)

    def body(*refs):
        if dres is None:
            x_ref, g_ref, dy_ref, dx_ref, dg_ref = refs
            r_ref = None
        else:
            x_ref, g_ref, dy_ref, r_ref, dx_ref, dg_ref = refs

        @pl.when(pl.program_id(0) == 0)
        def _():
            dg_ref[...] = jnp.zeros_like(dg_ref)

        xf = x_ref[...].astype(_F32)
        r = lax.rsqrt(jnp.mean(xf * xf, axis=-1, keepdims=True) + NORM_EPS)
        xh = xf * r
        dyf = dy_ref[...].astype(_F32)
        dg_ref[...] += jnp.sum(dyf * xh, axis=0, keepdims=True)
        gdy = dyf * g_ref[...]
        dx = r * (gdy - xh * jnp.mean(gdy * xh, axis=-1, keepdims=True))
        if r_ref is not None:
            dx = dx + r_ref[...].astype(_F32)
        dx_ref[...] = dx.astype(out_dtype)

    row = pl.BlockSpec((tm, d), lambda i: (i, 0))
    vec = pl.BlockSpec((1, d), lambda i: (0, 0))
    in_specs = [row, vec, row] + ([row] if dres is not None else [])
    args = (x, g.reshape(1, d).astype(_F32), dy) + ((dres,) if dres is not None else ())
    dx, dg = pl.pallas_call(
        body, name=name, grid=(m // tm,),
        in_specs=in_specs, out_specs=[row, vec],
        out_shape=[jax.ShapeDtypeStruct((m, d), out_dtype), jax.ShapeDtypeStruct((1, d), _F32)],
        compiler_params=_params(("arbitrary",)),
    )(*args)
    return dx, dg.reshape(d)


def _rotate(xf, c, sn):
    w = xf.shape[1]
    if w > LANES:
        c, sn = jnp.tile(c, (1, w // LANES)), jnp.tile(sn, (1, w // LANES))
    lane = lax.broadcasted_iota(jnp.int32, xf.shape, 1)
    sw = jnp.where((lane & 31) < 16, pltpu.roll(xf, w - 16, 1), pltpu.roll(xf, 16, 1))
    return xf * c + sw * sn


def _seg_mean(v, seg_ref):
    outs = []
    for c in range(v.shape[1] // LANES):
        piece = v[:, c * LANES:(c + 1) * LANES]
        hi = piece.astype(jnp.bfloat16)
        lo = (piece - hi.astype(_F32)).astype(jnp.bfloat16)
        outs.append(jnp.dot(hi, seg_ref[...], preferred_element_type=_F32)
                    + jnp.dot(lo, seg_ref[...], preferred_element_type=_F32))
    return jnp.concatenate(outs, axis=1) if len(outs) > 1 else outs[0]


def _seg_matrix():
    lane = jnp.arange(LANES) // GQA_HEAD_DIM
    return ((lane[:, None] == lane[None, :]).astype(_F32) / GQA_HEAD_DIM).astype(jnp.bfloat16)


def _headnorm_rope(x, gain, cos_t, sin_t, *, scale=None, name):
    s, w = x.shape
    ts = _rows(s, 512)

    def body(x_ref, g_ref, seg_ref, c_ref, s_ref, o_ref):
        xf = x_ref[...]
        r = lax.rsqrt(_seg_mean(xf * xf, seg_ref) + NORM_EPS)
        y = _rotate((xf * r) * g_ref[...], c_ref[...], s_ref[...])
        if scale is not None:
            y = y * scale
        o_ref[...] = y.astype(o_ref.dtype)

    row = pl.BlockSpec((ts, w), lambda i: (i, 0))
    tab = pl.BlockSpec((ts, LANES), lambda i: (i, 0))
    return pl.pallas_call(
        body, name=name, grid=(s // ts,),
        in_specs=[row, pl.BlockSpec((1, w), lambda i: (0, 0)), pl.BlockSpec((LANES, LANES), lambda i: (0, 0)), tab, tab],
        out_specs=row, out_shape=jax.ShapeDtypeStruct((s, w), _ACT),
        compiler_params=_params(("parallel",)),
    )(x, jnp.tile(gain.astype(_F32), w // GQA_HEAD_DIM).reshape(1, w), _seg_matrix(), cos_t, sin_t)


def _headnorm_rope_bwd(x, gain, dy, cos_t, sin_t, *, name):
    s, w = x.shape
    ts = _rows(s, 512)

    def body(x_ref, g_ref, seg_ref, c_ref, s_ref, dy_ref, dx_ref, dg_ref):
        @pl.when(pl.program_id(0) == 0)
        def _():
            dg_ref[...] = jnp.zeros_like(dg_ref)

        xf = x_ref[...]
        r = lax.rsqrt(_seg_mean(xf * xf, seg_ref) + NORM_EPS)
        xh = xf * r
        dyn = _rotate(dy_ref[...].astype(_F32), c_ref[...], -s_ref[...])
        dg_ref[...] += jnp.sum(dyn * xh, axis=0, keepdims=True)
        gdy = dyn * g_ref[...]
        dx_ref[...] = (r * (gdy - xh * _seg_mean(gdy * xh, seg_ref))).astype(dx_ref.dtype)

    row = pl.BlockSpec((ts, w), lambda i: (i, 0))
    vec = pl.BlockSpec((1, w), lambda i: (0, 0))
    tab = pl.BlockSpec((ts, LANES), lambda i: (i, 0))
    dx, dg = pl.pallas_call(
        body, name=name, grid=(s // ts,),
        in_specs=[row, vec, pl.BlockSpec((LANES, LANES), lambda i: (0, 0)), tab, tab, row],
        out_specs=[row, vec],
        out_shape=[jax.ShapeDtypeStruct((s, w), _ACT), jax.ShapeDtypeStruct((1, w), _F32)],
        compiler_params=_params(("arbitrary",)),
    )(x, jnp.tile(gain.astype(_F32), w // GQA_HEAD_DIM).reshape(1, w), _seg_matrix(), cos_t, sin_t, dy)
    return dx, dg.reshape(w // GQA_HEAD_DIM, GQA_HEAD_DIM).sum(axis=0)


def _rope(x, cos_t, sin_t, *, out_dtype, name, sum_chunks=False, scale=None):
    s, w = x.shape
    assert w % LANES == 0
    ts = _rows(s, 512)
    ow = LANES if sum_chunks else w

    def body(x_ref, c_ref, s_ref, o_ref):
        y = _rotate(x_ref[...].astype(_F32), c_ref[...], s_ref[...])
        if scale is not None:
            y = y * scale
        if sum_chunks:
            shift = w // 2
            while shift >= 32:
                y = y + pltpu.roll(y, shift, 1)
                shift //= 2
            y = y[:, :LANES]
        o_ref[...] = y.astype(out_dtype)

    return pl.pallas_call(
        body, name=name, grid=(s // ts,),
        in_specs=[pl.BlockSpec((ts, w), lambda i: (i, 0)), pl.BlockSpec((ts, LANES), lambda i: (i, 0)),
                  pl.BlockSpec((ts, LANES), lambda i: (i, 0))],
        out_specs=pl.BlockSpec((ts, ow), lambda i: (i, 0)),
        out_shape=jax.ShapeDtypeStruct((s, ow), out_dtype),
        compiler_params=_params(("parallel",)),
    )(x, cos_t, sin_t)


_NT = (((1,), (1,)), ((), ()))
_TN = (((0,), (0,)), ((), ()))
LOG2_E = 1.4426950408889634
LN_2 = 0.6931471805599453
ATTN_FWD_ROWS = 1024
ATTN_ROWS = 512
ATTN_CHAINS = 2
ATTN_FWD_KEYS = 2048
ATTN_BWD_KEYS = 1024


def _grid_step(dims):
    step, total = 0, 1
    for axis, n in enumerate(dims):
        step = step * n + pl.program_id(axis)
        total *= n
    return step, total


def _attn_fwd(q, k, v, *, out_dtype, name, gather=None):
    h, s, dk = q.shape
    g, _, dv = v.shape
    r = h // g
    rc = min(ATTN_FWD_ROWS, s)
    hp = min(r, ATTN_CHAINS)
    nrc = max(1, min(ATTN_CHAINS // hp, s // rc))
    tq = rc * nrc
    tk = min(ATTN_FWD_KEYS, s)
    nhp, nq, nk = r // hp, s // tq, s // tk
    units = [(a, c) for a in range(hp) for c in range(nrc)]

    def body(q_ref, k_ref, v_ref, *rest):
        if gather is None:
            o_ref, lse_ref = rest
        else:
            x_ref, o_ref, lse_ref, all_ref, send_sems, recv_sems, local_sem = rest
            start, forward, finish = _gather_phases(x_ref, all_ref, send_sems, recv_sems, local_sem)
            step, total = _grid_step((g, nhp, nq))
            pl.when(step == 0)(start)
            pl.when(step == total // 2)(forward)

        qs = [q_ref[a, c * rc:(c + 1) * rc, :] for a, c in units]
        init = tuple((jnp.full((rc, 1), NEG_INF, _F32), jnp.zeros((rc, 1), _F32), jnp.zeros((rc, dv), _F32))
                     for _ in units)

        def trip(j, carry):
            rows = pl.ds(pl.multiple_of(j * tk, tk), tk)
            kb, vb = k_ref[0, rows, :], v_ref[0, rows, :]
            out = []
            for u in range(len(units)):
                m, l, acc = carry[u]
                sc = lax.dot_general(qs[u], kb, _NT, preferred_element_type=_F32)
                m_new = jnp.maximum(m, jnp.max(sc, axis=1, keepdims=True))
                p = jnp.exp2(sc - m_new)
                alpha = jnp.exp2(m - m_new)
                l = alpha * l + jnp.sum(p, axis=1, keepdims=True)
                acc = alpha * acc + jnp.dot(p.astype(_MXU), vb, preferred_element_type=_F32)
                out.append((m_new, l, acc))
            return tuple(out)

        fin = lax.fori_loop(0, nk, trip, init)
        for u, (a, c) in enumerate(units):
            m, l, acc = fin[u]
            o_ref[a, c * rc:(c + 1) * rc, :] = (acc / l).astype(out_dtype)
            lse_ref[a, c * rc:(c + 1) * rc, :] = m + jnp.log2(l)

        if gather is not None:
            pl.when(step == total - 1)(finish)

    q_blk = lambda gg, hh, i: (gg * nhp + hh, i, 0)
    kv_blk = lambda gg, hh, i: (gg, 0, 0)
    in_specs = [pl.BlockSpec((hp, tq, dk), q_blk), pl.BlockSpec((1, s, dk), kv_blk), pl.BlockSpec((1, s, dv), kv_blk)]
    out_specs = [pl.BlockSpec((hp, tq, dv), q_blk), pl.BlockSpec((hp, tq, 1), q_blk)]
    out_shape = [jax.ShapeDtypeStruct((h, s, dv), out_dtype), jax.ShapeDtypeStruct((h, s, 1), _F32)]
    if gather is None:
        return pl.pallas_call(
            body, name=name, grid=(g, nhp, nq), in_specs=in_specs, out_specs=out_specs, out_shape=out_shape,
            compiler_params=_params(("parallel", "parallel", "parallel")),
        )(q, k, v)
    return pl.pallas_call(
        body, name=name, grid=(g, nhp, nq),
        in_specs=in_specs + [pl.BlockSpec(memory_space=pl.ANY)],
        out_specs=out_specs + [pl.BlockSpec(memory_space=pl.ANY)],
        out_shape=out_shape + [jax.ShapeDtypeStruct((N_DEV,) + gather.shape, gather.dtype)],
        scratch_shapes=_comm_scratch(),
        compiler_params=_params(("arbitrary", "arbitrary", "arbitrary")),
    )(q, k, v, gather)


def _attn_bwd(q, k, v, o, do, lse2, *, scale, name, exchange=None):
    h, s, dk = q.shape
    g, _, dv = v.shape
    r = h // g
    hp = min(r, ATTN_CHAINS)
    tq = min(ATTN_ROWS, s)
    tk = min(ATTN_BWD_KEYS * (ATTN_CHAINS // hp), s)
    nhp, nq, nk = r // hp, s // tq, s // tk

    def body(q_ref, k_ref, v_ref, o_ref, do_ref, lse_ref, *rest):
        if exchange is None:
            dq_ref, dk_ref, dv_ref = rest
        else:
            g_ref, dq_ref, dk_ref, dv_ref, got_ref, send_sems, recv_sems, local_sem = rest
            start, finish = _exchange_phases(g_ref, got_ref, send_sems, recv_sems, local_sem)
            step, total = _grid_step((g, nhp, nq))
            pl.when(step == 0)(start)
        hh, i = pl.program_id(1), pl.program_id(2)

        @pl.when((hh == 0) & (i == 0))
        def _():
            dk_ref[...] = jnp.zeros_like(dk_ref)
            dv_ref[...] = jnp.zeros_like(dv_ref)

        qs = [q_ref[a] for a in range(hp)]
        dos = [do_ref[a] for a in range(hp)]
        qts = [q_ref[a].T for a in range(hp)]
        dots = [do_ref[a].T for a in range(hp)]
        ls = [lse_ref[a] for a in range(hp)]
        dls = [jnp.sum(do_ref[a].astype(_F32) * o_ref[a].astype(_F32), axis=1, keepdims=True) for a in range(hp)]

        def trip(j, carry):
            rows = pl.ds(pl.multiple_of(j * tk, tk), tk)
            kb, vb = k_ref[0, rows, :], v_ref[0, rows, :]
            out = []
            for a in range(hp):
                p = jnp.exp2(lax.dot_general(qs[a], kb, _NT, preferred_element_type=_F32) - ls[a])
                dp = lax.dot_general(dos[a], vb, _NT, preferred_element_type=_F32)
                ds = (p * (dp - dls[a])).astype(_MXU)
                dv_ref[0, j] += jnp.dot(dots[a], p.astype(_MXU), preferred_element_type=_F32)
                dk_ref[0, j] += jnp.dot(qts[a], ds, preferred_element_type=_F32)
                out.append(carry[a] + jnp.dot(ds, kb, preferred_element_type=_F32))
            return tuple(out)

        fin = lax.fori_loop(0, nk, trip, tuple(jnp.zeros((tq, dk), _F32) for _ in range(hp)))
        for a in range(hp):
            dq_ref[a] = fin[a] * scale

        @pl.when((hh == nhp - 1) & (i == nq - 1))
        def _():
            dk_ref[...] = dk_ref[...] * LN_2

        if exchange is not None:
            pl.when(step == total - 1)(finish)

    q_blk = lambda gg, hh, i: (gg * nhp + hh, i, 0)
    kv_blk = lambda gg, hh, i: (gg, 0, 0)
    in_specs = [pl.BlockSpec((hp, tq, dk), q_blk), pl.BlockSpec((1, s, dk), kv_blk), pl.BlockSpec((1, s, dv), kv_blk),
                pl.BlockSpec((hp, tq, dv), q_blk), pl.BlockSpec((hp, tq, dv), q_blk), pl.BlockSpec((hp, tq, 1), q_blk)]
    out_specs = [pl.BlockSpec((hp, tq, dk), q_blk), pl.BlockSpec((1, nk, dk, tk), lambda gg, hh, i: (gg, 0, 0, 0)),
                 pl.BlockSpec((1, nk, dv, tk), lambda gg, hh, i: (gg, 0, 0, 0))]
    out_shape = [jax.ShapeDtypeStruct((h, s, dk), _F32), jax.ShapeDtypeStruct((g, nk, dk, tk), _F32),
                 jax.ShapeDtypeStruct((g, nk, dv, tk), _F32)]
    args = (q, k, v, o, do, lse2)
    scratch = []
    if exchange is not None:
        in_specs, args = in_specs + [pl.BlockSpec(memory_space=pl.ANY)], args + (exchange,)
        out_specs = out_specs + [pl.BlockSpec(memory_space=pl.ANY)]
        out_shape = out_shape + [jax.ShapeDtypeStruct(exchange.shape, exchange.dtype)]
        scratch = _comm_scratch()
    return pl.pallas_call(
        body, name=name, grid=(g, nhp, nq), in_specs=in_specs, out_specs=out_specs, out_shape=out_shape,
        scratch_shapes=scratch, compiler_params=_params(("arbitrary", "arbitrary", "arbitrary")),
    )(*args)


def _unchunk(xt):
    g, nk, d, tk = xt.shape
    return xt.transpose(1, 3, 0, 2).reshape(nk * tk, g * d)


WIN_ROWS = 512


def _win_geometry(s, dil):
    t = min(WIN_ROWS, s)
    length = s // dil
    lg = length.bit_length() - 1
    assert 1 << lg == length and t % DIL_HALF == 0 and s % t == 0
    return t, t + 2 * DIL_HALF, lg, t // DIL_HALF, s // DIL_HALF


def _win_specs(hn, t, d, halo_per_blk, n_halo):
    return [pl.BlockSpec((hn, DIL_HALF, d), lambda i: (0, jnp.maximum(i * halo_per_blk - 1, 0), 0)),
            pl.BlockSpec((hn, t, d), lambda i: (0, i, 0)),
            pl.BlockSpec((hn, DIL_HALF, d), lambda i: (0, jnp.minimum((i + 1) * halo_per_blk, n_halo - 1), 0))]


def _win_mask(i, t, w, lg, dil, wide_rows):
    shape = (w, t) if wide_rows else (t, w)
    rows = lax.broadcasted_iota(jnp.int32, shape, 0)
    cols = lax.broadcasted_iota(jnp.int32, shape, 1)
    base = i * t
    if wide_rows:
        pq, pk = base - DIL_HALF + rows, base + cols
    else:
        pq, pk = base + rows, base - DIL_HALF + cols
    arel = jnp.abs(pk - pq)
    valid = (arel <= DIL_HALF) & ((pk >> lg) == (pq >> lg))
    return valid, (-dil * arel).astype(_F32)


def _win3(lo_ref, mid_ref, hi_ref, a):
    return jnp.concatenate([lo_ref[a], mid_ref[a], hi_ref[a]], axis=0)


def _win_fwd(q, k, v, slopes, *, dil, name):
    hn, s, d = q.shape
    t, w, lg, hpb, n_halo = _win_geometry(s, dil)
    scale = d ** -0.5

    def body(sl_ref, q_ref, klo, kmid, khi, vlo, vmid, vhi, o_ref, lse_ref):
        valid, nb = _win_mask(pl.program_id(0), t, w, lg, dil, False)
        for a in range(hn):
            kw, vw = _win3(klo, kmid, khi, a), _win3(vlo, vmid, vhi, a)
            sc = lax.dot_general(q_ref[a], kw, _NT, preferred_element_type=_F32) * scale
            sc = jnp.where(valid, sc + sl_ref[a] * nb, NEG_INF)
            m = jnp.max(sc, axis=1, keepdims=True)
            e = jnp.exp(sc - m)
            den = jnp.sum(e, axis=1, keepdims=True)
            o_ref[a] = jnp.dot(e.astype(_MXU), vw, preferred_element_type=_F32) / den
            lse_ref[a] = m + jnp.log(den)

    kv = _win_specs(hn, t, d, hpb, n_halo)
    blk = lambda c: pl.BlockSpec((hn, t, c), lambda i: (0, i, 0))
    return pl.pallas_call(
        body, name=name, grid=(s // t,),
        in_specs=[pl.BlockSpec(memory_space=pltpu.SMEM), blk(d)] + kv + kv,
        out_specs=[blk(d), blk(1)],
        out_shape=[jax.ShapeDtypeStruct((hn, s, d), _F32), jax.ShapeDtypeStruct((hn, s, 1), _F32)],
        compiler_params=_params(("parallel",)),
    )(slopes.astype(_F32), q, k, k, k, v, v, v)


def _win_bwd_dq(q, k, v, do, lse, delta, slopes, *, dil, name):
    hn, s, d = q.shape
    t, w, lg, hpb, n_halo = _win_geometry(s, dil)
    scale = d ** -0.5

    def body(sl_ref, q_ref, klo, kmid, khi, vlo, vmid, vhi, do_ref, lse_ref, dl_ref, dq_ref):
        valid, nb = _win_mask(pl.program_id(0), t, w, lg, dil, False)
        for a in range(hn):
            kw, vw = _win3(klo, kmid, khi, a), _win3(vlo, vmid, vhi, a)
            sc = lax.dot_general(q_ref[a], kw, _NT, preferred_element_type=_F32) * scale
            p = jnp.exp(jnp.where(valid, sc + sl_ref[a] * nb, NEG_INF) - lse_ref[a])
            dp = lax.dot_general(do_ref[a], vw, _NT, preferred_element_type=_F32)
            ds = (p * (dp - dl_ref[a])).astype(_MXU)
            dq_ref[a] = jnp.dot(ds, kw, preferred_element_type=_F32) * scale

    kv = _win_specs(hn, t, d, hpb, n_halo)
    blk = lambda c: pl.BlockSpec((hn, t, c), lambda i: (0, i, 0))
    return pl.pallas_call(
        body, name=name, grid=(s // t,),
        in_specs=[pl.BlockSpec(memory_space=pltpu.SMEM), blk(d)] + kv + kv + [blk(d), blk(1), blk(1)],
        out_specs=blk(d),
        out_shape=jax.ShapeDtypeStruct((hn, s, d), _F32),
        compiler_params=_params(("parallel",)),
    )(slopes.astype(_F32), q, k, k, k, v, v, v, do, lse, delta)


def _win_bwd_dkv(q, k, v, do, lse, delta, slopes, *, dil, name):
    hn, s, d = q.shape
    t, w, lg, hpb, n_halo = _win_geometry(s, dil)
    scale = d ** -0.5

    def body(sl_ref, qlo, qmid, qhi, dolo, domid, dohi, llo, lmid, lhi, dllo, dlmid, dlhi, k_ref, v_ref,
             dk_ref, dv_ref):
        valid, nb = _win_mask(pl.program_id(0), t, w, lg, dil, True)
        for a in range(hn):
            qw, dow = _win3(qlo, qmid, qhi, a), _win3(dolo, domid, dohi, a)
            lw, dlw = _win3(llo, lmid, lhi, a), _win3(dllo, dlmid, dlhi, a)
            sc = lax.dot_general(qw, k_ref[a], _NT, preferred_element_type=_F32) * scale
            p = jnp.exp(jnp.where(valid, sc + sl_ref[a] * nb, NEG_INF) - lw)
            dp = lax.dot_general(dow, v_ref[a], _NT, preferred_element_type=_F32)
            ds = (p * (dp - dlw)).astype(_MXU)
            dv_ref[a] = lax.dot_general(p.astype(_MXU), dow, _TN, preferred_element_type=_F32)
            dk_ref[a] = lax.dot_general(ds, qw, _TN, preferred_element_type=_F32) * scale

    blk = lambda c: pl.BlockSpec((hn, t, c), lambda i: (0, i, 0))
    return pl.pallas_call(
        body, name=name, grid=(s // t,),
        in_specs=([pl.BlockSpec(memory_space=pltpu.SMEM)] + _win_specs(hn, t, d, hpb, n_halo)
                  + _win_specs(hn, t, d, hpb, n_halo) + _win_specs(hn, t, 1, hpb, n_halo)
                  + _win_specs(hn, t, 1, hpb, n_halo) + [blk(d), blk(d)]),
        out_specs=[blk(d), blk(d)],
        out_shape=[jax.ShapeDtypeStruct((hn, s, d), _F32), jax.ShapeDtypeStruct((hn, s, d), _F32)],
        compiler_params=_params(("parallel",)),
    )(slopes.astype(_F32), q, q, q, do, do, do, lse, lse, lse, delta, delta, delta, k, v)


def _merge_weights(lse):
    mx = jnp.max(lse, axis=0, keepdims=True)
    e = jnp.exp(lse - mx)
    return e / jnp.sum(e, axis=0, keepdims=True)


def _merge_fwd(o3, lse3, *, name):
    ng, sl, s, d = o3.shape
    t = _rows(s, 512)

    def body(o_ref, l_ref, c_ref):
        wts = _merge_weights(l_ref[:, 0])
        c_ref[0] = jnp.sum(wts * o_ref[:, 0], axis=0).astype(c_ref.dtype)

    return pl.pallas_call(
        body, name=name, grid=(sl, s // t),
        in_specs=[pl.BlockSpec((ng, 1, t, d), lambda a, i: (0, a, i, 0)),
                  pl.BlockSpec((ng, 1, t, 1), lambda a, i: (0, a, i, 0))],
        out_specs=pl.BlockSpec((1, t, d), lambda a, i: (a, i, 0)),
        out_shape=jax.ShapeDtypeStruct((sl, s, d), _ACT),
        compiler_params=_params(("parallel", "parallel")),
    )(o3, lse3)


def _merge_bwd(dcomb, o3, lse3, *, name):
    ng, sl, s, d = o3.shape
    t = _rows(s, 512)

    def body(dc_ref, o_ref, l_ref, do_ref, dl_ref):
        wts = _merge_weights(l_ref[:, 0])
        dc = dc_ref[0].astype(_F32)
        comb = jnp.sum(wts * o_ref[:, 0], axis=0)
        do_ref[:, 0] = (wts * dc[None]).astype(do_ref.dtype)
        dl_ref[:, 0] = wts * jnp.sum(dc * comb, axis=-1, keepdims=True)[None]

    big = pl.BlockSpec((ng, 1, t, d), lambda a, i: (0, a, i, 0))
    small = pl.BlockSpec((ng, 1, t, 1), lambda a, i: (0, a, i, 0))
    return pl.pallas_call(
        body, name=name, grid=(sl, s // t),
        in_specs=[pl.BlockSpec((1, t, d), lambda a, i: (a, i, 0)), big, small],
        out_specs=[big, small],
        out_shape=[jax.ShapeDtypeStruct((ng, sl, s, d), _ACT), jax.ShapeDtypeStruct((ng, sl, s, 1), _F32)],
        compiler_params=_params(("parallel", "parallel")),
    )(dcomb, o3, lse3)


SWIGLU_ROWS = 512


def _gate_up(a, w_gate, w_up, *, name):
    m, k = a.shape
    n = w_gate.shape[1]
    tm, tn, tk = _pick(m, SWIGLU_ROWS), _pick(n, MM_TILE), _pick(k, MM_TILE)
    nk = k // tk

    def body(a_ref, g_ref, u_ref, go_ref, uo_ref, act_ref, acc_g, acc_u):
        kk = pl.program_id(2)

        @pl.when(kk == 0)
        def _():
            acc_g[...] = jnp.zeros_like(acc_g)
            acc_u[...] = jnp.zeros_like(acc_u)

        av = a_ref[...].astype(_MXU)
        acc_g[...] += jnp.dot(av, g_ref[...].astype(_MXU), preferred_element_type=_F32)
        acc_u[...] += jnp.dot(av, u_ref[...].astype(_MXU), preferred_element_type=_F32)

        @pl.when(kk == nk - 1)
        def _():
            gf, uf = acc_g[...], acc_u[...]
            go_ref[...] = gf.astype(go_ref.dtype)
            uo_ref[...] = uf.astype(uo_ref.dtype)
            act_ref[...] = (gf * jax.nn.sigmoid(gf) * uf).astype(act_ref.dtype)

    w_spec = pl.BlockSpec((tk, tn), lambda i, j, kk: (kk, j))
    o_spec = pl.BlockSpec((tm, tn), lambda i, j, kk: (i, j))
    out = jax.ShapeDtypeStruct((m, n), _ACT)
    return pl.pallas_call(
        body, name=name, grid=(m // tm, n // tn, nk),
        in_specs=[pl.BlockSpec((tm, tk), lambda i, j, kk: (i, kk)), w_spec, w_spec],
        out_specs=[o_spec, o_spec, o_spec], out_shape=[out, out, out],
        scratch_shapes=[pltpu.VMEM((tm, tn), _F32), pltpu.VMEM((tm, tn), _F32)],
        compiler_params=_params(("parallel", "parallel", "arbitrary")),
    )(a, w_gate, w_up)


def _gate_up_bwd(dout, w_out, gate, up, *, name):
    m, k = dout.shape
    n = w_out.shape[0]
    tm, tn, tk = _pick(m, SWIGLU_ROWS), _pick(n, MM_TILE), _pick(k, MM_TILE)
    nk = k // tk

    def body(d_ref, w_ref, g_ref, u_ref, dg_ref, du_ref, acc):
        kk = pl.program_id(2)

        @pl.when(kk == 0)
        def _():
            acc[...] = jnp.zeros_like(acc)

        acc[...] += lax.dot_general(d_ref[...].astype(_MXU), w_ref[...].astype(_MXU), _NT,
                                    preferred_element_type=_F32)

        @pl.when(kk == nk - 1)
        def _():
            daf = acc[...]
            gf = g_ref[...].astype(_F32)
            sg = jax.nn.sigmoid(gf)
            dg_ref[...] = (daf * u_ref[...].astype(_F32) * (sg + gf * sg * (1.0 - sg))).astype(dg_ref.dtype)
            du_ref[...] = (daf * (gf * sg)).astype(du_ref.dtype)

    o_spec = pl.BlockSpec((tm, tn), lambda i, j, kk: (i, j))
    out = jax.ShapeDtypeStruct((m, n), _ACT)
    return pl.pallas_call(
        body, name=name, grid=(m // tm, n // tn, nk),
        in_specs=[pl.BlockSpec((tm, tk), lambda i, j, kk: (i, kk)), pl.BlockSpec((tn, tk), lambda i, j, kk: (j, kk)),
                  o_spec, o_spec],
        out_specs=[o_spec, o_spec], out_shape=[out, out],
        scratch_shapes=[pltpu.VMEM((tm, tn), _F32)],
        compiler_params=_params(("parallel", "parallel", "arbitrary")),
    )(dout, w_out, gate, up)


def _final_loss(x, g, target, *, name):
    m, d = x.shape
    tm = _rows(m, 512)

    def body(x_ref, g_ref, t_ref, loss_ref, dx_ref, dg_ref):
        @pl.when(pl.program_id(0) == 0)
        def _():
            loss_ref[...] = jnp.zeros_like(loss_ref)
            dg_ref[...] = jnp.zeros_like(dg_ref)

        xf = x_ref[...]
        r = lax.rsqrt(jnp.mean(xf * xf, axis=-1, keepdims=True) + NORM_EPS)
        xh = xf * r
        err = xh * g_ref[...] - t_ref[...]
        loss_ref[...] += 0.5 * jnp.sum(jnp.mean(err * err, axis=-1, keepdims=True))
        dy = err * (1.0 / d)
        dg_ref[...] += jnp.sum(dy * xh, axis=0, keepdims=True)
        gdy = dy * g_ref[...]
        dx_ref[...] = r * (gdy - xh * jnp.mean(gdy * xh, axis=-1, keepdims=True))

    row = pl.BlockSpec((tm, d), lambda i: (i, 0))
    vec = pl.BlockSpec((1, d), lambda i: (0, 0))
    loss, dx, dg = pl.pallas_call(
        body, name=name, grid=(m // tm,),
        in_specs=[row, vec, row],
        out_specs=[pl.BlockSpec((8, LANES), lambda i: (0, 0)), row, vec],
        out_shape=[jax.ShapeDtypeStruct((8, LANES), _F32), jax.ShapeDtypeStruct((m, d), _F32),
                   jax.ShapeDtypeStruct((1, d), _F32)],
        compiler_params=_params(("arbitrary",)),
    )(x, g.reshape(1, d), target)
    return loss[0, 0], dx, dg.reshape(d)


def _reduce_adamw(parts, w, m, v, *, name):
    rws, cols = w.shape
    tr = rws
    for cand in range(min(rws, 256), 0, -SUBLANES_16BIT):
        if cand % SUBLANES_16BIT == 0 and rws % cand == 0:
            tr = cand
            break

    def body(p_ref, w_ref, m_ref, v_ref, g_ref, d_ref, nm_ref, nv_ref):
        gsum = p_ref[0].astype(_F32)
        for dev in range(1, N_DEV):
            gsum = gsum + p_ref[dev].astype(_F32)
        m2 = ADAM_B1 * m_ref[...] + (1.0 - ADAM_B1) * gsum
        v2 = ADAM_B2 * v_ref[...] + (1.0 - ADAM_B2) * (gsum * gsum)
        m_hat = m2 / (1.0 - ADAM_B1 ** ADAM_STEP)
        v_hat = v2 / (1.0 - ADAM_B2 ** ADAM_STEP)
        g_ref[...] = gsum
        d_ref[...] = -ADAM_LR * (m_hat / (jnp.sqrt(v_hat) + ADAM_EPS) + ADAM_WD * w_ref[...])
        nm_ref[...] = m2
        nv_ref[...] = v2

    blk = pl.BlockSpec((tr, cols), lambda i: (i, 0))
    out = jax.ShapeDtypeStruct((rws, cols), _F32)
    return pl.pallas_call(
        body, name=name, grid=(rws // tr,),
        in_specs=[pl.BlockSpec((N_DEV, tr, cols), lambda i: (0, i, 0)), blk, blk, blk],
        out_specs=[blk, blk, blk, blk], out_shape=[out, out, out, out],
        compiler_params=_params(("parallel",)),
    )(parts, w, m, v)


def _mesh_pos():
    return lax.axis_index("x"), lax.axis_index("y"), lax.axis_index("c")


def _all_gather(block):
    rws, cols = block.shape

    def body(x_ref, out_ref, send_sems, recv_sems, local_sem):
        start, forward, finish = _gather_phases(x_ref, out_ref, send_sems, recv_sems, local_sem)
        start()
        forward()
        finish()

    return pl.pallas_call(
        body, name="weights_all_gather",
        out_shape=jax.ShapeDtypeStruct((N_DEV, rws, cols), block.dtype),
        in_specs=[pl.BlockSpec(memory_space=pl.ANY)],
        out_specs=pl.BlockSpec(memory_space=pl.ANY),
        scratch_shapes=_comm_scratch(),
    )(block)


def _comm_scratch():
    return [pltpu.SemaphoreType.DMA((N_DEV - 1,)), pltpu.SemaphoreType.DMA((N_DEV - 1,)), pltpu.SemaphoreType.DMA]


def _gather_phases(x_ref, out_ref, send_sems, recv_sems, local_sem):
    x, y, c = _mesh_pos()
    me, sibling = (x, y, c), (x, y, 1 - c)
    chips = [(1 - x, y), (x, 1 - y), (1 - x, 1 - y)]

    def slot(px, py, pc):
        return out_ref.at[4 * px + 2 * py + pc]

    def copy(k, blk, to, src=None):
        return pltpu.make_async_remote_copy(
            src_ref=slot(*blk) if src is None else src, dst_ref=slot(*blk),
            send_sem=send_sems.at[k], recv_sem=recv_sems.at[k],
            device_id=to, device_id_type=pl.DeviceIdType.MESH)

    mine = pltpu.make_async_copy(x_ref, slot(*me), local_sem)
    first = [copy(0, me, sibling, src=x_ref)]
    first += [copy(1 + j, me, (*chip, c), src=x_ref) for j, chip in enumerate(chips)]
    passed = [copy(4 + j, (*chip, c), sibling) for j, chip in enumerate(chips)]

    def start():
        mine.start()
        for cp in first:
            cp.start()

    def forward():
        for j, chip in enumerate(chips):
            copy(1 + j, (*chip, c), me).wait_recv()
            passed[j].start()

    def finish():
        copy(0, sibling, me).wait_recv()
        for j, chip in enumerate(chips):
            copy(4 + j, (*chip, 1 - c), me).wait_recv()
        for cp in first + passed:
            cp.wait_send()
        mine.wait()

    return start, forward, finish


def _exchange(parts):
    def body(g_ref, out_ref, send_sems, recv_sems, local_sem):
        start, finish = _exchange_phases(g_ref, out_ref, send_sems, recv_sems, local_sem)
        start()
        finish()

    return pl.pallas_call(
        body, name="grads_exchange",
        out_shape=jax.ShapeDtypeStruct(parts.shape, parts.dtype),
        in_specs=[pl.BlockSpec(memory_space=pl.ANY)],
        out_specs=pl.BlockSpec(memory_space=pl.ANY),
        scratch_shapes=_comm_scratch(),
    )(parts)


def _exchange_phases(g_ref, out_ref, send_sems, recv_sems, local_sem):
    x, y, c = _mesh_pos()
    me = 4 * x + 2 * y + c
    mine = pltpu.make_async_copy(g_ref.at[me], out_ref.at[me], local_sem)
    sends, arrivals = [], []
    for k in range(1, N_DEV):
        px = 1 - x if k & 4 else x
        py = 1 - y if k & 2 else y
        pc = 1 - c if k & 1 else c
        peer = 4 * px + 2 * py + pc
        sems = dict(send_sem=send_sems.at[k - 1], recv_sem=recv_sems.at[k - 1],
                    device_id=(px, py, pc), device_id_type=pl.DeviceIdType.MESH)
        sends.append(pltpu.make_async_remote_copy(src_ref=g_ref.at[peer], dst_ref=out_ref.at[me], **sems))
        arrivals.append(pltpu.make_async_remote_copy(src_ref=g_ref.at[peer], dst_ref=out_ref.at[peer], **sems))

    def start():
        mine.start()
        for cp in sends:
            cp.start()

    def finish():
        for cp in arrivals:
            cp.wait_recv()
        for cp in sends:
            cp.wait_send()
        mine.wait()

    return start, finish


_BIG = (("w_in_ab", 2), ("mla_w_uq", 1), ("mla_w_ukv", 1), ("w_out_ab", 2), ("gqa_w_q", 1), ("gqa_w_kv", 1),
        ("gqa_w_o", 1), ("ffn_w_in", 2), ("ffn_w_out", 1))
_SMALL = ("mix_norm_ab", "ffn_norm", "final_norm", "mix_norm_c", "mla_q_norm", "mla_kv_norm", "gqa_q_norm", "gqa_k_norm")
SMALL_ROWS = 16
GATHER_TAIL_ROWS = 16
EXCHANGE_TAIL_ROWS = 32


def _view3(a):
    return a.reshape(a.shape[0], a.shape[1], -1)


def _pad_rows(a2d):
    pad = -a2d.shape[0] % SUBLANES_16BIT
    return jnp.pad(a2d, ((0, pad), (0, 0))) if pad else a2d


def _pack_rows(shard):
    return _pad_rows(shard.reshape(-1, PACK_COLS))


def _packed_rows(shape):
    n = 1
    for d in shape:
        n *= d
    rows = n // PACK_COLS
    return rows + (-rows % SUBLANES_16BIT)


_FIRST = {"w_in_ab": (0, 1), "mla_w_uq": (0, 1), "mla_w_ukv": (0, 1), "w_out_ab": (0, 1), "ffn_w_in": (0, 1),
          "ffn_w_out": (0, 1)}
_REST = {"w_in_ab": (1, 1), "mla_w_uq": (1, 1), "mla_w_ukv": (1, 1), "w_out_ab": (1, 1), "gqa_w_q": (0, 2),
         "gqa_w_kv": (0, 2), "gqa_w_o": (0, 2), "ffn_w_in": (1, 3), "ffn_w_out": (1, 3)}
PACK_ROWS = 128


def _group_of(arrays, group):
    return {n: arrays[n][lo:lo + cnt] for n, (lo, cnt) in group.items()}


def _pack_big(shards):
    packed = jnp.concatenate([_pack_rows(shards[n]) for n, _ in _BIG if n in shards], axis=0)
    return jnp.pad(packed, ((0, -packed.shape[0] % PACK_ROWS), (0, 0)))


def _unpack_big(packed, shapes):
    out, off = {}, 0
    for n, _ in _BIG:
        if n not in shapes:
            continue
        size = 1
        for d in shapes[n]:
            size *= d
        out[n] = packed[off:off + size // PACK_COLS].reshape(shapes[n])
        off += _packed_rows(shapes[n])
    return out


def _unpack_gathered(gathered, shapes):
    out, off = {}, 0
    for n, axis in _BIG:
        if n not in shapes:
            continue
        size = 1
        for d in shapes[n]:
            size *= d
        l3 = (shapes[n][0], shapes[n][1], size // (shapes[n][0] * shapes[n][1]))
        sh = gathered[:, off:off + size // PACK_COLS].reshape((N_DEV,) + l3)
        if axis == 1:
            full = sh.transpose(1, 0, 2, 3).reshape(l3[0], N_DEV * l3[1], l3[2])
        else:
            full = sh.transpose(1, 2, 0, 3).reshape(l3[0], l3[1], N_DEV * l3[2])
        out[n] = full
        off += _packed_rows(shapes[n])
    return out


def _split_for_devices(full, axis):
    l, rws, cols = full.shape
    if axis == 1:
        sh = full.reshape(l, N_DEV, rws // N_DEV, cols).transpose(1, 0, 2, 3)
    else:
        sh = full.reshape(l, rws, N_DEV, cols // N_DEV).transpose(2, 0, 1, 3)
    flat = sh.reshape(N_DEV, -1, PACK_COLS)
    pad = -flat.shape[1] % SUBLANES_16BIT
    return jnp.pad(flat, ((0, 0), (0, pad), (0, 0))) if pad else flat


def _to_bits16(a_f32_rows):
    r = a_f32_rows.shape[0]
    return lax.bitcast_convert_type(a_f32_rows, jnp.bfloat16).reshape(2 * r, PACK_COLS)


def _from_bits16(a_bf16_rows):
    lead, r = a_bf16_rows.shape[:-2], a_bf16_rows.shape[-2]
    return lax.bitcast_convert_type(a_bf16_rows.reshape(lead + (r // 2, PACK_COLS, 2)), _F32)


def _small_sizes():
    return {"mix_norm_ab": 2 * D_MODEL, "ffn_norm": DEPTH * D_MODEL, "final_norm": D_MODEL, "mix_norm_c": 2 * D_MODEL,
            "mla_q_norm": 2 * MLA_Q_RANK, "mla_kv_norm": 2 * MLA_KV_RANK, "gqa_q_norm": 2 * GQA_HEAD_DIM,
            "gqa_k_norm": 2 * GQA_HEAD_DIM}


def _pack_small(vals):
    flat = jnp.concatenate([vals[n].reshape(-1).astype(_F32) for n in _SMALL])
    return jnp.pad(flat, (0, SMALL_ROWS * PACK_COLS - flat.shape[0])).reshape(SMALL_ROWS, PACK_COLS)


def _unpack_small(pack):
    flat, out, off = pack.reshape(-1), {}, 0
    sizes = _small_sizes()
    for n in _SMALL:
        out[n] = flat[off:off + sizes[n]]
        off += sizes[n]
    return out


def _angles(pos, dim):
    freqs = ROPE_THETA ** (-jnp.arange(0, dim, 2, dtype=_F32) / dim)
    ang = pos.astype(_F32)[:, None] * freqs[None, :]
    return jnp.cos(ang), jnp.sin(ang)


def _rope_tables(s):
    pos = jnp.arange(s)
    cos_t, sin_t = _angles(pos, MLA_ROPE)
    mla_c = jnp.tile(jnp.concatenate([cos_t, cos_t], -1), (1, LANES // 32))
    mla_s = jnp.tile(jnp.concatenate([-sin_t, sin_t], -1), (1, LANES // 32))
    rows = s // GRID_W
    row_idx = jnp.broadcast_to(jnp.arange(rows)[:, None], (rows, GRID_W)).reshape(-1)
    col_idx = jnp.broadcast_to(jnp.arange(GRID_W)[None, :], (rows, GRID_W)).reshape(-1)
    cos_r, sin_r = _angles(row_idx, GQA_HEAD_DIM // 2)
    cos_c, sin_c = _angles(col_idx, GQA_HEAD_DIM // 2)
    gqa_c = jnp.tile(jnp.concatenate([cos_r, cos_r, cos_c, cos_c], -1), (1, LANES // GQA_HEAD_DIM))
    gqa_s = jnp.tile(jnp.concatenate([-sin_r, sin_r, -sin_c, sin_c], -1), (1, LANES // GQA_HEAD_DIM))
    return (mla_c, mla_s), (gqa_c, gqa_s)


def _heads(x2d, h):
    s = x2d.shape[0]
    return x2d.reshape(s, h, -1).transpose(1, 0, 2)


def _unheads(xh):
    h, s, d = xh.shape
    return xh.transpose(1, 0, 2).reshape(s, h * d)


def _to_res(x2d, dil):
    s = x2d.shape[0]
    return x2d.reshape(s // dil, dil, DIL_SLOTS, -1).transpose(2, 1, 0, 3).reshape(DIL_SLOTS, s, -1)


def _from_res(xh, dil):
    sl, s, d = xh.shape
    return xh.reshape(sl, dil, s // dil, d).transpose(2, 1, 0, 3).reshape(s, sl * d)


def _res_to_tok(xh, dil):
    sl, s, c = xh.shape
    return xh.reshape(sl, dil, s // dil, c).transpose(0, 2, 1, 3).reshape(sl, s, c)


def _tok_to_res(xh, dil):
    sl, s, c = xh.shape
    return xh.reshape(sl, s // dil, dil, c).transpose(0, 2, 1, 3).reshape(sl, s, c)


def _ffn_fwd(x, w, tag):
    hn = _rmsnorm(x, w["norm"], out_dtype=_ACT, name=f"{tag}_norm")
    gate, up, act = _gate_up(hn, w["w_gate"], w["w_up"], name=f"{tag}_gate_up")
    out = _matmul(act, w["w_out"], res=x, name=f"{tag}_out")
    return out, (x, hn, gate, up, act)


def _ffn_bwd(dout, saved, w, tag):
    x, hn, gate, up, act = saved
    d_w_out = _matmul(act, dout, trans_a=True, name=f"{tag}_dwout")
    dgate, dup = _gate_up_bwd(dout, w["w_out"], gate, up, name=f"{tag}_dgate_up")
    d_w_gate = _matmul(hn, dgate, trans_a=True, name=f"{tag}_dwgate")
    d_w_up = _matmul(hn, dup, trans_a=True, name=f"{tag}_dwup")
    dhn = _matmul(dgate, w["w_gate"], trans_b=True, name=f"{tag}_dhn_gate")
    dhn = _matmul(dup, w["w_up"], trans_b=True, res=dhn, name=f"{tag}_dhn_up")
    dx, dnorm = _rmsnorm_bwd(x, w["norm"], dhn, dout, name=f"{tag}_dnorm")
    return dx, {"norm": dnorm, "w_gate": d_w_gate, "w_up": d_w_up, "w_out": d_w_out}


def _even_fwd(x, w, tabs, slopes, tag, gather=None):
    s = x.shape[0]
    (mla_c, mla_s), _ = tabs
    hn = _rmsnorm(x, w["norm"], out_dtype=_ACT, name=f"{tag}_norm")
    za = _matmul(hn, w["w_a"], name=f"{tag}_in_a")
    zb = _matmul(hn, w["w_b"], out_dtype=_ACT, name=f"{tag}_in_b")
    cq, ckv, kr = za[:, :MLA_Q_RANK], za[:, MLA_Q_RANK:MLA_Q_RANK + MLA_KV_RANK], za[:, MLA_Q_RANK + MLA_KV_RANK:]
    cqn = _rmsnorm(cq, w["q_norm"], out_dtype=_ACT, name=f"{tag}_qnorm")
    ckvn = _rmsnorm(ckv, w["kv_norm"], out_dtype=_ACT, name=f"{tag}_kvnorm")
    q = _matmul(cqn, w["w_uq"], scale=(MLA_NOPE + MLA_ROPE) ** -0.5 * LOG2_E,
                name=f"{tag}_uq")
    kv = _matmul(ckvn, w["w_ukv"], out_dtype=_ACT, name=f"{tag}_ukv")
    nn = MLA_HEADS * MLA_NOPE
    q_rope = _rope(q[:, nn:], mla_c, mla_s, out_dtype=_ACT, name=f"{tag}_rope_q")
    k_rope = _rope(kr, mla_c, mla_s, out_dtype=_ACT, name=f"{tag}_rope_k")[:, :MLA_ROPE]
    qh = jnp.concatenate([_heads(q[:, :nn].astype(_ACT), MLA_HEADS), _heads(q_rope, MLA_HEADS)], axis=-1)
    kh = jnp.concatenate([_heads(kv[:, :nn], MLA_HEADS),
                          jnp.broadcast_to(k_rope[None], (MLA_HEADS, s, MLA_ROPE))], axis=-1)
    vh = _heads(kv[:, nn:], MLA_HEADS)
    oa, lse_a, *gathered = _attn_fwd(qh, kh, vh, out_dtype=_ACT, name=f"{tag}_mla", gather=gather)

    nd, ng = DIL_HEADS * DIL_HEAD_DIM, DIL_SLOTS * DIL_HEAD_DIM
    outs, lses, dil_saved = [], [], []
    for gi, (_, dil) in enumerate(DIL_PAIRS):
        hs = slice(gi * DIL_SLOTS, (gi + 1) * DIL_SLOTS)
        qr, kr_, vr = (_to_res(zb[:, a * nd + gi * ng:a * nd + (gi + 1) * ng], dil) for a in range(3))
        o, l = _win_fwd(qr, kr_, vr, slopes[hs], dil=dil, name=f"{tag}_dil{gi}")
        dil_saved.append((qr, kr_, vr, l))
        outs.append(_res_to_tok(o, dil))
        lses.append(_res_to_tok(l, dil))
    o3, lse3 = jnp.stack(outs), jnp.stack(lses)
    comb = _merge_fwd(o3, lse3, name=f"{tag}_merge")
    cat = jnp.concatenate([_unheads(oa), _unheads(comb)], axis=-1)
    out = _matmul(cat, w["w_out"], res=x, name=f"{tag}_out")
    saved = (x, hn, cq, ckv, cqn, ckvn, qh, kh, vh, oa, lse_a, dil_saved, o3, lse3, cat)
    return out, saved, (gathered[0] if gathered else None)


def _even_bwd(dout, saved, w, tabs, slopes, tag, exchange=None):
    x, hn, cq, ckv, cqn, ckvn, qh, kh, vh, oa, lse_a, dil_saved, o3, lse3, cat = saved
    (mla_c, mla_s), _ = tabs
    nn = MLA_HEADS * MLA_NOPE
    dcat = _matmul(dout, w["w_out"], trans_b=True, out_dtype=_ACT, name=f"{tag}_dcat")
    d_w_out = _matmul(cat, dout, trans_a=True, name=f"{tag}_dwout")
    nv = MLA_HEADS * MLA_V

    doa = _heads(dcat[:, :nv], MLA_HEADS)
    dqh, dkt, dvt, *received = _attn_bwd(qh, kh, vh, oa, doa, lse_a, scale=(MLA_NOPE + MLA_ROPE) ** -0.5,
                                         name=f"{tag}_mla_bwd", exchange=exchange)
    dq_rope = _rope(_unheads(dqh[..., MLA_NOPE:]), mla_c, -mla_s, out_dtype=_ACT, name=f"{tag}_drope_q")
    dq = jnp.concatenate([_unheads(dqh[..., :MLA_NOPE]).astype(_ACT), dq_rope], axis=-1)
    dk3 = _unchunk(dkt).reshape(-1, MLA_HEADS, MLA_NOPE + MLA_ROPE)
    dkr = _rope(dk3[..., MLA_NOPE:].reshape(-1, MLA_HEADS * MLA_ROPE), mla_c, -mla_s, out_dtype=_F32,
                sum_chunks=True, name=f"{tag}_drope_k")
    dkv = jnp.concatenate([dk3[..., :MLA_NOPE].reshape(-1, nn), _unchunk(dvt)], axis=-1).astype(_ACT)
    d_w_uq = _matmul(cqn, dq, trans_a=True, name=f"{tag}_dwuq")
    d_w_ukv = _matmul(ckvn, dkv, trans_a=True, name=f"{tag}_dwukv")
    dcqn = _matmul(dq, w["w_uq"], trans_b=True, name=f"{tag}_dcqn")
    dckvn = _matmul(dkv, w["w_ukv"], trans_b=True, name=f"{tag}_dckvn")
    dcq, d_q_norm = _rmsnorm_bwd(cq, w["q_norm"], dcqn, out_dtype=_ACT, name=f"{tag}_dqnorm")
    dckv, d_kv_norm = _rmsnorm_bwd(ckv, w["kv_norm"], dckvn, out_dtype=_ACT, name=f"{tag}_dkvnorm")
    lane = jnp.arange(LANES) < MLA_ROPE
    dza = jnp.concatenate([dcq, dckv, jnp.where(lane[None], dkr, 0.0).astype(_ACT)], axis=-1)

    dcomb = _heads(dcat[:, nv:], DIL_SLOTS)
    do3, delta3 = _merge_bwd(dcomb, o3, lse3, name=f"{tag}_dmerge")
    dqs, dks, dvs = [], [], []
    for gi, (_, dil) in enumerate(DIL_PAIRS):
        hs = slice(gi * DIL_SLOTS, (gi + 1) * DIL_SLOTS)
        qr, kr_, vr, l = dil_saved[gi]
        grads = (qr, kr_, vr, _tok_to_res(do3[gi], dil), l, _tok_to_res(delta3[gi], dil), slopes[hs])
        dqs.append(_from_res(_win_bwd_dq(*grads, dil=dil, name=f"{tag}_dil{gi}_dq"), dil))
        b, c = _win_bwd_dkv(*grads, dil=dil, name=f"{tag}_dil{gi}_dkv")
        dks.append(_from_res(b, dil))
        dvs.append(_from_res(c, dil))
    dzb = jnp.concatenate(dqs + dks + dvs, axis=-1).astype(_ACT)

    d_w_a = _matmul(hn, dza, trans_a=True, name=f"{tag}_dwa")
    d_w_b = _matmul(hn, dzb, trans_a=True, name=f"{tag}_dwb")
    dhn = _matmul(dza, w["w_a"], trans_b=True, name=f"{tag}_dhn_a")
    dhn = _matmul(dzb, w["w_b"], trans_b=True, res=dhn, name=f"{tag}_dhn_b")
    dx, dnorm = _rmsnorm_bwd(x, w["norm"], dhn, dout, name=f"{tag}_dnorm")
    grads = {"norm": dnorm, "w_a": d_w_a, "w_b": d_w_b, "q_norm": d_q_norm, "kv_norm": d_kv_norm,
             "w_uq": d_w_uq, "w_ukv": d_w_ukv, "w_out": d_w_out}
    return dx, grads, (received[0] if received else None)


def _odd_fwd(x, w, tabs, tag):
    s = x.shape[0]
    _, (gqa_c, gqa_s) = tabs
    nk = GQA_KV_HEADS * GQA_HEAD_DIM
    hn = _rmsnorm(x, w["norm"], out_dtype=_ACT, name=f"{tag}_norm")
    q = _matmul(hn, w["w_q"], name=f"{tag}_q")
    kv = _matmul(hn, w["w_kv"], name=f"{tag}_kv")
    k = kv[:, :nk]
    qh = _heads(_headnorm_rope(q, w["q_norm"], gqa_c, gqa_s, scale=GQA_HEAD_DIM ** -0.5 * LOG2_E,
                               name=f"{tag}_prep_q"), GQA_HEADS)
    kh = _heads(_headnorm_rope(k, w["k_norm"], gqa_c, gqa_s, name=f"{tag}_prep_k"), GQA_KV_HEADS)
    vh = _heads(kv[:, nk:].astype(_ACT), GQA_KV_HEADS)
    o, lse = _attn_fwd(qh, kh, vh, out_dtype=_ACT, name=f"{tag}_gqa")
    ocat = _unheads(o)
    out = _matmul(ocat, w["w_o"], res=x, name=f"{tag}_out")
    return out, (x, hn, q, k, qh, kh, vh, o, lse, ocat)


def _odd_bwd(dout, saved, w, tabs, tag):
    x, hn, q, k, qh, kh, vh, o, lse, ocat = saved
    s = x.shape[0]
    _, (gqa_c, gqa_s) = tabs
    docat = _matmul(dout, w["w_o"], trans_b=True, out_dtype=_ACT, name=f"{tag}_docat")
    d_w_o = _matmul(ocat, dout, trans_a=True, name=f"{tag}_dwo")
    doh = _heads(docat, GQA_HEADS)
    dqh, dkt, dvt = _attn_bwd(qh, kh, vh, o, doh, lse, scale=GQA_HEAD_DIM ** -0.5, name=f"{tag}_gqa_bwd")
    dq, d_q_norm = _headnorm_rope_bwd(q, w["q_norm"], _unheads(dqh), gqa_c, gqa_s, name=f"{tag}_dprep_q")
    dk, d_k_norm = _headnorm_rope_bwd(k, w["k_norm"], _unchunk(dkt), gqa_c, gqa_s, name=f"{tag}_dprep_k")
    dkv = jnp.concatenate([dk, _unchunk(dvt).astype(_ACT)], axis=-1)
    d_w_q = _matmul(hn, dq, trans_a=True, name=f"{tag}_dwq")
    d_w_kv = _matmul(hn, dkv, trans_a=True, name=f"{tag}_dwkv")
    dhn = _matmul(dq, w["w_q"], trans_b=True, name=f"{tag}_dhn_q")
    dhn = _matmul(dkv, w["w_kv"], trans_b=True, res=dhn, name=f"{tag}_dhn_kv")
    dx, dnorm = _rmsnorm_bwd(x, w["norm"], dhn, dout, name=f"{tag}_dnorm")
    return dx, {"norm": dnorm, "w_q": d_w_q, "w_kv": d_w_kv, "q_norm": d_q_norm, "k_norm": d_k_norm, "w_o": d_w_o}


def _split_heads_cols(wm, heads, first):
    rws = wm.shape[0]
    w3 = wm.reshape(rws, heads, -1)
    return jnp.concatenate([w3[:, :, :first].reshape(rws, -1), w3[:, :, first:].reshape(rws, -1)], axis=-1)


def _merge_heads_cols(wm, heads, first):
    rws, cols = wm.shape
    a = wm[:, :heads * first].reshape(rws, heads, first)
    b = wm[:, heads * first:].reshape(rws, heads, cols // heads - first)
    return jnp.concatenate([a, b], axis=-1).reshape(rws, cols)


def _layer_weights(full, gains, layer):
    i = layer // 2

    def stacked(name, idx):
        for group, arrays in zip((_FIRST, _REST), full):
            lo, cnt = group.get(name, (0, 0))
            if lo <= idx < lo + cnt:
                return arrays[name][idx - lo]
        raise KeyError((name, idx))

    w_ffn_in = stacked("ffn_w_in", layer)
    ffn = {"norm": gains["ffn_norm"][layer], "w_gate": w_ffn_in[:, :FFN_HIDDEN], "w_up": w_ffn_in[:, FFN_HIDDEN:],
           "w_out": stacked("ffn_w_out", layer)}
    if layer % 2 == 0:
        w_in = stacked("w_in_ab", i)
        mix = {"norm": gains["mix_norm_ab"][i],
               "w_a": jnp.pad(w_in[:, :IN_A], ((0, 0), (0, IN_A_PAD - IN_A))), "w_b": w_in[:, IN_A:],
               "q_norm": gains["mla_q_norm"][i], "kv_norm": gains["mla_kv_norm"][i],
               "w_uq": _split_heads_cols(stacked("mla_w_uq", i), MLA_HEADS, MLA_NOPE),
               "w_ukv": _split_heads_cols(stacked("mla_w_ukv", i), MLA_HEADS, MLA_NOPE),
               "w_out": stacked("w_out_ab", i)}
    else:
        mix = {"norm": gains["mix_norm_c"][i], "w_q": stacked("gqa_w_q", i), "w_kv": stacked("gqa_w_kv", i),
               "q_norm": gains["gqa_q_norm"][i], "k_norm": gains["gqa_k_norm"][i], "w_o": stacked("gqa_w_o", i)}
    return mix, ffn


def _pack_grads(grads, group):
    parts = jnp.concatenate([_split_for_devices(jnp.stack(grads[n][group[n][0]:group[n][0] + group[n][1]]), axis)
                             for n, axis in _BIG if n in group], axis=1)
    return jnp.pad(parts, ((0, 0), (0, -parts.shape[1] % PACK_ROWS), (0, 0)))


def kernel(x, mix_norm_ab, w_in_ab, mla_q_norm, mla_kv_norm, mla_w_uq, mla_w_ukv, w_out_ab, mix_norm_c, gqa_w_q, gqa_w_kv, gqa_q_norm, gqa_k_norm, gqa_w_o, ffn_norm, ffn_w_in, ffn_w_out, final_norm, loss_target, m_mix_norm_ab, m_w_in_ab, m_mla_q_norm, m_mla_kv_norm, m_mla_w_uq, m_mla_w_ukv, m_w_out_ab, m_mix_norm_c, m_gqa_w_q, m_gqa_w_kv, m_gqa_q_norm, m_gqa_k_norm, m_gqa_w_o, m_ffn_norm, m_ffn_w_in, m_ffn_w_out, m_final_norm, v_mix_norm_ab, v_w_in_ab, v_mla_q_norm, v_mla_kv_norm, v_mla_w_uq, v_mla_w_ukv, v_w_out_ab, v_mix_norm_c, v_gqa_w_q, v_gqa_w_kv, v_gqa_q_norm, v_gqa_k_norm, v_gqa_w_o, v_ffn_norm, v_ffn_w_in, v_ffn_w_out, v_final_norm):
    wts = dict(mix_norm_ab=mix_norm_ab, w_in_ab=w_in_ab, mla_q_norm=mla_q_norm, mla_kv_norm=mla_kv_norm,
               mla_w_uq=mla_w_uq, mla_w_ukv=mla_w_ukv, w_out_ab=w_out_ab, mix_norm_c=mix_norm_c, gqa_w_q=gqa_w_q,
               gqa_w_kv=gqa_w_kv, gqa_q_norm=gqa_q_norm, gqa_k_norm=gqa_k_norm, gqa_w_o=gqa_w_o, ffn_norm=ffn_norm,
               ffn_w_in=ffn_w_in, ffn_w_out=ffn_w_out, final_norm=final_norm)
    mom = dict(mix_norm_ab=m_mix_norm_ab, w_in_ab=m_w_in_ab, mla_q_norm=m_mla_q_norm, mla_kv_norm=m_mla_kv_norm,
               mla_w_uq=m_mla_w_uq, mla_w_ukv=m_mla_w_ukv, w_out_ab=m_w_out_ab, mix_norm_c=m_mix_norm_c,
               gqa_w_q=m_gqa_w_q, gqa_w_kv=m_gqa_w_kv, gqa_q_norm=m_gqa_q_norm, gqa_k_norm=m_gqa_k_norm,
               gqa_w_o=m_gqa_w_o, ffn_norm=m_ffn_norm, ffn_w_in=m_ffn_w_in, ffn_w_out=m_ffn_w_out,
               final_norm=m_final_norm)
    var = dict(mix_norm_ab=v_mix_norm_ab, w_in_ab=v_w_in_ab, mla_q_norm=v_mla_q_norm, mla_kv_norm=v_mla_kv_norm,
               mla_w_uq=v_mla_w_uq, mla_w_ukv=v_mla_w_ukv, w_out_ab=v_w_out_ab, mix_norm_c=v_mix_norm_c,
               gqa_w_q=v_gqa_w_q, gqa_w_kv=v_gqa_w_kv, gqa_q_norm=v_gqa_q_norm, gqa_k_norm=v_gqa_k_norm,
               gqa_w_o=v_gqa_w_o, ffn_norm=v_ffn_norm, ffn_w_in=v_ffn_w_in, ffn_w_out=v_ffn_w_out,
               final_norm=v_final_norm)
    big_names = [n for n, _ in _BIG]
    groups = (_FIRST, _REST)
    w_grp = [_group_of(wts, grp) for grp in groups]
    shapes = [{n: a.shape for n, a in w.items()} for w in w_grp]
    w_packs = [_pack_big(w) for w in w_grp]
    n_rows = [p.shape[0] for p in w_packs]
    xs = x[0]
    s = xs.shape[0]
    me = 4 * lax.axis_index("x") + 2 * lax.axis_index("y") + lax.axis_index("c")
    c_cols = mix_norm_c.shape[1]

    tail = jnp.pad(mix_norm_c.reshape(-1), (0, GATHER_TAIL_ROWS // 2 * PACK_COLS - mix_norm_c.size))
    tail = _to_bits16(tail.reshape(GATHER_TAIL_ROWS // 2, PACK_COLS))
    gathered = _all_gather(jnp.concatenate([w_packs[0].astype(jnp.bfloat16), tail], axis=0))
    full = [_unpack_gathered(gathered[:, :n_rows[0]], shapes[0]), None]
    c_all = _from_bits16(gathered[:, n_rows[0]:]).reshape(N_DEV, -1)[:, :mix_norm_c.size]
    c_full = c_all.reshape(N_DEV, 2, c_cols).transpose(1, 0, 2).reshape(2, N_DEV * c_cols)
    gains = dict(mix_norm_ab=mix_norm_ab, mla_q_norm=mla_q_norm, mla_kv_norm=mla_kv_norm, mix_norm_c=c_full,
                 gqa_q_norm=gqa_q_norm, gqa_k_norm=gqa_k_norm, ffn_norm=ffn_norm)

    tabs = _rope_tables(s)
    slopes = jnp.exp2(-8.0 * jnp.arange(1, DIL_HEADS + 1, dtype=_F32) / DIL_HEADS)

    h = xs
    saved = []
    for layer in range(DEPTH):
        mix_w, ffn_w = _layer_weights(full, gains, layer)
        if layer == 0:
            h, sv_mix, rest = _even_fwd(h, mix_w, tabs, slopes, f"l{layer}_mix", gather=w_packs[1].astype(jnp.bfloat16))
            full[1] = _unpack_gathered(rest, shapes[1])
        elif layer % 2 == 0:
            h, sv_mix, _ = _even_fwd(h, mix_w, tabs, slopes, f"l{layer}_mix")
        else:
            h, sv_mix = _odd_fwd(h, mix_w, tabs, f"l{layer}_mix")
        h, sv_ffn = _ffn_fwd(h, ffn_w, f"l{layer}_ffn")
        saved.append((mix_w, ffn_w, sv_mix, sv_ffn))
    loss_local, dh, d_final = _final_loss(h, final_norm, loss_target[0], name="final_loss")

    gfull = {n: [None] * wts[n].shape[0] for n in big_names}
    gsmall = {n: [None] * (wts[n].shape[0] if wts[n].ndim > 1 else 1) for n in _SMALL}
    gsmall["final_norm"][0] = d_final
    for layer in reversed(range(DEPTH)):
        mix_w, ffn_w, sv_mix, sv_ffn = saved[layer]
        i = layer // 2
        dh, gf = _ffn_bwd(dh, sv_ffn, ffn_w, f"l{layer}_ffn")
        gsmall["ffn_norm"][layer] = gf["norm"]
        gfull["ffn_w_in"][layer] = jnp.concatenate([gf["w_gate"], gf["w_up"]], axis=-1)
        gfull["ffn_w_out"][layer] = gf["w_out"]
        if layer % 2 == 0:
            sending = _pack_grads(gfull, _REST).astype(jnp.bfloat16) if layer == 0 else None
            dh, gm, got = _even_bwd(dh, sv_mix, mix_w, tabs, slopes, f"l{layer}_mix", exchange=sending)
            if layer == 0:
                received_rest = got
            gsmall["mix_norm_ab"][i] = gm["norm"]
            gsmall["mla_q_norm"][i] = gm["q_norm"]
            gsmall["mla_kv_norm"][i] = gm["kv_norm"]
            gfull["w_in_ab"][i] = jnp.concatenate([gm["w_a"][:, :IN_A], gm["w_b"]], axis=-1)
            gfull["mla_w_uq"][i] = _merge_heads_cols(gm["w_uq"], MLA_HEADS, MLA_NOPE)
            gfull["mla_w_ukv"][i] = _merge_heads_cols(gm["w_ukv"], MLA_HEADS, MLA_NOPE)
            gfull["w_out_ab"][i] = gm["w_out"]
        else:
            dh, gm = _odd_bwd(dh, sv_mix, mix_w, tabs, f"l{layer}_mix")
            gsmall["mix_norm_c"][i] = gm["norm"]
            gsmall["gqa_q_norm"][i] = gm["q_norm"]
            gsmall["gqa_k_norm"][i] = gm["k_norm"]
            gfull["gqa_w_q"][i] = gm["w_q"]
            gfull["gqa_w_kv"][i] = gm["w_kv"]
            gfull["gqa_w_o"][i] = gm["w_o"]
    grad_x = dh[None]

    small_part = _pack_small({n: jnp.stack(gsmall[n]) for n in _SMALL})
    small_bits = jnp.broadcast_to(_to_bits16(small_part)[None], (N_DEV, EXCHANGE_TAIL_ROWS, PACK_COLS))
    received_first = _exchange(jnp.concatenate([_pack_grads(gfull, _FIRST).astype(jnp.bfloat16), small_bits], axis=1))
    big_packs = [_reduce_adamw(got, w_packs[gi], _pack_big(_group_of(mom, grp)), _pack_big(_group_of(var, grp)),
                               name=f"adamw_big{gi}")
                 for gi, (grp, got) in enumerate(zip(groups, (received_first, received_rest)))]

    def widen_c(shard):
        return lax.dynamic_update_slice(jnp.zeros((2, N_DEV * c_cols), _F32), shard, (0, me * c_cols))

    def small_of(src):
        return _pack_small({n: (widen_c(src[n]) if n == "mix_norm_c" else src[n]) for n in _SMALL})

    small_recv = _from_bits16(received_first[:, n_rows[0]:])
    small_packs = _reduce_adamw(small_recv, small_of(wts), small_of(mom), small_of(var), name="adamw_small")

    def outputs_of(which):
        by_group = [_unpack_big(big_packs[gi][which], shapes[gi]) for gi in range(len(groups))]
        small = _unpack_small(small_packs[which])
        res = {}
        for n in wts:
            if n in big_names:
                res[n] = jnp.concatenate([grp[n] for grp in by_group if n in grp], axis=0)
            elif n == "mix_norm_c":
                res[n] = lax.dynamic_slice(small[n].reshape(2, N_DEV * c_cols), (0, me * c_cols), (2, c_cols))
            else:
                res[n] = small[n].reshape(wts[n].shape)
        return [res[n] for n in wts]

    loss = lax.psum(loss_local, _AXES)
    return (loss, grad_x, *outputs_of(0), *outputs_of(1), *outputs_of(2), *outputs_of(3))
```

```python
import functools

import jax
import jax.numpy as jnp
from jax import lax
from jax.experimental import pallas as pl
from jax.experimental.pallas import tpu as pltpu

D_MODEL = 1024
DEPTH = 4
GRID_W = 64
NORM_EPS = 1e-6
ROPE_THETA = 10000.0
NEG_INF = -1e30
MLA_HEADS = 8
MLA_Q_RANK = 384
MLA_KV_RANK = 256
MLA_NOPE = 64
MLA_ROPE = 32
MLA_V = 64
DIL_PAIRS = ((128, 1), (512, 4), (2048, 16))
DIL_HALF = 64
DIL_SLOTS = 4
DIL_GROUPS = 3
DIL_HEADS = 12
DIL_HEAD_DIM = 64
GQA_HEADS = 16
GQA_KV_HEADS = 4
GQA_HEAD_DIM = 64
FFN_HIDDEN = 2816
IN_A = MLA_Q_RANK + MLA_KV_RANK + MLA_ROPE
IN_A_PAD = 768
IN_B = 3 * DIL_HEADS * DIL_HEAD_DIM
ADAM_LR = 0.001
ADAM_B1 = 0.9
ADAM_B2 = 0.999
ADAM_EPS = 1e-08
ADAM_WD = 0.01
ADAM_STEP = 10

LANES = 128
SUBLANES_16BIT = 16
VMEM_LIMIT_BYTES = 56 * 1024 * 1024

MM_TILE = 1408

N_DEV = 8
PACK_COLS = 1024

_MXU = jnp.bfloat16
_ACT = jnp.bfloat16
_F32 = jnp.float32

_AXES = ("x", "y", "c")


def _params(sem):
    return pltpu.CompilerParams(dimension_semantics=sem, vmem_limit_bytes=VMEM_LIMIT_BYTES)


def _pick(n, cap):
    for t in range(cap - cap % LANES, 0, -LANES):
        if n % t == 0:
            return t
    return n


def _rows(m, target):
    t = m
    while t > target and t % 2 == 0:
        t //= 2
    return t


def _matmul(a, b, *, trans_a=False, trans_b=False, res=None, scale=None, out_dtype=_F32, name):
    if trans_a:
        k, m = a.shape
    else:
        m, k = a.shape
    if trans_b:
        n, kb = b.shape
    else:
        kb, n = b.shape
    assert k == kb, (a.shape, b.shape)
    tm, tn, tk = _pick(m, MM_TILE), _pick(n, MM_TILE), _pick(k, MM_TILE)
    nk = k // tk
    dims = (((0 if trans_a else 1,), (1 if trans_b else 0,)), ((), ()))

    def body(*refs):
        if res is None:
            a_ref, b_ref, o_ref, acc = refs
            r_ref = None
        else:
            a_ref, b_ref, r_ref, o_ref, acc = refs
        kk = pl.program_id(2)

        @pl.when(kk == 0)
        def _():
            acc[...] = jnp.zeros_like(acc)

        acc[...] += lax.dot_general(a_ref[...].astype(_MXU), b_ref[...].astype(_MXU), dims,
                                    preferred_element_type=_F32)

        @pl.when(kk == nk - 1)
        def _():
            r = acc[...]
            if scale is not None:
                r = r * scale
            if r_ref is not None:
                r = r + r_ref[...].astype(_F32)
            o_ref[...] = r.astype(out_dtype)

    a_spec = (pl.BlockSpec((tk, tm), lambda i, j, kk: (kk, i)) if trans_a
              else pl.BlockSpec((tm, tk), lambda i, j, kk: (i, kk)))
    b_spec = (pl.BlockSpec((tn, tk), lambda i, j, kk: (j, kk)) if trans_b
              else pl.BlockSpec((tk, tn), lambda i, j, kk: (kk, j)))
    o_spec = pl.BlockSpec((tm, tn), lambda i, j, kk: (i, j))
    in_specs = [a_spec, b_spec] + ([o_spec] if res is not None else [])
    args = (a, b) + ((res,) if res is not None else ())
    return pl.pallas_call(
        body, name=name, grid=(m // tm, n // tn, nk),
        in_specs=in_specs, out_specs=o_spec,
        out_shape=jax.ShapeDtypeStruct((m, n), out_dtype),
        scratch_shapes=[pltpu.VMEM((tm, tn), _F32)],
        compiler_params=_params(("parallel", "parallel", "arbitrary")),
    )(*args)


def _rmsnorm(x, g, *, out_dtype, name, rows=512):
    m, d = x.shape
    tm = _rows(m, rows)

    def body(x_ref, g_ref, o_ref):
        xf = x_ref[...].astype(_F32)
        r = lax.rsqrt(jnp.mean(xf * xf, axis=-1, keepdims=True) + NORM_EPS)
        o_ref[...] = ((xf * r) * g_ref[...]).astype(out_dtype)

    return pl.pallas_call(
        body, name=name, grid=(m // tm,),
        in_specs=[pl.BlockSpec((tm, d), lambda i: (i, 0)), pl.BlockSpec((1, d), lambda i: (0, 0))],
        out_specs=pl.BlockSpec((tm, d), lambda i: (i, 0)),
        out_shape=jax.ShapeDtypeStruct((m, d), out_dtype),
        compiler_params=_params(("parallel",)),
    )(x, g.reshape(1, d).astype(_F32))


def _rmsnorm_bwd(x, g, dy, dres=None, *, out_dtype=_F32, name, rows=512):
    m, d = x.shape
    tm = _rows(m, rows)

    def body(*refs):
        if dres is None:
            x_ref, g_ref, dy_ref, dx_ref, dg_ref = refs
            r_ref = None
        else:
            x_ref, g_ref, dy_ref, r_ref, dx_ref, dg_ref = refs

        @pl.when(pl.program_id(0) == 0)
        def _():
            dg_ref[...] = jnp.zeros_like(dg_ref)

        xf = x_ref[...].astype(_F32)
        r = lax.rsqrt(jnp.mean(xf * xf, axis=-1, keepdims=True) + NORM_EPS)
        xh = xf * r
        dyf = dy_ref[...].astype(_F32)
        dg_ref[...] += jnp.sum(dyf * xh, axis=0, keepdims=True)
        gdy = dyf * g_ref[...]
        dx = r * (gdy - xh * jnp.mean(gdy * xh, axis=-1, keepdims=True))
        if r_ref is not None:
            dx = dx + r_ref[...].astype(_F32)
        dx_ref[...] = dx.astype(out_dtype)

    row = pl.BlockSpec((tm, d), lambda i: (i, 0))
    vec = pl.BlockSpec((1, d), lambda i: (0, 0))
    in_specs = [row, vec, row] + ([row] if dres is not None else [])
    args = (x, g.reshape(1, d).astype(_F32), dy) + ((dres,) if dres is not None else ())
    dx, dg = pl.pallas_call(
        body, name=name, grid=(m // tm,),
        in_specs=in_specs, out_specs=[row, vec],
        out_shape=[jax.ShapeDtypeStruct((m, d), out_dtype), jax.ShapeDtypeStruct((1, d), _F32)],
        compiler_params=_params(("arbitrary",)),
    )(*args)
    return dx, dg.reshape(d)


def _rotate(xf, c, sn):
    w = xf.shape[1]
    if w > LANES:
        c, sn = jnp.tile(c, (1, w // LANES)), jnp.tile(sn, (1, w // LANES))
    lane = lax.broadcasted_iota(jnp.int32, xf.shape, 1)
    sw = jnp.where((lane & 31) < 16, pltpu.roll(xf, w - 16, 1), pltpu.roll(xf, 16, 1))
    return xf * c + sw * sn


def _seg_mean(v, seg_ref):
    outs = []
    for c in range(v.shape[1] // LANES):
        piece = v[:, c * LANES:(c + 1) * LANES]
        hi = piece.astype(jnp.bfloat16)
        lo = (piece - hi.astype(_F32)).astype(jnp.bfloat16)
        outs.append(jnp.dot(hi, seg_ref[...], preferred_element_type=_F32)
                    + jnp.dot(lo, seg_ref[...], preferred_element_type=_F32))
    return jnp.concatenate(outs, axis=1) if len(outs) > 1 else outs[0]


def _seg_matrix():
    lane = jnp.arange(LANES) // GQA_HEAD_DIM
    return ((lane[:, None] == lane[None, :]).astype(_F32) / GQA_HEAD_DIM).astype(jnp.bfloat16)


def _headnorm_rope(x, gain, cos_t, sin_t, *, scale=None, name):
    s, w = x.shape
    ts = _rows(s, 512)

    def body(x_ref, g_ref, seg_ref, c_ref, s_ref, o_ref):
        xf = x_ref[...]
        r = lax.rsqrt(_seg_mean(xf * xf, seg_ref) + NORM_EPS)
        y = _rotate((xf * r) * g_ref[...], c_ref[...], s_ref[...])
        if scale is not None:
            y = y * scale
        o_ref[...] = y.astype(o_ref.dtype)

    row = pl.BlockSpec((ts, w), lambda i: (i, 0))
    tab = pl.BlockSpec((ts, LANES), lambda i: (i, 0))
    return pl.pallas_call(
        body, name=name, grid=(s // ts,),
        in_specs=[row, pl.BlockSpec((1, w), lambda i: (0, 0)), pl.BlockSpec((LANES, LANES), lambda i: (0, 0)), tab, tab],
        out_specs=row, out_shape=jax.ShapeDtypeStruct((s, w), _ACT),
        compiler_params=_params(("parallel",)),
    )(x, jnp.tile(gain.astype(_F32), w // GQA_HEAD_DIM).reshape(1, w), _seg_matrix(), cos_t, sin_t)


def _headnorm_rope_bwd(x, gain, dy, cos_t, sin_t, *, name):
    s, w = x.shape
    ts = _rows(s, 512)

    def body(x_ref, g_ref, seg_ref, c_ref, s_ref, dy_ref, dx_ref, dg_ref):
        @pl.when(pl.program_id(0) == 0)
        def _():
            dg_ref[...] = jnp.zeros_like(dg_ref)

        xf = x_ref[...]
        r = lax.rsqrt(_seg_mean(xf * xf, seg_ref) + NORM_EPS)
        xh = xf * r
        dyn = _rotate(dy_ref[...].astype(_F32), c_ref[...], -s_ref[...])
        dg_ref[...] += jnp.sum(dyn * xh, axis=0, keepdims=True)
        gdy = dyn * g_ref[...]
        dx_ref[...] = (r * (gdy - xh * _seg_mean(gdy * xh, seg_ref))).astype(dx_ref.dtype)

    row = pl.BlockSpec((ts, w), lambda i: (i, 0))
    vec = pl.BlockSpec((1, w), lambda i: (0, 0))
    tab = pl.BlockSpec((ts, LANES), lambda i: (i, 0))
    dx, dg = pl.pallas_call(
        body, name=name, grid=(s // ts,),
        in_specs=[row, vec, pl.BlockSpec((LANES, LANES), lambda i: (0, 0)), tab, tab, row],
        out_specs=[row, vec],
        out_shape=[jax.ShapeDtypeStruct((s, w), _ACT), jax.ShapeDtypeStruct((1, w), _F32)],
        compiler_params=_params(("arbitrary",)),
    )(x, jnp.tile(gain.astype(_F32), w // GQA_HEAD_DIM).reshape(1, w), _seg_matrix(), cos_t, sin_t, dy)
    return dx, dg.reshape(w // GQA_HEAD_DIM, GQA_HEAD_DIM).sum(axis=0)


def _rope(x, cos_t, sin_t, *, out_dtype, name, sum_chunks=False, scale=None):
    s, w = x.shape
    assert w % LANES == 0
    ts = _rows(s, 512)
    ow = LANES if sum_chunks else w

    def body(x_ref, c_ref, s_ref, o_ref):
        y = _rotate(x_ref[...].astype(_F32), c_ref[...], s_ref[...])
        if scale is not None:
            y = y * scale
        if sum_chunks:
            shift = w // 2
            while shift >= 32:
                y = y + pltpu.roll(y, shift, 1)
                shift //= 2
            y = y[:, :LANES]
        o_ref[...] = y.astype(out_dtype)

    return pl.pallas_call(
        body, name=name, grid=(s // ts,),
        in_specs=[pl.BlockSpec((ts, w), lambda i: (i, 0)), pl.BlockSpec((ts, LANES), lambda i: (i, 0)),
                  pl.BlockSpec((ts, LANES), lambda i: (i, 0))],
        out_specs=pl.BlockSpec((ts, ow), lambda i: (i, 0)),
        out_shape=jax.ShapeDtypeStruct((s, ow), out_dtype),
        compiler_params=_params(("parallel",)),
    )(x, cos_t, sin_t)


_NT = (((1,), (1,)), ((), ()))
_TN = (((0,), (0,)), ((), ()))
LOG2_E = 1.4426950408889634
LN_2 = 0.6931471805599453
ATTN_FWD_ROWS = 2048
ATTN_ROWS = 512
ATTN_CHAINS = 2
ATTN_FWD_KEYS = 1024
ATTN_BWD_KEYS = 1024


def _grid_step(dims):
    step, total = 0, 1
    for axis, n in enumerate(dims):
        step = step * n + pl.program_id(axis)
        total *= n
    return step, total


def _attn_fwd(qt, k, vt, *, out_dtype, name, gather=None):
    h, dk, s = qt.shape
    g, nk, dv, tk = vt.shape
    assert nk * tk == s
    r = h // g
    rc = min(ATTN_FWD_ROWS, s)
    hp = min(r, ATTN_CHAINS)
    nrc = max(1, min(ATTN_CHAINS // hp, s // rc))
    tq = rc * nrc
    nhp, nq = r // hp, s // tq
    units = [(a, c) for a in range(hp) for c in range(nrc)]

    def body(q_ref, k_ref, v_ref, *rest):
        if gather is None:
            o_ref, lse_ref = rest
        else:
            x_ref, o_ref, lse_ref, all_ref, send_sems, recv_sems, local_sem = rest
            start, forward, finish = _gather_phases(x_ref, all_ref, send_sems, recv_sems, local_sem)
            step, total = _grid_step((g, nhp, nq))
            pl.when(step == 0)(start)
            pl.when(step == total // 2)(forward)

        qs = [q_ref[a, :, c * rc:(c + 1) * rc] for a, c in units]
        init = tuple((jnp.full((1, rc), NEG_INF, _F32), jnp.zeros((1, rc), _F32), jnp.zeros((dv, rc), _F32))
                     for _ in units)

        def trip(j, carry):
            kb, vb = k_ref[0, pl.ds(pl.multiple_of(j * tk, tk), tk), :], v_ref[0, j]
            out = []
            for u in range(len(units)):
                m, l, acc = carry[u]
                sc = jnp.dot(kb, qs[u], preferred_element_type=_F32)
                m_new = jnp.maximum(m, jnp.max(sc, axis=0, keepdims=True))
                p = jnp.exp2(sc - m_new)
                alpha = jnp.exp2(m - m_new)
                l = alpha * l + jnp.sum(p, axis=0, keepdims=True)
                acc = alpha * acc + jnp.dot(vb, p.astype(_MXU), preferred_element_type=_F32)
                out.append((m_new, l, acc))
            return tuple(out)

        fin = lax.fori_loop(0, nk, trip, init)
        for u, (a, c) in enumerate(units):
            m, l, acc = fin[u]
            o_ref[a, :, c * rc:(c + 1) * rc] = (acc / l).astype(out_dtype)
            lse_ref[a, :, c * rc:(c + 1) * rc] = m + jnp.log2(l)

        if gather is not None:
            pl.when(step == total - 1)(finish)

    q_blk = lambda gg, hh, i: (gg * nhp + hh, 0, i)
    in_specs = [pl.BlockSpec((hp, dk, tq), q_blk), pl.BlockSpec((1, s, dk), lambda gg, hh, i: (gg, 0, 0)),
                pl.BlockSpec((1, nk, dv, tk), lambda gg, hh, i: (gg, 0, 0, 0))]
    out_specs = [pl.BlockSpec((hp, dv, tq), q_blk), pl.BlockSpec((hp, 1, tq), q_blk)]
    out_shape = [jax.ShapeDtypeStruct((h, dv, s), out_dtype), jax.ShapeDtypeStruct((h, 1, s), _F32)]
    if gather is None:
        return pl.pallas_call(
            body, name=name, grid=(g, nhp, nq), in_specs=in_specs, out_specs=out_specs, out_shape=out_shape,
            compiler_params=_params(("parallel", "parallel", "parallel")),
        )(qt, k, vt)
    return pl.pallas_call(
        body, name=name, grid=(g, nhp, nq),
        in_specs=in_specs + [pl.BlockSpec(memory_space=pl.ANY)],
        out_specs=out_specs + [pl.BlockSpec(memory_space=pl.ANY)],
        out_shape=out_shape + [jax.ShapeDtypeStruct((N_DEV,) + gather.shape, gather.dtype)],
        scratch_shapes=_comm_scratch(),
        compiler_params=_params(("arbitrary", "arbitrary", "arbitrary")),
    )(qt, k, vt, gather)


def _attn_bwd(qt, k, v, ot, do, lse2, *, scale, name, exchange=None):
    h, dk, s = qt.shape
    g, _, dv = v.shape
    r = h // g
    hp = min(r, ATTN_CHAINS)
    tq = min(ATTN_ROWS, s)
    tk = min(ATTN_BWD_KEYS * (ATTN_CHAINS // hp), s)
    nhp, nq, nk = r // hp, s // tq, s // tk

    def body(qt_ref, k_ref, v_ref, ot_ref, do_ref, lse_ref, *rest):
        if exchange is None:
            dq_ref, dk_ref, dv_ref = rest
        else:
            g_ref, dq_ref, dk_ref, dv_ref, got_ref, send_sems, recv_sems, local_sem = rest
            start, finish = _exchange_phases(g_ref, got_ref, send_sems, recv_sems, local_sem)
            step, total = _grid_step((g, nhp, nq))
            pl.when(step == 0)(start)
        hh, i = pl.program_id(1), pl.program_id(2)

        @pl.when((hh == 0) & (i == 0))
        def _():
            dk_ref[...] = jnp.zeros_like(dk_ref)
            dv_ref[...] = jnp.zeros_like(dv_ref)

        qts = [qt_ref[a] for a in range(hp)]
        qs = [qt_ref[a].T for a in range(hp)]
        dos = [do_ref[a] for a in range(hp)]
        dots = [do_ref[a].T for a in range(hp)]
        ls = [lse_ref[a].T for a in range(hp)]
        dls = [jnp.sum(do_ref[a].astype(_F32) * ot_ref[a].T.astype(_F32), axis=1, keepdims=True)
               for a in range(hp)]

        def trip(j, carry):
            rows = pl.ds(pl.multiple_of(j * tk, tk), tk)
            kb, vb = k_ref[0, rows, :], v_ref[0, rows, :]
            out = []
            for a in range(hp):
                p = jnp.exp2(lax.dot_general(qs[a], kb, _NT, preferred_element_type=_F32) - ls[a])
                dp = lax.dot_general(dos[a], vb, _NT, preferred_element_type=_F32)
                ds = (p * (dp - dls[a])).astype(_MXU)
                dv_ref[0, j] += jnp.dot(dots[a], p.astype(_MXU), preferred_element_type=_F32)
                dk_ref[0, j] += jnp.dot(qts[a], ds, preferred_element_type=_F32)
                out.append(carry[a] + jnp.dot(ds, kb, preferred_element_type=_F32))
            return tuple(out)

        fin = lax.fori_loop(0, nk, trip, tuple(jnp.zeros((tq, dk), _F32) for _ in range(hp)))
        for a in range(hp):
            dq_ref[a] = fin[a] * scale

        @pl.when((hh == nhp - 1) & (i == nq - 1))
        def _():
            dk_ref[...] = dk_ref[...] * LN_2

        if exchange is not None:
            pl.when(step == total - 1)(finish)

    q_blk = lambda gg, hh, i: (gg * nhp + hh, i, 0)
    qt_blk = lambda gg, hh, i: (gg * nhp + hh, 0, i)
    kv_blk = lambda gg, hh, i: (gg, 0, 0)
    in_specs = [pl.BlockSpec((hp, dk, tq), qt_blk), pl.BlockSpec((1, s, dk), kv_blk), pl.BlockSpec((1, s, dv), kv_blk),
                pl.BlockSpec((hp, dv, tq), qt_blk), pl.BlockSpec((hp, tq, dv), q_blk), pl.BlockSpec((hp, 1, tq), qt_blk)]
    out_specs = [pl.BlockSpec((hp, tq, dk), q_blk), pl.BlockSpec((1, nk, dk, tk), lambda gg, hh, i: (gg, 0, 0, 0)),
                 pl.BlockSpec((1, nk, dv, tk), lambda gg, hh, i: (gg, 0, 0, 0))]
    out_shape = [jax.ShapeDtypeStruct((h, s, dk), _F32), jax.ShapeDtypeStruct((g, nk, dk, tk), _F32),
                 jax.ShapeDtypeStruct((g, nk, dv, tk), _F32)]
    args = (qt, k, v, ot, do, lse2)
    scratch = []
    if exchange is not None:
        in_specs, args = in_specs + [pl.BlockSpec(memory_space=pl.ANY)], args + (exchange,)
        out_specs = out_specs + [pl.BlockSpec(memory_space=pl.ANY)]
        out_shape = out_shape + [jax.ShapeDtypeStruct(exchange.shape, exchange.dtype)]
        scratch = _comm_scratch()
    return pl.pallas_call(
        body, name=name, grid=(g, nhp, nq), in_specs=in_specs, out_specs=out_specs, out_shape=out_shape,
        scratch_shapes=scratch, compiler_params=_params(("arbitrary", "arbitrary", "arbitrary")),
    )(*args)


def _unchunk(xt):
    g, nk, d, tk = xt.shape
    return xt.transpose(1, 3, 0, 2).reshape(nk * tk, g * d)


def _chunk_t(x2d, g):
    s = x2d.shape[0]
    tk = min(ATTN_FWD_KEYS, s)
    return x2d.reshape(s // tk, tk, g, -1).transpose(2, 0, 3, 1)


def _heads_t(x2d, h):
    s = x2d.shape[0]
    return x2d.reshape(s, h, -1).transpose(1, 2, 0)


def _unheads_t(xt):
    h, d, s = xt.shape
    return xt.transpose(2, 0, 1).reshape(s, h * d)


WIN_ROWS = 512


def _win_geometry(s, dil):
    t = min(WIN_ROWS, s)
    length = s // dil
    lg = length.bit_length() - 1
    assert 1 << lg == length and t % DIL_HALF == 0 and s % t == 0
    return t, t + 2 * DIL_HALF, lg, t // DIL_HALF, s // DIL_HALF


def _win_specs(hn, t, d, halo_per_blk, n_halo):
    return [pl.BlockSpec((hn, DIL_HALF, d), lambda i: (0, jnp.maximum(i * halo_per_blk - 1, 0), 0)),
            pl.BlockSpec((hn, t, d), lambda i: (0, i, 0)),
            pl.BlockSpec((hn, DIL_HALF, d), lambda i: (0, jnp.minimum((i + 1) * halo_per_blk, n_halo - 1), 0))]


def _win_mask(i, t, w, lg, dil, wide_rows):
    shape = (w, t) if wide_rows else (t, w)
    rows = lax.broadcasted_iota(jnp.int32, shape, 0)
    cols = lax.broadcasted_iota(jnp.int32, shape, 1)
    base = i * t
    if wide_rows:
        pq, pk = base - DIL_HALF + rows, base + cols
    else:
        pq, pk = base + rows, base - DIL_HALF + cols
    arel = jnp.abs(pk - pq)
    valid = (arel <= DIL_HALF) & ((pk >> lg) == (pq >> lg))
    return valid, (-dil * arel).astype(_F32)


def _win3(lo_ref, mid_ref, hi_ref, a):
    return jnp.concatenate([lo_ref[a], mid_ref[a], hi_ref[a]], axis=0)


def _win_fwd(q, k, v, slopes, *, dil, name):
    hn, s, d = q.shape
    t, w, lg, hpb, n_halo = _win_geometry(s, dil)
    scale = d ** -0.5

    def body(sl_ref, q_ref, klo, kmid, khi, vlo, vmid, vhi, o_ref, lse_ref):
        valid, nb = _win_mask(pl.program_id(0), t, w, lg, dil, False)
        for a in range(hn):
            kw, vw = _win3(klo, kmid, khi, a), _win3(vlo, vmid, vhi, a)
            sc = lax.dot_general(q_ref[a], kw, _NT, preferred_element_type=_F32) * scale
            sc = jnp.where(valid, sc + sl_ref[a] * nb, NEG_INF)
            m = jnp.max(sc, axis=1, keepdims=True)
            e = jnp.exp(sc - m)
            den = jnp.sum(e, axis=1, keepdims=True)
            o_ref[a] = jnp.dot(e.astype(_MXU), vw, preferred_element_type=_F32) / den
            lse_ref[a] = m + jnp.log(den)

    kv = _win_specs(hn, t, d, hpb, n_halo)
    blk = lambda c: pl.BlockSpec((hn, t, c), lambda i: (0, i, 0))
    return pl.pallas_call(
        body, name=name, grid=(s // t,),
        in_specs=[pl.BlockSpec(memory_space=pltpu.SMEM), blk(d)] + kv + kv,
        out_specs=[blk(d), blk(1)],
        out_shape=[jax.ShapeDtypeStruct((hn, s, d), _F32), jax.ShapeDtypeStruct((hn, s, 1), _F32)],
        compiler_params=_params(("parallel",)),
    )(slopes.astype(_F32), q, k, k, k, v, v, v)


def _win_bwd_dq(q, k, v, do, lse, delta, slopes, *, dil, name):
    hn, s, d = q.shape
    t, w, lg, hpb, n_halo = _win_geometry(s, dil)
    scale = d ** -0.5

    def body(sl_ref, q_ref, klo, kmid, khi, vlo, vmid, vhi, do_ref, lse_ref, dl_ref, dq_ref):
        valid, nb = _win_mask(pl.program_id(0), t, w, lg, dil, False)
        for a in range(hn):
            kw, vw = _win3(klo, kmid, khi, a), _win3(vlo, vmid, vhi, a)
            sc = lax.dot_general(q_ref[a], kw, _NT, preferred_element_type=_F32) * scale
            p = jnp.exp(jnp.where(valid, sc + sl_ref[a] * nb, NEG_INF) - lse_ref[a])
            dp = lax.dot_general(do_ref[a], vw, _NT, preferred_element_type=_F32)
            ds = (p * (dp - dl_ref[a])).astype(_MXU)
            dq_ref[a] = jnp.dot(ds, kw, preferred_element_type=_F32) * scale

    kv = _win_specs(hn, t, d, hpb, n_halo)
    blk = lambda c: pl.BlockSpec((hn, t, c), lambda i: (0, i, 0))
    return pl.pallas_call(
        body, name=name, grid=(s // t,),
        in_specs=[pl.BlockSpec(memory_space=pltpu.SMEM), blk(d)] + kv + kv + [blk(d), blk(1), blk(1)],
        out_specs=blk(d),
        out_shape=jax.ShapeDtypeStruct((hn, s, d), _F32),
        compiler_params=_params(("parallel",)),
    )(slopes.astype(_F32), q, k, k, k, v, v, v, do, lse, delta)


def _win_bwd_dkv(q, k, v, do, lse, delta, slopes, *, dil, name):
    hn, s, d = q.shape
    t, w, lg, hpb, n_halo = _win_geometry(s, dil)
    scale = d ** -0.5

    def body(sl_ref, qlo, qmid, qhi, dolo, domid, dohi, llo, lmid, lhi, dllo, dlmid, dlhi, k_ref, v_ref,
             dk_ref, dv_ref):
        valid, nb = _win_mask(pl.program_id(0), t, w, lg, dil, True)
        for a in range(hn):
            qw, dow = _win3(qlo, qmid, qhi, a), _win3(dolo, domid, dohi, a)
            lw, dlw = _win3(llo, lmid, lhi, a), _win3(dllo, dlmid, dlhi, a)
            sc = lax.dot_general(qw, k_ref[a], _NT, preferred_element_type=_F32) * scale
            p = jnp.exp(jnp.where(valid, sc + sl_ref[a] * nb, NEG_INF) - lw)
            dp = lax.dot_general(dow, v_ref[a], _NT, preferred_element_type=_F32)
            ds = (p * (dp - dlw)).astype(_MXU)
            dv_ref[a] = lax.dot_general(p.astype(_MXU), dow, _TN, preferred_element_type=_F32)
            dk_ref[a] = lax.dot_general(ds, qw, _TN, preferred_element_type=_F32) * scale

    blk = lambda c: pl.BlockSpec((hn, t, c), lambda i: (0, i, 0))
    return pl.pallas_call(
        body, name=name, grid=(s // t,),
        in_specs=([pl.BlockSpec(memory_space=pltpu.SMEM)] + _win_specs(hn, t, d, hpb, n_halo)
                  + _win_specs(hn, t, d, hpb, n_halo) + _win_specs(hn, t, 1, hpb, n_halo)
                  + _win_specs(hn, t, 1, hpb, n_halo) + [blk(d), blk(d)]),
        out_specs=[blk(d), blk(d)],
        out_shape=[jax.ShapeDtypeStruct((hn, s, d), _F32), jax.ShapeDtypeStruct((hn, s, d), _F32)],
        compiler_params=_params(("parallel",)),
    )(slopes.astype(_F32), q, q, q, do, do, do, lse, lse, lse, delta, delta, delta, k, v)


def _merge_weights(lse):
    mx = jnp.max(lse, axis=0, keepdims=True)
    e = jnp.exp(lse - mx)
    return e / jnp.sum(e, axis=0, keepdims=True)


def _merge_fwd(o3, lse3, *, name):
    ng, sl, s, d = o3.shape
    t = _rows(s, 512)

    def body(o_ref, l_ref, c_ref):
        wts = _merge_weights(l_ref[:, 0])
        c_ref[0] = jnp.sum(wts * o_ref[:, 0], axis=0).astype(c_ref.dtype)

    return pl.pallas_call(
        body, name=name, grid=(sl, s // t),
        in_specs=[pl.BlockSpec((ng, 1, t, d), lambda a, i: (0, a, i, 0)),
                  pl.BlockSpec((ng, 1, t, 1), lambda a, i: (0, a, i, 0))],
        out_specs=pl.BlockSpec((1, t, d), lambda a, i: (a, i, 0)),
        out_shape=jax.ShapeDtypeStruct((sl, s, d), _ACT),
        compiler_params=_params(("parallel", "parallel")),
    )(o3, lse3)


def _merge_bwd(dcomb, o3, lse3, *, name):
    ng, sl, s, d = o3.shape
    t = _rows(s, 512)

    def body(dc_ref, o_ref, l_ref, do_ref, dl_ref):
        wts = _merge_weights(l_ref[:, 0])
        dc = dc_ref[0].astype(_F32)
        comb = jnp.sum(wts * o_ref[:, 0], axis=0)
        do_ref[:, 0] = (wts * dc[None]).astype(do_ref.dtype)
        dl_ref[:, 0] = wts * jnp.sum(dc * comb, axis=-1, keepdims=True)[None]

    big = pl.BlockSpec((ng, 1, t, d), lambda a, i: (0, a, i, 0))
    small = pl.BlockSpec((ng, 1, t, 1), lambda a, i: (0, a, i, 0))
    return pl.pallas_call(
        body, name=name, grid=(sl, s // t),
        in_specs=[pl.BlockSpec((1, t, d), lambda a, i: (a, i, 0)), big, small],
        out_specs=[big, small],
        out_shape=[jax.ShapeDtypeStruct((ng, sl, s, d), _ACT), jax.ShapeDtypeStruct((ng, sl, s, 1), _F32)],
        compiler_params=_params(("parallel", "parallel")),
    )(dcomb, o3, lse3)


SWIGLU_ROWS = 512


def _gate_up(a, w_gate, w_up, *, name):
    m, k = a.shape
    n = w_gate.shape[1]
    tm, tn, tk = _pick(m, SWIGLU_ROWS), _pick(n, MM_TILE), _pick(k, MM_TILE)
    nk = k // tk

    def body(a_ref, g_ref, u_ref, go_ref, uo_ref, act_ref, acc_g, acc_u):
        kk = pl.program_id(2)

        @pl.when(kk == 0)
        def _():
            acc_g[...] = jnp.zeros_like(acc_g)
            acc_u[...] = jnp.zeros_like(acc_u)

        av = a_ref[...].astype(_MXU)
        acc_g[...] += jnp.dot(av, g_ref[...].astype(_MXU), preferred_element_type=_F32)
        acc_u[...] += jnp.dot(av, u_ref[...].astype(_MXU), preferred_element_type=_F32)

        @pl.when(kk == nk - 1)
        def _():
            gf, uf = acc_g[...], acc_u[...]
            go_ref[...] = gf.astype(go_ref.dtype)
            uo_ref[...] = uf.astype(uo_ref.dtype)
            act_ref[...] = (gf * jax.nn.sigmoid(gf) * uf).astype(act_ref.dtype)

    w_spec = pl.BlockSpec((tk, tn), lambda i, j, kk: (kk, j))
    o_spec = pl.BlockSpec((tm, tn), lambda i, j, kk: (i, j))
    out = jax.ShapeDtypeStruct((m, n), _ACT)
    return pl.pallas_call(
        body, name=name, grid=(m // tm, n // tn, nk),
        in_specs=[pl.BlockSpec((tm, tk), lambda i, j, kk: (i, kk)), w_spec, w_spec],
        out_specs=[o_spec, o_spec, o_spec], out_shape=[out, out, out],
        scratch_shapes=[pltpu.VMEM((tm, tn), _F32), pltpu.VMEM((tm, tn), _F32)],
        compiler_params=_params(("parallel", "parallel", "arbitrary")),
    )(a, w_gate, w_up)


def _gate_up_bwd(dout, w_out, gate, up, *, name):
    m, k = dout.shape
    n = w_out.shape[0]
    tm, tn, tk = _pick(m, SWIGLU_ROWS), _pick(n, MM_TILE), _pick(k, MM_TILE)
    nk = k // tk

    def body(d_ref, w_ref, g_ref, u_ref, dg_ref, du_ref, acc):
        kk = pl.program_id(2)

        @pl.when(kk == 0)
        def _():
            acc[...] = jnp.zeros_like(acc)

        acc[...] += lax.dot_general(d_ref[...].astype(_MXU), w_ref[...].astype(_MXU), _NT,
                                    preferred_element_type=_F32)

        @pl.when(kk == nk - 1)
        def _():
            daf = acc[...]
            gf = g_ref[...].astype(_F32)
            sg = jax.nn.sigmoid(gf)
            dg_ref[...] = (daf * u_ref[...].astype(_F32) * (sg + gf * sg * (1.0 - sg))).astype(dg_ref.dtype)
            du_ref[...] = (daf * (gf * sg)).astype(du_ref.dtype)

    o_spec = pl.BlockSpec((tm, tn), lambda i, j, kk: (i, j))
    out = jax.ShapeDtypeStruct((m, n), _ACT)
    return pl.pallas_call(
        body, name=name, grid=(m // tm, n // tn, nk),
        in_specs=[pl.BlockSpec((tm, tk), lambda i, j, kk: (i, kk)), pl.BlockSpec((tn, tk), lambda i, j, kk: (j, kk)),
                  o_spec, o_spec],
        out_specs=[o_spec, o_spec], out_shape=[out, out],
        scratch_shapes=[pltpu.VMEM((tm, tn), _F32)],
        compiler_params=_params(("parallel", "parallel", "arbitrary")),
    )(dout, w_out, gate, up)


def _final_loss(x, g, target, *, name):
    m, d = x.shape
    tm = _rows(m, 512)

    def body(x_ref, g_ref, t_ref, loss_ref, dx_ref, dg_ref):
        @pl.when(pl.program_id(0) == 0)
        def _():
            loss_ref[...] = jnp.zeros_like(loss_ref)
            dg_ref[...] = jnp.zeros_like(dg_ref)

        xf = x_ref[...]
        r = lax.rsqrt(jnp.mean(xf * xf, axis=-1, keepdims=True) + NORM_EPS)
        xh = xf * r
        err = xh * g_ref[...] - t_ref[...]
        loss_ref[...] += 0.5 * jnp.sum(jnp.mean(err * err, axis=-1, keepdims=True))
        dy = err * (1.0 / d)
        dg_ref[...] += jnp.sum(dy * xh, axis=0, keepdims=True)
        gdy = dy * g_ref[...]
        dx_ref[...] = r * (gdy - xh * jnp.mean(gdy * xh, axis=-1, keepdims=True))

    row = pl.BlockSpec((tm, d), lambda i: (i, 0))
    vec = pl.BlockSpec((1, d), lambda i: (0, 0))
    loss, dx, dg = pl.pallas_call(
        body, name=name, grid=(m // tm,),
        in_specs=[row, vec, row],
        out_specs=[pl.BlockSpec((8, LANES), lambda i: (0, 0)), row, vec],
        out_shape=[jax.ShapeDtypeStruct((8, LANES), _F32), jax.ShapeDtypeStruct((m, d), _F32),
                   jax.ShapeDtypeStruct((1, d), _F32)],
        compiler_params=_params(("arbitrary",)),
    )(x, g.reshape(1, d), target)
    return loss[0, 0], dx, dg.reshape(d)


def _reduce_adamw(parts, w, m, v, *, name):
    rws, cols = w.shape
    tr = rws
    for cand in range(min(rws, 256), 0, -SUBLANES_16BIT):
        if cand % SUBLANES_16BIT == 0 and rws % cand == 0:
            tr = cand
            break

    def body(p_ref, w_ref, m_ref, v_ref, g_ref, d_ref, nm_ref, nv_ref):
        gsum = p_ref[0].astype(_F32)
        for dev in range(1, N_DEV):
            gsum = gsum + p_ref[dev].astype(_F32)
        m2 = ADAM_B1 * m_ref[...] + (1.0 - ADAM_B1) * gsum
        v2 = ADAM_B2 * v_ref[...] + (1.0 - ADAM_B2) * (gsum * gsum)
        m_hat = m2 / (1.0 - ADAM_B1 ** ADAM_STEP)
        v_hat = v2 / (1.0 - ADAM_B2 ** ADAM_STEP)
        g_ref[...] = gsum
        d_ref[...] = -ADAM_LR * (m_hat / (jnp.sqrt(v_hat) + ADAM_EPS) + ADAM_WD * w_ref[...])
        nm_ref[...] = m2
        nv_ref[...] = v2

    blk = pl.BlockSpec((tr, cols), lambda i: (i, 0))
    out = jax.ShapeDtypeStruct((rws, cols), _F32)
    return pl.pallas_call(
        body, name=name, grid=(rws // tr,),
        in_specs=[pl.BlockSpec((N_DEV, tr, cols), lambda i: (0, i, 0)), blk, blk, blk],
        out_specs=[blk, blk, blk, blk], out_shape=[out, out, out, out],
        compiler_params=_params(("parallel",)),
    )(parts, w, m, v)


def _mesh_pos():
    return lax.axis_index("x"), lax.axis_index("y"), lax.axis_index("c")


def _all_gather(block):
    rws, cols = block.shape

    def body(x_ref, out_ref, send_sems, recv_sems, local_sem):
        start, forward, finish = _gather_phases(x_ref, out_ref, send_sems, recv_sems, local_sem)
        start()
        forward()
        finish()

    return pl.pallas_call(
        body, name="weights_all_gather",
        out_shape=jax.ShapeDtypeStruct((N_DEV, rws, cols), block.dtype),
        in_specs=[pl.BlockSpec(memory_space=pl.ANY)],
        out_specs=pl.BlockSpec(memory_space=pl.ANY),
        scratch_shapes=_comm_scratch(),
    )(block)


def _comm_scratch():
    return [pltpu.SemaphoreType.DMA((N_DEV - 1,)), pltpu.SemaphoreType.DMA((N_DEV - 1,)), pltpu.SemaphoreType.DMA]


def _gather_phases(x_ref, out_ref, send_sems, recv_sems, local_sem):
    x, y, c = _mesh_pos()
    me, sibling = (x, y, c), (x, y, 1 - c)
    chips = [(1 - x, y), (x, 1 - y), (1 - x, 1 - y)]

    def slot(px, py, pc):
        return out_ref.at[4 * px + 2 * py + pc]

    def copy(k, blk, to, src=None):
        return pltpu.make_async_remote_copy(
            src_ref=slot(*blk) if src is None else src, dst_ref=slot(*blk),
            send_sem=send_sems.at[k], recv_sem=recv_sems.at[k],
            device_id=to, device_id_type=pl.DeviceIdType.MESH)

    mine = pltpu.make_async_copy(x_ref, slot(*me), local_sem)
    first = [copy(0, me, sibling, src=x_ref)]
    first += [copy(1 + j, me, (*chip, c), src=x_ref) for j, chip in enumerate(chips)]
    passed = [copy(4 + j, (*chip, c), sibling) for j, chip in enumerate(chips)]

    def start():
        mine.start()
        for cp in first:
            cp.start()

    def forward():
        for j, chip in enumerate(chips):
            copy(1 + j, (*chip, c), me).wait_recv()
            passed[j].start()

    def finish():
        copy(0, sibling, me).wait_recv()
        for j, chip in enumerate(chips):
            copy(4 + j, (*chip, 1 - c), me).wait_recv()
        for cp in first + passed:
            cp.wait_send()
        mine.wait()

    return start, forward, finish


def _exchange(parts):
    def body(g_ref, out_ref, send_sems, recv_sems, local_sem):
        start, finish = _exchange_phases(g_ref, out_ref, send_sems, recv_sems, local_sem)
        start()
        finish()

    return pl.pallas_call(
        body, name="grads_exchange",
        out_shape=jax.ShapeDtypeStruct(parts.shape, parts.dtype),
        in_specs=[pl.BlockSpec(memory_space=pl.ANY)],
        out_specs=pl.BlockSpec(memory_space=pl.ANY),
        scratch_shapes=_comm_scratch(),
    )(parts)


def _exchange_phases(g_ref, out_ref, send_sems, recv_sems, local_sem):
    x, y, c = _mesh_pos()
    me = 4 * x + 2 * y + c
    mine = pltpu.make_async_copy(g_ref.at[me], out_ref.at[me], local_sem)
    sends, arrivals = [], []
    for k in range(1, N_DEV):
        px = 1 - x if k & 4 else x
        py = 1 - y if k & 2 else y
        pc = 1 - c if k & 1 else c
        peer = 4 * px + 2 * py + pc
        sems = dict(send_sem=send_sems.at[k - 1], recv_sem=recv_sems.at[k - 1],
                    device_id=(px, py, pc), device_id_type=pl.DeviceIdType.MESH)
        sends.append(pltpu.make_async_remote_copy(src_ref=g_ref.at[peer], dst_ref=out_ref.at[me], **sems))
        arrivals.append(pltpu.make_async_remote_copy(src_ref=g_ref.at[peer], dst_ref=out_ref.at[peer], **sems))

    def start():
        mine.start()
        for cp in sends:
            cp.start()

    def finish():
        for cp in arrivals:
            cp.wait_recv()
        for cp in sends:
            cp.wait_send()
        mine.wait()

    return start, finish


_BIG = (("w_in_ab", 2), ("mla_w_uq", 1), ("mla_w_ukv", 1), ("w_out_ab", 2), ("gqa_w_q", 1), ("gqa_w_kv", 1),
        ("gqa_w_o", 1), ("ffn_w_in", 2), ("ffn_w_out", 1))
_SMALL = ("mix_norm_ab", "ffn_norm", "final_norm", "mix_norm_c", "mla_q_norm", "mla_kv_norm", "gqa_q_norm", "gqa_k_norm")
SMALL_ROWS = 16
GATHER_TAIL_ROWS = 16
EXCHANGE_TAIL_ROWS = 32


def _view3(a):
    return a.reshape(a.shape[0], a.shape[1], -1)


def _pad_rows(a2d):
    pad = -a2d.shape[0] % SUBLANES_16BIT
    return jnp.pad(a2d, ((0, pad), (0, 0))) if pad else a2d


def _pack_rows(shard):
    return _pad_rows(shard.reshape(-1, PACK_COLS))


def _packed_rows(shape):
    n = 1
    for d in shape:
        n *= d
    rows = n // PACK_COLS
    return rows + (-rows % SUBLANES_16BIT)


_FIRST = {"w_in_ab": (0, 1), "mla_w_uq": (0, 1), "mla_w_ukv": (0, 1), "w_out_ab": (0, 1), "ffn_w_in": (0, 1),
          "ffn_w_out": (0, 1)}
_REST = {"w_in_ab": (1, 1), "mla_w_uq": (1, 1), "mla_w_ukv": (1, 1), "w_out_ab": (1, 1), "gqa_w_q": (0, 2),
         "gqa_w_kv": (0, 2), "gqa_w_o": (0, 2), "ffn_w_in": (1, 3), "ffn_w_out": (1, 3)}
PACK_ROWS = 128


def _group_of(arrays, group):
    return {n: arrays[n][lo:lo + cnt] for n, (lo, cnt) in group.items()}


def _pack_big(shards):
    packed = jnp.concatenate([_pack_rows(shards[n]) for n, _ in _BIG if n in shards], axis=0)
    return jnp.pad(packed, ((0, -packed.shape[0] % PACK_ROWS), (0, 0)))


def _unpack_big(packed, shapes):
    out, off = {}, 0
    for n, _ in _BIG:
        if n not in shapes:
            continue
        size = 1
        for d in shapes[n]:
            size *= d
        out[n] = packed[off:off + size // PACK_COLS].reshape(shapes[n])
        off += _packed_rows(shapes[n])
    return out


def _unpack_gathered(gathered, shapes):
    out, off = {}, 0
    for n, axis in _BIG:
        if n not in shapes:
            continue
        size = 1
        for d in shapes[n]:
            size *= d
        l3 = (shapes[n][0], shapes[n][1], size // (shapes[n][0] * shapes[n][1]))
        sh = gathered[:, off:off + size // PACK_COLS].reshape((N_DEV,) + l3)
        if axis == 1:
            full = sh.transpose(1, 0, 2, 3).reshape(l3[0], N_DEV * l3[1], l3[2])
        else:
            full = sh.transpose(1, 2, 0, 3).reshape(l3[0], l3[1], N_DEV * l3[2])
        out[n] = full
        off += _packed_rows(shapes[n])
    return out


def _split_for_devices(full, axis):
    l, rws, cols = full.shape
    if axis == 1:
        sh = full.reshape(l, N_DEV, rws // N_DEV, cols).transpose(1, 0, 2, 3)
    else:
        sh = full.reshape(l, rws, N_DEV, cols // N_DEV).transpose(2, 0, 1, 3)
    flat = sh.reshape(N_DEV, -1, PACK_COLS)
    pad = -flat.shape[1] % SUBLANES_16BIT
    return jnp.pad(flat, ((0, 0), (0, pad), (0, 0))) if pad else flat


def _to_bits16(a_f32_rows):
    r = a_f32_rows.shape[0]
    return lax.bitcast_convert_type(a_f32_rows, jnp.bfloat16).reshape(2 * r, PACK_COLS)


def _from_bits16(a_bf16_rows):
    lead, r = a_bf16_rows.shape[:-2], a_bf16_rows.shape[-2]
    return lax.bitcast_convert_type(a_bf16_rows.reshape(lead + (r // 2, PACK_COLS, 2)), _F32)


def _small_sizes():
    return {"mix_norm_ab": 2 * D_MODEL, "ffn_norm": DEPTH * D_MODEL, "final_norm": D_MODEL, "mix_norm_c": 2 * D_MODEL,
            "mla_q_norm": 2 * MLA_Q_RANK, "mla_kv_norm": 2 * MLA_KV_RANK, "gqa_q_norm": 2 * GQA_HEAD_DIM,
            "gqa_k_norm": 2 * GQA_HEAD_DIM}


def _pack_small(vals):
    flat = jnp.concatenate([vals[n].reshape(-1).astype(_F32) for n in _SMALL])
    return jnp.pad(flat, (0, SMALL_ROWS * PACK_COLS - flat.shape[0])).reshape(SMALL_ROWS, PACK_COLS)


def _unpack_small(pack):
    flat, out, off = pack.reshape(-1), {}, 0
    sizes = _small_sizes()
    for n in _SMALL:
        out[n] = flat[off:off + sizes[n]]
        off += sizes[n]
    return out


def _angles(pos, dim):
    freqs = ROPE_THETA ** (-jnp.arange(0, dim, 2, dtype=_F32) / dim)
    ang = pos.astype(_F32)[:, None] * freqs[None, :]
    return jnp.cos(ang), jnp.sin(ang)


def _rope_tables(s):
    pos = jnp.arange(s)
    cos_t, sin_t = _angles(pos, MLA_ROPE)
    mla_c = jnp.tile(jnp.concatenate([cos_t, cos_t], -1), (1, LANES // 32))
    mla_s = jnp.tile(jnp.concatenate([-sin_t, sin_t], -1), (1, LANES // 32))
    rows = s // GRID_W
    row_idx = jnp.broadcast_to(jnp.arange(rows)[:, None], (rows, GRID_W)).reshape(-1)
    col_idx = jnp.broadcast_to(jnp.arange(GRID_W)[None, :], (rows, GRID_W)).reshape(-1)
    cos_r, sin_r = _angles(row_idx, GQA_HEAD_DIM // 2)
    cos_c, sin_c = _angles(col_idx, GQA_HEAD_DIM // 2)
    gqa_c = jnp.tile(jnp.concatenate([cos_r, cos_r, cos_c, cos_c], -1), (1, LANES // GQA_HEAD_DIM))
    gqa_s = jnp.tile(jnp.concatenate([-sin_r, sin_r, -sin_c, sin_c], -1), (1, LANES // GQA_HEAD_DIM))
    return (mla_c, mla_s), (gqa_c, gqa_s)


def _heads(x2d, h):
    s = x2d.shape[0]
    return x2d.reshape(s, h, -1).transpose(1, 0, 2)


def _unheads(xh):
    h, s, d = xh.shape
    return xh.transpose(1, 0, 2).reshape(s, h * d)


def _to_res(x2d, dil):
    s = x2d.shape[0]
    return x2d.reshape(s // dil, dil, DIL_SLOTS, -1).transpose(2, 1, 0, 3).reshape(DIL_SLOTS, s, -1)


def _from_res(xh, dil):
    sl, s, d = xh.shape
    return xh.reshape(sl, dil, s // dil, d).transpose(2, 1, 0, 3).reshape(s, sl * d)


def _res_to_tok(xh, dil):
    sl, s, c = xh.shape
    return xh.reshape(sl, dil, s // dil, c).transpose(0, 2, 1, 3).reshape(sl, s, c)


def _tok_to_res(xh, dil):
    sl, s, c = xh.shape
    return xh.reshape(sl, s // dil, dil, c).transpose(0, 2, 1, 3).reshape(sl, s, c)


def _ffn_fwd(x, w, tag):
    hn = _rmsnorm(x, w["norm"], out_dtype=_ACT, name=f"{tag}_norm")
    gate, up, act = _gate_up(hn, w["w_gate"], w["w_up"], name=f"{tag}_gate_up")
    out = _matmul(act, w["w_out"], res=x, name=f"{tag}_out")
    return out, (x, hn, gate, up, act)


def _ffn_bwd(dout, saved, w, tag):
    x, hn, gate, up, act = saved
    d_w_out = _matmul(act, dout, trans_a=True, name=f"{tag}_dwout")
    dgate, dup = _gate_up_bwd(dout, w["w_out"], gate, up, name=f"{tag}_dgate_up")
    d_w_gate = _matmul(hn, dgate, trans_a=True, name=f"{tag}_dwgate")
    d_w_up = _matmul(hn, dup, trans_a=True, name=f"{tag}_dwup")
    dhn = _matmul(dgate, w["w_gate"], trans_b=True, name=f"{tag}_dhn_gate")
    dhn = _matmul(dup, w["w_up"], trans_b=True, res=dhn, name=f"{tag}_dhn_up")
    dx, dnorm = _rmsnorm_bwd(x, w["norm"], dhn, dout, name=f"{tag}_dnorm")
    return dx, {"norm": dnorm, "w_gate": d_w_gate, "w_up": d_w_up, "w_out": d_w_out}


def _even_fwd(x, w, tabs, slopes, tag, gather=None):
    s = x.shape[0]
    (mla_c, mla_s), _ = tabs
    hn = _rmsnorm(x, w["norm"], out_dtype=_ACT, name=f"{tag}_norm")
    za = _matmul(hn, w["w_a"], name=f"{tag}_in_a")
    zb = _matmul(hn, w["w_b"], out_dtype=_ACT, name=f"{tag}_in_b")
    cq, ckv, kr = za[:, :MLA_Q_RANK], za[:, MLA_Q_RANK:MLA_Q_RANK + MLA_KV_RANK], za[:, MLA_Q_RANK + MLA_KV_RANK:]
    cqn = _rmsnorm(cq, w["q_norm"], out_dtype=_ACT, name=f"{tag}_qnorm")
    ckvn = _rmsnorm(ckv, w["kv_norm"], out_dtype=_ACT, name=f"{tag}_kvnorm")
    q = _matmul(cqn, w["w_uq"], scale=(MLA_NOPE + MLA_ROPE) ** -0.5 * LOG2_E,
                name=f"{tag}_uq")
    kv = _matmul(ckvn, w["w_ukv"], out_dtype=_ACT, name=f"{tag}_ukv")
    nn = MLA_HEADS * MLA_NOPE
    q_rope = _rope(q[:, nn:], mla_c, mla_s, out_dtype=_ACT, name=f"{tag}_rope_q")
    k_rope = _rope(kr, mla_c, mla_s, out_dtype=_ACT, name=f"{tag}_rope_k")[:, :MLA_ROPE]
    qh = jnp.concatenate([_heads_t(q[:, :nn].astype(_ACT), MLA_HEADS), _heads_t(q_rope, MLA_HEADS)], axis=1)
    kh = jnp.concatenate([_heads(kv[:, :nn], MLA_HEADS),
                          jnp.broadcast_to(k_rope[None], (MLA_HEADS, s, MLA_ROPE))], axis=-1)
    vh = _heads(kv[:, nn:], MLA_HEADS)
    oa, lse_a, *gathered = _attn_fwd(qh, kh, _chunk_t(kv[:, nn:], MLA_HEADS), out_dtype=_ACT, name=f"{tag}_mla",
                                     gather=gather)

    nd, ng = DIL_HEADS * DIL_HEAD_DIM, DIL_SLOTS * DIL_HEAD_DIM
    outs, lses, dil_saved = [], [], []
    for gi, (_, dil) in enumerate(DIL_PAIRS):
        hs = slice(gi * DIL_SLOTS, (gi + 1) * DIL_SLOTS)
        qr, kr_, vr = (_to_res(zb[:, a * nd + gi * ng:a * nd + (gi + 1) * ng], dil) for a in range(3))
        o, l = _win_fwd(qr, kr_, vr, slopes[hs], dil=dil, name=f"{tag}_dil{gi}")
        dil_saved.append((qr, kr_, vr, l))
        outs.append(_res_to_tok(o, dil))
        lses.append(_res_to_tok(l, dil))
    o3, lse3 = jnp.stack(outs), jnp.stack(lses)
    comb = _merge_fwd(o3, lse3, name=f"{tag}_merge")
    cat = jnp.concatenate([_unheads_t(oa), _unheads(comb)], axis=-1)
    out = _matmul(cat, w["w_out"], res=x, name=f"{tag}_out")
    saved = (x, hn, cq, ckv, cqn, ckvn, qh, kh, vh, oa, lse_a, dil_saved, o3, lse3, cat)
    return out, saved, (gathered[0] if gathered else None)


def _even_bwd(dout, saved, w, tabs, slopes, tag, exchange=None):
    x, hn, cq, ckv, cqn, ckvn, qh, kh, vh, oa, lse_a, dil_saved, o3, lse3, cat = saved
    (mla_c, mla_s), _ = tabs
    nn = MLA_HEADS * MLA_NOPE
    dcat = _matmul(dout, w["w_out"], trans_b=True, out_dtype=_ACT, name=f"{tag}_dcat")
    d_w_out = _matmul(cat, dout, trans_a=True, name=f"{tag}_dwout")
    nv = MLA_HEADS * MLA_V

    doa = _heads(dcat[:, :nv], MLA_HEADS)
    dqh, dkt, dvt, *received = _attn_bwd(qh, kh, vh, oa, doa, lse_a, scale=(MLA_NOPE + MLA_ROPE) ** -0.5,
                                         name=f"{tag}_mla_bwd", exchange=exchange)
    dq_rope = _rope(_unheads(dqh[..., MLA_NOPE:]), mla_c, -mla_s, out_dtype=_ACT, name=f"{tag}_drope_q")
    dq = jnp.concatenate([_unheads(dqh[..., :MLA_NOPE]).astype(_ACT), dq_rope], axis=-1)
    dk3 = _unchunk(dkt).reshape(-1, MLA_HEADS, MLA_NOPE + MLA_ROPE)
    dkr = _rope(dk3[..., MLA_NOPE:].reshape(-1, MLA_HEADS * MLA_ROPE), mla_c, -mla_s, out_dtype=_F32,
                sum_chunks=True, name=f"{tag}_drope_k")
    dkv = jnp.concatenate([dk3[..., :MLA_NOPE].reshape(-1, nn), _unchunk(dvt)], axis=-1).astype(_ACT)
    d_w_uq = _matmul(cqn, dq, trans_a=True, name=f"{tag}_dwuq")
    d_w_ukv = _matmul(ckvn, dkv, trans_a=True, name=f"{tag}_dwukv")
    dcqn = _matmul(dq, w["w_uq"], trans_b=True, name=f"{tag}_dcqn")
    dckvn = _matmul(dkv, w["w_ukv"], trans_b=True, name=f"{tag}_dckvn")
    dcq, d_q_norm = _rmsnorm_bwd(cq, w["q_norm"], dcqn, out_dtype=_ACT, name=f"{tag}_dqnorm")
    dckv, d_kv_norm = _rmsnorm_bwd(ckv, w["kv_norm"], dckvn, out_dtype=_ACT, name=f"{tag}_dkvnorm")
    lane = jnp.arange(LANES) < MLA_ROPE
    dza = jnp.concatenate([dcq, dckv, jnp.where(lane[None], dkr, 0.0).astype(_ACT)], axis=-1)

    dcomb = _heads(dcat[:, nv:], DIL_SLOTS)
    do3, delta3 = _merge_bwd(dcomb, o3, lse3, name=f"{tag}_dmerge")
    dqs, dks, dvs = [], [], []
    for gi, (_, dil) in enumerate(DIL_PAIRS):
        hs = slice(gi * DIL_SLOTS, (gi + 1) * DIL_SLOTS)
        qr, kr_, vr, l = dil_saved[gi]
        grads = (qr, kr_, vr, _tok_to_res(do3[gi], dil), l, _tok_to_res(delta3[gi], dil), slopes[hs])
        dqs.append(_from_res(_win_bwd_dq(*grads, dil=dil, name=f"{tag}_dil{gi}_dq"), dil))
        b, c = _win_bwd_dkv(*grads, dil=dil, name=f"{tag}_dil{gi}_dkv")
        dks.append(_from_res(b, dil))
        dvs.append(_from_res(c, dil))
    dzb = jnp.concatenate(dqs + dks + dvs, axis=-1).astype(_ACT)

    d_w_a = _matmul(hn, dza, trans_a=True, name=f"{tag}_dwa")
    d_w_b = _matmul(hn, dzb, trans_a=True, name=f"{tag}_dwb")
    dhn = _matmul(dza, w["w_a"], trans_b=True, name=f"{tag}_dhn_a")
    dhn = _matmul(dzb, w["w_b"], trans_b=True, res=dhn, name=f"{tag}_dhn_b")
    dx, dnorm = _rmsnorm_bwd(x, w["norm"], dhn, dout, name=f"{tag}_dnorm")
    grads = {"norm": dnorm, "w_a": d_w_a, "w_b": d_w_b, "q_norm": d_q_norm, "kv_norm": d_kv_norm,
             "w_uq": d_w_uq, "w_ukv": d_w_ukv, "w_out": d_w_out}
    return dx, grads, (received[0] if received else None)


def _odd_fwd(x, w, tabs, tag):
    s = x.shape[0]
    _, (gqa_c, gqa_s) = tabs
    nk = GQA_KV_HEADS * GQA_HEAD_DIM
    hn = _rmsnorm(x, w["norm"], out_dtype=_ACT, name=f"{tag}_norm")
    q = _matmul(hn, w["w_q"], name=f"{tag}_q")
    kv = _matmul(hn, w["w_kv"], name=f"{tag}_kv")
    k = kv[:, :nk]
    qh = _heads_t(_headnorm_rope(q, w["q_norm"], gqa_c, gqa_s, scale=GQA_HEAD_DIM ** -0.5 * LOG2_E,
                                 name=f"{tag}_prep_q"), GQA_HEADS)
    kh = _heads(_headnorm_rope(k, w["k_norm"], gqa_c, gqa_s, name=f"{tag}_prep_k"), GQA_KV_HEADS)
    v = kv[:, nk:].astype(_ACT)
    vh = _heads(v, GQA_KV_HEADS)
    o, lse = _attn_fwd(qh, kh, _chunk_t(v, GQA_KV_HEADS), out_dtype=_ACT, name=f"{tag}_gqa")
    ocat = _unheads_t(o)
    out = _matmul(ocat, w["w_o"], res=x, name=f"{tag}_out")
    return out, (x, hn, q, k, qh, kh, vh, o, lse, ocat)


def _odd_bwd(dout, saved, w, tabs, tag):
    x, hn, q, k, qh, kh, vh, o, lse, ocat = saved
    s = x.shape[0]
    _, (gqa_c, gqa_s) = tabs
    docat = _matmul(dout, w["w_o"], trans_b=True, out_dtype=_ACT, name=f"{tag}_docat")
    d_w_o = _matmul(ocat, dout, trans_a=True, name=f"{tag}_dwo")
    doh = _heads(docat, GQA_HEADS)
    dqh, dkt, dvt = _attn_bwd(qh, kh, vh, o, doh, lse, scale=GQA_HEAD_DIM ** -0.5, name=f"{tag}_gqa_bwd")
    dq, d_q_norm = _headnorm_rope_bwd(q, w["q_norm"], _unheads(dqh), gqa_c, gqa_s, name=f"{tag}_dprep_q")
    dk, d_k_norm = _headnorm_rope_bwd(k, w["k_norm"], _unchunk(dkt), gqa_c, gqa_s, name=f"{tag}_dprep_k")
    dkv = jnp.concatenate([dk, _unchunk(dvt).astype(_ACT)], axis=-1)
    d_w_q = _matmul(hn, dq, trans_a=True, name=f"{tag}_dwq")
    d_w_kv = _matmul(hn, dkv, trans_a=True, name=f"{tag}_dwkv")
    dhn = _matmul(dq, w["w_q"], trans_b=True, name=f"{tag}_dhn_q")
    dhn = _matmul(dkv, w["w_kv"], trans_b=True, res=dhn, name=f"{tag}_dhn_kv")
    dx, dnorm = _rmsnorm_bwd(x, w["norm"], dhn, dout, name=f"{tag}_dnorm")
    return dx, {"norm": dnorm, "w_q": d_w_q, "w_kv": d_w_kv, "q_norm": d_q_norm, "k_norm": d_k_norm, "w_o": d_w_o}


def _split_heads_cols(wm, heads, first):
    rws = wm.shape[0]
    w3 = wm.reshape(rws, heads, -1)
    return jnp.concatenate([w3[:, :, :first].reshape(rws, -1), w3[:, :, first:].reshape(rws, -1)], axis=-1)


def _merge_heads_cols(wm, heads, first):
    rws, cols = wm.shape
    a = wm[:, :heads * first].reshape(rws, heads, first)
    b = wm[:, heads * first:].reshape(rws, heads, cols // heads - first)
    return jnp.concatenate([a, b], axis=-1).reshape(rws, cols)


def _layer_weights(full, gains, layer):
    i = layer // 2

    def stacked(name, idx):
        for group, arrays in zip((_FIRST, _REST), full):
            lo, cnt = group.get(name, (0, 0))
            if lo <= idx < lo + cnt:
                return arrays[name][idx - lo]
        raise KeyError((name, idx))

    w_ffn_in = stacked("ffn_w_in", layer)
    ffn = {"norm": gains["ffn_norm"][layer], "w_gate": w_ffn_in[:, :FFN_HIDDEN], "w_up": w_ffn_in[:, FFN_HIDDEN:],
           "w_out": stacked("ffn_w_out", layer)}
    if layer % 2 == 0:
        w_in = stacked("w_in_ab", i)
        mix = {"norm": gains["mix_norm_ab"][i],
               "w_a": jnp.pad(w_in[:, :IN_A], ((0, 0), (0, IN_A_PAD - IN_A))), "w_b": w_in[:, IN_A:],
               "q_norm": gains["mla_q_norm"][i], "kv_norm": gains["mla_kv_norm"][i],
               "w_uq": _split_heads_cols(stacked("mla_w_uq", i), MLA_HEADS, MLA_NOPE),
               "w_ukv": _split_heads_cols(stacked("mla_w_ukv", i), MLA_HEADS, MLA_NOPE),
               "w_out": stacked("w_out_ab", i)}
    else:
        mix = {"norm": gains["mix_norm_c"][i], "w_q": stacked("gqa_w_q", i), "w_kv": stacked("gqa_w_kv", i),
               "q_norm": gains["gqa_q_norm"][i], "k_norm": gains["gqa_k_norm"][i], "w_o": stacked("gqa_w_o", i)}
    return mix, ffn


def _pack_grads(grads, group):
    parts = jnp.concatenate([_split_for_devices(jnp.stack(grads[n][group[n][0]:group[n][0] + group[n][1]]), axis)
                             for n, axis in _BIG if n in group], axis=1)
    return jnp.pad(parts, ((0, 0), (0, -parts.shape[1] % PACK_ROWS), (0, 0)))


def kernel(x, mix_norm_ab, w_in_ab, mla_q_norm, mla_kv_norm, mla_w_uq, mla_w_ukv, w_out_ab, mix_norm_c, gqa_w_q, gqa_w_kv, gqa_q_norm, gqa_k_norm, gqa_w_o, ffn_norm, ffn_w_in, ffn_w_out, final_norm, loss_target, m_mix_norm_ab, m_w_in_ab, m_mla_q_norm, m_mla_kv_norm, m_mla_w_uq, m_mla_w_ukv, m_w_out_ab, m_mix_norm_c, m_gqa_w_q, m_gqa_w_kv, m_gqa_q_norm, m_gqa_k_norm, m_gqa_w_o, m_ffn_norm, m_ffn_w_in, m_ffn_w_out, m_final_norm, v_mix_norm_ab, v_w_in_ab, v_mla_q_norm, v_mla_kv_norm, v_mla_w_uq, v_mla_w_ukv, v_w_out_ab, v_mix_norm_c, v_gqa_w_q, v_gqa_w_kv, v_gqa_q_norm, v_gqa_k_norm, v_gqa_w_o, v_ffn_norm, v_ffn_w_in, v_ffn_w_out, v_final_norm):
    wts = dict(mix_norm_ab=mix_norm_ab, w_in_ab=w_in_ab, mla_q_norm=mla_q_norm, mla_kv_norm=mla_kv_norm,
               mla_w_uq=mla_w_uq, mla_w_ukv=mla_w_ukv, w_out_ab=w_out_ab, mix_norm_c=mix_norm_c, gqa_w_q=gqa_w_q,
               gqa_w_kv=gqa_w_kv, gqa_q_norm=gqa_q_norm, gqa_k_norm=gqa_k_norm, gqa_w_o=gqa_w_o, ffn_norm=ffn_norm,
               ffn_w_in=ffn_w_in, ffn_w_out=ffn_w_out, final_norm=final_norm)
    mom = dict(mix_norm_ab=m_mix_norm_ab, w_in_ab=m_w_in_ab, mla_q_norm=m_mla_q_norm, mla_kv_norm=m_mla_kv_norm,
               mla_w_uq=m_mla_w_uq, mla_w_ukv=m_mla_w_ukv, w_out_ab=m_w_out_ab, mix_norm_c=m_mix_norm_c,
               gqa_w_q=m_gqa_w_q, gqa_w_kv=m_gqa_w_kv, gqa_q_norm=m_gqa_q_norm, gqa_k_norm=m_gqa_k_norm,
               gqa_w_o=m_gqa_w_o, ffn_norm=m_ffn_norm, ffn_w_in=m_ffn_w_in, ffn_w_out=m_ffn_w_out,
               final_norm=m_final_norm)
    var = dict(mix_norm_ab=v_mix_norm_ab, w_in_ab=v_w_in_ab, mla_q_norm=v_mla_q_norm, mla_kv_norm=v_mla_kv_norm,
               mla_w_uq=v_mla_w_uq, mla_w_ukv=v_mla_w_ukv, w_out_ab=v_w_out_ab, mix_norm_c=v_mix_norm_c,
               gqa_w_q=v_gqa_w_q, gqa_w_kv=v_gqa_w_kv, gqa_q_norm=v_gqa_q_norm, gqa_k_norm=v_gqa_k_norm,
               gqa_w_o=v_gqa_w_o, ffn_norm=v_ffn_norm, ffn_w_in=v_ffn_w_in, ffn_w_out=v_ffn_w_out,
               final_norm=v_final_norm)
    big_names = [n for n, _ in _BIG]
    groups = (_FIRST, _REST)
    w_grp = [_group_of(wts, grp) for grp in groups]
    shapes = [{n: a.shape for n, a in w.items()} for w in w_grp]
    w_packs = [_pack_big(w) for w in w_grp]
    n_rows = [p.shape[0] for p in w_packs]
    xs = x[0]
    s = xs.shape[0]
    me = 4 * lax.axis_index("x") + 2 * lax.axis_index("y") + lax.axis_index("c")
    c_cols = mix_norm_c.shape[1]

    tail = jnp.pad(mix_norm_c.reshape(-1), (0, GATHER_TAIL_ROWS // 2 * PACK_COLS - mix_norm_c.size))
    tail = _to_bits16(tail.reshape(GATHER_TAIL_ROWS // 2, PACK_COLS))
    gathered = _all_gather(jnp.concatenate([w_packs[0].astype(jnp.bfloat16), tail], axis=0))
    full = [_unpack_gathered(gathered[:, :n_rows[0]], shapes[0]), None]
    c_all = _from_bits16(gathered[:, n_rows[0]:]).reshape(N_DEV, -1)[:, :mix_norm_c.size]
    c_full = c_all.reshape(N_DEV, 2, c_cols).transpose(1, 0, 2).reshape(2, N_DEV * c_cols)
    gains = dict(mix_norm_ab=mix_norm_ab, mla_q_norm=mla_q_norm, mla_kv_norm=mla_kv_norm, mix_norm_c=c_full,
                 gqa_q_norm=gqa_q_norm, gqa_k_norm=gqa_k_norm, ffn_norm=ffn_norm)

    tabs = _rope_tables(s)
    slopes = jnp.exp2(-8.0 * jnp.arange(1, DIL_HEADS + 1, dtype=_F32) / DIL_HEADS)

    h = xs
    saved = []
    for layer in range(DEPTH):
        mix_w, ffn_w = _layer_weights(full, gains, layer)
        if layer == 0:
            h, sv_mix, rest = _even_fwd(h, mix_w, tabs, slopes, f"l{layer}_mix", gather=w_packs[1].astype(jnp.bfloat16))
            full[1] = _unpack_gathered(rest, shapes[1])
        elif layer % 2 == 0:
            h, sv_mix, _ = _even_fwd(h, mix_w, tabs, slopes, f"l{layer}_mix")
        else:
            h, sv_mix = _odd_fwd(h, mix_w, tabs, f"l{layer}_mix")
        h, sv_ffn = _ffn_fwd(h, ffn_w, f"l{layer}_ffn")
        saved.append((mix_w, ffn_w, sv_mix, sv_ffn))
    loss_local, dh, d_final = _final_loss(h, final_norm, loss_target[0], name="final_loss")

    gfull = {n: [None] * wts[n].shape[0] for n in big_names}
    gsmall = {n: [None] * (wts[n].shape[0] if wts[n].ndim > 1 else 1) for n in _SMALL}
    gsmall["final_norm"][0] = d_final
    for layer in reversed(range(DEPTH)):
        mix_w, ffn_w, sv_mix, sv_ffn = saved[layer]
        i = layer // 2
        dh, gf = _ffn_bwd(dh, sv_ffn, ffn_w, f"l{layer}_ffn")
        gsmall["ffn_norm"][layer] = gf["norm"]
        gfull["ffn_w_in"][layer] = jnp.concatenate([gf["w_gate"], gf["w_up"]], axis=-1)
        gfull["ffn_w_out"][layer] = gf["w_out"]
        if layer % 2 == 0:
            sending = _pack_grads(gfull, _REST).astype(jnp.bfloat16) if layer == 0 else None
            dh, gm, got = _even_bwd(dh, sv_mix, mix_w, tabs, slopes, f"l{layer}_mix", exchange=sending)
            if layer == 0:
                received_rest = got
            gsmall["mix_norm_ab"][i] = gm["norm"]
            gsmall["mla_q_norm"][i] = gm["q_norm"]
            gsmall["mla_kv_norm"][i] = gm["kv_norm"]
            gfull["w_in_ab"][i] = jnp.concatenate([gm["w_a"][:, :IN_A], gm["w_b"]], axis=-1)
            gfull["mla_w_uq"][i] = _merge_heads_cols(gm["w_uq"], MLA_HEADS, MLA_NOPE)
            gfull["mla_w_ukv"][i] = _merge_heads_cols(gm["w_ukv"], MLA_HEADS, MLA_NOPE)
            gfull["w_out_ab"][i] = gm["w_out"]
        else:
            dh, gm = _odd_bwd(dh, sv_mix, mix_w, tabs, f"l{layer}_mix")
            gsmall["mix_norm_c"][i] = gm["norm"]
            gsmall["gqa_q_norm"][i] = gm["q_norm"]
            gsmall["gqa_k_norm"][i] = gm["k_norm"]
            gfull["gqa_w_q"][i] = gm["w_q"]
            gfull["gqa_w_kv"][i] = gm["w_kv"]
            gfull["gqa_w_o"][i] = gm["w_o"]
    grad_x = dh[None]

    small_part = _pack_small({n: jnp.stack(gsmall[n]) for n in _SMALL})
    small_bits = jnp.broadcast_to(_to_bits16(small_part)[None], (N_DEV, EXCHANGE_TAIL_ROWS, PACK_COLS))
    received_first = _exchange(jnp.concatenate([_pack_grads(gfull, _FIRST).astype(jnp.bfloat16), small_bits], axis=1))
    big_packs = [_reduce_adamw(got, w_packs[gi], _pack_big(_group_of(mom, grp)), _pack_big(_group_of(var, grp)),
                               name=f"adamw_big{gi}")
                 for gi, (grp, got) in enumerate(zip(groups, (received_first, received_rest)))]

    def widen_c(shard):
        return lax.dynamic_update_slice(jnp.zeros((2, N_DEV * c_cols), _F32), shard, (0, me * c_cols))

    def small_of(src):
        return _pack_small({n: (widen_c(src[n]) if n == "mix_norm_c" else src[n]) for n in _SMALL})

    small_recv = _from_bits16(received_first[:, n_rows[0]:])
    small_packs = _reduce_adamw(small_recv, small_of(wts), small_of(mom), small_of(var), name="adamw_small")

    def outputs_of(which):
        by_group = [_unpack_big(big_packs[gi][which], shapes[gi]) for gi in range(len(groups))]
        small = _unpack_small(small_packs[which])
        res = {}
        for n in wts:
            if n in big_names:
                res[n] = jnp.concatenate([grp[n] for grp in by_group if n in grp], axis=0)
            elif n == "mix_norm_c":
                res[n] = lax.dynamic_slice(small[n].reshape(2, N_DEV * c_cols), (0, me * c_cols), (2, c_cols))
            else:
                res[n] = small[n].reshape(wts[n].shape)
        return [res[n] for n in wts]

    loss = lax.psum(loss_local, _AXES)
    return (loss, grad_x, *outputs_of(0), *outputs_of(1), *outputs_of(2), *outputs_of(3))
```

```python
import functools

import jax
import jax.numpy as jnp
from jax import lax
from jax.experimental import pallas as pl
from jax.experimental.pallas import tpu as pltpu

D_MODEL = 1024
DEPTH = 4
GRID_W = 64
NORM_EPS = 1e-6
ROPE_THETA = 10000.0
NEG_INF = -1e30
MLA_HEADS = 8
MLA_Q_RANK = 384
MLA_KV_RANK = 256
MLA_NOPE = 64
MLA_ROPE = 32
MLA_V = 64
DIL_PAIRS = ((128, 1), (512, 4), (2048, 16))
DIL_HALF = 64
DIL_SLOTS = 4
DIL_GROUPS = 3
DIL_HEADS = 12
DIL_HEAD_DIM = 64
GQA_HEADS = 16
GQA_KV_HEADS = 4
GQA_HEAD_DIM = 64
FFN_HIDDEN = 2816
IN_A = MLA_Q_RANK + MLA_KV_RANK + MLA_ROPE
IN_A_PAD = 768
IN_B = 3 * DIL_HEADS * DIL_HEAD_DIM
ADAM_LR = 0.001
ADAM_B1 = 0.9
ADAM_B2 = 0.999
ADAM_EPS = 1e-08
ADAM_WD = 0.01
ADAM_STEP = 10

LANES = 128
SUBLANES_16BIT = 16
VMEM_LIMIT_BYTES = 56 * 1024 * 1024

MM_TILE = 1408

N_DEV = 8
PACK_COLS = 1024

_MXU = jnp.bfloat16
_ACT = jnp.bfloat16
_F32 = jnp.float32

_AXES = ("x", "y", "c")


def _params(sem):
    return pltpu.CompilerParams(dimension_semantics=sem, vmem_limit_bytes=VMEM_LIMIT_BYTES)


def _pick(n, cap):
    for t in range(cap - cap % LANES, 0, -LANES):
        if n % t == 0:
            return t
    return n


def _rows(m, target):
    t = m
    while t > target and t % 2 == 0:
        t //= 2
    return t


def _matmul(a, b, *, trans_a=False, trans_b=False, res=None, scale=None, out_dtype=_F32, name):
    if trans_a:
        k, m = a.shape
    else:
        m, k = a.shape
    if trans_b:
        n, kb = b.shape
    else:
        kb, n = b.shape
    assert k == kb, (a.shape, b.shape)
    tm, tn, tk = _pick(m, MM_TILE), _pick(n, MM_TILE), _pick(k, MM_TILE)
    nk = k // tk
    dims = (((0 if trans_a else 1,), (1 if trans_b else 0,)), ((), ()))

    def body(*refs):
        if res is None:
            a_ref, b_ref, o_ref, acc = refs
            r_ref = None
        else:
            a_ref, b_ref, r_ref, o_ref, acc = refs
        kk = pl.program_id(2)

        @pl.when(kk == 0)
        def _():
            acc[...] = jnp.zeros_like(acc)

        acc[...] += lax.dot_general(a_ref[...].astype(_MXU), b_ref[...].astype(_MXU), dims,
                                    preferred_element_type=_F32)

        @pl.when(kk == nk - 1)
        def _():
            r = acc[...]
            if scale is not None:
                r = r * scale
            if r_ref is not None:
                r = r + r_ref[...].astype(_F32)
            o_ref[...] = r.astype(out_dtype)

    a_spec = (pl.BlockSpec((tk, tm), lambda i, j, kk: (kk, i)) if trans_a
              else pl.BlockSpec((tm, tk), lambda i, j, kk: (i, kk)))
    b_spec = (pl.BlockSpec((tn, tk), lambda i, j, kk: (j, kk)) if trans_b
              else pl.BlockSpec((tk, tn), lambda i, j, kk: (kk, j)))
    o_spec = pl.BlockSpec((tm, tn), lambda i, j, kk: (i, j))
    in_specs = [a_spec, b_spec] + ([o_spec] if res is not None else [])
    args = (a, b) + ((res,) if res is not None else ())
    return pl.pallas_call(
        body, name=name, grid=(m // tm, n // tn, nk),
        in_specs=in_specs, out_specs=o_spec,
        out_shape=jax.ShapeDtypeStruct((m, n), out_dtype),
        scratch_shapes=[pltpu.VMEM((tm, tn), _F32)],
        compiler_params=_params(("parallel", "parallel", "arbitrary")),
    )(*args)


def _rmsnorm(x, g, *, out_dtype, name, rows=512):
    m, d = x.shape
    tm = _rows(m, rows)

    def body(x_ref, g_ref, o_ref):
        xf = x_ref[...].astype(_F32)
        r = lax.rsqrt(jnp.mean(xf * xf, axis=-1, keepdims=True) + NORM_EPS)
        o_ref[...] = ((xf * r) * g_ref[...]).astype(out_dtype)

    return pl.pallas_call(
        body, name=name, grid=(m // tm,),
        in_specs=[pl.BlockSpec((tm, d), lambda i: (i, 0)), pl.BlockSpec((1, d), lambda i: (0, 0))],
        out_specs=pl.BlockSpec((tm, d), lambda i: (i, 0)),
        out_shape=jax.ShapeDtypeStruct((m, d), out_dtype),
        compiler_params=_params(("parallel",)),
    )(x, g.reshape(1, d).astype(_F32))


def _rmsnorm_bwd(x, g, dy, dres=None, *, out_dtype=_F32, name, rows=512):
    m, d = x.shape
    tm = _rows(m, rows)

    def body(*refs):
        if dres is None:
            x_ref, g_ref, dy_ref, dx_ref, dg_ref = refs
            r_ref = None
        else:
            x_ref, g_ref, dy_ref, r_ref, dx_ref, dg_ref = refs

        @pl.when(pl.program_id(0) == 0)
        def _():
            dg_ref[...] = jnp.zeros_like(dg_ref)

        xf = x_ref[...].astype(_F32)
        r = lax.rsqrt(jnp.mean(xf * xf, axis=-1, keepdims=True) + NORM_EPS)
        xh = xf * r
        dyf = dy_ref[...].astype(_F32)
        dg_ref[...] += jnp.sum(dyf * xh, axis=0, keepdims=True)
        gdy = dyf * g_ref[...]
        dx = r * (gdy - xh * jnp.mean(gdy * xh, axis=-1, keepdims=True))
        if r_ref is not None:
            dx = dx + r_ref[...].astype(_F32)
        dx_ref[...] = dx.astype(out_dtype)

    row = pl.BlockSpec((tm, d), lambda i: (i, 0))
    vec = pl.BlockSpec((1, d), lambda i: (0, 0))
    in_specs = [row, vec, row] + ([row] if dres is not None else [])
    args = (x, g.reshape(1, d).astype(_F32), dy) + ((dres,) if dres is not None else ())
    dx, dg = pl.pallas_call(
        body, name=name, grid=(m // tm,),
        in_specs=in_specs, out_specs=[row, vec],
        out_shape=[jax.ShapeDtypeStruct((m, d), out_dtype), jax.ShapeDtypeStruct((1, d), _F32)],
        compiler_params=_params(("arbitrary",)),
    )(*args)
    return dx, dg.reshape(d)


def _rotate(xf, c, sn):
    w = xf.shape[1]
    if w > LANES:
        c, sn = jnp.tile(c, (1, w // LANES)), jnp.tile(sn, (1, w // LANES))
    lane = lax.broadcasted_iota(jnp.int32, xf.shape, 1)
    sw = jnp.where((lane & 31) < 16, pltpu.roll(xf, w - 16, 1), pltpu.roll(xf, 16, 1))
    return xf * c + sw * sn


def _seg_mean(v, seg_ref):
    outs = []
    for c in range(v.shape[1] // LANES):
        piece = v[:, c * LANES:(c + 1) * LANES]
        hi = piece.astype(jnp.bfloat16)
        lo = (piece - hi.astype(_F32)).astype(jnp.bfloat16)
        outs.append(jnp.dot(hi, seg_ref[...], preferred_element_type=_F32)
                    + jnp.dot(lo, seg_ref[...], preferred_element_type=_F32))
    return jnp.concatenate(outs, axis=1) if len(outs) > 1 else outs[0]


def _seg_matrix():
    lane = jnp.arange(LANES) // GQA_HEAD_DIM
    return ((lane[:, None] == lane[None, :]).astype(_F32) / GQA_HEAD_DIM).astype(jnp.bfloat16)


def _headnorm_rope(x, gain, cos_t, sin_t, *, scale=None, name):
    s, w = x.shape
    ts = _rows(s, 512)

    def body(x_ref, g_ref, seg_ref, c_ref, s_ref, o_ref):
        xf = x_ref[...]
        r = lax.rsqrt(_seg_mean(xf * xf, seg_ref) + NORM_EPS)
        y = _rotate((xf * r) * g_ref[...], c_ref[...], s_ref[...])
        if scale is not None:
            y = y * scale
        o_ref[...] = y.astype(o_ref.dtype)

    row = pl.BlockSpec((ts, w), lambda i: (i, 0))
    tab = pl.BlockSpec((ts, LANES), lambda i: (i, 0))
    return pl.pallas_call(
        body, name=name, grid=(s // ts,),
        in_specs=[row, pl.BlockSpec((1, w), lambda i: (0, 0)), pl.BlockSpec((LANES, LANES), lambda i: (0, 0)), tab, tab],
        out_specs=row, out_shape=jax.ShapeDtypeStruct((s, w), _ACT),
        compiler_params=_params(("parallel",)),
    )(x, jnp.tile(gain.astype(_F32), w // GQA_HEAD_DIM).reshape(1, w), _seg_matrix(), cos_t, sin_t)


def _headnorm_rope_bwd(x, gain, dy, cos_t, sin_t, *, name):
    s, w = x.shape
    ts = _rows(s, 512)

    def body(x_ref, g_ref, seg_ref, c_ref, s_ref, dy_ref, dx_ref, dg_ref):
        @pl.when(pl.program_id(0) == 0)
        def _():
            dg_ref[...] = jnp.zeros_like(dg_ref)

        xf = x_ref[...]
        r = lax.rsqrt(_seg_mean(xf * xf, seg_ref) + NORM_EPS)
        xh = xf * r
        dyn = _rotate(dy_ref[...].astype(_F32), c_ref[...], -s_ref[...])
        dg_ref[...] += jnp.sum(dyn * xh, axis=0, keepdims=True)
        gdy = dyn * g_ref[...]
        dx_ref[...] = (r * (gdy - xh * _seg_mean(gdy * xh, seg_ref))).astype(dx_ref.dtype)

    row = pl.BlockSpec((ts, w), lambda i: (i, 0))
    vec = pl.BlockSpec((1, w), lambda i: (0, 0))
    tab = pl.BlockSpec((ts, LANES), lambda i: (i, 0))
    dx, dg = pl.pallas_call(
        body, name=name, grid=(s // ts,),
        in_specs=[row, vec, pl.BlockSpec((LANES, LANES), lambda i: (0, 0)), tab, tab, row],
        out_specs=[row, vec],
        out_shape=[jax.ShapeDtypeStruct((s, w), _ACT), jax.ShapeDtypeStruct((1, w), _F32)],
        compiler_params=_params(("arbitrary",)),
    )(x, jnp.tile(gain.astype(_F32), w // GQA_HEAD_DIM).reshape(1, w), _seg_matrix(), cos_t, sin_t, dy)
    return dx, dg.reshape(w // GQA_HEAD_DIM, GQA_HEAD_DIM).sum(axis=0)


def _rope(x, cos_t, sin_t, *, out_dtype, name, sum_chunks=False, scale=None):
    s, w = x.shape
    assert w % LANES == 0
    ts = _rows(s, 512)
    ow = LANES if sum_chunks else w

    def body(x_ref, c_ref, s_ref, o_ref):
        y = _rotate(x_ref[...].astype(_F32), c_ref[...], s_ref[...])
        if scale is not None:
            y = y * scale
        if sum_chunks:
            shift = w // 2
            while shift >= 32:
                y = y + pltpu.roll(y, shift, 1)
                shift //= 2
            y = y[:, :LANES]
        o_ref[...] = y.astype(out_dtype)

    return pl.pallas_call(
        body, name=name, grid=(s // ts,),
        in_specs=[pl.BlockSpec((ts, w), lambda i: (i, 0)), pl.BlockSpec((ts, LANES), lambda i: (i, 0)),
                  pl.BlockSpec((ts, LANES), lambda i: (i, 0))],
        out_specs=pl.BlockSpec((ts, ow), lambda i: (i, 0)),
        out_shape=jax.ShapeDtypeStruct((s, ow), out_dtype),
        compiler_params=_params(("parallel",)),
    )(x, cos_t, sin_t)


_NT = (((1,), (1,)), ((), ()))
_TN = (((0,), (0,)), ((), ()))
LOG2_E = 1.4426950408889634
LN_2 = 0.6931471805599453
ATTN_FWD_ROWS = 2048
ATTN_ROWS = 512
ATTN_CHAINS = 2
ATTN_FWD_KEYS = 1024
ATTN_BWD_KEYS = 1024


def _grid_step(dims):
    step, total = 0, 1
    for axis, n in enumerate(dims):
        step = step * n + pl.program_id(axis)
        total *= n
    return step, total


def _attn_fwd(qt, k, vt, *, out_dtype, name, gather=None):
    h, dk, s = qt.shape
    g, nk, dv, tk = vt.shape
    assert nk * tk == s
    r = h // g
    rc = min(ATTN_FWD_ROWS, s)
    hp = min(r, ATTN_CHAINS)
    nrc = max(1, min(ATTN_CHAINS // hp, s // rc))
    tq = rc * nrc
    nhp, nq = r // hp, s // tq
    units = [(a, c) for a in range(hp) for c in range(nrc)]

    def body(q_ref, k_ref, v_ref, *rest):
        if gather is None:
            o_ref, lse_ref = rest
        else:
            x_ref, o_ref, lse_ref, all_ref, send_sems, recv_sems, local_sem = rest
            start, forward, finish = _gather_phases(x_ref, all_ref, send_sems, recv_sems, local_sem)
            step, total = _grid_step((g, nhp, nq))
            pl.when(step == 0)(start)
            pl.when(step == total // 2)(forward)

        qs = [q_ref[a, :, c * rc:(c + 1) * rc] for a, c in units]
        init = tuple((jnp.full((1, rc), NEG_INF, _F32), jnp.zeros((1, rc), _F32), jnp.zeros((dv, rc), _F32))
                     for _ in units)

        def trip(j, carry):
            kb, vb = k_ref[0, pl.ds(pl.multiple_of(j * tk, tk), tk), :], v_ref[0, j]
            out = []
            for u in range(len(units)):
                m, l, acc = carry[u]
                sc = jnp.dot(kb, qs[u], preferred_element_type=_F32)
                m_new = jnp.maximum(m, jnp.max(sc, axis=0, keepdims=True))
                p = jnp.exp2(sc - m_new)
                alpha = jnp.exp2(m - m_new)
                l = alpha * l + jnp.sum(p, axis=0, keepdims=True)
                acc = alpha * acc + jnp.dot(vb, p.astype(_MXU), preferred_element_type=_F32)
                out.append((m_new, l, acc))
            return tuple(out)

        fin = lax.fori_loop(0, nk, trip, init)
        for u, (a, c) in enumerate(units):
            m, l, acc = fin[u]
            o_ref[a, :, c * rc:(c + 1) * rc] = (acc / l).astype(out_dtype)
            lse_ref[a, :, c * rc:(c + 1) * rc] = m + jnp.log2(l)

        if gather is not None:
            pl.when(step == total - 1)(finish)

    q_blk = lambda gg, hh, i: (gg * nhp + hh, 0, i)
    in_specs = [pl.BlockSpec((hp, dk, tq), q_blk), pl.BlockSpec((1, s, dk), lambda gg, hh, i: (gg, 0, 0)),
                pl.BlockSpec((1, nk, dv, tk), lambda gg, hh, i: (gg, 0, 0, 0))]
    out_specs = [pl.BlockSpec((hp, dv, tq), q_blk), pl.BlockSpec((hp, 1, tq), q_blk)]
    out_shape = [jax.ShapeDtypeStruct((h, dv, s), out_dtype), jax.ShapeDtypeStruct((h, 1, s), _F32)]
    if gather is None:
        return pl.pallas_call(
            body, name=name, grid=(g, nhp, nq), in_specs=in_specs, out_specs=out_specs, out_shape=out_shape,
            compiler_params=_params(("parallel", "parallel", "parallel")),
        )(qt, k, vt)
    return pl.pallas_call(
        body, name=name, grid=(g, nhp, nq),
        in_specs=in_specs + [pl.BlockSpec(memory_space=pl.ANY)],
        out_specs=out_specs + [pl.BlockSpec(memory_space=pl.ANY)],
        out_shape=out_shape + [jax.ShapeDtypeStruct((N_DEV,) + gather.shape, gather.dtype)],
        scratch_shapes=_comm_scratch(),
        compiler_params=_params(("arbitrary", "arbitrary", "arbitrary")),
    )(qt, k, vt, gather)


def _attn_bwd(qt, k, v, ot, do, lse2, *, scale, name, exchange=None):
    h, dk, s = qt.shape
    g, _, dv = v.shape
    r = h // g
    hp = min(r, ATTN_CHAINS)
    tq = min(ATTN_ROWS, s)
    tk = min(ATTN_BWD_KEYS * (ATTN_CHAINS // hp), s)
    nhp, nq, nk = r // hp, s // tq, s // tk

    def body(qt_ref, k_ref, v_ref, ot_ref, do_ref, lse_ref, *rest):
        if exchange is None:
            dq_ref, dk_ref, dv_ref = rest
        else:
            g_ref, dq_ref, dk_ref, dv_ref, got_ref, send_sems, recv_sems, local_sem = rest
            start, finish = _exchange_phases(g_ref, got_ref, send_sems, recv_sems, local_sem)
            step, total = _grid_step((g, nhp, nq))
            pl.when(step == 0)(start)
        hh, i = pl.program_id(1), pl.program_id(2)

        @pl.when((hh == 0) & (i == 0))
        def _():
            dk_ref[...] = jnp.zeros_like(dk_ref)
            dv_ref[...] = jnp.zeros_like(dv_ref)

        qts = [qt_ref[a] for a in range(hp)]
        qs = [qt_ref[a].T for a in range(hp)]
        dos = [do_ref[a] for a in range(hp)]
        dots = [do_ref[a].T for a in range(hp)]
        ls = [lse_ref[a].T for a in range(hp)]
        dls = [jnp.sum(do_ref[a].astype(_F32) * ot_ref[a].T.astype(_F32), axis=1, keepdims=True)
               for a in range(hp)]

        def trip(j, carry):
            rows = pl.ds(pl.multiple_of(j * tk, tk), tk)
            kb, vb = k_ref[0, rows, :], v_ref[0, rows, :]
            out = []
            for a in range(hp):
                p = jnp.exp2(lax.dot_general(qs[a], kb, _NT, preferred_element_type=_F32) - ls[a])
                dp = lax.dot_general(dos[a], vb, _NT, preferred_element_type=_F32)
                ds = (p * (dp - dls[a])).astype(_MXU)
                dv_ref[0, j] += jnp.dot(dots[a], p.astype(_MXU), preferred_element_type=_F32)
                dk_ref[0, j] += jnp.dot(qts[a], ds, preferred_element_type=_F32)
                out.append(carry[a] + jnp.dot(ds, kb, preferred_element_type=_F32))
            return tuple(out)

        fin = lax.fori_loop(0, nk, trip, tuple(jnp.zeros((tq, dk), _F32) for _ in range(hp)))
        for a in range(hp):
            dq_ref[a] = (fin[a] * scale).astype(dq_ref.dtype)

        @pl.when((hh == nhp - 1) & (i == nq - 1))
        def _():
            dk_ref[...] = dk_ref[...] * LN_2

        if exchange is not None:
            pl.when(step == total - 1)(finish)

    q_blk = lambda gg, hh, i: (gg * nhp + hh, i, 0)
    qt_blk = lambda gg, hh, i: (gg * nhp + hh, 0, i)
    kv_blk = lambda gg, hh, i: (gg, 0, 0)
    in_specs = [pl.BlockSpec((hp, dk, tq), qt_blk), pl.BlockSpec((1, s, dk), kv_blk), pl.BlockSpec((1, s, dv), kv_blk),
                pl.BlockSpec((hp, dv, tq), qt_blk), pl.BlockSpec((hp, tq, dv), q_blk), pl.BlockSpec((hp, 1, tq), qt_blk)]
    out_specs = [pl.BlockSpec((hp, tq, dk), q_blk), pl.BlockSpec((1, nk, dk, tk), lambda gg, hh, i: (gg, 0, 0, 0)),
                 pl.BlockSpec((1, nk, dv, tk), lambda gg, hh, i: (gg, 0, 0, 0))]
    out_shape = [jax.ShapeDtypeStruct((h, s, dk), _ACT), jax.ShapeDtypeStruct((g, nk, dk, tk), _F32),
                 jax.ShapeDtypeStruct((g, nk, dv, tk), _F32)]
    args = (qt, k, v, ot, do, lse2)
    scratch = []
    if exchange is not None:
        in_specs, args = in_specs + [pl.BlockSpec(memory_space=pl.ANY)], args + (exchange,)
        out_specs = out_specs + [pl.BlockSpec(memory_space=pl.ANY)]
        out_shape = out_shape + [jax.ShapeDtypeStruct(exchange.shape, exchange.dtype)]
        scratch = _comm_scratch()
    return pl.pallas_call(
        body, name=name, grid=(g, nhp, nq), in_specs=in_specs, out_specs=out_specs, out_shape=out_shape,
        scratch_shapes=scratch, compiler_params=_params(("arbitrary", "arbitrary", "arbitrary")),
    )(*args)


def _unchunk(xt):
    g, nk, d, tk = xt.shape
    return xt.transpose(1, 3, 0, 2).reshape(nk * tk, g * d)


def _chunk_t(x2d, g):
    s = x2d.shape[0]
    tk = min(ATTN_FWD_KEYS, s)
    return x2d.reshape(s // tk, tk, g, -1).transpose(2, 0, 3, 1)


def _heads_t(x2d, h):
    s = x2d.shape[0]
    return x2d.reshape(s, h, -1).transpose(1, 2, 0)


def _unheads_t(xt):
    h, d, s = xt.shape
    return xt.transpose(2, 0, 1).reshape(s, h * d)


WIN_ROWS = 512


def _win_geometry(s, dil):
    t = min(WIN_ROWS, s)
    length = s // dil
    lg = length.bit_length() - 1
    assert 1 << lg == length and t % DIL_HALF == 0 and s % t == 0
    return t, t + 2 * DIL_HALF, lg, t // DIL_HALF, s // DIL_HALF


def _win_specs(hn, t, d, halo_per_blk, n_halo):
    return [pl.BlockSpec((hn, DIL_HALF, d), lambda i: (0, jnp.maximum(i * halo_per_blk - 1, 0), 0)),
            pl.BlockSpec((hn, t, d), lambda i: (0, i, 0)),
            pl.BlockSpec((hn, DIL_HALF, d), lambda i: (0, jnp.minimum((i + 1) * halo_per_blk, n_halo - 1), 0))]


def _win_mask(i, t, w, lg, dil, wide_rows):
    shape = (w, t) if wide_rows else (t, w)
    rows = lax.broadcasted_iota(jnp.int32, shape, 0)
    cols = lax.broadcasted_iota(jnp.int32, shape, 1)
    base = i * t
    if wide_rows:
        pq, pk = base - DIL_HALF + rows, base + cols
    else:
        pq, pk = base + rows, base - DIL_HALF + cols
    arel = jnp.abs(pk - pq)
    valid = (arel <= DIL_HALF) & ((pk >> lg) == (pq >> lg))
    return valid, (-dil * arel).astype(_F32)


def _win3(lo_ref, mid_ref, hi_ref, a):
    return jnp.concatenate([lo_ref[a], mid_ref[a], hi_ref[a]], axis=0)


def _win_fwd(q, k, v, slopes, *, dil, name):
    hn, s, d = q.shape
    t, w, lg, hpb, n_halo = _win_geometry(s, dil)
    scale = d ** -0.5

    def body(sl_ref, q_ref, klo, kmid, khi, vlo, vmid, vhi, o_ref, lse_ref):
        valid, nb = _win_mask(pl.program_id(0), t, w, lg, dil, False)
        for a in range(hn):
            kw, vw = _win3(klo, kmid, khi, a), _win3(vlo, vmid, vhi, a)
            sc = lax.dot_general(q_ref[a], kw, _NT, preferred_element_type=_F32) * scale
            sc = jnp.where(valid, sc + sl_ref[a] * nb, NEG_INF)
            m = jnp.max(sc, axis=1, keepdims=True)
            e = jnp.exp(sc - m)
            den = jnp.sum(e, axis=1, keepdims=True)
            o_ref[a] = jnp.dot(e.astype(_MXU), vw, preferred_element_type=_F32) / den
            lse_ref[a] = m + jnp.log(den)

    kv = _win_specs(hn, t, d, hpb, n_halo)
    blk = lambda c: pl.BlockSpec((hn, t, c), lambda i: (0, i, 0))
    return pl.pallas_call(
        body, name=name, grid=(s // t,),
        in_specs=[pl.BlockSpec(memory_space=pltpu.SMEM), blk(d)] + kv + kv,
        out_specs=[blk(d), blk(1)],
        out_shape=[jax.ShapeDtypeStruct((hn, s, d), _F32), jax.ShapeDtypeStruct((hn, s, 1), _F32)],
        compiler_params=_params(("parallel",)),
    )(slopes.astype(_F32), q, k, k, k, v, v, v)


def _win_bwd_dq(q, k, v, do, lse, delta, slopes, *, dil, name):
    hn, s, d = q.shape
    t, w, lg, hpb, n_halo = _win_geometry(s, dil)
    scale = d ** -0.5

    def body(sl_ref, q_ref, klo, kmid, khi, vlo, vmid, vhi, do_ref, lse_ref, dl_ref, dq_ref):
        valid, nb = _win_mask(pl.program_id(0), t, w, lg, dil, False)
        for a in range(hn):
            kw, vw = _win3(klo, kmid, khi, a), _win3(vlo, vmid, vhi, a)
            sc = lax.dot_general(q_ref[a], kw, _NT, preferred_element_type=_F32) * scale
            p = jnp.exp(jnp.where(valid, sc + sl_ref[a] * nb, NEG_INF) - lse_ref[a])
            dp = lax.dot_general(do_ref[a], vw, _NT, preferred_element_type=_F32)
            ds = (p * (dp - dl_ref[a])).astype(_MXU)
            dq_ref[a] = (jnp.dot(ds, kw, preferred_element_type=_F32) * scale).astype(dq_ref.dtype)

    kv = _win_specs(hn, t, d, hpb, n_halo)
    blk = lambda c: pl.BlockSpec((hn, t, c), lambda i: (0, i, 0))
    return pl.pallas_call(
        body, name=name, grid=(s // t,),
        in_specs=[pl.BlockSpec(memory_space=pltpu.SMEM), blk(d)] + kv + kv + [blk(d), blk(1), blk(1)],
        out_specs=blk(d),
        out_shape=jax.ShapeDtypeStruct((hn, s, d), _ACT),
        compiler_params=_params(("parallel",)),
    )(slopes.astype(_F32), q, k, k, k, v, v, v, do, lse, delta)


def _win_bwd_dkv(q, k, v, do, lse, delta, slopes, *, dil, name):
    hn, s, d = q.shape
    t, w, lg, hpb, n_halo = _win_geometry(s, dil)
    scale = d ** -0.5

    def body(sl_ref, qlo, qmid, qhi, dolo, domid, dohi, llo, lmid, lhi, dllo, dlmid, dlhi, k_ref, v_ref,
             dk_ref, dv_ref):
        valid, nb = _win_mask(pl.program_id(0), t, w, lg, dil, True)
        for a in range(hn):
            qw, dow = _win3(qlo, qmid, qhi, a), _win3(dolo, domid, dohi, a)
            lw, dlw = _win3(llo, lmid, lhi, a), _win3(dllo, dlmid, dlhi, a)
            sc = lax.dot_general(qw, k_ref[a], _NT, preferred_element_type=_F32) * scale
            p = jnp.exp(jnp.where(valid, sc + sl_ref[a] * nb, NEG_INF) - lw)
            dp = lax.dot_general(dow, v_ref[a], _NT, preferred_element_type=_F32)
            ds = (p * (dp - dlw)).astype(_MXU)
            dv_ref[a] = lax.dot_general(p.astype(_MXU), dow, _TN, preferred_element_type=_F32).astype(dv_ref.dtype)
            dk_ref[a] = (lax.dot_general(ds, qw, _TN, preferred_element_type=_F32) * scale).astype(dk_ref.dtype)

    blk = lambda c: pl.BlockSpec((hn, t, c), lambda i: (0, i, 0))
    return pl.pallas_call(
        body, name=name, grid=(s // t,),
        in_specs=([pl.BlockSpec(memory_space=pltpu.SMEM)] + _win_specs(hn, t, d, hpb, n_halo)
                  + _win_specs(hn, t, d, hpb, n_halo) + _win_specs(hn, t, 1, hpb, n_halo)
                  + _win_specs(hn, t, 1, hpb, n_halo) + [blk(d), blk(d)]),
        out_specs=[blk(d), blk(d)],
        out_shape=[jax.ShapeDtypeStruct((hn, s, d), _ACT), jax.ShapeDtypeStruct((hn, s, d), _ACT)],
        compiler_params=_params(("parallel",)),
    )(slopes.astype(_F32), q, q, q, do, do, do, lse, lse, lse, delta, delta, delta, k, v)


def _merge_weights(lse):
    mx = jnp.max(lse, axis=0, keepdims=True)
    e = jnp.exp(lse - mx)
    return e / jnp.sum(e, axis=0, keepdims=True)


def _merge_fwd(o3, lse3, *, name):
    ng, sl, s, d = o3.shape
    t = _rows(s, 512)

    def body(o_ref, l_ref, c_ref):
        wts = _merge_weights(l_ref[:, 0])
        c_ref[0] = jnp.sum(wts * o_ref[:, 0], axis=0).astype(c_ref.dtype)

    return pl.pallas_call(
        body, name=name, grid=(sl, s // t),
        in_specs=[pl.BlockSpec((ng, 1, t, d), lambda a, i: (0, a, i, 0)),
                  pl.BlockSpec((ng, 1, t, 1), lambda a, i: (0, a, i, 0))],
        out_specs=pl.BlockSpec((1, t, d), lambda a, i: (a, i, 0)),
        out_shape=jax.ShapeDtypeStruct((sl, s, d), _ACT),
        compiler_params=_params(("parallel", "parallel")),
    )(o3, lse3)


def _merge_bwd(dcomb, o3, lse3, *, name):
    ng, sl, s, d = o3.shape
    t = _rows(s, 512)

    def body(dc_ref, o_ref, l_ref, do_ref, dl_ref):
        wts = _merge_weights(l_ref[:, 0])
        dc = dc_ref[0].astype(_F32)
        comb = jnp.sum(wts * o_ref[:, 0], axis=0)
        do_ref[:, 0] = (wts * dc[None]).astype(do_ref.dtype)
        dl_ref[:, 0] = wts * jnp.sum(dc * comb, axis=-1, keepdims=True)[None]

    big = pl.BlockSpec((ng, 1, t, d), lambda a, i: (0, a, i, 0))
    small = pl.BlockSpec((ng, 1, t, 1), lambda a, i: (0, a, i, 0))
    return pl.pallas_call(
        body, name=name, grid=(sl, s // t),
        in_specs=[pl.BlockSpec((1, t, d), lambda a, i: (a, i, 0)), big, small],
        out_specs=[big, small],
        out_shape=[jax.ShapeDtypeStruct((ng, sl, s, d), _ACT), jax.ShapeDtypeStruct((ng, sl, s, 1), _F32)],
        compiler_params=_params(("parallel", "parallel")),
    )(dcomb, o3, lse3)


SWIGLU_ROWS = 512


def _gate_up(a, w_in_t, *, name):
    m, k = a.shape
    n = w_in_t.shape[0] // 2
    tm, tn, tk = _pick(m, SWIGLU_ROWS), _pick(n, MM_TILE), _pick(k, MM_TILE)
    nk = k // tk

    def body(a_ref, g_ref, u_ref, go_ref, uo_ref, act_ref, acc_g, acc_u):
        kk = pl.program_id(2)

        @pl.when(kk == 0)
        def _():
            acc_g[...] = jnp.zeros_like(acc_g)
            acc_u[...] = jnp.zeros_like(acc_u)

        av = a_ref[...].astype(_MXU)
        acc_g[...] += lax.dot_general(av, g_ref[...].astype(_MXU), _NT, preferred_element_type=_F32)
        acc_u[...] += lax.dot_general(av, u_ref[...].astype(_MXU), _NT, preferred_element_type=_F32)

        @pl.when(kk == nk - 1)
        def _():
            gf, uf = acc_g[...], acc_u[...]
            go_ref[...] = gf.astype(go_ref.dtype)
            uo_ref[...] = uf.astype(uo_ref.dtype)
            act_ref[...] = (gf * jax.nn.sigmoid(gf) * uf).astype(act_ref.dtype)

    o_spec = pl.BlockSpec((tm, tn), lambda i, j, kk: (i, j))
    out = jax.ShapeDtypeStruct((m, n), _ACT)
    return pl.pallas_call(
        body, name=name, grid=(m // tm, n // tn, nk),
        in_specs=[pl.BlockSpec((tm, tk), lambda i, j, kk: (i, kk)),
                  pl.BlockSpec((tn, tk), lambda i, j, kk: (j, kk)),
                  pl.BlockSpec((tn, tk), lambda i, j, kk: (j + n // tn, kk))],
        out_specs=[o_spec, o_spec, o_spec], out_shape=[out, out, out],
        scratch_shapes=[pltpu.VMEM((tm, tn), _F32), pltpu.VMEM((tm, tn), _F32)],
        compiler_params=_params(("parallel", "parallel", "arbitrary")),
    )(a, w_in_t, w_in_t)


def _gate_up_bwd(dout, w_out, gate, up, *, name):
    m, k = dout.shape
    n = w_out.shape[0]
    tm, tn, tk = _pick(m, SWIGLU_ROWS), _pick(n, MM_TILE), _pick(k, MM_TILE)
    nk = k // tk

    def body(d_ref, w_ref, g_ref, u_ref, dg_ref, du_ref, acc):
        kk = pl.program_id(2)

        @pl.when(kk == 0)
        def _():
            acc[...] = jnp.zeros_like(acc)

        acc[...] += lax.dot_general(d_ref[...].astype(_MXU), w_ref[...].astype(_MXU), _NT,
                                    preferred_element_type=_F32)

        @pl.when(kk == nk - 1)
        def _():
            daf = acc[...]
            gf = g_ref[...].astype(_F32)
            sg = jax.nn.sigmoid(gf)
            dg_ref[...] = (daf * u_ref[...].astype(_F32) * (sg + gf * sg * (1.0 - sg))).astype(dg_ref.dtype)
            du_ref[...] = (daf * (gf * sg)).astype(du_ref.dtype)

    o_spec = pl.BlockSpec((tm, tn), lambda i, j, kk: (i, j))
    out = jax.ShapeDtypeStruct((m, n), _ACT)
    return pl.pallas_call(
        body, name=name, grid=(m // tm, n // tn, nk),
        in_specs=[pl.BlockSpec((tm, tk), lambda i, j, kk: (i, kk)), pl.BlockSpec((tn, tk), lambda i, j, kk: (j, kk)),
                  o_spec, o_spec],
        out_specs=[o_spec, o_spec], out_shape=[out, out],
        scratch_shapes=[pltpu.VMEM((tm, tn), _F32)],
        compiler_params=_params(("parallel", "parallel", "arbitrary")),
    )(dout, w_out, gate, up)


def _final_loss(x, g, target, *, name):
    m, d = x.shape
    tm = _rows(m, 512)

    def body(x_ref, g_ref, t_ref, loss_ref, dx_ref, dg_ref):
        @pl.when(pl.program_id(0) == 0)
        def _():
            loss_ref[...] = jnp.zeros_like(loss_ref)
            dg_ref[...] = jnp.zeros_like(dg_ref)

        xf = x_ref[...]
        r = lax.rsqrt(jnp.mean(xf * xf, axis=-1, keepdims=True) + NORM_EPS)
        xh = xf * r
        err = xh * g_ref[...] - t_ref[...]
        loss_ref[...] += 0.5 * jnp.sum(jnp.mean(err * err, axis=-1, keepdims=True))
        dy = err * (1.0 / d)
        dg_ref[...] += jnp.sum(dy * xh, axis=0, keepdims=True)
        gdy = dy * g_ref[...]
        dx_ref[...] = r * (gdy - xh * jnp.mean(gdy * xh, axis=-1, keepdims=True))

    row = pl.BlockSpec((tm, d), lambda i: (i, 0))
    vec = pl.BlockSpec((1, d), lambda i: (0, 0))
    loss, dx, dg = pl.pallas_call(
        body, name=name, grid=(m // tm,),
        in_specs=[row, vec, row],
        out_specs=[pl.BlockSpec((8, LANES), lambda i: (0, 0)), row, vec],
        out_shape=[jax.ShapeDtypeStruct((8, LANES), _F32), jax.ShapeDtypeStruct((m, d), _F32),
                   jax.ShapeDtypeStruct((1, d), _F32)],
        compiler_params=_params(("arbitrary",)),
    )(x, g.reshape(1, d), target)
    return loss[0, 0], dx, dg.reshape(d)


def _reduce_adamw(parts, w, m, v, *, name):
    rws, cols = w.shape
    tr = rws
    for cand in range(min(rws, 256), 0, -SUBLANES_16BIT):
        if cand % SUBLANES_16BIT == 0 and rws % cand == 0:
            tr = cand
            break

    def body(p_ref, w_ref, m_ref, v_ref, g_ref, d_ref, nm_ref, nv_ref):
        gsum = p_ref[0].astype(_F32)
        for dev in range(1, N_DEV):
            gsum = gsum + p_ref[dev].astype(_F32)
        m2 = ADAM_B1 * m_ref[...] + (1.0 - ADAM_B1) * gsum
        v2 = ADAM_B2 * v_ref[...] + (1.0 - ADAM_B2) * (gsum * gsum)
        m_hat = m2 / (1.0 - ADAM_B1 ** ADAM_STEP)
        v_hat = v2 / (1.0 - ADAM_B2 ** ADAM_STEP)
        g_ref[...] = gsum
        d_ref[...] = -ADAM_LR * (m_hat / (jnp.sqrt(v_hat) + ADAM_EPS) + ADAM_WD * w_ref[...])
        nm_ref[...] = m2
        nv_ref[...] = v2

    blk = pl.BlockSpec((tr, cols), lambda i: (i, 0))
    out = jax.ShapeDtypeStruct((rws, cols), _F32)
    return pl.pallas_call(
        body, name=name, grid=(rws // tr,),
        in_specs=[pl.BlockSpec((N_DEV, tr, cols), lambda i: (0, i, 0)), blk, blk, blk],
        out_specs=[blk, blk, blk, blk], out_shape=[out, out, out, out],
        compiler_params=_params(("parallel",)),
    )(parts, w, m, v)


def _mesh_pos():
    return lax.axis_index("x"), lax.axis_index("y"), lax.axis_index("c")


def _all_gather(block):
    rws, cols = block.shape

    def body(x_ref, out_ref, send_sems, recv_sems, local_sem):
        start, forward, finish = _gather_phases(x_ref, out_ref, send_sems, recv_sems, local_sem)
        start()
        forward()
        finish()

    return pl.pallas_call(
        body, name="weights_all_gather",
        out_shape=jax.ShapeDtypeStruct((N_DEV, rws, cols), block.dtype),
        in_specs=[pl.BlockSpec(memory_space=pl.ANY)],
        out_specs=pl.BlockSpec(memory_space=pl.ANY),
        scratch_shapes=_comm_scratch(),
    )(block)


def _comm_scratch():
    return [pltpu.SemaphoreType.DMA((N_DEV - 1,)), pltpu.SemaphoreType.DMA((N_DEV - 1,)), pltpu.SemaphoreType.DMA]


def _gather_phases(x_ref, out_ref, send_sems, recv_sems, local_sem):
    x, y, c = _mesh_pos()
    me, sibling = (x, y, c), (x, y, 1 - c)
    chips = [(1 - x, y), (x, 1 - y), (1 - x, 1 - y)]

    def slot(px, py, pc):
        return out_ref.at[4 * px + 2 * py + pc]

    def copy(k, blk, to, src=None):
        return pltpu.make_async_remote_copy(
            src_ref=slot(*blk) if src is None else src, dst_ref=slot(*blk),
            send_sem=send_sems.at[k], recv_sem=recv_sems.at[k],
            device_id=to, device_id_type=pl.DeviceIdType.MESH)

    mine = pltpu.make_async_copy(x_ref, slot(*me), local_sem)
    first = [copy(0, me, sibling, src=x_ref)]
    first += [copy(1 + j, me, (*chip, c), src=x_ref) for j, chip in enumerate(chips)]
    passed = [copy(4 + j, (*chip, c), sibling) for j, chip in enumerate(chips)]

    def start():
        mine.start()
        for cp in first:
            cp.start()

    def forward():
        for j, chip in enumerate(chips):
            copy(1 + j, (*chip, c), me).wait_recv()
            passed[j].start()

    def finish():
        copy(0, sibling, me).wait_recv()
        for j, chip in enumerate(chips):
            copy(4 + j, (*chip, 1 - c), me).wait_recv()
        for cp in first + passed:
            cp.wait_send()
        mine.wait()

    return start, forward, finish


def _exchange(parts):
    def body(g_ref, out_ref, send_sems, recv_sems, local_sem):
        start, finish = _exchange_phases(g_ref, out_ref, send_sems, recv_sems, local_sem)
        start()
        finish()

    return pl.pallas_call(
        body, name="grads_exchange",
        out_shape=jax.ShapeDtypeStruct(parts.shape, parts.dtype),
        in_specs=[pl.BlockSpec(memory_space=pl.ANY)],
        out_specs=pl.BlockSpec(memory_space=pl.ANY),
        scratch_shapes=_comm_scratch(),
    )(parts)


def _exchange_phases(g_ref, out_ref, send_sems, recv_sems, local_sem):
    x, y, c = _mesh_pos()
    me = 4 * x + 2 * y + c
    mine = pltpu.make_async_copy(g_ref.at[me], out_ref.at[me], local_sem)
    sends, arrivals = [], []
    for k in range(1, N_DEV):
        px = 1 - x if k & 4 else x
        py = 1 - y if k & 2 else y
        pc = 1 - c if k & 1 else c
        peer = 4 * px + 2 * py + pc
        sems = dict(send_sem=send_sems.at[k - 1], recv_sem=recv_sems.at[k - 1],
                    device_id=(px, py, pc), device_id_type=pl.DeviceIdType.MESH)
        sends.append(pltpu.make_async_remote_copy(src_ref=g_ref.at[peer], dst_ref=out_ref.at[me], **sems))
        arrivals.append(pltpu.make_async_remote_copy(src_ref=g_ref.at[peer], dst_ref=out_ref.at[peer], **sems))

    def start():
        mine.start()
        for cp in sends:
            cp.start()

    def finish():
        for cp in arrivals:
            cp.wait_recv()
        for cp in sends:
            cp.wait_send()
        mine.wait()

    return start, finish


_BIG = (("w_in_ab", 2), ("mla_w_uq", 1), ("mla_w_ukv", 1), ("w_out_ab", 2), ("gqa_w_q", 1), ("gqa_w_kv", 1),
        ("gqa_w_o", 1), ("ffn_w_in", 1), ("ffn_w_out", 1))
_TRANSPOSED = ("ffn_w_in",)


def _stored(arrays):
    return {n: (a.transpose(0, 2, 1) if n in _TRANSPOSED else a) for n, a in arrays.items()}
_SMALL = ("mix_norm_ab", "ffn_norm", "final_norm", "mix_norm_c", "mla_q_norm", "mla_kv_norm", "gqa_q_norm", "gqa_k_norm")
SMALL_ROWS = 16
GATHER_TAIL_ROWS = 16
EXCHANGE_TAIL_ROWS = 32


def _view3(a):
    return a.reshape(a.shape[0], a.shape[1], -1)


def _pad_rows(a2d):
    pad = -a2d.shape[0] % SUBLANES_16BIT
    return jnp.pad(a2d, ((0, pad), (0, 0))) if pad else a2d


def _pack_rows(shard):
    return _pad_rows(shard.reshape(-1, PACK_COLS))


def _packed_rows(shape):
    n = 1
    for d in shape:
        n *= d
    rows = n // PACK_COLS
    return rows + (-rows % SUBLANES_16BIT)


_FIRST = {"w_in_ab": (0, 1), "mla_w_uq": (0, 1), "mla_w_ukv": (0, 1), "w_out_ab": (0, 1), "ffn_w_in": (0, 1),
          "ffn_w_out": (0, 1)}
_REST = {"w_in_ab": (1, 1), "mla_w_uq": (1, 1), "mla_w_ukv": (1, 1), "w_out_ab": (1, 1), "gqa_w_q": (0, 2),
         "gqa_w_kv": (0, 2), "gqa_w_o": (0, 2), "ffn_w_in": (1, 3), "ffn_w_out": (1, 3)}
PACK_ROWS = 128


def _group_of(arrays, group):
    return {n: arrays[n][lo:lo + cnt] for n, (lo, cnt) in group.items()}


def _pack_big(shards):
    packed = jnp.concatenate([_pack_rows(shards[n]) for n, _ in _BIG if n in shards], axis=0)
    return jnp.pad(packed, ((0, -packed.shape[0] % PACK_ROWS), (0, 0)))


def _unpack_big(packed, shapes):
    out, off = {}, 0
    for n, _ in _BIG:
        if n not in shapes:
            continue
        size = 1
        for d in shapes[n]:
            size *= d
        out[n] = packed[off:off + size // PACK_COLS].reshape(shapes[n])
        off += _packed_rows(shapes[n])
    return out


def _unpack_gathered(gathered, shapes):
    out, off = {}, 0
    for n, axis in _BIG:
        if n not in shapes:
            continue
        size = 1
        for d in shapes[n]:
            size *= d
        l3 = (shapes[n][0], shapes[n][1], size // (shapes[n][0] * shapes[n][1]))
        sh = gathered[:, off:off + size // PACK_COLS].reshape((N_DEV,) + l3)
        if axis == 1:
            full = sh.transpose(1, 0, 2, 3).reshape(l3[0], N_DEV * l3[1], l3[2])
        else:
            full = sh.transpose(1, 2, 0, 3).reshape(l3[0], l3[1], N_DEV * l3[2])
        out[n] = full
        off += _packed_rows(shapes[n])
    return out


def _split_for_devices(full, axis):
    l, rws, cols = full.shape
    if axis == 1:
        sh = full.reshape(l, N_DEV, rws // N_DEV, cols).transpose(1, 0, 2, 3)
    else:
        sh = full.reshape(l, rws, N_DEV, cols // N_DEV).transpose(2, 0, 1, 3)
    flat = sh.reshape(N_DEV, -1, PACK_COLS)
    pad = -flat.shape[1] % SUBLANES_16BIT
    return jnp.pad(flat, ((0, 0), (0, pad), (0, 0))) if pad else flat


def _to_bits16(a_f32_rows):
    r = a_f32_rows.shape[0]
    return lax.bitcast_convert_type(a_f32_rows, jnp.bfloat16).reshape(2 * r, PACK_COLS)


def _from_bits16(a_bf16_rows):
    lead, r = a_bf16_rows.shape[:-2], a_bf16_rows.shape[-2]
    return lax.bitcast_convert_type(a_bf16_rows.reshape(lead + (r // 2, PACK_COLS, 2)), _F32)


def _small_sizes():
    return {"mix_norm_ab": 2 * D_MODEL, "ffn_norm": DEPTH * D_MODEL, "final_norm": D_MODEL, "mix_norm_c": 2 * D_MODEL,
            "mla_q_norm": 2 * MLA_Q_RANK, "mla_kv_norm": 2 * MLA_KV_RANK, "gqa_q_norm": 2 * GQA_HEAD_DIM,
            "gqa_k_norm": 2 * GQA_HEAD_DIM}


def _pack_small(vals):
    flat = jnp.concatenate([vals[n].reshape(-1).astype(_F32) for n in _SMALL])
    return jnp.pad(flat, (0, SMALL_ROWS * PACK_COLS - flat.shape[0])).reshape(SMALL_ROWS, PACK_COLS)


def _unpack_small(pack):
    flat, out, off = pack.reshape(-1), {}, 0
    sizes = _small_sizes()
    for n in _SMALL:
        out[n] = flat[off:off + sizes[n]]
        off += sizes[n]
    return out


def _angles(pos, dim):
    freqs = ROPE_THETA ** (-jnp.arange(0, dim, 2, dtype=_F32) / dim)
    ang = pos.astype(_F32)[:, None] * freqs[None, :]
    return jnp.cos(ang), jnp.sin(ang)


def _rope_tables(s):
    pos = jnp.arange(s)
    cos_t, sin_t = _angles(pos, MLA_ROPE)
    mla_c = jnp.tile(jnp.concatenate([cos_t, cos_t], -1), (1, LANES // 32))
    mla_s = jnp.tile(jnp.concatenate([-sin_t, sin_t], -1), (1, LANES // 32))
    rows = s // GRID_W
    row_idx = jnp.broadcast_to(jnp.arange(rows)[:, None], (rows, GRID_W)).reshape(-1)
    col_idx = jnp.broadcast_to(jnp.arange(GRID_W)[None, :], (rows, GRID_W)).reshape(-1)
    cos_r, sin_r = _angles(row_idx, GQA_HEAD_DIM // 2)
    cos_c, sin_c = _angles(col_idx, GQA_HEAD_DIM // 2)
    gqa_c = jnp.tile(jnp.concatenate([cos_r, cos_r, cos_c, cos_c], -1), (1, LANES // GQA_HEAD_DIM))
    gqa_s = jnp.tile(jnp.concatenate([-sin_r, sin_r, -sin_c, sin_c], -1), (1, LANES // GQA_HEAD_DIM))
    return (mla_c, mla_s), (gqa_c, gqa_s)


def _heads(x2d, h):
    s = x2d.shape[0]
    return x2d.reshape(s, h, -1).transpose(1, 0, 2)


def _unheads(xh):
    h, s, d = xh.shape
    return xh.transpose(1, 0, 2).reshape(s, h * d)


def _to_res(x2d, dil):
    s = x2d.shape[0]
    return x2d.reshape(s // dil, dil, DIL_SLOTS, -1).transpose(2, 1, 0, 3).reshape(DIL_SLOTS, s, -1)


def _from_res(xh, dil):
    sl, s, d = xh.shape
    return xh.reshape(sl, dil, s // dil, d).transpose(2, 1, 0, 3).reshape(s, sl * d)


def _res_to_tok(xh, dil):
    sl, s, c = xh.shape
    return xh.reshape(sl, dil, s // dil, c).transpose(0, 2, 1, 3).reshape(sl, s, c)


def _tok_to_res(xh, dil):
    sl, s, c = xh.shape
    return xh.reshape(sl, s // dil, dil, c).transpose(0, 2, 1, 3).reshape(sl, s, c)


def _ffn_fwd(x, w, tag):
    hn = _rmsnorm(x, w["norm"], out_dtype=_ACT, name=f"{tag}_norm")
    gate, up, act = _gate_up(hn, w["w_in_t"], name=f"{tag}_gate_up")
    out = _matmul(act, w["w_out"], res=x, name=f"{tag}_out")
    return out, (x, hn, gate, up, act)


def _ffn_bwd(dout, saved, w, tag):
    x, hn, gate, up, act = saved
    w_gate_t, w_up_t = w["w_in_t"][:FFN_HIDDEN], w["w_in_t"][FFN_HIDDEN:]
    d_w_out = _matmul(act, dout, trans_a=True, name=f"{tag}_dwout")
    dgate, dup = _gate_up_bwd(dout, w["w_out"], gate, up, name=f"{tag}_dgate_up")
    d_w_gate_t = _matmul(dgate, hn, trans_a=True, name=f"{tag}_dwgate")
    d_w_up_t = _matmul(dup, hn, trans_a=True, name=f"{tag}_dwup")
    dhn = _matmul(dgate, w_gate_t, name=f"{tag}_dhn_gate")
    dhn = _matmul(dup, w_up_t, res=dhn, name=f"{tag}_dhn_up")
    dx, dnorm = _rmsnorm_bwd(x, w["norm"], dhn, dout, name=f"{tag}_dnorm")
    return dx, {"norm": dnorm, "w_in_t": jnp.concatenate([d_w_gate_t, d_w_up_t], axis=0), "w_out": d_w_out}


def _even_fwd(x, w, tabs, slopes, tag, gather=None):
    s = x.shape[0]
    (mla_c, mla_s), _ = tabs
    hn = _rmsnorm(x, w["norm"], out_dtype=_ACT, name=f"{tag}_norm")
    za = _matmul(hn, w["w_a"], name=f"{tag}_in_a")
    zb = _matmul(hn, w["w_b"], out_dtype=_ACT, name=f"{tag}_in_b")
    cq, ckv, kr = za[:, :MLA_Q_RANK], za[:, MLA_Q_RANK:MLA_Q_RANK + MLA_KV_RANK], za[:, MLA_Q_RANK + MLA_KV_RANK:]
    cqn = _rmsnorm(cq, w["q_norm"], out_dtype=_ACT, name=f"{tag}_qnorm")
    ckvn = _rmsnorm(ckv, w["kv_norm"], out_dtype=_ACT, name=f"{tag}_kvnorm")
    q = _matmul(cqn, w["w_uq"], scale=(MLA_NOPE + MLA_ROPE) ** -0.5 * LOG2_E,
                name=f"{tag}_uq")
    kv = _matmul(ckvn, w["w_ukv"], out_dtype=_ACT, name=f"{tag}_ukv")
    nn = MLA_HEADS * MLA_NOPE
    q_rope = _rope(q[:, nn:], mla_c, mla_s, out_dtype=_ACT, name=f"{tag}_rope_q")
    k_rope = _rope(kr, mla_c, mla_s, out_dtype=_ACT, name=f"{tag}_rope_k")[:, :MLA_ROPE]
    qh = jnp.concatenate([_heads_t(q[:, :nn].astype(_ACT), MLA_HEADS), _heads_t(q_rope, MLA_HEADS)], axis=1)
    kh = jnp.concatenate([_heads(kv[:, :nn], MLA_HEADS),
                          jnp.broadcast_to(k_rope[None], (MLA_HEADS, s, MLA_ROPE))], axis=-1)
    vh = _heads(kv[:, nn:], MLA_HEADS)
    oa, lse_a, *gathered = _attn_fwd(qh, kh, _chunk_t(kv[:, nn:], MLA_HEADS), out_dtype=_ACT, name=f"{tag}_mla",
                                     gather=gather)

    nd, ng = DIL_HEADS * DIL_HEAD_DIM, DIL_SLOTS * DIL_HEAD_DIM
    outs, lses, dil_saved = [], [], []
    for gi, (_, dil) in enumerate(DIL_PAIRS):
        hs = slice(gi * DIL_SLOTS, (gi + 1) * DIL_SLOTS)
        qr, kr_, vr = (_to_res(zb[:, a * nd + gi * ng:a * nd + (gi + 1) * ng], dil) for a in range(3))
        o, l = _win_fwd(qr, kr_, vr, slopes[hs], dil=dil, name=f"{tag}_dil{gi}")
        dil_saved.append((qr, kr_, vr, l))
        outs.append(_res_to_tok(o, dil))
        lses.append(_res_to_tok(l, dil))
    o3, lse3 = jnp.stack(outs), jnp.stack(lses)
    comb = _merge_fwd(o3, lse3, name=f"{tag}_merge")
    cat = jnp.concatenate([_unheads_t(oa), _unheads(comb)], axis=-1)
    out = _matmul(cat, w["w_out"], res=x, name=f"{tag}_out")
    saved = (x, hn, cq, ckv, cqn, ckvn, qh, kh, vh, oa, lse_a, dil_saved, o3, lse3, cat)
    return out, saved, (gathered[0] if gathered else None)


def _even_bwd(dout, saved, w, tabs, slopes, tag, exchange=None):
    x, hn, cq, ckv, cqn, ckvn, qh, kh, vh, oa, lse_a, dil_saved, o3, lse3, cat = saved
    (mla_c, mla_s), _ = tabs
    nn = MLA_HEADS * MLA_NOPE
    dcat = _matmul(dout, w["w_out"], trans_b=True, out_dtype=_ACT, name=f"{tag}_dcat")
    d_w_out = _matmul(cat, dout, trans_a=True, name=f"{tag}_dwout")
    nv = MLA_HEADS * MLA_V

    doa = _heads(dcat[:, :nv], MLA_HEADS)
    dqh, dkt, dvt, *received = _attn_bwd(qh, kh, vh, oa, doa, lse_a, scale=(MLA_NOPE + MLA_ROPE) ** -0.5,
                                         name=f"{tag}_mla_bwd", exchange=exchange)
    dq_rope = _rope(_unheads(dqh[..., MLA_NOPE:]), mla_c, -mla_s, out_dtype=_ACT, name=f"{tag}_drope_q")
    dq = jnp.concatenate([_unheads(dqh[..., :MLA_NOPE]).astype(_ACT), dq_rope], axis=-1)
    dk3 = _unchunk(dkt).reshape(-1, MLA_HEADS, MLA_NOPE + MLA_ROPE)
    dkr = _rope(dk3[..., MLA_NOPE:].reshape(-1, MLA_HEADS * MLA_ROPE), mla_c, -mla_s, out_dtype=_F32,
                sum_chunks=True, name=f"{tag}_drope_k")
    dkv = jnp.concatenate([dk3[..., :MLA_NOPE].reshape(-1, nn), _unchunk(dvt)], axis=-1).astype(_ACT)
    d_w_uq = _matmul(cqn, dq, trans_a=True, name=f"{tag}_dwuq")
    d_w_ukv = _matmul(ckvn, dkv, trans_a=True, name=f"{tag}_dwukv")
    dcqn = _matmul(dq, w["w_uq"], trans_b=True, name=f"{tag}_dcqn")
    dckvn = _matmul(dkv, w["w_ukv"], trans_b=True, name=f"{tag}_dckvn")
    dcq, d_q_norm = _rmsnorm_bwd(cq, w["q_norm"], dcqn, out_dtype=_ACT, name=f"{tag}_dqnorm")
    dckv, d_kv_norm = _rmsnorm_bwd(ckv, w["kv_norm"], dckvn, out_dtype=_ACT, name=f"{tag}_dkvnorm")
    lane = jnp.arange(LANES) < MLA_ROPE
    dza = jnp.concatenate([dcq, dckv, jnp.where(lane[None], dkr, 0.0).astype(_ACT)], axis=-1)

    dcomb = _heads(dcat[:, nv:], DIL_SLOTS)
    do3, delta3 = _merge_bwd(dcomb, o3, lse3, name=f"{tag}_dmerge")
    dqs, dks, dvs = [], [], []
    for gi, (_, dil) in enumerate(DIL_PAIRS):
        hs = slice(gi * DIL_SLOTS, (gi + 1) * DIL_SLOTS)
        qr, kr_, vr, l = dil_saved[gi]
        grads = (qr, kr_, vr, _tok_to_res(do3[gi], dil), l, _tok_to_res(delta3[gi], dil), slopes[hs])
        dqs.append(_from_res(_win_bwd_dq(*grads, dil=dil, name=f"{tag}_dil{gi}_dq"), dil))
        b, c = _win_bwd_dkv(*grads, dil=dil, name=f"{tag}_dil{gi}_dkv")
        dks.append(_from_res(b, dil))
        dvs.append(_from_res(c, dil))
    dzb = jnp.concatenate(dqs + dks + dvs, axis=-1).astype(_ACT)

    d_w_a = _matmul(hn, dza, trans_a=True, name=f"{tag}_dwa")
    d_w_b = _matmul(hn, dzb, trans_a=True, name=f"{tag}_dwb")
    dhn = _matmul(dza, w["w_a"], trans_b=True, name=f"{tag}_dhn_a")
    dhn = _matmul(dzb, w["w_b"], trans_b=True, res=dhn, name=f"{tag}_dhn_b")
    dx, dnorm = _rmsnorm_bwd(x, w["norm"], dhn, dout, name=f"{tag}_dnorm")
    grads = {"norm": dnorm, "w_a": d_w_a, "w_b": d_w_b, "q_norm": d_q_norm, "kv_norm": d_kv_norm,
             "w_uq": d_w_uq, "w_ukv": d_w_ukv, "w_out": d_w_out}
    return dx, grads, (received[0] if received else None)


def _odd_fwd(x, w, tabs, tag):
    s = x.shape[0]
    _, (gqa_c, gqa_s) = tabs
    nk = GQA_KV_HEADS * GQA_HEAD_DIM
    hn = _rmsnorm(x, w["norm"], out_dtype=_ACT, name=f"{tag}_norm")
    q = _matmul(hn, w["w_q"], name=f"{tag}_q")
    kv = _matmul(hn, w["w_kv"], name=f"{tag}_kv")
    k = kv[:, :nk]
    qh = _heads_t(_headnorm_rope(q, w["q_norm"], gqa_c, gqa_s, scale=GQA_HEAD_DIM ** -0.5 * LOG2_E,
                                 name=f"{tag}_prep_q"), GQA_HEADS)
    kh = _heads(_headnorm_rope(k, w["k_norm"], gqa_c, gqa_s, name=f"{tag}_prep_k"), GQA_KV_HEADS)
    v = kv[:, nk:].astype(_ACT)
    vh = _heads(v, GQA_KV_HEADS)
    o, lse = _attn_fwd(qh, kh, _chunk_t(v, GQA_KV_HEADS), out_dtype=_ACT, name=f"{tag}_gqa")
    ocat = _unheads_t(o)
    out = _matmul(ocat, w["w_o"], res=x, name=f"{tag}_out")
    return out, (x, hn, q, k, qh, kh, vh, o, lse, ocat)


def _odd_bwd(dout, saved, w, tabs, tag):
    x, hn, q, k, qh, kh, vh, o, lse, ocat = saved
    s = x.shape[0]
    _, (gqa_c, gqa_s) = tabs
    docat = _matmul(dout, w["w_o"], trans_b=True, out_dtype=_ACT, name=f"{tag}_docat")
    d_w_o = _matmul(ocat, dout, trans_a=True, name=f"{tag}_dwo")
    doh = _heads(docat, GQA_HEADS)
    dqh, dkt, dvt = _attn_bwd(qh, kh, vh, o, doh, lse, scale=GQA_HEAD_DIM ** -0.5, name=f"{tag}_gqa_bwd")
    dq, d_q_norm = _headnorm_rope_bwd(q, w["q_norm"], _unheads(dqh), gqa_c, gqa_s, name=f"{tag}_dprep_q")
    dk, d_k_norm = _headnorm_rope_bwd(k, w["k_norm"], _unchunk(dkt), gqa_c, gqa_s, name=f"{tag}_dprep_k")
    dkv = jnp.concatenate([dk, _unchunk(dvt).astype(_ACT)], axis=-1)
    d_w_q = _matmul(hn, dq, trans_a=True, name=f"{tag}_dwq")
    d_w_kv = _matmul(hn, dkv, trans_a=True, name=f"{tag}_dwkv")
    dhn = _matmul(dq, w["w_q"], trans_b=True, name=f"{tag}_dhn_q")
    dhn = _matmul(dkv, w["w_kv"], trans_b=True, res=dhn, name=f"{tag}_dhn_kv")
    dx, dnorm = _rmsnorm_bwd(x, w["norm"], dhn, dout, name=f"{tag}_dnorm")
    return dx, {"norm": dnorm, "w_q": d_w_q, "w_kv": d_w_kv, "q_norm": d_q_norm, "k_norm": d_k_norm, "w_o": d_w_o}


def _split_heads_cols(wm, heads, first):
    rws = wm.shape[0]
    w3 = wm.reshape(rws, heads, -1)
    return jnp.concatenate([w3[:, :, :first].reshape(rws, -1), w3[:, :, first:].reshape(rws, -1)], axis=-1)


def _merge_heads_cols(wm, heads, first):
    rws, cols = wm.shape
    a = wm[:, :heads * first].reshape(rws, heads, first)
    b = wm[:, heads * first:].reshape(rws, heads, cols // heads - first)
    return jnp.concatenate([a, b], axis=-1).reshape(rws, cols)


def _layer_weights(full, gains, layer):
    i = layer // 2

    def stacked(name, idx):
        for group, arrays in zip((_FIRST, _REST), full):
            lo, cnt = group.get(name, (0, 0))
            if lo <= idx < lo + cnt:
                return arrays[name][idx - lo]
        raise KeyError((name, idx))

    ffn = {"norm": gains["ffn_norm"][layer], "w_in_t": stacked("ffn_w_in", layer), "w_out": stacked("ffn_w_out", layer)}
    if layer % 2 == 0:
        w_in = stacked("w_in_ab", i)
        mix = {"norm": gains["mix_norm_ab"][i],
               "w_a": jnp.pad(w_in[:, :IN_A], ((0, 0), (0, IN_A_PAD - IN_A))), "w_b": w_in[:, IN_A:],
               "q_norm": gains["mla_q_norm"][i], "kv_norm": gains["mla_kv_norm"][i],
               "w_uq": _split_heads_cols(stacked("mla_w_uq", i), MLA_HEADS, MLA_NOPE),
               "w_ukv": _split_heads_cols(stacked("mla_w_ukv", i), MLA_HEADS, MLA_NOPE),
               "w_out": stacked("w_out_ab", i)}
    else:
        mix = {"norm": gains["mix_norm_c"][i], "w_q": stacked("gqa_w_q", i), "w_kv": stacked("gqa_w_kv", i),
               "q_norm": gains["gqa_q_norm"][i], "k_norm": gains["gqa_k_norm"][i], "w_o": stacked("gqa_w_o", i)}
    return mix, ffn


def _pack_grads(grads, group):
    parts = jnp.concatenate([_split_for_devices(jnp.stack(grads[n][group[n][0]:group[n][0] + group[n][1]]), axis)
                             for n, axis in _BIG if n in group], axis=1)
    return jnp.pad(parts, ((0, 0), (0, -parts.shape[1] % PACK_ROWS), (0, 0)))


def kernel(x, mix_norm_ab, w_in_ab, mla_q_norm, mla_kv_norm, mla_w_uq, mla_w_ukv, w_out_ab, mix_norm_c, gqa_w_q, gqa_w_kv, gqa_q_norm, gqa_k_norm, gqa_w_o, ffn_norm, ffn_w_in, ffn_w_out, final_norm, loss_target, m_mix_norm_ab, m_w_in_ab, m_mla_q_norm, m_mla_kv_norm, m_mla_w_uq, m_mla_w_ukv, m_w_out_ab, m_mix_norm_c, m_gqa_w_q, m_gqa_w_kv, m_gqa_q_norm, m_gqa_k_norm, m_gqa_w_o, m_ffn_norm, m_ffn_w_in, m_ffn_w_out, m_final_norm, v_mix_norm_ab, v_w_in_ab, v_mla_q_norm, v_mla_kv_norm, v_mla_w_uq, v_mla_w_ukv, v_w_out_ab, v_mix_norm_c, v_gqa_w_q, v_gqa_w_kv, v_gqa_q_norm, v_gqa_k_norm, v_gqa_w_o, v_ffn_norm, v_ffn_w_in, v_ffn_w_out, v_final_norm):
    wts = dict(mix_norm_ab=mix_norm_ab, w_in_ab=w_in_ab, mla_q_norm=mla_q_norm, mla_kv_norm=mla_kv_norm,
               mla_w_uq=mla_w_uq, mla_w_ukv=mla_w_ukv, w_out_ab=w_out_ab, mix_norm_c=mix_norm_c, gqa_w_q=gqa_w_q,
               gqa_w_kv=gqa_w_kv, gqa_q_norm=gqa_q_norm, gqa_k_norm=gqa_k_norm, gqa_w_o=gqa_w_o, ffn_norm=ffn_norm,
               ffn_w_in=ffn_w_in, ffn_w_out=ffn_w_out, final_norm=final_norm)
    mom = dict(mix_norm_ab=m_mix_norm_ab, w_in_ab=m_w_in_ab, mla_q_norm=m_mla_q_norm, mla_kv_norm=m_mla_kv_norm,
               mla_w_uq=m_mla_w_uq, mla_w_ukv=m_mla_w_ukv, w_out_ab=m_w_out_ab, mix_norm_c=m_mix_norm_c,
               gqa_w_q=m_gqa_w_q, gqa_w_kv=m_gqa_w_kv, gqa_q_norm=m_gqa_q_norm, gqa_k_norm=m_gqa_k_norm,
               gqa_w_o=m_gqa_w_o, ffn_norm=m_ffn_norm, ffn_w_in=m_ffn_w_in, ffn_w_out=m_ffn_w_out,
               final_norm=m_final_norm)
    var = dict(mix_norm_ab=v_mix_norm_ab, w_in_ab=v_w_in_ab, mla_q_norm=v_mla_q_norm, mla_kv_norm=v_mla_kv_norm,
               mla_w_uq=v_mla_w_uq, mla_w_ukv=v_mla_w_ukv, w_out_ab=v_w_out_ab, mix_norm_c=v_mix_norm_c,
               gqa_w_q=v_gqa_w_q, gqa_w_kv=v_gqa_w_kv, gqa_q_norm=v_gqa_q_norm, gqa_k_norm=v_gqa_k_norm,
               gqa_w_o=v_gqa_w_o, ffn_norm=v_ffn_norm, ffn_w_in=v_ffn_w_in, ffn_w_out=v_ffn_w_out,
               final_norm=v_final_norm)
    big_names = [n for n, _ in _BIG]
    groups = (_FIRST, _REST)
    big_w, big_m, big_v = (_stored({n: src[n] for n in big_names}) for src in (wts, mom, var))
    w_grp = [_group_of(big_w, grp) for grp in groups]
    shapes = [{n: a.shape for n, a in w.items()} for w in w_grp]
    w_packs = [_pack_big(w) for w in w_grp]
    n_rows = [p.shape[0] for p in w_packs]
    xs = x[0]
    s = xs.shape[0]
    me = 4 * lax.axis_index("x") + 2 * lax.axis_index("y") + lax.axis_index("c")
    c_cols = mix_norm_c.shape[1]

    tail = jnp.pad(mix_norm_c.reshape(-1), (0, GATHER_TAIL_ROWS // 2 * PACK_COLS - mix_norm_c.size))
    tail = _to_bits16(tail.reshape(GATHER_TAIL_ROWS // 2, PACK_COLS))
    gathered = _all_gather(jnp.concatenate([w_packs[0].astype(jnp.bfloat16), tail], axis=0))
    full = [_unpack_gathered(gathered[:, :n_rows[0]], shapes[0]), None]
    c_all = _from_bits16(gathered[:, n_rows[0]:]).reshape(N_DEV, -1)[:, :mix_norm_c.size]
    c_full = c_all.reshape(N_DEV, 2, c_cols).transpose(1, 0, 2).reshape(2, N_DEV * c_cols)
    gains = dict(mix_norm_ab=mix_norm_ab, mla_q_norm=mla_q_norm, mla_kv_norm=mla_kv_norm, mix_norm_c=c_full,
                 gqa_q_norm=gqa_q_norm, gqa_k_norm=gqa_k_norm, ffn_norm=ffn_norm)

    tabs = _rope_tables(s)
    slopes = jnp.exp2(-8.0 * jnp.arange(1, DIL_HEADS + 1, dtype=_F32) / DIL_HEADS)

    h = xs
    saved = []
    for layer in range(DEPTH):
        mix_w, ffn_w = _layer_weights(full, gains, layer)
        if layer == 0:
            h, sv_mix, rest = _even_fwd(h, mix_w, tabs, slopes, f"l{layer}_mix", gather=w_packs[1].astype(jnp.bfloat16))
            full[1] = _unpack_gathered(rest, shapes[1])
        elif layer % 2 == 0:
            h, sv_mix, _ = _even_fwd(h, mix_w, tabs, slopes, f"l{layer}_mix")
        else:
            h, sv_mix = _odd_fwd(h, mix_w, tabs, f"l{layer}_mix")
        h, sv_ffn = _ffn_fwd(h, ffn_w, f"l{layer}_ffn")
        saved.append((mix_w, ffn_w, sv_mix, sv_ffn))
    loss_local, dh, d_final = _final_loss(h, final_norm, loss_target[0], name="final_loss")

    gfull = {n: [None] * wts[n].shape[0] for n in big_names}
    gsmall = {n: [None] * (wts[n].shape[0] if wts[n].ndim > 1 else 1) for n in _SMALL}
    gsmall["final_norm"][0] = d_final
    for layer in reversed(range(DEPTH)):
        mix_w, ffn_w, sv_mix, sv_ffn = saved[layer]
        i = layer // 2
        dh, gf = _ffn_bwd(dh, sv_ffn, ffn_w, f"l{layer}_ffn")
        gsmall["ffn_norm"][layer] = gf["norm"]
        gfull["ffn_w_in"][layer] = gf["w_in_t"]
        gfull["ffn_w_out"][layer] = gf["w_out"]
        if layer % 2 == 0:
            sending = _pack_grads(gfull, _REST).astype(jnp.bfloat16) if layer == 0 else None
            dh, gm, got = _even_bwd(dh, sv_mix, mix_w, tabs, slopes, f"l{layer}_mix", exchange=sending)
            if layer == 0:
                received_rest = got
            gsmall["mix_norm_ab"][i] = gm["norm"]
            gsmall["mla_q_norm"][i] = gm["q_norm"]
            gsmall["mla_kv_norm"][i] = gm["kv_norm"]
            gfull["w_in_ab"][i] = jnp.concatenate([gm["w_a"][:, :IN_A], gm["w_b"]], axis=-1)
            gfull["mla_w_uq"][i] = _merge_heads_cols(gm["w_uq"], MLA_HEADS, MLA_NOPE)
            gfull["mla_w_ukv"][i] = _merge_heads_cols(gm["w_ukv"], MLA_HEADS, MLA_NOPE)
            gfull["w_out_ab"][i] = gm["w_out"]
        else:
            dh, gm = _odd_bwd(dh, sv_mix, mix_w, tabs, f"l{layer}_mix")
            gsmall["mix_norm_c"][i] = gm["norm"]
            gsmall["gqa_q_norm"][i] = gm["q_norm"]
            gsmall["gqa_k_norm"][i] = gm["k_norm"]
            gfull["gqa_w_q"][i] = gm["w_q"]
            gfull["gqa_w_kv"][i] = gm["w_kv"]
            gfull["gqa_w_o"][i] = gm["w_o"]
    grad_x = dh[None]

    small_part = _pack_small({n: jnp.stack(gsmall[n]) for n in _SMALL})
    small_bits = jnp.broadcast_to(_to_bits16(small_part)[None], (N_DEV, EXCHANGE_TAIL_ROWS, PACK_COLS))
    received_first = _exchange(jnp.concatenate([_pack_grads(gfull, _FIRST).astype(jnp.bfloat16), small_bits], axis=1))
    big_packs = [_reduce_adamw(got, w_packs[gi], _pack_big(_group_of(big_m, grp)), _pack_big(_group_of(big_v, grp)),
                               name=f"adamw_big{gi}")
                 for gi, (grp, got) in enumerate(zip(groups, (received_first, received_rest)))]

    def widen_c(shard):
        return lax.dynamic_update_slice(jnp.zeros((2, N_DEV * c_cols), _F32), shard, (0, me * c_cols))

    def small_of(src):
        return _pack_small({n: (widen_c(src[n]) if n == "mix_norm_c" else src[n]) for n in _SMALL})

    small_recv = _from_bits16(received_first[:, n_rows[0]:])
    small_packs = _reduce_adamw(small_recv, small_of(wts), small_of(mom), small_of(var), name="adamw_small")

    def outputs_of(which):
        by_group = [_unpack_big(big_packs[gi][which], shapes[gi]) for gi in range(len(groups))]
        small = _unpack_small(small_packs[which])
        res = _stored({n: jnp.concatenate([grp[n] for grp in by_group if n in grp], axis=0) for n in big_names})
        for n in wts:
            if n in big_names:
                continue
            if n == "mix_norm_c":
                res[n] = lax.dynamic_slice(small[n].reshape(2, N_DEV * c_cols), (0, me * c_cols), (2, c_cols))
            else:
                res[n] = small[n].reshape(wts[n].shape)
        return [res[n] for n in wts]

    loss = lax.psum(loss_local, _AXES)
    return (loss, grad_x, *outputs_of(0), *outputs_of(1), *outputs_of(2), *outputs_of(3))
```

```python
import functools

import jax
import jax.numpy as jnp
from jax import lax
from jax.experimental import pallas as pl
from jax.experimental.pallas import tpu as pltpu

D_MODEL = 1024
DEPTH = 4
GRID_W = 64
NORM_EPS = 1e-6
ROPE_THETA = 10000.0
NEG_INF = -1e30
MLA_HEADS = 8
MLA_Q_RANK = 384
MLA_KV_RANK = 256
MLA_NOPE = 64
MLA_ROPE = 32
MLA_V = 64
DIL_PAIRS = ((128, 1), (512, 4), (2048, 16))
DIL_HALF = 64
DIL_SLOTS = 4
DIL_GROUPS = 3
DIL_HEADS = 12
DIL_HEAD_DIM = 64
GQA_HEADS = 16
GQA_KV_HEADS = 4
GQA_HEAD_DIM = 64
FFN_HIDDEN = 2816
IN_A = MLA_Q_RANK + MLA_KV_RANK + MLA_ROPE
IN_A_PAD = 768
IN_B = 3 * DIL_HEADS * DIL_HEAD_DIM
ADAM_LR = 0.001
ADAM_B1 = 0.9
ADAM_B2 = 0.999
ADAM_EPS = 1e-08
ADAM_WD = 0.01
ADAM_STEP = 10

LANES = 128
SUBLANES_16BIT = 16
VMEM_LIMIT_BYTES = 56 * 1024 * 1024

MM_TILE = 1408

N_DEV = 8
PACK_COLS = 1024

_MXU = jnp.bfloat16
_ACT = jnp.bfloat16
_F32 = jnp.float32

_AXES = ("x", "y", "c")


def _params(sem):
    return pltpu.CompilerParams(dimension_semantics=sem, vmem_limit_bytes=VMEM_LIMIT_BYTES)


def _pick(n, cap):
    for t in range(cap - cap % LANES, 0, -LANES):
        if n % t == 0:
            return t
    return n


def _rows(m, target):
    t = m
    while t > target and t % 2 == 0:
        t //= 2
    return t


def _matmul(a, b, *, trans_a=False, trans_b=False, res=None, scale=None, out_dtype=_F32, name):
    if trans_a:
        k, m = a.shape
    else:
        m, k = a.shape
    if trans_b:
        n, kb = b.shape
    else:
        kb, n = b.shape
    assert k == kb, (a.shape, b.shape)
    tm, tn, tk = _pick(m, MM_TILE), _pick(n, MM_TILE), _pick(k, MM_TILE)
    nk = k // tk
    dims = (((0 if trans_a else 1,), (1 if trans_b else 0,)), ((), ()))

    def body(*refs):
        if res is None:
            a_ref, b_ref, o_ref, acc = refs
            r_ref = None
        else:
            a_ref, b_ref, r_ref, o_ref, acc = refs
        kk = pl.program_id(2)

        @pl.when(kk == 0)
        def _():
            acc[...] = jnp.zeros_like(acc)

        acc[...] += lax.dot_general(a_ref[...].astype(_MXU), b_ref[...].astype(_MXU), dims,
                                    preferred_element_type=_F32)

        @pl.when(kk == nk - 1)
        def _():
            r = acc[...]
            if scale is not None:
                r = r * scale
            if r_ref is not None:
                r = r + r_ref[...].astype(_F32)
            o_ref[...] = r.astype(out_dtype)

    a_spec = (pl.BlockSpec((tk, tm), lambda i, j, kk: (kk, i)) if trans_a
              else pl.BlockSpec((tm, tk), lambda i, j, kk: (i, kk)))
    b_spec = (pl.BlockSpec((tn, tk), lambda i, j, kk: (j, kk)) if trans_b
              else pl.BlockSpec((tk, tn), lambda i, j, kk: (kk, j)))
    o_spec = pl.BlockSpec((tm, tn), lambda i, j, kk: (i, j))
    in_specs = [a_spec, b_spec] + ([o_spec] if res is not None else [])
    args = (a, b) + ((res,) if res is not None else ())
    return pl.pallas_call(
        body, name=name, grid=(m // tm, n // tn, nk),
        in_specs=in_specs, out_specs=o_spec,
        out_shape=jax.ShapeDtypeStruct((m, n), out_dtype),
        scratch_shapes=[pltpu.VMEM((tm, tn), _F32)],
        compiler_params=_params(("parallel", "parallel", "arbitrary")),
    )(*args)


def _rmsnorm(x, g, *, out_dtype, name, rows=512):
    m, d = x.shape
    tm = _rows(m, rows)

    def body(x_ref, g_ref, o_ref):
        xf = x_ref[...].astype(_F32)
        r = lax.rsqrt(jnp.mean(xf * xf, axis=-1, keepdims=True) + NORM_EPS)
        o_ref[...] = ((xf * r) * g_ref[...]).astype(out_dtype)

    return pl.pallas_call(
        body, name=name, grid=(m // tm,),
        in_specs=[pl.BlockSpec((tm, d), lambda i: (i, 0)), pl.BlockSpec((1, d), lambda i: (0, 0))],
        out_specs=pl.BlockSpec((tm, d), lambda i: (i, 0)),
        out_shape=jax.ShapeDtypeStruct((m, d), out_dtype),
        compiler_params=_params(("parallel",)),
    )(x, g.reshape(1, d).astype(_F32))


def _rmsnorm_bwd(x, g, dy, dres=None, *, out_dtype=_F32, name, rows=512):
    m, d = x.shape
    tm = _rows(m, rows)

    def body(*refs):
        if dres is None:
            x_ref, g_ref, dy_ref, dx_ref, dg_ref = refs
            r_ref = None
        else:
            x_ref, g_ref, dy_ref, r_ref, dx_ref, dg_ref = refs

        @pl.when(pl.program_id(0) == 0)
        def _():
            dg_ref[...] = jnp.zeros_like(dg_ref)

        xf = x_ref[...].astype(_F32)
        r = lax.rsqrt(jnp.mean(xf * xf, axis=-1, keepdims=True) + NORM_EPS)
        xh = xf * r
        dyf = dy_ref[...].astype(_F32)
        dg_ref[...] += jnp.sum(dyf * xh, axis=0, keepdims=True)
        gdy = dyf * g_ref[...]
        dx = r * (gdy - xh * jnp.mean(gdy * xh, axis=-1, keepdims=True))
        if r_ref is not None:
            dx = dx + r_ref[...].astype(_F32)
        dx_ref[...] = dx.astype(out_dtype)

    row = pl.BlockSpec((tm, d), lambda i: (i, 0))
    vec = pl.BlockSpec((1, d), lambda i: (0, 0))
    in_specs = [row, vec, row] + ([row] if dres is not None else [])
    args = (x, g.reshape(1, d).astype(_F32), dy) + ((dres,) if dres is not None else ())
    dx, dg = pl.pallas_call(
        body, name=name, grid=(m // tm,),
        in_specs=in_specs, out_specs=[row, vec],
        out_shape=[jax.ShapeDtypeStruct((m, d), out_dtype), jax.ShapeDtypeStruct((1, d), _F32)],
        compiler_params=_params(("arbitrary",)),
    )(*args)
    return dx, dg.reshape(d)


def _rotate(xf, c, sn):
    w = xf.shape[1]
    if w > LANES:
        c, sn = jnp.tile(c, (1, w // LANES)), jnp.tile(sn, (1, w // LANES))
    lane = lax.broadcasted_iota(jnp.int32, xf.shape, 1)
    sw = jnp.where((lane & 31) < 16, pltpu.roll(xf, w - 16, 1), pltpu.roll(xf, 16, 1))
    return xf * c + sw * sn


def _seg_mean(v, seg_ref):
    outs = []
    for c in range(v.shape[1] // LANES):
        piece = v[:, c * LANES:(c + 1) * LANES]
        hi = piece.astype(jnp.bfloat16)
        lo = (piece - hi.astype(_F32)).astype(jnp.bfloat16)
        outs.append(jnp.dot(hi, seg_ref[...], preferred_element_type=_F32)
                    + jnp.dot(lo, seg_ref[...], preferred_element_type=_F32))
    return jnp.concatenate(outs, axis=1) if len(outs) > 1 else outs[0]


def _seg_matrix():
    lane = jnp.arange(LANES) // GQA_HEAD_DIM
    return ((lane[:, None] == lane[None, :]).astype(_F32) / GQA_HEAD_DIM).astype(jnp.bfloat16)


def _headnorm_rope(x, gain, cos_t, sin_t, *, scale=None, name):
    s, w = x.shape
    ts = _rows(s, 512)

    def body(x_ref, g_ref, seg_ref, c_ref, s_ref, o_ref):
        xf = x_ref[...]
        r = lax.rsqrt(_seg_mean(xf * xf, seg_ref) + NORM_EPS)
        y = _rotate((xf * r) * g_ref[...], c_ref[...], s_ref[...])
        if scale is not None:
            y = y * scale
        o_ref[...] = y.astype(o_ref.dtype)

    row = pl.BlockSpec((ts, w), lambda i: (i, 0))
    tab = pl.BlockSpec((ts, LANES), lambda i: (i, 0))
    return pl.pallas_call(
        body, name=name, grid=(s // ts,),
        in_specs=[row, pl.BlockSpec((1, w), lambda i: (0, 0)), pl.BlockSpec((LANES, LANES), lambda i: (0, 0)), tab, tab],
        out_specs=row, out_shape=jax.ShapeDtypeStruct((s, w), _ACT),
        compiler_params=_params(("parallel",)),
    )(x, jnp.tile(gain.astype(_F32), w // GQA_HEAD_DIM).reshape(1, w), _seg_matrix(), cos_t, sin_t)


def _headnorm_rope_bwd(x, gain, dy, cos_t, sin_t, *, name):
    s, w = x.shape
    ts = _rows(s, 512)

    def body(x_ref, g_ref, seg_ref, c_ref, s_ref, dy_ref, dx_ref, dg_ref):
        @pl.when(pl.program_id(0) == 0)
        def _():
            dg_ref[...] = jnp.zeros_like(dg_ref)

        xf = x_ref[...]
        r = lax.rsqrt(_seg_mean(xf * xf, seg_ref) + NORM_EPS)
        xh = xf * r
        dyn = _rotate(dy_ref[...].astype(_F32), c_ref[...], -s_ref[...])
        dg_ref[...] += jnp.sum(dyn * xh, axis=0, keepdims=True)
        gdy = dyn * g_ref[...]
        dx_ref[...] = (r * (gdy - xh * _seg_mean(gdy * xh, seg_ref))).astype(dx_ref.dtype)

    row = pl.BlockSpec((ts, w), lambda i: (i, 0))
    vec = pl.BlockSpec((1, w), lambda i: (0, 0))
    tab = pl.BlockSpec((ts, LANES), lambda i: (i, 0))
    dx, dg = pl.pallas_call(
        body, name=name, grid=(s // ts,),
        in_specs=[row, vec, pl.BlockSpec((LANES, LANES), lambda i: (0, 0)), tab, tab, row],
        out_specs=[row, vec],
        out_shape=[jax.ShapeDtypeStruct((s, w), _ACT), jax.ShapeDtypeStruct((1, w), _F32)],
        compiler_params=_params(("arbitrary",)),
    )(x, jnp.tile(gain.astype(_F32), w // GQA_HEAD_DIM).reshape(1, w), _seg_matrix(), cos_t, sin_t, dy)
    return dx, dg.reshape(w // GQA_HEAD_DIM, GQA_HEAD_DIM).sum(axis=0)


def _rope(x, cos_t, sin_t, *, out_dtype, name, sum_chunks=False, scale=None):
    s, w = x.shape
    assert w % LANES == 0
    ts = _rows(s, 512)
    ow = LANES if sum_chunks else w

    def body(x_ref, c_ref, s_ref, o_ref):
        y = _rotate(x_ref[...].astype(_F32), c_ref[...], s_ref[...])
        if scale is not None:
            y = y * scale
        if sum_chunks:
            shift = w // 2
            while shift >= 32:
                y = y + pltpu.roll(y, shift, 1)
                shift //= 2
            y = y[:, :LANES]
        o_ref[...] = y.astype(out_dtype)

    return pl.pallas_call(
        body, name=name, grid=(s // ts,),
        in_specs=[pl.BlockSpec((ts, w), lambda i: (i, 0)), pl.BlockSpec((ts, LANES), lambda i: (i, 0)),
                  pl.BlockSpec((ts, LANES), lambda i: (i, 0))],
        out_specs=pl.BlockSpec((ts, ow), lambda i: (i, 0)),
        out_shape=jax.ShapeDtypeStruct((s, ow), out_dtype),
        compiler_params=_params(("parallel",)),
    )(x, cos_t, sin_t)


_NT = (((1,), (1,)), ((), ()))
_TN = (((0,), (0,)), ((), ()))
LOG2_E = 1.4426950408889634
LN_2 = 0.6931471805599453
ATTN_FWD_ROWS = 2048
ATTN_ROWS = 512
ATTN_CHAINS = 2
ATTN_FWD_KEYS = 1024
ATTN_BWD_KEYS = 1024


def _grid_step(dims):
    step, total = 0, 1
    for axis, n in enumerate(dims):
        step = step * n + pl.program_id(axis)
        total *= n
    return step, total


def _attn_fwd(qt, k, vt, *, out_dtype, name, gather=None):
    h, dk, s = qt.shape
    g, nk, dv, tk = vt.shape
    assert nk * tk == s
    r = h // g
    rc = min(ATTN_FWD_ROWS, s)
    hp = min(r, ATTN_CHAINS)
    nrc = max(1, min(ATTN_CHAINS // hp, s // rc))
    tq = rc * nrc
    nhp, nq = r // hp, s // tq
    units = [(a, c) for a in range(hp) for c in range(nrc)]

    def body(q_ref, k_ref, v_ref, *rest):
        if gather is None:
            o_ref, lse_ref = rest
        else:
            x_ref, o_ref, lse_ref, all_ref, send_sems, recv_sems, local_sem = rest
            start, forward, finish = _gather_phases(x_ref, all_ref, send_sems, recv_sems, local_sem)
            step, total = _grid_step((g, nhp, nq))
            pl.when(step == 0)(start)
            pl.when(step == total // 2)(forward)

        qs = [q_ref[a, :, c * rc:(c + 1) * rc] for a, c in units]
        init = tuple((jnp.full((1, rc), NEG_INF, _F32), jnp.zeros((1, rc), _F32), jnp.zeros((dv, rc), _F32))
                     for _ in units)

        def trip(j, carry):
            kb, vb = k_ref[0, pl.ds(pl.multiple_of(j * tk, tk), tk), :], v_ref[0, j]
            out = []
            for u in range(len(units)):
                m, l, acc = carry[u]
                sc = jnp.dot(kb, qs[u], preferred_element_type=_F32)
                m_new = jnp.maximum(m, jnp.max(sc, axis=0, keepdims=True))
                p = jnp.exp2(sc - m_new)
                alpha = jnp.exp2(m - m_new)
                l = alpha * l + jnp.sum(p, axis=0, keepdims=True)
                acc = alpha * acc + jnp.dot(vb, p.astype(_MXU), preferred_element_type=_F32)
                out.append((m_new, l, acc))
            return tuple(out)

        fin = lax.fori_loop(0, nk, trip, init)
        for u, (a, c) in enumerate(units):
            m, l, acc = fin[u]
            o_ref[a, :, c * rc:(c + 1) * rc] = (acc / l).astype(out_dtype)
            lse_ref[a, :, c * rc:(c + 1) * rc] = m + jnp.log2(l)

        if gather is not None:
            pl.when(step == total - 1)(finish)

    q_blk = lambda gg, hh, i: (gg * nhp + hh, 0, i)
    in_specs = [pl.BlockSpec((hp, dk, tq), q_blk), pl.BlockSpec((1, s, dk), lambda gg, hh, i: (gg, 0, 0)),
                pl.BlockSpec((1, nk, dv, tk), lambda gg, hh, i: (gg, 0, 0, 0))]
    out_specs = [pl.BlockSpec((hp, dv, tq), q_blk), pl.BlockSpec((hp, 1, tq), q_blk)]
    out_shape = [jax.ShapeDtypeStruct((h, dv, s), out_dtype), jax.ShapeDtypeStruct((h, 1, s), _F32)]
    if gather is None:
        return pl.pallas_call(
            body, name=name, grid=(g, nhp, nq), in_specs=in_specs, out_specs=out_specs, out_shape=out_shape,
            compiler_params=_params(("parallel", "parallel", "parallel")),
        )(qt, k, vt)
    return pl.pallas_call(
        body, name=name, grid=(g, nhp, nq),
        in_specs=in_specs + [pl.BlockSpec(memory_space=pl.ANY)],
        out_specs=out_specs + [pl.BlockSpec(memory_space=pl.ANY)],
        out_shape=out_shape + [jax.ShapeDtypeStruct((N_DEV,) + gather.shape, gather.dtype)],
        scratch_shapes=_comm_scratch(),
        compiler_params=_params(("arbitrary", "arbitrary", "arbitrary")),
    )(qt, k, vt, gather)


def _attn_bwd(qt, k, v, ot, do, lse2, *, scale, name, exchange=None):
    h, dk, s = qt.shape
    g, _, dv = v.shape
    r = h // g
    hp = min(r, ATTN_CHAINS)
    tq = min(ATTN_ROWS, s)
    tk = min(ATTN_BWD_KEYS * (ATTN_CHAINS // hp), s)
    nhp, nq, nk = r // hp, s // tq, s // tk

    def body(qt_ref, k_ref, v_ref, ot_ref, do_ref, lse_ref, *rest):
        if exchange is None:
            dq_ref, dk_ref, dv_ref = rest
        else:
            g_ref, dq_ref, dk_ref, dv_ref, got_ref, send_sems, recv_sems, local_sem = rest
            start, finish = _exchange_phases(g_ref, got_ref, send_sems, recv_sems, local_sem)
            step, total = _grid_step((g, nhp, nq))
            pl.when(step == 0)(start)
        hh, i = pl.program_id(1), pl.program_id(2)

        @pl.when((hh == 0) & (i == 0))
        def _():
            dk_ref[...] = jnp.zeros_like(dk_ref)
            dv_ref[...] = jnp.zeros_like(dv_ref)

        qts = [qt_ref[a] for a in range(hp)]
        qs = [qt_ref[a].T for a in range(hp)]
        dos = [do_ref[a] for a in range(hp)]
        dots = [do_ref[a].T for a in range(hp)]
        ls = [lse_ref[a].T for a in range(hp)]
        dls = [jnp.sum(do_ref[a].astype(_F32) * ot_ref[a].T.astype(_F32), axis=1, keepdims=True)
               for a in range(hp)]

        def trip(j, carry):
            rows = pl.ds(pl.multiple_of(j * tk, tk), tk)
            kb, vb = k_ref[0, rows, :], v_ref[0, rows, :]
            out = []
            for a in range(hp):
                p = jnp.exp2(lax.dot_general(qs[a], kb, _NT, preferred_element_type=_F32) - ls[a])
                dp = lax.dot_general(dos[a], vb, _NT, preferred_element_type=_F32)
                ds = (p * (dp - dls[a])).astype(_MXU)
                dv_ref[0, j] += jnp.dot(dots[a], p.astype(_MXU), preferred_element_type=_F32)
                dk_ref[0, j] += jnp.dot(qts[a], ds, preferred_element_type=_F32)
                out.append(carry[a] + jnp.dot(ds, kb, preferred_element_type=_F32))
            return tuple(out)

        fin = lax.fori_loop(0, nk, trip, tuple(jnp.zeros((tq, dk), _F32) for _ in range(hp)))
        for a in range(hp):
            dq_ref[a] = (fin[a] * scale).astype(dq_ref.dtype)

        @pl.when((hh == nhp - 1) & (i == nq - 1))
        def _():
            dk_ref[...] = dk_ref[...] * LN_2

        if exchange is not None:
            pl.when(step == total - 1)(finish)

    q_blk = lambda gg, hh, i: (gg * nhp + hh, i, 0)
    qt_blk = lambda gg, hh, i: (gg * nhp + hh, 0, i)
    kv_blk = lambda gg, hh, i: (gg, 0, 0)
    in_specs = [pl.BlockSpec((hp, dk, tq), qt_blk), pl.BlockSpec((1, s, dk), kv_blk), pl.BlockSpec((1, s, dv), kv_blk),
                pl.BlockSpec((hp, dv, tq), qt_blk), pl.BlockSpec((hp, tq, dv), q_blk), pl.BlockSpec((hp, 1, tq), qt_blk)]
    out_specs = [pl.BlockSpec((hp, tq, dk), q_blk), pl.BlockSpec((1, nk, dk, tk), lambda gg, hh, i: (gg, 0, 0, 0)),
                 pl.BlockSpec((1, nk, dv, tk), lambda gg, hh, i: (gg, 0, 0, 0))]
    out_shape = [jax.ShapeDtypeStruct((h, s, dk), _ACT), jax.ShapeDtypeStruct((g, nk, dk, tk), _F32),
                 jax.ShapeDtypeStruct((g, nk, dv, tk), _F32)]
    args = (qt, k, v, ot, do, lse2)
    scratch = []
    if exchange is not None:
        in_specs, args = in_specs + [pl.BlockSpec(memory_space=pl.ANY)], args + (exchange,)
        out_specs = out_specs + [pl.BlockSpec(memory_space=pl.ANY)]
        out_shape = out_shape + [jax.ShapeDtypeStruct(exchange.shape, exchange.dtype)]
        scratch = _comm_scratch()
    return pl.pallas_call(
        body, name=name, grid=(g, nhp, nq), in_specs=in_specs, out_specs=out_specs, out_shape=out_shape,
        scratch_shapes=scratch, compiler_params=_params(("arbitrary", "arbitrary", "arbitrary")),
    )(*args)


def _unchunk(xt):
    g, nk, d, tk = xt.shape
    return xt.transpose(1, 3, 0, 2).reshape(nk * tk, g * d)


def _chunk_t(x2d, g):
    s = x2d.shape[0]
    tk = min(ATTN_FWD_KEYS, s)
    return x2d.reshape(s // tk, tk, g, -1).transpose(2, 0, 3, 1)


def _heads_t(x2d, h):
    s = x2d.shape[0]
    return x2d.reshape(s, h, -1).transpose(1, 2, 0)


def _unheads_t(xt):
    h, d, s = xt.shape
    return xt.transpose(2, 0, 1).reshape(s, h * d)


WIN_ROWS = 512


def _win_geometry(s, dil):
    t = min(WIN_ROWS, s)
    length = s // dil
    lg = length.bit_length() - 1
    assert 1 << lg == length and t % DIL_HALF == 0 and s % t == 0
    return t, t + 2 * DIL_HALF, lg, t // DIL_HALF, s // DIL_HALF


def _win_specs(hn, t, d, halo_per_blk, n_halo):
    return [pl.BlockSpec((hn, DIL_HALF, d), lambda i: (0, jnp.maximum(i * halo_per_blk - 1, 0), 0)),
            pl.BlockSpec((hn, t, d), lambda i: (0, i, 0)),
            pl.BlockSpec((hn, DIL_HALF, d), lambda i: (0, jnp.minimum((i + 1) * halo_per_blk, n_halo - 1), 0))]


def _win_mask(i, t, w, lg, dil, wide_rows):
    shape = (w, t) if wide_rows else (t, w)
    rows = lax.broadcasted_iota(jnp.int32, shape, 0)
    cols = lax.broadcasted_iota(jnp.int32, shape, 1)
    base = i * t
    if wide_rows:
        pq, pk = base - DIL_HALF + rows, base + cols
    else:
        pq, pk = base + rows, base - DIL_HALF + cols
    arel = jnp.abs(pk - pq)
    valid = (arel <= DIL_HALF) & ((pk >> lg) == (pq >> lg))
    return valid, (-dil * arel).astype(_F32)


def _win3(lo_ref, mid_ref, hi_ref, a):
    return jnp.concatenate([lo_ref[a], mid_ref[a], hi_ref[a]], axis=0)


def _win_fwd(q, k, v, slopes, *, dil, name):
    hn, s, d = q.shape
    t, w, lg, hpb, n_halo = _win_geometry(s, dil)
    scale = d ** -0.5

    def body(sl_ref, q_ref, klo, kmid, khi, vlo, vmid, vhi, o_ref, lse_ref):
        valid, nb = _win_mask(pl.program_id(0), t, w, lg, dil, False)
        for a in range(hn):
            kw, vw = _win3(klo, kmid, khi, a), _win3(vlo, vmid, vhi, a)
            sc = lax.dot_general(q_ref[a], kw, _NT, preferred_element_type=_F32) * scale
            sc = jnp.where(valid, sc + sl_ref[a] * nb, NEG_INF)
            m = jnp.max(sc, axis=1, keepdims=True)
            e = jnp.exp(sc - m)
            den = jnp.sum(e, axis=1, keepdims=True)
            o_ref[a] = jnp.dot(e.astype(_MXU), vw, preferred_element_type=_F32) / den
            lse_ref[a] = m + jnp.log(den)

    kv = _win_specs(hn, t, d, hpb, n_halo)
    blk = lambda c: pl.BlockSpec((hn, t, c), lambda i: (0, i, 0))
    return pl.pallas_call(
        body, name=name, grid=(s // t,),
        in_specs=[pl.BlockSpec(memory_space=pltpu.SMEM), blk(d)] + kv + kv,
        out_specs=[blk(d), blk(1)],
        out_shape=[jax.ShapeDtypeStruct((hn, s, d), _F32), jax.ShapeDtypeStruct((hn, s, 1), _F32)],
        compiler_params=_params(("parallel",)),
    )(slopes.astype(_F32), q, k, k, k, v, v, v)


def _win_bwd_dq(q, k, v, do, lse, delta, slopes, *, dil, name):
    hn, s, d = q.shape
    t, w, lg, hpb, n_halo = _win_geometry(s, dil)
    scale = d ** -0.5

    def body(sl_ref, q_ref, klo, kmid, khi, vlo, vmid, vhi, do_ref, lse_ref, dl_ref, dq_ref):
        valid, nb = _win_mask(pl.program_id(0), t, w, lg, dil, False)
        for a in range(hn):
            kw, vw = _win3(klo, kmid, khi, a), _win3(vlo, vmid, vhi, a)
            sc = lax.dot_general(q_ref[a], kw, _NT, preferred_element_type=_F32) * scale
            p = jnp.exp(jnp.where(valid, sc + sl_ref[a] * nb, NEG_INF) - lse_ref[a])
            dp = lax.dot_general(do_ref[a], vw, _NT, preferred_element_type=_F32)
            ds = (p * (dp - dl_ref[a])).astype(_MXU)
            dq_ref[a] = (jnp.dot(ds, kw, preferred_element_type=_F32) * scale).astype(dq_ref.dtype)

    kv = _win_specs(hn, t, d, hpb, n_halo)
    blk = lambda c: pl.BlockSpec((hn, t, c), lambda i: (0, i, 0))
    return pl.pallas_call(
        body, name=name, grid=(s // t,),
        in_specs=[pl.BlockSpec(memory_space=pltpu.SMEM), blk(d)] + kv + kv + [blk(d), blk(1), blk(1)],
        out_specs=blk(d),
        out_shape=jax.ShapeDtypeStruct((hn, s, d), _ACT),
        compiler_params=_params(("parallel",)),
    )(slopes.astype(_F32), q, k, k, k, v, v, v, do, lse, delta)


def _win_bwd_dkv(q, k, v, do, lse, delta, slopes, *, dil, name):
    hn, s, d = q.shape
    t, w, lg, hpb, n_halo = _win_geometry(s, dil)
    scale = d ** -0.5

    def body(sl_ref, qlo, qmid, qhi, dolo, domid, dohi, llo, lmid, lhi, dllo, dlmid, dlhi, k_ref, v_ref,
             dk_ref, dv_ref):
        valid, nb = _win_mask(pl.program_id(0), t, w, lg, dil, True)
        for a in range(hn):
            qw, dow = _win3(qlo, qmid, qhi, a), _win3(dolo, domid, dohi, a)
            lw, dlw = _win3(llo, lmid, lhi, a), _win3(dllo, dlmid, dlhi, a)
            sc = lax.dot_general(qw, k_ref[a], _NT, preferred_element_type=_F32) * scale
            p = jnp.exp(jnp.where(valid, sc + sl_ref[a] * nb, NEG_INF) - lw)
            dp = lax.dot_general(dow, v_ref[a], _NT, preferred_element_type=_F32)
            ds = (p * (dp - dlw)).astype(_MXU)
            dv_ref[a] = lax.dot_general(p.astype(_MXU), dow, _TN, preferred_element_type=_F32).astype(dv_ref.dtype)
            dk_ref[a] = (lax.dot_general(ds, qw, _TN, preferred_element_type=_F32) * scale).astype(dk_ref.dtype)

    blk = lambda c: pl.BlockSpec((hn, t, c), lambda i: (0, i, 0))
    return pl.pallas_call(
        body, name=name, grid=(s // t,),
        in_specs=([pl.BlockSpec(memory_space=pltpu.SMEM)] + _win_specs(hn, t, d, hpb, n_halo)
                  + _win_specs(hn, t, d, hpb, n_halo) + _win_specs(hn, t, 1, hpb, n_halo)
                  + _win_specs(hn, t, 1, hpb, n_halo) + [blk(d), blk(d)]),
        out_specs=[blk(d), blk(d)],
        out_shape=[jax.ShapeDtypeStruct((hn, s, d), _ACT), jax.ShapeDtypeStruct((hn, s, d), _ACT)],
        compiler_params=_params(("parallel",)),
    )(slopes.astype(_F32), q, q, q, do, do, do, lse, lse, lse, delta, delta, delta, k, v)


def _merge_weights(lse):
    mx = jnp.max(lse, axis=0, keepdims=True)
    e = jnp.exp(lse - mx)
    return e / jnp.sum(e, axis=0, keepdims=True)


def _merge_fwd(o3, lse3, *, name):
    ng, sl, s, d = o3.shape
    t = _rows(s, 512)

    def body(o_ref, l_ref, c_ref):
        wts = _merge_weights(l_ref[:, 0])
        c_ref[0] = jnp.sum(wts * o_ref[:, 0], axis=0).astype(c_ref.dtype)

    return pl.pallas_call(
        body, name=name, grid=(sl, s // t),
        in_specs=[pl.BlockSpec((ng, 1, t, d), lambda a, i: (0, a, i, 0)),
                  pl.BlockSpec((ng, 1, t, 1), lambda a, i: (0, a, i, 0))],
        out_specs=pl.BlockSpec((1, t, d), lambda a, i: (a, i, 0)),
        out_shape=jax.ShapeDtypeStruct((sl, s, d), _ACT),
        compiler_params=_params(("parallel", "parallel")),
    )(o3, lse3)


def _merge_bwd(dcomb, o3, lse3, *, name):
    ng, sl, s, d = o3.shape
    t = _rows(s, 512)

    def body(dc_ref, o_ref, l_ref, do_ref, dl_ref):
        wts = _merge_weights(l_ref[:, 0])
        dc = dc_ref[0].astype(_F32)
        comb = jnp.sum(wts * o_ref[:, 0], axis=0)
        do_ref[:, 0] = (wts * dc[None]).astype(do_ref.dtype)
        dl_ref[:, 0] = wts * jnp.sum(dc * comb, axis=-1, keepdims=True)[None]

    big = pl.BlockSpec((ng, 1, t, d), lambda a, i: (0, a, i, 0))
    small = pl.BlockSpec((ng, 1, t, 1), lambda a, i: (0, a, i, 0))
    return pl.pallas_call(
        body, name=name, grid=(sl, s // t),
        in_specs=[pl.BlockSpec((1, t, d), lambda a, i: (a, i, 0)), big, small],
        out_specs=[big, small],
        out_shape=[jax.ShapeDtypeStruct((ng, sl, s, d), _ACT), jax.ShapeDtypeStruct((ng, sl, s, 1), _F32)],
        compiler_params=_params(("parallel", "parallel")),
    )(dcomb, o3, lse3)


SWIGLU_ROWS = 512


def _gate_up(a, w_in_t, *, name):
    m, k = a.shape
    n = w_in_t.shape[0] // 2
    tm, tn, tk = _pick(m, SWIGLU_ROWS), _pick(n, MM_TILE), _pick(k, MM_TILE)
    nk = k // tk

    def body(a_ref, g_ref, u_ref, go_ref, uo_ref, act_ref, acc_g, acc_u):
        kk = pl.program_id(2)

        @pl.when(kk == 0)
        def _():
            acc_g[...] = jnp.zeros_like(acc_g)
            acc_u[...] = jnp.zeros_like(acc_u)

        av = a_ref[...].astype(_MXU)
        acc_g[...] += lax.dot_general(av, g_ref[...].astype(_MXU), _NT, preferred_element_type=_F32)
        acc_u[...] += lax.dot_general(av, u_ref[...].astype(_MXU), _NT, preferred_element_type=_F32)

        @pl.when(kk == nk - 1)
        def _():
            gf, uf = acc_g[...], acc_u[...]
            go_ref[...] = gf.astype(go_ref.dtype)
            uo_ref[...] = uf.astype(uo_ref.dtype)
            act_ref[...] = (gf * jax.nn.sigmoid(gf) * uf).astype(act_ref.dtype)

    o_spec = pl.BlockSpec((tm, tn), lambda i, j, kk: (i, j))
    out = jax.ShapeDtypeStruct((m, n), _ACT)
    return pl.pallas_call(
        body, name=name, grid=(m // tm, n // tn, nk),
        in_specs=[pl.BlockSpec((tm, tk), lambda i, j, kk: (i, kk)),
                  pl.BlockSpec((tn, tk), lambda i, j, kk: (j, kk)),
                  pl.BlockSpec((tn, tk), lambda i, j, kk: (j + n // tn, kk))],
        out_specs=[o_spec, o_spec, o_spec], out_shape=[out, out, out],
        scratch_shapes=[pltpu.VMEM((tm, tn), _F32), pltpu.VMEM((tm, tn), _F32)],
        compiler_params=_params(("parallel", "parallel", "arbitrary")),
    )(a, w_in_t, w_in_t)


def _gate_up_bwd(dout, w_out, gate, up, *, name):
    m, k = dout.shape
    n = w_out.shape[0]
    tm, tn, tk = _pick(m, SWIGLU_ROWS), _pick(n, MM_TILE), _pick(k, MM_TILE)
    nk = k // tk

    def body(d_ref, w_ref, g_ref, u_ref, dg_ref, du_ref, acc):
        kk = pl.program_id(2)

        @pl.when(kk == 0)
        def _():
            acc[...] = jnp.zeros_like(acc)

        acc[...] += lax.dot_general(d_ref[...].astype(_MXU), w_ref[...].astype(_MXU), _NT,
                                    preferred_element_type=_F32)

        @pl.when(kk == nk - 1)
        def _():
            daf = acc[...]
            gf = g_ref[...].astype(_F32)
            sg = jax.nn.sigmoid(gf)
            dg_ref[...] = (daf * u_ref[...].astype(_F32) * (sg + gf * sg * (1.0 - sg))).astype(dg_ref.dtype)
            du_ref[...] = (daf * (gf * sg)).astype(du_ref.dtype)

    o_spec = pl.BlockSpec((tm, tn), lambda i, j, kk: (i, j))
    out = jax.ShapeDtypeStruct((m, n), _ACT)
    return pl.pallas_call(
        body, name=name, grid=(m // tm, n // tn, nk),
        in_specs=[pl.BlockSpec((tm, tk), lambda i, j, kk: (i, kk)), pl.BlockSpec((tn, tk), lambda i, j, kk: (j, kk)),
                  o_spec, o_spec],
        out_specs=[o_spec, o_spec], out_shape=[out, out],
        scratch_shapes=[pltpu.VMEM((tm, tn), _F32)],
        compiler_params=_params(("parallel", "parallel", "arbitrary")),
    )(dout, w_out, gate, up)


def _final_loss(x, g, target, *, name):
    m, d = x.shape
    tm = _rows(m, 512)

    def body(x_ref, g_ref, t_ref, loss_ref, dx_ref, dg_ref):
        @pl.when(pl.program_id(0) == 0)
        def _():
            loss_ref[...] = jnp.zeros_like(loss_ref)
            dg_ref[...] = jnp.zeros_like(dg_ref)

        xf = x_ref[...]
        r = lax.rsqrt(jnp.mean(xf * xf, axis=-1, keepdims=True) + NORM_EPS)
        xh = xf * r
        err = xh * g_ref[...] - t_ref[...]
        loss_ref[...] += 0.5 * jnp.sum(jnp.mean(err * err, axis=-1, keepdims=True))
        dy = err * (1.0 / d)
        dg_ref[...] += jnp.sum(dy * xh, axis=0, keepdims=True)
        gdy = dy * g_ref[...]
        dx_ref[...] = r * (gdy - xh * jnp.mean(gdy * xh, axis=-1, keepdims=True))

    row = pl.BlockSpec((tm, d), lambda i: (i, 0))
    vec = pl.BlockSpec((1, d), lambda i: (0, 0))
    loss, dx, dg = pl.pallas_call(
        body, name=name, grid=(m // tm,),
        in_specs=[row, vec, row],
        out_specs=[pl.BlockSpec((8, LANES), lambda i: (0, 0)), row, vec],
        out_shape=[jax.ShapeDtypeStruct((8, LANES), _F32), jax.ShapeDtypeStruct((m, d), _F32),
                   jax.ShapeDtypeStruct((1, d), _F32)],
        compiler_params=_params(("arbitrary",)),
    )(x, g.reshape(1, d), target)
    return loss[0, 0], dx, dg.reshape(d)


def _reduce_adamw(parts, w, m, v, *, name):
    rws, cols = w.shape
    tr = rws
    for cand in range(min(rws, 256), 0, -SUBLANES_16BIT):
        if cand % SUBLANES_16BIT == 0 and rws % cand == 0:
            tr = cand
            break

    def body(p_ref, w_ref, m_ref, v_ref, g_ref, d_ref, nm_ref, nv_ref):
        gsum = p_ref[0].astype(_F32)
        for dev in range(1, N_DEV):
            gsum = gsum + p_ref[dev].astype(_F32)
        m2 = ADAM_B1 * m_ref[...] + (1.0 - ADAM_B1) * gsum
        v2 = ADAM_B2 * v_ref[...] + (1.0 - ADAM_B2) * (gsum * gsum)
        m_hat = m2 / (1.0 - ADAM_B1 ** ADAM_STEP)
        v_hat = v2 / (1.0 - ADAM_B2 ** ADAM_STEP)
        g_ref[...] = gsum
        d_ref[...] = -ADAM_LR * (m_hat / (jnp.sqrt(v_hat) + ADAM_EPS) + ADAM_WD * w_ref[...])
        nm_ref[...] = m2
        nv_ref[...] = v2

    blk = pl.BlockSpec((tr, cols), lambda i: (i, 0))
    out = jax.ShapeDtypeStruct((rws, cols), _F32)
    return pl.pallas_call(
        body, name=name, grid=(rws // tr,),
        in_specs=[pl.BlockSpec((N_DEV, tr, cols), lambda i: (0, i, 0)), blk, blk, blk],
        out_specs=[blk, blk, blk, blk], out_shape=[out, out, out, out],
        compiler_params=_params(("parallel",)),
    )(parts, w, m, v)


def _mesh_pos():
    return lax.axis_index("x"), lax.axis_index("y"), lax.axis_index("c")


def _all_gather(block):
    rws, cols = block.shape

    def body(x_ref, out_ref, send_sems, recv_sems, local_sem):
        start, forward, finish = _gather_phases(x_ref, out_ref, send_sems, recv_sems, local_sem)
        start()
        forward()
        finish()

    return pl.pallas_call(
        body, name="weights_all_gather",
        out_shape=jax.ShapeDtypeStruct((N_DEV, rws, cols), block.dtype),
        in_specs=[pl.BlockSpec(memory_space=pl.ANY)],
        out_specs=pl.BlockSpec(memory_space=pl.ANY),
        scratch_shapes=_comm_scratch(),
    )(block)


def _comm_scratch():
    return [pltpu.SemaphoreType.DMA((N_DEV - 1,)), pltpu.SemaphoreType.DMA((N_DEV - 1,)), pltpu.SemaphoreType.DMA]


def _gather_phases(x_ref, out_ref, send_sems, recv_sems, local_sem):
    x, y, c = _mesh_pos()
    me, sibling = (x, y, c), (x, y, 1 - c)
    chips = [(1 - x, y), (x, 1 - y), (1 - x, 1 - y)]

    def slot(px, py, pc):
        return out_ref.at[4 * px + 2 * py + pc]

    def copy(k, blk, to, src=None):
        return pltpu.make_async_remote_copy(
            src_ref=slot(*blk) if src is None else src, dst_ref=slot(*blk),
            send_sem=send_sems.at[k], recv_sem=recv_sems.at[k],
            device_id=to, device_id_type=pl.DeviceIdType.MESH)

    mine = pltpu.make_async_copy(x_ref, slot(*me), local_sem)
    first = [copy(0, me, sibling, src=x_ref)]
    first += [copy(1 + j, me, (*chip, c), src=x_ref) for j, chip in enumerate(chips)]
    passed = [copy(4 + j, (*chip, c), sibling) for j, chip in enumerate(chips)]

    def start():
        mine.start()
        for cp in first:
            cp.start()

    def forward():
        for j, chip in enumerate(chips):
            copy(1 + j, (*chip, c), me).wait_recv()
            passed[j].start()

    def finish():
        copy(0, sibling, me).wait_recv()
        for j, chip in enumerate(chips):
            copy(4 + j, (*chip, 1 - c), me).wait_recv()
        for cp in first + passed:
            cp.wait_send()
        mine.wait()

    return start, forward, finish


def _exchange(parts):
    def body(g_ref, out_ref, send_sems, recv_sems, local_sem):
        start, finish = _exchange_phases(g_ref, out_ref, send_sems, recv_sems, local_sem)
        start()
        finish()

    return pl.pallas_call(
        body, name="grads_exchange",
        out_shape=jax.ShapeDtypeStruct(parts.shape, parts.dtype),
        in_specs=[pl.BlockSpec(memory_space=pl.ANY)],
        out_specs=pl.BlockSpec(memory_space=pl.ANY),
        scratch_shapes=_comm_scratch(),
    )(parts)


def _exchange_phases(g_ref, out_ref, send_sems, recv_sems, local_sem):
    x, y, c = _mesh_pos()
    me = 4 * x + 2 * y + c
    mine = pltpu.make_async_copy(g_ref.at[me], out_ref.at[me], local_sem)
    sends, arrivals = [], []
    for k in range(1, N_DEV):
        px = 1 - x if k & 4 else x
        py = 1 - y if k & 2 else y
        pc = 1 - c if k & 1 else c
        peer = 4 * px + 2 * py + pc
        sems = dict(send_sem=send_sems.at[k - 1], recv_sem=recv_sems.at[k - 1],
                    device_id=(px, py, pc), device_id_type=pl.DeviceIdType.MESH)
        sends.append(pltpu.make_async_remote_copy(src_ref=g_ref.at[peer], dst_ref=out_ref.at[me], **sems))
        arrivals.append(pltpu.make_async_remote_copy(src_ref=g_ref.at[peer], dst_ref=out_ref.at[peer], **sems))

    def start():
        mine.start()
        for cp in sends:
            cp.start()

    def finish():
        for cp in arrivals:
            cp.wait_recv()
        for cp in sends:
            cp.wait_send()
        mine.wait()

    return start, finish


_BIG = (("w_in_ab", 2), ("mla_w_uq", 1), ("mla_w_ukv", 1), ("w_out_ab", 2), ("gqa_w_q", 1), ("gqa_w_kv", 1),
        ("gqa_w_o", 1), ("ffn_w_in", 1), ("ffn_w_out", 1))
_TRANSPOSED = ("ffn_w_in",)


def _stored(arrays):
    return {n: (a.transpose(0, 2, 1) if n in _TRANSPOSED else a) for n, a in arrays.items()}
_SMALL = ("mix_norm_ab", "ffn_norm", "final_norm", "mix_norm_c", "mla_q_norm", "mla_kv_norm", "gqa_q_norm", "gqa_k_norm")
SMALL_ROWS = 16
GATHER_TAIL_ROWS = 16
EXCHANGE_TAIL_ROWS = 32


def _view3(a):
    return a.reshape(a.shape[0], a.shape[1], -1)


def _pad_rows(a2d):
    pad = -a2d.shape[0] % SUBLANES_16BIT
    return jnp.pad(a2d, ((0, pad), (0, 0))) if pad else a2d


def _pack_rows(shard):
    return _pad_rows(shard.reshape(-1, PACK_COLS))


def _packed_rows(shape):
    n = 1
    for d in shape:
        n *= d
    rows = n // PACK_COLS
    return rows + (-rows % SUBLANES_16BIT)


_FIRST = {"w_in_ab": (0, 1), "mla_w_uq": (0, 1), "mla_w_ukv": (0, 1), "w_out_ab": (0, 1), "ffn_w_in": (0, 1),
          "ffn_w_out": (0, 1)}
_REST = {"w_in_ab": (1, 1), "mla_w_uq": (1, 1), "mla_w_ukv": (1, 1), "w_out_ab": (1, 1), "gqa_w_q": (0, 2),
         "gqa_w_kv": (0, 2), "gqa_w_o": (0, 2), "ffn_w_in": (1, 3), "ffn_w_out": (1, 3)}
PACK_ROWS = 128


def _group_of(arrays, group):
    return {n: arrays[n][lo:lo + cnt] for n, (lo, cnt) in group.items()}


def _pack_big(shards):
    parts = [_pack_rows(shards[n]) for n, _ in _BIG if n in shards]
    fill = -sum(p.shape[0] for p in parts) % PACK_ROWS
    return jnp.concatenate(parts + [jnp.zeros((fill, PACK_COLS), parts[0].dtype)], axis=0)


def _unpack_big(packed, shapes):
    out, off = {}, 0
    for n, _ in _BIG:
        if n not in shapes:
            continue
        size = 1
        for d in shapes[n]:
            size *= d
        out[n] = packed[off:off + size // PACK_COLS].reshape(shapes[n])
        off += _packed_rows(shapes[n])
    return out


def _unpack_gathered(gathered, shapes):
    out, off = {}, 0
    for n, axis in _BIG:
        if n not in shapes:
            continue
        size = 1
        for d in shapes[n]:
            size *= d
        l3 = (shapes[n][0], shapes[n][1], size // (shapes[n][0] * shapes[n][1]))
        sh = gathered[:, off:off + size // PACK_COLS].reshape((N_DEV,) + l3)
        if axis == 1:
            full = sh.transpose(1, 0, 2, 3).reshape(l3[0], N_DEV * l3[1], l3[2])
        else:
            full = sh.transpose(1, 2, 0, 3).reshape(l3[0], l3[1], N_DEV * l3[2])
        out[n] = full
        off += _packed_rows(shapes[n])
    return out


def _split_for_devices(full, axis):
    l, rws, cols = full.shape
    if axis == 1:
        sh = full.reshape(l, N_DEV, rws // N_DEV, cols).transpose(1, 0, 2, 3)
    else:
        sh = full.reshape(l, rws, N_DEV, cols // N_DEV).transpose(2, 0, 1, 3)
    flat = sh.reshape(N_DEV, -1, PACK_COLS)
    pad = -flat.shape[1] % SUBLANES_16BIT
    return jnp.pad(flat, ((0, 0), (0, pad), (0, 0))) if pad else flat


def _to_bits16(a_f32_rows):
    r = a_f32_rows.shape[0]
    return lax.bitcast_convert_type(a_f32_rows, jnp.bfloat16).reshape(2 * r, PACK_COLS)


def _from_bits16(a_bf16_rows):
    lead, r = a_bf16_rows.shape[:-2], a_bf16_rows.shape[-2]
    return lax.bitcast_convert_type(a_bf16_rows.reshape(lead + (r // 2, PACK_COLS, 2)), _F32)


def _small_sizes():
    return {"mix_norm_ab": 2 * D_MODEL, "ffn_norm": DEPTH * D_MODEL, "final_norm": D_MODEL, "mix_norm_c": 2 * D_MODEL,
            "mla_q_norm": 2 * MLA_Q_RANK, "mla_kv_norm": 2 * MLA_KV_RANK, "gqa_q_norm": 2 * GQA_HEAD_DIM,
            "gqa_k_norm": 2 * GQA_HEAD_DIM}


def _pack_small(vals):
    flat = jnp.concatenate([vals[n].reshape(-1).astype(_F32) for n in _SMALL])
    return jnp.pad(flat, (0, SMALL_ROWS * PACK_COLS - flat.shape[0])).reshape(SMALL_ROWS, PACK_COLS)


def _unpack_small(pack):
    flat, out, off = pack.reshape(-1), {}, 0
    sizes = _small_sizes()
    for n in _SMALL:
        out[n] = flat[off:off + sizes[n]]
        off += sizes[n]
    return out


def _angles(pos, dim):
    freqs = ROPE_THETA ** (-jnp.arange(0, dim, 2, dtype=_F32) / dim)
    ang = pos.astype(_F32)[:, None] * freqs[None, :]
    return jnp.cos(ang), jnp.sin(ang)


def _rope_tables(s):
    pos = jnp.arange(s)
    cos_t, sin_t = _angles(pos, MLA_ROPE)
    mla_c = jnp.tile(jnp.concatenate([cos_t, cos_t], -1), (1, LANES // 32))
    mla_s = jnp.tile(jnp.concatenate([-sin_t, sin_t], -1), (1, LANES // 32))
    rows = s // GRID_W
    row_idx = jnp.broadcast_to(jnp.arange(rows)[:, None], (rows, GRID_W)).reshape(-1)
    col_idx = jnp.broadcast_to(jnp.arange(GRID_W)[None, :], (rows, GRID_W)).reshape(-1)
    cos_r, sin_r = _angles(row_idx, GQA_HEAD_DIM // 2)
    cos_c, sin_c = _angles(col_idx, GQA_HEAD_DIM // 2)
    gqa_c = jnp.tile(jnp.concatenate([cos_r, cos_r, cos_c, cos_c], -1), (1, LANES // GQA_HEAD_DIM))
    gqa_s = jnp.tile(jnp.concatenate([-sin_r, sin_r, -sin_c, sin_c], -1), (1, LANES // GQA_HEAD_DIM))
    return (mla_c, mla_s), (gqa_c, gqa_s)


def _heads(x2d, h):
    s = x2d.shape[0]
    return x2d.reshape(s, h, -1).transpose(1, 0, 2)


def _unheads(xh):
    h, s, d = xh.shape
    return xh.transpose(1, 0, 2).reshape(s, h * d)


def _to_res(x2d, dil):
    s, cols = x2d.shape
    if dil > 1:
        x2d = x2d.reshape(s // dil, dil, cols).transpose(1, 0, 2).reshape(s, cols)
    return x2d.reshape(s, cols // DIL_HEAD_DIM, DIL_HEAD_DIM).transpose(1, 0, 2)


def _from_res(xh, dil):
    heads, s, d = xh.shape
    x2d = xh.transpose(1, 0, 2).reshape(s, heads * d)
    if dil > 1:
        x2d = x2d.reshape(dil, s // dil, heads * d).transpose(1, 0, 2).reshape(s, heads * d)
    return x2d


def _group_major(w_b):
    rws = w_b.shape[0]
    return w_b.reshape(rws, 3, DIL_GROUPS, DIL_SLOTS * DIL_HEAD_DIM).transpose(0, 2, 1, 3).reshape(rws, -1)


def _res_to_tok(xh, dil):
    sl, s, c = xh.shape
    return xh.reshape(sl, dil, s // dil, c).transpose(0, 2, 1, 3).reshape(sl, s, c)


def _tok_to_res(xh, dil):
    sl, s, c = xh.shape
    return xh.reshape(sl, s // dil, dil, c).transpose(0, 2, 1, 3).reshape(sl, s, c)


def _ffn_fwd(x, w, tag):
    hn = _rmsnorm(x, w["norm"], out_dtype=_ACT, name=f"{tag}_norm")
    gate, up, act = _gate_up(hn, w["w_in_t"], name=f"{tag}_gate_up")
    out = _matmul(act, w["w_out"], res=x, name=f"{tag}_out")
    return out, (x, hn, gate, up, act)


def _ffn_bwd(dout, saved, w, tag):
    x, hn, gate, up, act = saved
    w_gate_t, w_up_t = w["w_in_t"][:FFN_HIDDEN], w["w_in_t"][FFN_HIDDEN:]
    d_w_out = _matmul(act, dout, trans_a=True, name=f"{tag}_dwout")
    dgate, dup = _gate_up_bwd(dout, w["w_out"], gate, up, name=f"{tag}_dgate_up")
    d_w_gate_t = _matmul(dgate, hn, trans_a=True, name=f"{tag}_dwgate")
    d_w_up_t = _matmul(dup, hn, trans_a=True, name=f"{tag}_dwup")
    dhn = _matmul(dgate, w_gate_t, name=f"{tag}_dhn_gate")
    dhn = _matmul(dup, w_up_t, res=dhn, name=f"{tag}_dhn_up")
    dx, dnorm = _rmsnorm_bwd(x, w["norm"], dhn, dout, name=f"{tag}_dnorm")
    return dx, {"norm": dnorm, "w_in_t": jnp.concatenate([d_w_gate_t, d_w_up_t], axis=0), "w_out": d_w_out}


def _even_fwd(x, w, tabs, slopes, tag, gather=None):
    s = x.shape[0]
    (mla_c, mla_s), _ = tabs
    hn = _rmsnorm(x, w["norm"], out_dtype=_ACT, name=f"{tag}_norm")
    za = _matmul(hn, w["w_a"], name=f"{tag}_in_a")
    zb = _matmul(hn, w["w_b"], out_dtype=_ACT, name=f"{tag}_in_b")
    cq, ckv, kr = za[:, :MLA_Q_RANK], za[:, MLA_Q_RANK:MLA_Q_RANK + MLA_KV_RANK], za[:, MLA_Q_RANK + MLA_KV_RANK:]
    cqn = _rmsnorm(cq, w["q_norm"], out_dtype=_ACT, name=f"{tag}_qnorm")
    ckvn = _rmsnorm(ckv, w["kv_norm"], out_dtype=_ACT, name=f"{tag}_kvnorm")
    q = _matmul(cqn, w["w_uq"], scale=(MLA_NOPE + MLA_ROPE) ** -0.5 * LOG2_E,
                name=f"{tag}_uq")
    kv = _matmul(ckvn, w["w_ukv"], out_dtype=_ACT, name=f"{tag}_ukv")
    nn = MLA_HEADS * MLA_NOPE
    q_rope = _rope(q[:, nn:], mla_c, mla_s, out_dtype=_ACT, name=f"{tag}_rope_q")
    k_rope = _rope(kr, mla_c, mla_s, out_dtype=_ACT, name=f"{tag}_rope_k")[:, :MLA_ROPE]
    qh = jnp.concatenate([_heads_t(q[:, :nn].astype(_ACT), MLA_HEADS), _heads_t(q_rope, MLA_HEADS)], axis=1)
    kh = jnp.concatenate([_heads(kv[:, :nn], MLA_HEADS),
                          jnp.broadcast_to(k_rope[None], (MLA_HEADS, s, MLA_ROPE))], axis=-1)
    vh = _heads(kv[:, nn:], MLA_HEADS)
    oa, lse_a, *gathered = _attn_fwd(qh, kh, _chunk_t(kv[:, nn:], MLA_HEADS), out_dtype=_ACT, name=f"{tag}_mla",
                                     gather=gather)

    ng = 3 * DIL_SLOTS * DIL_HEAD_DIM
    outs, lses, dil_saved = [], [], []
    for gi, (_, dil) in enumerate(DIL_PAIRS):
        hs = slice(gi * DIL_SLOTS, (gi + 1) * DIL_SLOTS)
        qkv = _to_res(zb[:, gi * ng:(gi + 1) * ng], dil)
        qr, kr_, vr = qkv[:DIL_SLOTS], qkv[DIL_SLOTS:2 * DIL_SLOTS], qkv[2 * DIL_SLOTS:]
        o, l = _win_fwd(qr, kr_, vr, slopes[hs], dil=dil, name=f"{tag}_dil{gi}")
        dil_saved.append((qr, kr_, vr, l))
        outs.append(_res_to_tok(o, dil))
        lses.append(_res_to_tok(l, dil))
    o3, lse3 = jnp.stack(outs), jnp.stack(lses)
    comb = _merge_fwd(o3, lse3, name=f"{tag}_merge")
    cat = jnp.concatenate([_unheads_t(oa), _unheads(comb)], axis=-1)
    out = _matmul(cat, w["w_out"], res=x, name=f"{tag}_out")
    saved = (x, hn, cq, ckv, cqn, ckvn, qh, kh, vh, oa, lse_a, dil_saved, o3, lse3, cat)
    return out, saved, (gathered[0] if gathered else None)


def _even_bwd(dout, saved, w, tabs, slopes, tag, exchange=None):
    x, hn, cq, ckv, cqn, ckvn, qh, kh, vh, oa, lse_a, dil_saved, o3, lse3, cat = saved
    (mla_c, mla_s), _ = tabs
    nn = MLA_HEADS * MLA_NOPE
    dcat = _matmul(dout, w["w_out"], trans_b=True, out_dtype=_ACT, name=f"{tag}_dcat")
    d_w_out = _matmul(cat, dout, trans_a=True, name=f"{tag}_dwout")
    nv = MLA_HEADS * MLA_V

    doa = _heads(dcat[:, :nv], MLA_HEADS)
    dqh, dkt, dvt, *received = _attn_bwd(qh, kh, vh, oa, doa, lse_a, scale=(MLA_NOPE + MLA_ROPE) ** -0.5,
                                         name=f"{tag}_mla_bwd", exchange=exchange)
    dq_rope = _rope(_unheads(dqh[..., MLA_NOPE:]), mla_c, -mla_s, out_dtype=_ACT, name=f"{tag}_drope_q")
    dq = jnp.concatenate([_unheads(dqh[..., :MLA_NOPE]).astype(_ACT), dq_rope], axis=-1)
    dk3 = _unchunk(dkt).reshape(-1, MLA_HEADS, MLA_NOPE + MLA_ROPE)
    dkr = _rope(dk3[..., MLA_NOPE:].reshape(-1, MLA_HEADS * MLA_ROPE), mla_c, -mla_s, out_dtype=_F32,
                sum_chunks=True, name=f"{tag}_drope_k")
    dkv = jnp.concatenate([dk3[..., :MLA_NOPE].reshape(-1, nn), _unchunk(dvt)], axis=-1).astype(_ACT)
    d_w_uq = _matmul(cqn, dq, trans_a=True, name=f"{tag}_dwuq")
    d_w_ukv = _matmul(ckvn, dkv, trans_a=True, name=f"{tag}_dwukv")
    dcqn = _matmul(dq, w["w_uq"], trans_b=True, name=f"{tag}_dcqn")
    dckvn = _matmul(dkv, w["w_ukv"], trans_b=True, name=f"{tag}_dckvn")
    dcq, d_q_norm = _rmsnorm_bwd(cq, w["q_norm"], dcqn, out_dtype=_ACT, name=f"{tag}_dqnorm")
    dckv, d_kv_norm = _rmsnorm_bwd(ckv, w["kv_norm"], dckvn, out_dtype=_ACT, name=f"{tag}_dkvnorm")
    lane = jnp.arange(LANES) < MLA_ROPE
    dza = jnp.concatenate([dcq, dckv, jnp.where(lane[None], dkr, 0.0).astype(_ACT)], axis=-1)

    dcomb = _heads(dcat[:, nv:], DIL_SLOTS)
    do3, delta3 = _merge_bwd(dcomb, o3, lse3, name=f"{tag}_dmerge")
    dslabs = []
    for gi, (_, dil) in enumerate(DIL_PAIRS):
        hs = slice(gi * DIL_SLOTS, (gi + 1) * DIL_SLOTS)
        qr, kr_, vr, l = dil_saved[gi]
        grads = (qr, kr_, vr, _tok_to_res(do3[gi], dil), l, _tok_to_res(delta3[gi], dil), slopes[hs])
        a = _win_bwd_dq(*grads, dil=dil, name=f"{tag}_dil{gi}_dq")
        b, c = _win_bwd_dkv(*grads, dil=dil, name=f"{tag}_dil{gi}_dkv")
        dslabs.append(_from_res(jnp.concatenate([a, b, c], axis=0), dil))
    dzb = jnp.concatenate(dslabs, axis=-1)

    d_w_a = _matmul(hn, dza, trans_a=True, name=f"{tag}_dwa")
    d_w_b = _matmul(hn, dzb, trans_a=True, name=f"{tag}_dwb")
    dhn = _matmul(dza, w["w_a"], trans_b=True, name=f"{tag}_dhn_a")
    dhn = _matmul(dzb, w["w_b"], trans_b=True, res=dhn, name=f"{tag}_dhn_b")
    dx, dnorm = _rmsnorm_bwd(x, w["norm"], dhn, dout, name=f"{tag}_dnorm")
    grads = {"norm": dnorm, "w_a": d_w_a, "w_b": d_w_b, "q_norm": d_q_norm, "kv_norm": d_kv_norm,
             "w_uq": d_w_uq, "w_ukv": d_w_ukv, "w_out": d_w_out}
    return dx, grads, (received[0] if received else None)


def _odd_fwd(x, w, tabs, tag):
    s = x.shape[0]
    _, (gqa_c, gqa_s) = tabs
    nk = GQA_KV_HEADS * GQA_HEAD_DIM
    hn = _rmsnorm(x, w["norm"], out_dtype=_ACT, name=f"{tag}_norm")
    q = _matmul(hn, w["w_q"], name=f"{tag}_q")
    kv = _matmul(hn, w["w_kv"], name=f"{tag}_kv")
    k = kv[:, :nk]
    qh = _heads_t(_headnorm_rope(q, w["q_norm"], gqa_c, gqa_s, scale=GQA_HEAD_DIM ** -0.5 * LOG2_E,
                                 name=f"{tag}_prep_q"), GQA_HEADS)
    kh = _heads(_headnorm_rope(k, w["k_norm"], gqa_c, gqa_s, name=f"{tag}_prep_k"), GQA_KV_HEADS)
    v = kv[:, nk:].astype(_ACT)
    vh = _heads(v, GQA_KV_HEADS)
    o, lse = _attn_fwd(qh, kh, _chunk_t(v, GQA_KV_HEADS), out_dtype=_ACT, name=f"{tag}_gqa")
    ocat = _unheads_t(o)
    out = _matmul(ocat, w["w_o"], res=x, name=f"{tag}_out")
    return out, (x, hn, q, k, qh, kh, vh, o, lse, ocat)


def _odd_bwd(dout, saved, w, tabs, tag):
    x, hn, q, k, qh, kh, vh, o, lse, ocat = saved
    s = x.shape[0]
    _, (gqa_c, gqa_s) = tabs
    docat = _matmul(dout, w["w_o"], trans_b=True, out_dtype=_ACT, name=f"{tag}_docat")
    d_w_o = _matmul(ocat, dout, trans_a=True, name=f"{tag}_dwo")
    doh = _heads(docat, GQA_HEADS)
    dqh, dkt, dvt = _attn_bwd(qh, kh, vh, o, doh, lse, scale=GQA_HEAD_DIM ** -0.5, name=f"{tag}_gqa_bwd")
    dq, d_q_norm = _headnorm_rope_bwd(q, w["q_norm"], _unheads(dqh), gqa_c, gqa_s, name=f"{tag}_dprep_q")
    dk, d_k_norm = _headnorm_rope_bwd(k, w["k_norm"], _unchunk(dkt), gqa_c, gqa_s, name=f"{tag}_dprep_k")
    dkv = jnp.concatenate([dk, _unchunk(dvt).astype(_ACT)], axis=-1)
    d_w_q = _matmul(hn, dq, trans_a=True, name=f"{tag}_dwq")
    d_w_kv = _matmul(hn, dkv, trans_a=True, name=f"{tag}_dwkv")
    dhn = _matmul(dq, w["w_q"], trans_b=True, name=f"{tag}_dhn_q")
    dhn = _matmul(dkv, w["w_kv"], trans_b=True, res=dhn, name=f"{tag}_dhn_kv")
    dx, dnorm = _rmsnorm_bwd(x, w["norm"], dhn, dout, name=f"{tag}_dnorm")
    return dx, {"norm": dnorm, "w_q": d_w_q, "w_kv": d_w_kv, "q_norm": d_q_norm, "k_norm": d_k_norm, "w_o": d_w_o}


def _split_heads_cols(wm, heads, first):
    rws = wm.shape[0]
    w3 = wm.reshape(rws, heads, -1)
    return jnp.concatenate([w3[:, :, :first].reshape(rws, -1), w3[:, :, first:].reshape(rws, -1)], axis=-1)


def _merge_heads_cols(wm, heads, first):
    rws, cols = wm.shape
    a = wm[:, :heads * first].reshape(rws, heads, first)
    b = wm[:, heads * first:].reshape(rws, heads, cols // heads - first)
    return jnp.concatenate([a, b], axis=-1).reshape(rws, cols)


def _layer_weights(full, gains, layer):
    i = layer // 2

    def stacked(name, idx):
        for group, arrays in zip((_FIRST, _REST), full):
            lo, cnt = group.get(name, (0, 0))
            if lo <= idx < lo + cnt:
                return arrays[name][idx - lo]
        raise KeyError((name, idx))

    ffn = {"norm": gains["ffn_norm"][layer], "w_in_t": stacked("ffn_w_in", layer), "w_out": stacked("ffn_w_out", layer)}
    if layer % 2 == 0:
        w_in = stacked("w_in_ab", i)
        mix = {"norm": gains["mix_norm_ab"][i],
               "w_a": jnp.pad(w_in[:, :IN_A], ((0, 0), (0, IN_A_PAD - IN_A))), "w_b": _group_major(w_in[:, IN_A:]),
               "q_norm": gains["mla_q_norm"][i], "kv_norm": gains["mla_kv_norm"][i],
               "w_uq": _split_heads_cols(stacked("mla_w_uq", i), MLA_HEADS, MLA_NOPE),
               "w_ukv": _split_heads_cols(stacked("mla_w_ukv", i), MLA_HEADS, MLA_NOPE),
               "w_out": stacked("w_out_ab", i)}
    else:
        mix = {"norm": gains["mix_norm_c"][i], "w_q": stacked("gqa_w_q", i), "w_kv": stacked("gqa_w_kv", i),
               "q_norm": gains["gqa_q_norm"][i], "k_norm": gains["gqa_k_norm"][i], "w_o": stacked("gqa_w_o", i)}
    return mix, ffn


def _pack_grads(grads, group):
    parts = [_split_for_devices(jnp.stack(grads[n][group[n][0]:group[n][0] + group[n][1]]), axis)
             for n, axis in _BIG if n in group]
    fill = -sum(p.shape[1] for p in parts) % PACK_ROWS
    return jnp.concatenate(parts + [jnp.zeros((N_DEV, fill, PACK_COLS), parts[0].dtype)], axis=1)


def kernel(x, mix_norm_ab, w_in_ab, mla_q_norm, mla_kv_norm, mla_w_uq, mla_w_ukv, w_out_ab, mix_norm_c, gqa_w_q, gqa_w_kv, gqa_q_norm, gqa_k_norm, gqa_w_o, ffn_norm, ffn_w_in, ffn_w_out, final_norm, loss_target, m_mix_norm_ab, m_w_in_ab, m_mla_q_norm, m_mla_kv_norm, m_mla_w_uq, m_mla_w_ukv, m_w_out_ab, m_mix_norm_c, m_gqa_w_q, m_gqa_w_kv, m_gqa_q_norm, m_gqa_k_norm, m_gqa_w_o, m_ffn_norm, m_ffn_w_in, m_ffn_w_out, m_final_norm, v_mix_norm_ab, v_w_in_ab, v_mla_q_norm, v_mla_kv_norm, v_mla_w_uq, v_mla_w_ukv, v_w_out_ab, v_mix_norm_c, v_gqa_w_q, v_gqa_w_kv, v_gqa_q_norm, v_gqa_k_norm, v_gqa_w_o, v_ffn_norm, v_ffn_w_in, v_ffn_w_out, v_final_norm):
    wts = dict(mix_norm_ab=mix_norm_ab, w_in_ab=w_in_ab, mla_q_norm=mla_q_norm, mla_kv_norm=mla_kv_norm,
               mla_w_uq=mla_w_uq, mla_w_ukv=mla_w_ukv, w_out_ab=w_out_ab, mix_norm_c=mix_norm_c, gqa_w_q=gqa_w_q,
               gqa_w_kv=gqa_w_kv, gqa_q_norm=gqa_q_norm, gqa_k_norm=gqa_k_norm, gqa_w_o=gqa_w_o, ffn_norm=ffn_norm,
               ffn_w_in=ffn_w_in, ffn_w_out=ffn_w_out, final_norm=final_norm)
    mom = dict(mix_norm_ab=m_mix_norm_ab, w_in_ab=m_w_in_ab, mla_q_norm=m_mla_q_norm, mla_kv_norm=m_mla_kv_norm,
               mla_w_uq=m_mla_w_uq, mla_w_ukv=m_mla_w_ukv, w_out_ab=m_w_out_ab, mix_norm_c=m_mix_norm_c,
               gqa_w_q=m_gqa_w_q, gqa_w_kv=m_gqa_w_kv, gqa_q_norm=m_gqa_q_norm, gqa_k_norm=m_gqa_k_norm,
               gqa_w_o=m_gqa_w_o, ffn_norm=m_ffn_norm, ffn_w_in=m_ffn_w_in, ffn_w_out=m_ffn_w_out,
               final_norm=m_final_norm)
    var = dict(mix_norm_ab=v_mix_norm_ab, w_in_ab=v_w_in_ab, mla_q_norm=v_mla_q_norm, mla_kv_norm=v_mla_kv_norm,
               mla_w_uq=v_mla_w_uq, mla_w_ukv=v_mla_w_ukv, w_out_ab=v_w_out_ab, mix_norm_c=v_mix_norm_c,
               gqa_w_q=v_gqa_w_q, gqa_w_kv=v_gqa_w_kv, gqa_q_norm=v_gqa_q_norm, gqa_k_norm=v_gqa_k_norm,
               gqa_w_o=v_gqa_w_o, ffn_norm=v_ffn_norm, ffn_w_in=v_ffn_w_in, ffn_w_out=v_ffn_w_out,
               final_norm=v_final_norm)
    big_names = [n for n, _ in _BIG]
    groups = (_FIRST, _REST)
    big_w, big_m, big_v = (_stored({n: src[n] for n in big_names}) for src in (wts, mom, var))
    w_grp = [_group_of(big_w, grp) for grp in groups]
    shapes = [{n: a.shape for n, a in w.items()} for w in w_grp]
    w_packs = [_pack_big(w) for w in w_grp]
    n_rows = [p.shape[0] for p in w_packs]
    xs = x[0]
    s = xs.shape[0]
    me = 4 * lax.axis_index("x") + 2 * lax.axis_index("y") + lax.axis_index("c")
    c_cols = mix_norm_c.shape[1]

    tail = jnp.pad(mix_norm_c.reshape(-1), (0, GATHER_TAIL_ROWS // 2 * PACK_COLS - mix_norm_c.size))
    tail = _to_bits16(tail.reshape(GATHER_TAIL_ROWS // 2, PACK_COLS))
    gathered = _all_gather(jnp.concatenate([w_packs[0].astype(jnp.bfloat16), tail], axis=0))
    full = [_unpack_gathered(gathered[:, :n_rows[0]], shapes[0]), None]
    c_all = _from_bits16(gathered[:, n_rows[0]:]).reshape(N_DEV, -1)[:, :mix_norm_c.size]
    c_full = c_all.reshape(N_DEV, 2, c_cols).transpose(1, 0, 2).reshape(2, N_DEV * c_cols)
    gains = dict(mix_norm_ab=mix_norm_ab, mla_q_norm=mla_q_norm, mla_kv_norm=mla_kv_norm, mix_norm_c=c_full,
                 gqa_q_norm=gqa_q_norm, gqa_k_norm=gqa_k_norm, ffn_norm=ffn_norm)

    tabs = _rope_tables(s)
    slopes = jnp.exp2(-8.0 * jnp.arange(1, DIL_HEADS + 1, dtype=_F32) / DIL_HEADS)

    h = xs
    saved = []
    for layer in range(DEPTH):
        mix_w, ffn_w = _layer_weights(full, gains, layer)
        if layer == 0:
            h, sv_mix, rest = _even_fwd(h, mix_w, tabs, slopes, f"l{layer}_mix", gather=w_packs[1].astype(jnp.bfloat16))
            full[1] = _unpack_gathered(rest, shapes[1])
        elif layer % 2 == 0:
            h, sv_mix, _ = _even_fwd(h, mix_w, tabs, slopes, f"l{layer}_mix")
        else:
            h, sv_mix = _odd_fwd(h, mix_w, tabs, f"l{layer}_mix")
        h, sv_ffn = _ffn_fwd(h, ffn_w, f"l{layer}_ffn")
        saved.append((mix_w, ffn_w, sv_mix, sv_ffn))
    loss_local, dh, d_final = _final_loss(h, final_norm, loss_target[0], name="final_loss")

    gfull = {n: [None] * wts[n].shape[0] for n in big_names}
    gsmall = {n: [None] * (wts[n].shape[0] if wts[n].ndim > 1 else 1) for n in _SMALL}
    gsmall["final_norm"][0] = d_final
    for layer in reversed(range(DEPTH)):
        mix_w, ffn_w, sv_mix, sv_ffn = saved[layer]
        i = layer // 2
        dh, gf = _ffn_bwd(dh, sv_ffn, ffn_w, f"l{layer}_ffn")
        gsmall["ffn_norm"][layer] = gf["norm"]
        gfull["ffn_w_in"][layer] = gf["w_in_t"]
        gfull["ffn_w_out"][layer] = gf["w_out"]
        if layer % 2 == 0:
            sending = _pack_grads(gfull, _REST).astype(jnp.bfloat16) if layer == 0 else None
            dh, gm, got = _even_bwd(dh, sv_mix, mix_w, tabs, slopes, f"l{layer}_mix", exchange=sending)
            if layer == 0:
                received_rest = got
            gsmall["mix_norm_ab"][i] = gm["norm"]
            gsmall["mla_q_norm"][i] = gm["q_norm"]
            gsmall["mla_kv_norm"][i] = gm["kv_norm"]
            gfull["w_in_ab"][i] = jnp.concatenate([gm["w_a"][:, :IN_A], _group_major(gm["w_b"])], axis=-1)
            gfull["mla_w_uq"][i] = _merge_heads_cols(gm["w_uq"], MLA_HEADS, MLA_NOPE)
            gfull["mla_w_ukv"][i] = _merge_heads_cols(gm["w_ukv"], MLA_HEADS, MLA_NOPE)
            gfull["w_out_ab"][i] = gm["w_out"]
        else:
            dh, gm = _odd_bwd(dh, sv_mix, mix_w, tabs, f"l{layer}_mix")
            gsmall["mix_norm_c"][i] = gm["norm"]
            gsmall["gqa_q_norm"][i] = gm["q_norm"]
            gsmall["gqa_k_norm"][i] = gm["k_norm"]
            gfull["gqa_w_q"][i] = gm["w_q"]
            gfull["gqa_w_kv"][i] = gm["w_kv"]
            gfull["gqa_w_o"][i] = gm["w_o"]
    grad_x = dh[None]

    small_part = _pack_small({n: jnp.stack(gsmall[n]) for n in _SMALL})
    small_bits = jnp.broadcast_to(_to_bits16(small_part)[None], (N_DEV, EXCHANGE_TAIL_ROWS, PACK_COLS))
    received_first = _exchange(jnp.concatenate([_pack_grads(gfull, _FIRST).astype(jnp.bfloat16), small_bits], axis=1))
    big_packs = [_reduce_adamw(got, w_packs[gi], _pack_big(_group_of(big_m, grp)), _pack_big(_group_of(big_v, grp)),
                               name=f"adamw_big{gi}")
                 for gi, (grp, got) in enumerate(zip(groups, (received_first, received_rest)))]

    def widen_c(shard):
        return lax.dynamic_update_slice(jnp.zeros((2, N_DEV * c_cols), _F32), shard, (0, me * c_cols))

    def small_of(src):
        return _pack_small({n: (widen_c(src[n]) if n == "mix_norm_c" else src[n]) for n in _SMALL})

    small_recv = _from_bits16(received_first[:, n_rows[0]:])
    small_packs = _reduce_adamw(small_recv, small_of(wts), small_of(mom), small_of(var), name="adamw_small")

    def outputs_of(which):
        by_group = [_unpack_big(big_packs[gi][which], shapes[gi]) for gi in range(len(groups))]
        small = _unpack_small(small_packs[which])
        res = _stored({n: jnp.concatenate([grp[n] for grp in by_group if n in grp], axis=0) for n in big_names})
        for n in wts:
            if n in big_names:
                continue
            if n == "mix_norm_c":
                res[n] = lax.dynamic_slice(small[n].reshape(2, N_DEV * c_cols), (0, me * c_cols), (2, c_cols))
            else:
                res[n] = small[n].reshape(wts[n].shape)
        return [res[n] for n in wts]

    loss = lax.psum(loss_local, _AXES)
    return (loss, grad_x, *outputs_of(0), *outputs_of(1), *outputs_of(2), *outputs_of(3))
```

```python
import functools

import jax
import jax.numpy as jnp
from jax import lax
from jax.experimental import pallas as pl
from jax.experimental.pallas import tpu as pltpu

D_MODEL = 1024
DEPTH = 4
GRID_W = 64
NORM_EPS = 1e-6
ROPE_THETA = 10000.0
NEG_INF = -1e30
MLA_HEADS = 8
MLA_Q_RANK = 384
MLA_KV_RANK = 256
MLA_NOPE = 64
MLA_ROPE = 32
MLA_V = 64
DIL_PAIRS = ((128, 1), (512, 4), (2048, 16))
DIL_HALF = 64
DIL_SLOTS = 4
DIL_GROUPS = 3
DIL_HEADS = 12
DIL_HEAD_DIM = 64
GQA_HEADS = 16
GQA_KV_HEADS = 4
GQA_HEAD_DIM = 64
FFN_HIDDEN = 2816
IN_A = MLA_Q_RANK + MLA_KV_RANK + MLA_ROPE
IN_A_PAD = 768
IN_B = 3 * DIL_HEADS * DIL_HEAD_DIM
ADAM_LR = 0.001
ADAM_B1 = 0.9
ADAM_B2 = 0.999
ADAM_EPS = 1e-08
ADAM_WD = 0.01
ADAM_STEP = 10

LANES = 128
SUBLANES_16BIT = 16
VMEM_LIMIT_BYTES = 56 * 1024 * 1024

MM_TILE = 1408

N_DEV = 8
PACK_COLS = 1024

_MXU = jnp.bfloat16
_ACT = jnp.bfloat16
_F32 = jnp.float32

_AXES = ("x", "y", "c")


def _params(sem):
    return pltpu.CompilerParams(dimension_semantics=sem, vmem_limit_bytes=VMEM_LIMIT_BYTES)


def _pick(n, cap):
    for t in range(cap - cap % LANES, 0, -LANES):
        if n % t == 0:
            return t
    return n


def _rows(m, target):
    t = m
    while t > target and t % 2 == 0:
        t //= 2
    return t


def _matmul(a, b, *, trans_a=False, trans_b=False, res=None, scale=None, out_dtype=_F32, name):
    if trans_a:
        k, m = a.shape
    else:
        m, k = a.shape
    if trans_b:
        n, kb = b.shape
    else:
        kb, n = b.shape
    assert k == kb, (a.shape, b.shape)
    tm, tn, tk = _pick(m, MM_TILE), _pick(n, MM_TILE), _pick(k, MM_TILE)
    nk = k // tk
    dims = (((0 if trans_a else 1,), (1 if trans_b else 0,)), ((), ()))

    def body(*refs):
        if res is None:
            a_ref, b_ref, o_ref, acc = refs
            r_ref = None
        else:
            a_ref, b_ref, r_ref, o_ref, acc = refs
        kk = pl.program_id(2)

        @pl.when(kk == 0)
        def _():
            acc[...] = jnp.zeros_like(acc)

        acc[...] += lax.dot_general(a_ref[...].astype(_MXU), b_ref[...].astype(_MXU), dims,
                                    preferred_element_type=_F32)

        @pl.when(kk == nk - 1)
        def _():
            r = acc[...]
            if scale is not None:
                r = r * scale
            if r_ref is not None:
                r = r + r_ref[...].astype(_F32)
            o_ref[...] = r.astype(out_dtype)

    a_spec = (pl.BlockSpec((tk, tm), lambda i, j, kk: (kk, i)) if trans_a
              else pl.BlockSpec((tm, tk), lambda i, j, kk: (i, kk)))
    b_spec = (pl.BlockSpec((tn, tk), lambda i, j, kk: (j, kk)) if trans_b
              else pl.BlockSpec((tk, tn), lambda i, j, kk: (kk, j)))
    o_spec = pl.BlockSpec((tm, tn), lambda i, j, kk: (i, j))
    in_specs = [a_spec, b_spec] + ([o_spec] if res is not None else [])
    args = (a, b) + ((res,) if res is not None else ())
    return pl.pallas_call(
        body, name=name, grid=(m // tm, n // tn, nk),
        in_specs=in_specs, out_specs=o_spec,
        out_shape=jax.ShapeDtypeStruct((m, n), out_dtype),
        scratch_shapes=[pltpu.VMEM((tm, tn), _F32)],
        compiler_params=_params(("parallel", "parallel", "arbitrary")),
    )(*args)


def _rmsnorm(x, g, *, out_dtype, name, rows=512):
    m, d = x.shape
    tm = _rows(m, rows)

    def body(x_ref, g_ref, o_ref):
        xf = x_ref[...].astype(_F32)
        r = lax.rsqrt(jnp.mean(xf * xf, axis=-1, keepdims=True) + NORM_EPS)
        o_ref[...] = ((xf * r) * g_ref[...]).astype(out_dtype)

    return pl.pallas_call(
        body, name=name, grid=(m // tm,),
        in_specs=[pl.BlockSpec((tm, d), lambda i: (i, 0)), pl.BlockSpec((1, d), lambda i: (0, 0))],
        out_specs=pl.BlockSpec((tm, d), lambda i: (i, 0)),
        out_shape=jax.ShapeDtypeStruct((m, d), out_dtype),
        compiler_params=_params(("parallel",)),
    )(x, g.reshape(1, d).astype(_F32))


def _rmsnorm_bwd(x, g, dy, dres=None, *, out_dtype=_F32, name, rows=512):
    m, d = x.shape
    tm = _rows(m, rows)

    def body(*refs):
        if dres is None:
            x_ref, g_ref, dy_ref, dx_ref, dg_ref = refs
            r_ref = None
        else:
            x_ref, g_ref, dy_ref, r_ref, dx_ref, dg_ref = refs

        @pl.when(pl.program_id(0) == 0)
        def _():
            dg_ref[...] = jnp.zeros_like(dg_ref)

        xf = x_ref[...].astype(_F32)
        r = lax.rsqrt(jnp.mean(xf * xf, axis=-1, keepdims=True) + NORM_EPS)
        xh = xf * r
        dyf = dy_ref[...].astype(_F32)
        dg_ref[...] += jnp.sum(dyf * xh, axis=0, keepdims=True)
        gdy = dyf * g_ref[...]
        dx = r * (gdy - xh * jnp.mean(gdy * xh, axis=-1, keepdims=True))
        if r_ref is not None:
            dx = dx + r_ref[...].astype(_F32)
        dx_ref[...] = dx.astype(out_dtype)

    row = pl.BlockSpec((tm, d), lambda i: (i, 0))
    vec = pl.BlockSpec((1, d), lambda i: (0, 0))
    in_specs = [row, vec, row] + ([row] if dres is not None else [])
    args = (x, g.reshape(1, d).astype(_F32), dy) + ((dres,) if dres is not None else ())
    dx, dg = pl.pallas_call(
        body, name=name, grid=(m // tm,),
        in_specs=in_specs, out_specs=[row, vec],
        out_shape=[jax.ShapeDtypeStruct((m, d), out_dtype), jax.ShapeDtypeStruct((1, d), _F32)],
        compiler_params=_params(("arbitrary",)),
    )(*args)
    return dx, dg.reshape(d)


def _rotate(xf, c, sn):
    w = xf.shape[1]
    if w > LANES:
        c, sn = jnp.tile(c, (1, w // LANES)), jnp.tile(sn, (1, w // LANES))
    lane = lax.broadcasted_iota(jnp.int32, xf.shape, 1)
    sw = jnp.where((lane & 31) < 16, pltpu.roll(xf, w - 16, 1), pltpu.roll(xf, 16, 1))
    return xf * c + sw * sn


def _seg_mean(v, seg_ref):
    outs = []
    for c in range(v.shape[1] // LANES):
        piece = v[:, c * LANES:(c + 1) * LANES]
        hi = piece.astype(jnp.bfloat16)
        lo = (piece - hi.astype(_F32)).astype(jnp.bfloat16)
        outs.append(jnp.dot(hi, seg_ref[...], preferred_element_type=_F32)
                    + jnp.dot(lo, seg_ref[...], preferred_element_type=_F32))
    return jnp.concatenate(outs, axis=1) if len(outs) > 1 else outs[0]


def _seg_matrix():
    lane = jnp.arange(LANES) // GQA_HEAD_DIM
    return ((lane[:, None] == lane[None, :]).astype(_F32) / GQA_HEAD_DIM).astype(jnp.bfloat16)


def _headnorm_rope(x, gain, cos_t, sin_t, *, scale=None, name):
    s, w = x.shape
    ts = _rows(s, 512)

    def body(x_ref, g_ref, seg_ref, c_ref, s_ref, o_ref):
        xf = x_ref[...]
        r = lax.rsqrt(_seg_mean(xf * xf, seg_ref) + NORM_EPS)
        y = _rotate((xf * r) * g_ref[...], c_ref[...], s_ref[...])
        if scale is not None:
            y = y * scale
        o_ref[...] = y.astype(o_ref.dtype)

    row = pl.BlockSpec((ts, w), lambda i: (i, 0))
    tab = pl.BlockSpec((ts, LANES), lambda i: (i, 0))
    return pl.pallas_call(
        body, name=name, grid=(s // ts,),
        in_specs=[row, pl.BlockSpec((1, w), lambda i: (0, 0)), pl.BlockSpec((LANES, LANES), lambda i: (0, 0)), tab, tab],
        out_specs=row, out_shape=jax.ShapeDtypeStruct((s, w), _ACT),
        compiler_params=_params(("parallel",)),
    )(x, jnp.tile(gain.astype(_F32), w // GQA_HEAD_DIM).reshape(1, w), _seg_matrix(), cos_t, sin_t)


def _headnorm_rope_bwd(x, gain, dy, cos_t, sin_t, *, name):
    s, w = x.shape
    ts = _rows(s, 512)

    def body(x_ref, g_ref, seg_ref, c_ref, s_ref, dy_ref, dx_ref, dg_ref):
        @pl.when(pl.program_id(0) == 0)
        def _():
            dg_ref[...] = jnp.zeros_like(dg_ref)

        xf = x_ref[...]
        r = lax.rsqrt(_seg_mean(xf * xf, seg_ref) + NORM_EPS)
        xh = xf * r
        dyn = _rotate(dy_ref[...].astype(_F32), c_ref[...], -s_ref[...])
        dg_ref[...] += jnp.sum(dyn * xh, axis=0, keepdims=True)
        gdy = dyn * g_ref[...]
        dx_ref[...] = (r * (gdy - xh * _seg_mean(gdy * xh, seg_ref))).astype(dx_ref.dtype)

    row = pl.BlockSpec((ts, w), lambda i: (i, 0))
    vec = pl.BlockSpec((1, w), lambda i: (0, 0))
    tab = pl.BlockSpec((ts, LANES), lambda i: (i, 0))
    dx, dg = pl.pallas_call(
        body, name=name, grid=(s // ts,),
        in_specs=[row, vec, pl.BlockSpec((LANES, LANES), lambda i: (0, 0)), tab, tab, row],
        out_specs=[row, vec],
        out_shape=[jax.ShapeDtypeStruct((s, w), _ACT), jax.ShapeDtypeStruct((1, w), _F32)],
        compiler_params=_params(("arbitrary",)),
    )(x, jnp.tile(gain.astype(_F32), w // GQA_HEAD_DIM).reshape(1, w), _seg_matrix(), cos_t, sin_t, dy)
    return dx, dg.reshape(w // GQA_HEAD_DIM, GQA_HEAD_DIM).sum(axis=0)


def _rope(x, cos_t, sin_t, *, out_dtype, name, sum_chunks=False, scale=None):
    s, w = x.shape
    assert w % LANES == 0
    ts = _rows(s, 512)
    ow = LANES if sum_chunks else w

    def body(x_ref, c_ref, s_ref, o_ref):
        y = _rotate(x_ref[...].astype(_F32), c_ref[...], s_ref[...])
        if scale is not None:
            y = y * scale
        if sum_chunks:
            shift = w // 2
            while shift >= 32:
                y = y + pltpu.roll(y, shift, 1)
                shift //= 2
            y = y[:, :LANES]
        o_ref[...] = y.astype(out_dtype)

    return pl.pallas_call(
        body, name=name, grid=(s // ts,),
        in_specs=[pl.BlockSpec((ts, w), lambda i: (i, 0)), pl.BlockSpec((ts, LANES), lambda i: (i, 0)),
                  pl.BlockSpec((ts, LANES), lambda i: (i, 0))],
        out_specs=pl.BlockSpec((ts, ow), lambda i: (i, 0)),
        out_shape=jax.ShapeDtypeStruct((s, ow), out_dtype),
        compiler_params=_params(("parallel",)),
    )(x, cos_t, sin_t)


_NT = (((1,), (1,)), ((), ()))
_TN = (((0,), (0,)), ((), ()))
LOG2_E = 1.4426950408889634
LN_2 = 0.6931471805599453
ATTN_FWD_ROWS = 2048
ATTN_ROWS = 512
ATTN_CHAINS = 2
ATTN_FWD_KEYS = 1024
ATTN_BWD_KEYS = 1024


def _grid_step(dims):
    step, total = 0, 1
    for axis, n in enumerate(dims):
        step = step * n + pl.program_id(axis)
        total *= n
    return step, total


def _attn_fwd(qt, k, vt, *, out_dtype, name, gather=None):
    h, dk, s = qt.shape
    g, nk, dv, tk = vt.shape
    assert nk * tk == s
    r = h // g
    rc = min(ATTN_FWD_ROWS, s)
    hp = min(r, ATTN_CHAINS)
    nrc = max(1, min(ATTN_CHAINS // hp, s // rc))
    tq = rc * nrc
    nhp, nq = r // hp, s // tq
    units = [(a, c) for a in range(hp) for c in range(nrc)]

    def body(q_ref, k_ref, v_ref, *rest):
        if gather is None:
            o_ref, lse_ref = rest
        else:
            x_ref, o_ref, lse_ref, all_ref, send_sems, recv_sems, local_sem = rest
            start, forward, finish = _gather_phases(x_ref, all_ref, send_sems, recv_sems, local_sem)
            step, total = _grid_step((g, nhp, nq))
            pl.when(step == 0)(start)
            pl.when(step == 3 * total // 4)(forward)

        qs = [q_ref[a, :, c * rc:(c + 1) * rc] for a, c in units]
        init = tuple((jnp.full((1, rc), NEG_INF, _F32), jnp.zeros((1, rc), _F32), jnp.zeros((dv, rc), _F32))
                     for _ in units)

        def trip(j, carry):
            kb, vb = k_ref[0, pl.ds(pl.multiple_of(j * tk, tk), tk), :], v_ref[0, j]
            out = []
            for u in range(len(units)):
                m, l, acc = carry[u]
                sc = jnp.dot(kb, qs[u], preferred_element_type=_F32)
                m_new = jnp.maximum(m, jnp.max(sc, axis=0, keepdims=True))
                p = jnp.exp2(sc - m_new)
                alpha = jnp.exp2(m - m_new)
                l = alpha * l + jnp.sum(p, axis=0, keepdims=True)
                acc = alpha * acc + jnp.dot(vb, p.astype(_MXU), preferred_element_type=_F32)
                out.append((m_new, l, acc))
            return tuple(out)

        fin = lax.fori_loop(0, nk, trip, init)
        for u, (a, c) in enumerate(units):
            m, l, acc = fin[u]
            o_ref[a, :, c * rc:(c + 1) * rc] = (acc / l).astype(out_dtype)
            lse_ref[a, :, c * rc:(c + 1) * rc] = m + jnp.log2(l)

        if gather is not None:
            pl.when(step == total - 1)(finish)

    q_blk = lambda gg, hh, i: (gg * nhp + hh, 0, i)
    in_specs = [pl.BlockSpec((hp, dk, tq), q_blk), pl.BlockSpec((1, s, dk), lambda gg, hh, i: (gg, 0, 0)),
                pl.BlockSpec((1, nk, dv, tk), lambda gg, hh, i: (gg, 0, 0, 0))]
    out_specs = [pl.BlockSpec((hp, dv, tq), q_blk), pl.BlockSpec((hp, 1, tq), q_blk)]
    out_shape = [jax.ShapeDtypeStruct((h, dv, s), out_dtype), jax.ShapeDtypeStruct((h, 1, s), _F32)]
    if gather is None:
        return pl.pallas_call(
            body, name=name, grid=(g, nhp, nq), in_specs=in_specs, out_specs=out_specs, out_shape=out_shape,
            compiler_params=_params(("parallel", "parallel", "parallel")),
        )(qt, k, vt)
    return pl.pallas_call(
        body, name=name, grid=(g, nhp, nq),
        in_specs=in_specs + [pl.BlockSpec(memory_space=pl.ANY)],
        out_specs=out_specs + [pl.BlockSpec(memory_space=pl.ANY)],
        out_shape=out_shape + [jax.ShapeDtypeStruct((N_DEV,) + gather.shape, gather.dtype)],
        scratch_shapes=_comm_scratch(),
        compiler_params=_params(("arbitrary", "arbitrary", "arbitrary")),
    )(qt, k, vt, gather)


def _attn_bwd(qt, k, v, ot, do, lse2, *, scale, name, exchange=None):
    h, dk, s = qt.shape
    g, _, dv = v.shape
    r = h // g
    hp = min(r, ATTN_CHAINS)
    tq = min(ATTN_ROWS, s)
    tk = min(ATTN_BWD_KEYS * (ATTN_CHAINS // hp), s)
    nhp, nq, nk = r // hp, s // tq, s // tk

    def body(qt_ref, k_ref, v_ref, ot_ref, do_ref, lse_ref, *rest):
        if exchange is None:
            dq_ref, dk_ref, dv_ref = rest
        else:
            g_ref, dq_ref, dk_ref, dv_ref, got_ref, send_sems, recv_sems, local_sem = rest
            start, finish = _exchange_phases(g_ref, got_ref, send_sems, recv_sems, local_sem)
            step, total = _grid_step((g, nhp, nq))
            pl.when(step == 0)(start)
        hh, i = pl.program_id(1), pl.program_id(2)

        @pl.when((hh == 0) & (i == 0))
        def _():
            dk_ref[...] = jnp.zeros_like(dk_ref)
            dv_ref[...] = jnp.zeros_like(dv_ref)

        qts = [qt_ref[a] for a in range(hp)]
        qs = [qt_ref[a].T for a in range(hp)]
        dos = [do_ref[a] for a in range(hp)]
        dots = [do_ref[a].T for a in range(hp)]
        ls = [lse_ref[a].T for a in range(hp)]
        dls = [jnp.sum(do_ref[a].astype(_F32) * ot_ref[a].T.astype(_F32), axis=1, keepdims=True)
               for a in range(hp)]

        def trip(j, carry):
            rows = pl.ds(pl.multiple_of(j * tk, tk), tk)
            kb, vb = k_ref[0, rows, :], v_ref[0, rows, :]
            out = []
            for a in range(hp):
                p = jnp.exp2(lax.dot_general(qs[a], kb, _NT, preferred_element_type=_F32) - ls[a])
                dp = lax.dot_general(dos[a], vb, _NT, preferred_element_type=_F32)
                ds = (p * (dp - dls[a])).astype(_MXU)
                dv_ref[0, j] += jnp.dot(dots[a], p.astype(_MXU), preferred_element_type=_F32)
                dk_ref[0, j] += jnp.dot(qts[a], ds, preferred_element_type=_F32)
                out.append(carry[a] + jnp.dot(ds, kb, preferred_element_type=_F32))
            return tuple(out)

        fin = lax.fori_loop(0, nk, trip, tuple(jnp.zeros((tq, dk), _F32) for _ in range(hp)))
        for a in range(hp):
            dq_ref[a] = (fin[a] * scale).astype(dq_ref.dtype)

        @pl.when((hh == nhp - 1) & (i == nq - 1))
        def _():
            dk_ref[...] = dk_ref[...] * LN_2

        if exchange is not None:
            pl.when(step == total - 1)(finish)

    q_blk = lambda gg, hh, i: (gg * nhp + hh, i, 0)
    qt_blk = lambda gg, hh, i: (gg * nhp + hh, 0, i)
    kv_blk = lambda gg, hh, i: (gg, 0, 0)
    in_specs = [pl.BlockSpec((hp, dk, tq), qt_blk), pl.BlockSpec((1, s, dk), kv_blk), pl.BlockSpec((1, s, dv), kv_blk),
                pl.BlockSpec((hp, dv, tq), qt_blk), pl.BlockSpec((hp, tq, dv), q_blk), pl.BlockSpec((hp, 1, tq), qt_blk)]
    out_specs = [pl.BlockSpec((hp, tq, dk), q_blk), pl.BlockSpec((1, nk, dk, tk), lambda gg, hh, i: (gg, 0, 0, 0)),
                 pl.BlockSpec((1, nk, dv, tk), lambda gg, hh, i: (gg, 0, 0, 0))]
    out_shape = [jax.ShapeDtypeStruct((h, s, dk), _ACT), jax.ShapeDtypeStruct((g, nk, dk, tk), _F32),
                 jax.ShapeDtypeStruct((g, nk, dv, tk), _F32)]
    args = (qt, k, v, ot, do, lse2)
    scratch = []
    if exchange is not None:
        in_specs, args = in_specs + [pl.BlockSpec(memory_space=pl.ANY)], args + (exchange,)
        out_specs = out_specs + [pl.BlockSpec(memory_space=pl.ANY)]
        out_shape = out_shape + [jax.ShapeDtypeStruct(exchange.shape, exchange.dtype)]
        scratch = _comm_scratch()
    return pl.pallas_call(
        body, name=name, grid=(g, nhp, nq), in_specs=in_specs, out_specs=out_specs, out_shape=out_shape,
        scratch_shapes=scratch, compiler_params=_params(("arbitrary", "arbitrary", "arbitrary")),
    )(*args)


def _unchunk(xt):
    g, nk, d, tk = xt.shape
    return xt.transpose(1, 3, 0, 2).reshape(nk * tk, g * d)


def _chunk_t(x2d, g):
    s = x2d.shape[0]
    tk = min(ATTN_FWD_KEYS, s)
    return x2d.reshape(s // tk, tk, g, -1).transpose(2, 0, 3, 1)


def _heads_t(x2d, h):
    s = x2d.shape[0]
    return x2d.reshape(s, h, -1).transpose(1, 2, 0)


def _unheads_t(xt):
    h, d, s = xt.shape
    return xt.transpose(2, 0, 1).reshape(s, h * d)


WIN_ROWS = 512


def _win_geometry(s, dil):
    t = min(WIN_ROWS, s)
    length = s // dil
    lg = length.bit_length() - 1
    assert 1 << lg == length and t % DIL_HALF == 0 and s % t == 0
    return t, t + 2 * DIL_HALF, lg, t // DIL_HALF, s // DIL_HALF


def _win_specs(hn, t, d, halo_per_blk, n_halo):
    return [pl.BlockSpec((hn, DIL_HALF, d), lambda i: (0, jnp.maximum(i * halo_per_blk - 1, 0), 0)),
            pl.BlockSpec((hn, t, d), lambda i: (0, i, 0)),
            pl.BlockSpec((hn, DIL_HALF, d), lambda i: (0, jnp.minimum((i + 1) * halo_per_blk, n_halo - 1), 0))]


def _win_mask(i, t, w, lg, dil, wide_rows):
    shape = (w, t) if wide_rows else (t, w)
    rows = lax.broadcasted_iota(jnp.int32, shape, 0)
    cols = lax.broadcasted_iota(jnp.int32, shape, 1)
    base = i * t
    if wide_rows:
        pq, pk = base - DIL_HALF + rows, base + cols
    else:
        pq, pk = base + rows, base - DIL_HALF + cols
    arel = jnp.abs(pk - pq)
    valid = (arel <= DIL_HALF) & ((pk >> lg) == (pq >> lg))
    return valid, (-dil * arel).astype(_F32)


def _win3(lo_ref, mid_ref, hi_ref, a):
    return jnp.concatenate([lo_ref[a], mid_ref[a], hi_ref[a]], axis=0)


def _win_fwd(q, k, v, slopes, *, dil, name):
    hn, s, d = q.shape
    t, w, lg, hpb, n_halo = _win_geometry(s, dil)
    scale = d ** -0.5

    def body(sl_ref, q_ref, klo, kmid, khi, vlo, vmid, vhi, o_ref, lse_ref):
        valid, nb = _win_mask(pl.program_id(0), t, w, lg, dil, False)
        for a in range(hn):
            kw, vw = _win3(klo, kmid, khi, a), _win3(vlo, vmid, vhi, a)
            sc = lax.dot_general(q_ref[a], kw, _NT, preferred_element_type=_F32) * scale
            sc = jnp.where(valid, sc + sl_ref[a] * nb, NEG_INF)
            m = jnp.max(sc, axis=1, keepdims=True)
            e = jnp.exp(sc - m)
            den = jnp.sum(e, axis=1, keepdims=True)
            o_ref[a] = jnp.dot(e.astype(_MXU), vw, preferred_element_type=_F32) / den
            lse_ref[a] = m + jnp.log(den)

    kv = _win_specs(hn, t, d, hpb, n_halo)
    blk = lambda c: pl.BlockSpec((hn, t, c), lambda i: (0, i, 0))
    return pl.pallas_call(
        body, name=name, grid=(s // t,),
        in_specs=[pl.BlockSpec(memory_space=pltpu.SMEM), blk(d)] + kv + kv,
        out_specs=[blk(d), blk(1)],
        out_shape=[jax.ShapeDtypeStruct((hn, s, d), _F32), jax.ShapeDtypeStruct((hn, s, 1), _F32)],
        compiler_params=_params(("parallel",)),
    )(slopes.astype(_F32), q, k, k, k, v, v, v)


def _win_bwd_dq(q, k, v, do, lse, delta, slopes, *, dil, name):
    hn, s, d = q.shape
    t, w, lg, hpb, n_halo = _win_geometry(s, dil)
    scale = d ** -0.5

    def body(sl_ref, q_ref, klo, kmid, khi, vlo, vmid, vhi, do_ref, lse_ref, dl_ref, dq_ref):
        valid, nb = _win_mask(pl.program_id(0), t, w, lg, dil, False)
        for a in range(hn):
            kw, vw = _win3(klo, kmid, khi, a), _win3(vlo, vmid, vhi, a)
            sc = lax.dot_general(q_ref[a], kw, _NT, preferred_element_type=_F32) * scale
            p = jnp.exp(jnp.where(valid, sc + sl_ref[a] * nb, NEG_INF) - lse_ref[a])
            dp = lax.dot_general(do_ref[a], vw, _NT, preferred_element_type=_F32)
            ds = (p * (dp - dl_ref[a])).astype(_MXU)
            dq_ref[a] = (jnp.dot(ds, kw, preferred_element_type=_F32) * scale).astype(dq_ref.dtype)

    kv = _win_specs(hn, t, d, hpb, n_halo)
    blk = lambda c: pl.BlockSpec((hn, t, c), lambda i: (0, i, 0))
    return pl.pallas_call(
        body, name=name, grid=(s // t,),
        in_specs=[pl.BlockSpec(memory_space=pltpu.SMEM), blk(d)] + kv + kv + [blk(d), blk(1), blk(1)],
        out_specs=blk(d),
        out_shape=jax.ShapeDtypeStruct((hn, s, d), _ACT),
        compiler_params=_params(("parallel",)),
    )(slopes.astype(_F32), q, k, k, k, v, v, v, do, lse, delta)


def _win_bwd_dkv(q, k, v, do, lse, delta, slopes, *, dil, name):
    hn, s, d = q.shape
    t, w, lg, hpb, n_halo = _win_geometry(s, dil)
    scale = d ** -0.5

    def body(sl_ref, qlo, qmid, qhi, dolo, domid, dohi, llo, lmid, lhi, dllo, dlmid, dlhi, k_ref, v_ref,
             dk_ref, dv_ref):
        valid, nb = _win_mask(pl.program_id(0), t, w, lg, dil, True)
        for a in range(hn):
            qw, dow = _win3(qlo, qmid, qhi, a), _win3(dolo, domid, dohi, a)
            lw, dlw = _win3(llo, lmid, lhi, a), _win3(dllo, dlmid, dlhi, a)
            sc = lax.dot_general(qw, k_ref[a], _NT, preferred_element_type=_F32) * scale
            p = jnp.exp(jnp.where(valid, sc + sl_ref[a] * nb, NEG_INF) - lw)
            dp = lax.dot_general(dow, v_ref[a], _NT, preferred_element_type=_F32)
            ds = (p * (dp - dlw)).astype(_MXU)
            dv_ref[a] = lax.dot_general(p.astype(_MXU), dow, _TN, preferred_element_type=_F32).astype(dv_ref.dtype)
            dk_ref[a] = (lax.dot_general(ds, qw, _TN, preferred_element_type=_F32) * scale).astype(dk_ref.dtype)

    blk = lambda c: pl.BlockSpec((hn, t, c), lambda i: (0, i, 0))
    return pl.pallas_call(
        body, name=name, grid=(s // t,),
        in_specs=([pl.BlockSpec(memory_space=pltpu.SMEM)] + _win_specs(hn, t, d, hpb, n_halo)
                  + _win_specs(hn, t, d, hpb, n_halo) + _win_specs(hn, t, 1, hpb, n_halo)
                  + _win_specs(hn, t, 1, hpb, n_halo) + [blk(d), blk(d)]),
        out_specs=[blk(d), blk(d)],
        out_shape=[jax.ShapeDtypeStruct((hn, s, d), _ACT), jax.ShapeDtypeStruct((hn, s, d), _ACT)],
        compiler_params=_params(("parallel",)),
    )(slopes.astype(_F32), q, q, q, do, do, do, lse, lse, lse, delta, delta, delta, k, v)


def _merge_weights(lses):
    mx = functools.reduce(jnp.maximum, lses)
    es = [jnp.exp(l - mx) for l in lses]
    den = functools.reduce(lambda a, b: a + b, es)
    return [e / den for e in es]


def _merge_fwd(outs, lses, *, name):
    ng = len(outs)
    sl, s, d = outs[0].shape
    t = _rows(s, 512)

    def body(*refs):
        o_refs, l_refs, c_ref = refs[:ng], refs[ng:2 * ng], refs[2 * ng]
        wts = _merge_weights([r[0] for r in l_refs])
        comb = functools.reduce(lambda a, b: a + b, [w * r[0] for w, r in zip(wts, o_refs)])
        c_ref[0] = comb.astype(c_ref.dtype)

    big = pl.BlockSpec((1, t, d), lambda a, i: (a, i, 0))
    small = pl.BlockSpec((1, t, 1), lambda a, i: (a, i, 0))
    return pl.pallas_call(
        body, name=name, grid=(sl, s // t),
        in_specs=[big] * ng + [small] * ng, out_specs=big,
        out_shape=jax.ShapeDtypeStruct((sl, s, d), _ACT),
        compiler_params=_params(("parallel", "parallel")),
    )(*outs, *lses)


def _merge_bwd(dcomb, outs, lses, *, name):
    ng = len(outs)
    sl, s, d = outs[0].shape
    t = _rows(s, 512)

    def body(*refs):
        dc_ref, o_refs, l_refs = refs[0], refs[1:1 + ng], refs[1 + ng:1 + 2 * ng]
        do_refs, dl_refs = refs[1 + 2 * ng:1 + 3 * ng], refs[1 + 3 * ng:]
        wts = _merge_weights([r[0] for r in l_refs])
        dc = dc_ref[0].astype(_F32)
        comb = functools.reduce(lambda a, b: a + b, [w * r[0] for w, r in zip(wts, o_refs)])
        dot = jnp.sum(dc * comb, axis=-1, keepdims=True)
        for w, do_ref, dl_ref in zip(wts, do_refs, dl_refs):
            do_ref[0] = (w * dc).astype(do_ref.dtype)
            dl_ref[0] = w * dot

    big = pl.BlockSpec((1, t, d), lambda a, i: (a, i, 0))
    small = pl.BlockSpec((1, t, 1), lambda a, i: (a, i, 0))
    res = pl.pallas_call(
        body, name=name, grid=(sl, s // t),
        in_specs=[big] + [big] * ng + [small] * ng, out_specs=[big] * ng + [small] * ng,
        out_shape=[jax.ShapeDtypeStruct((sl, s, d), _ACT)] * ng + [jax.ShapeDtypeStruct((sl, s, 1), _F32)] * ng,
        compiler_params=_params(("parallel", "parallel")),
    )(dcomb, *outs, *lses)
    return res[:ng], res[ng:]


SWIGLU_ROWS = 512


def _gate_up(a, w_in_t, *, name):
    m, k = a.shape
    n = w_in_t.shape[0] // 2
    tm, tn, tk = _pick(m, SWIGLU_ROWS), _pick(n, MM_TILE), _pick(k, MM_TILE)
    nk = k // tk

    def body(a_ref, g_ref, u_ref, go_ref, uo_ref, act_ref, acc_g, acc_u):
        kk = pl.program_id(2)

        @pl.when(kk == 0)
        def _():
            acc_g[...] = jnp.zeros_like(acc_g)
            acc_u[...] = jnp.zeros_like(acc_u)

        av = a_ref[...].astype(_MXU)
        acc_g[...] += lax.dot_general(av, g_ref[...].astype(_MXU), _NT, preferred_element_type=_F32)
        acc_u[...] += lax.dot_general(av, u_ref[...].astype(_MXU), _NT, preferred_element_type=_F32)

        @pl.when(kk == nk - 1)
        def _():
            gf, uf = acc_g[...], acc_u[...]
            go_ref[...] = gf.astype(go_ref.dtype)
            uo_ref[...] = uf.astype(uo_ref.dtype)
            act_ref[...] = (gf * jax.nn.sigmoid(gf) * uf).astype(act_ref.dtype)

    o_spec = pl.BlockSpec((tm, tn), lambda i, j, kk: (i, j))
    out = jax.ShapeDtypeStruct((m, n), _ACT)
    return pl.pallas_call(
        body, name=name, grid=(m // tm, n // tn, nk),
        in_specs=[pl.BlockSpec((tm, tk), lambda i, j, kk: (i, kk)),
                  pl.BlockSpec((tn, tk), lambda i, j, kk: (j, kk)),
                  pl.BlockSpec((tn, tk), lambda i, j, kk: (j + n // tn, kk))],
        out_specs=[o_spec, o_spec, o_spec], out_shape=[out, out, out],
        scratch_shapes=[pltpu.VMEM((tm, tn), _F32), pltpu.VMEM((tm, tn), _F32)],
        compiler_params=_params(("parallel", "parallel", "arbitrary")),
    )(a, w_in_t, w_in_t)


def _gate_up_bwd(dout, w_out, gate, up, *, name):
    m, k = dout.shape
    n = w_out.shape[0]
    tm, tn, tk = _pick(m, SWIGLU_ROWS), _pick(n, MM_TILE), _pick(k, MM_TILE)
    nk = k // tk

    def body(d_ref, w_ref, g_ref, u_ref, dg_ref, du_ref, acc):
        kk = pl.program_id(2)

        @pl.when(kk == 0)
        def _():
            acc[...] = jnp.zeros_like(acc)

        acc[...] += lax.dot_general(d_ref[...].astype(_MXU), w_ref[...].astype(_MXU), _NT,
                                    preferred_element_type=_F32)

        @pl.when(kk == nk - 1)
        def _():
            daf = acc[...]
            gf = g_ref[...].astype(_F32)
            sg = jax.nn.sigmoid(gf)
            dg_ref[...] = (daf * u_ref[...].astype(_F32) * (sg + gf * sg * (1.0 - sg))).astype(dg_ref.dtype)
            du_ref[...] = (daf * (gf * sg)).astype(du_ref.dtype)

    o_spec = pl.BlockSpec((tm, tn), lambda i, j, kk: (i, j))
    out = jax.ShapeDtypeStruct((m, n), _ACT)
    return pl.pallas_call(
        body, name=name, grid=(m // tm, n // tn, nk),
        in_specs=[pl.BlockSpec((tm, tk), lambda i, j, kk: (i, kk)), pl.BlockSpec((tn, tk), lambda i, j, kk: (j, kk)),
                  o_spec, o_spec],
        out_specs=[o_spec, o_spec], out_shape=[out, out],
        scratch_shapes=[pltpu.VMEM((tm, tn), _F32)],
        compiler_params=_params(("parallel", "parallel", "arbitrary")),
    )(dout, w_out, gate, up)


def _final_loss(x, g, target, *, name):
    m, d = x.shape
    tm = _rows(m, 512)

    def body(x_ref, g_ref, t_ref, loss_ref, dx_ref, dg_ref):
        @pl.when(pl.program_id(0) == 0)
        def _():
            loss_ref[...] = jnp.zeros_like(loss_ref)
            dg_ref[...] = jnp.zeros_like(dg_ref)

        xf = x_ref[...]
        r = lax.rsqrt(jnp.mean(xf * xf, axis=-1, keepdims=True) + NORM_EPS)
        xh = xf * r
        err = xh * g_ref[...] - t_ref[...]
        loss_ref[...] += 0.5 * jnp.sum(jnp.mean(err * err, axis=-1, keepdims=True))
        dy = err * (1.0 / d)
        dg_ref[...] += jnp.sum(dy * xh, axis=0, keepdims=True)
        gdy = dy * g_ref[...]
        dx_ref[...] = r * (gdy - xh * jnp.mean(gdy * xh, axis=-1, keepdims=True))

    row = pl.BlockSpec((tm, d), lambda i: (i, 0))
    vec = pl.BlockSpec((1, d), lambda i: (0, 0))
    loss, dx, dg = pl.pallas_call(
        body, name=name, grid=(m // tm,),
        in_specs=[row, vec, row],
        out_specs=[pl.BlockSpec((8, LANES), lambda i: (0, 0)), row, vec],
        out_shape=[jax.ShapeDtypeStruct((8, LANES), _F32), jax.ShapeDtypeStruct((m, d), _F32),
                   jax.ShapeDtypeStruct((1, d), _F32)],
        compiler_params=_params(("arbitrary",)),
    )(x, g.reshape(1, d), target)
    return loss[0, 0], dx, dg.reshape(d)


def _reduce_adamw(parts, w, m, v, *, name):
    rws, cols = w.shape
    tr = rws
    for cand in range(min(rws, 256), 0, -SUBLANES_16BIT):
        if cand % SUBLANES_16BIT == 0 and rws % cand == 0:
            tr = cand
            break

    def body(p_ref, w_ref, m_ref, v_ref, g_ref, d_ref, nm_ref, nv_ref):
        gsum = p_ref[0].astype(_F32)
        for dev in range(1, N_DEV):
            gsum = gsum + p_ref[dev].astype(_F32)
        m2 = ADAM_B1 * m_ref[...] + (1.0 - ADAM_B1) * gsum
        v2 = ADAM_B2 * v_ref[...] + (1.0 - ADAM_B2) * (gsum * gsum)
        m_hat = m2 / (1.0 - ADAM_B1 ** ADAM_STEP)
        v_hat = v2 / (1.0 - ADAM_B2 ** ADAM_STEP)
        g_ref[...] = gsum
        d_ref[...] = -ADAM_LR * (m_hat / (jnp.sqrt(v_hat) + ADAM_EPS) + ADAM_WD * w_ref[...])
        nm_ref[...] = m2
        nv_ref[...] = v2

    blk = pl.BlockSpec((tr, cols), lambda i: (i, 0))
    out = jax.ShapeDtypeStruct((rws, cols), _F32)
    return pl.pallas_call(
        body, name=name, grid=(rws // tr,),
        in_specs=[pl.BlockSpec((N_DEV, tr, cols), lambda i: (0, i, 0)), blk, blk, blk],
        out_specs=[blk, blk, blk, blk], out_shape=[out, out, out, out],
        compiler_params=_params(("parallel",)),
    )(parts, w, m, v)


def _mesh_pos():
    return lax.axis_index("x"), lax.axis_index("y"), lax.axis_index("c")


def _all_gather(block):
    rws, cols = block.shape

    def body(x_ref, out_ref, send_sems, recv_sems, local_sem):
        start, forward, finish = _gather_phases(x_ref, out_ref, send_sems, recv_sems, local_sem)
        start()
        forward()
        finish()

    return pl.pallas_call(
        body, name="weights_all_gather",
        out_shape=jax.ShapeDtypeStruct((N_DEV, rws, cols), block.dtype),
        in_specs=[pl.BlockSpec(memory_space=pl.ANY)],
        out_specs=pl.BlockSpec(memory_space=pl.ANY),
        scratch_shapes=_comm_scratch(),
    )(block)


def _comm_scratch():
    return [pltpu.SemaphoreType.DMA((N_DEV - 1,)), pltpu.SemaphoreType.DMA((N_DEV - 1,)), pltpu.SemaphoreType.DMA]


def _gather_phases(x_ref, out_ref, send_sems, recv_sems, local_sem):
    x, y, c = _mesh_pos()
    me, sibling = (x, y, c), (x, y, 1 - c)
    chips = [(1 - x, y), (x, 1 - y), (1 - x, 1 - y)]

    def slot(px, py, pc):
        return out_ref.at[4 * px + 2 * py + pc]

    def copy(k, blk, to, src=None):
        return pltpu.make_async_remote_copy(
            src_ref=slot(*blk) if src is None else src, dst_ref=slot(*blk),
            send_sem=send_sems.at[k], recv_sem=recv_sems.at[k],
            device_id=to, device_id_type=pl.DeviceIdType.MESH)

    mine = pltpu.make_async_copy(x_ref, slot(*me), local_sem)
    first = [copy(0, me, sibling, src=x_ref)]
    first += [copy(1 + j, me, (*chip, c), src=x_ref) for j, chip in enumerate(chips)]
    passed = [copy(4 + j, (*chip, c), sibling) for j, chip in enumerate(chips)]

    def start():
        mine.start()
        for cp in first:
            cp.start()

    def forward():
        for j, chip in enumerate(chips):
            copy(1 + j, (*chip, c), me).wait_recv()
            passed[j].start()

    def finish():
        copy(0, sibling, me).wait_recv()
        for j, chip in enumerate(chips):
            copy(4 + j, (*chip, 1 - c), me).wait_recv()
        for cp in first + passed:
            cp.wait_send()
        mine.wait()

    return start, forward, finish


def _exchange(parts):
    def body(g_ref, out_ref, send_sems, recv_sems, local_sem):
        start, finish = _exchange_phases(g_ref, out_ref, send_sems, recv_sems, local_sem)
        start()
        finish()

    return pl.pallas_call(
        body, name="grads_exchange",
        out_shape=jax.ShapeDtypeStruct(parts.shape, parts.dtype),
        in_specs=[pl.BlockSpec(memory_space=pl.ANY)],
        out_specs=pl.BlockSpec(memory_space=pl.ANY),
        scratch_shapes=_comm_scratch(),
    )(parts)


def _exchange_phases(g_ref, out_ref, send_sems, recv_sems, local_sem):
    x, y, c = _mesh_pos()
    me = 4 * x + 2 * y + c
    mine = pltpu.make_async_copy(g_ref.at[me], out_ref.at[me], local_sem)
    sends, arrivals = [], []
    for k in range(1, N_DEV):
        px = 1 - x if k & 4 else x
        py = 1 - y if k & 2 else y
        pc = 1 - c if k & 1 else c
        peer = 4 * px + 2 * py + pc
        sems = dict(send_sem=send_sems.at[k - 1], recv_sem=recv_sems.at[k - 1],
                    device_id=(px, py, pc), device_id_type=pl.DeviceIdType.MESH)
        sends.append(pltpu.make_async_remote_copy(src_ref=g_ref.at[peer], dst_ref=out_ref.at[me], **sems))
        arrivals.append(pltpu.make_async_remote_copy(src_ref=g_ref.at[peer], dst_ref=out_ref.at[peer], **sems))

    def start():
        mine.start()
        for cp in sends:
            cp.start()

    def finish():
        for cp in arrivals:
            cp.wait_recv()
        for cp in sends:
            cp.wait_send()
        mine.wait()

    return start, finish


_BIG = (("w_in_ab", 2), ("mla_w_uq", 1), ("mla_w_ukv", 1), ("w_out_ab", 2), ("gqa_w_q", 1), ("gqa_w_kv", 1),
        ("gqa_w_o", 1), ("ffn_w_in", 1), ("ffn_w_out", 1))
_TRANSPOSED = ("ffn_w_in",)


def _stored(arrays):
    return {n: (a.transpose(0, 2, 1) if n in _TRANSPOSED else a) for n, a in arrays.items()}
_SMALL = ("mix_norm_ab", "ffn_norm", "final_norm", "mix_norm_c", "mla_q_norm", "mla_kv_norm", "gqa_q_norm", "gqa_k_norm")
SMALL_ROWS = 16
GATHER_TAIL_ROWS = 16
EXCHANGE_TAIL_ROWS = 32


def _view3(a):
    return a.reshape(a.shape[0], a.shape[1], -1)


def _pad_rows(a2d):
    pad = -a2d.shape[0] % SUBLANES_16BIT
    return jnp.pad(a2d, ((0, pad), (0, 0))) if pad else a2d


def _pack_rows(shard):
    return _pad_rows(shard.reshape(-1, PACK_COLS))


def _packed_rows(shape):
    n = 1
    for d in shape:
        n *= d
    rows = n // PACK_COLS
    return rows + (-rows % SUBLANES_16BIT)


_FIRST = {"w_in_ab": (0, 1), "mla_w_uq": (0, 1), "mla_w_ukv": (0, 1)}
_REST = {"w_in_ab": (1, 1), "mla_w_uq": (1, 1), "mla_w_ukv": (1, 1), "w_out_ab": (0, 2), "gqa_w_q": (0, 2),
         "gqa_w_kv": (0, 2), "gqa_w_o": (0, 2), "ffn_w_in": (0, 4), "ffn_w_out": (0, 4)}
PACK_ROWS = 128


def _group_of(arrays, group):
    return {n: arrays[n][lo:lo + cnt] for n, (lo, cnt) in group.items()}


def _pack_big(shards):
    parts = [_pack_rows(shards[n]) for n, _ in _BIG if n in shards]
    fill = -sum(p.shape[0] for p in parts) % PACK_ROWS
    return jnp.concatenate(parts + [jnp.zeros((fill, PACK_COLS), parts[0].dtype)], axis=0)


def _unpack_big(packed, shapes):
    out, off = {}, 0
    for n, _ in _BIG:
        if n not in shapes:
            continue
        size = 1
        for d in shapes[n]:
            size *= d
        out[n] = packed[off:off + size // PACK_COLS].reshape(shapes[n])
        off += _packed_rows(shapes[n])
    return out


def _unpack_gathered(gathered, shapes):
    out, off = {}, 0
    for n, axis in _BIG:
        if n not in shapes:
            continue
        size = 1
        for d in shapes[n]:
            size *= d
        l3 = (shapes[n][0], shapes[n][1], size // (shapes[n][0] * shapes[n][1]))
        sh = gathered[:, off:off + size // PACK_COLS].reshape((N_DEV,) + l3)
        if axis == 1:
            full = sh.transpose(1, 0, 2, 3).reshape(l3[0], N_DEV * l3[1], l3[2])
        else:
            full = sh.transpose(1, 2, 0, 3).reshape(l3[0], l3[1], N_DEV * l3[2])
        out[n] = full
        off += _packed_rows(shapes[n])
    return out


def _split_for_devices(full, axis):
    l, rws, cols = full.shape
    if axis == 1:
        sh = full.reshape(l, N_DEV, rws // N_DEV, cols).transpose(1, 0, 2, 3)
    else:
        sh = full.reshape(l, rws, N_DEV, cols // N_DEV).transpose(2, 0, 1, 3)
    flat = sh.reshape(N_DEV, -1, PACK_COLS)
    pad = -flat.shape[1] % SUBLANES_16BIT
    return jnp.pad(flat, ((0, 0), (0, pad), (0, 0))) if pad else flat


def _to_bits16(a_f32_rows):
    r = a_f32_rows.shape[0]
    return lax.bitcast_convert_type(a_f32_rows, jnp.bfloat16).reshape(2 * r, PACK_COLS)


def _from_bits16(a_bf16_rows):
    lead, r = a_bf16_rows.shape[:-2], a_bf16_rows.shape[-2]
    return lax.bitcast_convert_type(a_bf16_rows.reshape(lead + (r // 2, PACK_COLS, 2)), _F32)


def _small_sizes():
    return {"mix_norm_ab": 2 * D_MODEL, "ffn_norm": DEPTH * D_MODEL, "final_norm": D_MODEL, "mix_norm_c": 2 * D_MODEL,
            "mla_q_norm": 2 * MLA_Q_RANK, "mla_kv_norm": 2 * MLA_KV_RANK, "gqa_q_norm": 2 * GQA_HEAD_DIM,
            "gqa_k_norm": 2 * GQA_HEAD_DIM}


def _pack_small(vals):
    flat = jnp.concatenate([vals[n].reshape(-1).astype(_F32) for n in _SMALL])
    return jnp.pad(flat, (0, SMALL_ROWS * PACK_COLS - flat.shape[0])).reshape(SMALL_ROWS, PACK_COLS)


def _unpack_small(pack):
    flat, out, off = pack.reshape(-1), {}, 0
    sizes = _small_sizes()
    for n in _SMALL:
        out[n] = flat[off:off + sizes[n]]
        off += sizes[n]
    return out


def _angles(pos, dim):
    freqs = ROPE_THETA ** (-jnp.arange(0, dim, 2, dtype=_F32) / dim)
    ang = pos.astype(_F32)[:, None] * freqs[None, :]
    return jnp.cos(ang), jnp.sin(ang)


def _rope_tables(s):
    pos = jnp.arange(s)
    cos_t, sin_t = _angles(pos, MLA_ROPE)
    mla_c = jnp.tile(jnp.concatenate([cos_t, cos_t], -1), (1, LANES // 32))
    mla_s = jnp.tile(jnp.concatenate([-sin_t, sin_t], -1), (1, LANES // 32))
    rows = s // GRID_W
    row_idx = jnp.broadcast_to(jnp.arange(rows)[:, None], (rows, GRID_W)).reshape(-1)
    col_idx = jnp.broadcast_to(jnp.arange(GRID_W)[None, :], (rows, GRID_W)).reshape(-1)
    cos_r, sin_r = _angles(row_idx, GQA_HEAD_DIM // 2)
    cos_c, sin_c = _angles(col_idx, GQA_HEAD_DIM // 2)
    gqa_c = jnp.tile(jnp.concatenate([cos_r, cos_r, cos_c, cos_c], -1), (1, LANES // GQA_HEAD_DIM))
    gqa_s = jnp.tile(jnp.concatenate([-sin_r, sin_r, -sin_c, sin_c], -1), (1, LANES // GQA_HEAD_DIM))
    return (mla_c, mla_s), (gqa_c, gqa_s)


def _heads(x2d, h):
    s = x2d.shape[0]
    return x2d.reshape(s, h, -1).transpose(1, 0, 2)


def _unheads(xh):
    h, s, d = xh.shape
    return xh.transpose(1, 0, 2).reshape(s, h * d)


def _to_res(x2d, dil):
    s, cols = x2d.shape
    if dil > 1:
        x2d = x2d.reshape(s // dil, dil, cols).transpose(1, 0, 2).reshape(s, cols)
    return x2d.reshape(s, cols // DIL_HEAD_DIM, DIL_HEAD_DIM).transpose(1, 0, 2)


def _from_res(xh, dil):
    heads, s, d = xh.shape
    x2d = xh.transpose(1, 0, 2).reshape(s, heads * d)
    if dil > 1:
        x2d = x2d.reshape(dil, s // dil, heads * d).transpose(1, 0, 2).reshape(s, heads * d)
    return x2d


def _group_major(w_b):
    rws = w_b.shape[0]
    return w_b.reshape(rws, 3, DIL_GROUPS, DIL_SLOTS * DIL_HEAD_DIM).transpose(0, 2, 1, 3).reshape(rws, -1)


def _res_to_tok(xh, dil):
    sl, s, c = xh.shape
    return xh.reshape(sl, dil, s // dil, c).transpose(0, 2, 1, 3).reshape(sl, s, c)


def _tok_to_res(xh, dil):
    sl, s, c = xh.shape
    return xh.reshape(sl, s // dil, dil, c).transpose(0, 2, 1, 3).reshape(sl, s, c)


def _ffn_fwd(x, w, tag):
    hn = _rmsnorm(x, w["norm"], out_dtype=_ACT, name=f"{tag}_norm")
    gate, up, act = _gate_up(hn, w["w_in_t"], name=f"{tag}_gate_up")
    out = _matmul(act, w["w_out"], res=x, name=f"{tag}_out")
    return out, (x, hn, gate, up, act)


def _ffn_bwd(dout, saved, w, tag):
    x, hn, gate, up, act = saved
    w_gate_t, w_up_t = w["w_in_t"][:FFN_HIDDEN], w["w_in_t"][FFN_HIDDEN:]
    d_w_out = _matmul(act, dout, trans_a=True, name=f"{tag}_dwout")
    dgate, dup = _gate_up_bwd(dout, w["w_out"], gate, up, name=f"{tag}_dgate_up")
    d_w_gate_t = _matmul(dgate, hn, trans_a=True, name=f"{tag}_dwgate")
    d_w_up_t = _matmul(dup, hn, trans_a=True, name=f"{tag}_dwup")
    dhn = _matmul(dgate, w_gate_t, name=f"{tag}_dhn_gate")
    dhn = _matmul(dup, w_up_t, res=dhn, name=f"{tag}_dhn_up")
    dx, dnorm = _rmsnorm_bwd(x, w["norm"], dhn, dout, name=f"{tag}_dnorm")
    return dx, {"norm": dnorm, "w_in_t": jnp.concatenate([d_w_gate_t, d_w_up_t], axis=0), "w_out": d_w_out}


def _even_fwd(x, w, tabs, slopes, tag, gather=None, late=None):
    s = x.shape[0]
    (mla_c, mla_s), _ = tabs
    hn = _rmsnorm(x, w["norm"], out_dtype=_ACT, name=f"{tag}_norm")
    za = _matmul(hn, w["w_a"], name=f"{tag}_in_a")
    zb = _matmul(hn, w["w_b"], out_dtype=_ACT, name=f"{tag}_in_b")
    cq, ckv, kr = za[:, :MLA_Q_RANK], za[:, MLA_Q_RANK:MLA_Q_RANK + MLA_KV_RANK], za[:, MLA_Q_RANK + MLA_KV_RANK:]
    cqn = _rmsnorm(cq, w["q_norm"], out_dtype=_ACT, name=f"{tag}_qnorm")
    ckvn = _rmsnorm(ckv, w["kv_norm"], out_dtype=_ACT, name=f"{tag}_kvnorm")
    q = _matmul(cqn, w["w_uq"], scale=(MLA_NOPE + MLA_ROPE) ** -0.5 * LOG2_E,
                name=f"{tag}_uq")
    kv = _matmul(ckvn, w["w_ukv"], out_dtype=_ACT, name=f"{tag}_ukv")
    nn = MLA_HEADS * MLA_NOPE
    q_rope = _rope(q[:, nn:], mla_c, mla_s, out_dtype=_ACT, name=f"{tag}_rope_q")
    k_rope = _rope(kr, mla_c, mla_s, out_dtype=_ACT, name=f"{tag}_rope_k")[:, :MLA_ROPE]
    qh = jnp.concatenate([_heads_t(q[:, :nn].astype(_ACT), MLA_HEADS), _heads_t(q_rope, MLA_HEADS)], axis=1)
    kh = jnp.concatenate([_heads(kv[:, :nn], MLA_HEADS),
                          jnp.broadcast_to(k_rope[None], (MLA_HEADS, s, MLA_ROPE))], axis=-1)
    vh = _heads(kv[:, nn:], MLA_HEADS)
    oa, lse_a, *gathered = _attn_fwd(qh, kh, _chunk_t(kv[:, nn:], MLA_HEADS), out_dtype=_ACT, name=f"{tag}_mla",
                                     gather=gather)
    if late is not None:
        w = late(gathered[0])

    ng = 3 * DIL_SLOTS * DIL_HEAD_DIM
    outs, lses, dil_saved = [], [], []
    for gi, (_, dil) in enumerate(DIL_PAIRS):
        hs = slice(gi * DIL_SLOTS, (gi + 1) * DIL_SLOTS)
        qkv = _to_res(zb[:, gi * ng:(gi + 1) * ng], dil)
        qr, kr_, vr = qkv[:DIL_SLOTS], qkv[DIL_SLOTS:2 * DIL_SLOTS], qkv[2 * DIL_SLOTS:]
        o, l = _win_fwd(qr, kr_, vr, slopes[hs], dil=dil, name=f"{tag}_dil{gi}")
        dil_saved.append((qr, kr_, vr, l))
        outs.append(_res_to_tok(o, dil))
        lses.append(_res_to_tok(l, dil))
    o3, lse3 = tuple(outs), tuple(lses)
    comb = _merge_fwd(o3, lse3, name=f"{tag}_merge")
    cat = jnp.concatenate([_unheads_t(oa), _unheads(comb)], axis=-1)
    out = _matmul(cat, w["w_out"], res=x, name=f"{tag}_out")
    saved = (x, hn, cq, ckv, cqn, ckvn, qh, kh, vh, oa, lse_a, dil_saved, o3, lse3, cat)
    return out, saved, (gathered[0] if gathered else None)


def _even_bwd(dout, saved, w, tabs, slopes, tag, exchange=None):
    x, hn, cq, ckv, cqn, ckvn, qh, kh, vh, oa, lse_a, dil_saved, o3, lse3, cat = saved
    (mla_c, mla_s), _ = tabs
    nn = MLA_HEADS * MLA_NOPE
    dcat = _matmul(dout, w["w_out"], trans_b=True, out_dtype=_ACT, name=f"{tag}_dcat")
    d_w_out = _matmul(cat, dout, trans_a=True, name=f"{tag}_dwout")
    nv = MLA_HEADS * MLA_V

    doa = _heads(dcat[:, :nv], MLA_HEADS)
    dqh, dkt, dvt, *received = _attn_bwd(qh, kh, vh, oa, doa, lse_a, scale=(MLA_NOPE + MLA_ROPE) ** -0.5,
                                         name=f"{tag}_mla_bwd",
                                         exchange=None if exchange is None else exchange(d_w_out))
    dq_rope = _rope(_unheads(dqh[..., MLA_NOPE:]), mla_c, -mla_s, out_dtype=_ACT, name=f"{tag}_drope_q")
    dq = jnp.concatenate([_unheads(dqh[..., :MLA_NOPE]).astype(_ACT), dq_rope], axis=-1)
    dk3 = _unchunk(dkt).reshape(-1, MLA_HEADS, MLA_NOPE + MLA_ROPE)
    dkr = _rope(dk3[..., MLA_NOPE:].reshape(-1, MLA_HEADS * MLA_ROPE), mla_c, -mla_s, out_dtype=_F32,
                sum_chunks=True, name=f"{tag}_drope_k")
    dkv = jnp.concatenate([dk3[..., :MLA_NOPE].reshape(-1, nn), _unchunk(dvt)], axis=-1).astype(_ACT)
    d_w_uq = _matmul(cqn, dq, trans_a=True, name=f"{tag}_dwuq")
    d_w_ukv = _matmul(ckvn, dkv, trans_a=True, name=f"{tag}_dwukv")
    dcqn = _matmul(dq, w["w_uq"], trans_b=True, name=f"{tag}_dcqn")
    dckvn = _matmul(dkv, w["w_ukv"], trans_b=True, name=f"{tag}_dckvn")
    dcq, d_q_norm = _rmsnorm_bwd(cq, w["q_norm"], dcqn, out_dtype=_ACT, name=f"{tag}_dqnorm")
    dckv, d_kv_norm = _rmsnorm_bwd(ckv, w["kv_norm"], dckvn, out_dtype=_ACT, name=f"{tag}_dkvnorm")
    lane = jnp.arange(LANES) < MLA_ROPE
    dza = jnp.concatenate([dcq, dckv, jnp.where(lane[None], dkr, 0.0).astype(_ACT)], axis=-1)

    dcomb = _heads(dcat[:, nv:], DIL_SLOTS)
    do3, delta3 = _merge_bwd(dcomb, o3, lse3, name=f"{tag}_dmerge")
    dslabs = []
    for gi, (_, dil) in enumerate(DIL_PAIRS):
        hs = slice(gi * DIL_SLOTS, (gi + 1) * DIL_SLOTS)
        qr, kr_, vr, l = dil_saved[gi]
        grads = (qr, kr_, vr, _tok_to_res(do3[gi], dil), l, _tok_to_res(delta3[gi], dil), slopes[hs])
        a = _win_bwd_dq(*grads, dil=dil, name=f"{tag}_dil{gi}_dq")
        b, c = _win_bwd_dkv(*grads, dil=dil, name=f"{tag}_dil{gi}_dkv")
        dslabs.append(_from_res(jnp.concatenate([a, b, c], axis=0), dil))
    dzb = jnp.concatenate(dslabs, axis=-1)

    d_w_a = _matmul(hn, dza, trans_a=True, name=f"{tag}_dwa")
    d_w_b = _matmul(hn, dzb, trans_a=True, name=f"{tag}_dwb")
    dhn = _matmul(dza, w["w_a"], trans_b=True, name=f"{tag}_dhn_a")
    dhn = _matmul(dzb, w["w_b"], trans_b=True, res=dhn, name=f"{tag}_dhn_b")
    dx, dnorm = _rmsnorm_bwd(x, w["norm"], dhn, dout, name=f"{tag}_dnorm")
    grads = {"norm": dnorm, "w_a": d_w_a, "w_b": d_w_b, "q_norm": d_q_norm, "kv_norm": d_kv_norm,
             "w_uq": d_w_uq, "w_ukv": d_w_ukv, "w_out": d_w_out}
    return dx, grads, (received[0] if received else None)


def _odd_fwd(x, w, tabs, tag):
    s = x.shape[0]
    _, (gqa_c, gqa_s) = tabs
    nk = GQA_KV_HEADS * GQA_HEAD_DIM
    hn = _rmsnorm(x, w["norm"], out_dtype=_ACT, name=f"{tag}_norm")
    q = _matmul(hn, w["w_q"], name=f"{tag}_q")
    kv = _matmul(hn, w["w_kv"], name=f"{tag}_kv")
    k = kv[:, :nk]
    qh = _heads_t(_headnorm_rope(q, w["q_norm"], gqa_c, gqa_s, scale=GQA_HEAD_DIM ** -0.5 * LOG2_E,
                                 name=f"{tag}_prep_q"), GQA_HEADS)
    kh = _heads(_headnorm_rope(k, w["k_norm"], gqa_c, gqa_s, name=f"{tag}_prep_k"), GQA_KV_HEADS)
    v = kv[:, nk:].astype(_ACT)
    vh = _heads(v, GQA_KV_HEADS)
    o, lse = _attn_fwd(qh, kh, _chunk_t(v, GQA_KV_HEADS), out_dtype=_ACT, name=f"{tag}_gqa")
    ocat = _unheads_t(o)
    out = _matmul(ocat, w["w_o"], res=x, name=f"{tag}_out")
    return out, (x, hn, q, k, qh, kh, vh, o, lse, ocat)


def _odd_bwd(dout, saved, w, tabs, tag):
    x, hn, q, k, qh, kh, vh, o, lse, ocat = saved
    s = x.shape[0]
    _, (gqa_c, gqa_s) = tabs
    docat = _matmul(dout, w["w_o"], trans_b=True, out_dtype=_ACT, name=f"{tag}_docat")
    d_w_o = _matmul(ocat, dout, trans_a=True, name=f"{tag}_dwo")
    doh = _heads(docat, GQA_HEADS)
    dqh, dkt, dvt = _attn_bwd(qh, kh, vh, o, doh, lse, scale=GQA_HEAD_DIM ** -0.5, name=f"{tag}_gqa_bwd")
    dq, d_q_norm = _headnorm_rope_bwd(q, w["q_norm"], _unheads(dqh), gqa_c, gqa_s, name=f"{tag}_dprep_q")
    dk, d_k_norm = _headnorm_rope_bwd(k, w["k_norm"], _unchunk(dkt), gqa_c, gqa_s, name=f"{tag}_dprep_k")
    dkv = jnp.concatenate([dk, _unchunk(dvt).astype(_ACT)], axis=-1)
    d_w_q = _matmul(hn, dq, trans_a=True, name=f"{tag}_dwq")
    d_w_kv = _matmul(hn, dkv, trans_a=True, name=f"{tag}_dwkv")
    dhn = _matmul(dq, w["w_q"], trans_b=True, name=f"{tag}_dhn_q")
    dhn = _matmul(dkv, w["w_kv"], trans_b=True, res=dhn, name=f"{tag}_dhn_kv")
    dx, dnorm = _rmsnorm_bwd(x, w["norm"], dhn, dout, name=f"{tag}_dnorm")
    return dx, {"norm": dnorm, "w_q": d_w_q, "w_kv": d_w_kv, "q_norm": d_q_norm, "k_norm": d_k_norm, "w_o": d_w_o}


def _split_heads_cols(wm, heads, first):
    rws = wm.shape[0]
    w3 = wm.reshape(rws, heads, -1)
    return jnp.concatenate([w3[:, :, :first].reshape(rws, -1), w3[:, :, first:].reshape(rws, -1)], axis=-1)


def _merge_heads_cols(wm, heads, first):
    rws, cols = wm.shape
    a = wm[:, :heads * first].reshape(rws, heads, first)
    b = wm[:, heads * first:].reshape(rws, heads, cols // heads - first)
    return jnp.concatenate([a, b], axis=-1).reshape(rws, cols)


def _layer_weights(full, gains, layer):
    i = layer // 2

    def stacked(name, idx):
        for group, arrays in zip((_FIRST, _REST), full):
            lo, cnt = group.get(name, (0, 0))
            if lo <= idx < lo + cnt:
                return None if arrays is None else arrays[name][idx - lo]
        raise KeyError((name, idx))

    ffn = {"norm": gains["ffn_norm"][layer], "w_in_t": stacked("ffn_w_in", layer), "w_out": stacked("ffn_w_out", layer)}
    if layer % 2 == 0:
        w_in = stacked("w_in_ab", i)
        mix = {"norm": gains["mix_norm_ab"][i],
               "w_a": jnp.pad(w_in[:, :IN_A], ((0, 0), (0, IN_A_PAD - IN_A))), "w_b": _group_major(w_in[:, IN_A:]),
               "q_norm": gains["mla_q_norm"][i], "kv_norm": gains["mla_kv_norm"][i],
               "w_uq": _split_heads_cols(stacked("mla_w_uq", i), MLA_HEADS, MLA_NOPE),
               "w_ukv": _split_heads_cols(stacked("mla_w_ukv", i), MLA_HEADS, MLA_NOPE),
               "w_out": stacked("w_out_ab", i)}
    else:
        mix = {"norm": gains["mix_norm_c"][i], "w_q": stacked("gqa_w_q", i), "w_kv": stacked("gqa_w_kv", i),
               "q_norm": gains["gqa_q_norm"][i], "k_norm": gains["gqa_k_norm"][i], "w_o": stacked("gqa_w_o", i)}
    return mix, ffn


def _pack_grads(grads, group):
    parts = [_split_for_devices(jnp.stack(grads[n][group[n][0]:group[n][0] + group[n][1]]), axis)
             for n, axis in _BIG if n in group]
    fill = -sum(p.shape[1] for p in parts) % PACK_ROWS
    return jnp.concatenate(parts + [jnp.zeros((N_DEV, fill, PACK_COLS), parts[0].dtype)], axis=1)


def kernel(x, mix_norm_ab, w_in_ab, mla_q_norm, mla_kv_norm, mla_w_uq, mla_w_ukv, w_out_ab, mix_norm_c, gqa_w_q, gqa_w_kv, gqa_q_norm, gqa_k_norm, gqa_w_o, ffn_norm, ffn_w_in, ffn_w_out, final_norm, loss_target, m_mix_norm_ab, m_w_in_ab, m_mla_q_norm, m_mla_kv_norm, m_mla_w_uq, m_mla_w_ukv, m_w_out_ab, m_mix_norm_c, m_gqa_w_q, m_gqa_w_kv, m_gqa_q_norm, m_gqa_k_norm, m_gqa_w_o, m_ffn_norm, m_ffn_w_in, m_ffn_w_out, m_final_norm, v_mix_norm_ab, v_w_in_ab, v_mla_q_norm, v_mla_kv_norm, v_mla_w_uq, v_mla_w_ukv, v_w_out_ab, v_mix_norm_c, v_gqa_w_q, v_gqa_w_kv, v_gqa_q_norm, v_gqa_k_norm, v_gqa_w_o, v_ffn_norm, v_ffn_w_in, v_ffn_w_out, v_final_norm):
    wts = dict(mix_norm_ab=mix_norm_ab, w_in_ab=w_in_ab, mla_q_norm=mla_q_norm, mla_kv_norm=mla_kv_norm,
               mla_w_uq=mla_w_uq, mla_w_ukv=mla_w_ukv, w_out_ab=w_out_ab, mix_norm_c=mix_norm_c, gqa_w_q=gqa_w_q,
               gqa_w_kv=gqa_w_kv, gqa_q_norm=gqa_q_norm, gqa_k_norm=gqa_k_norm, gqa_w_o=gqa_w_o, ffn_norm=ffn_norm,
               ffn_w_in=ffn_w_in, ffn_w_out=ffn_w_out, final_norm=final_norm)
    mom = dict(mix_norm_ab=m_mix_norm_ab, w_in_ab=m_w_in_ab, mla_q_norm=m_mla_q_norm, mla_kv_norm=m_mla_kv_norm,
               mla_w_uq=m_mla_w_uq, mla_w_ukv=m_mla_w_ukv, w_out_ab=m_w_out_ab, mix_norm_c=m_mix_norm_c,
               gqa_w_q=m_gqa_w_q, gqa_w_kv=m_gqa_w_kv, gqa_q_norm=m_gqa_q_norm, gqa_k_norm=m_gqa_k_norm,
               gqa_w_o=m_gqa_w_o, ffn_norm=m_ffn_norm, ffn_w_in=m_ffn_w_in, ffn_w_out=m_ffn_w_out,
               final_norm=m_final_norm)
    var = dict(mix_norm_ab=v_mix_norm_ab, w_in_ab=v_w_in_ab, mla_q_norm=v_mla_q_norm, mla_kv_norm=v_mla_kv_norm,
               mla_w_uq=v_mla_w_uq, mla_w_ukv=v_mla_w_ukv, w_out_ab=v_w_out_ab, mix_norm_c=v_mix_norm_c,
               gqa_w_q=v_gqa_w_q, gqa_w_kv=v_gqa_w_kv, gqa_q_norm=v_gqa_q_norm, gqa_k_norm=v_gqa_k_norm,
               gqa_w_o=v_gqa_w_o, ffn_norm=v_ffn_norm, ffn_w_in=v_ffn_w_in, ffn_w_out=v_ffn_w_out,
               final_norm=v_final_norm)
    big_names = [n for n, _ in _BIG]
    groups = (_FIRST, _REST)
    big_w, big_m, big_v = (_stored({n: src[n] for n in big_names}) for src in (wts, mom, var))
    w_grp = [_group_of(big_w, grp) for grp in groups]
    shapes = [{n: a.shape for n, a in w.items()} for w in w_grp]
    w_packs = [_pack_big(w) for w in w_grp]
    n_rows = [p.shape[0] for p in w_packs]
    xs = x[0]
    s = xs.shape[0]
    me = 4 * lax.axis_index("x") + 2 * lax.axis_index("y") + lax.axis_index("c")
    c_cols = mix_norm_c.shape[1]

    tail = jnp.pad(mix_norm_c.reshape(-1), (0, GATHER_TAIL_ROWS // 2 * PACK_COLS - mix_norm_c.size))
    tail = _to_bits16(tail.reshape(GATHER_TAIL_ROWS // 2, PACK_COLS))
    gathered = _all_gather(jnp.concatenate([w_packs[0].astype(jnp.bfloat16), tail], axis=0))
    full = [_unpack_gathered(gathered[:, :n_rows[0]], shapes[0]), None]
    c_all = _from_bits16(gathered[:, n_rows[0]:]).reshape(N_DEV, -1)[:, :mix_norm_c.size]
    c_full = c_all.reshape(N_DEV, 2, c_cols).transpose(1, 0, 2).reshape(2, N_DEV * c_cols)
    gains = dict(mix_norm_ab=mix_norm_ab, mla_q_norm=mla_q_norm, mla_kv_norm=mla_kv_norm, mix_norm_c=c_full,
                 gqa_q_norm=gqa_q_norm, gqa_k_norm=gqa_k_norm, ffn_norm=ffn_norm)

    tabs = _rope_tables(s)
    slopes = jnp.exp2(-8.0 * jnp.arange(1, DIL_HEADS + 1, dtype=_F32) / DIL_HEADS)

    h = xs
    saved = []
    for layer in range(DEPTH):
        mix_w, ffn_w = _layer_weights(full, gains, layer)
        if layer == 0:
            def late(rest):
                full[1] = _unpack_gathered(rest, shapes[1])
                return _layer_weights(full, gains, 0)[0]

            h, sv_mix, _ = _even_fwd(h, mix_w, tabs, slopes, f"l{layer}_mix", gather=w_packs[1].astype(jnp.bfloat16),
                                     late=late)
            mix_w, ffn_w = _layer_weights(full, gains, 0)
        elif layer % 2 == 0:
            h, sv_mix, _ = _even_fwd(h, mix_w, tabs, slopes, f"l{layer}_mix")
        else:
            h, sv_mix = _odd_fwd(h, mix_w, tabs, f"l{layer}_mix")
        h, sv_ffn = _ffn_fwd(h, ffn_w, f"l{layer}_ffn")
        saved.append((mix_w, ffn_w, sv_mix, sv_ffn))
    loss_local, dh, d_final = _final_loss(h, final_norm, loss_target[0], name="final_loss")

    gfull = {n: [None] * wts[n].shape[0] for n in big_names}
    gsmall = {n: [None] * (wts[n].shape[0] if wts[n].ndim > 1 else 1) for n in _SMALL}
    gsmall["final_norm"][0] = d_final
    for layer in reversed(range(DEPTH)):
        mix_w, ffn_w, sv_mix, sv_ffn = saved[layer]
        i = layer // 2
        dh, gf = _ffn_bwd(dh, sv_ffn, ffn_w, f"l{layer}_ffn")
        gsmall["ffn_norm"][layer] = gf["norm"]
        gfull["ffn_w_in"][layer] = gf["w_in_t"]
        gfull["ffn_w_out"][layer] = gf["w_out"]
        if layer % 2 == 0:
            def sending(d_w_out):
                gfull["w_out_ab"][0] = d_w_out
                return _pack_grads(gfull, _REST).astype(jnp.bfloat16)

            dh, gm, got = _even_bwd(dh, sv_mix, mix_w, tabs, slopes, f"l{layer}_mix",
                                    exchange=sending if layer == 0 else None)
            if layer == 0:
                received_rest = got
            gsmall["mix_norm_ab"][i] = gm["norm"]
            gsmall["mla_q_norm"][i] = gm["q_norm"]
            gsmall["mla_kv_norm"][i] = gm["kv_norm"]
            gfull["w_in_ab"][i] = jnp.concatenate([gm["w_a"][:, :IN_A], _group_major(gm["w_b"])], axis=-1)
            gfull["mla_w_uq"][i] = _merge_heads_cols(gm["w_uq"], MLA_HEADS, MLA_NOPE)
            gfull["mla_w_ukv"][i] = _merge_heads_cols(gm["w_ukv"], MLA_HEADS, MLA_NOPE)
            gfull["w_out_ab"][i] = gm["w_out"]
        else:
            dh, gm = _odd_bwd(dh, sv_mix, mix_w, tabs, f"l{layer}_mix")
            gsmall["mix_norm_c"][i] = gm["norm"]
            gsmall["gqa_q_norm"][i] = gm["q_norm"]
            gsmall["gqa_k_norm"][i] = gm["k_norm"]
            gfull["gqa_w_q"][i] = gm["w_q"]
            gfull["gqa_w_kv"][i] = gm["w_kv"]
            gfull["gqa_w_o"][i] = gm["w_o"]
    grad_x = dh[None]

    small_part = _pack_small({n: jnp.stack(gsmall[n]) for n in _SMALL})
    small_bits = jnp.broadcast_to(_to_bits16(small_part)[None], (N_DEV, EXCHANGE_TAIL_ROWS, PACK_COLS))
    received_first = _exchange(jnp.concatenate([_pack_grads(gfull, _FIRST).astype(jnp.bfloat16), small_bits], axis=1))
    big_packs = [_reduce_adamw(got, w_packs[gi], _pack_big(_group_of(big_m, grp)), _pack_big(_group_of(big_v, grp)),
                               name=f"adamw_big{gi}")
                 for gi, (grp, got) in enumerate(zip(groups, (received_first, received_rest)))]

    def widen_c(shard):
        return lax.dynamic_update_slice(jnp.zeros((2, N_DEV * c_cols), _F32), shard, (0, me * c_cols))

    def small_of(src):
        return _pack_small({n: (widen_c(src[n]) if n == "mix_norm_c" else src[n]) for n in _SMALL})

    small_recv = _from_bits16(received_first[:, n_rows[0]:])
    small_packs = _reduce_adamw(small_recv, small_of(wts), small_of(mom), small_of(var), name="adamw_small")

    def outputs_of(which):
        by_group = [_unpack_big(big_packs[gi][which], shapes[gi]) for gi in range(len(groups))]
        small = _unpack_small(small_packs[which])
        res = _stored({n: jnp.concatenate([grp[n] for grp in by_group if n in grp], axis=0) for n in big_names})
        for n in wts:
            if n in big_names:
                continue
            if n == "mix_norm_c":
                res[n] = lax.dynamic_slice(small[n].reshape(2, N_DEV * c_cols), (0, me * c_cols), (2, c_cols))
            else:
                res[n] = small[n].reshape(wts[n].shape)
        return [res[n] for n in wts]

    loss = lax.psum(loss_local, _AXES)
    return (loss, grad_x, *outputs_of(0), *outputs_of(1), *outputs_of(2), *outputs_of(3))
```

```python
import functools

import jax
import jax.numpy as jnp
from jax import lax
from jax.experimental import pallas as pl
from jax.experimental.pallas import tpu as pltpu

D_MODEL = 1024
DEPTH = 4
GRID_W = 64
NORM_EPS = 1e-6
ROPE_THETA = 10000.0
NEG_INF = -1e30
MLA_HEADS = 8
MLA_Q_RANK = 384
MLA_KV_RANK = 256
MLA_NOPE = 64
MLA_ROPE = 32
MLA_V = 64
DIL_PAIRS = ((128, 1), (512, 4), (2048, 16))
DIL_HALF = 64
DIL_SLOTS = 4
DIL_GROUPS = 3
DIL_HEADS = 12
DIL_HEAD_DIM = 64
GQA_HEADS = 16
GQA_KV_HEADS = 4
GQA_HEAD_DIM = 64
FFN_HIDDEN = 2816
IN_A = MLA_Q_RANK + MLA_KV_RANK + MLA_ROPE
IN_A_PAD = 768
IN_B = 3 * DIL_HEADS * DIL_HEAD_DIM
ADAM_LR = 0.001
ADAM_B1 = 0.9
ADAM_B2 = 0.999
ADAM_EPS = 1e-08
ADAM_WD = 0.01
ADAM_STEP = 10

LANES = 128
SUBLANES_16BIT = 16
VMEM_LIMIT_BYTES = 56 * 1024 * 1024

MM_TILE = 1408

N_DEV = 8
PACK_COLS = 1024

_MXU = jnp.bfloat16
_ACT = jnp.bfloat16
_F32 = jnp.float32

_AXES = ("x", "y", "c")


def _params(sem):
    return pltpu.CompilerParams(dimension_semantics=sem, vmem_limit_bytes=VMEM_LIMIT_BYTES)


def _pick(n, cap):
    for t in range(cap - cap % LANES, 0, -LANES):
        if n % t == 0:
            return t
    return n


def _rows(m, target):
    t = m
    while t > target and t % 2 == 0:
        t //= 2
    return t


def _matmul(a, b, *, trans_a=False, trans_b=False, res=None, scale=None, out_dtype=_F32, name):
    if trans_a:
        k, m = a.shape
    else:
        m, k = a.shape
    if trans_b:
        n, kb = b.shape
    else:
        kb, n = b.shape
    assert k == kb, (a.shape, b.shape)
    tm, tn, tk = _pick(m, MM_TILE), _pick(n, MM_TILE), _pick(k, MM_TILE)
    nk = k // tk
    dims = (((0 if trans_a else 1,), (1 if trans_b else 0,)), ((), ()))

    def body(*refs):
        if res is None:
            a_ref, b_ref, o_ref, acc = refs
            r_ref = None
        else:
            a_ref, b_ref, r_ref, o_ref, acc = refs
        kk = pl.program_id(2)

        @pl.when(kk == 0)
        def _():
            acc[...] = jnp.zeros_like(acc)

        acc[...] += lax.dot_general(a_ref[...].astype(_MXU), b_ref[...].astype(_MXU), dims,
                                    preferred_element_type=_F32)

        @pl.when(kk == nk - 1)
        def _():
            r = acc[...]
            if scale is not None:
                r = r * scale
            if r_ref is not None:
                r = r + r_ref[...].astype(_F32)
            o_ref[...] = r.astype(out_dtype)

    a_spec = (pl.BlockSpec((tk, tm), lambda i, j, kk: (kk, i)) if trans_a
              else pl.BlockSpec((tm, tk), lambda i, j, kk: (i, kk)))
    b_spec = (pl.BlockSpec((tn, tk), lambda i, j, kk: (j, kk)) if trans_b
              else pl.BlockSpec((tk, tn), lambda i, j, kk: (kk, j)))
    o_spec = pl.BlockSpec((tm, tn), lambda i, j, kk: (i, j))
    in_specs = [a_spec, b_spec] + ([o_spec] if res is not None else [])
    args = (a, b) + ((res,) if res is not None else ())
    return pl.pallas_call(
        body, name=name, grid=(m // tm, n // tn, nk),
        in_specs=in_specs, out_specs=o_spec,
        out_shape=jax.ShapeDtypeStruct((m, n), out_dtype),
        scratch_shapes=[pltpu.VMEM((tm, tn), _F32)],
        compiler_params=_params(("parallel", "parallel", "arbitrary")),
    )(*args)


def _rmsnorm(x, g, *, out_dtype, name, rows=512):
    m, d = x.shape
    tm = _rows(m, rows)

    def body(x_ref, g_ref, o_ref):
        xf = x_ref[...].astype(_F32)
        r = lax.rsqrt(jnp.mean(xf * xf, axis=-1, keepdims=True) + NORM_EPS)
        o_ref[...] = ((xf * r) * g_ref[...]).astype(out_dtype)

    return pl.pallas_call(
        body, name=name, grid=(m // tm,),
        in_specs=[pl.BlockSpec((tm, d), lambda i: (i, 0)), pl.BlockSpec((1, d), lambda i: (0, 0))],
        out_specs=pl.BlockSpec((tm, d), lambda i: (i, 0)),
        out_shape=jax.ShapeDtypeStruct((m, d), out_dtype),
        compiler_params=_params(("parallel",)),
    )(x, g.reshape(1, d).astype(_F32))


def _rmsnorm_bwd(x, g, dy, dres=None, *, out_dtype=_F32, name, rows=512):
    m, d = x.shape
    tm = _rows(m, rows)

    def body(*refs):
        if dres is None:
            x_ref, g_ref, dy_ref, dx_ref, dg_ref = refs
            r_ref = None
        else:
            x_ref, g_ref, dy_ref, r_ref, dx_ref, dg_ref = refs

        @pl.when(pl.program_id(0) == 0)
        def _():
            dg_ref[...] = jnp.zeros_like(dg_ref)

        xf = x_ref[...].astype(_F32)
        r = lax.rsqrt(jnp.mean(xf * xf, axis=-1, keepdims=True) + NORM_EPS)
        xh = xf * r
        dyf = dy_ref[...].astype(_F32)
        dg_ref[...] += jnp.sum(dyf * xh, axis=0, keepdims=True)
        gdy = dyf * g_ref[...]
        dx = r * (gdy - xh * jnp.mean(gdy * xh, axis=-1, keepdims=True))
        if r_ref is not None:
            dx = dx + r_ref[...].astype(_F32)
        dx_ref[...] = dx.astype(out_dtype)

    row = pl.BlockSpec((tm, d), lambda i: (i, 0))
    vec = pl.BlockSpec((1, d), lambda i: (0, 0))
    in_specs = [row, vec, row] + ([row] if dres is not None else [])
    args = (x, g.reshape(1, d).astype(_F32), dy) + ((dres,) if dres is not None else ())
    dx, dg = pl.pallas_call(
        body, name=name, grid=(m // tm,),
        in_specs=in_specs, out_specs=[row, vec],
        out_shape=[jax.ShapeDtypeStruct((m, d), out_dtype), jax.ShapeDtypeStruct((1, d), _F32)],
        compiler_params=_params(("arbitrary",)),
    )(*args)
    return dx, dg.reshape(d)


def _rotate(xf, c, sn):
    w = xf.shape[1]
    if w > LANES:
        c, sn = jnp.tile(c, (1, w // LANES)), jnp.tile(sn, (1, w // LANES))
    lane = lax.broadcasted_iota(jnp.int32, xf.shape, 1)
    sw = jnp.where((lane & 31) < 16, pltpu.roll(xf, w - 16, 1), pltpu.roll(xf, 16, 1))
    return xf * c + sw * sn


def _seg_mean(v, seg_ref):
    outs = []
    for c in range(v.shape[1] // LANES):
        piece = v[:, c * LANES:(c + 1) * LANES]
        hi = piece.astype(jnp.bfloat16)
        lo = (piece - hi.astype(_F32)).astype(jnp.bfloat16)
        outs.append(jnp.dot(hi, seg_ref[...], preferred_element_type=_F32)
                    + jnp.dot(lo, seg_ref[...], preferred_element_type=_F32))
    return jnp.concatenate(outs, axis=1) if len(outs) > 1 else outs[0]


def _seg_matrix():
    lane = jnp.arange(LANES) // GQA_HEAD_DIM
    return ((lane[:, None] == lane[None, :]).astype(_F32) / GQA_HEAD_DIM).astype(jnp.bfloat16)


def _headnorm_rope(x, gain, cos_t, sin_t, *, scale=None, name):
    s, w = x.shape
    ts = _rows(s, 512)

    def body(x_ref, g_ref, seg_ref, c_ref, s_ref, o_ref):
        xf = x_ref[...]
        r = lax.rsqrt(_seg_mean(xf * xf, seg_ref) + NORM_EPS)
        y = _rotate((xf * r) * g_ref[...], c_ref[...], s_ref[...])
        if scale is not None:
            y = y * scale
        o_ref[...] = y.astype(o_ref.dtype)

    row = pl.BlockSpec((ts, w), lambda i: (i, 0))
    tab = pl.BlockSpec((ts, LANES), lambda i: (i, 0))
    return pl.pallas_call(
        body, name=name, grid=(s // ts,),
        in_specs=[row, pl.BlockSpec((1, w), lambda i: (0, 0)), pl.BlockSpec((LANES, LANES), lambda i: (0, 0)), tab, tab],
        out_specs=row, out_shape=jax.ShapeDtypeStruct((s, w), _ACT),
        compiler_params=_params(("parallel",)),
    )(x, jnp.tile(gain.astype(_F32), w // GQA_HEAD_DIM).reshape(1, w), _seg_matrix(), cos_t, sin_t)


def _headnorm_rope_bwd(x, gain, dy, cos_t, sin_t, *, name):
    s, w = x.shape
    ts = _rows(s, 512)

    def body(x_ref, g_ref, seg_ref, c_ref, s_ref, dy_ref, dx_ref, dg_ref):
        @pl.when(pl.program_id(0) == 0)
        def _():
            dg_ref[...] = jnp.zeros_like(dg_ref)

        xf = x_ref[...]
        r = lax.rsqrt(_seg_mean(xf * xf, seg_ref) + NORM_EPS)
        xh = xf * r
        dyn = _rotate(dy_ref[...].astype(_F32), c_ref[...], -s_ref[...])
        dg_ref[...] += jnp.sum(dyn * xh, axis=0, keepdims=True)
        gdy = dyn * g_ref[...]
        dx_ref[...] = (r * (gdy - xh * _seg_mean(gdy * xh, seg_ref))).astype(dx_ref.dtype)

    row = pl.BlockSpec((ts, w), lambda i: (i, 0))
    vec = pl.BlockSpec((1, w), lambda i: (0, 0))
    tab = pl.BlockSpec((ts, LANES), lambda i: (i, 0))
    dx, dg = pl.pallas_call(
        body, name=name, grid=(s // ts,),
        in_specs=[row, vec, pl.BlockSpec((LANES, LANES), lambda i: (0, 0)), tab, tab, row],
        out_specs=[row, vec],
        out_shape=[jax.ShapeDtypeStruct((s, w), _ACT), jax.ShapeDtypeStruct((1, w), _F32)],
        compiler_params=_params(("arbitrary",)),
    )(x, jnp.tile(gain.astype(_F32), w // GQA_HEAD_DIM).reshape(1, w), _seg_matrix(), cos_t, sin_t, dy)
    return dx, dg.reshape(w // GQA_HEAD_DIM, GQA_HEAD_DIM).sum(axis=0)


def _rope(x, cos_t, sin_t, *, out_dtype, name, sum_chunks=False, scale=None):
    s, w = x.shape
    assert w % LANES == 0
    ts = _rows(s, 512)
    ow = LANES if sum_chunks else w

    def body(x_ref, c_ref, s_ref, o_ref):
        y = _rotate(x_ref[...].astype(_F32), c_ref[...], s_ref[...])
        if scale is not None:
            y = y * scale
        if sum_chunks:
            shift = w // 2
            while shift >= 32:
                y = y + pltpu.roll(y, shift, 1)
                shift //= 2
            y = y[:, :LANES]
        o_ref[...] = y.astype(out_dtype)

    return pl.pallas_call(
        body, name=name, grid=(s // ts,),
        in_specs=[pl.BlockSpec((ts, w), lambda i: (i, 0)), pl.BlockSpec((ts, LANES), lambda i: (i, 0)),
                  pl.BlockSpec((ts, LANES), lambda i: (i, 0))],
        out_specs=pl.BlockSpec((ts, ow), lambda i: (i, 0)),
        out_shape=jax.ShapeDtypeStruct((s, ow), out_dtype),
        compiler_params=_params(("parallel",)),
    )(x, cos_t, sin_t)


_NT = (((1,), (1,)), ((), ()))
_TN = (((0,), (0,)), ((), ()))
LOG2_E = 1.4426950408889634
LN_2 = 0.6931471805599453
ATTN_FWD_ROWS = 2048
ATTN_ROWS = 512
ATTN_CHAINS = 2
ATTN_FWD_KEYS = 1024
ATTN_BWD_KEYS = 1024


def _grid_step(dims):
    step, total = 0, 1
    for axis, n in enumerate(dims):
        step = step * n + pl.program_id(axis)
        total *= n
    return step, total


def _attn_fwd(qt, k, vt, *, out_dtype, name, gather=None):
    h, dk, s = qt.shape
    g, nk, dv, tk = vt.shape
    assert nk * tk == s
    r = h // g
    rc = min(ATTN_FWD_ROWS, s)
    hp = min(r, ATTN_CHAINS)
    nrc = max(1, min(ATTN_CHAINS // hp, s // rc))
    tq = rc * nrc
    nhp, nq = r // hp, s // tq
    units = [(a, c) for a in range(hp) for c in range(nrc)]

    def body(q_ref, k_ref, v_ref, *rest):
        if gather is None:
            o_ref, lse_ref = rest
        else:
            x_ref, o_ref, lse_ref, all_ref, send_sems, recv_sems, local_sem = rest
            start, forward, finish = _gather_phases(x_ref, all_ref, send_sems, recv_sems, local_sem)
            step, total = _grid_step((g, nhp, nq))
            pl.when(step == 0)(start)
            pl.when(step == 3 * total // 4)(forward)

        qs = [q_ref[a, :, c * rc:(c + 1) * rc] for a, c in units]
        init = tuple((jnp.full((1, rc), NEG_INF, _F32), jnp.zeros((1, rc), _F32), jnp.zeros((dv, rc), _F32))
                     for _ in units)

        def trip(j, carry):
            kb, vb = k_ref[0, pl.ds(pl.multiple_of(j * tk, tk), tk), :], v_ref[0, j]
            out = []
            for u in range(len(units)):
                m, l, acc = carry[u]
                sc = jnp.dot(kb, qs[u], preferred_element_type=_F32)
                m_new = jnp.maximum(m, jnp.max(sc, axis=0, keepdims=True))
                p = jnp.exp2(sc - m_new)
                alpha = jnp.exp2(m - m_new)
                l = alpha * l + jnp.sum(p, axis=0, keepdims=True)
                acc = alpha * acc + jnp.dot(vb, p.astype(_MXU), preferred_element_type=_F32)
                out.append((m_new, l, acc))
            return tuple(out)

        fin = lax.fori_loop(0, nk, trip, init)
        for u, (a, c) in enumerate(units):
            m, l, acc = fin[u]
            o_ref[a, :, c * rc:(c + 1) * rc] = (acc / l).astype(out_dtype)
            lse_ref[a, :, c * rc:(c + 1) * rc] = m + jnp.log2(l)

        if gather is not None:
            pl.when(step == total - 1)(finish)

    q_blk = lambda gg, hh, i: (gg * nhp + hh, 0, i)
    in_specs = [pl.BlockSpec((hp, dk, tq), q_blk), pl.BlockSpec((1, s, dk), lambda gg, hh, i: (gg, 0, 0)),
                pl.BlockSpec((1, nk, dv, tk), lambda gg, hh, i: (gg, 0, 0, 0))]
    out_specs = [pl.BlockSpec((hp, dv, tq), q_blk), pl.BlockSpec((hp, 1, tq), q_blk)]
    out_shape = [jax.ShapeDtypeStruct((h, dv, s), out_dtype), jax.ShapeDtypeStruct((h, 1, s), _F32)]
    if gather is None:
        return pl.pallas_call(
            body, name=name, grid=(g, nhp, nq), in_specs=in_specs, out_specs=out_specs, out_shape=out_shape,
            compiler_params=_params(("parallel", "parallel", "parallel")),
        )(qt, k, vt)
    return pl.pallas_call(
        body, name=name, grid=(g, nhp, nq),
        in_specs=in_specs + [pl.BlockSpec(memory_space=pl.ANY)],
        out_specs=out_specs + [pl.BlockSpec(memory_space=pl.ANY)],
        out_shape=out_shape + [jax.ShapeDtypeStruct((N_DEV,) + gather.shape, gather.dtype)],
        scratch_shapes=_comm_scratch(),
        compiler_params=_params(("arbitrary", "arbitrary", "arbitrary")),
    )(qt, k, vt, gather)


def _attn_bwd(qt, k, v, ot, do, lse2, *, scale, name, exchange=None):
    h, dk, s = qt.shape
    g, _, dv = v.shape
    r = h // g
    hp = min(r, ATTN_CHAINS)
    tq = min(ATTN_ROWS, s)
    tk = min(ATTN_BWD_KEYS * (ATTN_CHAINS // hp), s)
    nhp, nq, nk = r // hp, s // tq, s // tk

    def body(qt_ref, k_ref, v_ref, ot_ref, do_ref, lse_ref, *rest):
        if exchange is None:
            dq_ref, dk_ref, dv_ref = rest
        else:
            g_ref, dq_ref, dk_ref, dv_ref, got_ref, send_sems, recv_sems, local_sem = rest
            start, finish = _exchange_phases(g_ref, got_ref, send_sems, recv_sems, local_sem)
            step, total = _grid_step((g, nhp, nq))
            pl.when(step == 0)(start)
        hh, i = pl.program_id(1), pl.program_id(2)

        @pl.when((hh == 0) & (i == 0))
        def _():
            dk_ref[...] = jnp.zeros_like(dk_ref)
            dv_ref[...] = jnp.zeros_like(dv_ref)

        qts = [qt_ref[a] for a in range(hp)]
        qs = [qt_ref[a].T for a in range(hp)]
        dos = [do_ref[a] for a in range(hp)]
        dots = [do_ref[a].T for a in range(hp)]
        ls = [lse_ref[a].T for a in range(hp)]
        dls = [jnp.sum(do_ref[a].astype(_F32) * ot_ref[a].T.astype(_F32), axis=1, keepdims=True)
               for a in range(hp)]

        def trip(j, carry):
            rows = pl.ds(pl.multiple_of(j * tk, tk), tk)
            kb, vb = k_ref[0, rows, :], v_ref[0, rows, :]
            out = []
            for a in range(hp):
                p = jnp.exp2(lax.dot_general(qs[a], kb, _NT, preferred_element_type=_F32) - ls[a])
                dp = lax.dot_general(dos[a], vb, _NT, preferred_element_type=_F32)
                ds = (p * (dp - dls[a])).astype(_MXU)
                dv_ref[0, j] += jnp.dot(dots[a], p.astype(_MXU), preferred_element_type=_F32)
                dk_ref[0, j] += jnp.dot(qts[a], ds, preferred_element_type=_F32)
                out.append(carry[a] + jnp.dot(ds, kb, preferred_element_type=_F32))
            return tuple(out)

        fin = lax.fori_loop(0, nk, trip, tuple(jnp.zeros((tq, dk), _F32) for _ in range(hp)))
        for a in range(hp):
            dq_ref[a] = (fin[a] * scale).astype(dq_ref.dtype)

        @pl.when((hh == nhp - 1) & (i == nq - 1))
        def _():
            dk_ref[...] = dk_ref[...] * LN_2

        if exchange is not None:
            pl.when(step == total - 1)(finish)

    q_blk = lambda gg, hh, i: (gg * nhp + hh, i, 0)
    qt_blk = lambda gg, hh, i: (gg * nhp + hh, 0, i)
    kv_blk = lambda gg, hh, i: (gg, 0, 0)
    in_specs = [pl.BlockSpec((hp, dk, tq), qt_blk), pl.BlockSpec((1, s, dk), kv_blk), pl.BlockSpec((1, s, dv), kv_blk),
                pl.BlockSpec((hp, dv, tq), qt_blk), pl.BlockSpec((hp, tq, dv), q_blk), pl.BlockSpec((hp, 1, tq), qt_blk)]
    out_specs = [pl.BlockSpec((hp, tq, dk), q_blk), pl.BlockSpec((1, nk, dk, tk), lambda gg, hh, i: (gg, 0, 0, 0)),
                 pl.BlockSpec((1, nk, dv, tk), lambda gg, hh, i: (gg, 0, 0, 0))]
    out_shape = [jax.ShapeDtypeStruct((h, s, dk), _ACT), jax.ShapeDtypeStruct((g, nk, dk, tk), _F32),
                 jax.ShapeDtypeStruct((g, nk, dv, tk), _F32)]
    args = (qt, k, v, ot, do, lse2)
    scratch = []
    if exchange is not None:
        in_specs, args = in_specs + [pl.BlockSpec(memory_space=pl.ANY)], args + (exchange,)
        out_specs = out_specs + [pl.BlockSpec(memory_space=pl.ANY)]
        out_shape = out_shape + [jax.ShapeDtypeStruct(exchange.shape, exchange.dtype)]
        scratch = _comm_scratch()
    return pl.pallas_call(
        body, name=name, grid=(g, nhp, nq), in_specs=in_specs, out_specs=out_specs, out_shape=out_shape,
        scratch_shapes=scratch, compiler_params=_params(("arbitrary", "arbitrary", "arbitrary")),
    )(*args)


def _unchunk(xt):
    g, nk, d, tk = xt.shape
    return xt.transpose(1, 3, 0, 2).reshape(nk * tk, g * d)


def _chunk_t(x2d, g):
    s = x2d.shape[0]
    tk = min(ATTN_FWD_KEYS, s)
    return x2d.reshape(s // tk, tk, g, -1).transpose(2, 0, 3, 1)


def _heads_t(x2d, h):
    s = x2d.shape[0]
    return x2d.reshape(s, h, -1).transpose(1, 2, 0)


def _unheads_t(xt):
    h, d, s = xt.shape
    return xt.transpose(2, 0, 1).reshape(s, h * d)


WIN_ROWS = 512


def _win_geometry(s, dil):
    t = min(WIN_ROWS, s)
    length = s // dil
    lg = length.bit_length() - 1
    assert 1 << lg == length and t % DIL_HALF == 0 and s % t == 0
    return t, t + 2 * DIL_HALF, lg, t // DIL_HALF, s // DIL_HALF


def _win_specs(hn, t, d, halo_per_blk, n_halo, part=0):
    return [pl.BlockSpec((hn, DIL_HALF, d), lambda i: (part, jnp.maximum(i * halo_per_blk - 1, 0), 0)),
            pl.BlockSpec((hn, t, d), lambda i: (part, i, 0)),
            pl.BlockSpec((hn, DIL_HALF, d), lambda i: (part, jnp.minimum((i + 1) * halo_per_blk, n_halo - 1), 0))]


def _win_mask(i, t, w, lg, dil, wide_rows):
    shape = (w, t) if wide_rows else (t, w)
    rows = lax.broadcasted_iota(jnp.int32, shape, 0)
    cols = lax.broadcasted_iota(jnp.int32, shape, 1)
    base = i * t
    if wide_rows:
        pq, pk = base - DIL_HALF + rows, base + cols
    else:
        pq, pk = base + rows, base - DIL_HALF + cols
    arel = jnp.abs(pk - pq)
    valid = (arel <= DIL_HALF) & ((pk >> lg) == (pq >> lg))
    return valid, (-dil * arel).astype(_F32)


def _win3(lo_ref, mid_ref, hi_ref, a):
    return jnp.concatenate([lo_ref[a], mid_ref[a], hi_ref[a]], axis=0)


def _win_fwd(qkv, slopes, *, dil, name):
    hn, s, d = qkv.shape[0] // 3, qkv.shape[1], qkv.shape[2]
    t, w, lg, hpb, n_halo = _win_geometry(s, dil)
    scale = d ** -0.5

    def body(sl_ref, q_ref, klo, kmid, khi, vlo, vmid, vhi, o_ref, lse_ref):
        valid, nb = _win_mask(pl.program_id(0), t, w, lg, dil, False)
        for a in range(hn):
            kw, vw = _win3(klo, kmid, khi, a), _win3(vlo, vmid, vhi, a)
            sc = lax.dot_general(q_ref[a], kw, _NT, preferred_element_type=_F32) * scale
            sc = jnp.where(valid, sc + sl_ref[a] * nb, NEG_INF)
            m = jnp.max(sc, axis=1, keepdims=True)
            e = jnp.exp(sc - m)
            den = jnp.sum(e, axis=1, keepdims=True)
            o_ref[a] = jnp.dot(e.astype(_MXU), vw, preferred_element_type=_F32) / den
            lse_ref[a] = m + jnp.log(den)

    blk = lambda c: pl.BlockSpec((hn, t, c), lambda i: (0, i, 0))
    return pl.pallas_call(
        body, name=name, grid=(s // t,),
        in_specs=([pl.BlockSpec(memory_space=pltpu.SMEM), blk(d)] + _win_specs(hn, t, d, hpb, n_halo, 1)
                  + _win_specs(hn, t, d, hpb, n_halo, 2)),
        out_specs=[blk(d), blk(1)],
        out_shape=[jax.ShapeDtypeStruct((hn, s, d), _F32), jax.ShapeDtypeStruct((hn, s, 1), _F32)],
        compiler_params=_params(("parallel",)),
    )(slopes.astype(_F32), qkv, qkv, qkv, qkv, qkv, qkv, qkv)


def _win_bwd_dq(qkv, do, lse, delta, slopes, *, dil, name):
    hn, s, d = qkv.shape[0] // 3, qkv.shape[1], qkv.shape[2]
    t, w, lg, hpb, n_halo = _win_geometry(s, dil)
    scale = d ** -0.5

    def body(sl_ref, q_ref, klo, kmid, khi, vlo, vmid, vhi, do_ref, lse_ref, dl_ref, dq_ref):
        valid, nb = _win_mask(pl.program_id(0), t, w, lg, dil, False)
        for a in range(hn):
            kw, vw = _win3(klo, kmid, khi, a), _win3(vlo, vmid, vhi, a)
            sc = lax.dot_general(q_ref[a], kw, _NT, preferred_element_type=_F32) * scale
            p = jnp.exp(jnp.where(valid, sc + sl_ref[a] * nb, NEG_INF) - lse_ref[a])
            dp = lax.dot_general(do_ref[a], vw, _NT, preferred_element_type=_F32)
            ds = (p * (dp - dl_ref[a])).astype(_MXU)
            dq_ref[a] = (jnp.dot(ds, kw, preferred_element_type=_F32) * scale).astype(dq_ref.dtype)

    blk = lambda c: pl.BlockSpec((hn, t, c), lambda i: (0, i, 0))
    return pl.pallas_call(
        body, name=name, grid=(s // t,),
        in_specs=([pl.BlockSpec(memory_space=pltpu.SMEM), blk(d)] + _win_specs(hn, t, d, hpb, n_halo, 1)
                  + _win_specs(hn, t, d, hpb, n_halo, 2) + [blk(d), blk(1), blk(1)]),
        out_specs=blk(d),
        out_shape=jax.ShapeDtypeStruct((hn, s, d), _ACT),
        compiler_params=_params(("parallel",)),
    )(slopes.astype(_F32), qkv, qkv, qkv, qkv, qkv, qkv, qkv, do, lse, delta)


def _win_bwd_dkv(qkv, do, lse, delta, slopes, *, dil, name):
    hn, s, d = qkv.shape[0] // 3, qkv.shape[1], qkv.shape[2]
    t, w, lg, hpb, n_halo = _win_geometry(s, dil)
    scale = d ** -0.5

    def body(sl_ref, qlo, qmid, qhi, dolo, domid, dohi, llo, lmid, lhi, dllo, dlmid, dlhi, k_ref, v_ref, dkv_ref):
        valid, nb = _win_mask(pl.program_id(0), t, w, lg, dil, True)
        for a in range(hn):
            qw, dow = _win3(qlo, qmid, qhi, a), _win3(dolo, domid, dohi, a)
            lw, dlw = _win3(llo, lmid, lhi, a), _win3(dllo, dlmid, dlhi, a)
            sc = lax.dot_general(qw, k_ref[a], _NT, preferred_element_type=_F32) * scale
            p = jnp.exp(jnp.where(valid, sc + sl_ref[a] * nb, NEG_INF) - lw)
            dp = lax.dot_general(dow, v_ref[a], _NT, preferred_element_type=_F32)
            ds = (p * (dp - dlw)).astype(_MXU)
            dkv_ref[hn + a] = lax.dot_general(p.astype(_MXU), dow, _TN,
                                              preferred_element_type=_F32).astype(dkv_ref.dtype)
            dkv_ref[a] = (lax.dot_general(ds, qw, _TN, preferred_element_type=_F32) * scale).astype(dkv_ref.dtype)

    part = lambda which: pl.BlockSpec((hn, t, d), lambda i: (which, i, 0))
    return pl.pallas_call(
        body, name=name, grid=(s // t,),
        in_specs=([pl.BlockSpec(memory_space=pltpu.SMEM)] + _win_specs(hn, t, d, hpb, n_halo)
                  + _win_specs(hn, t, d, hpb, n_halo) + _win_specs(hn, t, 1, hpb, n_halo)
                  + _win_specs(hn, t, 1, hpb, n_halo) + [part(1), part(2)]),
        out_specs=pl.BlockSpec((2 * hn, t, d), lambda i: (0, i, 0)),
        out_shape=jax.ShapeDtypeStruct((2 * hn, s, d), _ACT),
        compiler_params=_params(("parallel",)),
    )(slopes.astype(_F32), qkv, qkv, qkv, do, do, do, lse, lse, lse, delta, delta, delta, qkv, qkv)


def _merge_weights(lses):
    mx = functools.reduce(jnp.maximum, lses)
    es = [jnp.exp(l - mx) for l in lses]
    den = functools.reduce(lambda a, b: a + b, es)
    return [e / den for e in es]


def _merge_fwd(outs, lses, *, name):
    ng = len(outs)
    sl, s, d = outs[0].shape
    t = _rows(s, 512)

    def body(*refs):
        o_refs, l_refs, c_ref = refs[:ng], refs[ng:2 * ng], refs[2 * ng]
        wts = _merge_weights([r[0] for r in l_refs])
        comb = functools.reduce(lambda a, b: a + b, [w * r[0] for w, r in zip(wts, o_refs)])
        c_ref[0] = comb.astype(c_ref.dtype)

    big = pl.BlockSpec((1, t, d), lambda a, i: (a, i, 0))
    small = pl.BlockSpec((1, t, 1), lambda a, i: (a, i, 0))
    return pl.pallas_call(
        body, name=name, grid=(sl, s // t),
        in_specs=[big] * ng + [small] * ng, out_specs=big,
        out_shape=jax.ShapeDtypeStruct((sl, s, d), _ACT),
        compiler_params=_params(("parallel", "parallel")),
    )(*outs, *lses)


def _merge_bwd(dcomb, outs, lses, *, name):
    ng = len(outs)
    sl, s, d = outs[0].shape
    t = _rows(s, 512)

    def body(*refs):
        dc_ref, o_refs, l_refs = refs[0], refs[1:1 + ng], refs[1 + ng:1 + 2 * ng]
        do_refs, dl_refs = refs[1 + 2 * ng:1 + 3 * ng], refs[1 + 3 * ng:]
        wts = _merge_weights([r[0] for r in l_refs])
        dc = dc_ref[0].astype(_F32)
        comb = functools.reduce(lambda a, b: a + b, [w * r[0] for w, r in zip(wts, o_refs)])
        dot = jnp.sum(dc * comb, axis=-1, keepdims=True)
        for w, do_ref, dl_ref in zip(wts, do_refs, dl_refs):
            do_ref[0] = (w * dc).astype(do_ref.dtype)
            dl_ref[0] = w * dot

    big = pl.BlockSpec((1, t, d), lambda a, i: (a, i, 0))
    small = pl.BlockSpec((1, t, 1), lambda a, i: (a, i, 0))
    res = pl.pallas_call(
        body, name=name, grid=(sl, s // t),
        in_specs=[big] + [big] * ng + [small] * ng, out_specs=[big] * ng + [small] * ng,
        out_shape=[jax.ShapeDtypeStruct((sl, s, d), _ACT)] * ng + [jax.ShapeDtypeStruct((sl, s, 1), _F32)] * ng,
        compiler_params=_params(("parallel", "parallel")),
    )(dcomb, *outs, *lses)
    return res[:ng], res[ng:]


SWIGLU_ROWS = 512


def _gate_up(a, w_in_t, *, name):
    m, k = a.shape
    n = w_in_t.shape[0] // 2
    tm, tn, tk = _pick(m, SWIGLU_ROWS), _pick(n, MM_TILE), _pick(k, MM_TILE)
    nk = k // tk

    def body(a_ref, g_ref, u_ref, go_ref, uo_ref, act_ref, acc_g, acc_u):
        kk = pl.program_id(2)

        @pl.when(kk == 0)
        def _():
            acc_g[...] = jnp.zeros_like(acc_g)
            acc_u[...] = jnp.zeros_like(acc_u)

        av = a_ref[...].astype(_MXU)
        acc_g[...] += lax.dot_general(av, g_ref[...].astype(_MXU), _NT, preferred_element_type=_F32)
        acc_u[...] += lax.dot_general(av, u_ref[...].astype(_MXU), _NT, preferred_element_type=_F32)

        @pl.when(kk == nk - 1)
        def _():
            gf, uf = acc_g[...], acc_u[...]
            go_ref[...] = gf.astype(go_ref.dtype)
            uo_ref[...] = uf.astype(uo_ref.dtype)
            act_ref[...] = (gf * jax.nn.sigmoid(gf) * uf).astype(act_ref.dtype)

    o_spec = pl.BlockSpec((tm, tn), lambda i, j, kk: (i, j))
    out = jax.ShapeDtypeStruct((m, n), _ACT)
    return pl.pallas_call(
        body, name=name, grid=(m // tm, n // tn, nk),
        in_specs=[pl.BlockSpec((tm, tk), lambda i, j, kk: (i, kk)),
                  pl.BlockSpec((tn, tk), lambda i, j, kk: (j, kk)),
                  pl.BlockSpec((tn, tk), lambda i, j, kk: (j + n // tn, kk))],
        out_specs=[o_spec, o_spec, o_spec], out_shape=[out, out, out],
        scratch_shapes=[pltpu.VMEM((tm, tn), _F32), pltpu.VMEM((tm, tn), _F32)],
        compiler_params=_params(("parallel", "parallel", "arbitrary")),
    )(a, w_in_t, w_in_t)


def _gate_up_bwd(dout, w_out, gate, up, *, name):
    m, k = dout.shape
    n = w_out.shape[0]
    tm, tn, tk = _pick(m, SWIGLU_ROWS), _pick(n, MM_TILE), _pick(k, MM_TILE)
    nk = k // tk

    def body(d_ref, w_ref, g_ref, u_ref, dg_ref, du_ref, acc):
        kk = pl.program_id(2)

        @pl.when(kk == 0)
        def _():
            acc[...] = jnp.zeros_like(acc)

        acc[...] += lax.dot_general(d_ref[...].astype(_MXU), w_ref[...].astype(_MXU), _NT,
                                    preferred_element_type=_F32)

        @pl.when(kk == nk - 1)
        def _():
            daf = acc[...]
            gf = g_ref[...].astype(_F32)
            sg = jax.nn.sigmoid(gf)
            dg_ref[...] = (daf * u_ref[...].astype(_F32) * (sg + gf * sg * (1.0 - sg))).astype(dg_ref.dtype)
            du_ref[...] = (daf * (gf * sg)).astype(du_ref.dtype)

    o_spec = pl.BlockSpec((tm, tn), lambda i, j, kk: (i, j))
    out = jax.ShapeDtypeStruct((m, n), _ACT)
    return pl.pallas_call(
        body, name=name, grid=(m // tm, n // tn, nk),
        in_specs=[pl.BlockSpec((tm, tk), lambda i, j, kk: (i, kk)), pl.BlockSpec((tn, tk), lambda i, j, kk: (j, kk)),
                  o_spec, o_spec],
        out_specs=[o_spec, o_spec], out_shape=[out, out],
        scratch_shapes=[pltpu.VMEM((tm, tn), _F32)],
        compiler_params=_params(("parallel", "parallel", "arbitrary")),
    )(dout, w_out, gate, up)


def _final_loss(x, g, target, *, name):
    m, d = x.shape
    tm = _rows(m, 512)

    def body(x_ref, g_ref, t_ref, loss_ref, dx_ref, dg_ref):
        @pl.when(pl.program_id(0) == 0)
        def _():
            loss_ref[...] = jnp.zeros_like(loss_ref)
            dg_ref[...] = jnp.zeros_like(dg_ref)

        xf = x_ref[...]
        r = lax.rsqrt(jnp.mean(xf * xf, axis=-1, keepdims=True) + NORM_EPS)
        xh = xf * r
        err = xh * g_ref[...] - t_ref[...]
        loss_ref[...] += 0.5 * jnp.sum(jnp.mean(err * err, axis=-1, keepdims=True))
        dy = err * (1.0 / d)
        dg_ref[...] += jnp.sum(dy * xh, axis=0, keepdims=True)
        gdy = dy * g_ref[...]
        dx_ref[...] = r * (gdy - xh * jnp.mean(gdy * xh, axis=-1, keepdims=True))

    row = pl.BlockSpec((tm, d), lambda i: (i, 0))
    vec = pl.BlockSpec((1, d), lambda i: (0, 0))
    loss, dx, dg = pl.pallas_call(
        body, name=name, grid=(m // tm,),
        in_specs=[row, vec, row],
        out_specs=[pl.BlockSpec((8, LANES), lambda i: (0, 0)), row, vec],
        out_shape=[jax.ShapeDtypeStruct((8, LANES), _F32), jax.ShapeDtypeStruct((m, d), _F32),
                   jax.ShapeDtypeStruct((1, d), _F32)],
        compiler_params=_params(("arbitrary",)),
    )(x, g.reshape(1, d), target)
    return loss[0, 0], dx, dg.reshape(d)


def _reduce_adamw(parts, w, m, v, *, name):
    rws, cols = w.shape
    tr = rws
    for cand in range(min(rws, 256), 0, -SUBLANES_16BIT):
        if cand % SUBLANES_16BIT == 0 and rws % cand == 0:
            tr = cand
            break

    def body(p_ref, w_ref, m_ref, v_ref, g_ref, d_ref, nm_ref, nv_ref):
        gsum = p_ref[0].astype(_F32)
        for dev in range(1, N_DEV):
            gsum = gsum + p_ref[dev].astype(_F32)
        m2 = ADAM_B1 * m_ref[...] + (1.0 - ADAM_B1) * gsum
        v2 = ADAM_B2 * v_ref[...] + (1.0 - ADAM_B2) * (gsum * gsum)
        m_hat = m2 / (1.0 - ADAM_B1 ** ADAM_STEP)
        v_hat = v2 / (1.0 - ADAM_B2 ** ADAM_STEP)
        g_ref[...] = gsum
        d_ref[...] = -ADAM_LR * (m_hat / (jnp.sqrt(v_hat) + ADAM_EPS) + ADAM_WD * w_ref[...])
        nm_ref[...] = m2
        nv_ref[...] = v2

    blk = pl.BlockSpec((tr, cols), lambda i: (i, 0))
    out = jax.ShapeDtypeStruct((rws, cols), _F32)
    return pl.pallas_call(
        body, name=name, grid=(rws // tr,),
        in_specs=[pl.BlockSpec((N_DEV, tr, cols), lambda i: (0, i, 0)), blk, blk, blk],
        out_specs=[blk, blk, blk, blk], out_shape=[out, out, out, out],
        compiler_params=_params(("parallel",)),
    )(parts, w, m, v)


def _mesh_pos():
    return lax.axis_index("x"), lax.axis_index("y"), lax.axis_index("c")


def _all_gather(block):
    rws, cols = block.shape

    def body(x_ref, out_ref, send_sems, recv_sems, local_sem):
        start, forward, finish = _gather_phases(x_ref, out_ref, send_sems, recv_sems, local_sem)
        start()
        forward()
        finish()

    return pl.pallas_call(
        body, name="weights_all_gather",
        out_shape=jax.ShapeDtypeStruct((N_DEV, rws, cols), block.dtype),
        in_specs=[pl.BlockSpec(memory_space=pl.ANY)],
        out_specs=pl.BlockSpec(memory_space=pl.ANY),
        scratch_shapes=_comm_scratch(),
    )(block)


def _comm_scratch():
    return [pltpu.SemaphoreType.DMA((N_DEV - 1,)), pltpu.SemaphoreType.DMA((N_DEV - 1,)), pltpu.SemaphoreType.DMA]


def _gather_phases(x_ref, out_ref, send_sems, recv_sems, local_sem):
    x, y, c = _mesh_pos()
    me, sibling = (x, y, c), (x, y, 1 - c)
    chips = [(1 - x, y), (x, 1 - y), (1 - x, 1 - y)]

    def slot(px, py, pc):
        return out_ref.at[4 * px + 2 * py + pc]

    def copy(k, blk, to, src=None):
        return pltpu.make_async_remote_copy(
            src_ref=slot(*blk) if src is None else src, dst_ref=slot(*blk),
            send_sem=send_sems.at[k], recv_sem=recv_sems.at[k],
            device_id=to, device_id_type=pl.DeviceIdType.MESH)

    mine = pltpu.make_async_copy(x_ref, slot(*me), local_sem)
    first = [copy(0, me, sibling, src=x_ref)]
    first += [copy(1 + j, me, (*chip, c), src=x_ref) for j, chip in enumerate(chips)]
    passed = [copy(4 + j, (*chip, c), sibling) for j, chip in enumerate(chips)]

    def start():
        mine.start()
        for cp in first:
            cp.start()

    def forward():
        for j, chip in enumerate(chips):
            copy(1 + j, (*chip, c), me).wait_recv()
            passed[j].start()

    def finish():
        copy(0, sibling, me).wait_recv()
        for j, chip in enumerate(chips):
            copy(4 + j, (*chip, 1 - c), me).wait_recv()
        for cp in first + passed:
            cp.wait_send()
        mine.wait()

    return start, forward, finish


def _exchange(parts):
    def body(g_ref, out_ref, send_sems, recv_sems, local_sem):
        start, finish = _exchange_phases(g_ref, out_ref, send_sems, recv_sems, local_sem)
        start()
        finish()

    return pl.pallas_call(
        body, name="grads_exchange",
        out_shape=jax.ShapeDtypeStruct(parts.shape, parts.dtype),
        in_specs=[pl.BlockSpec(memory_space=pl.ANY)],
        out_specs=pl.BlockSpec(memory_space=pl.ANY),
        scratch_shapes=_comm_scratch(),
    )(parts)


def _exchange_phases(g_ref, out_ref, send_sems, recv_sems, local_sem):
    x, y, c = _mesh_pos()
    me = 4 * x + 2 * y + c
    mine = pltpu.make_async_copy(g_ref.at[me], out_ref.at[me], local_sem)
    sends, arrivals = [], []
    for k in range(1, N_DEV):
        px = 1 - x if k & 4 else x
        py = 1 - y if k & 2 else y
        pc = 1 - c if k & 1 else c
        peer = 4 * px + 2 * py + pc
        sems = dict(send_sem=send_sems.at[k - 1], recv_sem=recv_sems.at[k - 1],
                    device_id=(px, py, pc), device_id_type=pl.DeviceIdType.MESH)
        sends.append(pltpu.make_async_remote_copy(src_ref=g_ref.at[peer], dst_ref=out_ref.at[me], **sems))
        arrivals.append(pltpu.make_async_remote_copy(src_ref=g_ref.at[peer], dst_ref=out_ref.at[peer], **sems))

    def start():
        mine.start()
        for cp in sends:
            cp.start()

    def finish():
        for cp in arrivals:
            cp.wait_recv()
        for cp in sends:
            cp.wait_send()
        mine.wait()

    return start, finish


_BIG = (("w_in_ab", 2), ("mla_w_uq", 1), ("mla_w_ukv", 1), ("w_out_ab", 2), ("gqa_w_q", 1), ("gqa_w_kv", 1),
        ("gqa_w_o", 1), ("ffn_w_in", 1), ("ffn_w_out", 1))
_TRANSPOSED = ("ffn_w_in",)


def _stored(arrays):
    return {n: (a.transpose(0, 2, 1) if n in _TRANSPOSED else a) for n, a in arrays.items()}
_SMALL = ("mix_norm_ab", "ffn_norm", "final_norm", "mix_norm_c", "mla_q_norm", "mla_kv_norm", "gqa_q_norm", "gqa_k_norm")
SMALL_ROWS = 16
GATHER_TAIL_ROWS = 16
EXCHANGE_TAIL_ROWS = 32


def _view3(a):
    return a.reshape(a.shape[0], a.shape[1], -1)


def _pad_rows(a2d):
    pad = -a2d.shape[0] % SUBLANES_16BIT
    return jnp.pad(a2d, ((0, pad), (0, 0))) if pad else a2d


def _pack_rows(shard):
    return _pad_rows(shard.reshape(-1, PACK_COLS))


def _packed_rows(shape):
    n = 1
    for d in shape:
        n *= d
    rows = n // PACK_COLS
    return rows + (-rows % SUBLANES_16BIT)


_FIRST = {"w_in_ab": (0, 1), "mla_w_uq": (0, 1), "mla_w_ukv": (0, 1)}
_REST = {"w_in_ab": (1, 1), "mla_w_uq": (1, 1), "mla_w_ukv": (1, 1), "w_out_ab": (0, 2), "gqa_w_q": (0, 2),
         "gqa_w_kv": (0, 2), "gqa_w_o": (0, 2), "ffn_w_in": (0, 4), "ffn_w_out": (0, 4)}
PACK_ROWS = 128


def _group_of(arrays, group):
    return {n: arrays[n][lo:lo + cnt] for n, (lo, cnt) in group.items()}


def _pack_big(shards):
    parts = [_pack_rows(shards[n]) for n, _ in _BIG if n in shards]
    fill = -sum(p.shape[0] for p in parts) % PACK_ROWS
    return jnp.concatenate(parts + [jnp.zeros((fill, PACK_COLS), parts[0].dtype)], axis=0)


def _unpack_big(packed, shapes):
    out, off = {}, 0
    for n, _ in _BIG:
        if n not in shapes:
            continue
        size = 1
        for d in shapes[n]:
            size *= d
        out[n] = packed[off:off + size // PACK_COLS].reshape(shapes[n])
        off += _packed_rows(shapes[n])
    return out


def _unpack_gathered(gathered, shapes):
    out, off = {}, 0
    for n, axis in _BIG:
        if n not in shapes:
            continue
        size = 1
        for d in shapes[n]:
            size *= d
        l3 = (shapes[n][0], shapes[n][1], size // (shapes[n][0] * shapes[n][1]))
        sh = gathered[:, off:off + size // PACK_COLS].reshape((N_DEV,) + l3)
        if axis == 1:
            full = sh.transpose(1, 0, 2, 3).reshape(l3[0], N_DEV * l3[1], l3[2])
        else:
            full = sh.transpose(1, 2, 0, 3).reshape(l3[0], l3[1], N_DEV * l3[2])
        out[n] = full
        off += _packed_rows(shapes[n])
    return out


def _split_for_devices(layers, axis):
    rws, cols = layers[0].shape
    if axis == 1:
        flat = jnp.concatenate([g.reshape(N_DEV, -1, PACK_COLS) for g in layers], axis=1)
    else:
        sh = jnp.stack(layers).reshape(len(layers), rws, N_DEV, cols // N_DEV).transpose(2, 0, 1, 3)
        flat = sh.reshape(N_DEV, -1, PACK_COLS)
    pad = -flat.shape[1] % SUBLANES_16BIT
    return jnp.pad(flat, ((0, 0), (0, pad), (0, 0))) if pad else flat


def _to_bits16(a_f32_rows):
    r = a_f32_rows.shape[0]
    return lax.bitcast_convert_type(a_f32_rows, jnp.bfloat16).reshape(2 * r, PACK_COLS)


def _from_bits16(a_bf16_rows):
    lead, r = a_bf16_rows.shape[:-2], a_bf16_rows.shape[-2]
    return lax.bitcast_convert_type(a_bf16_rows.reshape(lead + (r // 2, PACK_COLS, 2)), _F32)


def _small_sizes():
    return {"mix_norm_ab": 2 * D_MODEL, "ffn_norm": DEPTH * D_MODEL, "final_norm": D_MODEL, "mix_norm_c": 2 * D_MODEL,
            "mla_q_norm": 2 * MLA_Q_RANK, "mla_kv_norm": 2 * MLA_KV_RANK, "gqa_q_norm": 2 * GQA_HEAD_DIM,
            "gqa_k_norm": 2 * GQA_HEAD_DIM}


def _pack_small(vals):
    flat = jnp.concatenate([vals[n].reshape(-1).astype(_F32) for n in _SMALL])
    return jnp.pad(flat, (0, SMALL_ROWS * PACK_COLS - flat.shape[0])).reshape(SMALL_ROWS, PACK_COLS)


def _unpack_small(pack):
    flat, out, off = pack.reshape(-1), {}, 0
    sizes = _small_sizes()
    for n in _SMALL:
        out[n] = flat[off:off + sizes[n]]
        off += sizes[n]
    return out


def _angles(pos, dim):
    freqs = ROPE_THETA ** (-jnp.arange(0, dim, 2, dtype=_F32) / dim)
    ang = pos.astype(_F32)[:, None] * freqs[None, :]
    return jnp.cos(ang), jnp.sin(ang)


def _rope_tables(s):
    pos = jnp.arange(s)
    cos_t, sin_t = _angles(pos, MLA_ROPE)
    mla_c = jnp.tile(jnp.concatenate([cos_t, cos_t], -1), (1, LANES // 32))
    mla_s = jnp.tile(jnp.concatenate([-sin_t, sin_t], -1), (1, LANES // 32))
    rows = s // GRID_W
    row_idx = jnp.broadcast_to(jnp.arange(rows)[:, None], (rows, GRID_W)).reshape(-1)
    col_idx = jnp.broadcast_to(jnp.arange(GRID_W)[None, :], (rows, GRID_W)).reshape(-1)
    cos_r, sin_r = _angles(row_idx, GQA_HEAD_DIM // 2)
    cos_c, sin_c = _angles(col_idx, GQA_HEAD_DIM // 2)
    gqa_c = jnp.tile(jnp.concatenate([cos_r, cos_r, cos_c, cos_c], -1), (1, LANES // GQA_HEAD_DIM))
    gqa_s = jnp.tile(jnp.concatenate([-sin_r, sin_r, -sin_c, sin_c], -1), (1, LANES // GQA_HEAD_DIM))
    return (mla_c, mla_s), (gqa_c, gqa_s)


def _heads(x2d, h):
    s = x2d.shape[0]
    return x2d.reshape(s, h, -1).transpose(1, 0, 2)


def _unheads(xh):
    h, s, d = xh.shape
    return xh.transpose(1, 0, 2).reshape(s, h * d)


def _to_res(x2d, dil):
    s, cols = x2d.shape
    if dil > 1:
        x2d = x2d.reshape(s // dil, dil, cols).transpose(1, 0, 2).reshape(s, cols)
    return x2d.reshape(s, cols // DIL_HEAD_DIM, DIL_HEAD_DIM).transpose(1, 0, 2)


def _from_res(xh, dil):
    heads, s, d = xh.shape
    x2d = xh.transpose(1, 0, 2).reshape(s, heads * d)
    if dil > 1:
        x2d = x2d.reshape(dil, s // dil, heads * d).transpose(1, 0, 2).reshape(s, heads * d)
    return x2d


def _group_major(w_b):
    rws = w_b.shape[0]
    return w_b.reshape(rws, 3, DIL_GROUPS, DIL_SLOTS * DIL_HEAD_DIM).transpose(0, 2, 1, 3).reshape(rws, -1)


def _res_to_tok(xh, dil):
    sl, s, c = xh.shape
    return xh.reshape(sl, dil, s // dil, c).transpose(0, 2, 1, 3).reshape(sl, s, c)


def _tok_to_res(xh, dil):
    sl, s, c = xh.shape
    return xh.reshape(sl, s // dil, dil, c).transpose(0, 2, 1, 3).reshape(sl, s, c)


def _ffn_fwd(x, w, tag):
    hn = _rmsnorm(x, w["norm"], out_dtype=_ACT, name=f"{tag}_norm")
    gate, up, act = _gate_up(hn, w["w_in_t"], name=f"{tag}_gate_up")
    out = _matmul(act, w["w_out"], res=x, name=f"{tag}_out")
    return out, (x, hn, gate, up, act)


def _ffn_bwd(dout, saved, w, tag):
    x, hn, gate, up, act = saved
    w_gate_t, w_up_t = w["w_in_t"][:FFN_HIDDEN], w["w_in_t"][FFN_HIDDEN:]
    d_w_out = _matmul(act, dout, trans_a=True, name=f"{tag}_dwout")
    dgate, dup = _gate_up_bwd(dout, w["w_out"], gate, up, name=f"{tag}_dgate_up")
    d_w_gate_t = _matmul(dgate, hn, trans_a=True, name=f"{tag}_dwgate")
    d_w_up_t = _matmul(dup, hn, trans_a=True, name=f"{tag}_dwup")
    dhn = _matmul(dgate, w_gate_t, name=f"{tag}_dhn_gate")
    dhn = _matmul(dup, w_up_t, res=dhn, name=f"{tag}_dhn_up")
    dx, dnorm = _rmsnorm_bwd(x, w["norm"], dhn, dout, name=f"{tag}_dnorm")
    return dx, {"norm": dnorm, "w_in_t": jnp.concatenate([d_w_gate_t, d_w_up_t], axis=0), "w_out": d_w_out}


def _even_fwd(x, w, tabs, slopes, tag, gather=None, late=None):
    s = x.shape[0]
    (mla_c, mla_s), _ = tabs
    hn = _rmsnorm(x, w["norm"], out_dtype=_ACT, name=f"{tag}_norm")
    za = _matmul(hn, w["w_a"], name=f"{tag}_in_a")
    zb = _matmul(hn, w["w_b"], out_dtype=_ACT, name=f"{tag}_in_b")
    cq, ckv, kr = za[:, :MLA_Q_RANK], za[:, MLA_Q_RANK:MLA_Q_RANK + MLA_KV_RANK], za[:, MLA_Q_RANK + MLA_KV_RANK:]
    cqn = _rmsnorm(cq, w["q_norm"], out_dtype=_ACT, name=f"{tag}_qnorm")
    ckvn = _rmsnorm(ckv, w["kv_norm"], out_dtype=_ACT, name=f"{tag}_kvnorm")
    q = _matmul(cqn, w["w_uq"], scale=(MLA_NOPE + MLA_ROPE) ** -0.5 * LOG2_E,
                name=f"{tag}_uq")
    kv = _matmul(ckvn, w["w_ukv"], out_dtype=_ACT, name=f"{tag}_ukv")
    nn = MLA_HEADS * MLA_NOPE
    q_rope = _rope(q[:, nn:], mla_c, mla_s, out_dtype=_ACT, name=f"{tag}_rope_q")
    k_rope = _rope(kr, mla_c, mla_s, out_dtype=_ACT, name=f"{tag}_rope_k")[:, :MLA_ROPE]
    qh = jnp.concatenate([_heads_t(q[:, :nn].astype(_ACT), MLA_HEADS), _heads_t(q_rope, MLA_HEADS)], axis=1)
    kh = jnp.concatenate([_heads(kv[:, :nn], MLA_HEADS),
                          jnp.broadcast_to(k_rope[None], (MLA_HEADS, s, MLA_ROPE))], axis=-1)
    vh = _heads(kv[:, nn:], MLA_HEADS)
    oa, lse_a, *gathered = _attn_fwd(qh, kh, _chunk_t(kv[:, nn:], MLA_HEADS), out_dtype=_ACT, name=f"{tag}_mla",
                                     gather=gather)
    if late is not None:
        w = late(gathered[0])

    ng = 3 * DIL_SLOTS * DIL_HEAD_DIM
    outs, lses, dil_saved = [], [], []
    for gi, (_, dil) in enumerate(DIL_PAIRS):
        hs = slice(gi * DIL_SLOTS, (gi + 1) * DIL_SLOTS)
        qkv = _to_res(zb[:, gi * ng:(gi + 1) * ng], dil)
        o, l = _win_fwd(qkv, slopes[hs], dil=dil, name=f"{tag}_dil{gi}")
        dil_saved.append((qkv, l))
        outs.append(_res_to_tok(o, dil))
        lses.append(_res_to_tok(l, dil))
    o3, lse3 = tuple(outs), tuple(lses)
    comb = _merge_fwd(o3, lse3, name=f"{tag}_merge")
    cat = jnp.concatenate([_unheads_t(oa), _unheads(comb)], axis=-1)
    out = _matmul(cat, w["w_out"], res=x, name=f"{tag}_out")
    saved = (x, hn, cq, ckv, cqn, ckvn, qh, kh, vh, oa, lse_a, dil_saved, o3, lse3, cat)
    return out, saved, (gathered[0] if gathered else None)


def _even_bwd(dout, saved, w, tabs, slopes, tag, exchange=None):
    x, hn, cq, ckv, cqn, ckvn, qh, kh, vh, oa, lse_a, dil_saved, o3, lse3, cat = saved
    (mla_c, mla_s), _ = tabs
    nn = MLA_HEADS * MLA_NOPE
    dcat = _matmul(dout, w["w_out"], trans_b=True, out_dtype=_ACT, name=f"{tag}_dcat")
    d_w_out = _matmul(cat, dout, trans_a=True, name=f"{tag}_dwout")
    nv = MLA_HEADS * MLA_V

    doa = _heads(dcat[:, :nv], MLA_HEADS)
    dqh, dkt, dvt, *received = _attn_bwd(qh, kh, vh, oa, doa, lse_a, scale=(MLA_NOPE + MLA_ROPE) ** -0.5,
                                         name=f"{tag}_mla_bwd",
                                         exchange=None if exchange is None else exchange(d_w_out))
    dq_rope = _rope(_unheads(dqh[..., MLA_NOPE:]), mla_c, -mla_s, out_dtype=_ACT, name=f"{tag}_drope_q")
    dq = jnp.concatenate([_unheads(dqh[..., :MLA_NOPE]).astype(_ACT), dq_rope], axis=-1)
    dk3 = _unchunk(dkt).reshape(-1, MLA_HEADS, MLA_NOPE + MLA_ROPE)
    dkr = _rope(dk3[..., MLA_NOPE:].reshape(-1, MLA_HEADS * MLA_ROPE), mla_c, -mla_s, out_dtype=_F32,
                sum_chunks=True, name=f"{tag}_drope_k")
    dkv = jnp.concatenate([dk3[..., :MLA_NOPE].reshape(-1, nn), _unchunk(dvt)], axis=-1).astype(_ACT)
    d_w_uq = _matmul(cqn, dq, trans_a=True, name=f"{tag}_dwuq")
    d_w_ukv = _matmul(ckvn, dkv, trans_a=True, name=f"{tag}_dwukv")
    dcqn = _matmul(dq, w["w_uq"], trans_b=True, name=f"{tag}_dcqn")
    dckvn = _matmul(dkv, w["w_ukv"], trans_b=True, name=f"{tag}_dckvn")
    dcq, d_q_norm = _rmsnorm_bwd(cq, w["q_norm"], dcqn, out_dtype=_ACT, name=f"{tag}_dqnorm")
    dckv, d_kv_norm = _rmsnorm_bwd(ckv, w["kv_norm"], dckvn, out_dtype=_ACT, name=f"{tag}_dkvnorm")
    lane = jnp.arange(LANES) < MLA_ROPE
    dza = jnp.concatenate([dcq, dckv, jnp.where(lane[None], dkr, 0.0).astype(_ACT)], axis=-1)

    dcomb = _heads(dcat[:, nv:], DIL_SLOTS)
    do3, delta3 = _merge_bwd(dcomb, o3, lse3, name=f"{tag}_dmerge")
    dslabs = []
    for gi, (_, dil) in enumerate(DIL_PAIRS):
        hs = slice(gi * DIL_SLOTS, (gi + 1) * DIL_SLOTS)
        qkv, l = dil_saved[gi]
        grads = (qkv, _tok_to_res(do3[gi], dil), l, _tok_to_res(delta3[gi], dil), slopes[hs])
        a = _win_bwd_dq(*grads, dil=dil, name=f"{tag}_dil{gi}_dq")
        bc = _win_bwd_dkv(*grads, dil=dil, name=f"{tag}_dil{gi}_dkv")
        dslabs.append(_from_res(jnp.concatenate([a, bc], axis=0), dil))
    dzb = jnp.concatenate(dslabs, axis=-1)

    d_w_a = _matmul(hn, dza, trans_a=True, name=f"{tag}_dwa")
    d_w_b = _matmul(hn, dzb, trans_a=True, name=f"{tag}_dwb")
    dhn = _matmul(dza, w["w_a"], trans_b=True, name=f"{tag}_dhn_a")
    dhn = _matmul(dzb, w["w_b"], trans_b=True, res=dhn, name=f"{tag}_dhn_b")
    dx, dnorm = _rmsnorm_bwd(x, w["norm"], dhn, dout, name=f"{tag}_dnorm")
    grads = {"norm": dnorm, "w_a": d_w_a, "w_b": d_w_b, "q_norm": d_q_norm, "kv_norm": d_kv_norm,
             "w_uq": d_w_uq, "w_ukv": d_w_ukv, "w_out": d_w_out}
    return dx, grads, (received[0] if received else None)


def _odd_fwd(x, w, tabs, tag):
    s = x.shape[0]
    _, (gqa_c, gqa_s) = tabs
    nk = GQA_KV_HEADS * GQA_HEAD_DIM
    hn = _rmsnorm(x, w["norm"], out_dtype=_ACT, name=f"{tag}_norm")
    q = _matmul(hn, w["w_q"], name=f"{tag}_q")
    kv = _matmul(hn, w["w_kv"], name=f"{tag}_kv")
    k = kv[:, :nk]
    qh = _heads_t(_headnorm_rope(q, w["q_norm"], gqa_c, gqa_s, scale=GQA_HEAD_DIM ** -0.5 * LOG2_E,
                                 name=f"{tag}_prep_q"), GQA_HEADS)
    kh = _heads(_headnorm_rope(k, w["k_norm"], gqa_c, gqa_s, name=f"{tag}_prep_k"), GQA_KV_HEADS)
    v = kv[:, nk:].astype(_ACT)
    vh = _heads(v, GQA_KV_HEADS)
    o, lse = _attn_fwd(qh, kh, _chunk_t(v, GQA_KV_HEADS), out_dtype=_ACT, name=f"{tag}_gqa")
    ocat = _unheads_t(o)
    out = _matmul(ocat, w["w_o"], res=x, name=f"{tag}_out")
    return out, (x, hn, q, k, qh, kh, vh, o, lse, ocat)


def _odd_bwd(dout, saved, w, tabs, tag):
    x, hn, q, k, qh, kh, vh, o, lse, ocat = saved
    s = x.shape[0]
    _, (gqa_c, gqa_s) = tabs
    docat = _matmul(dout, w["w_o"], trans_b=True, out_dtype=_ACT, name=f"{tag}_docat")
    d_w_o = _matmul(ocat, dout, trans_a=True, name=f"{tag}_dwo")
    doh = _heads(docat, GQA_HEADS)
    dqh, dkt, dvt = _attn_bwd(qh, kh, vh, o, doh, lse, scale=GQA_HEAD_DIM ** -0.5, name=f"{tag}_gqa_bwd")
    dq, d_q_norm = _headnorm_rope_bwd(q, w["q_norm"], _unheads(dqh), gqa_c, gqa_s, name=f"{tag}_dprep_q")
    dk, d_k_norm = _headnorm_rope_bwd(k, w["k_norm"], _unchunk(dkt), gqa_c, gqa_s, name=f"{tag}_dprep_k")
    dkv = jnp.concatenate([dk, _unchunk(dvt).astype(_ACT)], axis=-1)
    d_w_q = _matmul(hn, dq, trans_a=True, name=f"{tag}_dwq")
    d_w_kv = _matmul(hn, dkv, trans_a=True, name=f"{tag}_dwkv")
    dhn = _matmul(dq, w["w_q"], trans_b=True, name=f"{tag}_dhn_q")
    dhn = _matmul(dkv, w["w_kv"], trans_b=True, res=dhn, name=f"{tag}_dhn_kv")
    dx, dnorm = _rmsnorm_bwd(x, w["norm"], dhn, dout, name=f"{tag}_dnorm")
    return dx, {"norm": dnorm, "w_q": d_w_q, "w_kv": d_w_kv, "q_norm": d_q_norm, "k_norm": d_k_norm, "w_o": d_w_o}


def _split_heads_cols(wm, heads, first):
    rws = wm.shape[0]
    w3 = wm.reshape(rws, heads, -1)
    return jnp.concatenate([w3[:, :, :first].reshape(rws, -1), w3[:, :, first:].reshape(rws, -1)], axis=-1)


def _merge_heads_cols(wm, heads, first):
    rws, cols = wm.shape
    a = wm[:, :heads * first].reshape(rws, heads, first)
    b = wm[:, heads * first:].reshape(rws, heads, cols // heads - first)
    return jnp.concatenate([a, b], axis=-1).reshape(rws, cols)


def _layer_weights(full, gains, layer):
    i = layer // 2

    def stacked(name, idx):
        for group, arrays in zip((_FIRST, _REST), full):
            lo, cnt = group.get(name, (0, 0))
            if lo <= idx < lo + cnt:
                return None if arrays is None else arrays[name][idx - lo]
        raise KeyError((name, idx))

    ffn = {"norm": gains["ffn_norm"][layer], "w_in_t": stacked("ffn_w_in", layer), "w_out": stacked("ffn_w_out", layer)}
    if layer % 2 == 0:
        w_in = stacked("w_in_ab", i)
        mix = {"norm": gains["mix_norm_ab"][i],
               "w_a": jnp.pad(w_in[:, :IN_A], ((0, 0), (0, IN_A_PAD - IN_A))), "w_b": _group_major(w_in[:, IN_A:]),
               "q_norm": gains["mla_q_norm"][i], "kv_norm": gains["mla_kv_norm"][i],
               "w_uq": _split_heads_cols(stacked("mla_w_uq", i), MLA_HEADS, MLA_NOPE),
               "w_ukv": _split_heads_cols(stacked("mla_w_ukv", i), MLA_HEADS, MLA_NOPE),
               "w_out": stacked("w_out_ab", i)}
    else:
        mix = {"norm": gains["mix_norm_c"][i], "w_q": stacked("gqa_w_q", i), "w_kv": stacked("gqa_w_kv", i),
               "q_norm": gains["gqa_q_norm"][i], "k_norm": gains["gqa_k_norm"][i], "w_o": stacked("gqa_w_o", i)}
    return mix, ffn


def _pack_grads(grads, group):
    parts = [_split_for_devices(grads[n][group[n][0]:group[n][0] + group[n][1]], axis)
             for n, axis in _BIG if n in group]
    fill = -sum(p.shape[1] for p in parts) % PACK_ROWS
    return jnp.concatenate(parts + [jnp.zeros((N_DEV, fill, PACK_COLS), parts[0].dtype)], axis=1)


def kernel(x, mix_norm_ab, w_in_ab, mla_q_norm, mla_kv_norm, mla_w_uq, mla_w_ukv, w_out_ab, mix_norm_c, gqa_w_q, gqa_w_kv, gqa_q_norm, gqa_k_norm, gqa_w_o, ffn_norm, ffn_w_in, ffn_w_out, final_norm, loss_target, m_mix_norm_ab, m_w_in_ab, m_mla_q_norm, m_mla_kv_norm, m_mla_w_uq, m_mla_w_ukv, m_w_out_ab, m_mix_norm_c, m_gqa_w_q, m_gqa_w_kv, m_gqa_q_norm, m_gqa_k_norm, m_gqa_w_o, m_ffn_norm, m_ffn_w_in, m_ffn_w_out, m_final_norm, v_mix_norm_ab, v_w_in_ab, v_mla_q_norm, v_mla_kv_norm, v_mla_w_uq, v_mla_w_ukv, v_w_out_ab, v_mix_norm_c, v_gqa_w_q, v_gqa_w_kv, v_gqa_q_norm, v_gqa_k_norm, v_gqa_w_o, v_ffn_norm, v_ffn_w_in, v_ffn_w_out, v_final_norm):
    wts = dict(mix_norm_ab=mix_norm_ab, w_in_ab=w_in_ab, mla_q_norm=mla_q_norm, mla_kv_norm=mla_kv_norm,
               mla_w_uq=mla_w_uq, mla_w_ukv=mla_w_ukv, w_out_ab=w_out_ab, mix_norm_c=mix_norm_c, gqa_w_q=gqa_w_q,
               gqa_w_kv=gqa_w_kv, gqa_q_norm=gqa_q_norm, gqa_k_norm=gqa_k_norm, gqa_w_o=gqa_w_o, ffn_norm=ffn_norm,
               ffn_w_in=ffn_w_in, ffn_w_out=ffn_w_out, final_norm=final_norm)
    mom = dict(mix_norm_ab=m_mix_norm_ab, w_in_ab=m_w_in_ab, mla_q_norm=m_mla_q_norm, mla_kv_norm=m_mla_kv_norm,
               mla_w_uq=m_mla_w_uq, mla_w_ukv=m_mla_w_ukv, w_out_ab=m_w_out_ab, mix_norm_c=m_mix_norm_c,
               gqa_w_q=m_gqa_w_q, gqa_w_kv=m_gqa_w_kv, gqa_q_norm=m_gqa_q_norm, gqa_k_norm=m_gqa_k_norm,
               gqa_w_o=m_gqa_w_o, ffn_norm=m_ffn_norm, ffn_w_in=m_ffn_w_in, ffn_w_out=m_ffn_w_out,
               final_norm=m_final_norm)
    var = dict(mix_norm_ab=v_mix_norm_ab, w_in_ab=v_w_in_ab, mla_q_norm=v_mla_q_norm, mla_kv_norm=v_mla_kv_norm,
               mla_w_uq=v_mla_w_uq, mla_w_ukv=v_mla_w_ukv, w_out_ab=v_w_out_ab, mix_norm_c=v_mix_norm_c,
               gqa_w_q=v_gqa_w_q, gqa_w_kv=v_gqa_w_kv, gqa_q_norm=v_gqa_q_norm, gqa_k_norm=v_gqa_k_norm,
               gqa_w_o=v_gqa_w_o, ffn_norm=v_ffn_norm, ffn_w_in=v_ffn_w_in, ffn_w_out=v_ffn_w_out,
               final_norm=v_final_norm)
    big_names = [n for n, _ in _BIG]
    groups = (_FIRST, _REST)
    big_w, big_m, big_v = (_stored({n: src[n] for n in big_names}) for src in (wts, mom, var))
    w_grp = [_group_of(big_w, grp) for grp in groups]
    shapes = [{n: a.shape for n, a in w.items()} for w in w_grp]
    w_packs = [_pack_big(w) for w in w_grp]
    n_rows = [p.shape[0] for p in w_packs]
    xs = x[0]
    s = xs.shape[0]
    me = 4 * lax.axis_index("x") + 2 * lax.axis_index("y") + lax.axis_index("c")
    c_cols = mix_norm_c.shape[1]

    tail = jnp.pad(mix_norm_c.reshape(-1), (0, GATHER_TAIL_ROWS // 2 * PACK_COLS - mix_norm_c.size))
    tail = _to_bits16(tail.reshape(GATHER_TAIL_ROWS // 2, PACK_COLS))
    gathered = _all_gather(jnp.concatenate([w_packs[0].astype(jnp.bfloat16), tail], axis=0))
    full = [_unpack_gathered(gathered[:, :n_rows[0]], shapes[0]), None]
    c_all = _from_bits16(gathered[:, n_rows[0]:]).reshape(N_DEV, -1)[:, :mix_norm_c.size]
    c_full = c_all.reshape(N_DEV, 2, c_cols).transpose(1, 0, 2).reshape(2, N_DEV * c_cols)
    gains = dict(mix_norm_ab=mix_norm_ab, mla_q_norm=mla_q_norm, mla_kv_norm=mla_kv_norm, mix_norm_c=c_full,
                 gqa_q_norm=gqa_q_norm, gqa_k_norm=gqa_k_norm, ffn_norm=ffn_norm)

    tabs = _rope_tables(s)
    slopes = jnp.exp2(-8.0 * jnp.arange(1, DIL_HEADS + 1, dtype=_F32) / DIL_HEADS)

    h = xs
    saved = []
    for layer in range(DEPTH):
        mix_w, ffn_w = _layer_weights(full, gains, layer)
        if layer == 0:
            def late(rest):
                full[1] = _unpack_gathered(rest, shapes[1])
                return _layer_weights(full, gains, 0)[0]

            h, sv_mix, _ = _even_fwd(h, mix_w, tabs, slopes, f"l{layer}_mix", gather=w_packs[1].astype(jnp.bfloat16),
                                     late=late)
            mix_w, ffn_w = _layer_weights(full, gains, 0)
        elif layer % 2 == 0:
            h, sv_mix, _ = _even_fwd(h, mix_w, tabs, slopes, f"l{layer}_mix")
        else:
            h, sv_mix = _odd_fwd(h, mix_w, tabs, f"l{layer}_mix")
        h, sv_ffn = _ffn_fwd(h, ffn_w, f"l{layer}_ffn")
        saved.append((mix_w, ffn_w, sv_mix, sv_ffn))
    loss_local, dh, d_final = _final_loss(h, final_norm, loss_target[0], name="final_loss")

    gfull = {n: [None] * wts[n].shape[0] for n in big_names}
    gsmall = {n: [None] * (wts[n].shape[0] if wts[n].ndim > 1 else 1) for n in _SMALL}
    gsmall["final_norm"][0] = d_final
    for layer in reversed(range(DEPTH)):
        mix_w, ffn_w, sv_mix, sv_ffn = saved[layer]
        i = layer // 2
        dh, gf = _ffn_bwd(dh, sv_ffn, ffn_w, f"l{layer}_ffn")
        gsmall["ffn_norm"][layer] = gf["norm"]
        gfull["ffn_w_in"][layer] = gf["w_in_t"]
        gfull["ffn_w_out"][layer] = gf["w_out"]
        if layer % 2 == 0:
            def sending(d_w_out):
                gfull["w_out_ab"][0] = d_w_out
                return _pack_grads(gfull, _REST).astype(jnp.bfloat16)

            dh, gm, got = _even_bwd(dh, sv_mix, mix_w, tabs, slopes, f"l{layer}_mix",
                                    exchange=sending if layer == 0 else None)
            if layer == 0:
                received_rest = got
            gsmall["mix_norm_ab"][i] = gm["norm"]
            gsmall["mla_q_norm"][i] = gm["q_norm"]
            gsmall["mla_kv_norm"][i] = gm["kv_norm"]
            gfull["w_in_ab"][i] = jnp.concatenate([gm["w_a"][:, :IN_A], _group_major(gm["w_b"])], axis=-1)
            gfull["mla_w_uq"][i] = _merge_heads_cols(gm["w_uq"], MLA_HEADS, MLA_NOPE)
            gfull["mla_w_ukv"][i] = _merge_heads_cols(gm["w_ukv"], MLA_HEADS, MLA_NOPE)
            gfull["w_out_ab"][i] = gm["w_out"]
        else:
            dh, gm = _odd_bwd(dh, sv_mix, mix_w, tabs, f"l{layer}_mix")
            gsmall["mix_norm_c"][i] = gm["norm"]
            gsmall["gqa_q_norm"][i] = gm["q_norm"]
            gsmall["gqa_k_norm"][i] = gm["k_norm"]
            gfull["gqa_w_q"][i] = gm["w_q"]
            gfull["gqa_w_kv"][i] = gm["w_kv"]
            gfull["gqa_w_o"][i] = gm["w_o"]
    grad_x = dh[None]

    small_part = _pack_small({n: jnp.stack(gsmall[n]) for n in _SMALL})
    small_bits = jnp.broadcast_to(_to_bits16(small_part)[None], (N_DEV, EXCHANGE_TAIL_ROWS, PACK_COLS))
    received_first = _exchange(jnp.concatenate([_pack_grads(gfull, _FIRST).astype(jnp.bfloat16), small_bits], axis=1))
    big_packs = [_reduce_adamw(got, w_packs[gi], _pack_big(_group_of(big_m, grp)), _pack_big(_group_of(big_v, grp)),
                               name=f"adamw_big{gi}")
                 for gi, (grp, got) in enumerate(zip(groups, (received_first, received_rest)))]

    def widen_c(shard):
        return lax.dynamic_update_slice(jnp.zeros((2, N_DEV * c_cols), _F32), shard, (0, me * c_cols))

    def small_of(src):
        return _pack_small({n: (widen_c(src[n]) if n == "mix_norm_c" else src[n]) for n in _SMALL})

    small_recv = _from_bits16(received_first[:, n_rows[0]:])
    small_packs = _reduce_adamw(small_recv, small_of(wts), small_of(mom), small_of(var), name="adamw_small")

    def outputs_of(which):
        by_group = [_unpack_big(big_packs[gi][which], shapes[gi]) for gi in range(len(groups))]
        small = _unpack_small(small_packs[which])
        res = _stored({n: jnp.concatenate([grp[n] for grp in by_group if n in grp], axis=0) for n in big_names})
        for n in wts:
            if n in big_names:
                continue
            if n == "mix_norm_c":
                res[n] = lax.dynamic_slice(small[n].reshape(2, N_DEV * c_cols), (0, me * c_cols), (2, c_cols))
            else:
                res[n] = small[n].reshape(wts[n].shape)
        return [res[n] for n in wts]

    loss = lax.psum(loss_local, _AXES)
    return (loss, grad_x, *outputs_of(0), *outputs_of(1), *outputs_of(2), *outputs_of(3))
```

```python
import functools

import jax
import jax.numpy as jnp
from jax import lax
from jax.experimental import pallas as pl
from jax.experimental.pallas import tpu as pltpu

D_MODEL = 1024
DEPTH = 4
GRID_W = 64
NORM_EPS = 1e-6
ROPE_THETA = 10000.0
NEG_INF = -1e30
MLA_HEADS = 8
MLA_Q_RANK = 384
MLA_KV_RANK = 256
MLA_NOPE = 64
MLA_ROPE = 32
MLA_V = 64
DIL_PAIRS = ((128, 1), (512, 4), (2048, 16))
DIL_HALF = 64
DIL_SLOTS = 4
DIL_GROUPS = 3
DIL_HEADS = 12
DIL_HEAD_DIM = 64
GQA_HEADS = 16
GQA_KV_HEADS = 4
GQA_HEAD_DIM = 64
FFN_HIDDEN = 2816
IN_A = MLA_Q_RANK + MLA_KV_RANK + MLA_ROPE
IN_A_PAD = 768
IN_B = 3 * DIL_HEADS * DIL_HEAD_DIM
ADAM_LR = 0.001
ADAM_B1 = 0.9
ADAM_B2 = 0.999
ADAM_EPS = 1e-08
ADAM_WD = 0.01
ADAM_STEP = 10

LANES = 128
SUBLANES_16BIT = 16
VMEM_LIMIT_BYTES = 56 * 1024 * 1024

MM_TILE = 1408
ROW_BLOCK = 512
ROT_CHUNK = 32

N_DEV = 8
PACK_COLS = 1024

_MXU = jnp.bfloat16
_ACT = jnp.bfloat16
_F32 = jnp.float32

_AXES = ("x", "y", "c")


def _params(sem):
    return pltpu.CompilerParams(dimension_semantics=sem, vmem_limit_bytes=VMEM_LIMIT_BYTES)


def _pick(n, cap):
    for t in range(cap - cap % LANES, 0, -LANES):
        if n % t == 0:
            return t
    return n


def _rows(m, target):
    t = m
    while t > target and t % 2 == 0:
        t //= 2
    return t


def _matmul(a, b, *, trans_a=False, trans_b=False, res=None, scale=None, out_dtype=_F32, name):
    if trans_a:
        k, m = a.shape
    else:
        m, k = a.shape
    if trans_b:
        n, kb = b.shape
    else:
        kb, n = b.shape
    assert k == kb, (a.shape, b.shape)
    tm, tn, tk = _pick(m, MM_TILE), _pick(n, MM_TILE), _pick(k, MM_TILE)
    nk = k // tk
    dims = (((0 if trans_a else 1,), (1 if trans_b else 0,)), ((), ()))

    def body(*refs):
        if res is None:
            a_ref, b_ref, o_ref, acc = refs
            r_ref = None
        else:
            a_ref, b_ref, r_ref, o_ref, acc = refs
        kk = pl.program_id(2)

        @pl.when(kk == 0)
        def _():
            acc[...] = jnp.zeros_like(acc)

        acc[...] += lax.dot_general(a_ref[...].astype(_MXU), b_ref[...].astype(_MXU), dims,
                                    preferred_element_type=_F32)

        @pl.when(kk == nk - 1)
        def _():
            r = acc[...]
            if scale is not None:
                r = r * scale
            if r_ref is not None:
                r = r + r_ref[...].astype(_F32)
            o_ref[...] = r.astype(out_dtype)

    a_spec = (pl.BlockSpec((tk, tm), lambda i, j, kk: (kk, i)) if trans_a
              else pl.BlockSpec((tm, tk), lambda i, j, kk: (i, kk)))
    b_spec = (pl.BlockSpec((tn, tk), lambda i, j, kk: (j, kk)) if trans_b
              else pl.BlockSpec((tk, tn), lambda i, j, kk: (kk, j)))
    o_spec = pl.BlockSpec((tm, tn), lambda i, j, kk: (i, j))
    in_specs = [a_spec, b_spec] + ([o_spec] if res is not None else [])
    args = (a, b) + ((res,) if res is not None else ())
    return pl.pallas_call(
        body, name=name, grid=(m // tm, n // tn, nk),
        in_specs=in_specs, out_specs=o_spec,
        out_shape=jax.ShapeDtypeStruct((m, n), out_dtype),
        scratch_shapes=[pltpu.VMEM((tm, tn), _F32)],
        compiler_params=_params(("parallel", "parallel", "arbitrary")),
    )(*args)


def _rmsnorm(x, g, *, out_dtype, name, rows=ROW_BLOCK):
    m, d = x.shape
    tm = _rows(m, rows)

    def body(x_ref, g_ref, o_ref):
        xf = x_ref[...].astype(_F32)
        r = lax.rsqrt(jnp.mean(xf * xf, axis=-1, keepdims=True) + NORM_EPS)
        o_ref[...] = ((xf * r) * g_ref[...]).astype(out_dtype)

    return pl.pallas_call(
        body, name=name, grid=(m // tm,),
        in_specs=[pl.BlockSpec((tm, d), lambda i: (i, 0)), pl.BlockSpec((1, d), lambda i: (0, 0))],
        out_specs=pl.BlockSpec((tm, d), lambda i: (i, 0)),
        out_shape=jax.ShapeDtypeStruct((m, d), out_dtype),
        compiler_params=_params(("parallel",)),
    )(x, g.reshape(1, d).astype(_F32))


def _rmsnorm_bwd(x, g, dy, dres=None, *, out_dtype=_F32, name, rows=ROW_BLOCK):
    m, d = x.shape
    tm = _rows(m, rows)

    def body(*refs):
        if dres is None:
            x_ref, g_ref, dy_ref, dx_ref, dg_ref = refs
            r_ref = None
        else:
            x_ref, g_ref, dy_ref, r_ref, dx_ref, dg_ref = refs

        @pl.when(pl.program_id(0) == 0)
        def _():
            dg_ref[...] = jnp.zeros_like(dg_ref)

        xf = x_ref[...].astype(_F32)
        r = lax.rsqrt(jnp.mean(xf * xf, axis=-1, keepdims=True) + NORM_EPS)
        xh = xf * r
        dyf = dy_ref[...].astype(_F32)
        dg_ref[...] += jnp.sum(dyf * xh, axis=0, keepdims=True)
        gdy = dyf * g_ref[...]
        dx = r * (gdy - xh * jnp.mean(gdy * xh, axis=-1, keepdims=True))
        if r_ref is not None:
            dx = dx + r_ref[...].astype(_F32)
        dx_ref[...] = dx.astype(out_dtype)

    row = pl.BlockSpec((tm, d), lambda i: (i, 0))
    vec = pl.BlockSpec((1, d), lambda i: (0, 0))
    in_specs = [row, vec, row] + ([row] if dres is not None else [])
    args = (x, g.reshape(1, d).astype(_F32), dy) + ((dres,) if dres is not None else ())
    dx, dg = pl.pallas_call(
        body, name=name, grid=(m // tm,),
        in_specs=in_specs, out_specs=[row, vec],
        out_shape=[jax.ShapeDtypeStruct((m, d), out_dtype), jax.ShapeDtypeStruct((1, d), _F32)],
        compiler_params=_params(("arbitrary",)),
    )(*args)
    return dx, dg.reshape(d)


def _rotate(xf, c, sn):
    w = xf.shape[1]
    half = ROT_CHUNK // 2
    if w > LANES:
        c, sn = jnp.tile(c, (1, w // LANES)), jnp.tile(sn, (1, w // LANES))
    lane = lax.broadcasted_iota(jnp.int32, xf.shape, 1)
    sw = jnp.where((lane & (ROT_CHUNK - 1)) < half, pltpu.roll(xf, w - half, 1), pltpu.roll(xf, half, 1))
    return xf * c + sw * sn


def _seg_mean(v, seg_ref):
    outs = []
    for c in range(v.shape[1] // LANES):
        piece = v[:, c * LANES:(c + 1) * LANES]
        hi = piece.astype(jnp.bfloat16)
        lo = (piece - hi.astype(_F32)).astype(jnp.bfloat16)
        outs.append(jnp.dot(hi, seg_ref[...], preferred_element_type=_F32)
                    + jnp.dot(lo, seg_ref[...], preferred_element_type=_F32))
    return jnp.concatenate(outs, axis=1) if len(outs) > 1 else outs[0]


def _seg_matrix():
    lane = jnp.arange(LANES) // GQA_HEAD_DIM
    return ((lane[:, None] == lane[None, :]).astype(_F32) / GQA_HEAD_DIM).astype(jnp.bfloat16)


def _headnorm_rope(x, gain, cos_t, sin_t, *, scale=None, name):
    s, w = x.shape
    ts = _rows(s, ROW_BLOCK)

    def body(x_ref, g_ref, seg_ref, c_ref, s_ref, o_ref):
        xf = x_ref[...]
        r = lax.rsqrt(_seg_mean(xf * xf, seg_ref) + NORM_EPS)
        y = _rotate((xf * r) * g_ref[...], c_ref[...], s_ref[...])
        if scale is not None:
            y = y * scale
        o_ref[...] = y.astype(o_ref.dtype)

    row = pl.BlockSpec((ts, w), lambda i: (i, 0))
    tab = pl.BlockSpec((ts, LANES), lambda i: (i, 0))
    return pl.pallas_call(
        body, name=name, grid=(s // ts,),
        in_specs=[row, pl.BlockSpec((1, w), lambda i: (0, 0)), pl.BlockSpec((LANES, LANES), lambda i: (0, 0)), tab, tab],
        out_specs=row, out_shape=jax.ShapeDtypeStruct((s, w), _ACT),
        compiler_params=_params(("parallel",)),
    )(x, jnp.tile(gain.astype(_F32), w // GQA_HEAD_DIM).reshape(1, w), _seg_matrix(), cos_t, sin_t)


def _headnorm_rope_bwd(x, gain, dy, cos_t, sin_t, *, name):
    s, w = x.shape
    ts = _rows(s, ROW_BLOCK)

    def body(x_ref, g_ref, seg_ref, c_ref, s_ref, dy_ref, dx_ref, dg_ref):
        @pl.when(pl.program_id(0) == 0)
        def _():
            dg_ref[...] = jnp.zeros_like(dg_ref)

        xf = x_ref[...]
        r = lax.rsqrt(_seg_mean(xf * xf, seg_ref) + NORM_EPS)
        xh = xf * r
        dyn = _rotate(dy_ref[...].astype(_F32), c_ref[...], -s_ref[...])
        dg_ref[...] += jnp.sum(dyn * xh, axis=0, keepdims=True)
        gdy = dyn * g_ref[...]
        dx_ref[...] = (r * (gdy - xh * _seg_mean(gdy * xh, seg_ref))).astype(dx_ref.dtype)

    row = pl.BlockSpec((ts, w), lambda i: (i, 0))
    vec = pl.BlockSpec((1, w), lambda i: (0, 0))
    tab = pl.BlockSpec((ts, LANES), lambda i: (i, 0))
    dx, dg = pl.pallas_call(
        body, name=name, grid=(s // ts,),
        in_specs=[row, vec, pl.BlockSpec((LANES, LANES), lambda i: (0, 0)), tab, tab, row],
        out_specs=[row, vec],
        out_shape=[jax.ShapeDtypeStruct((s, w), _ACT), jax.ShapeDtypeStruct((1, w), _F32)],
        compiler_params=_params(("arbitrary",)),
    )(x, jnp.tile(gain.astype(_F32), w // GQA_HEAD_DIM).reshape(1, w), _seg_matrix(), cos_t, sin_t, dy)
    return dx, dg.reshape(w // GQA_HEAD_DIM, GQA_HEAD_DIM).sum(axis=0)


def _rope(x, cos_t, sin_t, *, out_dtype, name, sum_chunks=False, scale=None):
    s, w = x.shape
    assert w % LANES == 0
    ts = _rows(s, ROW_BLOCK)
    ow = LANES if sum_chunks else w

    def body(x_ref, c_ref, s_ref, o_ref):
        y = _rotate(x_ref[...].astype(_F32), c_ref[...], s_ref[...])
        if scale is not None:
            y = y * scale
        if sum_chunks:
            shift = w // 2
            while shift >= ROT_CHUNK:
                y = y + pltpu.roll(y, shift, 1)
                shift //= 2
            y = y[:, :LANES]
        o_ref[...] = y.astype(out_dtype)

    return pl.pallas_call(
        body, name=name, grid=(s // ts,),
        in_specs=[pl.BlockSpec((ts, w), lambda i: (i, 0)), pl.BlockSpec((ts, LANES), lambda i: (i, 0)),
                  pl.BlockSpec((ts, LANES), lambda i: (i, 0))],
        out_specs=pl.BlockSpec((ts, ow), lambda i: (i, 0)),
        out_shape=jax.ShapeDtypeStruct((s, ow), out_dtype),
        compiler_params=_params(("parallel",)),
    )(x, cos_t, sin_t)


_NT = (((1,), (1,)), ((), ()))
_TN = (((0,), (0,)), ((), ()))
LOG2_E = 1.4426950408889634
LN_2 = 0.6931471805599453
ATTN_FWD_ROWS = 2048
ATTN_ROWS = 512
ATTN_CHAINS = 2
ATTN_FWD_KEYS = 1024
ATTN_BWD_KEYS = 1024


def _grid_step(dims):
    step, total = 0, 1
    for axis, n in enumerate(dims):
        step = step * n + pl.program_id(axis)
        total *= n
    return step, total


def _attn_fwd(qt, k, vt, *, out_dtype, name, gather=None):
    h, dk, s = qt.shape
    g, nk, dv, tk = vt.shape
    assert nk * tk == s
    r = h // g
    rc = min(ATTN_FWD_ROWS, s)
    hp = min(r, ATTN_CHAINS)
    nrc = max(1, min(ATTN_CHAINS // hp, s // rc))
    tq = rc * nrc
    nhp, nq = r // hp, s // tq
    units = [(a, c) for a in range(hp) for c in range(nrc)]

    def body(q_ref, k_ref, v_ref, *rest):
        if gather is None:
            o_ref, lse_ref = rest
        else:
            x_ref, o_ref, lse_ref, all_ref, send_sems, recv_sems, local_sem = rest
            start, forward, finish = _gather_phases(x_ref, all_ref, send_sems, recv_sems, local_sem)
            step, total = _grid_step((g, nhp, nq))
            pl.when(step == 0)(start)
            pl.when(step == 3 * total // 4)(forward)

        qs = [q_ref[a, :, c * rc:(c + 1) * rc] for a, c in units]
        init = tuple((jnp.full((1, rc), NEG_INF, _F32), jnp.zeros((1, rc), _F32), jnp.zeros((dv, rc), _F32))
                     for _ in units)

        def trip(j, carry):
            kb, vb = k_ref[0, pl.ds(pl.multiple_of(j * tk, tk), tk), :], v_ref[0, j]
            out = []
            for u in range(len(units)):
                m, l, acc = carry[u]
                sc = jnp.dot(kb, qs[u], preferred_element_type=_F32)
                m_new = jnp.maximum(m, jnp.max(sc, axis=0, keepdims=True))
                p = jnp.exp2(sc - m_new)
                alpha = jnp.exp2(m - m_new)
                l = alpha * l + jnp.sum(p, axis=0, keepdims=True)
                acc = alpha * acc + jnp.dot(vb, p.astype(_MXU), preferred_element_type=_F32)
                out.append((m_new, l, acc))
            return tuple(out)

        fin = lax.fori_loop(0, nk, trip, init)
        for u, (a, c) in enumerate(units):
            m, l, acc = fin[u]
            o_ref[a, :, c * rc:(c + 1) * rc] = (acc / l).astype(out_dtype)
            lse_ref[a, :, c * rc:(c + 1) * rc] = m + jnp.log2(l)

        if gather is not None:
            pl.when(step == total - 1)(finish)

    q_blk = lambda gg, hh, i: (gg * nhp + hh, 0, i)
    in_specs = [pl.BlockSpec((hp, dk, tq), q_blk), pl.BlockSpec((1, s, dk), lambda gg, hh, i: (gg, 0, 0)),
                pl.BlockSpec((1, nk, dv, tk), lambda gg, hh, i: (gg, 0, 0, 0))]
    out_specs = [pl.BlockSpec((hp, dv, tq), q_blk), pl.BlockSpec((hp, 1, tq), q_blk)]
    out_shape = [jax.ShapeDtypeStruct((h, dv, s), out_dtype), jax.ShapeDtypeStruct((h, 1, s), _F32)]
    if gather is None:
        return pl.pallas_call(
            body, name=name, grid=(g, nhp, nq), in_specs=in_specs, out_specs=out_specs, out_shape=out_shape,
            compiler_params=_params(("parallel", "parallel", "parallel")),
        )(qt, k, vt)
    return pl.pallas_call(
        body, name=name, grid=(g, nhp, nq),
        in_specs=in_specs + [pl.BlockSpec(memory_space=pl.ANY)],
        out_specs=out_specs + [pl.BlockSpec(memory_space=pl.ANY)],
        out_shape=out_shape + [jax.ShapeDtypeStruct((N_DEV,) + gather.shape, gather.dtype)],
        scratch_shapes=_comm_scratch(),
        compiler_params=_params(("arbitrary", "arbitrary", "arbitrary")),
    )(qt, k, vt, gather)


def _attn_bwd(qt, k, v, ot, do, lse2, *, scale, name, exchange=None):
    h, dk, s = qt.shape
    g, _, dv = v.shape
    r = h // g
    hp = min(r, ATTN_CHAINS)
    tq = min(ATTN_ROWS, s)
    tk = min(ATTN_BWD_KEYS * (ATTN_CHAINS // hp), s)
    nhp, nq, nk = r // hp, s // tq, s // tk

    def body(qt_ref, k_ref, v_ref, ot_ref, do_ref, lse_ref, *rest):
        if exchange is None:
            dq_ref, dk_ref, dv_ref = rest
        else:
            g_ref, dq_ref, dk_ref, dv_ref, got_ref, send_sems, recv_sems, local_sem = rest
            start, finish = _exchange_phases(g_ref, got_ref, send_sems, recv_sems, local_sem)
            step, total = _grid_step((g, nhp, nq))
            pl.when(step == 0)(start)
        hh, i = pl.program_id(1), pl.program_id(2)

        @pl.when((hh == 0) & (i == 0))
        def _():
            dk_ref[...] = jnp.zeros_like(dk_ref)
            dv_ref[...] = jnp.zeros_like(dv_ref)

        qts = [qt_ref[a] for a in range(hp)]
        qs = [qt_ref[a].T for a in range(hp)]
        dos = [do_ref[a] for a in range(hp)]
        dots = [do_ref[a].T for a in range(hp)]
        ls = [lse_ref[a].T for a in range(hp)]
        dls = [jnp.sum(do_ref[a].astype(_F32) * ot_ref[a].T.astype(_F32), axis=1, keepdims=True)
               for a in range(hp)]

        def trip(j, carry):
            rows = pl.ds(pl.multiple_of(j * tk, tk), tk)
            kb, vb = k_ref[0, rows, :], v_ref[0, rows, :]
            out = []
            for a in range(hp):
                p = jnp.exp2(lax.dot_general(qs[a], kb, _NT, preferred_element_type=_F32) - ls[a])
                dp = lax.dot_general(dos[a], vb, _NT, preferred_element_type=_F32)
                ds = (p * (dp - dls[a])).astype(_MXU)
                dv_ref[0, j] += jnp.dot(dots[a], p.astype(_MXU), preferred_element_type=_F32)
                dk_ref[0, j] += jnp.dot(qts[a], ds, preferred_element_type=_F32)
                out.append(carry[a] + jnp.dot(ds, kb, preferred_element_type=_F32))
            return tuple(out)

        fin = lax.fori_loop(0, nk, trip, tuple(jnp.zeros((tq, dk), _F32) for _ in range(hp)))
        for a in range(hp):
            dq_ref[a] = (fin[a] * scale).astype(dq_ref.dtype)

        @pl.when((hh == nhp - 1) & (i == nq - 1))
        def _():
            dk_ref[...] = dk_ref[...] * LN_2

        if exchange is not None:
            pl.when(step == total - 1)(finish)

    q_blk = lambda gg, hh, i: (gg * nhp + hh, i, 0)
    qt_blk = lambda gg, hh, i: (gg * nhp + hh, 0, i)
    kv_blk = lambda gg, hh, i: (gg, 0, 0)
    in_specs = [pl.BlockSpec((hp, dk, tq), qt_blk), pl.BlockSpec((1, s, dk), kv_blk), pl.BlockSpec((1, s, dv), kv_blk),
                pl.BlockSpec((hp, dv, tq), qt_blk), pl.BlockSpec((hp, tq, dv), q_blk), pl.BlockSpec((hp, 1, tq), qt_blk)]
    out_specs = [pl.BlockSpec((hp, tq, dk), q_blk), pl.BlockSpec((1, nk, dk, tk), lambda gg, hh, i: (gg, 0, 0, 0)),
                 pl.BlockSpec((1, nk, dv, tk), lambda gg, hh, i: (gg, 0, 0, 0))]
    out_shape = [jax.ShapeDtypeStruct((h, s, dk), _ACT), jax.ShapeDtypeStruct((g, nk, dk, tk), _F32),
                 jax.ShapeDtypeStruct((g, nk, dv, tk), _F32)]
    args = (qt, k, v, ot, do, lse2)
    scratch = []
    if exchange is not None:
        in_specs, args = in_specs + [pl.BlockSpec(memory_space=pl.ANY)], args + (exchange,)
        out_specs = out_specs + [pl.BlockSpec(memory_space=pl.ANY)]
        out_shape = out_shape + [jax.ShapeDtypeStruct(exchange.shape, exchange.dtype)]
        scratch = _comm_scratch()
    return pl.pallas_call(
        body, name=name, grid=(g, nhp, nq), in_specs=in_specs, out_specs=out_specs, out_shape=out_shape,
        scratch_shapes=scratch, compiler_params=_params(("arbitrary", "arbitrary", "arbitrary")),
    )(*args)


def _unchunk(xt):
    g, nk, d, tk = xt.shape
    return xt.transpose(1, 3, 0, 2).reshape(nk * tk, g * d)


def _chunk_t(x2d, g):
    s = x2d.shape[0]
    tk = min(ATTN_FWD_KEYS, s)
    return x2d.reshape(s // tk, tk, g, -1).transpose(2, 0, 3, 1)


def _heads_t(x2d, h):
    s = x2d.shape[0]
    return x2d.reshape(s, h, -1).transpose(1, 2, 0)


def _unheads_t(xt):
    h, d, s = xt.shape
    return xt.transpose(2, 0, 1).reshape(s, h * d)


WIN_ROWS = 512


def _win_geometry(s, dil):
    t = min(WIN_ROWS, s)
    length = s // dil
    lg = length.bit_length() - 1
    assert 1 << lg == length and t % DIL_HALF == 0 and s % t == 0
    return t, t + 2 * DIL_HALF, lg, t // DIL_HALF, s // DIL_HALF


def _win_specs(hn, t, d, halo_per_blk, n_halo, part=0):
    return [pl.BlockSpec((hn, DIL_HALF, d), lambda i: (part, jnp.maximum(i * halo_per_blk - 1, 0), 0)),
            pl.BlockSpec((hn, t, d), lambda i: (part, i, 0)),
            pl.BlockSpec((hn, DIL_HALF, d), lambda i: (part, jnp.minimum((i + 1) * halo_per_blk, n_halo - 1), 0))]


def _win_mask(i, t, w, lg, dil, wide_rows):
    shape = (w, t) if wide_rows else (t, w)
    rows = lax.broadcasted_iota(jnp.int32, shape, 0)
    cols = lax.broadcasted_iota(jnp.int32, shape, 1)
    base = i * t
    if wide_rows:
        pq, pk = base - DIL_HALF + rows, base + cols
    else:
        pq, pk = base + rows, base - DIL_HALF + cols
    arel = jnp.abs(pk - pq)
    valid = (arel <= DIL_HALF) & ((pk >> lg) == (pq >> lg))
    return valid, (-dil * arel).astype(_F32)


def _win3(lo_ref, mid_ref, hi_ref, a):
    return jnp.concatenate([lo_ref[a], mid_ref[a], hi_ref[a]], axis=0)


def _win_fwd(qkv, slopes, *, dil, name):
    hn, s, d = qkv.shape[0] // 3, qkv.shape[1], qkv.shape[2]
    t, w, lg, hpb, n_halo = _win_geometry(s, dil)
    scale = d ** -0.5

    def body(sl_ref, q_ref, klo, kmid, khi, vlo, vmid, vhi, o_ref, lse_ref):
        valid, nb = _win_mask(pl.program_id(0), t, w, lg, dil, False)
        for a in range(hn):
            kw, vw = _win3(klo, kmid, khi, a), _win3(vlo, vmid, vhi, a)
            sc = lax.dot_general(q_ref[a], kw, _NT, preferred_element_type=_F32) * scale
            sc = jnp.where(valid, sc + sl_ref[a] * nb, NEG_INF)
            m = jnp.max(sc, axis=1, keepdims=True)
            e = jnp.exp(sc - m)
            den = jnp.sum(e, axis=1, keepdims=True)
            o_ref[a] = jnp.dot(e.astype(_MXU), vw, preferred_element_type=_F32) / den
            lse_ref[a] = m + jnp.log(den)

    blk = lambda c: pl.BlockSpec((hn, t, c), lambda i: (0, i, 0))
    return pl.pallas_call(
        body, name=name, grid=(s // t,),
        in_specs=([pl.BlockSpec(memory_space=pltpu.SMEM), blk(d)] + _win_specs(hn, t, d, hpb, n_halo, 1)
                  + _win_specs(hn, t, d, hpb, n_halo, 2)),
        out_specs=[blk(d), blk(1)],
        out_shape=[jax.ShapeDtypeStruct((hn, s, d), _F32), jax.ShapeDtypeStruct((hn, s, 1), _F32)],
        compiler_params=_params(("parallel",)),
    )(slopes.astype(_F32), qkv, qkv, qkv, qkv, qkv, qkv, qkv)


def _win_bwd_dq(qkv, do, lse, delta, slopes, *, dil, name):
    hn, s, d = qkv.shape[0] // 3, qkv.shape[1], qkv.shape[2]
    t, w, lg, hpb, n_halo = _win_geometry(s, dil)
    scale = d ** -0.5

    def body(sl_ref, q_ref, klo, kmid, khi, vlo, vmid, vhi, do_ref, lse_ref, dl_ref, dq_ref):
        valid, nb = _win_mask(pl.program_id(0), t, w, lg, dil, False)
        for a in range(hn):
            kw, vw = _win3(klo, kmid, khi, a), _win3(vlo, vmid, vhi, a)
            sc = lax.dot_general(q_ref[a], kw, _NT, preferred_element_type=_F32) * scale
            p = jnp.exp(jnp.where(valid, sc + sl_ref[a] * nb, NEG_INF) - lse_ref[a])
            dp = lax.dot_general(do_ref[a], vw, _NT, preferred_element_type=_F32)
            ds = (p * (dp - dl_ref[a])).astype(_MXU)
            dq_ref[a] = (jnp.dot(ds, kw, preferred_element_type=_F32) * scale).astype(dq_ref.dtype)

    blk = lambda c: pl.BlockSpec((hn, t, c), lambda i: (0, i, 0))
    return pl.pallas_call(
        body, name=name, grid=(s // t,),
        in_specs=([pl.BlockSpec(memory_space=pltpu.SMEM), blk(d)] + _win_specs(hn, t, d, hpb, n_halo, 1)
                  + _win_specs(hn, t, d, hpb, n_halo, 2) + [blk(d), blk(1), blk(1)]),
        out_specs=blk(d),
        out_shape=jax.ShapeDtypeStruct((hn, s, d), _ACT),
        compiler_params=_params(("parallel",)),
    )(slopes.astype(_F32), qkv, qkv, qkv, qkv, qkv, qkv, qkv, do, lse, delta)


def _win_bwd_dkv(qkv, do, lse, delta, slopes, *, dil, name):
    hn, s, d = qkv.shape[0] // 3, qkv.shape[1], qkv.shape[2]
    t, w, lg, hpb, n_halo = _win_geometry(s, dil)
    scale = d ** -0.5

    def body(sl_ref, qlo, qmid, qhi, dolo, domid, dohi, llo, lmid, lhi, dllo, dlmid, dlhi, k_ref, v_ref, dkv_ref):
        valid, nb = _win_mask(pl.program_id(0), t, w, lg, dil, True)
        for a in range(hn):
            qw, dow = _win3(qlo, qmid, qhi, a), _win3(dolo, domid, dohi, a)
            lw, dlw = _win3(llo, lmid, lhi, a), _win3(dllo, dlmid, dlhi, a)
            sc = lax.dot_general(qw, k_ref[a], _NT, preferred_element_type=_F32) * scale
            p = jnp.exp(jnp.where(valid, sc + sl_ref[a] * nb, NEG_INF) - lw)
            dp = lax.dot_general(dow, v_ref[a], _NT, preferred_element_type=_F32)
            ds = (p * (dp - dlw)).astype(_MXU)
            dkv_ref[hn + a] = lax.dot_general(p.astype(_MXU), dow, _TN,
                                              preferred_element_type=_F32).astype(dkv_ref.dtype)
            dkv_ref[a] = (lax.dot_general(ds, qw, _TN, preferred_element_type=_F32) * scale).astype(dkv_ref.dtype)

    part = lambda which: pl.BlockSpec((hn, t, d), lambda i: (which, i, 0))
    return pl.pallas_call(
        body, name=name, grid=(s // t,),
        in_specs=([pl.BlockSpec(memory_space=pltpu.SMEM)] + _win_specs(hn, t, d, hpb, n_halo)
                  + _win_specs(hn, t, d, hpb, n_halo) + _win_specs(hn, t, 1, hpb, n_halo)
                  + _win_specs(hn, t, 1, hpb, n_halo) + [part(1), part(2)]),
        out_specs=pl.BlockSpec((2 * hn, t, d), lambda i: (0, i, 0)),
        out_shape=jax.ShapeDtypeStruct((2 * hn, s, d), _ACT),
        compiler_params=_params(("parallel",)),
    )(slopes.astype(_F32), qkv, qkv, qkv, do, do, do, lse, lse, lse, delta, delta, delta, qkv, qkv)


def _merge_weights(lses):
    mx = functools.reduce(jnp.maximum, lses)
    es = [jnp.exp(l - mx) for l in lses]
    den = functools.reduce(lambda a, b: a + b, es)
    return [e / den for e in es]


def _merge_fwd(outs, lses, *, name):
    ng = len(outs)
    sl, s, d = outs[0].shape
    t = _rows(s, ROW_BLOCK)

    def body(*refs):
        o_refs, l_refs, c_ref = refs[:ng], refs[ng:2 * ng], refs[2 * ng]
        wts = _merge_weights([r[0] for r in l_refs])
        comb = functools.reduce(lambda a, b: a + b, [w * r[0] for w, r in zip(wts, o_refs)])
        c_ref[0] = comb.astype(c_ref.dtype)

    big = pl.BlockSpec((1, t, d), lambda a, i: (a, i, 0))
    small = pl.BlockSpec((1, t, 1), lambda a, i: (a, i, 0))
    return pl.pallas_call(
        body, name=name, grid=(sl, s // t),
        in_specs=[big] * ng + [small] * ng, out_specs=big,
        out_shape=jax.ShapeDtypeStruct((sl, s, d), _ACT),
        compiler_params=_params(("parallel", "parallel")),
    )(*outs, *lses)


def _merge_bwd(dcomb, outs, lses, *, name):
    ng = len(outs)
    sl, s, d = outs[0].shape
    t = _rows(s, ROW_BLOCK)

    def body(*refs):
        dc_ref, o_refs, l_refs = refs[0], refs[1:1 + ng], refs[1 + ng:1 + 2 * ng]
        do_refs, dl_refs = refs[1 + 2 * ng:1 + 3 * ng], refs[1 + 3 * ng:]
        wts = _merge_weights([r[0] for r in l_refs])
        dc = dc_ref[0].astype(_F32)
        comb = functools.reduce(lambda a, b: a + b, [w * r[0] for w, r in zip(wts, o_refs)])
        dot = jnp.sum(dc * comb, axis=-1, keepdims=True)
        for w, do_ref, dl_ref in zip(wts, do_refs, dl_refs):
            do_ref[0] = (w * dc).astype(do_ref.dtype)
            dl_ref[0] = w * dot

    big = pl.BlockSpec((1, t, d), lambda a, i: (a, i, 0))
    small = pl.BlockSpec((1, t, 1), lambda a, i: (a, i, 0))
    res = pl.pallas_call(
        body, name=name, grid=(sl, s // t),
        in_specs=[big] + [big] * ng + [small] * ng, out_specs=[big] * ng + [small] * ng,
        out_shape=[jax.ShapeDtypeStruct((sl, s, d), _ACT)] * ng + [jax.ShapeDtypeStruct((sl, s, 1), _F32)] * ng,
        compiler_params=_params(("parallel", "parallel")),
    )(dcomb, *outs, *lses)
    return res[:ng], res[ng:]


SWIGLU_ROWS = 512


def _gate_up(a, w_in_t, *, name):
    m, k = a.shape
    n = w_in_t.shape[0] // 2
    tm, tn, tk = _pick(m, SWIGLU_ROWS), _pick(n, MM_TILE), _pick(k, MM_TILE)
    nk = k // tk

    def body(a_ref, g_ref, u_ref, go_ref, uo_ref, act_ref, acc_g, acc_u):
        kk = pl.program_id(2)

        @pl.when(kk == 0)
        def _():
            acc_g[...] = jnp.zeros_like(acc_g)
            acc_u[...] = jnp.zeros_like(acc_u)

        av = a_ref[...].astype(_MXU)
        acc_g[...] += lax.dot_general(av, g_ref[...].astype(_MXU), _NT, preferred_element_type=_F32)
        acc_u[...] += lax.dot_general(av, u_ref[...].astype(_MXU), _NT, preferred_element_type=_F32)

        @pl.when(kk == nk - 1)
        def _():
            gf, uf = acc_g[...], acc_u[...]
            go_ref[...] = gf.astype(go_ref.dtype)
            uo_ref[...] = uf.astype(uo_ref.dtype)
            act_ref[...] = (gf * jax.nn.sigmoid(gf) * uf).astype(act_ref.dtype)

    o_spec = pl.BlockSpec((tm, tn), lambda i, j, kk: (i, j))
    out = jax.ShapeDtypeStruct((m, n), _ACT)
    return pl.pallas_call(
        body, name=name, grid=(m // tm, n // tn, nk),
        in_specs=[pl.BlockSpec((tm, tk), lambda i, j, kk: (i, kk)),
                  pl.BlockSpec((tn, tk), lambda i, j, kk: (j, kk)),
                  pl.BlockSpec((tn, tk), lambda i, j, kk: (j + n // tn, kk))],
        out_specs=[o_spec, o_spec, o_spec], out_shape=[out, out, out],
        scratch_shapes=[pltpu.VMEM((tm, tn), _F32), pltpu.VMEM((tm, tn), _F32)],
        compiler_params=_params(("parallel", "parallel", "arbitrary")),
    )(a, w_in_t, w_in_t)


def _gate_up_bwd(dout, w_out, gate, up, *, name):
    m, k = dout.shape
    n = w_out.shape[0]
    tm, tn, tk = _pick(m, SWIGLU_ROWS), _pick(n, MM_TILE), _pick(k, MM_TILE)
    nk = k // tk

    def body(d_ref, w_ref, g_ref, u_ref, dg_ref, du_ref, acc):
        kk = pl.program_id(2)

        @pl.when(kk == 0)
        def _():
            acc[...] = jnp.zeros_like(acc)

        acc[...] += lax.dot_general(d_ref[...].astype(_MXU), w_ref[...].astype(_MXU), _NT,
                                    preferred_element_type=_F32)

        @pl.when(kk == nk - 1)
        def _():
            daf = acc[...]
            gf = g_ref[...].astype(_F32)
            sg = jax.nn.sigmoid(gf)
            dg_ref[...] = (daf * u_ref[...].astype(_F32) * (sg + gf * sg * (1.0 - sg))).astype(dg_ref.dtype)
            du_ref[...] = (daf * (gf * sg)).astype(du_ref.dtype)

    o_spec = pl.BlockSpec((tm, tn), lambda i, j, kk: (i, j))
    out = jax.ShapeDtypeStruct((m, n), _ACT)
    return pl.pallas_call(
        body, name=name, grid=(m // tm, n // tn, nk),
        in_specs=[pl.BlockSpec((tm, tk), lambda i, j, kk: (i, kk)), pl.BlockSpec((tn, tk), lambda i, j, kk: (j, kk)),
                  o_spec, o_spec],
        out_specs=[o_spec, o_spec], out_shape=[out, out],
        scratch_shapes=[pltpu.VMEM((tm, tn), _F32)],
        compiler_params=_params(("parallel", "parallel", "arbitrary")),
    )(dout, w_out, gate, up)


def _final_loss(x, g, target, *, name):
    m, d = x.shape
    tm = _rows(m, ROW_BLOCK)

    def body(x_ref, g_ref, t_ref, loss_ref, dx_ref, dg_ref):
        @pl.when(pl.program_id(0) == 0)
        def _():
            loss_ref[...] = jnp.zeros_like(loss_ref)
            dg_ref[...] = jnp.zeros_like(dg_ref)

        xf = x_ref[...]
        r = lax.rsqrt(jnp.mean(xf * xf, axis=-1, keepdims=True) + NORM_EPS)
        xh = xf * r
        err = xh * g_ref[...] - t_ref[...]
        loss_ref[...] += 0.5 * jnp.sum(jnp.mean(err * err, axis=-1, keepdims=True))
        dy = err * (1.0 / d)
        dg_ref[...] += jnp.sum(dy * xh, axis=0, keepdims=True)
        gdy = dy * g_ref[...]
        dx_ref[...] = r * (gdy - xh * jnp.mean(gdy * xh, axis=-1, keepdims=True))

    row = pl.BlockSpec((tm, d), lambda i: (i, 0))
    vec = pl.BlockSpec((1, d), lambda i: (0, 0))
    loss, dx, dg = pl.pallas_call(
        body, name=name, grid=(m // tm,),
        in_specs=[row, vec, row],
        out_specs=[pl.BlockSpec((8, LANES), lambda i: (0, 0)), row, vec],
        out_shape=[jax.ShapeDtypeStruct((8, LANES), _F32), jax.ShapeDtypeStruct((m, d), _F32),
                   jax.ShapeDtypeStruct((1, d), _F32)],
        compiler_params=_params(("arbitrary",)),
    )(x, g.reshape(1, d), target)
    return loss[0, 0], dx, dg.reshape(d)


def _reduce_adamw(parts, w, m, v, *, name):
    rws, cols = w.shape
    tr = rws
    for cand in range(min(rws, 256), 0, -SUBLANES_16BIT):
        if cand % SUBLANES_16BIT == 0 and rws % cand == 0:
            tr = cand
            break

    def body(p_ref, w_ref, m_ref, v_ref, g_ref, d_ref, nm_ref, nv_ref):
        gsum = p_ref[0].astype(_F32)
        for dev in range(1, N_DEV):
            gsum = gsum + p_ref[dev].astype(_F32)
        m2 = ADAM_B1 * m_ref[...] + (1.0 - ADAM_B1) * gsum
        v2 = ADAM_B2 * v_ref[...] + (1.0 - ADAM_B2) * (gsum * gsum)
        m_hat = m2 / (1.0 - ADAM_B1 ** ADAM_STEP)
        v_hat = v2 / (1.0 - ADAM_B2 ** ADAM_STEP)
        g_ref[...] = gsum
        d_ref[...] = -ADAM_LR * (m_hat / (jnp.sqrt(v_hat) + ADAM_EPS) + ADAM_WD * w_ref[...])
        nm_ref[...] = m2
        nv_ref[...] = v2

    blk = pl.BlockSpec((tr, cols), lambda i: (i, 0))
    out = jax.ShapeDtypeStruct((rws, cols), _F32)
    return pl.pallas_call(
        body, name=name, grid=(rws // tr,),
        in_specs=[pl.BlockSpec((N_DEV, tr, cols), lambda i: (0, i, 0)), blk, blk, blk],
        out_specs=[blk, blk, blk, blk], out_shape=[out, out, out, out],
        compiler_params=_params(("parallel",)),
    )(parts, w, m, v)


def _mesh_pos():
    return lax.axis_index("x"), lax.axis_index("y"), lax.axis_index("c")


def _all_gather(block):
    rws, cols = block.shape

    def body(x_ref, out_ref, send_sems, recv_sems, local_sem):
        start, forward, finish = _gather_phases(x_ref, out_ref, send_sems, recv_sems, local_sem)
        start()
        forward()
        finish()

    return pl.pallas_call(
        body, name="weights_all_gather",
        out_shape=jax.ShapeDtypeStruct((N_DEV, rws, cols), block.dtype),
        in_specs=[pl.BlockSpec(memory_space=pl.ANY)],
        out_specs=pl.BlockSpec(memory_space=pl.ANY),
        scratch_shapes=_comm_scratch(),
    )(block)


def _comm_scratch():
    return [pltpu.SemaphoreType.DMA((N_DEV - 1,)), pltpu.SemaphoreType.DMA((N_DEV - 1,)), pltpu.SemaphoreType.DMA]


def _gather_phases(x_ref, out_ref, send_sems, recv_sems, local_sem):
    x, y, c = _mesh_pos()
    me, sibling = (x, y, c), (x, y, 1 - c)
    chips = [(1 - x, y), (x, 1 - y), (1 - x, 1 - y)]

    def slot(px, py, pc):
        return out_ref.at[4 * px + 2 * py + pc]

    def copy(k, blk, to, src=None):
        return pltpu.make_async_remote_copy(
            src_ref=slot(*blk) if src is None else src, dst_ref=slot(*blk),
            send_sem=send_sems.at[k], recv_sem=recv_sems.at[k],
            device_id=to, device_id_type=pl.DeviceIdType.MESH)

    mine = pltpu.make_async_copy(x_ref, slot(*me), local_sem)
    first = [copy(0, me, sibling, src=x_ref)]
    first += [copy(1 + j, me, (*chip, c), src=x_ref) for j, chip in enumerate(chips)]
    passed = [copy(4 + j, (*chip, c), sibling) for j, chip in enumerate(chips)]

    def start():
        mine.start()
        for cp in first:
            cp.start()

    def forward():
        for j, chip in enumerate(chips):
            copy(1 + j, (*chip, c), me).wait_recv()
            passed[j].start()

    def finish():
        copy(0, sibling, me).wait_recv()
        for j, chip in enumerate(chips):
            copy(4 + j, (*chip, 1 - c), me).wait_recv()
        for cp in first + passed:
            cp.wait_send()
        mine.wait()

    return start, forward, finish


def _exchange(parts):
    def body(g_ref, out_ref, send_sems, recv_sems, local_sem):
        start, finish = _exchange_phases(g_ref, out_ref, send_sems, recv_sems, local_sem)
        start()
        finish()

    return pl.pallas_call(
        body, name="grads_exchange",
        out_shape=jax.ShapeDtypeStruct(parts.shape, parts.dtype),
        in_specs=[pl.BlockSpec(memory_space=pl.ANY)],
        out_specs=pl.BlockSpec(memory_space=pl.ANY),
        scratch_shapes=_comm_scratch(),
    )(parts)


def _exchange_phases(g_ref, out_ref, send_sems, recv_sems, local_sem):
    x, y, c = _mesh_pos()
    me = 4 * x + 2 * y + c
    mine = pltpu.make_async_copy(g_ref.at[me], out_ref.at[me], local_sem)
    sends, arrivals = [], []
    for k in range(1, N_DEV):
        px = 1 - x if k & 4 else x
        py = 1 - y if k & 2 else y
        pc = 1 - c if k & 1 else c
        peer = 4 * px + 2 * py + pc
        sems = dict(send_sem=send_sems.at[k - 1], recv_sem=recv_sems.at[k - 1],
                    device_id=(px, py, pc), device_id_type=pl.DeviceIdType.MESH)
        sends.append(pltpu.make_async_remote_copy(src_ref=g_ref.at[peer], dst_ref=out_ref.at[me], **sems))
        arrivals.append(pltpu.make_async_remote_copy(src_ref=g_ref.at[peer], dst_ref=out_ref.at[peer], **sems))

    def start():
        mine.start()
        for cp in sends:
            cp.start()

    def finish():
        for cp in arrivals:
            cp.wait_recv()
        for cp in sends:
            cp.wait_send()
        mine.wait()

    return start, finish


_BIG = (("w_in_ab", 2), ("mla_w_uq", 1), ("mla_w_ukv", 1), ("w_out_ab", 2), ("gqa_w_q", 1), ("gqa_w_kv", 1),
        ("gqa_w_o", 1), ("ffn_w_in", 1), ("ffn_w_out", 1))
_TRANSPOSED = ("ffn_w_in",)


def _stored(arrays):
    return {n: (a.transpose(0, 2, 1) if n in _TRANSPOSED else a) for n, a in arrays.items()}
_SMALL = ("mix_norm_ab", "ffn_norm", "final_norm", "mix_norm_c", "mla_q_norm", "mla_kv_norm", "gqa_q_norm", "gqa_k_norm")
SMALL_ROWS = 16
GATHER_TAIL_ROWS = 16
EXCHANGE_TAIL_ROWS = 32


def _pad_rows(a2d):
    pad = -a2d.shape[0] % SUBLANES_16BIT
    return jnp.pad(a2d, ((0, pad), (0, 0))) if pad else a2d


def _pack_rows(shard):
    return _pad_rows(shard.reshape(-1, PACK_COLS))


def _packed_rows(shape):
    n = 1
    for d in shape:
        n *= d
    rows = n // PACK_COLS
    return rows + (-rows % SUBLANES_16BIT)


_FIRST = {"w_in_ab": (0, 1), "mla_w_uq": (0, 1), "mla_w_ukv": (0, 1)}
_REST = {"w_in_ab": (1, 1), "mla_w_uq": (1, 1), "mla_w_ukv": (1, 1), "w_out_ab": (0, 2), "gqa_w_q": (0, 2),
         "gqa_w_kv": (0, 2), "gqa_w_o": (0, 2), "ffn_w_in": (0, 4), "ffn_w_out": (0, 4)}
PACK_ROWS = 128


def _group_of(arrays, group):
    return {n: arrays[n][lo:lo + cnt] for n, (lo, cnt) in group.items()}


def _pack_big(shards):
    parts = [_pack_rows(shards[n]) for n, _ in _BIG if n in shards]
    fill = -sum(p.shape[0] for p in parts) % PACK_ROWS
    return jnp.concatenate(parts + [jnp.zeros((fill, PACK_COLS), parts[0].dtype)], axis=0)


def _unpack_big(packed, shapes):
    out, off = {}, 0
    for n, _ in _BIG:
        if n not in shapes:
            continue
        size = 1
        for d in shapes[n]:
            size *= d
        out[n] = packed[off:off + size // PACK_COLS].reshape(shapes[n])
        off += _packed_rows(shapes[n])
    return out


def _unpack_gathered(gathered, shapes):
    out, off = {}, 0
    for n, axis in _BIG:
        if n not in shapes:
            continue
        size = 1
        for d in shapes[n]:
            size *= d
        l3 = (shapes[n][0], shapes[n][1], size // (shapes[n][0] * shapes[n][1]))
        sh = gathered[:, off:off + size // PACK_COLS].reshape((N_DEV,) + l3)
        if axis == 1:
            full = sh.transpose(1, 0, 2, 3).reshape(l3[0], N_DEV * l3[1], l3[2])
        else:
            full = sh.transpose(1, 2, 0, 3).reshape(l3[0], l3[1], N_DEV * l3[2])
        out[n] = full
        off += _packed_rows(shapes[n])
    return out


def _split_for_devices(layers, axis):
    rws, cols = layers[0].shape
    if axis == 1:
        flat = jnp.concatenate([g.reshape(N_DEV, -1, PACK_COLS) for g in layers], axis=1)
    else:
        sh = jnp.stack(layers).reshape(len(layers), rws, N_DEV, cols // N_DEV).transpose(2, 0, 1, 3)
        flat = sh.reshape(N_DEV, -1, PACK_COLS)
    pad = -flat.shape[1] % SUBLANES_16BIT
    return jnp.pad(flat, ((0, 0), (0, pad), (0, 0))) if pad else flat


def _to_bits16(a_f32_rows):
    r = a_f32_rows.shape[0]
    return lax.bitcast_convert_type(a_f32_rows, jnp.bfloat16).reshape(2 * r, PACK_COLS)


def _from_bits16(a_bf16_rows):
    lead, r = a_bf16_rows.shape[:-2], a_bf16_rows.shape[-2]
    return lax.bitcast_convert_type(a_bf16_rows.reshape(lead + (r // 2, PACK_COLS, 2)), _F32)


def _small_sizes():
    return {"mix_norm_ab": 2 * D_MODEL, "ffn_norm": DEPTH * D_MODEL, "final_norm": D_MODEL, "mix_norm_c": 2 * D_MODEL,
            "mla_q_norm": 2 * MLA_Q_RANK, "mla_kv_norm": 2 * MLA_KV_RANK, "gqa_q_norm": 2 * GQA_HEAD_DIM,
            "gqa_k_norm": 2 * GQA_HEAD_DIM}


def _pack_small(vals):
    flat = jnp.concatenate([vals[n].reshape(-1).astype(_F32) for n in _SMALL])
    return jnp.pad(flat, (0, SMALL_ROWS * PACK_COLS - flat.shape[0])).reshape(SMALL_ROWS, PACK_COLS)


def _unpack_small(pack):
    flat, out, off = pack.reshape(-1), {}, 0
    sizes = _small_sizes()
    for n in _SMALL:
        out[n] = flat[off:off + sizes[n]]
        off += sizes[n]
    return out


def _angles(pos, dim):
    freqs = ROPE_THETA ** (-jnp.arange(0, dim, 2, dtype=_F32) / dim)
    ang = pos.astype(_F32)[:, None] * freqs[None, :]
    return jnp.cos(ang), jnp.sin(ang)


def _rope_tables(s):
    pos = jnp.arange(s)
    cos_t, sin_t = _angles(pos, MLA_ROPE)
    mla_c = jnp.tile(jnp.concatenate([cos_t, cos_t], -1), (1, LANES // ROT_CHUNK))
    mla_s = jnp.tile(jnp.concatenate([-sin_t, sin_t], -1), (1, LANES // ROT_CHUNK))
    rows = s // GRID_W
    row_idx = jnp.broadcast_to(jnp.arange(rows)[:, None], (rows, GRID_W)).reshape(-1)
    col_idx = jnp.broadcast_to(jnp.arange(GRID_W)[None, :], (rows, GRID_W)).reshape(-1)
    cos_r, sin_r = _angles(row_idx, GQA_HEAD_DIM // 2)
    cos_c, sin_c = _angles(col_idx, GQA_HEAD_DIM // 2)
    gqa_c = jnp.tile(jnp.concatenate([cos_r, cos_r, cos_c, cos_c], -1), (1, LANES // GQA_HEAD_DIM))
    gqa_s = jnp.tile(jnp.concatenate([-sin_r, sin_r, -sin_c, sin_c], -1), (1, LANES // GQA_HEAD_DIM))
    return (mla_c, mla_s), (gqa_c, gqa_s)


def _heads(x2d, h):
    s = x2d.shape[0]
    return x2d.reshape(s, h, -1).transpose(1, 0, 2)


def _unheads(xh):
    h, s, d = xh.shape
    return xh.transpose(1, 0, 2).reshape(s, h * d)


def _to_res(x2d, dil):
    s, cols = x2d.shape
    if dil > 1:
        x2d = x2d.reshape(s // dil, dil, cols).transpose(1, 0, 2).reshape(s, cols)
    return x2d.reshape(s, cols // DIL_HEAD_DIM, DIL_HEAD_DIM).transpose(1, 0, 2)


def _from_res(xh, dil):
    heads, s, d = xh.shape
    x2d = xh.transpose(1, 0, 2).reshape(s, heads * d)
    if dil > 1:
        x2d = x2d.reshape(dil, s // dil, heads * d).transpose(1, 0, 2).reshape(s, heads * d)
    return x2d


def _group_major(w_b):
    rws = w_b.shape[0]
    return w_b.reshape(rws, 3, DIL_GROUPS, DIL_SLOTS * DIL_HEAD_DIM).transpose(0, 2, 1, 3).reshape(rws, -1)


def _res_to_tok(xh, dil):
    sl, s, c = xh.shape
    if dil == 1:
        return xh
    return xh.reshape(sl, dil, s // dil, c).transpose(0, 2, 1, 3).reshape(sl, s, c)


def _tok_to_res(xh, dil):
    sl, s, c = xh.shape
    if dil == 1:
        return xh
    return xh.reshape(sl, s // dil, dil, c).transpose(0, 2, 1, 3).reshape(sl, s, c)


def _ffn_fwd(x, w, tag):
    hn = _rmsnorm(x, w["norm"], out_dtype=_ACT, name=f"{tag}_norm")
    gate, up, act = _gate_up(hn, w["w_in_t"], name=f"{tag}_gate_up")
    out = _matmul(act, w["w_out"], res=x, name=f"{tag}_out")
    return out, (x, hn, gate, up, act)


def _ffn_bwd(dout, saved, w, tag):
    x, hn, gate, up, act = saved
    w_gate_t, w_up_t = w["w_in_t"][:FFN_HIDDEN], w["w_in_t"][FFN_HIDDEN:]
    d_w_out = _matmul(act, dout, trans_a=True, name=f"{tag}_dwout")
    dgate, dup = _gate_up_bwd(dout, w["w_out"], gate, up, name=f"{tag}_dgate_up")
    d_w_gate_t = _matmul(dgate, hn, trans_a=True, name=f"{tag}_dwgate")
    d_w_up_t = _matmul(dup, hn, trans_a=True, name=f"{tag}_dwup")
    dhn = _matmul(dgate, w_gate_t, name=f"{tag}_dhn_gate")
    dhn = _matmul(dup, w_up_t, res=dhn, name=f"{tag}_dhn_up")
    dx, dnorm = _rmsnorm_bwd(x, w["norm"], dhn, dout, name=f"{tag}_dnorm")
    return dx, {"norm": dnorm, "w_in_t": jnp.concatenate([d_w_gate_t, d_w_up_t], axis=0), "w_out": d_w_out}


def _even_fwd(x, w, tabs, slopes, tag, gather=None, late=None):
    s = x.shape[0]
    (mla_c, mla_s), _ = tabs
    hn = _rmsnorm(x, w["norm"], out_dtype=_ACT, name=f"{tag}_norm")
    za = _matmul(hn, w["w_a"], name=f"{tag}_in_a")
    zb = _matmul(hn, w["w_b"], out_dtype=_ACT, name=f"{tag}_in_b")
    cq, ckv, kr = za[:, :MLA_Q_RANK], za[:, MLA_Q_RANK:MLA_Q_RANK + MLA_KV_RANK], za[:, MLA_Q_RANK + MLA_KV_RANK:]
    cqn = _rmsnorm(cq, w["q_norm"], out_dtype=_ACT, name=f"{tag}_qnorm")
    ckvn = _rmsnorm(ckv, w["kv_norm"], out_dtype=_ACT, name=f"{tag}_kvnorm")
    q = _matmul(cqn, w["w_uq"], scale=(MLA_NOPE + MLA_ROPE) ** -0.5 * LOG2_E,
                name=f"{tag}_uq")
    kv = _matmul(ckvn, w["w_ukv"], out_dtype=_ACT, name=f"{tag}_ukv")
    nn = MLA_HEADS * MLA_NOPE
    q_rope = _rope(q[:, nn:], mla_c, mla_s, out_dtype=_ACT, name=f"{tag}_rope_q")
    k_rope = _rope(kr, mla_c, mla_s, out_dtype=_ACT, name=f"{tag}_rope_k")[:, :MLA_ROPE]
    qh = jnp.concatenate([_heads_t(q[:, :nn].astype(_ACT), MLA_HEADS), _heads_t(q_rope, MLA_HEADS)], axis=1)
    kh = jnp.concatenate([_heads(kv[:, :nn], MLA_HEADS),
                          jnp.broadcast_to(k_rope[None], (MLA_HEADS, s, MLA_ROPE))], axis=-1)
    vh = _heads(kv[:, nn:], MLA_HEADS)
    oa, lse_a, *gathered = _attn_fwd(qh, kh, _chunk_t(kv[:, nn:], MLA_HEADS), out_dtype=_ACT, name=f"{tag}_mla",
                                     gather=gather)
    if late is not None:
        w = late(gathered[0])

    ng = 3 * DIL_SLOTS * DIL_HEAD_DIM
    outs, lses, dil_saved = [], [], []
    for gi, (_, dil) in enumerate(DIL_PAIRS):
        hs = slice(gi * DIL_SLOTS, (gi + 1) * DIL_SLOTS)
        qkv = _to_res(zb[:, gi * ng:(gi + 1) * ng], dil)
        o, l = _win_fwd(qkv, slopes[hs], dil=dil, name=f"{tag}_dil{gi}")
        dil_saved.append((qkv, l))
        outs.append(_res_to_tok(o, dil))
        lses.append(_res_to_tok(l, dil))
    o3, lse3 = tuple(outs), tuple(lses)
    comb = _merge_fwd(o3, lse3, name=f"{tag}_merge")
    cat = jnp.concatenate([_unheads_t(oa), _unheads(comb)], axis=-1)
    out = _matmul(cat, w["w_out"], res=x, name=f"{tag}_out")
    saved = (x, hn, cq, ckv, cqn, ckvn, qh, kh, vh, oa, lse_a, dil_saved, o3, lse3, cat)
    return out, saved, (gathered[0] if gathered else None)


def _even_bwd(dout, saved, w, tabs, slopes, tag, exchange=None):
    x, hn, cq, ckv, cqn, ckvn, qh, kh, vh, oa, lse_a, dil_saved, o3, lse3, cat = saved
    (mla_c, mla_s), _ = tabs
    nn = MLA_HEADS * MLA_NOPE
    dcat = _matmul(dout, w["w_out"], trans_b=True, out_dtype=_ACT, name=f"{tag}_dcat")
    d_w_out = _matmul(cat, dout, trans_a=True, name=f"{tag}_dwout")
    nv = MLA_HEADS * MLA_V

    doa = _heads(dcat[:, :nv], MLA_HEADS)
    dqh, dkt, dvt, *received = _attn_bwd(qh, kh, vh, oa, doa, lse_a, scale=(MLA_NOPE + MLA_ROPE) ** -0.5,
                                         name=f"{tag}_mla_bwd",
                                         exchange=None if exchange is None else exchange(d_w_out))
    dq_rope = _rope(_unheads(dqh[..., MLA_NOPE:]), mla_c, -mla_s, out_dtype=_ACT, name=f"{tag}_drope_q")
    dq = jnp.concatenate([_unheads(dqh[..., :MLA_NOPE]).astype(_ACT), dq_rope], axis=-1)
    dk3 = _unchunk(dkt).reshape(-1, MLA_HEADS, MLA_NOPE + MLA_ROPE)
    dkr = _rope(dk3[..., MLA_NOPE:].reshape(-1, MLA_HEADS * MLA_ROPE), mla_c, -mla_s, out_dtype=_F32,
                sum_chunks=True, name=f"{tag}_drope_k")
    dkv = jnp.concatenate([dk3[..., :MLA_NOPE].reshape(-1, nn), _unchunk(dvt)], axis=-1).astype(_ACT)
    d_w_uq = _matmul(cqn, dq, trans_a=True, name=f"{tag}_dwuq")
    d_w_ukv = _matmul(ckvn, dkv, trans_a=True, name=f"{tag}_dwukv")
    dcqn = _matmul(dq, w["w_uq"], trans_b=True, name=f"{tag}_dcqn")
    dckvn = _matmul(dkv, w["w_ukv"], trans_b=True, name=f"{tag}_dckvn")
    dcq, d_q_norm = _rmsnorm_bwd(cq, w["q_norm"], dcqn, out_dtype=_ACT, name=f"{tag}_dqnorm")
    dckv, d_kv_norm = _rmsnorm_bwd(ckv, w["kv_norm"], dckvn, out_dtype=_ACT, name=f"{tag}_dkvnorm")
    lane = jnp.arange(LANES) < MLA_ROPE
    dza = jnp.concatenate([dcq, dckv, jnp.where(lane[None], dkr, 0.0).astype(_ACT)], axis=-1)

    dcomb = _heads(dcat[:, nv:], DIL_SLOTS)
    do3, delta3 = _merge_bwd(dcomb, o3, lse3, name=f"{tag}_dmerge")
    dslabs = []
    for gi, (_, dil) in enumerate(DIL_PAIRS):
        hs = slice(gi * DIL_SLOTS, (gi + 1) * DIL_SLOTS)
        qkv, l = dil_saved[gi]
        grads = (qkv, _tok_to_res(do3[gi], dil), l, _tok_to_res(delta3[gi], dil), slopes[hs])
        a = _win_bwd_dq(*grads, dil=dil, name=f"{tag}_dil{gi}_dq")
        bc = _win_bwd_dkv(*grads, dil=dil, name=f"{tag}_dil{gi}_dkv")
        dslabs.append(_from_res(jnp.concatenate([a, bc], axis=0), dil))
    dzb = jnp.concatenate(dslabs, axis=-1)

    d_w_a = _matmul(hn, dza, trans_a=True, name=f"{tag}_dwa")
    d_w_b = _matmul(hn, dzb, trans_a=True, name=f"{tag}_dwb")
    dhn = _matmul(dza, w["w_a"], trans_b=True, name=f"{tag}_dhn_a")
    dhn = _matmul(dzb, w["w_b"], trans_b=True, res=dhn, name=f"{tag}_dhn_b")
    dx, dnorm = _rmsnorm_bwd(x, w["norm"], dhn, dout, name=f"{tag}_dnorm")
    grads = {"norm": dnorm, "w_a": d_w_a, "w_b": d_w_b, "q_norm": d_q_norm, "kv_norm": d_kv_norm,
             "w_uq": d_w_uq, "w_ukv": d_w_ukv, "w_out": d_w_out}
    return dx, grads, (received[0] if received else None)


def _odd_fwd(x, w, tabs, tag):
    s = x.shape[0]
    _, (gqa_c, gqa_s) = tabs
    nk = GQA_KV_HEADS * GQA_HEAD_DIM
    hn = _rmsnorm(x, w["norm"], out_dtype=_ACT, name=f"{tag}_norm")
    q = _matmul(hn, w["w_q"], name=f"{tag}_q")
    kv = _matmul(hn, w["w_kv"], name=f"{tag}_kv")
    k = kv[:, :nk]
    qh = _heads_t(_headnorm_rope(q, w["q_norm"], gqa_c, gqa_s, scale=GQA_HEAD_DIM ** -0.5 * LOG2_E,
                                 name=f"{tag}_prep_q"), GQA_HEADS)
    kh = _heads(_headnorm_rope(k, w["k_norm"], gqa_c, gqa_s, name=f"{tag}_prep_k"), GQA_KV_HEADS)
    v = kv[:, nk:].astype(_ACT)
    vh = _heads(v, GQA_KV_HEADS)
    o, lse = _attn_fwd(qh, kh, _chunk_t(v, GQA_KV_HEADS), out_dtype=_ACT, name=f"{tag}_gqa")
    ocat = _unheads_t(o)
    out = _matmul(ocat, w["w_o"], res=x, name=f"{tag}_out")
    return out, (x, hn, q, k, qh, kh, vh, o, lse, ocat)


def _odd_bwd(dout, saved, w, tabs, tag):
    x, hn, q, k, qh, kh, vh, o, lse, ocat = saved
    s = x.shape[0]
    _, (gqa_c, gqa_s) = tabs
    docat = _matmul(dout, w["w_o"], trans_b=True, out_dtype=_ACT, name=f"{tag}_docat")
    d_w_o = _matmul(ocat, dout, trans_a=True, name=f"{tag}_dwo")
    doh = _heads(docat, GQA_HEADS)
    dqh, dkt, dvt = _attn_bwd(qh, kh, vh, o, doh, lse, scale=GQA_HEAD_DIM ** -0.5, name=f"{tag}_gqa_bwd")
    dq, d_q_norm = _headnorm_rope_bwd(q, w["q_norm"], _unheads(dqh), gqa_c, gqa_s, name=f"{tag}_dprep_q")
    dk, d_k_norm = _headnorm_rope_bwd(k, w["k_norm"], _unchunk(dkt), gqa_c, gqa_s, name=f"{tag}_dprep_k")
    dkv = jnp.concatenate([dk, _unchunk(dvt).astype(_ACT)], axis=-1)
    d_w_q = _matmul(hn, dq, trans_a=True, name=f"{tag}_dwq")
    d_w_kv = _matmul(hn, dkv, trans_a=True, name=f"{tag}_dwkv")
    dhn = _matmul(dq, w["w_q"], trans_b=True, name=f"{tag}_dhn_q")
    dhn = _matmul(dkv, w["w_kv"], trans_b=True, res=dhn, name=f"{tag}_dhn_kv")
    dx, dnorm = _rmsnorm_bwd(x, w["norm"], dhn, dout, name=f"{tag}_dnorm")
    return dx, {"norm": dnorm, "w_q": d_w_q, "w_kv": d_w_kv, "q_norm": d_q_norm, "k_norm": d_k_norm, "w_o": d_w_o}


def _split_heads_cols(wm, heads, first):
    rws = wm.shape[0]
    w3 = wm.reshape(rws, heads, -1)
    return jnp.concatenate([w3[:, :, :first].reshape(rws, -1), w3[:, :, first:].reshape(rws, -1)], axis=-1)


def _merge_heads_cols(wm, heads, first):
    rws, cols = wm.shape
    a = wm[:, :heads * first].reshape(rws, heads, first)
    b = wm[:, heads * first:].reshape(rws, heads, cols // heads - first)
    return jnp.concatenate([a, b], axis=-1).reshape(rws, cols)


def _layer_weights(full, gains, layer):
    i = layer // 2

    def stacked(name, idx):
        for group, arrays in zip((_FIRST, _REST), full):
            lo, cnt = group.get(name, (0, 0))
            if lo <= idx < lo + cnt:
                return None if arrays is None else arrays[name][idx - lo]
        raise KeyError((name, idx))

    ffn = {"norm": gains["ffn_norm"][layer], "w_in_t": stacked("ffn_w_in", layer), "w_out": stacked("ffn_w_out", layer)}
    if layer % 2 == 0:
        w_in = stacked("w_in_ab", i)
        mix = {"norm": gains["mix_norm_ab"][i],
               "w_a": jnp.pad(w_in[:, :IN_A], ((0, 0), (0, IN_A_PAD - IN_A))), "w_b": _group_major(w_in[:, IN_A:]),
               "q_norm": gains["mla_q_norm"][i], "kv_norm": gains["mla_kv_norm"][i],
               "w_uq": _split_heads_cols(stacked("mla_w_uq", i), MLA_HEADS, MLA_NOPE),
               "w_ukv": _split_heads_cols(stacked("mla_w_ukv", i), MLA_HEADS, MLA_NOPE),
               "w_out": stacked("w_out_ab", i)}
    else:
        mix = {"norm": gains["mix_norm_c"][i], "w_q": stacked("gqa_w_q", i), "w_kv": stacked("gqa_w_kv", i),
               "q_norm": gains["gqa_q_norm"][i], "k_norm": gains["gqa_k_norm"][i], "w_o": stacked("gqa_w_o", i)}
    return mix, ffn


def _pack_grads(grads, group):
    parts = [_split_for_devices(grads[n][group[n][0]:group[n][0] + group[n][1]], axis)
             for n, axis in _BIG if n in group]
    fill = -sum(p.shape[1] for p in parts) % PACK_ROWS
    return jnp.concatenate(parts + [jnp.zeros((N_DEV, fill, PACK_COLS), parts[0].dtype)], axis=1)


def kernel(x, mix_norm_ab, w_in_ab, mla_q_norm, mla_kv_norm, mla_w_uq, mla_w_ukv, w_out_ab, mix_norm_c, gqa_w_q, gqa_w_kv, gqa_q_norm, gqa_k_norm, gqa_w_o, ffn_norm, ffn_w_in, ffn_w_out, final_norm, loss_target, m_mix_norm_ab, m_w_in_ab, m_mla_q_norm, m_mla_kv_norm, m_mla_w_uq, m_mla_w_ukv, m_w_out_ab, m_mix_norm_c, m_gqa_w_q, m_gqa_w_kv, m_gqa_q_norm, m_gqa_k_norm, m_gqa_w_o, m_ffn_norm, m_ffn_w_in, m_ffn_w_out, m_final_norm, v_mix_norm_ab, v_w_in_ab, v_mla_q_norm, v_mla_kv_norm, v_mla_w_uq, v_mla_w_ukv, v_w_out_ab, v_mix_norm_c, v_gqa_w_q, v_gqa_w_kv, v_gqa_q_norm, v_gqa_k_norm, v_gqa_w_o, v_ffn_norm, v_ffn_w_in, v_ffn_w_out, v_final_norm):
    wts = dict(mix_norm_ab=mix_norm_ab, w_in_ab=w_in_ab, mla_q_norm=mla_q_norm, mla_kv_norm=mla_kv_norm,
               mla_w_uq=mla_w_uq, mla_w_ukv=mla_w_ukv, w_out_ab=w_out_ab, mix_norm_c=mix_norm_c, gqa_w_q=gqa_w_q,
               gqa_w_kv=gqa_w_kv, gqa_q_norm=gqa_q_norm, gqa_k_norm=gqa_k_norm, gqa_w_o=gqa_w_o, ffn_norm=ffn_norm,
               ffn_w_in=ffn_w_in, ffn_w_out=ffn_w_out, final_norm=final_norm)
    mom = dict(mix_norm_ab=m_mix_norm_ab, w_in_ab=m_w_in_ab, mla_q_norm=m_mla_q_norm, mla_kv_norm=m_mla_kv_norm,
               mla_w_uq=m_mla_w_uq, mla_w_ukv=m_mla_w_ukv, w_out_ab=m_w_out_ab, mix_norm_c=m_mix_norm_c,
               gqa_w_q=m_gqa_w_q, gqa_w_kv=m_gqa_w_kv, gqa_q_norm=m_gqa_q_norm, gqa_k_norm=m_gqa_k_norm,
               gqa_w_o=m_gqa_w_o, ffn_norm=m_ffn_norm, ffn_w_in=m_ffn_w_in, ffn_w_out=m_ffn_w_out,
               final_norm=m_final_norm)
    var = dict(mix_norm_ab=v_mix_norm_ab, w_in_ab=v_w_in_ab, mla_q_norm=v_mla_q_norm, mla_kv_norm=v_mla_kv_norm,
               mla_w_uq=v_mla_w_uq, mla_w_ukv=v_mla_w_ukv, w_out_ab=v_w_out_ab, mix_norm_c=v_mix_norm_c,
               gqa_w_q=v_gqa_w_q, gqa_w_kv=v_gqa_w_kv, gqa_q_norm=v_gqa_q_norm, gqa_k_norm=v_gqa_k_norm,
               gqa_w_o=v_gqa_w_o, ffn_norm=v_ffn_norm, ffn_w_in=v_ffn_w_in, ffn_w_out=v_ffn_w_out,
               final_norm=v_final_norm)
    big_names = [n for n, _ in _BIG]
    groups = (_FIRST, _REST)
    big_w, big_m, big_v = (_stored({n: src[n] for n in big_names}) for src in (wts, mom, var))
    w_grp = [_group_of(big_w, grp) for grp in groups]
    shapes = [{n: a.shape for n, a in w.items()} for w in w_grp]
    w_packs = [_pack_big(w) for w in w_grp]
    n_rows = [p.shape[0] for p in w_packs]
    xs = x[0]
    s = xs.shape[0]
    me = 4 * lax.axis_index("x") + 2 * lax.axis_index("y") + lax.axis_index("c")
    c_cols = mix_norm_c.shape[1]

    tail = jnp.pad(mix_norm_c.reshape(-1), (0, GATHER_TAIL_ROWS // 2 * PACK_COLS - mix_norm_c.size))
    tail = _to_bits16(tail.reshape(GATHER_TAIL_ROWS // 2, PACK_COLS))
    gathered = _all_gather(jnp.concatenate([w_packs[0].astype(jnp.bfloat16), tail], axis=0))
    full = [_unpack_gathered(gathered[:, :n_rows[0]], shapes[0]), None]
    c_all = _from_bits16(gathered[:, n_rows[0]:]).reshape(N_DEV, -1)[:, :mix_norm_c.size]
    c_full = c_all.reshape(N_DEV, 2, c_cols).transpose(1, 0, 2).reshape(2, N_DEV * c_cols)
    gains = dict(mix_norm_ab=mix_norm_ab, mla_q_norm=mla_q_norm, mla_kv_norm=mla_kv_norm, mix_norm_c=c_full,
                 gqa_q_norm=gqa_q_norm, gqa_k_norm=gqa_k_norm, ffn_norm=ffn_norm)

    tabs = _rope_tables(s)
    slopes = jnp.exp2(-8.0 * jnp.arange(1, DIL_HEADS + 1, dtype=_F32) / DIL_HEADS)

    h = xs
    saved = []
    for layer in range(DEPTH):
        mix_w, ffn_w = _layer_weights(full, gains, layer)
        if layer == 0:
            def late(rest):
                full[1] = _unpack_gathered(rest, shapes[1])
                return _layer_weights(full, gains, 0)[0]

            h, sv_mix, _ = _even_fwd(h, mix_w, tabs, slopes, f"l{layer}_mix", gather=w_packs[1].astype(jnp.bfloat16),
                                     late=late)
            mix_w, ffn_w = _layer_weights(full, gains, 0)
        elif layer % 2 == 0:
            h, sv_mix, _ = _even_fwd(h, mix_w, tabs, slopes, f"l{layer}_mix")
        else:
            h, sv_mix = _odd_fwd(h, mix_w, tabs, f"l{layer}_mix")
        h, sv_ffn = _ffn_fwd(h, ffn_w, f"l{layer}_ffn")
        saved.append((mix_w, ffn_w, sv_mix, sv_ffn))
    loss_local, dh, d_final = _final_loss(h, final_norm, loss_target[0], name="final_loss")

    gfull = {n: [None] * wts[n].shape[0] for n in big_names}
    gsmall = {n: [None] * (wts[n].shape[0] if wts[n].ndim > 1 else 1) for n in _SMALL}
    gsmall["final_norm"][0] = d_final
    for layer in reversed(range(DEPTH)):
        mix_w, ffn_w, sv_mix, sv_ffn = saved[layer]
        i = layer // 2
        dh, gf = _ffn_bwd(dh, sv_ffn, ffn_w, f"l{layer}_ffn")
        gsmall["ffn_norm"][layer] = gf["norm"]
        gfull["ffn_w_in"][layer] = gf["w_in_t"]
        gfull["ffn_w_out"][layer] = gf["w_out"]
        if layer % 2 == 0:
            def sending(d_w_out):
                gfull["w_out_ab"][0] = d_w_out
                return _pack_grads(gfull, _REST).astype(jnp.bfloat16)

            dh, gm, got = _even_bwd(dh, sv_mix, mix_w, tabs, slopes, f"l{layer}_mix",
                                    exchange=sending if layer == 0 else None)
            if layer == 0:
                received_rest = got
            gsmall["mix_norm_ab"][i] = gm["norm"]
            gsmall["mla_q_norm"][i] = gm["q_norm"]
            gsmall["mla_kv_norm"][i] = gm["kv_norm"]
            gfull["w_in_ab"][i] = jnp.concatenate([gm["w_a"][:, :IN_A], _group_major(gm["w_b"])], axis=-1)
            gfull["mla_w_uq"][i] = _merge_heads_cols(gm["w_uq"], MLA_HEADS, MLA_NOPE)
            gfull["mla_w_ukv"][i] = _merge_heads_cols(gm["w_ukv"], MLA_HEADS, MLA_NOPE)
            gfull["w_out_ab"][i] = gm["w_out"]
        else:
            dh, gm = _odd_bwd(dh, sv_mix, mix_w, tabs, f"l{layer}_mix")
            gsmall["mix_norm_c"][i] = gm["norm"]
            gsmall["gqa_q_norm"][i] = gm["q_norm"]
            gsmall["gqa_k_norm"][i] = gm["k_norm"]
            gfull["gqa_w_q"][i] = gm["w_q"]
            gfull["gqa_w_kv"][i] = gm["w_kv"]
            gfull["gqa_w_o"][i] = gm["w_o"]
    grad_x = dh[None]

    small_part = _pack_small({n: jnp.stack(gsmall[n]) for n in _SMALL})
    small_bits = jnp.broadcast_to(_to_bits16(small_part)[None], (N_DEV, EXCHANGE_TAIL_ROWS, PACK_COLS))
    received_first = _exchange(jnp.concatenate([_pack_grads(gfull, _FIRST).astype(jnp.bfloat16), small_bits], axis=1))
    big_packs = [_reduce_adamw(got, w_packs[gi], _pack_big(_group_of(big_m, grp)), _pack_big(_group_of(big_v, grp)),
                               name=f"adamw_big{gi}")
                 for gi, (grp, got) in enumerate(zip(groups, (received_first, received_rest)))]

    def widen_c(shard):
        return lax.dynamic_update_slice(jnp.zeros((2, N_DEV * c_cols), _F32), shard, (0, me * c_cols))

    def small_of(src):
        return _pack_small({n: (widen_c(src[n]) if n == "mix_norm_c" else src[n]) for n in _SMALL})

    small_recv = _from_bits16(received_first[:, n_rows[0]:])
    small_packs = _reduce_adamw(small_recv, small_of(wts), small_of(mom), small_of(var), name="adamw_small")

    def outputs_of(which):
        by_group = [_unpack_big(big_packs[gi][which], shapes[gi]) for gi in range(len(groups))]
        small = _unpack_small(small_packs[which])
        res = _stored({n: jnp.concatenate([grp[n] for grp in by_group if n in grp], axis=0) for n in big_names})
        for n in wts:
            if n in big_names:
                continue
            if n == "mix_norm_c":
                res[n] = lax.dynamic_slice(small[n].reshape(2, N_DEV * c_cols), (0, me * c_cols), (2, c_cols))
            else:
                res[n] = small[n].reshape(wts[n].shape)
        return [res[n] for n in wts]

    loss = lax.psum(loss_local, _AXES)
    return (loss, grad_x, *outputs_of(0), *outputs_of(1), *outputs_of(2), *outputs_of(3))
```

```python
import functools

import jax
import jax.numpy as jnp
from jax import lax
from jax.experimental import pallas as pl
from jax.experimental.pallas import tpu as pltpu

D_MODEL = 1024
DEPTH = 4
GRID_W = 64
NORM_EPS = 1e-6
ROPE_THETA = 10000.0
NEG_INF = -1e30
MLA_HEADS = 8
MLA_Q_RANK = 384
MLA_KV_RANK = 256
MLA_NOPE = 64
MLA_ROPE = 32
MLA_V = 64
DIL_PAIRS = ((128, 1), (512, 4), (2048, 16))
DIL_HALF = 64
DIL_SLOTS = 4
DIL_GROUPS = 3
DIL_HEADS = 12
DIL_HEAD_DIM = 64
GQA_HEADS = 16
GQA_KV_HEADS = 4
GQA_HEAD_DIM = 64
FFN_HIDDEN = 2816
IN_A = MLA_Q_RANK + MLA_KV_RANK + MLA_ROPE
IN_A_PAD = 768
IN_B = 3 * DIL_HEADS * DIL_HEAD_DIM
ADAM_LR = 0.001
ADAM_B1 = 0.9
ADAM_B2 = 0.999
ADAM_EPS = 1e-08
ADAM_WD = 0.01
ADAM_STEP = 10

LANES = 128
SUBLANES_16BIT = 16
VMEM_LIMIT_BYTES = 56 * 1024 * 1024

MM_TILE = 1408
ROW_BLOCK = 512
ROT_CHUNK = 32

N_DEV = 8
PACK_COLS = 1024

_MXU = jnp.bfloat16
_ACT = jnp.bfloat16
_F32 = jnp.float32

_AXES = ("x", "y", "c")


def _params(sem):
    return pltpu.CompilerParams(dimension_semantics=sem, vmem_limit_bytes=VMEM_LIMIT_BYTES)


def _pick(n, cap):
    for t in range(cap - cap % LANES, 0, -LANES):
        if n % t == 0:
            return t
    return n


def _rows(m, target):
    t = m
    while t > target and t % 2 == 0:
        t //= 2
    return t


def _matmul(a, b, *, trans_a=False, trans_b=False, res=None, scale=None, out_dtype=_F32, name):
    if trans_a:
        k, m = a.shape
    else:
        m, k = a.shape
    if trans_b:
        n, kb = b.shape
    else:
        kb, n = b.shape
    assert k == kb, (a.shape, b.shape)
    tm, tn, tk = _pick(m, MM_TILE), _pick(n, MM_TILE), _pick(k, MM_TILE)
    nk = k // tk
    dims = (((0 if trans_a else 1,), (1 if trans_b else 0,)), ((), ()))

    def body(*refs):
        if res is None:
            a_ref, b_ref, o_ref, acc = refs
            r_ref = None
        else:
            a_ref, b_ref, r_ref, o_ref, acc = refs
        kk = pl.program_id(2)

        @pl.when(kk == 0)
        def _():
            acc[...] = jnp.zeros_like(acc)

        acc[...] += lax.dot_general(a_ref[...].astype(_MXU), b_ref[...].astype(_MXU), dims,
                                    preferred_element_type=_F32)

        @pl.when(kk == nk - 1)
        def _():
            r = acc[...]
            if scale is not None:
                r = r * scale
            if r_ref is not None:
                r = r + r_ref[...].astype(_F32)
            o_ref[...] = r.astype(out_dtype)

    a_spec = (pl.BlockSpec((tk, tm), lambda i, j, kk: (kk, i)) if trans_a
              else pl.BlockSpec((tm, tk), lambda i, j, kk: (i, kk)))
    b_spec = (pl.BlockSpec((tn, tk), lambda i, j, kk: (j, kk)) if trans_b
              else pl.BlockSpec((tk, tn), lambda i, j, kk: (kk, j)))
    o_spec = pl.BlockSpec((tm, tn), lambda i, j, kk: (i, j))
    in_specs = [a_spec, b_spec] + ([o_spec] if res is not None else [])
    args = (a, b) + ((res,) if res is not None else ())
    return pl.pallas_call(
        body, name=name, grid=(m // tm, n // tn, nk),
        in_specs=in_specs, out_specs=o_spec,
        out_shape=jax.ShapeDtypeStruct((m, n), out_dtype),
        scratch_shapes=[pltpu.VMEM((tm, tn), _F32)],
        compiler_params=_params(("parallel", "parallel", "arbitrary")),
    )(*args)


def _rmsnorm(x, g, *, out_dtype, name, rows=ROW_BLOCK):
    m, d = x.shape
    tm = _rows(m, rows)

    def body(x_ref, g_ref, o_ref):
        xf = x_ref[...].astype(_F32)
        r = lax.rsqrt(jnp.mean(xf * xf, axis=-1, keepdims=True) + NORM_EPS)
        o_ref[...] = ((xf * r) * g_ref[...]).astype(out_dtype)

    return pl.pallas_call(
        body, name=name, grid=(m // tm,),
        in_specs=[pl.BlockSpec((tm, d), lambda i: (i, 0)), pl.BlockSpec((1, d), lambda i: (0, 0))],
        out_specs=pl.BlockSpec((tm, d), lambda i: (i, 0)),
        out_shape=jax.ShapeDtypeStruct((m, d), out_dtype),
        compiler_params=_params(("parallel",)),
    )(x, g.reshape(1, d).astype(_F32))


def _rmsnorm_bwd(x, g, dy, dres=None, *, out_dtype=_F32, name, rows=ROW_BLOCK):
    m, d = x.shape
    tm = _rows(m, rows)

    def body(*refs):
        if dres is None:
            x_ref, g_ref, dy_ref, dx_ref, dg_ref = refs
            r_ref = None
        else:
            x_ref, g_ref, dy_ref, r_ref, dx_ref, dg_ref = refs

        @pl.when(pl.program_id(0) == 0)
        def _():
            dg_ref[...] = jnp.zeros_like(dg_ref)

        xf = x_ref[...].astype(_F32)
        r = lax.rsqrt(jnp.mean(xf * xf, axis=-1, keepdims=True) + NORM_EPS)
        xh = xf * r
        dyf = dy_ref[...].astype(_F32)
        dg_ref[...] += jnp.sum(dyf * xh, axis=0, keepdims=True)
        gdy = dyf * g_ref[...]
        dx = r * (gdy - xh * jnp.mean(gdy * xh, axis=-1, keepdims=True))
        if r_ref is not None:
            dx = dx + r_ref[...].astype(_F32)
        dx_ref[...] = dx.astype(out_dtype)

    row = pl.BlockSpec((tm, d), lambda i: (i, 0))
    vec = pl.BlockSpec((1, d), lambda i: (0, 0))
    in_specs = [row, vec, row] + ([row] if dres is not None else [])
    args = (x, g.reshape(1, d).astype(_F32), dy) + ((dres,) if dres is not None else ())
    dx, dg = pl.pallas_call(
        body, name=name, grid=(m // tm,),
        in_specs=in_specs, out_specs=[row, vec],
        out_shape=[jax.ShapeDtypeStruct((m, d), out_dtype), jax.ShapeDtypeStruct((1, d), _F32)],
        compiler_params=_params(("arbitrary",)),
    )(*args)
    return dx, dg.reshape(d)


def _rotate(xf, c, sn):
    w = xf.shape[1]
    half = ROT_CHUNK // 2
    if w > LANES:
        c, sn = jnp.tile(c, (1, w // LANES)), jnp.tile(sn, (1, w // LANES))
    lane = lax.broadcasted_iota(jnp.int32, xf.shape, 1)
    sw = jnp.where((lane & (ROT_CHUNK - 1)) < half, pltpu.roll(xf, w - half, 1), pltpu.roll(xf, half, 1))
    return xf * c + sw * sn


def _seg_mean(v, seg_ref):
    outs = []
    for c in range(v.shape[1] // LANES):
        piece = v[:, c * LANES:(c + 1) * LANES]
        hi = piece.astype(jnp.bfloat16)
        lo = (piece - hi.astype(_F32)).astype(jnp.bfloat16)
        outs.append(jnp.dot(hi, seg_ref[...], preferred_element_type=_F32)
                    + jnp.dot(lo, seg_ref[...], preferred_element_type=_F32))
    return jnp.concatenate(outs, axis=1) if len(outs) > 1 else outs[0]


def _seg_matrix():
    lane = jnp.arange(LANES) // GQA_HEAD_DIM
    return ((lane[:, None] == lane[None, :]).astype(_F32) / GQA_HEAD_DIM).astype(jnp.bfloat16)


def _headnorm_rope(x, gain, cos_t, sin_t, *, scale=None, name):
    s, w = x.shape
    ts = _rows(s, ROW_BLOCK)

    def body(x_ref, g_ref, seg_ref, c_ref, s_ref, o_ref):
        xf = x_ref[...]
        r = lax.rsqrt(_seg_mean(xf * xf, seg_ref) + NORM_EPS)
        y = _rotate((xf * r) * g_ref[...], c_ref[...], s_ref[...])
        if scale is not None:
            y = y * scale
        o_ref[...] = y.astype(o_ref.dtype)

    row = pl.BlockSpec((ts, w), lambda i: (i, 0))
    tab = pl.BlockSpec((ts, LANES), lambda i: (i, 0))
    return pl.pallas_call(
        body, name=name, grid=(s // ts,),
        in_specs=[row, pl.BlockSpec((1, w), lambda i: (0, 0)), pl.BlockSpec((LANES, LANES), lambda i: (0, 0)), tab, tab],
        out_specs=row, out_shape=jax.ShapeDtypeStruct((s, w), _ACT),
        compiler_params=_params(("parallel",)),
    )(x, jnp.tile(gain.astype(_F32), w // GQA_HEAD_DIM).reshape(1, w), _seg_matrix(), cos_t, sin_t)


def _headnorm_rope_bwd(x, gain, dy, cos_t, sin_t, *, name):
    s, w = x.shape
    ts = _rows(s, ROW_BLOCK)

    def body(x_ref, g_ref, seg_ref, c_ref, s_ref, dy_ref, dx_ref, dg_ref):
        @pl.when(pl.program_id(0) == 0)
        def _():
            dg_ref[...] = jnp.zeros_like(dg_ref)

        xf = x_ref[...]
        r = lax.rsqrt(_seg_mean(xf * xf, seg_ref) + NORM_EPS)
        xh = xf * r
        dyn = _rotate(dy_ref[...].astype(_F32), c_ref[...], -s_ref[...])
        dg_ref[...] += jnp.sum(dyn * xh, axis=0, keepdims=True)
        gdy = dyn * g_ref[...]
        dx_ref[...] = (r * (gdy - xh * _seg_mean(gdy * xh, seg_ref))).astype(dx_ref.dtype)

    row = pl.BlockSpec((ts, w), lambda i: (i, 0))
    vec = pl.BlockSpec((1, w), lambda i: (0, 0))
    tab = pl.BlockSpec((ts, LANES), lambda i: (i, 0))
    dx, dg = pl.pallas_call(
        body, name=name, grid=(s // ts,),
        in_specs=[row, vec, pl.BlockSpec((LANES, LANES), lambda i: (0, 0)), tab, tab, row],
        out_specs=[row, vec],
        out_shape=[jax.ShapeDtypeStruct((s, w), _ACT), jax.ShapeDtypeStruct((1, w), _F32)],
        compiler_params=_params(("arbitrary",)),
    )(x, jnp.tile(gain.astype(_F32), w // GQA_HEAD_DIM).reshape(1, w), _seg_matrix(), cos_t, sin_t, dy)
    return dx, dg.reshape(w // GQA_HEAD_DIM, GQA_HEAD_DIM).sum(axis=0)


def _rope(x, cos_t, sin_t, *, out_dtype, name, sum_chunks=False, scale=None):
    s, w = x.shape
    assert w % LANES == 0
    ts = _rows(s, ROW_BLOCK)
    ow = LANES if sum_chunks else w

    def body(x_ref, c_ref, s_ref, o_ref):
        y = _rotate(x_ref[...].astype(_F32), c_ref[...], s_ref[...])
        if scale is not None:
            y = y * scale
        if sum_chunks:
            shift = w // 2
            while shift >= ROT_CHUNK:
                y = y + pltpu.roll(y, shift, 1)
                shift //= 2
            y = y[:, :LANES]
        o_ref[...] = y.astype(out_dtype)

    return pl.pallas_call(
        body, name=name, grid=(s // ts,),
        in_specs=[pl.BlockSpec((ts, w), lambda i: (i, 0)), pl.BlockSpec((ts, LANES), lambda i: (i, 0)),
                  pl.BlockSpec((ts, LANES), lambda i: (i, 0))],
        out_specs=pl.BlockSpec((ts, ow), lambda i: (i, 0)),
        out_shape=jax.ShapeDtypeStruct((s, ow), out_dtype),
        compiler_params=_params(("parallel",)),
    )(x, cos_t, sin_t)


_NT = (((1,), (1,)), ((), ()))
_TN = (((0,), (0,)), ((), ()))
LOG2_E = 1.4426950408889634
LN_2 = 0.6931471805599453
ATTN_FWD_ROWS = 2048
ATTN_ROWS = 512
ATTN_CHAINS = 2
ATTN_FWD_KEYS = 1024
ATTN_BWD_KEYS = 2048


def _grid_step(dims):
    step, total = 0, 1
    for axis, n in enumerate(dims):
        step = step * n + pl.program_id(axis)
        total *= n
    return step, total


def _attn_fwd(qt, k, vt, *, out_dtype, name, gather=None):
    h, dk, s = qt.shape
    g, nk, dv, tk = vt.shape
    assert nk * tk == s
    r = h // g
    rc = min(ATTN_FWD_ROWS, s)
    hp = min(r, ATTN_CHAINS)
    nrc = max(1, min(ATTN_CHAINS // hp, s // rc))
    tq = rc * nrc
    nhp, nq = r // hp, s // tq
    units = [(a, c) for a in range(hp) for c in range(nrc)]

    def body(q_ref, k_ref, v_ref, *rest):
        if gather is None:
            o_ref, lse_ref = rest
        else:
            x_ref, o_ref, lse_ref, all_ref, send_sems, recv_sems, local_sem = rest
            start, forward, finish = _gather_phases(x_ref, all_ref, send_sems, recv_sems, local_sem)
            step, total = _grid_step((g, nhp, nq))
            pl.when(step == 0)(start)
            pl.when(step == 3 * total // 4)(forward)

        qs = [q_ref[a, :, c * rc:(c + 1) * rc] for a, c in units]
        init = tuple((jnp.full((1, rc), NEG_INF, _F32), jnp.zeros((1, rc), _F32), jnp.zeros((dv, rc), _F32))
                     for _ in units)

        def trip(j, carry):
            kb, vb = k_ref[0, pl.ds(pl.multiple_of(j * tk, tk), tk), :], v_ref[0, j]
            out = []
            for u in range(len(units)):
                m, l, acc = carry[u]
                sc = jnp.dot(kb, qs[u], preferred_element_type=_F32)
                m_new = jnp.maximum(m, jnp.max(sc, axis=0, keepdims=True))
                p = jnp.exp2(sc - m_new)
                alpha = jnp.exp2(m - m_new)
                l = alpha * l + jnp.sum(p, axis=0, keepdims=True)
                acc = alpha * acc + jnp.dot(vb, p.astype(_MXU), preferred_element_type=_F32)
                out.append((m_new, l, acc))
            return tuple(out)

        fin = lax.fori_loop(0, nk, trip, init)
        for u, (a, c) in enumerate(units):
            m, l, acc = fin[u]
            o_ref[a, :, c * rc:(c + 1) * rc] = (acc / l).astype(out_dtype)
            lse_ref[a, :, c * rc:(c + 1) * rc] = m + jnp.log2(l)

        if gather is not None:
            pl.when(step == total - 1)(finish)

    q_blk = lambda gg, hh, i: (gg * nhp + hh, 0, i)
    in_specs = [pl.BlockSpec((hp, dk, tq), q_blk), pl.BlockSpec((1, s, dk), lambda gg, hh, i: (gg, 0, 0)),
                pl.BlockSpec((1, nk, dv, tk), lambda gg, hh, i: (gg, 0, 0, 0))]
    out_specs = [pl.BlockSpec((hp, dv, tq), q_blk), pl.BlockSpec((hp, 1, tq), q_blk)]
    out_shape = [jax.ShapeDtypeStruct((h, dv, s), out_dtype), jax.ShapeDtypeStruct((h, 1, s), _F32)]
    if gather is None:
        return pl.pallas_call(
            body, name=name, grid=(g, nhp, nq), in_specs=in_specs, out_specs=out_specs, out_shape=out_shape,
            compiler_params=_params(("parallel", "parallel", "parallel")),
        )(qt, k, vt)
    return pl.pallas_call(
        body, name=name, grid=(g, nhp, nq),
        in_specs=in_specs + [pl.BlockSpec(memory_space=pl.ANY)],
        out_specs=out_specs + [pl.BlockSpec(memory_space=pl.ANY)],
        out_shape=out_shape + [jax.ShapeDtypeStruct((N_DEV,) + gather.shape, gather.dtype)],
        scratch_shapes=_comm_scratch(),
        compiler_params=_params(("arbitrary", "arbitrary", "arbitrary")),
    )(qt, k, vt, gather)


def _attn_bwd(qt, k, v, ot, do, lse2, *, scale, name, exchange=None):
    h, dk, s = qt.shape
    g, _, dv = v.shape
    r = h // g
    hp = min(r, ATTN_CHAINS)
    tq = min(ATTN_ROWS, s)
    tk = min(ATTN_BWD_KEYS, s)
    nhp, nq, nk = r // hp, s // tq, s // tk

    def body(qt_ref, k_ref, v_ref, ot_ref, do_ref, lse_ref, *rest):
        if exchange is None:
            dq_ref, dk_ref, dv_ref = rest
        else:
            g_ref, dq_ref, dk_ref, dv_ref, got_ref, send_sems, recv_sems, local_sem = rest
            start, finish = _exchange_phases(g_ref, got_ref, send_sems, recv_sems, local_sem)
            step, total = _grid_step((g, nhp, nq))
            pl.when(step == 0)(start)
        hh, i = pl.program_id(1), pl.program_id(2)

        @pl.when((hh == 0) & (i == 0))
        def _():
            dk_ref[...] = jnp.zeros_like(dk_ref)
            dv_ref[...] = jnp.zeros_like(dv_ref)

        qts = [qt_ref[a] for a in range(hp)]
        qs = [qt_ref[a].T for a in range(hp)]
        dos = [do_ref[a] for a in range(hp)]
        dots = [do_ref[a].T for a in range(hp)]
        ls = [lse_ref[a].T for a in range(hp)]
        dls = [jnp.sum(do_ref[a].astype(_F32) * ot_ref[a].T.astype(_F32), axis=1, keepdims=True)
               for a in range(hp)]

        def trip(j, carry):
            rows = pl.ds(pl.multiple_of(j * tk, tk), tk)
            kb, vb = k_ref[0, rows, :], v_ref[0, rows, :]
            out = []
            for a in range(hp):
                p = jnp.exp2(lax.dot_general(qs[a], kb, _NT, preferred_element_type=_F32) - ls[a])
                dp = lax.dot_general(dos[a], vb, _NT, preferred_element_type=_F32)
                ds = (p * (dp - dls[a])).astype(_MXU)
                dv_ref[0, j] += jnp.dot(dots[a], p.astype(_MXU), preferred_element_type=_F32)
                dk_ref[0, j] += jnp.dot(qts[a], ds, preferred_element_type=_F32)
                out.append(carry[a] + jnp.dot(ds, kb, preferred_element_type=_F32))
            return tuple(out)

        fin = lax.fori_loop(0, nk, trip, tuple(jnp.zeros((tq, dk), _F32) for _ in range(hp)))
        for a in range(hp):
            dq_ref[a] = (fin[a] * scale).astype(dq_ref.dtype)

        @pl.when((hh == nhp - 1) & (i == nq - 1))
        def _():
            dk_ref[...] = dk_ref[...] * LN_2

        if exchange is not None:
            pl.when(step == total - 1)(finish)

    q_blk = lambda gg, hh, i: (gg * nhp + hh, i, 0)
    qt_blk = lambda gg, hh, i: (gg * nhp + hh, 0, i)
    kv_blk = lambda gg, hh, i: (gg, 0, 0)
    in_specs = [pl.BlockSpec((hp, dk, tq), qt_blk), pl.BlockSpec((1, s, dk), kv_blk), pl.BlockSpec((1, s, dv), kv_blk),
                pl.BlockSpec((hp, dv, tq), qt_blk), pl.BlockSpec((hp, tq, dv), q_blk), pl.BlockSpec((hp, 1, tq), qt_blk)]
    out_specs = [pl.BlockSpec((hp, tq, dk), q_blk), pl.BlockSpec((1, nk, dk, tk), lambda gg, hh, i: (gg, 0, 0, 0)),
                 pl.BlockSpec((1, nk, dv, tk), lambda gg, hh, i: (gg, 0, 0, 0))]
    out_shape = [jax.ShapeDtypeStruct((h, s, dk), _ACT), jax.ShapeDtypeStruct((g, nk, dk, tk), _F32),
                 jax.ShapeDtypeStruct((g, nk, dv, tk), _F32)]
    args = (qt, k, v, ot, do, lse2)
    scratch = []
    if exchange is not None:
        in_specs, args = in_specs + [pl.BlockSpec(memory_space=pl.ANY)], args + (exchange,)
        out_specs = out_specs + [pl.BlockSpec(memory_space=pl.ANY)]
        out_shape = out_shape + [jax.ShapeDtypeStruct(exchange.shape, exchange.dtype)]
        scratch = _comm_scratch()
    return pl.pallas_call(
        body, name=name, grid=(g, nhp, nq), in_specs=in_specs, out_specs=out_specs, out_shape=out_shape,
        scratch_shapes=scratch, compiler_params=_params(("arbitrary", "arbitrary", "arbitrary")),
    )(*args)


def _unchunk(xt):
    g, nk, d, tk = xt.shape
    return xt.transpose(1, 3, 0, 2).reshape(nk * tk, g * d)


def _chunk_t(x2d, g):
    s = x2d.shape[0]
    tk = min(ATTN_FWD_KEYS, s)
    return x2d.reshape(s // tk, tk, g, -1).transpose(2, 0, 3, 1)


def _heads_t(x2d, h):
    s = x2d.shape[0]
    return x2d.reshape(s, h, -1).transpose(1, 2, 0)


def _unheads_t(xt):
    h, d, s = xt.shape
    return xt.transpose(2, 0, 1).reshape(s, h * d)


WIN_ROWS = 512


def _win_geometry(s, dil):
    t = min(WIN_ROWS, s)
    length = s // dil
    lg = length.bit_length() - 1
    assert 1 << lg == length and t % DIL_HALF == 0 and s % t == 0
    return t, t + 2 * DIL_HALF, lg, t // DIL_HALF, s // DIL_HALF


def _win_specs(hn, t, d, halo_per_blk, n_halo, part=0):
    return [pl.BlockSpec((hn, DIL_HALF, d), lambda i: (part, jnp.maximum(i * halo_per_blk - 1, 0), 0)),
            pl.BlockSpec((hn, t, d), lambda i: (part, i, 0)),
            pl.BlockSpec((hn, DIL_HALF, d), lambda i: (part, jnp.minimum((i + 1) * halo_per_blk, n_halo - 1), 0))]


def _win_mask(i, t, w, lg, dil, wide_rows):
    shape = (w, t) if wide_rows else (t, w)
    rows = lax.broadcasted_iota(jnp.int32, shape, 0)
    cols = lax.broadcasted_iota(jnp.int32, shape, 1)
    base = i * t
    if wide_rows:
        pq, pk = base - DIL_HALF + rows, base + cols
    else:
        pq, pk = base + rows, base - DIL_HALF + cols
    arel = jnp.abs(pk - pq)
    valid = (arel <= DIL_HALF) & ((pk >> lg) == (pq >> lg))
    return valid, (-dil * arel).astype(_F32)


def _win3(lo_ref, mid_ref, hi_ref, a):
    return jnp.concatenate([lo_ref[a], mid_ref[a], hi_ref[a]], axis=0)


def _win_fwd(qkv, slopes, *, dil, name):
    hn, s, d = qkv.shape[0] // 3, qkv.shape[1], qkv.shape[2]
    t, w, lg, hpb, n_halo = _win_geometry(s, dil)
    scale = d ** -0.5

    def body(sl_ref, q_ref, klo, kmid, khi, vlo, vmid, vhi, o_ref, lse_ref):
        valid, nb = _win_mask(pl.program_id(0), t, w, lg, dil, False)
        for a in range(hn):
            kw, vw = _win3(klo, kmid, khi, a), _win3(vlo, vmid, vhi, a)
            sc = lax.dot_general(q_ref[a], kw, _NT, preferred_element_type=_F32) * scale
            sc = jnp.where(valid, sc + sl_ref[a] * nb, NEG_INF)
            m = jnp.max(sc, axis=1, keepdims=True)
            e = jnp.exp(sc - m)
            den = jnp.sum(e, axis=1, keepdims=True)
            o_ref[a] = jnp.dot(e.astype(_MXU), vw, preferred_element_type=_F32) / den
            lse_ref[a] = m + jnp.log(den)

    blk = lambda c: pl.BlockSpec((hn, t, c), lambda i: (0, i, 0))
    return pl.pallas_call(
        body, name=name, grid=(s // t,),
        in_specs=([pl.BlockSpec(memory_space=pltpu.SMEM), blk(d)] + _win_specs(hn, t, d, hpb, n_halo, 1)
                  + _win_specs(hn, t, d, hpb, n_halo, 2)),
        out_specs=[blk(d), blk(1)],
        out_shape=[jax.ShapeDtypeStruct((hn, s, d), _F32), jax.ShapeDtypeStruct((hn, s, 1), _F32)],
        compiler_params=_params(("parallel",)),
    )(slopes.astype(_F32), qkv, qkv, qkv, qkv, qkv, qkv, qkv)


def _win_bwd_dq(qkv, do, lse, delta, slopes, *, dil, name):
    hn, s, d = qkv.shape[0] // 3, qkv.shape[1], qkv.shape[2]
    t, w, lg, hpb, n_halo = _win_geometry(s, dil)
    scale = d ** -0.5

    def body(sl_ref, q_ref, klo, kmid, khi, vlo, vmid, vhi, do_ref, lse_ref, dl_ref, dq_ref):
        valid, nb = _win_mask(pl.program_id(0), t, w, lg, dil, False)
        for a in range(hn):
            kw, vw = _win3(klo, kmid, khi, a), _win3(vlo, vmid, vhi, a)
            sc = lax.dot_general(q_ref[a], kw, _NT, preferred_element_type=_F32) * scale
            p = jnp.exp(jnp.where(valid, sc + sl_ref[a] * nb, NEG_INF) - lse_ref[a])
            dp = lax.dot_general(do_ref[a], vw, _NT, preferred_element_type=_F32)
            ds = (p * (dp - dl_ref[a])).astype(_MXU)
            dq_ref[a] = (jnp.dot(ds, kw, preferred_element_type=_F32) * scale).astype(dq_ref.dtype)

    blk = lambda c: pl.BlockSpec((hn, t, c), lambda i: (0, i, 0))
    return pl.pallas_call(
        body, name=name, grid=(s // t,),
        in_specs=([pl.BlockSpec(memory_space=pltpu.SMEM), blk(d)] + _win_specs(hn, t, d, hpb, n_halo, 1)
                  + _win_specs(hn, t, d, hpb, n_halo, 2) + [blk(d), blk(1), blk(1)]),
        out_specs=blk(d),
        out_shape=jax.ShapeDtypeStruct((hn, s, d), _ACT),
        compiler_params=_params(("parallel",)),
    )(slopes.astype(_F32), qkv, qkv, qkv, qkv, qkv, qkv, qkv, do, lse, delta)


def _win_bwd_dkv(qkv, do, lse, delta, slopes, *, dil, name):
    hn, s, d = qkv.shape[0] // 3, qkv.shape[1], qkv.shape[2]
    t, w, lg, hpb, n_halo = _win_geometry(s, dil)
    scale = d ** -0.5

    def body(sl_ref, qlo, qmid, qhi, dolo, domid, dohi, llo, lmid, lhi, dllo, dlmid, dlhi, k_ref, v_ref, dkv_ref):
        valid, nb = _win_mask(pl.program_id(0), t, w, lg, dil, True)
        for a in range(hn):
            qw, dow = _win3(qlo, qmid, qhi, a), _win3(dolo, domid, dohi, a)
            lw, dlw = _win3(llo, lmid, lhi, a), _win3(dllo, dlmid, dlhi, a)
            sc = lax.dot_general(qw, k_ref[a], _NT, preferred_element_type=_F32) * scale
            p = jnp.exp(jnp.where(valid, sc + sl_ref[a] * nb, NEG_INF) - lw)
            dp = lax.dot_general(dow, v_ref[a], _NT, preferred_element_type=_F32)
            ds = (p * (dp - dlw)).astype(_MXU)
            dkv_ref[hn + a] = lax.dot_general(p.astype(_MXU), dow, _TN,
                                              preferred_element_type=_F32).astype(dkv_ref.dtype)
            dkv_ref[a] = (lax.dot_general(ds, qw, _TN, preferred_element_type=_F32) * scale).astype(dkv_ref.dtype)

    part = lambda which: pl.BlockSpec((hn, t, d), lambda i: (which, i, 0))
    return pl.pallas_call(
        body, name=name, grid=(s // t,),
        in_specs=([pl.BlockSpec(memory_space=pltpu.SMEM)] + _win_specs(hn, t, d, hpb, n_halo)
                  + _win_specs(hn, t, d, hpb, n_halo) + _win_specs(hn, t, 1, hpb, n_halo)
                  + _win_specs(hn, t, 1, hpb, n_halo) + [part(1), part(2)]),
        out_specs=pl.BlockSpec((2 * hn, t, d), lambda i: (0, i, 0)),
        out_shape=jax.ShapeDtypeStruct((2 * hn, s, d), _ACT),
        compiler_params=_params(("parallel",)),
    )(slopes.astype(_F32), qkv, qkv, qkv, do, do, do, lse, lse, lse, delta, delta, delta, qkv, qkv)


def _merge_weights(lses):
    mx = functools.reduce(jnp.maximum, lses)
    es = [jnp.exp(l - mx) for l in lses]
    den = functools.reduce(lambda a, b: a + b, es)
    return [e / den for e in es]


def _merge_fwd(outs, lses, *, name):
    ng = len(outs)
    sl, s, d = outs[0].shape
    t = _rows(s, ROW_BLOCK)

    def body(*refs):
        o_refs, l_refs, c_ref = refs[:ng], refs[ng:2 * ng], refs[2 * ng]
        wts = _merge_weights([r[0] for r in l_refs])
        comb = functools.reduce(lambda a, b: a + b, [w * r[0] for w, r in zip(wts, o_refs)])
        c_ref[0] = comb.astype(c_ref.dtype)

    big = pl.BlockSpec((1, t, d), lambda a, i: (a, i, 0))
    small = pl.BlockSpec((1, t, 1), lambda a, i: (a, i, 0))
    return pl.pallas_call(
        body, name=name, grid=(sl, s // t),
        in_specs=[big] * ng + [small] * ng, out_specs=big,
        out_shape=jax.ShapeDtypeStruct((sl, s, d), _ACT),
        compiler_params=_params(("parallel", "parallel")),
    )(*outs, *lses)


def _merge_bwd(dcomb, outs, lses, *, name):
    ng = len(outs)
    sl, s, d = outs[0].shape
    t = _rows(s, ROW_BLOCK)

    def body(*refs):
        dc_ref, o_refs, l_refs = refs[0], refs[1:1 + ng], refs[1 + ng:1 + 2 * ng]
        do_refs, dl_refs = refs[1 + 2 * ng:1 + 3 * ng], refs[1 + 3 * ng:]
        wts = _merge_weights([r[0] for r in l_refs])
        dc = dc_ref[0].astype(_F32)
        comb = functools.reduce(lambda a, b: a + b, [w * r[0] for w, r in zip(wts, o_refs)])
        dot = jnp.sum(dc * comb, axis=-1, keepdims=True)
        for w, do_ref, dl_ref in zip(wts, do_refs, dl_refs):
            do_ref[0] = (w * dc).astype(do_ref.dtype)
            dl_ref[0] = w * dot

    big = pl.BlockSpec((1, t, d), lambda a, i: (a, i, 0))
    small = pl.BlockSpec((1, t, 1), lambda a, i: (a, i, 0))
    res = pl.pallas_call(
        body, name=name, grid=(sl, s // t),
        in_specs=[big] + [big] * ng + [small] * ng, out_specs=[big] * ng + [small] * ng,
        out_shape=[jax.ShapeDtypeStruct((sl, s, d), _ACT)] * ng + [jax.ShapeDtypeStruct((sl, s, 1), _F32)] * ng,
        compiler_params=_params(("parallel", "parallel")),
    )(dcomb, *outs, *lses)
    return res[:ng], res[ng:]


SWIGLU_ROWS = 512


def _gate_up(a, w_in_t, *, name):
    m, k = a.shape
    n = w_in_t.shape[0] // 2
    tm, tn, tk = _pick(m, SWIGLU_ROWS), _pick(n, MM_TILE), _pick(k, MM_TILE)
    nk = k // tk

    def body(a_ref, g_ref, u_ref, go_ref, uo_ref, act_ref, acc_g, acc_u):
        kk = pl.program_id(2)

        @pl.when(kk == 0)
        def _():
            acc_g[...] = jnp.zeros_like(acc_g)
            acc_u[...] = jnp.zeros_like(acc_u)

        av = a_ref[...].astype(_MXU)
        acc_g[...] += lax.dot_general(av, g_ref[...].astype(_MXU), _NT, preferred_element_type=_F32)
        acc_u[...] += lax.dot_general(av, u_ref[...].astype(_MXU), _NT, preferred_element_type=_F32)

        @pl.when(kk == nk - 1)
        def _():
            gf, uf = acc_g[...], acc_u[...]
            go_ref[...] = gf.astype(go_ref.dtype)
            uo_ref[...] = uf.astype(uo_ref.dtype)
            act_ref[...] = (gf * jax.nn.sigmoid(gf) * uf).astype(act_ref.dtype)

    o_spec = pl.BlockSpec((tm, tn), lambda i, j, kk: (i, j))
    out = jax.ShapeDtypeStruct((m, n), _ACT)
    return pl.pallas_call(
        body, name=name, grid=(m // tm, n // tn, nk),
        in_specs=[pl.BlockSpec((tm, tk), lambda i, j, kk: (i, kk)),
                  pl.BlockSpec((tn, tk), lambda i, j, kk: (j, kk)),
                  pl.BlockSpec((tn, tk), lambda i, j, kk: (j + n // tn, kk))],
        out_specs=[o_spec, o_spec, o_spec], out_shape=[out, out, out],
        scratch_shapes=[pltpu.VMEM((tm, tn), _F32), pltpu.VMEM((tm, tn), _F32)],
        compiler_params=_params(("parallel", "parallel", "arbitrary")),
    )(a, w_in_t, w_in_t)


def _gate_up_bwd(dout, w_out, gate, up, *, name):
    m, k = dout.shape
    n = w_out.shape[0]
    tm, tn, tk = _pick(m, SWIGLU_ROWS), _pick(n, MM_TILE), _pick(k, MM_TILE)
    nk = k // tk

    def body(d_ref, w_ref, g_ref, u_ref, dg_ref, du_ref, acc):
        kk = pl.program_id(2)

        @pl.when(kk == 0)
        def _():
            acc[...] = jnp.zeros_like(acc)

        acc[...] += lax.dot_general(d_ref[...].astype(_MXU), w_ref[...].astype(_MXU), _NT,
                                    preferred_element_type=_F32)

        @pl.when(kk == nk - 1)
        def _():
            daf = acc[...]
            gf = g_ref[...].astype(_F32)
            sg = jax.nn.sigmoid(gf)
            dg_ref[...] = (daf * u_ref[...].astype(_F32) * (sg + gf * sg * (1.0 - sg))).astype(dg_ref.dtype)
            du_ref[...] = (daf * (gf * sg)).astype(du_ref.dtype)

    o_spec = pl.BlockSpec((tm, tn), lambda i, j, kk: (i, j))
    out = jax.ShapeDtypeStruct((m, n), _ACT)
    return pl.pallas_call(
        body, name=name, grid=(m // tm, n // tn, nk),
        in_specs=[pl.BlockSpec((tm, tk), lambda i, j, kk: (i, kk)), pl.BlockSpec((tn, tk), lambda i, j, kk: (j, kk)),
                  o_spec, o_spec],
        out_specs=[o_spec, o_spec], out_shape=[out, out],
        scratch_shapes=[pltpu.VMEM((tm, tn), _F32)],
        compiler_params=_params(("parallel", "parallel", "arbitrary")),
    )(dout, w_out, gate, up)


def _final_loss(x, g, target, *, name):
    m, d = x.shape
    tm = _rows(m, ROW_BLOCK)

    def body(x_ref, g_ref, t_ref, loss_ref, dx_ref, dg_ref):
        @pl.when(pl.program_id(0) == 0)
        def _():
            loss_ref[...] = jnp.zeros_like(loss_ref)
            dg_ref[...] = jnp.zeros_like(dg_ref)

        xf = x_ref[...]
        r = lax.rsqrt(jnp.mean(xf * xf, axis=-1, keepdims=True) + NORM_EPS)
        xh = xf * r
        err = xh * g_ref[...] - t_ref[...]
        loss_ref[...] += 0.5 * jnp.sum(jnp.mean(err * err, axis=-1, keepdims=True))
        dy = err * (1.0 / d)
        dg_ref[...] += jnp.sum(dy * xh, axis=0, keepdims=True)
        gdy = dy * g_ref[...]
        dx_ref[...] = r * (gdy - xh * jnp.mean(gdy * xh, axis=-1, keepdims=True))

    row = pl.BlockSpec((tm, d), lambda i: (i, 0))
    vec = pl.BlockSpec((1, d), lambda i: (0, 0))
    loss, dx, dg = pl.pallas_call(
        body, name=name, grid=(m // tm,),
        in_specs=[row, vec, row],
        out_specs=[pl.BlockSpec((8, LANES), lambda i: (0, 0)), row, vec],
        out_shape=[jax.ShapeDtypeStruct((8, LANES), _F32), jax.ShapeDtypeStruct((m, d), _F32),
                   jax.ShapeDtypeStruct((1, d), _F32)],
        compiler_params=_params(("arbitrary",)),
    )(x, g.reshape(1, d), target)
    return loss[0, 0], dx, dg.reshape(d)


def _reduce_adamw(parts, w, m, v, *, name):
    rws, cols = w.shape
    tr = rws
    for cand in range(min(rws, 256), 0, -SUBLANES_16BIT):
        if cand % SUBLANES_16BIT == 0 and rws % cand == 0:
            tr = cand
            break

    def body(p_ref, w_ref, m_ref, v_ref, g_ref, d_ref, nm_ref, nv_ref):
        gsum = p_ref[0].astype(_F32)
        for dev in range(1, N_DEV):
            gsum = gsum + p_ref[dev].astype(_F32)
        m2 = ADAM_B1 * m_ref[...] + (1.0 - ADAM_B1) * gsum
        v2 = ADAM_B2 * v_ref[...] + (1.0 - ADAM_B2) * (gsum * gsum)
        m_hat = m2 / (1.0 - ADAM_B1 ** ADAM_STEP)
        v_hat = v2 / (1.0 - ADAM_B2 ** ADAM_STEP)
        g_ref[...] = gsum
        d_ref[...] = -ADAM_LR * (m_hat / (jnp.sqrt(v_hat) + ADAM_EPS) + ADAM_WD * w_ref[...])
        nm_ref[...] = m2
        nv_ref[...] = v2

    blk = pl.BlockSpec((tr, cols), lambda i: (i, 0))
    out = jax.ShapeDtypeStruct((rws, cols), _F32)
    return pl.pallas_call(
        body, name=name, grid=(rws // tr,),
        in_specs=[pl.BlockSpec((N_DEV, tr, cols), lambda i: (0, i, 0)), blk, blk, blk],
        out_specs=[blk, blk, blk, blk], out_shape=[out, out, out, out],
        compiler_params=_params(("parallel",)),
    )(parts, w, m, v)


def _mesh_pos():
    return lax.axis_index("x"), lax.axis_index("y"), lax.axis_index("c")


def _all_gather(block):
    rws, cols = block.shape

    def body(x_ref, out_ref, send_sems, recv_sems, local_sem):
        start, forward, finish = _gather_phases(x_ref, out_ref, send_sems, recv_sems, local_sem)
        start()
        forward()
        finish()

    return pl.pallas_call(
        body, name="weights_all_gather",
        out_shape=jax.ShapeDtypeStruct((N_DEV, rws, cols), block.dtype),
        in_specs=[pl.BlockSpec(memory_space=pl.ANY)],
        out_specs=pl.BlockSpec(memory_space=pl.ANY),
        scratch_shapes=_comm_scratch(),
    )(block)


def _comm_scratch():
    return [pltpu.SemaphoreType.DMA((N_DEV - 1,)), pltpu.SemaphoreType.DMA((N_DEV - 1,)), pltpu.SemaphoreType.DMA]


def _gather_phases(x_ref, out_ref, send_sems, recv_sems, local_sem):
    x, y, c = _mesh_pos()
    me, sibling = (x, y, c), (x, y, 1 - c)
    chips = [(1 - x, y), (x, 1 - y), (1 - x, 1 - y)]

    def slot(px, py, pc):
        return out_ref.at[4 * px + 2 * py + pc]

    def copy(k, blk, to, src=None):
        return pltpu.make_async_remote_copy(
            src_ref=slot(*blk) if src is None else src, dst_ref=slot(*blk),
            send_sem=send_sems.at[k], recv_sem=recv_sems.at[k],
            device_id=to, device_id_type=pl.DeviceIdType.MESH)

    mine = pltpu.make_async_copy(x_ref, slot(*me), local_sem)
    first = [copy(0, me, sibling, src=x_ref)]
    first += [copy(1 + j, me, (*chip, c), src=x_ref) for j, chip in enumerate(chips)]
    passed = [copy(4 + j, (*chip, c), sibling) for j, chip in enumerate(chips)]

    def start():
        mine.start()
        for cp in first:
            cp.start()

    def forward():
        for j, chip in enumerate(chips):
            copy(1 + j, (*chip, c), me).wait_recv()
            passed[j].start()

    def finish():
        copy(0, sibling, me).wait_recv()
        for j, chip in enumerate(chips):
            copy(4 + j, (*chip, 1 - c), me).wait_recv()
        for cp in first + passed:
            cp.wait_send()
        mine.wait()

    return start, forward, finish


def _exchange(parts):
    def body(g_ref, out_ref, send_sems, recv_sems, local_sem):
        start, finish = _exchange_phases(g_ref, out_ref, send_sems, recv_sems, local_sem)
        start()
        finish()

    return pl.pallas_call(
        body, name="grads_exchange",
        out_shape=jax.ShapeDtypeStruct(parts.shape, parts.dtype),
        in_specs=[pl.BlockSpec(memory_space=pl.ANY)],
        out_specs=pl.BlockSpec(memory_space=pl.ANY),
        scratch_shapes=_comm_scratch(),
    )(parts)


def _exchange_phases(g_ref, out_ref, send_sems, recv_sems, local_sem):
    x, y, c = _mesh_pos()
    me = 4 * x + 2 * y + c
    mine = pltpu.make_async_copy(g_ref.at[me], out_ref.at[me], local_sem)
    sends, arrivals = [], []
    for k in range(1, N_DEV):
        px = 1 - x if k & 4 else x
        py = 1 - y if k & 2 else y
        pc = 1 - c if k & 1 else c
        peer = 4 * px + 2 * py + pc
        sems = dict(send_sem=send_sems.at[k - 1], recv_sem=recv_sems.at[k - 1],
                    device_id=(px, py, pc), device_id_type=pl.DeviceIdType.MESH)
        sends.append(pltpu.make_async_remote_copy(src_ref=g_ref.at[peer], dst_ref=out_ref.at[me], **sems))
        arrivals.append(pltpu.make_async_remote_copy(src_ref=g_ref.at[peer], dst_ref=out_ref.at[peer], **sems))

    def start():
        mine.start()
        for cp in sends:
            cp.start()

    def finish():
        for cp in arrivals:
            cp.wait_recv()
        for cp in sends:
            cp.wait_send()
        mine.wait()

    return start, finish


_BIG = (("w_in_ab", 2), ("mla_w_uq", 1), ("mla_w_ukv", 1), ("w_out_ab", 2), ("gqa_w_q", 1), ("gqa_w_kv", 1),
        ("gqa_w_o", 1), ("ffn_w_in", 1), ("ffn_w_out", 1))
_TRANSPOSED = ("ffn_w_in",)


def _stored(arrays):
    return {n: (a.transpose(0, 2, 1) if n in _TRANSPOSED else a) for n, a in arrays.items()}
_SMALL = ("mix_norm_ab", "ffn_norm", "final_norm", "mix_norm_c", "mla_q_norm", "mla_kv_norm", "gqa_q_norm", "gqa_k_norm")
SMALL_ROWS = 16
GATHER_TAIL_ROWS = 16
EXCHANGE_TAIL_ROWS = 32


def _pad_rows(a2d):
    pad = -a2d.shape[0] % SUBLANES_16BIT
    return jnp.pad(a2d, ((0, pad), (0, 0))) if pad else a2d


def _pack_rows(shard):
    return _pad_rows(shard.reshape(-1, PACK_COLS))


def _packed_rows(shape):
    n = 1
    for d in shape:
        n *= d
    rows = n // PACK_COLS
    return rows + (-rows % SUBLANES_16BIT)


_FIRST = {"w_in_ab": (0, 1), "mla_w_uq": (0, 1), "mla_w_ukv": (0, 1)}
_REST = {"w_in_ab": (1, 1), "mla_w_uq": (1, 1), "mla_w_ukv": (1, 1), "w_out_ab": (0, 2), "gqa_w_q": (0, 2),
         "gqa_w_kv": (0, 2), "gqa_w_o": (0, 2), "ffn_w_in": (0, 4), "ffn_w_out": (0, 4)}
PACK_ROWS = 128


def _group_of(arrays, group):
    return {n: arrays[n][lo:lo + cnt] for n, (lo, cnt) in group.items()}


def _pack_big(shards):
    parts = [_pack_rows(shards[n]) for n, _ in _BIG if n in shards]
    fill = -sum(p.shape[0] for p in parts) % PACK_ROWS
    return jnp.concatenate(parts + [jnp.zeros((fill, PACK_COLS), parts[0].dtype)], axis=0)


def _unpack_big(packed, shapes):
    out, off = {}, 0
    for n, _ in _BIG:
        if n not in shapes:
            continue
        size = 1
        for d in shapes[n]:
            size *= d
        out[n] = packed[off:off + size // PACK_COLS].reshape(shapes[n])
        off += _packed_rows(shapes[n])
    return out


def _unpack_gathered(gathered, shapes):
    out, off = {}, 0
    for n, axis in _BIG:
        if n not in shapes:
            continue
        size = 1
        for d in shapes[n]:
            size *= d
        l3 = (shapes[n][0], shapes[n][1], size // (shapes[n][0] * shapes[n][1]))
        sh = gathered[:, off:off + size // PACK_COLS].reshape((N_DEV,) + l3)
        if axis == 1:
            full = sh.transpose(1, 0, 2, 3).reshape(l3[0], N_DEV * l3[1], l3[2])
        else:
            full = sh.transpose(1, 2, 0, 3).reshape(l3[0], l3[1], N_DEV * l3[2])
        out[n] = full
        off += _packed_rows(shapes[n])
    return out


def _split_for_devices(layers, axis):
    rws, cols = layers[0].shape
    if axis == 1:
        flat = jnp.concatenate([g.reshape(N_DEV, -1, PACK_COLS) for g in layers], axis=1)
    else:
        sh = jnp.stack(layers).reshape(len(layers), rws, N_DEV, cols // N_DEV).transpose(2, 0, 1, 3)
        flat = sh.reshape(N_DEV, -1, PACK_COLS)
    pad = -flat.shape[1] % SUBLANES_16BIT
    return jnp.pad(flat, ((0, 0), (0, pad), (0, 0))) if pad else flat


def _to_bits16(a_f32_rows):
    r = a_f32_rows.shape[0]
    return lax.bitcast_convert_type(a_f32_rows, jnp.bfloat16).reshape(2 * r, PACK_COLS)


def _from_bits16(a_bf16_rows):
    lead, r = a_bf16_rows.shape[:-2], a_bf16_rows.shape[-2]
    return lax.bitcast_convert_type(a_bf16_rows.reshape(lead + (r // 2, PACK_COLS, 2)), _F32)


def _small_sizes():
    return {"mix_norm_ab": 2 * D_MODEL, "ffn_norm": DEPTH * D_MODEL, "final_norm": D_MODEL, "mix_norm_c": 2 * D_MODEL,
            "mla_q_norm": 2 * MLA_Q_RANK, "mla_kv_norm": 2 * MLA_KV_RANK, "gqa_q_norm": 2 * GQA_HEAD_DIM,
            "gqa_k_norm": 2 * GQA_HEAD_DIM}


def _pack_small(vals):
    flat = jnp.concatenate([vals[n].reshape(-1).astype(_F32) for n in _SMALL])
    return jnp.pad(flat, (0, SMALL_ROWS * PACK_COLS - flat.shape[0])).reshape(SMALL_ROWS, PACK_COLS)


def _unpack_small(pack):
    flat, out, off = pack.reshape(-1), {}, 0
    sizes = _small_sizes()
    for n in _SMALL:
        out[n] = flat[off:off + sizes[n]]
        off += sizes[n]
    return out


def _angles(pos, dim):
    freqs = ROPE_THETA ** (-jnp.arange(0, dim, 2, dtype=_F32) / dim)
    ang = pos.astype(_F32)[:, None] * freqs[None, :]
    return jnp.cos(ang), jnp.sin(ang)


def _rope_tables(s):
    pos = jnp.arange(s)
    cos_t, sin_t = _angles(pos, MLA_ROPE)
    mla_c = jnp.tile(jnp.concatenate([cos_t, cos_t], -1), (1, LANES // ROT_CHUNK))
    mla_s = jnp.tile(jnp.concatenate([-sin_t, sin_t], -1), (1, LANES // ROT_CHUNK))
    rows = s // GRID_W
    row_idx = jnp.broadcast_to(jnp.arange(rows)[:, None], (rows, GRID_W)).reshape(-1)
    col_idx = jnp.broadcast_to(jnp.arange(GRID_W)[None, :], (rows, GRID_W)).reshape(-1)
    cos_r, sin_r = _angles(row_idx, GQA_HEAD_DIM // 2)
    cos_c, sin_c = _angles(col_idx, GQA_HEAD_DIM // 2)
    gqa_c = jnp.tile(jnp.concatenate([cos_r, cos_r, cos_c, cos_c], -1), (1, LANES // GQA_HEAD_DIM))
    gqa_s = jnp.tile(jnp.concatenate([-sin_r, sin_r, -sin_c, sin_c], -1), (1, LANES // GQA_HEAD_DIM))
    return (mla_c, mla_s), (gqa_c, gqa_s)


def _heads(x2d, h):
    s = x2d.shape[0]
    return x2d.reshape(s, h, -1).transpose(1, 0, 2)


def _unheads(xh):
    h, s, d = xh.shape
    return xh.transpose(1, 0, 2).reshape(s, h * d)


def _to_res(x2d, dil):
    s, cols = x2d.shape
    if dil > 1:
        x2d = x2d.reshape(s // dil, dil, cols).transpose(1, 0, 2).reshape(s, cols)
    return x2d.reshape(s, cols // DIL_HEAD_DIM, DIL_HEAD_DIM).transpose(1, 0, 2)


def _from_res(xh, dil):
    heads, s, d = xh.shape
    x2d = xh.transpose(1, 0, 2).reshape(s, heads * d)
    if dil > 1:
        x2d = x2d.reshape(dil, s // dil, heads * d).transpose(1, 0, 2).reshape(s, heads * d)
    return x2d


def _group_major(w_b):
    rws = w_b.shape[0]
    return w_b.reshape(rws, 3, DIL_GROUPS, DIL_SLOTS * DIL_HEAD_DIM).transpose(0, 2, 1, 3).reshape(rws, -1)


def _res_to_tok(xh, dil):
    sl, s, c = xh.shape
    if dil == 1:
        return xh
    return xh.reshape(sl, dil, s // dil, c).transpose(0, 2, 1, 3).reshape(sl, s, c)


def _tok_to_res(xh, dil):
    sl, s, c = xh.shape
    if dil == 1:
        return xh
    return xh.reshape(sl, s // dil, dil, c).transpose(0, 2, 1, 3).reshape(sl, s, c)


def _ffn_fwd(x, w, tag):
    hn = _rmsnorm(x, w["norm"], out_dtype=_ACT, name=f"{tag}_norm")
    gate, up, act = _gate_up(hn, w["w_in_t"], name=f"{tag}_gate_up")
    out = _matmul(act, w["w_out"], res=x, name=f"{tag}_out")
    return out, (x, hn, gate, up, act)


def _ffn_bwd(dout, saved, w, tag):
    x, hn, gate, up, act = saved
    w_gate_t, w_up_t = w["w_in_t"][:FFN_HIDDEN], w["w_in_t"][FFN_HIDDEN:]
    d_w_out = _matmul(act, dout, trans_a=True, name=f"{tag}_dwout")
    dgate, dup = _gate_up_bwd(dout, w["w_out"], gate, up, name=f"{tag}_dgate_up")
    d_w_gate_t = _matmul(dgate, hn, trans_a=True, name=f"{tag}_dwgate")
    d_w_up_t = _matmul(dup, hn, trans_a=True, name=f"{tag}_dwup")
    dhn = _matmul(dgate, w_gate_t, name=f"{tag}_dhn_gate")
    dhn = _matmul(dup, w_up_t, res=dhn, name=f"{tag}_dhn_up")
    dx, dnorm = _rmsnorm_bwd(x, w["norm"], dhn, dout, name=f"{tag}_dnorm")
    return dx, {"norm": dnorm, "w_in_t": jnp.concatenate([d_w_gate_t, d_w_up_t], axis=0), "w_out": d_w_out}


def _even_fwd(x, w, tabs, slopes, tag, gather=None, late=None):
    s = x.shape[0]
    (mla_c, mla_s), _ = tabs
    hn = _rmsnorm(x, w["norm"], out_dtype=_ACT, name=f"{tag}_norm")
    za = _matmul(hn, w["w_a"], name=f"{tag}_in_a")
    zb = _matmul(hn, w["w_b"], out_dtype=_ACT, name=f"{tag}_in_b")
    cq, ckv, kr = za[:, :MLA_Q_RANK], za[:, MLA_Q_RANK:MLA_Q_RANK + MLA_KV_RANK], za[:, MLA_Q_RANK + MLA_KV_RANK:]
    cqn = _rmsnorm(cq, w["q_norm"], out_dtype=_ACT, name=f"{tag}_qnorm")
    ckvn = _rmsnorm(ckv, w["kv_norm"], out_dtype=_ACT, name=f"{tag}_kvnorm")
    q = _matmul(cqn, w["w_uq"], scale=(MLA_NOPE + MLA_ROPE) ** -0.5 * LOG2_E,
                name=f"{tag}_uq")
    kv = _matmul(ckvn, w["w_ukv"], out_dtype=_ACT, name=f"{tag}_ukv")
    nn = MLA_HEADS * MLA_NOPE
    q_rope = _rope(q[:, nn:], mla_c, mla_s, out_dtype=_ACT, name=f"{tag}_rope_q")
    k_rope = _rope(kr, mla_c, mla_s, out_dtype=_ACT, name=f"{tag}_rope_k")[:, :MLA_ROPE]
    qh = jnp.concatenate([_heads_t(q[:, :nn].astype(_ACT), MLA_HEADS), _heads_t(q_rope, MLA_HEADS)], axis=1)
    kh = jnp.concatenate([_heads(kv[:, :nn], MLA_HEADS),
                          jnp.broadcast_to(k_rope[None], (MLA_HEADS, s, MLA_ROPE))], axis=-1)
    vh = _heads(kv[:, nn:], MLA_HEADS)
    oa, lse_a, *gathered = _attn_fwd(qh, kh, _chunk_t(kv[:, nn:], MLA_HEADS), out_dtype=_ACT, name=f"{tag}_mla",
                                     gather=gather)
    if late is not None:
        w = late(gathered[0])

    ng = 3 * DIL_SLOTS * DIL_HEAD_DIM
    outs, lses, dil_saved = [], [], []
    for gi, (_, dil) in enumerate(DIL_PAIRS):
        hs = slice(gi * DIL_SLOTS, (gi + 1) * DIL_SLOTS)
        qkv = _to_res(zb[:, gi * ng:(gi + 1) * ng], dil)
        o, l = _win_fwd(qkv, slopes[hs], dil=dil, name=f"{tag}_dil{gi}")
        dil_saved.append((qkv, l))
        outs.append(_res_to_tok(o, dil))
        lses.append(_res_to_tok(l, dil))
    o3, lse3 = tuple(outs), tuple(lses)
    comb = _merge_fwd(o3, lse3, name=f"{tag}_merge")
    cat = jnp.concatenate([_unheads_t(oa), _unheads(comb)], axis=-1)
    out = _matmul(cat, w["w_out"], res=x, name=f"{tag}_out")
    saved = (x, hn, cq, ckv, cqn, ckvn, qh, kh, vh, oa, lse_a, dil_saved, o3, lse3, cat)
    return out, saved, (gathered[0] if gathered else None)


def _even_bwd(dout, saved, w, tabs, slopes, tag, exchange=None):
    x, hn, cq, ckv, cqn, ckvn, qh, kh, vh, oa, lse_a, dil_saved, o3, lse3, cat = saved
    (mla_c, mla_s), _ = tabs
    nn = MLA_HEADS * MLA_NOPE
    dcat = _matmul(dout, w["w_out"], trans_b=True, out_dtype=_ACT, name=f"{tag}_dcat")
    d_w_out = _matmul(cat, dout, trans_a=True, name=f"{tag}_dwout")
    nv = MLA_HEADS * MLA_V

    doa = _heads(dcat[:, :nv], MLA_HEADS)
    dqh, dkt, dvt, *received = _attn_bwd(qh, kh, vh, oa, doa, lse_a, scale=(MLA_NOPE + MLA_ROPE) ** -0.5,
                                         name=f"{tag}_mla_bwd",
                                         exchange=None if exchange is None else exchange(d_w_out))
    dq_rope = _rope(_unheads(dqh[..., MLA_NOPE:]), mla_c, -mla_s, out_dtype=_ACT, name=f"{tag}_drope_q")
    dq = jnp.concatenate([_unheads(dqh[..., :MLA_NOPE]).astype(_ACT), dq_rope], axis=-1)
    dk3 = _unchunk(dkt).reshape(-1, MLA_HEADS, MLA_NOPE + MLA_ROPE)
    dkr = _rope(dk3[..., MLA_NOPE:].reshape(-1, MLA_HEADS * MLA_ROPE), mla_c, -mla_s, out_dtype=_F32,
                sum_chunks=True, name=f"{tag}_drope_k")
    dkv = jnp.concatenate([dk3[..., :MLA_NOPE].reshape(-1, nn), _unchunk(dvt)], axis=-1).astype(_ACT)
    d_w_uq = _matmul(cqn, dq, trans_a=True, name=f"{tag}_dwuq")
    d_w_ukv = _matmul(ckvn, dkv, trans_a=True, name=f"{tag}_dwukv")
    dcqn = _matmul(dq, w["w_uq"], trans_b=True, name=f"{tag}_dcqn")
    dckvn = _matmul(dkv, w["w_ukv"], trans_b=True, name=f"{tag}_dckvn")
    dcq, d_q_norm = _rmsnorm_bwd(cq, w["q_norm"], dcqn, out_dtype=_ACT, name=f"{tag}_dqnorm")
    dckv, d_kv_norm = _rmsnorm_bwd(ckv, w["kv_norm"], dckvn, out_dtype=_ACT, name=f"{tag}_dkvnorm")
    lane = jnp.arange(LANES) < MLA_ROPE
    dza = jnp.concatenate([dcq, dckv, jnp.where(lane[None], dkr, 0.0).astype(_ACT)], axis=-1)

    dcomb = _heads(dcat[:, nv:], DIL_SLOTS)
    do3, delta3 = _merge_bwd(dcomb, o3, lse3, name=f"{tag}_dmerge")
    dslabs = []
    for gi, (_, dil) in enumerate(DIL_PAIRS):
        hs = slice(gi * DIL_SLOTS, (gi + 1) * DIL_SLOTS)
        qkv, l = dil_saved[gi]
        grads = (qkv, _tok_to_res(do3[gi], dil), l, _tok_to_res(delta3[gi], dil), slopes[hs])
        a = _win_bwd_dq(*grads, dil=dil, name=f"{tag}_dil{gi}_dq")
        bc = _win_bwd_dkv(*grads, dil=dil, name=f"{tag}_dil{gi}_dkv")
        dslabs.append(_from_res(jnp.concatenate([a, bc], axis=0), dil))
    dzb = jnp.concatenate(dslabs, axis=-1)

    d_w_a = _matmul(hn, dza, trans_a=True, name=f"{tag}_dwa")
    d_w_b = _matmul(hn, dzb, trans_a=True, name=f"{tag}_dwb")
    dhn = _matmul(dza, w["w_a"], trans_b=True, name=f"{tag}_dhn_a")
    dhn = _matmul(dzb, w["w_b"], trans_b=True, res=dhn, name=f"{tag}_dhn_b")
    dx, dnorm = _rmsnorm_bwd(x, w["norm"], dhn, dout, name=f"{tag}_dnorm")
    grads = {"norm": dnorm, "w_a": d_w_a, "w_b": d_w_b, "q_norm": d_q_norm, "kv_norm": d_kv_norm,
             "w_uq": d_w_uq, "w_ukv": d_w_ukv, "w_out": d_w_out}
    return dx, grads, (received[0] if received else None)


def _odd_fwd(x, w, tabs, tag):
    s = x.shape[0]
    _, (gqa_c, gqa_s) = tabs
    nk = GQA_KV_HEADS * GQA_HEAD_DIM
    hn = _rmsnorm(x, w["norm"], out_dtype=_ACT, name=f"{tag}_norm")
    q = _matmul(hn, w["w_q"], name=f"{tag}_q")
    kv = _matmul(hn, w["w_kv"], name=f"{tag}_kv")
    k = kv[:, :nk]
    qh = _heads_t(_headnorm_rope(q, w["q_norm"], gqa_c, gqa_s, scale=GQA_HEAD_DIM ** -0.5 * LOG2_E,
                                 name=f"{tag}_prep_q"), GQA_HEADS)
    kh = _heads(_headnorm_rope(k, w["k_norm"], gqa_c, gqa_s, name=f"{tag}_prep_k"), GQA_KV_HEADS)
    v = kv[:, nk:].astype(_ACT)
    vh = _heads(v, GQA_KV_HEADS)
    o, lse = _attn_fwd(qh, kh, _chunk_t(v, GQA_KV_HEADS), out_dtype=_ACT, name=f"{tag}_gqa")
    ocat = _unheads_t(o)
    out = _matmul(ocat, w["w_o"], res=x, name=f"{tag}_out")
    return out, (x, hn, q, k, qh, kh, vh, o, lse, ocat)


def _odd_bwd(dout, saved, w, tabs, tag):
    x, hn, q, k, qh, kh, vh, o, lse, ocat = saved
    s = x.shape[0]
    _, (gqa_c, gqa_s) = tabs
    docat = _matmul(dout, w["w_o"], trans_b=True, out_dtype=_ACT, name=f"{tag}_docat")
    d_w_o = _matmul(ocat, dout, trans_a=True, name=f"{tag}_dwo")
    doh = _heads(docat, GQA_HEADS)
    dqh, dkt, dvt = _attn_bwd(qh, kh, vh, o, doh, lse, scale=GQA_HEAD_DIM ** -0.5, name=f"{tag}_gqa_bwd")
    dq, d_q_norm = _headnorm_rope_bwd(q, w["q_norm"], _unheads(dqh), gqa_c, gqa_s, name=f"{tag}_dprep_q")
    dk, d_k_norm = _headnorm_rope_bwd(k, w["k_norm"], _unchunk(dkt), gqa_c, gqa_s, name=f"{tag}_dprep_k")
    dkv = jnp.concatenate([dk, _unchunk(dvt).astype(_ACT)], axis=-1)
    d_w_q = _matmul(hn, dq, trans_a=True, name=f"{tag}_dwq")
    d_w_kv = _matmul(hn, dkv, trans_a=True, name=f"{tag}_dwkv")
    dhn = _matmul(dq, w["w_q"], trans_b=True, name=f"{tag}_dhn_q")
    dhn = _matmul(dkv, w["w_kv"], trans_b=True, res=dhn, name=f"{tag}_dhn_kv")
    dx, dnorm = _rmsnorm_bwd(x, w["norm"], dhn, dout, name=f"{tag}_dnorm")
    return dx, {"norm": dnorm, "w_q": d_w_q, "w_kv": d_w_kv, "q_norm": d_q_norm, "k_norm": d_k_norm, "w_o": d_w_o}


def _split_heads_cols(wm, heads, first):
    rws = wm.shape[0]
    w3 = wm.reshape(rws, heads, -1)
    return jnp.concatenate([w3[:, :, :first].reshape(rws, -1), w3[:, :, first:].reshape(rws, -1)], axis=-1)


def _merge_heads_cols(wm, heads, first):
    rws, cols = wm.shape
    a = wm[:, :heads * first].reshape(rws, heads, first)
    b = wm[:, heads * first:].reshape(rws, heads, cols // heads - first)
    return jnp.concatenate([a, b], axis=-1).reshape(rws, cols)


def _layer_weights(full, gains, layer):
    i = layer // 2

    def stacked(name, idx):
        for group, arrays in zip((_FIRST, _REST), full):
            lo, cnt = group.get(name, (0, 0))
            if lo <= idx < lo + cnt:
                return None if arrays is None else arrays[name][idx - lo]
        raise KeyError((name, idx))

    ffn = {"norm": gains["ffn_norm"][layer], "w_in_t": stacked("ffn_w_in", layer), "w_out": stacked("ffn_w_out", layer)}
    if layer % 2 == 0:
        w_in = stacked("w_in_ab", i)
        mix = {"norm": gains["mix_norm_ab"][i],
               "w_a": jnp.pad(w_in[:, :IN_A], ((0, 0), (0, IN_A_PAD - IN_A))), "w_b": _group_major(w_in[:, IN_A:]),
               "q_norm": gains["mla_q_norm"][i], "kv_norm": gains["mla_kv_norm"][i],
               "w_uq": _split_heads_cols(stacked("mla_w_uq", i), MLA_HEADS, MLA_NOPE),
               "w_ukv": _split_heads_cols(stacked("mla_w_ukv", i), MLA_HEADS, MLA_NOPE),
               "w_out": stacked("w_out_ab", i)}
    else:
        mix = {"norm": gains["mix_norm_c"][i], "w_q": stacked("gqa_w_q", i), "w_kv": stacked("gqa_w_kv", i),
               "q_norm": gains["gqa_q_norm"][i], "k_norm": gains["gqa_k_norm"][i], "w_o": stacked("gqa_w_o", i)}
    return mix, ffn


def _pack_grads(grads, group):
    parts = [_split_for_devices(grads[n][group[n][0]:group[n][0] + group[n][1]], axis)
             for n, axis in _BIG if n in group]
    fill = -sum(p.shape[1] for p in parts) % PACK_ROWS
    return jnp.concatenate(parts + [jnp.zeros((N_DEV, fill, PACK_COLS), parts[0].dtype)], axis=1)


def kernel(x, mix_norm_ab, w_in_ab, mla_q_norm, mla_kv_norm, mla_w_uq, mla_w_ukv, w_out_ab, mix_norm_c, gqa_w_q, gqa_w_kv, gqa_q_norm, gqa_k_norm, gqa_w_o, ffn_norm, ffn_w_in, ffn_w_out, final_norm, loss_target, m_mix_norm_ab, m_w_in_ab, m_mla_q_norm, m_mla_kv_norm, m_mla_w_uq, m_mla_w_ukv, m_w_out_ab, m_mix_norm_c, m_gqa_w_q, m_gqa_w_kv, m_gqa_q_norm, m_gqa_k_norm, m_gqa_w_o, m_ffn_norm, m_ffn_w_in, m_ffn_w_out, m_final_norm, v_mix_norm_ab, v_w_in_ab, v_mla_q_norm, v_mla_kv_norm, v_mla_w_uq, v_mla_w_ukv, v_w_out_ab, v_mix_norm_c, v_gqa_w_q, v_gqa_w_kv, v_gqa_q_norm, v_gqa_k_norm, v_gqa_w_o, v_ffn_norm, v_ffn_w_in, v_ffn_w_out, v_final_norm):
    wts = dict(mix_norm_ab=mix_norm_ab, w_in_ab=w_in_ab, mla_q_norm=mla_q_norm, mla_kv_norm=mla_kv_norm,
               mla_w_uq=mla_w_uq, mla_w_ukv=mla_w_ukv, w_out_ab=w_out_ab, mix_norm_c=mix_norm_c, gqa_w_q=gqa_w_q,
               gqa_w_kv=gqa_w_kv, gqa_q_norm=gqa_q_norm, gqa_k_norm=gqa_k_norm, gqa_w_o=gqa_w_o, ffn_norm=ffn_norm,
               ffn_w_in=ffn_w_in, ffn_w_out=ffn_w_out, final_norm=final_norm)
    mom = dict(mix_norm_ab=m_mix_norm_ab, w_in_ab=m_w_in_ab, mla_q_norm=m_mla_q_norm, mla_kv_norm=m_mla_kv_norm,
               mla_w_uq=m_mla_w_uq, mla_w_ukv=m_mla_w_ukv, w_out_ab=m_w_out_ab, mix_norm_c=m_mix_norm_c,
               gqa_w_q=m_gqa_w_q, gqa_w_kv=m_gqa_w_kv, gqa_q_norm=m_gqa_q_norm, gqa_k_norm=m_gqa_k_norm,
               gqa_w_o=m_gqa_w_o, ffn_norm=m_ffn_norm, ffn_w_in=m_ffn_w_in, ffn_w_out=m_ffn_w_out,
               final_norm=m_final_norm)
    var = dict(mix_norm_ab=v_mix_norm_ab, w_in_ab=v_w_in_ab, mla_q_norm=v_mla_q_norm, mla_kv_norm=v_mla_kv_norm,
               mla_w_uq=v_mla_w_uq, mla_w_ukv=v_mla_w_ukv, w_out_ab=v_w_out_ab, mix_norm_c=v_mix_norm_c,
               gqa_w_q=v_gqa_w_q, gqa_w_kv=v_gqa_w_kv, gqa_q_norm=v_gqa_q_norm, gqa_k_norm=v_gqa_k_norm,
               gqa_w_o=v_gqa_w_o, ffn_norm=v_ffn_norm, ffn_w_in=v_ffn_w_in, ffn_w_out=v_ffn_w_out,
               final_norm=v_final_norm)
    big_names = [n for n, _ in _BIG]
    groups = (_FIRST, _REST)
    big_w, big_m, big_v = (_stored({n: src[n] for n in big_names}) for src in (wts, mom, var))
    w_grp = [_group_of(big_w, grp) for grp in groups]
    shapes = [{n: a.shape for n, a in w.items()} for w in w_grp]
    w_packs = [_pack_big(w) for w in w_grp]
    n_rows = [p.shape[0] for p in w_packs]
    xs = x[0]
    s = xs.shape[0]
    me = 4 * lax.axis_index("x") + 2 * lax.axis_index("y") + lax.axis_index("c")
    c_cols = mix_norm_c.shape[1]

    tail = jnp.pad(mix_norm_c.reshape(-1), (0, GATHER_TAIL_ROWS // 2 * PACK_COLS - mix_norm_c.size))
    tail = _to_bits16(tail.reshape(GATHER_TAIL_ROWS // 2, PACK_COLS))
    gathered = _all_gather(jnp.concatenate([w_packs[0].astype(jnp.bfloat16), tail], axis=0))
    full = [_unpack_gathered(gathered[:, :n_rows[0]], shapes[0]), None]
    c_all = _from_bits16(gathered[:, n_rows[0]:]).reshape(N_DEV, -1)[:, :mix_norm_c.size]
    c_full = c_all.reshape(N_DEV, 2, c_cols).transpose(1, 0, 2).reshape(2, N_DEV * c_cols)
    gains = dict(mix_norm_ab=mix_norm_ab, mla_q_norm=mla_q_norm, mla_kv_norm=mla_kv_norm, mix_norm_c=c_full,
                 gqa_q_norm=gqa_q_norm, gqa_k_norm=gqa_k_norm, ffn_norm=ffn_norm)

    tabs = _rope_tables(s)
    slopes = jnp.exp2(-8.0 * jnp.arange(1, DIL_HEADS + 1, dtype=_F32) / DIL_HEADS)

    h = xs
    saved = []
    for layer in range(DEPTH):
        mix_w, ffn_w = _layer_weights(full, gains, layer)
        if layer == 0:
            def late(rest):
                full[1] = _unpack_gathered(rest, shapes[1])
                return _layer_weights(full, gains, 0)[0]

            h, sv_mix, _ = _even_fwd(h, mix_w, tabs, slopes, f"l{layer}_mix", gather=w_packs[1].astype(jnp.bfloat16),
                                     late=late)
            mix_w, ffn_w = _layer_weights(full, gains, 0)
        elif layer % 2 == 0:
            h, sv_mix, _ = _even_fwd(h, mix_w, tabs, slopes, f"l{layer}_mix")
        else:
            h, sv_mix = _odd_fwd(h, mix_w, tabs, f"l{layer}_mix")
        h, sv_ffn = _ffn_fwd(h, ffn_w, f"l{layer}_ffn")
        saved.append((mix_w, ffn_w, sv_mix, sv_ffn))
    loss_local, dh, d_final = _final_loss(h, final_norm, loss_target[0], name="final_loss")

    gfull = {n: [None] * wts[n].shape[0] for n in big_names}
    gsmall = {n: [None] * (wts[n].shape[0] if wts[n].ndim > 1 else 1) for n in _SMALL}
    gsmall["final_norm"][0] = d_final
    for layer in reversed(range(DEPTH)):
        mix_w, ffn_w, sv_mix, sv_ffn = saved[layer]
        i = layer // 2
        dh, gf = _ffn_bwd(dh, sv_ffn, ffn_w, f"l{layer}_ffn")
        gsmall["ffn_norm"][layer] = gf["norm"]
        gfull["ffn_w_in"][layer] = gf["w_in_t"]
        gfull["ffn_w_out"][layer] = gf["w_out"]
        if layer % 2 == 0:
            def sending(d_w_out):
                gfull["w_out_ab"][0] = d_w_out
                return _pack_grads(gfull, _REST).astype(jnp.bfloat16)

            dh, gm, got = _even_bwd(dh, sv_mix, mix_w, tabs, slopes, f"l{layer}_mix",
                                    exchange=sending if layer == 0 else None)
            if layer == 0:
                received_rest = got
            gsmall["mix_norm_ab"][i] = gm["norm"]
            gsmall["mla_q_norm"][i] = gm["q_norm"]
            gsmall["mla_kv_norm"][i] = gm["kv_norm"]
            gfull["w_in_ab"][i] = jnp.concatenate([gm["w_a"][:, :IN_A], _group_major(gm["w_b"])], axis=-1)
            gfull["mla_w_uq"][i] = _merge_heads_cols(gm["w_uq"], MLA_HEADS, MLA_NOPE)
            gfull["mla_w_ukv"][i] = _merge_heads_cols(gm["w_ukv"], MLA_HEADS, MLA_NOPE)
            gfull["w_out_ab"][i] = gm["w_out"]
        else:
            dh, gm = _odd_bwd(dh, sv_mix, mix_w, tabs, f"l{layer}_mix")
            gsmall["mix_norm_c"][i] = gm["norm"]
            gsmall["gqa_q_norm"][i] = gm["q_norm"]
            gsmall["gqa_k_norm"][i] = gm["k_norm"]
            gfull["gqa_w_q"][i] = gm["w_q"]
            gfull["gqa_w_kv"][i] = gm["w_kv"]
            gfull["gqa_w_o"][i] = gm["w_o"]
    grad_x = dh[None]

    small_part = _pack_small({n: jnp.stack(gsmall[n]) for n in _SMALL})
    small_bits = jnp.broadcast_to(_to_bits16(small_part)[None], (N_DEV, EXCHANGE_TAIL_ROWS, PACK_COLS))
    received_first = _exchange(jnp.concatenate([_pack_grads(gfull, _FIRST).astype(jnp.bfloat16), small_bits], axis=1))
    big_packs = [_reduce_adamw(got, w_packs[gi], _pack_big(_group_of(big_m, grp)), _pack_big(_group_of(big_v, grp)),
                               name=f"adamw_big{gi}")
                 for gi, (grp, got) in enumerate(zip(groups, (received_first, received_rest)))]

    def widen_c(shard):
        return lax.dynamic_update_slice(jnp.zeros((2, N_DEV * c_cols), _F32), shard, (0, me * c_cols))

    def small_of(src):
        return _pack_small({n: (widen_c(src[n]) if n == "mix_norm_c" else src[n]) for n in _SMALL})

    small_recv = _from_bits16(received_first[:, n_rows[0]:])
    small_packs = _reduce_adamw(small_recv, small_of(wts), small_of(mom), small_of(var), name="adamw_small")

    def outputs_of(which):
        by_group = [_unpack_big(big_packs[gi][which], shapes[gi]) for gi in range(len(groups))]
        small = _unpack_small(small_packs[which])
        res = _stored({n: jnp.concatenate([grp[n] for grp in by_group if n in grp], axis=0) for n in big_names})
        for n in wts:
            if n in big_names:
                continue
            if n == "mix_norm_c":
                res[n] = lax.dynamic_slice(small[n].reshape(2, N_DEV * c_cols), (0, me * c_cols), (2, c_cols))
            else:
                res[n] = small[n].reshape(wts[n].shape)
        return [res[n] for n in wts]

    loss = lax.psum(loss_local, _AXES)
    return (loss, grad_x, *outputs_of(0), *outputs_of(1), *outputs_of(2), *outputs_of(3))
```

```python
import functools

import jax
import jax.numpy as jnp
from jax import lax
from jax.experimental import pallas as pl
from jax.experimental.pallas import tpu as pltpu

D_MODEL = 1024
DEPTH = 4
GRID_W = 64
NORM_EPS = 1e-6
ROPE_THETA = 10000.0
NEG_INF = -1e30
MLA_HEADS = 8
MLA_Q_RANK = 384
MLA_KV_RANK = 256
MLA_NOPE = 64
MLA_ROPE = 32
MLA_V = 64
DIL_PAIRS = ((128, 1), (512, 4), (2048, 16))
DIL_HALF = 64
DIL_SLOTS = 4
DIL_GROUPS = 3
DIL_HEADS = 12
DIL_HEAD_DIM = 64
GQA_HEADS = 16
GQA_KV_HEADS = 4
GQA_HEAD_DIM = 64
FFN_HIDDEN = 2816
IN_A = MLA_Q_RANK + MLA_KV_RANK + MLA_ROPE
IN_A_PAD = 768
IN_B = 3 * DIL_HEADS * DIL_HEAD_DIM
ADAM_LR = 0.001
ADAM_B1 = 0.9
ADAM_B2 = 0.999
ADAM_EPS = 1e-08
ADAM_WD = 0.01
ADAM_STEP = 10

LANES = 128
SUBLANES_16BIT = 16
VMEM_LIMIT_BYTES = 56 * 1024 * 1024

MM_TILE = 1408
ROW_BLOCK = 512
ROT_CHUNK = 32

N_DEV = 8
PACK_COLS = 1024

_MXU = jnp.bfloat16
_ACT = jnp.bfloat16
_F32 = jnp.float32

_AXES = ("x", "y", "c")


def _params(sem):
    return pltpu.CompilerParams(dimension_semantics=sem, vmem_limit_bytes=VMEM_LIMIT_BYTES)


def _pick(n, cap):
    for t in range(cap - cap % LANES, 0, -LANES):
        if n % t == 0:
            return t
    return n


def _rows(m, target):
    t = m
    while t > target and t % 2 == 0:
        t //= 2
    return t


def _matmul(a, b, *, trans_a=False, trans_b=False, res=None, scale=None, out_dtype=_F32, name):
    if trans_a:
        k, m = a.shape
    else:
        m, k = a.shape
    if trans_b:
        n, kb = b.shape
    else:
        kb, n = b.shape
    assert k == kb, (a.shape, b.shape)
    tm, tn, tk = _pick(m, MM_TILE), _pick(n, MM_TILE), _pick(k, MM_TILE)
    nk = k // tk
    dims = (((0 if trans_a else 1,), (1 if trans_b else 0,)), ((), ()))

    def body(*refs):
        if res is None:
            a_ref, b_ref, o_ref, acc = refs
            r_ref = None
        else:
            a_ref, b_ref, r_ref, o_ref, acc = refs
        kk = pl.program_id(2)

        @pl.when(kk == 0)
        def _():
            acc[...] = jnp.zeros_like(acc)

        acc[...] += lax.dot_general(a_ref[...].astype(_MXU), b_ref[...].astype(_MXU), dims,
                                    preferred_element_type=_F32)

        @pl.when(kk == nk - 1)
        def _():
            r = acc[...]
            if scale is not None:
                r = r * scale
            if r_ref is not None:
                r = r + r_ref[...].astype(_F32)
            o_ref[...] = r.astype(out_dtype)

    a_spec = (pl.BlockSpec((tk, tm), lambda i, j, kk: (kk, i)) if trans_a
              else pl.BlockSpec((tm, tk), lambda i, j, kk: (i, kk)))
    b_spec = (pl.BlockSpec((tn, tk), lambda i, j, kk: (j, kk)) if trans_b
              else pl.BlockSpec((tk, tn), lambda i, j, kk: (kk, j)))
    o_spec = pl.BlockSpec((tm, tn), lambda i, j, kk: (i, j))
    in_specs = [a_spec, b_spec] + ([o_spec] if res is not None else [])
    args = (a, b) + ((res,) if res is not None else ())
    return pl.pallas_call(
        body, name=name, grid=(m // tm, n // tn, nk),
        in_specs=in_specs, out_specs=o_spec,
        out_shape=jax.ShapeDtypeStruct((m, n), out_dtype),
        scratch_shapes=[pltpu.VMEM((tm, tn), _F32)],
        compiler_params=_params(("parallel", "parallel", "arbitrary")),
    )(*args)


def _rmsnorm(x, g, *, out_dtype, name, rows=ROW_BLOCK):
    m, d = x.shape
    tm = _rows(m, rows)

    def body(x_ref, g_ref, o_ref):
        xf = x_ref[...].astype(_F32)
        r = lax.rsqrt(jnp.mean(xf * xf, axis=-1, keepdims=True) + NORM_EPS)
        o_ref[...] = ((xf * r) * g_ref[...]).astype(out_dtype)

    return pl.pallas_call(
        body, name=name, grid=(m // tm,),
        in_specs=[pl.BlockSpec((tm, d), lambda i: (i, 0)), pl.BlockSpec((1, d), lambda i: (0, 0))],
        out_specs=pl.BlockSpec((tm, d), lambda i: (i, 0)),
        out_shape=jax.ShapeDtypeStruct((m, d), out_dtype),
        compiler_params=_params(("parallel",)),
    )(x, g.reshape(1, d).astype(_F32))


def _rmsnorm_bwd(x, g, dy, dres=None, *, out_dtype=_F32, name, rows=ROW_BLOCK):
    m, d = x.shape
    tm = _rows(m, rows)

    def body(*refs):
        if dres is None:
            x_ref, g_ref, dy_ref, dx_ref, dg_ref = refs
            r_ref = None
        else:
            x_ref, g_ref, dy_ref, r_ref, dx_ref, dg_ref = refs

        @pl.when(pl.program_id(0) == 0)
        def _():
            dg_ref[...] = jnp.zeros_like(dg_ref)

        xf = x_ref[...].astype(_F32)
        r = lax.rsqrt(jnp.mean(xf * xf, axis=-1, keepdims=True) + NORM_EPS)
        xh = xf * r
        dyf = dy_ref[...].astype(_F32)
        dg_ref[...] += jnp.sum(dyf * xh, axis=0, keepdims=True)
        gdy = dyf * g_ref[...]
        dx = r * (gdy - xh * jnp.mean(gdy * xh, axis=-1, keepdims=True))
        if r_ref is not None:
            dx = dx + r_ref[...].astype(_F32)
        dx_ref[...] = dx.astype(out_dtype)

    row = pl.BlockSpec((tm, d), lambda i: (i, 0))
    vec = pl.BlockSpec((1, d), lambda i: (0, 0))
    in_specs = [row, vec, row] + ([row] if dres is not None else [])
    args = (x, g.reshape(1, d).astype(_F32), dy) + ((dres,) if dres is not None else ())
    dx, dg = pl.pallas_call(
        body, name=name, grid=(m // tm,),
        in_specs=in_specs, out_specs=[row, vec],
        out_shape=[jax.ShapeDtypeStruct((m, d), out_dtype), jax.ShapeDtypeStruct((1, d), _F32)],
        compiler_params=_params(("arbitrary",)),
    )(*args)
    return dx, dg.reshape(d)


def _rotate(xf, c, sn):
    w = xf.shape[1]
    half = ROT_CHUNK // 2
    if w > LANES:
        c, sn = jnp.tile(c, (1, w // LANES)), jnp.tile(sn, (1, w // LANES))
    lane = lax.broadcasted_iota(jnp.int32, xf.shape, 1)
    sw = jnp.where((lane & (ROT_CHUNK - 1)) < half, pltpu.roll(xf, w - half, 1), pltpu.roll(xf, half, 1))
    return xf * c + sw * sn


def _seg_mean(v, seg_ref):
    outs = []
    for c in range(v.shape[1] // LANES):
        piece = v[:, c * LANES:(c + 1) * LANES]
        hi = piece.astype(jnp.bfloat16)
        lo = (piece - hi.astype(_F32)).astype(jnp.bfloat16)
        outs.append(jnp.dot(hi, seg_ref[...], preferred_element_type=_F32)
                    + jnp.dot(lo, seg_ref[...], preferred_element_type=_F32))
    return jnp.concatenate(outs, axis=1) if len(outs) > 1 else outs[0]


def _seg_matrix():
    lane = jnp.arange(LANES) // GQA_HEAD_DIM
    return ((lane[:, None] == lane[None, :]).astype(_F32) / GQA_HEAD_DIM).astype(jnp.bfloat16)


def _headnorm_rope(x, gain, cos_t, sin_t, *, scale=None, name):
    s, w = x.shape
    ts = _rows(s, ROW_BLOCK)

    def body(x_ref, g_ref, seg_ref, c_ref, s_ref, o_ref):
        xf = x_ref[...]
        r = lax.rsqrt(_seg_mean(xf * xf, seg_ref) + NORM_EPS)
        y = _rotate((xf * r) * g_ref[...], c_ref[...], s_ref[...])
        if scale is not None:
            y = y * scale
        o_ref[...] = y.astype(o_ref.dtype)

    row = pl.BlockSpec((ts, w), lambda i: (i, 0))
    tab = pl.BlockSpec((ts, LANES), lambda i: (i, 0))
    return pl.pallas_call(
        body, name=name, grid=(s // ts,),
        in_specs=[row, pl.BlockSpec((1, w), lambda i: (0, 0)), pl.BlockSpec((LANES, LANES), lambda i: (0, 0)), tab, tab],
        out_specs=row, out_shape=jax.ShapeDtypeStruct((s, w), _ACT),
        compiler_params=_params(("parallel",)),
    )(x, jnp.tile(gain.astype(_F32), w // GQA_HEAD_DIM).reshape(1, w), _seg_matrix(), cos_t, sin_t)


def _headnorm_rope_bwd(x, gain, dy, cos_t, sin_t, *, name):
    s, w = x.shape
    ts = _rows(s, ROW_BLOCK)

    def body(x_ref, g_ref, seg_ref, c_ref, s_ref, dy_ref, dx_ref, dg_ref):
        @pl.when(pl.program_id(0) == 0)
        def _():
            dg_ref[...] = jnp.zeros_like(dg_ref)

        xf = x_ref[...]
        r = lax.rsqrt(_seg_mean(xf * xf, seg_ref) + NORM_EPS)
        xh = xf * r
        dyn = _rotate(dy_ref[...].astype(_F32), c_ref[...], -s_ref[...])
        dg_ref[...] += jnp.sum(dyn * xh, axis=0, keepdims=True)
        gdy = dyn * g_ref[...]
        dx_ref[...] = (r * (gdy - xh * _seg_mean(gdy * xh, seg_ref))).astype(dx_ref.dtype)

    row = pl.BlockSpec((ts, w), lambda i: (i, 0))
    vec = pl.BlockSpec((1, w), lambda i: (0, 0))
    tab = pl.BlockSpec((ts, LANES), lambda i: (i, 0))
    dx, dg = pl.pallas_call(
        body, name=name, grid=(s // ts,),
        in_specs=[row, vec, pl.BlockSpec((LANES, LANES), lambda i: (0, 0)), tab, tab, row],
        out_specs=[row, vec],
        out_shape=[jax.ShapeDtypeStruct((s, w), _ACT), jax.ShapeDtypeStruct((1, w), _F32)],
        compiler_params=_params(("arbitrary",)),
    )(x, jnp.tile(gain.astype(_F32), w // GQA_HEAD_DIM).reshape(1, w), _seg_matrix(), cos_t, sin_t, dy)
    return dx, dg.reshape(w // GQA_HEAD_DIM, GQA_HEAD_DIM).sum(axis=0)


def _rope(x, cos_t, sin_t, *, out_dtype, name, sum_chunks=False, scale=None):
    s, w = x.shape
    assert w % LANES == 0
    ts = _rows(s, ROW_BLOCK)
    ow = LANES if sum_chunks else w

    def body(x_ref, c_ref, s_ref, o_ref):
        y = _rotate(x_ref[...].astype(_F32), c_ref[...], s_ref[...])
        if scale is not None:
            y = y * scale
        if sum_chunks:
            shift = w // 2
            while shift >= ROT_CHUNK:
                y = y + pltpu.roll(y, shift, 1)
                shift //= 2
            y = y[:, :LANES]
        o_ref[...] = y.astype(out_dtype)

    return pl.pallas_call(
        body, name=name, grid=(s // ts,),
        in_specs=[pl.BlockSpec((ts, w), lambda i: (i, 0)), pl.BlockSpec((ts, LANES), lambda i: (i, 0)),
                  pl.BlockSpec((ts, LANES), lambda i: (i, 0))],
        out_specs=pl.BlockSpec((ts, ow), lambda i: (i, 0)),
        out_shape=jax.ShapeDtypeStruct((s, ow), out_dtype),
        compiler_params=_params(("parallel",)),
    )(x, cos_t, sin_t)


_NT = (((1,), (1,)), ((), ()))
_TN = (((0,), (0,)), ((), ()))
LOG2_E = 1.4426950408889634
LN_2 = 0.6931471805599453
ATTN_FWD_ROWS = 2048
ATTN_ROWS = 512
ATTN_CHAINS = 2
ATTN_FWD_KEYS = 1024
ATTN_BWD_KEYS = 4096


def _grid_step(dims):
    step, total = 0, 1
    for axis, n in enumerate(dims):
        step = step * n + pl.program_id(axis)
        total *= n
    return step, total


def _attn_fwd(qt, k, vt, *, out_dtype, name, gather=None):
    h, dk, s = qt.shape
    g, nk, dv, tk = vt.shape
    assert nk * tk == s
    r = h // g
    rc = min(ATTN_FWD_ROWS, s)
    hp = min(r, ATTN_CHAINS)
    nrc = max(1, min(ATTN_CHAINS // hp, s // rc))
    tq = rc * nrc
    nhp, nq = r // hp, s // tq
    units = [(a, c) for a in range(hp) for c in range(nrc)]

    def body(q_ref, k_ref, v_ref, *rest):
        if gather is None:
            o_ref, lse_ref = rest
        else:
            x_ref, o_ref, lse_ref, all_ref, send_sems, recv_sems, local_sem = rest
            start, forward, finish = _gather_phases(x_ref, all_ref, send_sems, recv_sems, local_sem)
            step, total = _grid_step((g, nhp, nq))
            pl.when(step == 0)(start)
            pl.when(step == 3 * total // 4)(forward)

        qs = [q_ref[a, :, c * rc:(c + 1) * rc] for a, c in units]
        init = tuple((jnp.full((1, rc), NEG_INF, _F32), jnp.zeros((1, rc), _F32), jnp.zeros((dv, rc), _F32))
                     for _ in units)

        def trip(j, carry):
            kb, vb = k_ref[0, pl.ds(pl.multiple_of(j * tk, tk), tk), :], v_ref[0, j]
            out = []
            for u in range(len(units)):
                m, l, acc = carry[u]
                sc = jnp.dot(kb, qs[u], preferred_element_type=_F32)
                m_new = jnp.maximum(m, jnp.max(sc, axis=0, keepdims=True))
                p = jnp.exp2(sc - m_new)
                alpha = jnp.exp2(m - m_new)
                l = alpha * l + jnp.sum(p, axis=0, keepdims=True)
                acc = alpha * acc + jnp.dot(vb, p.astype(_MXU), preferred_element_type=_F32)
                out.append((m_new, l, acc))
            return tuple(out)

        fin = lax.fori_loop(0, nk, trip, init)
        for u, (a, c) in enumerate(units):
            m, l, acc = fin[u]
            o_ref[a, :, c * rc:(c + 1) * rc] = (acc / l).astype(out_dtype)
            lse_ref[a, :, c * rc:(c + 1) * rc] = m + jnp.log2(l)

        if gather is not None:
            pl.when(step == total - 1)(finish)

    q_blk = lambda gg, hh, i: (gg * nhp + hh, 0, i)
    in_specs = [pl.BlockSpec((hp, dk, tq), q_blk), pl.BlockSpec((1, s, dk), lambda gg, hh, i: (gg, 0, 0)),
                pl.BlockSpec((1, nk, dv, tk), lambda gg, hh, i: (gg, 0, 0, 0))]
    out_specs = [pl.BlockSpec((hp, dv, tq), q_blk), pl.BlockSpec((hp, 1, tq), q_blk)]
    out_shape = [jax.ShapeDtypeStruct((h, dv, s), out_dtype), jax.ShapeDtypeStruct((h, 1, s), _F32)]
    if gather is None:
        return pl.pallas_call(
            body, name=name, grid=(g, nhp, nq), in_specs=in_specs, out_specs=out_specs, out_shape=out_shape,
            compiler_params=_params(("parallel", "parallel", "parallel")),
        )(qt, k, vt)
    return pl.pallas_call(
        body, name=name, grid=(g, nhp, nq),
        in_specs=in_specs + [pl.BlockSpec(memory_space=pl.ANY)],
        out_specs=out_specs + [pl.BlockSpec(memory_space=pl.ANY)],
        out_shape=out_shape + [jax.ShapeDtypeStruct((N_DEV,) + gather.shape, gather.dtype)],
        scratch_shapes=_comm_scratch(),
        compiler_params=_params(("arbitrary", "arbitrary", "arbitrary")),
    )(qt, k, vt, gather)


def _attn_bwd(qt, k, v, ot, do, lse2, *, scale, name, exchange=None):
    h, dk, s = qt.shape
    g, _, dv = v.shape
    r = h // g
    hp = min(r, ATTN_CHAINS)
    tq = min(ATTN_ROWS // hp, s)
    tk = min(ATTN_BWD_KEYS, s)
    nhp, nq, nk = r // hp, s // tq, s // tk

    def body(qt_ref, k_ref, v_ref, ot_ref, do_ref, lse_ref, *rest):
        if exchange is None:
            dq_ref, dk_ref, dv_ref = rest
        else:
            g_ref, dq_ref, dk_ref, dv_ref, got_ref, send_sems, recv_sems, local_sem = rest
            start, finish = _exchange_phases(g_ref, got_ref, send_sems, recv_sems, local_sem)
            step, total = _grid_step((g, nhp, nq))
            pl.when(step == 0)(start)
        hh, i = pl.program_id(1), pl.program_id(2)

        @pl.when((hh == 0) & (i == 0))
        def _():
            dk_ref[...] = jnp.zeros_like(dk_ref)
            dv_ref[...] = jnp.zeros_like(dv_ref)

        qts = [qt_ref[a] for a in range(hp)]
        qs = [qt_ref[a].T for a in range(hp)]
        dos = [do_ref[a] for a in range(hp)]
        dots = [do_ref[a].T for a in range(hp)]
        ls = [lse_ref[a].T for a in range(hp)]
        dls = [jnp.sum(do_ref[a].astype(_F32) * ot_ref[a].T.astype(_F32), axis=1, keepdims=True)
               for a in range(hp)]

        def trip(j, carry):
            rows = pl.ds(pl.multiple_of(j * tk, tk), tk)
            kb, vb = k_ref[0, rows, :], v_ref[0, rows, :]
            out = []
            for a in range(hp):
                p = jnp.exp2(lax.dot_general(qs[a], kb, _NT, preferred_element_type=_F32) - ls[a])
                dp = lax.dot_general(dos[a], vb, _NT, preferred_element_type=_F32)
                ds = (p * (dp - dls[a])).astype(_MXU)
                dv_ref[0, j] += jnp.dot(dots[a], p.astype(_MXU), preferred_element_type=_F32)
                dk_ref[0, j] += jnp.dot(qts[a], ds, preferred_element_type=_F32)
                out.append(carry[a] + jnp.dot(ds, kb, preferred_element_type=_F32))
            return tuple(out)

        fin = lax.fori_loop(0, nk, trip, tuple(jnp.zeros((tq, dk), _F32) for _ in range(hp)))
        for a in range(hp):
            dq_ref[a] = (fin[a] * scale).astype(dq_ref.dtype)

        @pl.when((hh == nhp - 1) & (i == nq - 1))
        def _():
            dk_ref[...] = dk_ref[...] * LN_2

        if exchange is not None:
            pl.when(step == total - 1)(finish)

    q_blk = lambda gg, hh, i: (gg * nhp + hh, i, 0)
    qt_blk = lambda gg, hh, i: (gg * nhp + hh, 0, i)
    kv_blk = lambda gg, hh, i: (gg, 0, 0)
    in_specs = [pl.BlockSpec((hp, dk, tq), qt_blk), pl.BlockSpec((1, s, dk), kv_blk), pl.BlockSpec((1, s, dv), kv_blk),
                pl.BlockSpec((hp, dv, tq), qt_blk), pl.BlockSpec((hp, tq, dv), q_blk), pl.BlockSpec((hp, 1, tq), qt_blk)]
    out_specs = [pl.BlockSpec((hp, tq, dk), q_blk), pl.BlockSpec((1, nk, dk, tk), lambda gg, hh, i: (gg, 0, 0, 0)),
                 pl.BlockSpec((1, nk, dv, tk), lambda gg, hh, i: (gg, 0, 0, 0))]
    out_shape = [jax.ShapeDtypeStruct((h, s, dk), _ACT), jax.ShapeDtypeStruct((g, nk, dk, tk), _F32),
                 jax.ShapeDtypeStruct((g, nk, dv, tk), _F32)]
    args = (qt, k, v, ot, do, lse2)
    scratch = []
    if exchange is not None:
        in_specs, args = in_specs + [pl.BlockSpec(memory_space=pl.ANY)], args + (exchange,)
        out_specs = out_specs + [pl.BlockSpec(memory_space=pl.ANY)]
        out_shape = out_shape + [jax.ShapeDtypeStruct(exchange.shape, exchange.dtype)]
        scratch = _comm_scratch()
    return pl.pallas_call(
        body, name=name, grid=(g, nhp, nq), in_specs=in_specs, out_specs=out_specs, out_shape=out_shape,
        scratch_shapes=scratch, compiler_params=_params(("arbitrary", "arbitrary", "arbitrary")),
    )(*args)


def _unchunk(xt):
    g, nk, d, tk = xt.shape
    return xt.transpose(1, 3, 0, 2).reshape(nk * tk, g * d)


def _chunk_t(x2d, g):
    s = x2d.shape[0]
    tk = min(ATTN_FWD_KEYS, s)
    return x2d.reshape(s // tk, tk, g, -1).transpose(2, 0, 3, 1)


def _heads_t(x2d, h):
    s = x2d.shape[0]
    return x2d.reshape(s, h, -1).transpose(1, 2, 0)


def _unheads_t(xt):
    h, d, s = xt.shape
    return xt.transpose(2, 0, 1).reshape(s, h * d)


WIN_ROWS = 512


def _win_geometry(s, dil):
    t = min(WIN_ROWS, s)
    length = s // dil
    lg = length.bit_length() - 1
    assert 1 << lg == length and t % DIL_HALF == 0 and s % t == 0
    return t, t + 2 * DIL_HALF, lg, t // DIL_HALF, s // DIL_HALF


def _win_specs(hn, t, d, halo_per_blk, n_halo, part=0):
    return [pl.BlockSpec((hn, DIL_HALF, d), lambda i: (part, jnp.maximum(i * halo_per_blk - 1, 0), 0)),
            pl.BlockSpec((hn, t, d), lambda i: (part, i, 0)),
            pl.BlockSpec((hn, DIL_HALF, d), lambda i: (part, jnp.minimum((i + 1) * halo_per_blk, n_halo - 1), 0))]


def _win_mask(i, t, w, lg, dil, wide_rows):
    shape = (w, t) if wide_rows else (t, w)
    rows = lax.broadcasted_iota(jnp.int32, shape, 0)
    cols = lax.broadcasted_iota(jnp.int32, shape, 1)
    base = i * t
    if wide_rows:
        pq, pk = base - DIL_HALF + rows, base + cols
    else:
        pq, pk = base + rows, base - DIL_HALF + cols
    arel = jnp.abs(pk - pq)
    valid = (arel <= DIL_HALF) & ((pk >> lg) == (pq >> lg))
    return valid, (-dil * arel).astype(_F32)


def _win3(lo_ref, mid_ref, hi_ref, a):
    return jnp.concatenate([lo_ref[a], mid_ref[a], hi_ref[a]], axis=0)


def _win_fwd(qkv, slopes, *, dil, name):
    hn, s, d = qkv.shape[0] // 3, qkv.shape[1], qkv.shape[2]
    t, w, lg, hpb, n_halo = _win_geometry(s, dil)
    scale = d ** -0.5

    def body(sl_ref, q_ref, klo, kmid, khi, vlo, vmid, vhi, o_ref, lse_ref):
        valid, nb = _win_mask(pl.program_id(0), t, w, lg, dil, False)
        for a in range(hn):
            kw, vw = _win3(klo, kmid, khi, a), _win3(vlo, vmid, vhi, a)
            sc = lax.dot_general(q_ref[a], kw, _NT, preferred_element_type=_F32) * scale
            sc = jnp.where(valid, sc + sl_ref[a] * nb, NEG_INF)
            m = jnp.max(sc, axis=1, keepdims=True)
            e = jnp.exp(sc - m)
            den = jnp.sum(e, axis=1, keepdims=True)
            o_ref[a] = jnp.dot(e.astype(_MXU), vw, preferred_element_type=_F32) / den
            lse_ref[a] = m + jnp.log(den)

    blk = lambda c: pl.BlockSpec((hn, t, c), lambda i: (0, i, 0))
    return pl.pallas_call(
        body, name=name, grid=(s // t,),
        in_specs=([pl.BlockSpec(memory_space=pltpu.SMEM), blk(d)] + _win_specs(hn, t, d, hpb, n_halo, 1)
                  + _win_specs(hn, t, d, hpb, n_halo, 2)),
        out_specs=[blk(d), blk(1)],
        out_shape=[jax.ShapeDtypeStruct((hn, s, d), _F32), jax.ShapeDtypeStruct((hn, s, 1), _F32)],
        compiler_params=_params(("parallel",)),
    )(slopes.astype(_F32), qkv, qkv, qkv, qkv, qkv, qkv, qkv)


def _win_bwd_dq(qkv, do, lse, delta, slopes, *, dil, name):
    hn, s, d = qkv.shape[0] // 3, qkv.shape[1], qkv.shape[2]
    t, w, lg, hpb, n_halo = _win_geometry(s, dil)
    scale = d ** -0.5

    def body(sl_ref, q_ref, klo, kmid, khi, vlo, vmid, vhi, do_ref, lse_ref, dl_ref, dq_ref):
        valid, nb = _win_mask(pl.program_id(0), t, w, lg, dil, False)
        for a in range(hn):
            kw, vw = _win3(klo, kmid, khi, a), _win3(vlo, vmid, vhi, a)
            sc = lax.dot_general(q_ref[a], kw, _NT, preferred_element_type=_F32) * scale
            p = jnp.exp(jnp.where(valid, sc + sl_ref[a] * nb, NEG_INF) - lse_ref[a])
            dp = lax.dot_general(do_ref[a], vw, _NT, preferred_element_type=_F32)
            ds = (p * (dp - dl_ref[a])).astype(_MXU)
            dq_ref[a] = (jnp.dot(ds, kw, preferred_element_type=_F32) * scale).astype(dq_ref.dtype)

    blk = lambda c: pl.BlockSpec((hn, t, c), lambda i: (0, i, 0))
    return pl.pallas_call(
        body, name=name, grid=(s // t,),
        in_specs=([pl.BlockSpec(memory_space=pltpu.SMEM), blk(d)] + _win_specs(hn, t, d, hpb, n_halo, 1)
                  + _win_specs(hn, t, d, hpb, n_halo, 2) + [blk(d), blk(1), blk(1)]),
        out_specs=blk(d),
        out_shape=jax.ShapeDtypeStruct((hn, s, d), _ACT),
        compiler_params=_params(("parallel",)),
    )(slopes.astype(_F32), qkv, qkv, qkv, qkv, qkv, qkv, qkv, do, lse, delta)


def _win_bwd_dkv(qkv, do, lse, delta, slopes, *, dil, name):
    hn, s, d = qkv.shape[0] // 3, qkv.shape[1], qkv.shape[2]
    t, w, lg, hpb, n_halo = _win_geometry(s, dil)
    scale = d ** -0.5

    def body(sl_ref, qlo, qmid, qhi, dolo, domid, dohi, llo, lmid, lhi, dllo, dlmid, dlhi, k_ref, v_ref, dkv_ref):
        valid, nb = _win_mask(pl.program_id(0), t, w, lg, dil, True)
        for a in range(hn):
            qw, dow = _win3(qlo, qmid, qhi, a), _win3(dolo, domid, dohi, a)
            lw, dlw = _win3(llo, lmid, lhi, a), _win3(dllo, dlmid, dlhi, a)
            sc = lax.dot_general(qw, k_ref[a], _NT, preferred_element_type=_F32) * scale
            p = jnp.exp(jnp.where(valid, sc + sl_ref[a] * nb, NEG_INF) - lw)
            dp = lax.dot_general(dow, v_ref[a], _NT, preferred_element_type=_F32)
            ds = (p * (dp - dlw)).astype(_MXU)
            dkv_ref[hn + a] = lax.dot_general(p.astype(_MXU), dow, _TN,
                                              preferred_element_type=_F32).astype(dkv_ref.dtype)
            dkv_ref[a] = (lax.dot_general(ds, qw, _TN, preferred_element_type=_F32) * scale).astype(dkv_ref.dtype)

    part = lambda which: pl.BlockSpec((hn, t, d), lambda i: (which, i, 0))
    return pl.pallas_call(
        body, name=name, grid=(s // t,),
        in_specs=([pl.BlockSpec(memory_space=pltpu.SMEM)] + _win_specs(hn, t, d, hpb, n_halo)
                  + _win_specs(hn, t, d, hpb, n_halo) + _win_specs(hn, t, 1, hpb, n_halo)
                  + _win_specs(hn, t, 1, hpb, n_halo) + [part(1), part(2)]),
        out_specs=pl.BlockSpec((2 * hn, t, d), lambda i: (0, i, 0)),
        out_shape=jax.ShapeDtypeStruct((2 * hn, s, d), _ACT),
        compiler_params=_params(("parallel",)),
    )(slopes.astype(_F32), qkv, qkv, qkv, do, do, do, lse, lse, lse, delta, delta, delta, qkv, qkv)


def _merge_weights(lses):
    mx = functools.reduce(jnp.maximum, lses)
    es = [jnp.exp(l - mx) for l in lses]
    den = functools.reduce(lambda a, b: a + b, es)
    return [e / den for e in es]


def _merge_fwd(outs, lses, *, name):
    ng = len(outs)
    sl, s, d = outs[0].shape
    t = _rows(s, ROW_BLOCK)

    def body(*refs):
        o_refs, l_refs, c_ref = refs[:ng], refs[ng:2 * ng], refs[2 * ng]
        wts = _merge_weights([r[0] for r in l_refs])
        comb = functools.reduce(lambda a, b: a + b, [w * r[0] for w, r in zip(wts, o_refs)])
        c_ref[0] = comb.astype(c_ref.dtype)

    big = pl.BlockSpec((1, t, d), lambda a, i: (a, i, 0))
    small = pl.BlockSpec((1, t, 1), lambda a, i: (a, i, 0))
    return pl.pallas_call(
        body, name=name, grid=(sl, s // t),
        in_specs=[big] * ng + [small] * ng, out_specs=big,
        out_shape=jax.ShapeDtypeStruct((sl, s, d), _ACT),
        compiler_params=_params(("parallel", "parallel")),
    )(*outs, *lses)


def _merge_bwd(dcomb, outs, lses, *, name):
    ng = len(outs)
    sl, s, d = outs[0].shape
    t = _rows(s, ROW_BLOCK)

    def body(*refs):
        dc_ref, o_refs, l_refs = refs[0], refs[1:1 + ng], refs[1 + ng:1 + 2 * ng]
        do_refs, dl_refs = refs[1 + 2 * ng:1 + 3 * ng], refs[1 + 3 * ng:]
        wts = _merge_weights([r[0] for r in l_refs])
        dc = dc_ref[0].astype(_F32)
        comb = functools.reduce(lambda a, b: a + b, [w * r[0] for w, r in zip(wts, o_refs)])
        dot = jnp.sum(dc * comb, axis=-1, keepdims=True)
        for w, do_ref, dl_ref in zip(wts, do_refs, dl_refs):
            do_ref[0] = (w * dc).astype(do_ref.dtype)
            dl_ref[0] = w * dot

    big = pl.BlockSpec((1, t, d), lambda a, i: (a, i, 0))
    small = pl.BlockSpec((1, t, 1), lambda a, i: (a, i, 0))
    res = pl.pallas_call(
        body, name=name, grid=(sl, s // t),
        in_specs=[big] + [big] * ng + [small] * ng, out_specs=[big] * ng + [small] * ng,
        out_shape=[jax.ShapeDtypeStruct((sl, s, d), _ACT)] * ng + [jax.ShapeDtypeStruct((sl, s, 1), _F32)] * ng,
        compiler_params=_params(("parallel", "parallel")),
    )(dcomb, *outs, *lses)
    return res[:ng], res[ng:]


SWIGLU_ROWS = 512


def _gate_up(a, w_in_t, *, name):
    m, k = a.shape
    n = w_in_t.shape[0] // 2
    tm, tn, tk = _pick(m, SWIGLU_ROWS), _pick(n, MM_TILE), _pick(k, MM_TILE)
    nk = k // tk

    def body(a_ref, g_ref, u_ref, go_ref, uo_ref, act_ref, acc_g, acc_u):
        kk = pl.program_id(2)

        @pl.when(kk == 0)
        def _():
            acc_g[...] = jnp.zeros_like(acc_g)
            acc_u[...] = jnp.zeros_like(acc_u)

        av = a_ref[...].astype(_MXU)
        acc_g[...] += lax.dot_general(av, g_ref[...].astype(_MXU), _NT, preferred_element_type=_F32)
        acc_u[...] += lax.dot_general(av, u_ref[...].astype(_MXU), _NT, preferred_element_type=_F32)

        @pl.when(kk == nk - 1)
        def _():
            gf, uf = acc_g[...], acc_u[...]
            go_ref[...] = gf.astype(go_ref.dtype)
            uo_ref[...] = uf.astype(uo_ref.dtype)
            act_ref[...] = (gf * jax.nn.sigmoid(gf) * uf).astype(act_ref.dtype)

    o_spec = pl.BlockSpec((tm, tn), lambda i, j, kk: (i, j))
    out = jax.ShapeDtypeStruct((m, n), _ACT)
    return pl.pallas_call(
        body, name=name, grid=(m // tm, n // tn, nk),
        in_specs=[pl.BlockSpec((tm, tk), lambda i, j, kk: (i, kk)),
                  pl.BlockSpec((tn, tk), lambda i, j, kk: (j, kk)),
                  pl.BlockSpec((tn, tk), lambda i, j, kk: (j + n // tn, kk))],
        out_specs=[o_spec, o_spec, o_spec], out_shape=[out, out, out],
        scratch_shapes=[pltpu.VMEM((tm, tn), _F32), pltpu.VMEM((tm, tn), _F32)],
        compiler_params=_params(("parallel", "parallel", "arbitrary")),
    )(a, w_in_t, w_in_t)


def _gate_up_bwd(dout, w_out, gate, up, *, name):
    m, k = dout.shape
    n = w_out.shape[0]
    tm, tn, tk = _pick(m, SWIGLU_ROWS), _pick(n, MM_TILE), _pick(k, MM_TILE)
    nk = k // tk

    def body(d_ref, w_ref, g_ref, u_ref, dg_ref, du_ref, acc):
        kk = pl.program_id(2)

        @pl.when(kk == 0)
        def _():
            acc[...] = jnp.zeros_like(acc)

        acc[...] += lax.dot_general(d_ref[...].astype(_MXU), w_ref[...].astype(_MXU), _NT,
                                    preferred_element_type=_F32)

        @pl.when(kk == nk - 1)
        def _():
            daf = acc[...]
            gf = g_ref[...].astype(_F32)
            sg = jax.nn.sigmoid(gf)
            dg_ref[...] = (daf * u_ref[...].astype(_F32) * (sg + gf * sg * (1.0 - sg))).astype(dg_ref.dtype)
            du_ref[...] = (daf * (gf * sg)).astype(du_ref.dtype)

    o_spec = pl.BlockSpec((tm, tn), lambda i, j, kk: (i, j))
    out = jax.ShapeDtypeStruct((m, n), _ACT)
    return pl.pallas_call(
        body, name=name, grid=(m // tm, n // tn, nk),
        in_specs=[pl.BlockSpec((tm, tk), lambda i, j, kk: (i, kk)), pl.BlockSpec((tn, tk), lambda i, j, kk: (j, kk)),
                  o_spec, o_spec],
        out_specs=[o_spec, o_spec], out_shape=[out, out],
        scratch_shapes=[pltpu.VMEM((tm, tn), _F32)],
        compiler_params=_params(("parallel", "parallel", "arbitrary")),
    )(dout, w_out, gate, up)


def _final_loss(x, g, target, *, name):
    m, d = x.shape
    tm = _rows(m, ROW_BLOCK)

    def body(x_ref, g_ref, t_ref, loss_ref, dx_ref, dg_ref):
        @pl.when(pl.program_id(0) == 0)
        def _():
            loss_ref[...] = jnp.zeros_like(loss_ref)
            dg_ref[...] = jnp.zeros_like(dg_ref)

        xf = x_ref[...]
        r = lax.rsqrt(jnp.mean(xf * xf, axis=-1, keepdims=True) + NORM_EPS)
        xh = xf * r
        err = xh * g_ref[...] - t_ref[...]
        loss_ref[...] += 0.5 * jnp.sum(jnp.mean(err * err, axis=-1, keepdims=True))
        dy = err * (1.0 / d)
        dg_ref[...] += jnp.sum(dy * xh, axis=0, keepdims=True)
        gdy = dy * g_ref[...]
        dx_ref[...] = r * (gdy - xh * jnp.mean(gdy * xh, axis=-1, keepdims=True))

    row = pl.BlockSpec((tm, d), lambda i: (i, 0))
    vec = pl.BlockSpec((1, d), lambda i: (0, 0))
    loss, dx, dg = pl.pallas_call(
        body, name=name, grid=(m // tm,),
        in_specs=[row, vec, row],
        out_specs=[pl.BlockSpec((8, LANES), lambda i: (0, 0)), row, vec],
        out_shape=[jax.ShapeDtypeStruct((8, LANES), _F32), jax.ShapeDtypeStruct((m, d), _F32),
                   jax.ShapeDtypeStruct((1, d), _F32)],
        compiler_params=_params(("arbitrary",)),
    )(x, g.reshape(1, d), target)
    return loss[0, 0], dx, dg.reshape(d)


def _reduce_adamw(parts, w, m, v, *, name):
    rws, cols = w.shape
    tr = rws
    for cand in range(min(rws, 256), 0, -SUBLANES_16BIT):
        if cand % SUBLANES_16BIT == 0 and rws % cand == 0:
            tr = cand
            break

    def body(p_ref, w_ref, m_ref, v_ref, g_ref, d_ref, nm_ref, nv_ref):
        gsum = p_ref[0].astype(_F32)
        for dev in range(1, N_DEV):
            gsum = gsum + p_ref[dev].astype(_F32)
        m2 = ADAM_B1 * m_ref[...] + (1.0 - ADAM_B1) * gsum
        v2 = ADAM_B2 * v_ref[...] + (1.0 - ADAM_B2) * (gsum * gsum)
        m_hat = m2 / (1.0 - ADAM_B1 ** ADAM_STEP)
        v_hat = v2 / (1.0 - ADAM_B2 ** ADAM_STEP)
        g_ref[...] = gsum
        d_ref[...] = -ADAM_LR * (m_hat / (jnp.sqrt(v_hat) + ADAM_EPS) + ADAM_WD * w_ref[...])
        nm_ref[...] = m2
        nv_ref[...] = v2

    blk = pl.BlockSpec((tr, cols), lambda i: (i, 0))
    out = jax.ShapeDtypeStruct((rws, cols), _F32)
    return pl.pallas_call(
        body, name=name, grid=(rws // tr,),
        in_specs=[pl.BlockSpec((N_DEV, tr, cols), lambda i: (0, i, 0)), blk, blk, blk],
        out_specs=[blk, blk, blk, blk], out_shape=[out, out, out, out],
        compiler_params=_params(("parallel",)),
    )(parts, w, m, v)


def _mesh_pos():
    return lax.axis_index("x"), lax.axis_index("y"), lax.axis_index("c")


def _all_gather(block):
    rws, cols = block.shape

    def body(x_ref, out_ref, send_sems, recv_sems, local_sem):
        start, forward, finish = _gather_phases(x_ref, out_ref, send_sems, recv_sems, local_sem)
        start()
        forward()
        finish()

    return pl.pallas_call(
        body, name="weights_all_gather",
        out_shape=jax.ShapeDtypeStruct((N_DEV, rws, cols), block.dtype),
        in_specs=[pl.BlockSpec(memory_space=pl.ANY)],
        out_specs=pl.BlockSpec(memory_space=pl.ANY),
        scratch_shapes=_comm_scratch(),
    )(block)


def _comm_scratch():
    return [pltpu.SemaphoreType.DMA((N_DEV - 1,)), pltpu.SemaphoreType.DMA((N_DEV - 1,)), pltpu.SemaphoreType.DMA]


def _gather_phases(x_ref, out_ref, send_sems, recv_sems, local_sem):
    x, y, c = _mesh_pos()
    me, sibling = (x, y, c), (x, y, 1 - c)
    chips = [(1 - x, y), (x, 1 - y), (1 - x, 1 - y)]

    def slot(px, py, pc):
        return out_ref.at[4 * px + 2 * py + pc]

    def copy(k, blk, to, src=None):
        return pltpu.make_async_remote_copy(
            src_ref=slot(*blk) if src is None else src, dst_ref=slot(*blk),
            send_sem=send_sems.at[k], recv_sem=recv_sems.at[k],
            device_id=to, device_id_type=pl.DeviceIdType.MESH)

    mine = pltpu.make_async_copy(x_ref, slot(*me), local_sem)
    first = [copy(0, me, sibling, src=x_ref)]
    first += [copy(1 + j, me, (*chip, c), src=x_ref) for j, chip in enumerate(chips)]
    passed = [copy(4 + j, (*chip, c), sibling) for j, chip in enumerate(chips)]

    def start():
        mine.start()
        for cp in first:
            cp.start()

    def forward():
        for j, chip in enumerate(chips):
            copy(1 + j, (*chip, c), me).wait_recv()
            passed[j].start()

    def finish():
        copy(0, sibling, me).wait_recv()
        for j, chip in enumerate(chips):
            copy(4 + j, (*chip, 1 - c), me).wait_recv()
        for cp in first + passed:
            cp.wait_send()
        mine.wait()

    return start, forward, finish


def _exchange(parts):
    def body(g_ref, out_ref, send_sems, recv_sems, local_sem):
        start, finish = _exchange_phases(g_ref, out_ref, send_sems, recv_sems, local_sem)
        start()
        finish()

    return pl.pallas_call(
        body, name="grads_exchange",
        out_shape=jax.ShapeDtypeStruct(parts.shape, parts.dtype),
        in_specs=[pl.BlockSpec(memory_space=pl.ANY)],
        out_specs=pl.BlockSpec(memory_space=pl.ANY),
        scratch_shapes=_comm_scratch(),
    )(parts)


def _exchange_phases(g_ref, out_ref, send_sems, recv_sems, local_sem):
    x, y, c = _mesh_pos()
    me = 4 * x + 2 * y + c
    mine = pltpu.make_async_copy(g_ref.at[me], out_ref.at[me], local_sem)
    sends, arrivals = [], []
    for k in range(1, N_DEV):
        px = 1 - x if k & 4 else x
        py = 1 - y if k & 2 else y
        pc = 1 - c if k & 1 else c
        peer = 4 * px + 2 * py + pc
        sems = dict(send_sem=send_sems.at[k - 1], recv_sem=recv_sems.at[k - 1],
                    device_id=(px, py, pc), device_id_type=pl.DeviceIdType.MESH)
        sends.append(pltpu.make_async_remote_copy(src_ref=g_ref.at[peer], dst_ref=out_ref.at[me], **sems))
        arrivals.append(pltpu.make_async_remote_copy(src_ref=g_ref.at[peer], dst_ref=out_ref.at[peer], **sems))

    def start():
        mine.start()
        for cp in sends:
            cp.start()

    def finish():
        for cp in arrivals:
            cp.wait_recv()
        for cp in sends:
            cp.wait_send()
        mine.wait()

    return start, finish


_BIG = (("w_in_ab", 2), ("mla_w_uq", 1), ("mla_w_ukv", 1), ("w_out_ab", 2), ("gqa_w_q", 1), ("gqa_w_kv", 1),
        ("gqa_w_o", 1), ("ffn_w_in", 1), ("ffn_w_out", 1))
_TRANSPOSED = ("ffn_w_in",)


def _stored(arrays):
    return {n: (a.transpose(0, 2, 1) if n in _TRANSPOSED else a) for n, a in arrays.items()}
_SMALL = ("mix_norm_ab", "ffn_norm", "final_norm", "mix_norm_c", "mla_q_norm", "mla_kv_norm", "gqa_q_norm", "gqa_k_norm")
SMALL_ROWS = 16
GATHER_TAIL_ROWS = 16
EXCHANGE_TAIL_ROWS = 32


def _pad_rows(a2d):
    pad = -a2d.shape[0] % SUBLANES_16BIT
    return jnp.pad(a2d, ((0, pad), (0, 0))) if pad else a2d


def _pack_rows(shard):
    return _pad_rows(shard.reshape(-1, PACK_COLS))


def _packed_rows(shape):
    n = 1
    for d in shape:
        n *= d
    rows = n // PACK_COLS
    return rows + (-rows % SUBLANES_16BIT)


_FIRST = {"w_in_ab": (0, 1), "mla_w_uq": (0, 1), "mla_w_ukv": (0, 1)}
_REST = {"w_in_ab": (1, 1), "mla_w_uq": (1, 1), "mla_w_ukv": (1, 1), "w_out_ab": (0, 2), "gqa_w_q": (0, 2),
         "gqa_w_kv": (0, 2), "gqa_w_o": (0, 2), "ffn_w_in": (0, 4), "ffn_w_out": (0, 4)}
PACK_ROWS = 128


def _group_of(arrays, group):
    return {n: arrays[n][lo:lo + cnt] for n, (lo, cnt) in group.items()}


def _pack_big(shards):
    parts = [_pack_rows(shards[n]) for n, _ in _BIG if n in shards]
    fill = -sum(p.shape[0] for p in parts) % PACK_ROWS
    return jnp.concatenate(parts + [jnp.zeros((fill, PACK_COLS), parts[0].dtype)], axis=0)


def _unpack_big(packed, shapes):
    out, off = {}, 0
    for n, _ in _BIG:
        if n not in shapes:
            continue
        size = 1
        for d in shapes[n]:
            size *= d
        out[n] = packed[off:off + size // PACK_COLS].reshape(shapes[n])
        off += _packed_rows(shapes[n])
    return out


def _unpack_gathered(gathered, shapes):
    out, off = {}, 0
    for n, axis in _BIG:
        if n not in shapes:
            continue
        size = 1
        for d in shapes[n]:
            size *= d
        l3 = (shapes[n][0], shapes[n][1], size // (shapes[n][0] * shapes[n][1]))
        sh = gathered[:, off:off + size // PACK_COLS].reshape((N_DEV,) + l3)
        if axis == 1:
            full = sh.transpose(1, 0, 2, 3).reshape(l3[0], N_DEV * l3[1], l3[2])
        else:
            full = sh.transpose(1, 2, 0, 3).reshape(l3[0], l3[1], N_DEV * l3[2])
        out[n] = full
        off += _packed_rows(shapes[n])
    return out


def _split_for_devices(layers, axis):
    rws, cols = layers[0].shape
    if axis == 1:
        flat = jnp.concatenate([g.reshape(N_DEV, -1, PACK_COLS) for g in layers], axis=1)
    else:
        sh = jnp.stack(layers).reshape(len(layers), rws, N_DEV, cols // N_DEV).transpose(2, 0, 1, 3)
        flat = sh.reshape(N_DEV, -1, PACK_COLS)
    pad = -flat.shape[1] % SUBLANES_16BIT
    return jnp.pad(flat, ((0, 0), (0, pad), (0, 0))) if pad else flat


def _to_bits16(a_f32_rows):
    r = a_f32_rows.shape[0]
    return lax.bitcast_convert_type(a_f32_rows, jnp.bfloat16).reshape(2 * r, PACK_COLS)


def _from_bits16(a_bf16_rows):
    lead, r = a_bf16_rows.shape[:-2], a_bf16_rows.shape[-2]
    return lax.bitcast_convert_type(a_bf16_rows.reshape(lead + (r // 2, PACK_COLS, 2)), _F32)


def _small_sizes():
    return {"mix_norm_ab": 2 * D_MODEL, "ffn_norm": DEPTH * D_MODEL, "final_norm": D_MODEL, "mix_norm_c": 2 * D_MODEL,
            "mla_q_norm": 2 * MLA_Q_RANK, "mla_kv_norm": 2 * MLA_KV_RANK, "gqa_q_norm": 2 * GQA_HEAD_DIM,
            "gqa_k_norm": 2 * GQA_HEAD_DIM}


def _pack_small(vals):
    flat = jnp.concatenate([vals[n].reshape(-1).astype(_F32) for n in _SMALL])
    return jnp.pad(flat, (0, SMALL_ROWS * PACK_COLS - flat.shape[0])).reshape(SMALL_ROWS, PACK_COLS)


def _unpack_small(pack):
    flat, out, off = pack.reshape(-1), {}, 0
    sizes = _small_sizes()
    for n in _SMALL:
        out[n] = flat[off:off + sizes[n]]
        off += sizes[n]
    return out


def _angles(pos, dim):
    freqs = ROPE_THETA ** (-jnp.arange(0, dim, 2, dtype=_F32) / dim)
    ang = pos.astype(_F32)[:, None] * freqs[None, :]
    return jnp.cos(ang), jnp.sin(ang)


def _rope_tables(s):
    pos = jnp.arange(s)
    cos_t, sin_t = _angles(pos, MLA_ROPE)
    mla_c = jnp.tile(jnp.concatenate([cos_t, cos_t], -1), (1, LANES // ROT_CHUNK))
    mla_s = jnp.tile(jnp.concatenate([-sin_t, sin_t], -1), (1, LANES // ROT_CHUNK))
    rows = s // GRID_W
    row_idx = jnp.broadcast_to(jnp.arange(rows)[:, None], (rows, GRID_W)).reshape(-1)
    col_idx = jnp.broadcast_to(jnp.arange(GRID_W)[None, :], (rows, GRID_W)).reshape(-1)
    cos_r, sin_r = _angles(row_idx, GQA_HEAD_DIM // 2)
    cos_c, sin_c = _angles(col_idx, GQA_HEAD_DIM // 2)
    gqa_c = jnp.tile(jnp.concatenate([cos_r, cos_r, cos_c, cos_c], -1), (1, LANES // GQA_HEAD_DIM))
    gqa_s = jnp.tile(jnp.concatenate([-sin_r, sin_r, -sin_c, sin_c], -1), (1, LANES // GQA_HEAD_DIM))
    return (mla_c, mla_s), (gqa_c, gqa_s)


def _heads(x2d, h):
    s = x2d.shape[0]
    return x2d.reshape(s, h, -1).transpose(1, 0, 2)


def _unheads(xh):
    h, s, d = xh.shape
    return xh.transpose(1, 0, 2).reshape(s, h * d)


def _to_res(x2d, dil):
    s, cols = x2d.shape
    if dil > 1:
        x2d = x2d.reshape(s // dil, dil, cols).transpose(1, 0, 2).reshape(s, cols)
    return x2d.reshape(s, cols // DIL_HEAD_DIM, DIL_HEAD_DIM).transpose(1, 0, 2)


def _from_res(xh, dil):
    heads, s, d = xh.shape
    x2d = xh.transpose(1, 0, 2).reshape(s, heads * d)
    if dil > 1:
        x2d = x2d.reshape(dil, s // dil, heads * d).transpose(1, 0, 2).reshape(s, heads * d)
    return x2d


def _group_major(w_b):
    rws = w_b.shape[0]
    return w_b.reshape(rws, 3, DIL_GROUPS, DIL_SLOTS * DIL_HEAD_DIM).transpose(0, 2, 1, 3).reshape(rws, -1)


def _res_to_tok(xh, dil):
    sl, s, c = xh.shape
    if dil == 1:
        return xh
    return xh.reshape(sl, dil, s // dil, c).transpose(0, 2, 1, 3).reshape(sl, s, c)


def _tok_to_res(xh, dil):
    sl, s, c = xh.shape
    if dil == 1:
        return xh
    return xh.reshape(sl, s // dil, dil, c).transpose(0, 2, 1, 3).reshape(sl, s, c)


def _ffn_fwd(x, w, tag):
    hn = _rmsnorm(x, w["norm"], out_dtype=_ACT, name=f"{tag}_norm")
    gate, up, act = _gate_up(hn, w["w_in_t"], name=f"{tag}_gate_up")
    out = _matmul(act, w["w_out"], res=x, name=f"{tag}_out")
    return out, (x, hn, gate, up, act)


def _ffn_bwd(dout, saved, w, tag):
    x, hn, gate, up, act = saved
    w_gate_t, w_up_t = w["w_in_t"][:FFN_HIDDEN], w["w_in_t"][FFN_HIDDEN:]
    d_w_out = _matmul(act, dout, trans_a=True, name=f"{tag}_dwout")
    dgate, dup = _gate_up_bwd(dout, w["w_out"], gate, up, name=f"{tag}_dgate_up")
    d_w_gate_t = _matmul(dgate, hn, trans_a=True, name=f"{tag}_dwgate")
    d_w_up_t = _matmul(dup, hn, trans_a=True, name=f"{tag}_dwup")
    dhn = _matmul(dgate, w_gate_t, name=f"{tag}_dhn_gate")
    dhn = _matmul(dup, w_up_t, res=dhn, name=f"{tag}_dhn_up")
    dx, dnorm = _rmsnorm_bwd(x, w["norm"], dhn, dout, name=f"{tag}_dnorm")
    return dx, {"norm": dnorm, "w_in_t": jnp.concatenate([d_w_gate_t, d_w_up_t], axis=0), "w_out": d_w_out}


def _even_fwd(x, w, tabs, slopes, tag, gather=None, late=None):
    s = x.shape[0]
    (mla_c, mla_s), _ = tabs
    hn = _rmsnorm(x, w["norm"], out_dtype=_ACT, name=f"{tag}_norm")
    za = _matmul(hn, w["w_a"], name=f"{tag}_in_a")
    zb = _matmul(hn, w["w_b"], out_dtype=_ACT, name=f"{tag}_in_b")
    cq, ckv, kr = za[:, :MLA_Q_RANK], za[:, MLA_Q_RANK:MLA_Q_RANK + MLA_KV_RANK], za[:, MLA_Q_RANK + MLA_KV_RANK:]
    cqn = _rmsnorm(cq, w["q_norm"], out_dtype=_ACT, name=f"{tag}_qnorm")
    ckvn = _rmsnorm(ckv, w["kv_norm"], out_dtype=_ACT, name=f"{tag}_kvnorm")
    q = _matmul(cqn, w["w_uq"], scale=(MLA_NOPE + MLA_ROPE) ** -0.5 * LOG2_E,
                name=f"{tag}_uq")
    kv = _matmul(ckvn, w["w_ukv"], out_dtype=_ACT, name=f"{tag}_ukv")
    nn = MLA_HEADS * MLA_NOPE
    q_rope = _rope(q[:, nn:], mla_c, mla_s, out_dtype=_ACT, name=f"{tag}_rope_q")
    k_rope = _rope(kr, mla_c, mla_s, out_dtype=_ACT, name=f"{tag}_rope_k")[:, :MLA_ROPE]
    qh = jnp.concatenate([_heads_t(q[:, :nn].astype(_ACT), MLA_HEADS), _heads_t(q_rope, MLA_HEADS)], axis=1)
    kh = jnp.concatenate([_heads(kv[:, :nn], MLA_HEADS),
                          jnp.broadcast_to(k_rope[None], (MLA_HEADS, s, MLA_ROPE))], axis=-1)
    vh = _heads(kv[:, nn:], MLA_HEADS)
    oa, lse_a, *gathered = _attn_fwd(qh, kh, _chunk_t(kv[:, nn:], MLA_HEADS), out_dtype=_ACT, name=f"{tag}_mla",
                                     gather=gather)
    if late is not None:
        w = late(gathered[0])

    ng = 3 * DIL_SLOTS * DIL_HEAD_DIM
    outs, lses, dil_saved = [], [], []
    for gi, (_, dil) in enumerate(DIL_PAIRS):
        hs = slice(gi * DIL_SLOTS, (gi + 1) * DIL_SLOTS)
        qkv = _to_res(zb[:, gi * ng:(gi + 1) * ng], dil)
        o, l = _win_fwd(qkv, slopes[hs], dil=dil, name=f"{tag}_dil{gi}")
        dil_saved.append((qkv, l))
        outs.append(_res_to_tok(o, dil))
        lses.append(_res_to_tok(l, dil))
    o3, lse3 = tuple(outs), tuple(lses)
    comb = _merge_fwd(o3, lse3, name=f"{tag}_merge")
    cat = jnp.concatenate([_unheads_t(oa), _unheads(comb)], axis=-1)
    out = _matmul(cat, w["w_out"], res=x, name=f"{tag}_out")
    saved = (x, hn, cq, ckv, cqn, ckvn, qh, kh, vh, oa, lse_a, dil_saved, o3, lse3, cat)
    return out, saved, (gathered[0] if gathered else None)


def _even_bwd(dout, saved, w, tabs, slopes, tag, exchange=None):
    x, hn, cq, ckv, cqn, ckvn, qh, kh, vh, oa, lse_a, dil_saved, o3, lse3, cat = saved
    (mla_c, mla_s), _ = tabs
    nn = MLA_HEADS * MLA_NOPE
    dcat = _matmul(dout, w["w_out"], trans_b=True, out_dtype=_ACT, name=f"{tag}_dcat")
    d_w_out = _matmul(cat, dout, trans_a=True, name=f"{tag}_dwout")
    nv = MLA_HEADS * MLA_V

    doa = _heads(dcat[:, :nv], MLA_HEADS)
    dqh, dkt, dvt, *received = _attn_bwd(qh, kh, vh, oa, doa, lse_a, scale=(MLA_NOPE + MLA_ROPE) ** -0.5,
                                         name=f"{tag}_mla_bwd",
                                         exchange=None if exchange is None else exchange(d_w_out))
    dq_rope = _rope(_unheads(dqh[..., MLA_NOPE:]), mla_c, -mla_s, out_dtype=_ACT, name=f"{tag}_drope_q")
    dq = jnp.concatenate([_unheads(dqh[..., :MLA_NOPE]).astype(_ACT), dq_rope], axis=-1)
    dk3 = _unchunk(dkt).reshape(-1, MLA_HEADS, MLA_NOPE + MLA_ROPE)
    dkr = _rope(dk3[..., MLA_NOPE:].reshape(-1, MLA_HEADS * MLA_ROPE), mla_c, -mla_s, out_dtype=_F32,
                sum_chunks=True, name=f"{tag}_drope_k")
    dkv = jnp.concatenate([dk3[..., :MLA_NOPE].reshape(-1, nn), _unchunk(dvt)], axis=-1).astype(_ACT)
    d_w_uq = _matmul(cqn, dq, trans_a=True, name=f"{tag}_dwuq")
    d_w_ukv = _matmul(ckvn, dkv, trans_a=True, name=f"{tag}_dwukv")
    dcqn = _matmul(dq, w["w_uq"], trans_b=True, name=f"{tag}_dcqn")
    dckvn = _matmul(dkv, w["w_ukv"], trans_b=True, name=f"{tag}_dckvn")
    dcq, d_q_norm = _rmsnorm_bwd(cq, w["q_norm"], dcqn, out_dtype=_ACT, name=f"{tag}_dqnorm")
    dckv, d_kv_norm = _rmsnorm_bwd(ckv, w["kv_norm"], dckvn, out_dtype=_ACT, name=f"{tag}_dkvnorm")
    lane = jnp.arange(LANES) < MLA_ROPE
    dza = jnp.concatenate([dcq, dckv, jnp.where(lane[None], dkr, 0.0).astype(_ACT)], axis=-1)

    dcomb = _heads(dcat[:, nv:], DIL_SLOTS)
    do3, delta3 = _merge_bwd(dcomb, o3, lse3, name=f"{tag}_dmerge")
    dslabs = []
    for gi, (_, dil) in enumerate(DIL_PAIRS):
        hs = slice(gi * DIL_SLOTS, (gi + 1) * DIL_SLOTS)
        qkv, l = dil_saved[gi]
        grads = (qkv, _tok_to_res(do3[gi], dil), l, _tok_to_res(delta3[gi], dil), slopes[hs])
        a = _win_bwd_dq(*grads, dil=dil, name=f"{tag}_dil{gi}_dq")
        bc = _win_bwd_dkv(*grads, dil=dil, name=f"{tag}_dil{gi}_dkv")
        dslabs.append(_from_res(jnp.concatenate([a, bc], axis=0), dil))
    dzb = jnp.concatenate(dslabs, axis=-1)

    d_w_a = _matmul(hn, dza, trans_a=True, name=f"{tag}_dwa")
    d_w_b = _matmul(hn, dzb, trans_a=True, name=f"{tag}_dwb")
    dhn = _matmul(dza, w["w_a"], trans_b=True, name=f"{tag}_dhn_a")
    dhn = _matmul(dzb, w["w_b"], trans_b=True, res=dhn, name=f"{tag}_dhn_b")
    dx, dnorm = _rmsnorm_bwd(x, w["norm"], dhn, dout, name=f"{tag}_dnorm")
    grads = {"norm": dnorm, "w_a": d_w_a, "w_b": d_w_b, "q_norm": d_q_norm, "kv_norm": d_kv_norm,
             "w_uq": d_w_uq, "w_ukv": d_w_ukv, "w_out": d_w_out}
    return dx, grads, (received[0] if received else None)


def _odd_fwd(x, w, tabs, tag):
    s = x.shape[0]
    _, (gqa_c, gqa_s) = tabs
    nk = GQA_KV_HEADS * GQA_HEAD_DIM
    hn = _rmsnorm(x, w["norm"], out_dtype=_ACT, name=f"{tag}_norm")
    q = _matmul(hn, w["w_q"], name=f"{tag}_q")
    kv = _matmul(hn, w["w_kv"], name=f"{tag}_kv")
    k = kv[:, :nk]
    qh = _heads_t(_headnorm_rope(q, w["q_norm"], gqa_c, gqa_s, scale=GQA_HEAD_DIM ** -0.5 * LOG2_E,
                                 name=f"{tag}_prep_q"), GQA_HEADS)
    kh = _heads(_headnorm_rope(k, w["k_norm"], gqa_c, gqa_s, name=f"{tag}_prep_k"), GQA_KV_HEADS)
    v = kv[:, nk:].astype(_ACT)
    vh = _heads(v, GQA_KV_HEADS)
    o, lse = _attn_fwd(qh, kh, _chunk_t(v, GQA_KV_HEADS), out_dtype=_ACT, name=f"{tag}_gqa")
    ocat = _unheads_t(o)
    out = _matmul(ocat, w["w_o"], res=x, name=f"{tag}_out")
    return out, (x, hn, q, k, qh, kh, vh, o, lse, ocat)


def _odd_bwd(dout, saved, w, tabs, tag):
    x, hn, q, k, qh, kh, vh, o, lse, ocat = saved
    s = x.shape[0]
    _, (gqa_c, gqa_s) = tabs
    docat = _matmul(dout, w["w_o"], trans_b=True, out_dtype=_ACT, name=f"{tag}_docat")
    d_w_o = _matmul(ocat, dout, trans_a=True, name=f"{tag}_dwo")
    doh = _heads(docat, GQA_HEADS)
    dqh, dkt, dvt = _attn_bwd(qh, kh, vh, o, doh, lse, scale=GQA_HEAD_DIM ** -0.5, name=f"{tag}_gqa_bwd")
    dq, d_q_norm = _headnorm_rope_bwd(q, w["q_norm"], _unheads(dqh), gqa_c, gqa_s, name=f"{tag}_dprep_q")
    dk, d_k_norm = _headnorm_rope_bwd(k, w["k_norm"], _unchunk(dkt), gqa_c, gqa_s, name=f"{tag}_dprep_k")
    dkv = jnp.concatenate([dk, _unchunk(dvt).astype(_ACT)], axis=-1)
    d_w_q = _matmul(hn, dq, trans_a=True, name=f"{tag}_dwq")
    d_w_kv = _matmul(hn, dkv, trans_a=True, name=f"{tag}_dwkv")
    dhn = _matmul(dq, w["w_q"], trans_b=True, name=f"{tag}_dhn_q")
    dhn = _matmul(dkv, w["w_kv"], trans_b=True, res=dhn, name=f"{tag}_dhn_kv")
    dx, dnorm = _rmsnorm_bwd(x, w["norm"], dhn, dout, name=f"{tag}_dnorm")
    return dx, {"norm": dnorm, "w_q": d_w_q, "w_kv": d_w_kv, "q_norm": d_q_norm, "k_norm": d_k_norm, "w_o": d_w_o}


def _split_heads_cols(wm, heads, first):
    rws = wm.shape[0]
    w3 = wm.reshape(rws, heads, -1)
    return jnp.concatenate([w3[:, :, :first].reshape(rws, -1), w3[:, :, first:].reshape(rws, -1)], axis=-1)


def _merge_heads_cols(wm, heads, first):
    rws, cols = wm.shape
    a = wm[:, :heads * first].reshape(rws, heads, first)
    b = wm[:, heads * first:].reshape(rws, heads, cols // heads - first)
    return jnp.concatenate([a, b], axis=-1).reshape(rws, cols)


def _layer_weights(full, gains, layer):
    i = layer // 2

    def stacked(name, idx):
        for group, arrays in zip((_FIRST, _REST), full):
            lo, cnt = group.get(name, (0, 0))
            if lo <= idx < lo + cnt:
                return None if arrays is None else arrays[name][idx - lo]
        raise KeyError((name, idx))

    ffn = {"norm": gains["ffn_norm"][layer], "w_in_t": stacked("ffn_w_in", layer), "w_out": stacked("ffn_w_out", layer)}
    if layer % 2 == 0:
        w_in = stacked("w_in_ab", i)
        mix = {"norm": gains["mix_norm_ab"][i],
               "w_a": jnp.pad(w_in[:, :IN_A], ((0, 0), (0, IN_A_PAD - IN_A))), "w_b": _group_major(w_in[:, IN_A:]),
               "q_norm": gains["mla_q_norm"][i], "kv_norm": gains["mla_kv_norm"][i],
               "w_uq": _split_heads_cols(stacked("mla_w_uq", i), MLA_HEADS, MLA_NOPE),
               "w_ukv": _split_heads_cols(stacked("mla_w_ukv", i), MLA_HEADS, MLA_NOPE),
               "w_out": stacked("w_out_ab", i)}
    else:
        mix = {"norm": gains["mix_norm_c"][i], "w_q": stacked("gqa_w_q", i), "w_kv": stacked("gqa_w_kv", i),
               "q_norm": gains["gqa_q_norm"][i], "k_norm": gains["gqa_k_norm"][i], "w_o": stacked("gqa_w_o", i)}
    return mix, ffn


def _pack_grads(grads, group):
    parts = [_split_for_devices(grads[n][group[n][0]:group[n][0] + group[n][1]], axis)
             for n, axis in _BIG if n in group]
    fill = -sum(p.shape[1] for p in parts) % PACK_ROWS
    return jnp.concatenate(parts + [jnp.zeros((N_DEV, fill, PACK_COLS), parts[0].dtype)], axis=1)


def kernel(x, mix_norm_ab, w_in_ab, mla_q_norm, mla_kv_norm, mla_w_uq, mla_w_ukv, w_out_ab, mix_norm_c, gqa_w_q, gqa_w_kv, gqa_q_norm, gqa_k_norm, gqa_w_o, ffn_norm, ffn_w_in, ffn_w_out, final_norm, loss_target, m_mix_norm_ab, m_w_in_ab, m_mla_q_norm, m_mla_kv_norm, m_mla_w_uq, m_mla_w_ukv, m_w_out_ab, m_mix_norm_c, m_gqa_w_q, m_gqa_w_kv, m_gqa_q_norm, m_gqa_k_norm, m_gqa_w_o, m_ffn_norm, m_ffn_w_in, m_ffn_w_out, m_final_norm, v_mix_norm_ab, v_w_in_ab, v_mla_q_norm, v_mla_kv_norm, v_mla_w_uq, v_mla_w_ukv, v_w_out_ab, v_mix_norm_c, v_gqa_w_q, v_gqa_w_kv, v_gqa_q_norm, v_gqa_k_norm, v_gqa_w_o, v_ffn_norm, v_ffn_w_in, v_ffn_w_out, v_final_norm):
    wts = dict(mix_norm_ab=mix_norm_ab, w_in_ab=w_in_ab, mla_q_norm=mla_q_norm, mla_kv_norm=mla_kv_norm,
               mla_w_uq=mla_w_uq, mla_w_ukv=mla_w_ukv, w_out_ab=w_out_ab, mix_norm_c=mix_norm_c, gqa_w_q=gqa_w_q,
               gqa_w_kv=gqa_w_kv, gqa_q_norm=gqa_q_norm, gqa_k_norm=gqa_k_norm, gqa_w_o=gqa_w_o, ffn_norm=ffn_norm,
               ffn_w_in=ffn_w_in, ffn_w_out=ffn_w_out, final_norm=final_norm)
    mom = dict(mix_norm_ab=m_mix_norm_ab, w_in_ab=m_w_in_ab, mla_q_norm=m_mla_q_norm, mla_kv_norm=m_mla_kv_norm,
               mla_w_uq=m_mla_w_uq, mla_w_ukv=m_mla_w_ukv, w_out_ab=m_w_out_ab, mix_norm_c=m_mix_norm_c,
               gqa_w_q=m_gqa_w_q, gqa_w_kv=m_gqa_w_kv, gqa_q_norm=m_gqa_q_norm, gqa_k_norm=m_gqa_k_norm,
               gqa_w_o=m_gqa_w_o, ffn_norm=m_ffn_norm, ffn_w_in=m_ffn_w_in, ffn_w_out=m_ffn_w_out,
               final_norm=m_final_norm)
    var = dict(mix_norm_ab=v_mix_norm_ab, w_in_ab=v_w_in_ab, mla_q_norm=v_mla_q_norm, mla_kv_norm=v_mla_kv_norm,
               mla_w_uq=v_mla_w_uq, mla_w_ukv=v_mla_w_ukv, w_out_ab=v_w_out_ab, mix_norm_c=v_mix_norm_c,
               gqa_w_q=v_gqa_w_q, gqa_w_kv=v_gqa_w_kv, gqa_q_norm=v_gqa_q_norm, gqa_k_norm=v_gqa_k_norm,
               gqa_w_o=v_gqa_w_o, ffn_norm=v_ffn_norm, ffn_w_in=v_ffn_w_in, ffn_w_out=v_ffn_w_out,
               final_norm=v_final_norm)
    big_names = [n for n, _ in _BIG]
    groups = (_FIRST, _REST)
    big_w, big_m, big_v = (_stored({n: src[n] for n in big_names}) for src in (wts, mom, var))
    w_grp = [_group_of(big_w, grp) for grp in groups]
    shapes = [{n: a.shape for n, a in w.items()} for w in w_grp]
    w_packs = [_pack_big(w) for w in w_grp]
    n_rows = [p.shape[0] for p in w_packs]
    xs = x[0]
    s = xs.shape[0]
    me = 4 * lax.axis_index("x") + 2 * lax.axis_index("y") + lax.axis_index("c")
    c_cols = mix_norm_c.shape[1]

    tail = jnp.pad(mix_norm_c.reshape(-1), (0, GATHER_TAIL_ROWS // 2 * PACK_COLS - mix_norm_c.size))
    tail = _to_bits16(tail.reshape(GATHER_TAIL_ROWS // 2, PACK_COLS))
    gathered = _all_gather(jnp.concatenate([w_packs[0].astype(jnp.bfloat16), tail], axis=0))
    full = [_unpack_gathered(gathered[:, :n_rows[0]], shapes[0]), None]
    c_all = _from_bits16(gathered[:, n_rows[0]:]).reshape(N_DEV, -1)[:, :mix_norm_c.size]
    c_full = c_all.reshape(N_DEV, 2, c_cols).transpose(1, 0, 2).reshape(2, N_DEV * c_cols)
    gains = dict(mix_norm_ab=mix_norm_ab, mla_q_norm=mla_q_norm, mla_kv_norm=mla_kv_norm, mix_norm_c=c_full,
                 gqa_q_norm=gqa_q_norm, gqa_k_norm=gqa_k_norm, ffn_norm=ffn_norm)

    tabs = _rope_tables(s)
    slopes = jnp.exp2(-8.0 * jnp.arange(1, DIL_HEADS + 1, dtype=_F32) / DIL_HEADS)

    h = xs
    saved = []
    for layer in range(DEPTH):
        mix_w, ffn_w = _layer_weights(full, gains, layer)
        if layer == 0:
            def late(rest):
                full[1] = _unpack_gathered(rest, shapes[1])
                return _layer_weights(full, gains, 0)[0]

            h, sv_mix, _ = _even_fwd(h, mix_w, tabs, slopes, f"l{layer}_mix", gather=w_packs[1].astype(jnp.bfloat16),
                                     late=late)
            mix_w, ffn_w = _layer_weights(full, gains, 0)
        elif layer % 2 == 0:
            h, sv_mix, _ = _even_fwd(h, mix_w, tabs, slopes, f"l{layer}_mix")
        else:
            h, sv_mix = _odd_fwd(h, mix_w, tabs, f"l{layer}_mix")
        h, sv_ffn = _ffn_fwd(h, ffn_w, f"l{layer}_ffn")
        saved.append((mix_w, ffn_w, sv_mix, sv_ffn))
    loss_local, dh, d_final = _final_loss(h, final_norm, loss_target[0], name="final_loss")

    gfull = {n: [None] * wts[n].shape[0] for n in big_names}
    gsmall = {n: [None] * (wts[n].shape[0] if wts[n].ndim > 1 else 1) for n in _SMALL}
    gsmall["final_norm"][0] = d_final
    for layer in reversed(range(DEPTH)):
        mix_w, ffn_w, sv_mix, sv_ffn = saved[layer]
        i = layer // 2
        dh, gf = _ffn_bwd(dh, sv_ffn, ffn_w, f"l{layer}_ffn")
        gsmall["ffn_norm"][layer] = gf["norm"]
        gfull["ffn_w_in"][layer] = gf["w_in_t"]
        gfull["ffn_w_out"][layer] = gf["w_out"]
        if layer % 2 == 0:
            def sending(d_w_out):
                gfull["w_out_ab"][0] = d_w_out
                return _pack_grads(gfull, _REST).astype(jnp.bfloat16)

            dh, gm, got = _even_bwd(dh, sv_mix, mix_w, tabs, slopes, f"l{layer}_mix",
                                    exchange=sending if layer == 0 else None)
            if layer == 0:
                received_rest = got
            gsmall["mix_norm_ab"][i] = gm["norm"]
            gsmall["mla_q_norm"][i] = gm["q_norm"]
            gsmall["mla_kv_norm"][i] = gm["kv_norm"]
            gfull["w_in_ab"][i] = jnp.concatenate([gm["w_a"][:, :IN_A], _group_major(gm["w_b"])], axis=-1)
            gfull["mla_w_uq"][i] = _merge_heads_cols(gm["w_uq"], MLA_HEADS, MLA_NOPE)
            gfull["mla_w_ukv"][i] = _merge_heads_cols(gm["w_ukv"], MLA_HEADS, MLA_NOPE)
            gfull["w_out_ab"][i] = gm["w_out"]
        else:
            dh, gm = _odd_bwd(dh, sv_mix, mix_w, tabs, f"l{layer}_mix")
            gsmall["mix_norm_c"][i] = gm["norm"]
            gsmall["gqa_q_norm"][i] = gm["q_norm"]
            gsmall["gqa_k_norm"][i] = gm["k_norm"]
            gfull["gqa_w_q"][i] = gm["w_q"]
            gfull["gqa_w_kv"][i] = gm["w_kv"]
            gfull["gqa_w_o"][i] = gm["w_o"]
    grad_x = dh[None]

    small_part = _pack_small({n: jnp.stack(gsmall[n]) for n in _SMALL})
    small_bits = jnp.broadcast_to(_to_bits16(small_part)[None], (N_DEV, EXCHANGE_TAIL_ROWS, PACK_COLS))
    received_first = _exchange(jnp.concatenate([_pack_grads(gfull, _FIRST).astype(jnp.bfloat16), small_bits], axis=1))
    big_packs = [_reduce_adamw(got, w_packs[gi], _pack_big(_group_of(big_m, grp)), _pack_big(_group_of(big_v, grp)),
                               name=f"adamw_big{gi}")
                 for gi, (grp, got) in enumerate(zip(groups, (received_first, received_rest)))]

    def widen_c(shard):
        return lax.dynamic_update_slice(jnp.zeros((2, N_DEV * c_cols), _F32), shard, (0, me * c_cols))

    def small_of(src):
        return _pack_small({n: (widen_c(src[n]) if n == "mix_norm_c" else src[n]) for n in _SMALL})

    small_recv = _from_bits16(received_first[:, n_rows[0]:])
    small_packs = _reduce_adamw(small_recv, small_of(wts), small_of(mom), small_of(var), name="adamw_small")

    def outputs_of(which):
        by_group = [_unpack_big(big_packs[gi][which], shapes[gi]) for gi in range(len(groups))]
        small = _unpack_small(small_packs[which])
        res = _stored({n: jnp.concatenate([grp[n] for grp in by_group if n in grp], axis=0) for n in big_names})
        for n in wts:
            if n in big_names:
                continue
            if n == "mix_norm_c":
                res[n] = lax.dynamic_slice(small[n].reshape(2, N_DEV * c_cols), (0, me * c_cols), (2, c_cols))
            else:
                res[n] = small[n].reshape(wts[n].shape)
        return [res[n] for n in wts]

    loss = lax.psum(loss_local, _AXES)
    return (loss, grad_x, *outputs_of(0), *outputs_of(1), *outputs_of(2), *outputs_of(3))
```

```python
import functools

import jax
import jax.numpy as jnp
from jax import lax
from jax.experimental import pallas as pl
from jax.experimental.pallas import tpu as pltpu

D_MODEL = 1024
DEPTH = 4
GRID_W = 64
NORM_EPS = 1e-6
ROPE_THETA = 10000.0
NEG_INF = -1e30
MLA_HEADS = 8
MLA_Q_RANK = 384
MLA_KV_RANK = 256
MLA_NOPE = 64
MLA_ROPE = 32
MLA_V = 64
DIL_PAIRS = ((128, 1), (512, 4), (2048, 16))
DIL_HALF = 64
DIL_SLOTS = 4
DIL_GROUPS = 3
DIL_HEADS = 12
DIL_HEAD_DIM = 64
GQA_HEADS = 16
GQA_KV_HEADS = 4
GQA_HEAD_DIM = 64
FFN_HIDDEN = 2816
IN_A = MLA_Q_RANK + MLA_KV_RANK + MLA_ROPE
IN_A_PAD = 768
IN_B = 3 * DIL_HEADS * DIL_HEAD_DIM
ADAM_LR = 0.001
ADAM_B1 = 0.9
ADAM_B2 = 0.999
ADAM_EPS = 1e-08
ADAM_WD = 0.01
ADAM_STEP = 10

LANES = 128
SUBLANES_16BIT = 16
VMEM_LIMIT_BYTES = 56 * 1024 * 1024

MM_TILE = 1408
ROW_BLOCK = 512
ROT_CHUNK = 32

N_DEV = 8
PACK_COLS = 1024

_MXU = jnp.bfloat16
_ACT = jnp.bfloat16
_F32 = jnp.float32

_AXES = ("x", "y", "c")


def _params(sem):
    return pltpu.CompilerParams(dimension_semantics=sem, vmem_limit_bytes=VMEM_LIMIT_BYTES)


def _pick(n, cap):
    for t in range(cap - cap % LANES, 0, -LANES):
        if n % t == 0:
            return t
    return n


def _rows(m, target):
    t = m
    while t > target and t % 2 == 0:
        t //= 2
    return t


def _matmul(a, b, *, trans_a=False, trans_b=False, res=None, scale=None, out_dtype=_F32, name):
    if trans_a:
        k, m = a.shape
    else:
        m, k = a.shape
    if trans_b:
        n, kb = b.shape
    else:
        kb, n = b.shape
    assert k == kb, (a.shape, b.shape)
    tm, tn, tk = _pick(m, MM_TILE), _pick(n, MM_TILE), _pick(k, MM_TILE)
    nk = k // tk
    dims = (((0 if trans_a else 1,), (1 if trans_b else 0,)), ((), ()))

    def body(*refs):
        if res is None:
            a_ref, b_ref, o_ref, acc = refs
            r_ref = None
        else:
            a_ref, b_ref, r_ref, o_ref, acc = refs
        kk = pl.program_id(2)

        @pl.when(kk == 0)
        def _():
            acc[...] = jnp.zeros_like(acc)

        acc[...] += lax.dot_general(a_ref[...].astype(_MXU), b_ref[...].astype(_MXU), dims,
                                    preferred_element_type=_F32)

        @pl.when(kk == nk - 1)
        def _():
            r = acc[...]
            if scale is not None:
                r = r * scale
            if r_ref is not None:
                r = r + r_ref[...].astype(_F32)
            o_ref[...] = r.astype(out_dtype)

    a_spec = (pl.BlockSpec((tk, tm), lambda i, j, kk: (kk, i)) if trans_a
              else pl.BlockSpec((tm, tk), lambda i, j, kk: (i, kk)))
    b_spec = (pl.BlockSpec((tn, tk), lambda i, j, kk: (j, kk)) if trans_b
              else pl.BlockSpec((tk, tn), lambda i, j, kk: (kk, j)))
    o_spec = pl.BlockSpec((tm, tn), lambda i, j, kk: (i, j))
    in_specs = [a_spec, b_spec] + ([o_spec] if res is not None else [])
    args = (a, b) + ((res,) if res is not None else ())
    return pl.pallas_call(
        body, name=name, grid=(m // tm, n // tn, nk),
        in_specs=in_specs, out_specs=o_spec,
        out_shape=jax.ShapeDtypeStruct((m, n), out_dtype),
        scratch_shapes=[pltpu.VMEM((tm, tn), _F32)],
        compiler_params=_params(("parallel", "parallel", "arbitrary")),
    )(*args)


def _rmsnorm(x, g, *, out_dtype, name, rows=ROW_BLOCK):
    m, d = x.shape
    tm = _rows(m, rows)

    def body(x_ref, g_ref, o_ref):
        xf = x_ref[...].astype(_F32)
        r = lax.rsqrt(jnp.mean(xf * xf, axis=-1, keepdims=True) + NORM_EPS)
        o_ref[...] = ((xf * r) * g_ref[...]).astype(out_dtype)

    return pl.pallas_call(
        body, name=name, grid=(m // tm,),
        in_specs=[pl.BlockSpec((tm, d), lambda i: (i, 0)), pl.BlockSpec((1, d), lambda i: (0, 0))],
        out_specs=pl.BlockSpec((tm, d), lambda i: (i, 0)),
        out_shape=jax.ShapeDtypeStruct((m, d), out_dtype),
        compiler_params=_params(("parallel",)),
    )(x, g.reshape(1, d).astype(_F32))


def _rmsnorm_bwd(x, g, dy, dres=None, *, out_dtype=_F32, name, rows=ROW_BLOCK):
    m, d = x.shape
    tm = _rows(m, rows)

    def body(*refs):
        if dres is None:
            x_ref, g_ref, dy_ref, dx_ref, dg_ref = refs
            r_ref = None
        else:
            x_ref, g_ref, dy_ref, r_ref, dx_ref, dg_ref = refs

        @pl.when(pl.program_id(0) == 0)
        def _():
            dg_ref[...] = jnp.zeros_like(dg_ref)

        xf = x_ref[...].astype(_F32)
        r = lax.rsqrt(jnp.mean(xf * xf, axis=-1, keepdims=True) + NORM_EPS)
        xh = xf * r
        dyf = dy_ref[...].astype(_F32)
        dg_ref[...] += jnp.sum(dyf * xh, axis=0, keepdims=True)
        gdy = dyf * g_ref[...]
        dx = r * (gdy - xh * jnp.mean(gdy * xh, axis=-1, keepdims=True))
        if r_ref is not None:
            dx = dx + r_ref[...].astype(_F32)
        dx_ref[...] = dx.astype(out_dtype)

    row = pl.BlockSpec((tm, d), lambda i: (i, 0))
    vec = pl.BlockSpec((1, d), lambda i: (0, 0))
    in_specs = [row, vec, row] + ([row] if dres is not None else [])
    args = (x, g.reshape(1, d).astype(_F32), dy) + ((dres,) if dres is not None else ())
    dx, dg = pl.pallas_call(
        body, name=name, grid=(m // tm,),
        in_specs=in_specs, out_specs=[row, vec],
        out_shape=[jax.ShapeDtypeStruct((m, d), out_dtype), jax.ShapeDtypeStruct((1, d), _F32)],
        compiler_params=_params(("arbitrary",)),
    )(*args)
    return dx, dg.reshape(d)


def _rotate(xf, c, sn):
    w = xf.shape[1]
    half = ROT_CHUNK // 2
    if w > LANES:
        c, sn = jnp.tile(c, (1, w // LANES)), jnp.tile(sn, (1, w // LANES))
    lane = lax.broadcasted_iota(jnp.int32, xf.shape, 1)
    sw = jnp.where((lane & (ROT_CHUNK - 1)) < half, pltpu.roll(xf, w - half, 1), pltpu.roll(xf, half, 1))
    return xf * c + sw * sn


def _seg_mean(v, seg_ref):
    outs = []
    for c in range(v.shape[1] // LANES):
        piece = v[:, c * LANES:(c + 1) * LANES]
        hi = piece.astype(jnp.bfloat16)
        lo = (piece - hi.astype(_F32)).astype(jnp.bfloat16)
        outs.append(jnp.dot(hi, seg_ref[...], preferred_element_type=_F32)
                    + jnp.dot(lo, seg_ref[...], preferred_element_type=_F32))
    return jnp.concatenate(outs, axis=1) if len(outs) > 1 else outs[0]


def _seg_matrix():
    lane = jnp.arange(LANES) // GQA_HEAD_DIM
    return ((lane[:, None] == lane[None, :]).astype(_F32) / GQA_HEAD_DIM).astype(jnp.bfloat16)


def _headnorm_rope(x, gain, cos_t, sin_t, *, scale=None, name):
    s, w = x.shape
    ts = _rows(s, ROW_BLOCK)

    def body(x_ref, g_ref, seg_ref, c_ref, s_ref, o_ref):
        xf = x_ref[...]
        r = lax.rsqrt(_seg_mean(xf * xf, seg_ref) + NORM_EPS)
        y = _rotate((xf * r) * g_ref[...], c_ref[...], s_ref[...])
        if scale is not None:
            y = y * scale
        o_ref[...] = y.astype(o_ref.dtype)

    row = pl.BlockSpec((ts, w), lambda i: (i, 0))
    tab = pl.BlockSpec((ts, LANES), lambda i: (i, 0))
    return pl.pallas_call(
        body, name=name, grid=(s // ts,),
        in_specs=[row, pl.BlockSpec((1, w), lambda i: (0, 0)), pl.BlockSpec((LANES, LANES), lambda i: (0, 0)), tab, tab],
        out_specs=row, out_shape=jax.ShapeDtypeStruct((s, w), _ACT),
        compiler_params=_params(("parallel",)),
    )(x, jnp.tile(gain.astype(_F32), w // GQA_HEAD_DIM).reshape(1, w), _seg_matrix(), cos_t, sin_t)


def _headnorm_rope_bwd(x, gain, dy, cos_t, sin_t, *, name):
    s, w = x.shape
    ts = _rows(s, ROW_BLOCK)

    def body(x_ref, g_ref, seg_ref, c_ref, s_ref, dy_ref, dx_ref, dg_ref):
        @pl.when(pl.program_id(0) == 0)
        def _():
            dg_ref[...] = jnp.zeros_like(dg_ref)

        xf = x_ref[...]
        r = lax.rsqrt(_seg_mean(xf * xf, seg_ref) + NORM_EPS)
        xh = xf * r
        dyn = _rotate(dy_ref[...].astype(_F32), c_ref[...], -s_ref[...])
        dg_ref[...] += jnp.sum(dyn * xh, axis=0, keepdims=True)
        gdy = dyn * g_ref[...]
        dx_ref[...] = (r * (gdy - xh * _seg_mean(gdy * xh, seg_ref))).astype(dx_ref.dtype)

    row = pl.BlockSpec((ts, w), lambda i: (i, 0))
    vec = pl.BlockSpec((1, w), lambda i: (0, 0))
    tab = pl.BlockSpec((ts, LANES), lambda i: (i, 0))
    dx, dg = pl.pallas_call(
        body, name=name, grid=(s // ts,),
        in_specs=[row, vec, pl.BlockSpec((LANES, LANES), lambda i: (0, 0)), tab, tab, row],
        out_specs=[row, vec],
        out_shape=[jax.ShapeDtypeStruct((s, w), _ACT), jax.ShapeDtypeStruct((1, w), _F32)],
        compiler_params=_params(("arbitrary",)),
    )(x, jnp.tile(gain.astype(_F32), w // GQA_HEAD_DIM).reshape(1, w), _seg_matrix(), cos_t, sin_t, dy)
    return dx, dg.reshape(w // GQA_HEAD_DIM, GQA_HEAD_DIM).sum(axis=0)


def _rope(x, cos_t, sin_t, *, out_dtype, name, sum_chunks=False, scale=None):
    s, w = x.shape
    assert w % LANES == 0
    ts = _rows(s, ROW_BLOCK)
    ow = LANES if sum_chunks else w

    def body(x_ref, c_ref, s_ref, o_ref):
        y = _rotate(x_ref[...].astype(_F32), c_ref[...], s_ref[...])
        if scale is not None:
            y = y * scale
        if sum_chunks:
            shift = w // 2
            while shift >= ROT_CHUNK:
                y = y + pltpu.roll(y, shift, 1)
                shift //= 2
            y = y[:, :LANES]
        o_ref[...] = y.astype(out_dtype)

    return pl.pallas_call(
        body, name=name, grid=(s // ts,),
        in_specs=[pl.BlockSpec((ts, w), lambda i: (i, 0)), pl.BlockSpec((ts, LANES), lambda i: (i, 0)),
                  pl.BlockSpec((ts, LANES), lambda i: (i, 0))],
        out_specs=pl.BlockSpec((ts, ow), lambda i: (i, 0)),
        out_shape=jax.ShapeDtypeStruct((s, ow), out_dtype),
        compiler_params=_params(("parallel",)),
    )(x, cos_t, sin_t)


_NT = (((1,), (1,)), ((), ()))
_TN = (((0,), (0,)), ((), ()))
LOG2_E = 1.4426950408889634
LN_2 = 0.6931471805599453
ATTN_FWD_ROWS = 2048
ATTN_ROWS = 512
ATTN_CHAINS = 2
ATTN_FWD_KEYS = 1024
ATTN_BWD_KEYS = 4096


def _grid_step(dims):
    step, total = 0, 1
    for axis, n in enumerate(dims):
        step = step * n + pl.program_id(axis)
        total *= n
    return step, total


def _attn_fwd(qt, k, vt, *, out_dtype, name, gather=None):
    h, dk, s = qt.shape
    g, nk, dv, tk = vt.shape
    assert nk * tk == s
    r = h // g
    rc = min(ATTN_FWD_ROWS, s)
    hp = min(r, ATTN_CHAINS)
    nrc = max(1, min(ATTN_CHAINS // hp, s // rc))
    tq = rc * nrc
    nhp, nq = r // hp, s // tq
    units = [(a, c) for a in range(hp) for c in range(nrc)]

    def body(q_ref, k_ref, v_ref, *rest):
        if gather is None:
            o_ref, lse_ref = rest
        else:
            x_ref, o_ref, lse_ref, all_ref, send_sems, recv_sems, local_sem = rest
            start, forward, finish = _gather_phases(x_ref, all_ref, send_sems, recv_sems, local_sem)
            step, total = _grid_step((g, nhp, nq))
            pl.when(step == 0)(start)
            pl.when(step == 3 * total // 4)(forward)

        qs = [q_ref[a, :, c * rc:(c + 1) * rc] for a, c in units]
        init = tuple((jnp.full((1, rc), NEG_INF, _F32), jnp.zeros((1, rc), _F32), jnp.zeros((dv, rc), _F32))
                     for _ in units)

        def trip(j, carry):
            kb, vb = k_ref[0, pl.ds(pl.multiple_of(j * tk, tk), tk), :], v_ref[0, j]
            out = []
            for u in range(len(units)):
                m, l, acc = carry[u]
                sc = jnp.dot(kb, qs[u], preferred_element_type=_F32)
                m_new = jnp.maximum(m, jnp.max(sc, axis=0, keepdims=True))
                p = jnp.exp2(sc - m_new)
                alpha = jnp.exp2(m - m_new)
                l = alpha * l + jnp.sum(p, axis=0, keepdims=True)
                acc = alpha * acc + jnp.dot(vb, p.astype(_MXU), preferred_element_type=_F32)
                out.append((m_new, l, acc))
            return tuple(out)

        fin = lax.fori_loop(0, nk, trip, init)
        for u, (a, c) in enumerate(units):
            m, l, acc = fin[u]
            o_ref[a, :, c * rc:(c + 1) * rc] = (acc / l).astype(out_dtype)
            lse_ref[a, :, c * rc:(c + 1) * rc] = m + jnp.log2(l)

        if gather is not None:
            pl.when(step == total - 1)(finish)

    q_blk = lambda gg, hh, i: (gg * nhp + hh, 0, i)
    in_specs = [pl.BlockSpec((hp, dk, tq), q_blk), pl.BlockSpec((1, s, dk), lambda gg, hh, i: (gg, 0, 0)),
                pl.BlockSpec((1, nk, dv, tk), lambda gg, hh, i: (gg, 0, 0, 0))]
    out_specs = [pl.BlockSpec((hp, dv, tq), q_blk), pl.BlockSpec((hp, 1, tq), q_blk)]
    out_shape = [jax.ShapeDtypeStruct((h, dv, s), out_dtype), jax.ShapeDtypeStruct((h, 1, s), _F32)]
    if gather is None:
        return pl.pallas_call(
            body, name=name, grid=(g, nhp, nq), in_specs=in_specs, out_specs=out_specs, out_shape=out_shape,
            compiler_params=_params(("parallel", "parallel", "parallel")),
        )(qt, k, vt)
    return pl.pallas_call(
        body, name=name, grid=(g, nhp, nq),
        in_specs=in_specs + [pl.BlockSpec(memory_space=pl.ANY)],
        out_specs=out_specs + [pl.BlockSpec(memory_space=pl.ANY)],
        out_shape=out_shape + [jax.ShapeDtypeStruct((N_DEV,) + gather.shape, gather.dtype)],
        scratch_shapes=_comm_scratch(),
        compiler_params=_params(("arbitrary", "arbitrary", "arbitrary")),
    )(qt, k, vt, gather)


def _attn_bwd(qt, k, v, ot, do, lse2, *, scale, name, exchange=None):
    h, dk, s = qt.shape
    g, _, dv = v.shape
    r = h // g
    hp = min(r, ATTN_CHAINS)
    tq = min(ATTN_ROWS // hp, s)
    tk = min(ATTN_BWD_KEYS, s)
    nhp, nq, nk = r // hp, s // tq, s // tk

    def body(qt_ref, k_ref, v_ref, ot_ref, do_ref, lse_ref, *rest):
        if exchange is None:
            dq_ref, dk_ref, dv_ref = rest
        else:
            g_ref, dq_ref, dk_ref, dv_ref, got_ref, send_sems, recv_sems, local_sem = rest
            start, finish = _exchange_phases(g_ref, got_ref, send_sems, recv_sems, local_sem)
            step, total = _grid_step((g, nhp, nq))
            pl.when(step == 0)(start)
        hh, i = pl.program_id(1), pl.program_id(2)

        @pl.when((hh == 0) & (i == 0))
        def _():
            dk_ref[...] = jnp.zeros_like(dk_ref)
            dv_ref[...] = jnp.zeros_like(dv_ref)

        qts = [qt_ref[a] for a in range(hp)]
        qs = [qt_ref[a].T for a in range(hp)]
        dos = [do_ref[a] for a in range(hp)]
        dots = [do_ref[a].T for a in range(hp)]
        ls = [lse_ref[a].T for a in range(hp)]
        dls = [jnp.sum(do_ref[a].astype(_F32) * ot_ref[a].T.astype(_F32), axis=1, keepdims=True)
               for a in range(hp)]

        def trip(j, carry):
            rows = pl.ds(pl.multiple_of(j * tk, tk), tk)
            kb, vb = k_ref[0, rows, :], v_ref[0, rows, :]
            out = []
            for a in range(hp):
                p = jnp.exp2(lax.dot_general(qs[a], kb, _NT, preferred_element_type=_F32) - ls[a])
                dp = lax.dot_general(dos[a], vb, _NT, preferred_element_type=_F32)
                ds = (p * (dp - dls[a])).astype(_MXU)
                dv_ref[0, j] += jnp.dot(dots[a], p.astype(_MXU), preferred_element_type=_F32)
                dk_ref[0, j] += jnp.dot(qts[a], ds, preferred_element_type=_F32)
                out.append(carry[a] + jnp.dot(ds, kb, preferred_element_type=_F32))
            return tuple(out)

        fin = lax.fori_loop(0, nk, trip, tuple(jnp.zeros((tq, dk), _F32) for _ in range(hp)))
        for a in range(hp):
            dq_ref[a] = (fin[a] * scale).astype(dq_ref.dtype)

        @pl.when((hh == nhp - 1) & (i == nq - 1))
        def _():
            dk_ref[...] = dk_ref[...] * LN_2

        if exchange is not None:
            pl.when(step == total - 1)(finish)

    q_blk = lambda gg, hh, i: (gg * nhp + hh, i, 0)
    qt_blk = lambda gg, hh, i: (gg * nhp + hh, 0, i)
    kv_blk = lambda gg, hh, i: (gg, 0, 0)
    in_specs = [pl.BlockSpec((hp, dk, tq), qt_blk), pl.BlockSpec((1, s, dk), kv_blk), pl.BlockSpec((1, s, dv), kv_blk),
                pl.BlockSpec((hp, dv, tq), qt_blk), pl.BlockSpec((hp, tq, dv), q_blk), pl.BlockSpec((hp, 1, tq), qt_blk)]
    out_specs = [pl.BlockSpec((hp, tq, dk), q_blk), pl.BlockSpec((1, nk, dk, tk), lambda gg, hh, i: (gg, 0, 0, 0)),
                 pl.BlockSpec((1, nk, dv, tk), lambda gg, hh, i: (gg, 0, 0, 0))]
    out_shape = [jax.ShapeDtypeStruct((h, s, dk), _ACT), jax.ShapeDtypeStruct((g, nk, dk, tk), _F32),
                 jax.ShapeDtypeStruct((g, nk, dv, tk), _F32)]
    args = (qt, k, v, ot, do, lse2)
    scratch = []
    if exchange is not None:
        in_specs, args = in_specs + [pl.BlockSpec(memory_space=pl.ANY)], args + (exchange,)
        out_specs = out_specs + [pl.BlockSpec(memory_space=pl.ANY)]
        out_shape = out_shape + [jax.ShapeDtypeStruct(exchange.shape, exchange.dtype)]
        scratch = _comm_scratch()
    return pl.pallas_call(
        body, name=name, grid=(g, nhp, nq), in_specs=in_specs, out_specs=out_specs, out_shape=out_shape,
        scratch_shapes=scratch, compiler_params=_params(("arbitrary", "arbitrary", "arbitrary")),
    )(*args)


def _unchunk(xt):
    g, nk, d, tk = xt.shape
    return xt.transpose(1, 3, 0, 2).reshape(nk * tk, g * d)


def _chunk_t(x2d, g):
    s = x2d.shape[0]
    tk = min(ATTN_FWD_KEYS, s)
    return x2d.reshape(s // tk, tk, g, -1).transpose(2, 0, 3, 1)


def _heads_t(x2d, h):
    s = x2d.shape[0]
    return x2d.reshape(s, h, -1).transpose(1, 2, 0)


def _unheads_t(xt):
    h, d, s = xt.shape
    return xt.transpose(2, 0, 1).reshape(s, h * d)


WIN_ROWS = 512


def _win_geometry(s, dil):
    t = min(WIN_ROWS, s)
    length = s // dil
    lg = length.bit_length() - 1
    assert 1 << lg == length and t % DIL_HALF == 0 and s % t == 0
    return t, t + 2 * DIL_HALF, lg, t // DIL_HALF, s // DIL_HALF


def _win_specs(hn, t, d, halo_per_blk, n_halo, part=0):
    return [pl.BlockSpec((hn, DIL_HALF, d), lambda i: (part, jnp.maximum(i * halo_per_blk - 1, 0), 0)),
            pl.BlockSpec((hn, t, d), lambda i: (part, i, 0)),
            pl.BlockSpec((hn, DIL_HALF, d), lambda i: (part, jnp.minimum((i + 1) * halo_per_blk, n_halo - 1), 0))]


def _win_mask(i, t, w, lg, dil, wide_rows):
    shape = (w, t) if wide_rows else (t, w)
    rows = lax.broadcasted_iota(jnp.int32, shape, 0)
    cols = lax.broadcasted_iota(jnp.int32, shape, 1)
    base = i * t
    if wide_rows:
        pq, pk = base - DIL_HALF + rows, base + cols
    else:
        pq, pk = base + rows, base - DIL_HALF + cols
    arel = jnp.abs(pk - pq)
    valid = (arel <= DIL_HALF) & ((pk >> lg) == (pq >> lg))
    return valid, (-dil * arel).astype(_F32)


def _win3(lo_ref, mid_ref, hi_ref, a):
    return jnp.concatenate([lo_ref[a], mid_ref[a], hi_ref[a]], axis=0)


def _win_fwd(qkv, slopes, *, dil, name):
    hn, s, d = qkv.shape[0] // 3, qkv.shape[1], qkv.shape[2]
    t, w, lg, hpb, n_halo = _win_geometry(s, dil)
    scale = d ** -0.5

    def body(sl_ref, q_ref, klo, kmid, khi, vlo, vmid, vhi, o_ref, lse_ref):
        valid, nb = _win_mask(pl.program_id(0), t, w, lg, dil, False)
        for a in range(hn):
            kw, vw = _win3(klo, kmid, khi, a), _win3(vlo, vmid, vhi, a)
            sc = lax.dot_general(q_ref[a], kw, _NT, preferred_element_type=_F32) * scale
            sc = jnp.where(valid, sc + sl_ref[a] * nb, NEG_INF)
            m = jnp.max(sc, axis=1, keepdims=True)
            e = jnp.exp(sc - m)
            den = jnp.sum(e, axis=1, keepdims=True)
            o_ref[a] = jnp.dot(e.astype(_MXU), vw, preferred_element_type=_F32) / den
            lse_ref[a] = m + jnp.log(den)

    blk = lambda c: pl.BlockSpec((hn, t, c), lambda i: (0, i, 0))
    return pl.pallas_call(
        body, name=name, grid=(s // t,),
        in_specs=([pl.BlockSpec(memory_space=pltpu.SMEM), blk(d)] + _win_specs(hn, t, d, hpb, n_halo, 1)
                  + _win_specs(hn, t, d, hpb, n_halo, 2)),
        out_specs=[blk(d), blk(1)],
        out_shape=[jax.ShapeDtypeStruct((hn, s, d), _F32), jax.ShapeDtypeStruct((hn, s, 1), _F32)],
        compiler_params=_params(("parallel",)),
    )(slopes.astype(_F32), qkv, qkv, qkv, qkv, qkv, qkv, qkv)


def _win_bwd_dq(qkv, do, lse, delta, slopes, *, dil, name):
    hn, s, d = qkv.shape[0] // 3, qkv.shape[1], qkv.shape[2]
    t, w, lg, hpb, n_halo = _win_geometry(s, dil)
    scale = d ** -0.5

    def body(sl_ref, q_ref, klo, kmid, khi, vlo, vmid, vhi, do_ref, lse_ref, dl_ref, dq_ref):
        valid, nb = _win_mask(pl.program_id(0), t, w, lg, dil, False)
        for a in range(hn):
            kw, vw = _win3(klo, kmid, khi, a), _win3(vlo, vmid, vhi, a)
            sc = lax.dot_general(q_ref[a], kw, _NT, preferred_element_type=_F32) * scale
            p = jnp.exp(jnp.where(valid, sc + sl_ref[a] * nb, NEG_INF) - lse_ref[a])
            dp = lax.dot_general(do_ref[a], vw, _NT, preferred_element_type=_F32)
            ds = (p * (dp - dl_ref[a])).astype(_MXU)
            dq_ref[a] = (jnp.dot(ds, kw, preferred_element_type=_F32) * scale).astype(dq_ref.dtype)

    blk = lambda c: pl.BlockSpec((hn, t, c), lambda i: (0, i, 0))
    return pl.pallas_call(
        body, name=name, grid=(s // t,),
        in_specs=([pl.BlockSpec(memory_space=pltpu.SMEM), blk(d)] + _win_specs(hn, t, d, hpb, n_halo, 1)
                  + _win_specs(hn, t, d, hpb, n_halo, 2) + [blk(d), blk(1), blk(1)]),
        out_specs=blk(d),
        out_shape=jax.ShapeDtypeStruct((hn, s, d), _ACT),
        compiler_params=_params(("parallel",)),
    )(slopes.astype(_F32), qkv, qkv, qkv, qkv, qkv, qkv, qkv, do, lse, delta)


def _win_bwd_dkv(qkv, do, lse, delta, slopes, *, dil, name):
    hn, s, d = qkv.shape[0] // 3, qkv.shape[1], qkv.shape[2]
    t, w, lg, hpb, n_halo = _win_geometry(s, dil)
    scale = d ** -0.5

    def body(sl_ref, qlo, qmid, qhi, dolo, domid, dohi, llo, lmid, lhi, dllo, dlmid, dlhi, k_ref, v_ref, dkv_ref):
        valid, nb = _win_mask(pl.program_id(0), t, w, lg, dil, True)
        for a in range(hn):
            qw, dow = _win3(qlo, qmid, qhi, a), _win3(dolo, domid, dohi, a)
            lw, dlw = _win3(llo, lmid, lhi, a), _win3(dllo, dlmid, dlhi, a)
            sc = lax.dot_general(qw, k_ref[a], _NT, preferred_element_type=_F32) * scale
            p = jnp.exp(jnp.where(valid, sc + sl_ref[a] * nb, NEG_INF) - lw)
            dp = lax.dot_general(dow, v_ref[a], _NT, preferred_element_type=_F32)
            ds = (p * (dp - dlw)).astype(_MXU)
            dkv_ref[hn + a] = lax.dot_general(p.astype(_MXU), dow, _TN,
                                              preferred_element_type=_F32).astype(dkv_ref.dtype)
            dkv_ref[a] = (lax.dot_general(ds, qw, _TN, preferred_element_type=_F32) * scale).astype(dkv_ref.dtype)

    part = lambda which: pl.BlockSpec((hn, t, d), lambda i: (which, i, 0))
    return pl.pallas_call(
        body, name=name, grid=(s // t,),
        in_specs=([pl.BlockSpec(memory_space=pltpu.SMEM)] + _win_specs(hn, t, d, hpb, n_halo)
                  + _win_specs(hn, t, d, hpb, n_halo) + _win_specs(hn, t, 1, hpb, n_halo)
                  + _win_specs(hn, t, 1, hpb, n_halo) + [part(1), part(2)]),
        out_specs=pl.BlockSpec((2 * hn, t, d), lambda i: (0, i, 0)),
        out_shape=jax.ShapeDtypeStruct((2 * hn, s, d), _ACT),
        compiler_params=_params(("parallel",)),
    )(slopes.astype(_F32), qkv, qkv, qkv, do, do, do, lse, lse, lse, delta, delta, delta, qkv, qkv)


def _merge_weights(lses):
    mx = functools.reduce(jnp.maximum, lses)
    es = [jnp.exp(l - mx) for l in lses]
    den = functools.reduce(lambda a, b: a + b, es)
    return [e / den for e in es]


def _merge_fwd(outs, lses, *, name):
    ng = len(outs)
    sl, s, d = outs[0].shape
    t = _rows(s, ROW_BLOCK)

    def body(*refs):
        o_refs, l_refs, c_ref = refs[:ng], refs[ng:2 * ng], refs[2 * ng]
        wts = _merge_weights([r[0] for r in l_refs])
        comb = functools.reduce(lambda a, b: a + b, [w * r[0] for w, r in zip(wts, o_refs)])
        c_ref[0] = comb.astype(c_ref.dtype)

    big = pl.BlockSpec((1, t, d), lambda a, i: (a, i, 0))
    small = pl.BlockSpec((1, t, 1), lambda a, i: (a, i, 0))
    return pl.pallas_call(
        body, name=name, grid=(sl, s // t),
        in_specs=[big] * ng + [small] * ng, out_specs=big,
        out_shape=jax.ShapeDtypeStruct((sl, s, d), _ACT),
        compiler_params=_params(("parallel", "parallel")),
    )(*outs, *lses)


def _merge_bwd(dcomb, outs, lses, *, name):
    ng = len(outs)
    sl, s, d = outs[0].shape
    t = _rows(s, ROW_BLOCK)

    def body(*refs):
        dc_ref, o_refs, l_refs = refs[0], refs[1:1 + ng], refs[1 + ng:1 + 2 * ng]
        do_refs, dl_refs = refs[1 + 2 * ng:1 + 3 * ng], refs[1 + 3 * ng:]
        wts = _merge_weights([r[0] for r in l_refs])
        dc = dc_ref[0].astype(_F32)
        comb = functools.reduce(lambda a, b: a + b, [w * r[0] for w, r in zip(wts, o_refs)])
        dot = jnp.sum(dc * comb, axis=-1, keepdims=True)
        for w, do_ref, dl_ref in zip(wts, do_refs, dl_refs):
            do_ref[0] = (w * dc).astype(do_ref.dtype)
            dl_ref[0] = w * dot

    big = pl.BlockSpec((1, t, d), lambda a, i: (a, i, 0))
    small = pl.BlockSpec((1, t, 1), lambda a, i: (a, i, 0))
    res = pl.pallas_call(
        body, name=name, grid=(sl, s // t),
        in_specs=[big] + [big] * ng + [small] * ng, out_specs=[big] * ng + [small] * ng,
        out_shape=[jax.ShapeDtypeStruct((sl, s, d), _ACT)] * ng + [jax.ShapeDtypeStruct((sl, s, 1), _F32)] * ng,
        compiler_params=_params(("parallel", "parallel")),
    )(dcomb, *outs, *lses)
    return res[:ng], res[ng:]


SWIGLU_ROWS = 1024


def _gate_up(a, w_in_t, *, name):
    m, k = a.shape
    n = w_in_t.shape[0] // 2
    tm, tn, tk = _pick(m, SWIGLU_ROWS), _pick(n, MM_TILE), _pick(k, MM_TILE)
    nk = k // tk

    def body(a_ref, g_ref, u_ref, go_ref, uo_ref, act_ref, acc_g, acc_u):
        kk = pl.program_id(2)

        @pl.when(kk == 0)
        def _():
            acc_g[...] = jnp.zeros_like(acc_g)
            acc_u[...] = jnp.zeros_like(acc_u)

        av = a_ref[...].astype(_MXU)
        acc_g[...] += lax.dot_general(av, g_ref[...].astype(_MXU), _NT, preferred_element_type=_F32)
        acc_u[...] += lax.dot_general(av, u_ref[...].astype(_MXU), _NT, preferred_element_type=_F32)

        @pl.when(kk == nk - 1)
        def _():
            gf, uf = acc_g[...], acc_u[...]
            go_ref[...] = gf.astype(go_ref.dtype)
            uo_ref[...] = uf.astype(uo_ref.dtype)
            act_ref[...] = (gf * jax.nn.sigmoid(gf) * uf).astype(act_ref.dtype)

    o_spec = pl.BlockSpec((tm, tn), lambda i, j, kk: (i, j))
    out = jax.ShapeDtypeStruct((m, n), _ACT)
    return pl.pallas_call(
        body, name=name, grid=(m // tm, n // tn, nk),
        in_specs=[pl.BlockSpec((tm, tk), lambda i, j, kk: (i, kk)),
                  pl.BlockSpec((tn, tk), lambda i, j, kk: (j, kk)),
                  pl.BlockSpec((tn, tk), lambda i, j, kk: (j + n // tn, kk))],
        out_specs=[o_spec, o_spec, o_spec], out_shape=[out, out, out],
        scratch_shapes=[pltpu.VMEM((tm, tn), _F32), pltpu.VMEM((tm, tn), _F32)],
        compiler_params=_params(("parallel", "parallel", "arbitrary")),
    )(a, w_in_t, w_in_t)


def _gate_up_bwd(dout, w_out, gate, up, *, name):
    m, k = dout.shape
    n = w_out.shape[0]
    tm, tn, tk = _pick(m, SWIGLU_ROWS), _pick(n, MM_TILE), _pick(k, MM_TILE)
    nk = k // tk

    def body(d_ref, w_ref, g_ref, u_ref, dg_ref, du_ref, acc):
        kk = pl.program_id(2)

        @pl.when(kk == 0)
        def _():
            acc[...] = jnp.zeros_like(acc)

        acc[...] += lax.dot_general(d_ref[...].astype(_MXU), w_ref[...].astype(_MXU), _NT,
                                    preferred_element_type=_F32)

        @pl.when(kk == nk - 1)
        def _():
            daf = acc[...]
            gf = g_ref[...].astype(_F32)
            sg = jax.nn.sigmoid(gf)
            dg_ref[...] = (daf * u_ref[...].astype(_F32) * (sg + gf * sg * (1.0 - sg))).astype(dg_ref.dtype)
            du_ref[...] = (daf * (gf * sg)).astype(du_ref.dtype)

    o_spec = pl.BlockSpec((tm, tn), lambda i, j, kk: (i, j))
    out = jax.ShapeDtypeStruct((m, n), _ACT)
    return pl.pallas_call(
        body, name=name, grid=(m // tm, n // tn, nk),
        in_specs=[pl.BlockSpec((tm, tk), lambda i, j, kk: (i, kk)), pl.BlockSpec((tn, tk), lambda i, j, kk: (j, kk)),
                  o_spec, o_spec],
        out_specs=[o_spec, o_spec], out_shape=[out, out],
        scratch_shapes=[pltpu.VMEM((tm, tn), _F32)],
        compiler_params=_params(("parallel", "parallel", "arbitrary")),
    )(dout, w_out, gate, up)


def _final_loss(x, g, target, *, name):
    m, d = x.shape
    tm = _rows(m, ROW_BLOCK)

    def body(x_ref, g_ref, t_ref, loss_ref, dx_ref, dg_ref):
        @pl.when(pl.program_id(0) == 0)
        def _():
            loss_ref[...] = jnp.zeros_like(loss_ref)
            dg_ref[...] = jnp.zeros_like(dg_ref)

        xf = x_ref[...]
        r = lax.rsqrt(jnp.mean(xf * xf, axis=-1, keepdims=True) + NORM_EPS)
        xh = xf * r
        err = xh * g_ref[...] - t_ref[...]
        loss_ref[...] += 0.5 * jnp.sum(jnp.mean(err * err, axis=-1, keepdims=True))
        dy = err * (1.0 / d)
        dg_ref[...] += jnp.sum(dy * xh, axis=0, keepdims=True)
        gdy = dy * g_ref[...]
        dx_ref[...] = r * (gdy - xh * jnp.mean(gdy * xh, axis=-1, keepdims=True))

    row = pl.BlockSpec((tm, d), lambda i: (i, 0))
    vec = pl.BlockSpec((1, d), lambda i: (0, 0))
    loss, dx, dg = pl.pallas_call(
        body, name=name, grid=(m // tm,),
        in_specs=[row, vec, row],
        out_specs=[pl.BlockSpec((8, LANES), lambda i: (0, 0)), row, vec],
        out_shape=[jax.ShapeDtypeStruct((8, LANES), _F32), jax.ShapeDtypeStruct((m, d), _F32),
                   jax.ShapeDtypeStruct((1, d), _F32)],
        compiler_params=_params(("arbitrary",)),
    )(x, g.reshape(1, d), target)
    return loss[0, 0], dx, dg.reshape(d)


def _reduce_adamw(parts, w, m, v, *, name):
    rws, cols = w.shape
    tr = rws
    for cand in range(min(rws, 256), 0, -SUBLANES_16BIT):
        if cand % SUBLANES_16BIT == 0 and rws % cand == 0:
            tr = cand
            break

    def body(p_ref, w_ref, m_ref, v_ref, g_ref, d_ref, nm_ref, nv_ref):
        gsum = p_ref[0].astype(_F32)
        for dev in range(1, N_DEV):
            gsum = gsum + p_ref[dev].astype(_F32)
        m2 = ADAM_B1 * m_ref[...] + (1.0 - ADAM_B1) * gsum
        v2 = ADAM_B2 * v_ref[...] + (1.0 - ADAM_B2) * (gsum * gsum)
        m_hat = m2 / (1.0 - ADAM_B1 ** ADAM_STEP)
        v_hat = v2 / (1.0 - ADAM_B2 ** ADAM_STEP)
        g_ref[...] = gsum
        d_ref[...] = -ADAM_LR * (m_hat / (jnp.sqrt(v_hat) + ADAM_EPS) + ADAM_WD * w_ref[...])
        nm_ref[...] = m2
        nv_ref[...] = v2

    blk = pl.BlockSpec((tr, cols), lambda i: (i, 0))
    out = jax.ShapeDtypeStruct((rws, cols), _F32)
    return pl.pallas_call(
        body, name=name, grid=(rws // tr,),
        in_specs=[pl.BlockSpec((N_DEV, tr, cols), lambda i: (0, i, 0)), blk, blk, blk],
        out_specs=[blk, blk, blk, blk], out_shape=[out, out, out, out],
        compiler_params=_params(("parallel",)),
    )(parts, w, m, v)


def _mesh_pos():
    return lax.axis_index("x"), lax.axis_index("y"), lax.axis_index("c")


def _all_gather(block):
    rws, cols = block.shape

    def body(x_ref, out_ref, send_sems, recv_sems, local_sem):
        start, forward, finish = _gather_phases(x_ref, out_ref, send_sems, recv_sems, local_sem)
        start()
        forward()
        finish()

    return pl.pallas_call(
        body, name="weights_all_gather",
        out_shape=jax.ShapeDtypeStruct((N_DEV, rws, cols), block.dtype),
        in_specs=[pl.BlockSpec(memory_space=pl.ANY)],
        out_specs=pl.BlockSpec(memory_space=pl.ANY),
        scratch_shapes=_comm_scratch(),
    )(block)


def _comm_scratch():
    return [pltpu.SemaphoreType.DMA((N_DEV - 1,)), pltpu.SemaphoreType.DMA((N_DEV - 1,)), pltpu.SemaphoreType.DMA]


def _gather_phases(x_ref, out_ref, send_sems, recv_sems, local_sem):
    x, y, c = _mesh_pos()
    me, sibling = (x, y, c), (x, y, 1 - c)
    chips = [(1 - x, y), (x, 1 - y), (1 - x, 1 - y)]

    def slot(px, py, pc):
        return out_ref.at[4 * px + 2 * py + pc]

    def copy(k, blk, to, src=None):
        return pltpu.make_async_remote_copy(
            src_ref=slot(*blk) if src is None else src, dst_ref=slot(*blk),
            send_sem=send_sems.at[k], recv_sem=recv_sems.at[k],
            device_id=to, device_id_type=pl.DeviceIdType.MESH)

    mine = pltpu.make_async_copy(x_ref, slot(*me), local_sem)
    first = [copy(0, me, sibling, src=x_ref)]
    first += [copy(1 + j, me, (*chip, c), src=x_ref) for j, chip in enumerate(chips)]
    passed = [copy(4 + j, (*chip, c), sibling) for j, chip in enumerate(chips)]

    def start():
        mine.start()
        for cp in first:
            cp.start()

    def forward():
        for j, chip in enumerate(chips):
            copy(1 + j, (*chip, c), me).wait_recv()
            passed[j].start()

    def finish():
        copy(0, sibling, me).wait_recv()
        for j, chip in enumerate(chips):
            copy(4 + j, (*chip, 1 - c), me).wait_recv()
        for cp in first + passed:
            cp.wait_send()
        mine.wait()

    return start, forward, finish


def _exchange(parts):
    def body(g_ref, out_ref, send_sems, recv_sems, local_sem):
        start, finish = _exchange_phases(g_ref, out_ref, send_sems, recv_sems, local_sem)
        start()
        finish()

    return pl.pallas_call(
        body, name="grads_exchange",
        out_shape=jax.ShapeDtypeStruct(parts.shape, parts.dtype),
        in_specs=[pl.BlockSpec(memory_space=pl.ANY)],
        out_specs=pl.BlockSpec(memory_space=pl.ANY),
        scratch_shapes=_comm_scratch(),
    )(parts)


def _exchange_phases(g_ref, out_ref, send_sems, recv_sems, local_sem):
    x, y, c = _mesh_pos()
    me = 4 * x + 2 * y + c
    mine = pltpu.make_async_copy(g_ref.at[me], out_ref.at[me], local_sem)
    sends, arrivals = [], []
    for k in range(1, N_DEV):
        px = 1 - x if k & 4 else x
        py = 1 - y if k & 2 else y
        pc = 1 - c if k & 1 else c
        peer = 4 * px + 2 * py + pc
        sems = dict(send_sem=send_sems.at[k - 1], recv_sem=recv_sems.at[k - 1],
                    device_id=(px, py, pc), device_id_type=pl.DeviceIdType.MESH)
        sends.append(pltpu.make_async_remote_copy(src_ref=g_ref.at[peer], dst_ref=out_ref.at[me], **sems))
        arrivals.append(pltpu.make_async_remote_copy(src_ref=g_ref.at[peer], dst_ref=out_ref.at[peer], **sems))

    def start():
        mine.start()
        for cp in sends:
            cp.start()

    def finish():
        for cp in arrivals:
            cp.wait_recv()
        for cp in sends:
            cp.wait_send()
        mine.wait()

    return start, finish


_BIG = (("w_in_ab", 2), ("mla_w_uq", 1), ("mla_w_ukv", 1), ("w_out_ab", 2), ("gqa_w_q", 1), ("gqa_w_kv", 1),
        ("gqa_w_o", 1), ("ffn_w_in", 1), ("ffn_w_out", 1))
_TRANSPOSED = ("ffn_w_in",)


def _stored(arrays):
    return {n: (a.transpose(0, 2, 1) if n in _TRANSPOSED else a) for n, a in arrays.items()}
_SMALL = ("mix_norm_ab", "ffn_norm", "final_norm", "mix_norm_c", "mla_q_norm", "mla_kv_norm", "gqa_q_norm", "gqa_k_norm")
SMALL_ROWS = 16
GATHER_TAIL_ROWS = 16
EXCHANGE_TAIL_ROWS = 32


def _pad_rows(a2d):
    pad = -a2d.shape[0] % SUBLANES_16BIT
    return jnp.pad(a2d, ((0, pad), (0, 0))) if pad else a2d


def _pack_rows(shard):
    return _pad_rows(shard.reshape(-1, PACK_COLS))


def _packed_rows(shape):
    n = 1
    for d in shape:
        n *= d
    rows = n // PACK_COLS
    return rows + (-rows % SUBLANES_16BIT)


_FIRST = {"w_in_ab": (0, 1), "mla_w_uq": (0, 1), "mla_w_ukv": (0, 1)}
_REST = {"w_in_ab": (1, 1), "mla_w_uq": (1, 1), "mla_w_ukv": (1, 1), "w_out_ab": (0, 2), "gqa_w_q": (0, 2),
         "gqa_w_kv": (0, 2), "gqa_w_o": (0, 2), "ffn_w_in": (0, 4), "ffn_w_out": (0, 4)}
PACK_ROWS = 128


def _group_of(arrays, group):
    return {n: arrays[n][lo:lo + cnt] for n, (lo, cnt) in group.items()}


def _pack_big(shards):
    parts = [_pack_rows(shards[n]) for n, _ in _BIG if n in shards]
    fill = -sum(p.shape[0] for p in parts) % PACK_ROWS
    return jnp.concatenate(parts + [jnp.zeros((fill, PACK_COLS), parts[0].dtype)], axis=0)


def _unpack_big(packed, shapes):
    out, off = {}, 0
    for n, _ in _BIG:
        if n not in shapes:
            continue
        size = 1
        for d in shapes[n]:
            size *= d
        out[n] = packed[off:off + size // PACK_COLS].reshape(shapes[n])
        off += _packed_rows(shapes[n])
    return out


def _unpack_gathered(gathered, shapes):
    out, off = {}, 0
    for n, axis in _BIG:
        if n not in shapes:
            continue
        size = 1
        for d in shapes[n]:
            size *= d
        l3 = (shapes[n][0], shapes[n][1], size // (shapes[n][0] * shapes[n][1]))
        sh = gathered[:, off:off + size // PACK_COLS].reshape((N_DEV,) + l3)
        if axis == 1:
            full = sh.transpose(1, 0, 2, 3).reshape(l3[0], N_DEV * l3[1], l3[2])
        else:
            full = sh.transpose(1, 2, 0, 3).reshape(l3[0], l3[1], N_DEV * l3[2])
        out[n] = full
        off += _packed_rows(shapes[n])
    return out


def _split_for_devices(layers, axis):
    rws, cols = layers[0].shape
    if axis == 1:
        flat = jnp.concatenate([g.reshape(N_DEV, -1, PACK_COLS) for g in layers], axis=1)
    else:
        sh = jnp.stack(layers).reshape(len(layers), rws, N_DEV, cols // N_DEV).transpose(2, 0, 1, 3)
        flat = sh.reshape(N_DEV, -1, PACK_COLS)
    pad = -flat.shape[1] % SUBLANES_16BIT
    return jnp.pad(flat, ((0, 0), (0, pad), (0, 0))) if pad else flat


def _to_bits16(a_f32_rows):
    r = a_f32_rows.shape[0]
    return lax.bitcast_convert_type(a_f32_rows, jnp.bfloat16).reshape(2 * r, PACK_COLS)


def _from_bits16(a_bf16_rows):
    lead, r = a_bf16_rows.shape[:-2], a_bf16_rows.shape[-2]
    return lax.bitcast_convert_type(a_bf16_rows.reshape(lead + (r // 2, PACK_COLS, 2)), _F32)


def _small_sizes():
    return {"mix_norm_ab": 2 * D_MODEL, "ffn_norm": DEPTH * D_MODEL, "final_norm": D_MODEL, "mix_norm_c": 2 * D_MODEL,
            "mla_q_norm": 2 * MLA_Q_RANK, "mla_kv_norm": 2 * MLA_KV_RANK, "gqa_q_norm": 2 * GQA_HEAD_DIM,
            "gqa_k_norm": 2 * GQA_HEAD_DIM}


def _pack_small(vals):
    flat = jnp.concatenate([vals[n].reshape(-1).astype(_F32) for n in _SMALL])
    return jnp.pad(flat, (0, SMALL_ROWS * PACK_COLS - flat.shape[0])).reshape(SMALL_ROWS, PACK_COLS)


def _unpack_small(pack):
    flat, out, off = pack.reshape(-1), {}, 0
    sizes = _small_sizes()
    for n in _SMALL:
        out[n] = flat[off:off + sizes[n]]
        off += sizes[n]
    return out


def _angles(pos, dim):
    freqs = ROPE_THETA ** (-jnp.arange(0, dim, 2, dtype=_F32) / dim)
    ang = pos.astype(_F32)[:, None] * freqs[None, :]
    return jnp.cos(ang), jnp.sin(ang)


def _rope_tables(s):
    pos = jnp.arange(s)
    cos_t, sin_t = _angles(pos, MLA_ROPE)
    mla_c = jnp.tile(jnp.concatenate([cos_t, cos_t], -1), (1, LANES // ROT_CHUNK))
    mla_s = jnp.tile(jnp.concatenate([-sin_t, sin_t], -1), (1, LANES // ROT_CHUNK))
    rows = s // GRID_W
    row_idx = jnp.broadcast_to(jnp.arange(rows)[:, None], (rows, GRID_W)).reshape(-1)
    col_idx = jnp.broadcast_to(jnp.arange(GRID_W)[None, :], (rows, GRID_W)).reshape(-1)
    cos_r, sin_r = _angles(row_idx, GQA_HEAD_DIM // 2)
    cos_c, sin_c = _angles(col_idx, GQA_HEAD_DIM // 2)
    gqa_c = jnp.tile(jnp.concatenate([cos_r, cos_r, cos_c, cos_c], -1), (1, LANES // GQA_HEAD_DIM))
    gqa_s = jnp.tile(jnp.concatenate([-sin_r, sin_r, -sin_c, sin_c], -1), (1, LANES // GQA_HEAD_DIM))
    return (mla_c, mla_s), (gqa_c, gqa_s)


def _heads(x2d, h):
    s = x2d.shape[0]
    return x2d.reshape(s, h, -1).transpose(1, 0, 2)


def _unheads(xh):
    h, s, d = xh.shape
    return xh.transpose(1, 0, 2).reshape(s, h * d)


def _to_res(x2d, dil):
    s, cols = x2d.shape
    if dil > 1:
        x2d = x2d.reshape(s // dil, dil, cols).transpose(1, 0, 2).reshape(s, cols)
    return x2d.reshape(s, cols // DIL_HEAD_DIM, DIL_HEAD_DIM).transpose(1, 0, 2)


def _from_res(xh, dil):
    heads, s, d = xh.shape
    x2d = xh.transpose(1, 0, 2).reshape(s, heads * d)
    if dil > 1:
        x2d = x2d.reshape(dil, s // dil, heads * d).transpose(1, 0, 2).reshape(s, heads * d)
    return x2d


def _group_major(w_b):
    rws = w_b.shape[0]
    return w_b.reshape(rws, 3, DIL_GROUPS, DIL_SLOTS * DIL_HEAD_DIM).transpose(0, 2, 1, 3).reshape(rws, -1)


def _res_to_tok(xh, dil):
    sl, s, c = xh.shape
    if dil == 1:
        return xh
    return xh.reshape(sl, dil, s // dil, c).transpose(0, 2, 1, 3).reshape(sl, s, c)


def _tok_to_res(xh, dil):
    sl, s, c = xh.shape
    if dil == 1:
        return xh
    return xh.reshape(sl, s // dil, dil, c).transpose(0, 2, 1, 3).reshape(sl, s, c)


def _ffn_fwd(x, w, tag):
    hn = _rmsnorm(x, w["norm"], out_dtype=_ACT, name=f"{tag}_norm")
    gate, up, act = _gate_up(hn, w["w_in_t"], name=f"{tag}_gate_up")
    out = _matmul(act, w["w_out"], res=x, name=f"{tag}_out")
    return out, (x, hn, gate, up, act)


def _ffn_bwd(dout, saved, w, tag):
    x, hn, gate, up, act = saved
    w_gate_t, w_up_t = w["w_in_t"][:FFN_HIDDEN], w["w_in_t"][FFN_HIDDEN:]
    d_w_out = _matmul(act, dout, trans_a=True, name=f"{tag}_dwout")
    dgate, dup = _gate_up_bwd(dout, w["w_out"], gate, up, name=f"{tag}_dgate_up")
    d_w_gate_t = _matmul(dgate, hn, trans_a=True, name=f"{tag}_dwgate")
    d_w_up_t = _matmul(dup, hn, trans_a=True, name=f"{tag}_dwup")
    dhn = _matmul(dgate, w_gate_t, name=f"{tag}_dhn_gate")
    dhn = _matmul(dup, w_up_t, res=dhn, name=f"{tag}_dhn_up")
    dx, dnorm = _rmsnorm_bwd(x, w["norm"], dhn, dout, name=f"{tag}_dnorm")
    return dx, {"norm": dnorm, "w_in_t": jnp.concatenate([d_w_gate_t, d_w_up_t], axis=0), "w_out": d_w_out}


def _even_fwd(x, w, tabs, slopes, tag, gather=None, late=None):
    s = x.shape[0]
    (mla_c, mla_s), _ = tabs
    hn = _rmsnorm(x, w["norm"], out_dtype=_ACT, name=f"{tag}_norm")
    za = _matmul(hn, w["w_a"], name=f"{tag}_in_a")
    zb = _matmul(hn, w["w_b"], out_dtype=_ACT, name=f"{tag}_in_b")
    cq, ckv, kr = za[:, :MLA_Q_RANK], za[:, MLA_Q_RANK:MLA_Q_RANK + MLA_KV_RANK], za[:, MLA_Q_RANK + MLA_KV_RANK:]
    cqn = _rmsnorm(cq, w["q_norm"], out_dtype=_ACT, name=f"{tag}_qnorm")
    ckvn = _rmsnorm(ckv, w["kv_norm"], out_dtype=_ACT, name=f"{tag}_kvnorm")
    q = _matmul(cqn, w["w_uq"], scale=(MLA_NOPE + MLA_ROPE) ** -0.5 * LOG2_E,
                name=f"{tag}_uq")
    kv = _matmul(ckvn, w["w_ukv"], out_dtype=_ACT, name=f"{tag}_ukv")
    nn = MLA_HEADS * MLA_NOPE
    q_rope = _rope(q[:, nn:], mla_c, mla_s, out_dtype=_ACT, name=f"{tag}_rope_q")
    k_rope = _rope(kr, mla_c, mla_s, out_dtype=_ACT, name=f"{tag}_rope_k")[:, :MLA_ROPE]
    qh = jnp.concatenate([_heads_t(q[:, :nn].astype(_ACT), MLA_HEADS), _heads_t(q_rope, MLA_HEADS)], axis=1)
    kh = jnp.concatenate([_heads(kv[:, :nn], MLA_HEADS),
                          jnp.broadcast_to(k_rope[None], (MLA_HEADS, s, MLA_ROPE))], axis=-1)
    vh = _heads(kv[:, nn:], MLA_HEADS)
    oa, lse_a, *gathered = _attn_fwd(qh, kh, _chunk_t(kv[:, nn:], MLA_HEADS), out_dtype=_ACT, name=f"{tag}_mla",
                                     gather=gather)
    if late is not None:
        w = late(gathered[0])

    ng = 3 * DIL_SLOTS * DIL_HEAD_DIM
    outs, lses, dil_saved = [], [], []
    for gi, (_, dil) in enumerate(DIL_PAIRS):
        hs = slice(gi * DIL_SLOTS, (gi + 1) * DIL_SLOTS)
        qkv = _to_res(zb[:, gi * ng:(gi + 1) * ng], dil)
        o, l = _win_fwd(qkv, slopes[hs], dil=dil, name=f"{tag}_dil{gi}")
        dil_saved.append((qkv, l))
        outs.append(_res_to_tok(o, dil))
        lses.append(_res_to_tok(l, dil))
    o3, lse3 = tuple(outs), tuple(lses)
    comb = _merge_fwd(o3, lse3, name=f"{tag}_merge")
    cat = jnp.concatenate([_unheads_t(oa), _unheads(comb)], axis=-1)
    out = _matmul(cat, w["w_out"], res=x, name=f"{tag}_out")
    saved = (x, hn, cq, ckv, cqn, ckvn, qh, kh, vh, oa, lse_a, dil_saved, o3, lse3, cat)
    return out, saved, (gathered[0] if gathered else None)


def _even_bwd(dout, saved, w, tabs, slopes, tag, exchange=None):
    x, hn, cq, ckv, cqn, ckvn, qh, kh, vh, oa, lse_a, dil_saved, o3, lse3, cat = saved
    (mla_c, mla_s), _ = tabs
    nn = MLA_HEADS * MLA_NOPE
    dcat = _matmul(dout, w["w_out"], trans_b=True, out_dtype=_ACT, name=f"{tag}_dcat")
    d_w_out = _matmul(cat, dout, trans_a=True, name=f"{tag}_dwout")
    nv = MLA_HEADS * MLA_V

    doa = _heads(dcat[:, :nv], MLA_HEADS)
    dqh, dkt, dvt, *received = _attn_bwd(qh, kh, vh, oa, doa, lse_a, scale=(MLA_NOPE + MLA_ROPE) ** -0.5,
                                         name=f"{tag}_mla_bwd",
                                         exchange=None if exchange is None else exchange(d_w_out))
    dq_rope = _rope(_unheads(dqh[..., MLA_NOPE:]), mla_c, -mla_s, out_dtype=_ACT, name=f"{tag}_drope_q")
    dq = jnp.concatenate([_unheads(dqh[..., :MLA_NOPE]).astype(_ACT), dq_rope], axis=-1)
    dk3 = _unchunk(dkt).reshape(-1, MLA_HEADS, MLA_NOPE + MLA_ROPE)
    dkr = _rope(dk3[..., MLA_NOPE:].reshape(-1, MLA_HEADS * MLA_ROPE), mla_c, -mla_s, out_dtype=_F32,
                sum_chunks=True, name=f"{tag}_drope_k")
    dkv = jnp.concatenate([dk3[..., :MLA_NOPE].reshape(-1, nn), _unchunk(dvt)], axis=-1).astype(_ACT)
    d_w_uq = _matmul(cqn, dq, trans_a=True, name=f"{tag}_dwuq")
    d_w_ukv = _matmul(ckvn, dkv, trans_a=True, name=f"{tag}_dwukv")
    dcqn = _matmul(dq, w["w_uq"], trans_b=True, name=f"{tag}_dcqn")
    dckvn = _matmul(dkv, w["w_ukv"], trans_b=True, name=f"{tag}_dckvn")
    dcq, d_q_norm = _rmsnorm_bwd(cq, w["q_norm"], dcqn, out_dtype=_ACT, name=f"{tag}_dqnorm")
    dckv, d_kv_norm = _rmsnorm_bwd(ckv, w["kv_norm"], dckvn, out_dtype=_ACT, name=f"{tag}_dkvnorm")
    lane = jnp.arange(LANES) < MLA_ROPE
    dza = jnp.concatenate([dcq, dckv, jnp.where(lane[None], dkr, 0.0).astype(_ACT)], axis=-1)

    dcomb = _heads(dcat[:, nv:], DIL_SLOTS)
    do3, delta3 = _merge_bwd(dcomb, o3, lse3, name=f"{tag}_dmerge")
    dslabs = []
    for gi, (_, dil) in enumerate(DIL_PAIRS):
        hs = slice(gi * DIL_SLOTS, (gi + 1) * DIL_SLOTS)
        qkv, l = dil_saved[gi]
        grads = (qkv, _tok_to_res(do3[gi], dil), l, _tok_to_res(delta3[gi], dil), slopes[hs])
        a = _win_bwd_dq(*grads, dil=dil, name=f"{tag}_dil{gi}_dq")
        bc = _win_bwd_dkv(*grads, dil=dil, name=f"{tag}_dil{gi}_dkv")
        dslabs.append(_from_res(jnp.concatenate([a, bc], axis=0), dil))
    dzb = jnp.concatenate(dslabs, axis=-1)

    d_w_a = _matmul(hn, dza, trans_a=True, name=f"{tag}_dwa")
    d_w_b = _matmul(hn, dzb, trans_a=True, name=f"{tag}_dwb")
    dhn = _matmul(dza, w["w_a"], trans_b=True, name=f"{tag}_dhn_a")
    dhn = _matmul(dzb, w["w_b"], trans_b=True, res=dhn, name=f"{tag}_dhn_b")
    dx, dnorm = _rmsnorm_bwd(x, w["norm"], dhn, dout, name=f"{tag}_dnorm")
    grads = {"norm": dnorm, "w_a": d_w_a, "w_b": d_w_b, "q_norm": d_q_norm, "kv_norm": d_kv_norm,
             "w_uq": d_w_uq, "w_ukv": d_w_ukv, "w_out": d_w_out}
    return dx, grads, (received[0] if received else None)


def _odd_fwd(x, w, tabs, tag):
    s = x.shape[0]
    _, (gqa_c, gqa_s) = tabs
    nk = GQA_KV_HEADS * GQA_HEAD_DIM
    hn = _rmsnorm(x, w["norm"], out_dtype=_ACT, name=f"{tag}_norm")
    q = _matmul(hn, w["w_q"], name=f"{tag}_q")
    kv = _matmul(hn, w["w_kv"], name=f"{tag}_kv")
    k = kv[:, :nk]
    qh = _heads_t(_headnorm_rope(q, w["q_norm"], gqa_c, gqa_s, scale=GQA_HEAD_DIM ** -0.5 * LOG2_E,
                                 name=f"{tag}_prep_q"), GQA_HEADS)
    kh = _heads(_headnorm_rope(k, w["k_norm"], gqa_c, gqa_s, name=f"{tag}_prep_k"), GQA_KV_HEADS)
    v = kv[:, nk:].astype(_ACT)
    vh = _heads(v, GQA_KV_HEADS)
    o, lse = _attn_fwd(qh, kh, _chunk_t(v, GQA_KV_HEADS), out_dtype=_ACT, name=f"{tag}_gqa")
    ocat = _unheads_t(o)
    out = _matmul(ocat, w["w_o"], res=x, name=f"{tag}_out")
    return out, (x, hn, q, k, qh, kh, vh, o, lse, ocat)


def _odd_bwd(dout, saved, w, tabs, tag):
    x, hn, q, k, qh, kh, vh, o, lse, ocat = saved
    s = x.shape[0]
    _, (gqa_c, gqa_s) = tabs
    docat = _matmul(dout, w["w_o"], trans_b=True, out_dtype=_ACT, name=f"{tag}_docat")
    d_w_o = _matmul(ocat, dout, trans_a=True, name=f"{tag}_dwo")
    doh = _heads(docat, GQA_HEADS)
    dqh, dkt, dvt = _attn_bwd(qh, kh, vh, o, doh, lse, scale=GQA_HEAD_DIM ** -0.5, name=f"{tag}_gqa_bwd")
    dq, d_q_norm = _headnorm_rope_bwd(q, w["q_norm"], _unheads(dqh), gqa_c, gqa_s, name=f"{tag}_dprep_q")
    dk, d_k_norm = _headnorm_rope_bwd(k, w["k_norm"], _unchunk(dkt), gqa_c, gqa_s, name=f"{tag}_dprep_k")
    dkv = jnp.concatenate([dk, _unchunk(dvt).astype(_ACT)], axis=-1)
    d_w_q = _matmul(hn, dq, trans_a=True, name=f"{tag}_dwq")
    d_w_kv = _matmul(hn, dkv, trans_a=True, name=f"{tag}_dwkv")
    dhn = _matmul(dq, w["w_q"], trans_b=True, name=f"{tag}_dhn_q")
    dhn = _matmul(dkv, w["w_kv"], trans_b=True, res=dhn, name=f"{tag}_dhn_kv")
    dx, dnorm = _rmsnorm_bwd(x, w["norm"], dhn, dout, name=f"{tag}_dnorm")
    return dx, {"norm": dnorm, "w_q": d_w_q, "w_kv": d_w_kv, "q_norm": d_q_norm, "k_norm": d_k_norm, "w_o": d_w_o}


def _split_heads_cols(wm, heads, first):
    rws = wm.shape[0]
    w3 = wm.reshape(rws, heads, -1)
    return jnp.concatenate([w3[:, :, :first].reshape(rws, -1), w3[:, :, first:].reshape(rws, -1)], axis=-1)


def _merge_heads_cols(wm, heads, first):
    rws, cols = wm.shape
    a = wm[:, :heads * first].reshape(rws, heads, first)
    b = wm[:, heads * first:].reshape(rws, heads, cols // heads - first)
    return jnp.concatenate([a, b], axis=-1).reshape(rws, cols)


def _layer_weights(full, gains, layer):
    i = layer // 2

    def stacked(name, idx):
        for group, arrays in zip((_FIRST, _REST), full):
            lo, cnt = group.get(name, (0, 0))
            if lo <= idx < lo + cnt:
                return None if arrays is None else arrays[name][idx - lo]
        raise KeyError((name, idx))

    ffn = {"norm": gains["ffn_norm"][layer], "w_in_t": stacked("ffn_w_in", layer), "w_out": stacked("ffn_w_out", layer)}
    if layer % 2 == 0:
        w_in = stacked("w_in_ab", i)
        mix = {"norm": gains["mix_norm_ab"][i],
               "w_a": jnp.pad(w_in[:, :IN_A], ((0, 0), (0, IN_A_PAD - IN_A))), "w_b": _group_major(w_in[:, IN_A:]),
               "q_norm": gains["mla_q_norm"][i], "kv_norm": gains["mla_kv_norm"][i],
               "w_uq": _split_heads_cols(stacked("mla_w_uq", i), MLA_HEADS, MLA_NOPE),
               "w_ukv": _split_heads_cols(stacked("mla_w_ukv", i), MLA_HEADS, MLA_NOPE),
               "w_out": stacked("w_out_ab", i)}
    else:
        mix = {"norm": gains["mix_norm_c"][i], "w_q": stacked("gqa_w_q", i), "w_kv": stacked("gqa_w_kv", i),
               "q_norm": gains["gqa_q_norm"][i], "k_norm": gains["gqa_k_norm"][i], "w_o": stacked("gqa_w_o", i)}
    return mix, ffn


def _pack_grads(grads, group):
    parts = [_split_for_devices(grads[n][group[n][0]:group[n][0] + group[n][1]], axis)
             for n, axis in _BIG if n in group]
    fill = -sum(p.shape[1] for p in parts) % PACK_ROWS
    return jnp.concatenate(parts + [jnp.zeros((N_DEV, fill, PACK_COLS), parts[0].dtype)], axis=1)


def kernel(x, mix_norm_ab, w_in_ab, mla_q_norm, mla_kv_norm, mla_w_uq, mla_w_ukv, w_out_ab, mix_norm_c, gqa_w_q, gqa_w_kv, gqa_q_norm, gqa_k_norm, gqa_w_o, ffn_norm, ffn_w_in, ffn_w_out, final_norm, loss_target, m_mix_norm_ab, m_w_in_ab, m_mla_q_norm, m_mla_kv_norm, m_mla_w_uq, m_mla_w_ukv, m_w_out_ab, m_mix_norm_c, m_gqa_w_q, m_gqa_w_kv, m_gqa_q_norm, m_gqa_k_norm, m_gqa_w_o, m_ffn_norm, m_ffn_w_in, m_ffn_w_out, m_final_norm, v_mix_norm_ab, v_w_in_ab, v_mla_q_norm, v_mla_kv_norm, v_mla_w_uq, v_mla_w_ukv, v_w_out_ab, v_mix_norm_c, v_gqa_w_q, v_gqa_w_kv, v_gqa_q_norm, v_gqa_k_norm, v_gqa_w_o, v_ffn_norm, v_ffn_w_in, v_ffn_w_out, v_final_norm):
    wts = dict(mix_norm_ab=mix_norm_ab, w_in_ab=w_in_ab, mla_q_norm=mla_q_norm, mla_kv_norm=mla_kv_norm,
               mla_w_uq=mla_w_uq, mla_w_ukv=mla_w_ukv, w_out_ab=w_out_ab, mix_norm_c=mix_norm_c, gqa_w_q=gqa_w_q,
               gqa_w_kv=gqa_w_kv, gqa_q_norm=gqa_q_norm, gqa_k_norm=gqa_k_norm, gqa_w_o=gqa_w_o, ffn_norm=ffn_norm,
               ffn_w_in=ffn_w_in, ffn_w_out=ffn_w_out, final_norm=final_norm)
    mom = dict(mix_norm_ab=m_mix_norm_ab, w_in_ab=m_w_in_ab, mla_q_norm=m_mla_q_norm, mla_kv_norm=m_mla_kv_norm,
               mla_w_uq=m_mla_w_uq, mla_w_ukv=m_mla_w_ukv, w_out_ab=m_w_out_ab, mix_norm_c=m_mix_norm_c,
               gqa_w_q=m_gqa_w_q, gqa_w_kv=m_gqa_w_kv, gqa_q_norm=m_gqa_q_norm, gqa_k_norm=m_gqa_k_norm,
               gqa_w_o=m_gqa_w_o, ffn_norm=m_ffn_norm, ffn_w_in=m_ffn_w_in, ffn_w_out=m_ffn_w_out,
               final_norm=m_final_norm)
    var = dict(mix_norm_ab=v_mix_norm_ab, w_in_ab=v_w_in_ab, mla_q_norm=v_mla_q_norm, mla_kv_norm=v_mla_kv_norm,
               mla_w_uq=v_mla_w_uq, mla_w_ukv=v_mla_w_ukv, w_out_ab=v_w_out_ab, mix_norm_c=v_mix_norm_c,
               gqa_w_q=v_gqa_w_q, gqa_w_kv=v_gqa_w_kv, gqa_q_norm=v_gqa_q_norm, gqa_k_norm=v_gqa_k_norm,
               gqa_w_o=v_gqa_w_o, ffn_norm=v_ffn_norm, ffn_w_in=v_ffn_w_in, ffn_w_out=v_ffn_w_out,
               final_norm=v_final_norm)
    big_names = [n for n, _ in _BIG]
    groups = (_FIRST, _REST)
    big_w, big_m, big_v = (_stored({n: src[n] for n in big_names}) for src in (wts, mom, var))
    w_grp = [_group_of(big_w, grp) for grp in groups]
    shapes = [{n: a.shape for n, a in w.items()} for w in w_grp]
    w_packs = [_pack_big(w) for w in w_grp]
    n_rows = [p.shape[0] for p in w_packs]
    xs = x[0]
    s = xs.shape[0]
    me = 4 * lax.axis_index("x") + 2 * lax.axis_index("y") + lax.axis_index("c")
    c_cols = mix_norm_c.shape[1]

    tail = jnp.pad(mix_norm_c.reshape(-1), (0, GATHER_TAIL_ROWS // 2 * PACK_COLS - mix_norm_c.size))
    tail = _to_bits16(tail.reshape(GATHER_TAIL_ROWS // 2, PACK_COLS))
    gathered = _all_gather(jnp.concatenate([w_packs[0].astype(jnp.bfloat16), tail], axis=0))
    full = [_unpack_gathered(gathered[:, :n_rows[0]], shapes[0]), None]
    c_all = _from_bits16(gathered[:, n_rows[0]:]).reshape(N_DEV, -1)[:, :mix_norm_c.size]
    c_full = c_all.reshape(N_DEV, 2, c_cols).transpose(1, 0, 2).reshape(2, N_DEV * c_cols)
    gains = dict(mix_norm_ab=mix_norm_ab, mla_q_norm=mla_q_norm, mla_kv_norm=mla_kv_norm, mix_norm_c=c_full,
                 gqa_q_norm=gqa_q_norm, gqa_k_norm=gqa_k_norm, ffn_norm=ffn_norm)

    tabs = _rope_tables(s)
    slopes = jnp.exp2(-8.0 * jnp.arange(1, DIL_HEADS + 1, dtype=_F32) / DIL_HEADS)

    h = xs
    saved = []
    for layer in range(DEPTH):
        mix_w, ffn_w = _layer_weights(full, gains, layer)
        if layer == 0:
            def late(rest):
                full[1] = _unpack_gathered(rest, shapes[1])
                return _layer_weights(full, gains, 0)[0]

            h, sv_mix, _ = _even_fwd(h, mix_w, tabs, slopes, f"l{layer}_mix", gather=w_packs[1].astype(jnp.bfloat16),
                                     late=late)
            mix_w, ffn_w = _layer_weights(full, gains, 0)
        elif layer % 2 == 0:
            h, sv_mix, _ = _even_fwd(h, mix_w, tabs, slopes, f"l{layer}_mix")
        else:
            h, sv_mix = _odd_fwd(h, mix_w, tabs, f"l{layer}_mix")
        h, sv_ffn = _ffn_fwd(h, ffn_w, f"l{layer}_ffn")
        saved.append((mix_w, ffn_w, sv_mix, sv_ffn))
    loss_local, dh, d_final = _final_loss(h, final_norm, loss_target[0], name="final_loss")

    gfull = {n: [None] * wts[n].shape[0] for n in big_names}
    gsmall = {n: [None] * (wts[n].shape[0] if wts[n].ndim > 1 else 1) for n in _SMALL}
    gsmall["final_norm"][0] = d_final
    for layer in reversed(range(DEPTH)):
        mix_w, ffn_w, sv_mix, sv_ffn = saved[layer]
        i = layer // 2
        dh, gf = _ffn_bwd(dh, sv_ffn, ffn_w, f"l{layer}_ffn")
        gsmall["ffn_norm"][layer] = gf["norm"]
        gfull["ffn_w_in"][layer] = gf["w_in_t"]
        gfull["ffn_w_out"][layer] = gf["w_out"]
        if layer % 2 == 0:
            def sending(d_w_out):
                gfull["w_out_ab"][0] = d_w_out
                return _pack_grads(gfull, _REST).astype(jnp.bfloat16)

            dh, gm, got = _even_bwd(dh, sv_mix, mix_w, tabs, slopes, f"l{layer}_mix",
                                    exchange=sending if layer == 0 else None)
            if layer == 0:
                received_rest = got
            gsmall["mix_norm_ab"][i] = gm["norm"]
            gsmall["mla_q_norm"][i] = gm["q_norm"]
            gsmall["mla_kv_norm"][i] = gm["kv_norm"]
            gfull["w_in_ab"][i] = jnp.concatenate([gm["w_a"][:, :IN_A], _group_major(gm["w_b"])], axis=-1)
            gfull["mla_w_uq"][i] = _merge_heads_cols(gm["w_uq"], MLA_HEADS, MLA_NOPE)
            gfull["mla_w_ukv"][i] = _merge_heads_cols(gm["w_ukv"], MLA_HEADS, MLA_NOPE)
            gfull["w_out_ab"][i] = gm["w_out"]
        else:
            dh, gm = _odd_bwd(dh, sv_mix, mix_w, tabs, f"l{layer}_mix")
            gsmall["mix_norm_c"][i] = gm["norm"]
            gsmall["gqa_q_norm"][i] = gm["q_norm"]
            gsmall["gqa_k_norm"][i] = gm["k_norm"]
            gfull["gqa_w_q"][i] = gm["w_q"]
            gfull["gqa_w_kv"][i] = gm["w_kv"]
            gfull["gqa_w_o"][i] = gm["w_o"]
    grad_x = dh[None]

    small_part = _pack_small({n: jnp.stack(gsmall[n]) for n in _SMALL})
    small_bits = jnp.broadcast_to(_to_bits16(small_part)[None], (N_DEV, EXCHANGE_TAIL_ROWS, PACK_COLS))
    received_first = _exchange(jnp.concatenate([_pack_grads(gfull, _FIRST).astype(jnp.bfloat16), small_bits], axis=1))
    big_packs = [_reduce_adamw(got, w_packs[gi], _pack_big(_group_of(big_m, grp)), _pack_big(_group_of(big_v, grp)),
                               name=f"adamw_big{gi}")
                 for gi, (grp, got) in enumerate(zip(groups, (received_first, received_rest)))]

    def widen_c(shard):
        return lax.dynamic_update_slice(jnp.zeros((2, N_DEV * c_cols), _F32), shard, (0, me * c_cols))

    def small_of(src):
        return _pack_small({n: (widen_c(src[n]) if n == "mix_norm_c" else src[n]) for n in _SMALL})

    small_recv = _from_bits16(received_first[:, n_rows[0]:])
    small_packs = _reduce_adamw(small_recv, small_of(wts), small_of(mom), small_of(var), name="adamw_small")

    def outputs_of(which):
        by_group = [_unpack_big(big_packs[gi][which], shapes[gi]) for gi in range(len(groups))]
        small = _unpack_small(small_packs[which])
        res = _stored({n: jnp.concatenate([grp[n] for grp in by_group if n in grp], axis=0) for n in big_names})
        for n in wts:
            if n in big_names:
                continue
            if n == "mix_norm_c":
                res[n] = lax.dynamic_slice(small[n].reshape(2, N_DEV * c_cols), (0, me * c_cols), (2, c_cols))
            else:
                res[n] = small[n].reshape(wts[n].shape)
        return [res[n] for n in wts]

    loss = lax.psum(loss_local, _AXES)
    return (loss, grad_x, *outputs_of(0), *outputs_of(1), *outputs_of(2), *outputs_of(3))
```
